```python
import math
import jax, jax.numpy as jnp
from jax import lax
import numpy as np

D_MODEL = 1024
BATCH = 8
SEQ = 2048
DEPTH = 2
DEC_BATCH = 128
DEC_SEQ = 1
PAST_LEN = 16384
PAGE_SIZE = 128

S5_WIDTH = D_MODEL // 2
S5_GROUP = 16
S5_GROUPS = S5_WIDTH // S5_GROUP
S5_STATE = 64
GDN_HEAD_DIM = 128
GDN_WIDTH = D_MODEL // 2
GDN_HEADS = GDN_WIDTH // GDN_HEAD_DIM
GDN_CONV = 4
GDN_CHUNK = 64
QKV_WIDTH = 3 * GDN_WIDTH
D_FF = (7 * D_MODEL) // 2
N_EXPERTS = 8
TOP_K = 2
N_DENSE = (DEPTH + 1) // 2
N_MOE = DEPTH // 2
N_MOD = 6
NORM_EPS = 1e-6
L2_EPS = 1e-6
IN_SIZES = (S5_WIDTH, QKV_WIDTH, GDN_WIDTH, GDN_HEADS, GDN_HEADS, D_MODEL, D_MODEL)
IN_COLS = S5_WIDTH + QKV_WIDTH + GDN_WIDTH + 2 * GDN_HEADS + 2 * D_MODEL

kernel_name = 'hybrid_s5_gdn_adaln_moe_step'


def rmsnorm(x, g):
    xf = x.astype(jnp.float32)
    y = xf * lax.rsqrt(jnp.mean(xf * xf, axis=-1, keepdims=True) + NORM_EPS)
    return (y * g.astype(jnp.float32)).astype(x.dtype)


def l2norm(x):
    return x * lax.rsqrt(jnp.sum(x * x, axis=-1, keepdims=True) + L2_EPS)


def _cplx_combine(e1, e2):
    a1r, a1i, b1r, b1i = e1
    a2r, a2i, b2r, b2i = e2
    return (a2r * a1r - a2i * a1i, a2r * a1i + a2i * a1r,
            a2r * b1r - a2i * b1i + b2r, a2r * b1i + a2i * b1r + b2i)


def s5_ssm(u, lam_re, lam_im, log_dt, b_re, b_im, c_re, c_im, d_skip, s0_re, s0_im):
    f32 = jnp.float32
    nb, L, _ = u.shape
    uf = u.astype(f32).reshape(nb, L, S5_GROUPS, S5_GROUP)
    lr = lam_re.astype(f32)
    li = lam_im.astype(f32)
    dt = jnp.exp(log_dt.astype(f32))[:, None]
    mag = jnp.exp(lr * dt)
    ab_re = mag * jnp.cos(li * dt)
    ab_im = mag * jnp.sin(li * dt)
    den = lr * lr + li * li
    nr = ab_re - 1.0
    k_re = (nr * lr + ab_im * li) / den
    k_im = (ab_im * lr - nr * li) / den
    br = b_re.astype(f32)
    bi = b_im.astype(f32)
    bb_re = k_re[..., None] * br - k_im[..., None] * bi
    bb_im = k_re[..., None] * bi + k_im[..., None] * br
    bu_re = jnp.einsum('blgc,gpc->blgp', uf, bb_re)
    bu_im = jnp.einsum('blgc,gpc->blgp', uf, bb_im)
    a_re = jnp.broadcast_to(ab_re, bu_re.shape)
    a_im = jnp.broadcast_to(ab_im, bu_im.shape)
    p_re, p_im, s_re, s_im = lax.associative_scan(_cplx_combine, (a_re, a_im, bu_re, bu_im), axis=1)
    s0r = s0_re.astype(f32)[:, None]
    s0i = s0_im.astype(f32)[:, None]
    s_re = s_re + p_re * s0r - p_im * s0i
    s_im = s_im + p_re * s0i + p_im * s0r
    y = (jnp.einsum('blgp,gcp->blgc', s_re, c_re.astype(f32))
         - jnp.einsum('blgp,gcp->blgc', s_im, c_im.astype(f32)))
    y = y.reshape(nb, L, S5_WIDTH) + d_skip.astype(f32) * uf.reshape(nb, L, S5_WIDTH)
    return y, s_re[:, -1], s_im[:, -1]


def causal_conv(x, buf, w):
    xp = jnp.concatenate([buf.astype(x.dtype), x], axis=1)
    L = x.shape[1]
    out = xp[:, 0:L] * w[0]
    for j in range(1, GDN_CONV):
        out = out + xp[:, j:j + L] * w[j]
    return jax.nn.silu(out), xp[:, -(GDN_CONV - 1):]


def gated_delta_chunked(q, k, v, g, beta, s0):
    nb, L, H, dk = q.shape
    dv = v.shape[-1]
    C = min(GDN_CHUNK, L)
    pad = (-L) % C
    if pad:
        pw = ((0, 0), (0, pad), (0, 0), (0, 0))
        q, k, v = jnp.pad(q, pw), jnp.pad(k, pw), jnp.pad(v, pw)
        g, beta = jnp.pad(g, pw[:3]), jnp.pad(beta, pw[:3])
    n = (L + pad) // C

    def chunks(t):
        t = t.reshape((nb, n, C, H) + t.shape[3:])
        return jnp.moveaxis(t, (1, 3), (0, 2))

    q = chunks(q) * (dk ** -0.5)
    k, v, g, beta = chunks(k), chunks(v), chunks(g), chunks(beta)
    gc = jnp.cumsum(g, axis=-1)
    tril = jnp.tril(jnp.ones((C, C), bool))
    strict = jnp.tril(jnp.ones((C, C), bool), -1)
    diff = gc[..., :, None] - gc[..., None, :]
    decay = jnp.where(tril, jnp.exp(jnp.where(tril, diff, 0.0)), 0.0)
    kb = k * beta[..., None]
    lower = jnp.where(strict, jnp.einsum('nbhcd,nbhmd->nbhcm', kb, k) * decay, 0.0)
    a_mat = lower + jnp.eye(C, dtype=q.dtype)
    rhs = jnp.concatenate([v * beta[..., None], kb * jnp.exp(gc)[..., None]], axis=-1)
    sol = lax.linalg.triangular_solve(a_mat, rhs, left_side=True, lower=True)
    u_c, w_c = sol[..., :dv], sol[..., dv:]
    attn = jnp.where(tril, jnp.einsum('nbhcd,nbhmd->nbhcm', q, k) * decay, 0.0)
    qg = q * jnp.exp(gc)[..., None]
    kg = k * jnp.exp(gc[..., -1:] - gc)[..., None]
    glast = jnp.exp(gc[..., -1])

    def step(S, xs):
        qg_i, kg_i, u_i, w_i, attn_i, gl_i = xs
        v_new = u_i - jnp.einsum('bhcd,bhde->bhce', w_i, S)
        o = jnp.einsum('bhcd,bhde->bhce', qg_i, S) + jnp.einsum('bhcm,bhme->bhce', attn_i, v_new)
        S = S * gl_i[..., None, None] + jnp.einsum('bhcd,bhce->bhde', kg_i, v_new)
        return S, o

    S, o = lax.scan(step, s0, (qg, kg, u_c, w_c, attn, glast))
    o = jnp.moveaxis(o, (0, 2), (1, 3)).reshape(nb, n * C, H, dv)[:, :L]
    return o, S


def mixer(h, s5r0, s5i0, S0, conv0, p, l):
    f32 = jnp.float32
    nb, L, _ = h.shape
    proj = h @ p['w_in'][l]
    split_at = np.cumsum(IN_SIZES)[:-1].tolist()
    u, qkv, z, a_lg, b_lg, gate_a, gate_b = jnp.split(proj, split_at, axis=-1)
    y_s5, s5r, s5i = s5_ssm(u, p['s5_lambda_re'][l], p['s5_lambda_im'][l], p['s5_log_dt'][l],
                            p['s5_b_re'][l], p['s5_b_im'][l], p['s5_c_re'][l], p['s5_c_im'][l],
                            p['s5_d'][l], s5r0, s5i0)
    y_s5 = jax.nn.gelu(y_s5).astype(h.dtype)
    glu_a, glu_b = jnp.split(y_s5 @ p['w_s5_glu'][l], 2, axis=-1)
    branch_a = glu_a * jax.nn.sigmoid(glu_b)
    qkv_c, conv_new = causal_conv(qkv, conv0, p['gdn_conv_w'][l])
    q, k, v = jnp.split(qkv_c.astype(f32), 3, axis=-1)
    q = l2norm(q.reshape(nb, L, GDN_HEADS, GDN_HEAD_DIM))
    k = l2norm(k.reshape(nb, L, GDN_HEADS, GDN_HEAD_DIM))
    v = v.reshape(nb, L, GDN_HEADS, GDN_HEAD_DIM)
    g = -jnp.exp(p['gdn_a_log'][l].astype(f32)) * jax.nn.softplus(a_lg.astype(f32) + p['gdn_dt_bias'][l].astype(f32))
    beta = jax.nn.sigmoid(b_lg.astype(f32))
    o, S = gated_delta_chunked(q, k, v, g, beta, S0.astype(f32))
    o = rmsnorm(o, p['gdn_norm_w'][l]) * jax.nn.silu(z.astype(f32).reshape(nb, L, GDN_HEADS, GDN_HEAD_DIM))
    branch_b = o.reshape(nb, L, GDN_WIDTH).astype(h.dtype) @ p['w_gdn_out'][l]
    merged = jax.nn.sigmoid(gate_a) * branch_a + jax.nn.sigmoid(gate_b) * branch_b
    out = merged @ p['w_out'][l]
    return (out, s5r.astype(s5r0.dtype), s5i.astype(s5i0.dtype), S.astype(S0.dtype),
            conv_new.astype(conv0.dtype))


def swiglu(h, w_gu, w_down):
    gt, up = jnp.split(h @ w_gu, 2, axis=-1)
    return (jax.nn.silu(gt) * up) @ w_down


def moe(h, w_r, b_r, w_gu, w_down):
    shp = h.shape
    t = h.reshape(-1, shp[-1])
    logits = (t @ w_r).astype(jnp.float32) + b_r.astype(jnp.float32)
    top_v, top_i = lax.top_k(logits, TOP_K)
    wts = jax.nn.softmax(top_v, axis=-1)
    comb = jnp.sum(jax.nn.one_hot(top_i, N_EXPERTS, dtype=jnp.float32) * wts[..., None], axis=1)
    comb = comb.astype(t.dtype)
    out = jnp.zeros_like(t)
    for e in range(N_EXPERTS):
        out = out + comb[:, e:e + 1] * swiglu(t, w_gu[e], w_down[e])
    return out.reshape(shp)


def trunk(x, c, s5r0, s5i0, S0, conv0, p):
    cs = jax.nn.silu(c)
    sr_l, si_l, sg_l, sc_l = [], [], [], []
    for l in range(DEPTH):
        mod = cs @ p['w_ada'][l] + p['b_ada'][l]
        sh_m, sc_m, gt_m, sh_f, sc_f, gt_f = jnp.split(mod[:, None, :], N_MOD, axis=-1)
        h = rmsnorm(x, p['g_mix'][l]) * (1.0 + sc_m) + sh_m
        out, sr, si, sg, cb = mixer(h, s5r0[l], s5i0[l], S0[l], conv0[l], p, l)
        x = x + gt_m * out
        h = rmsnorm(x, p['g_ffn'][l]) * (1.0 + sc_f) + sh_f
        if l % 2 == 0:
            f = swiglu(h, p['w_ffn_gate_up'][l // 2], p['w_ffn_down'][l // 2])
        else:
            f = moe(h, p['w_router'][l // 2], p['b_router'][l // 2],
                    p['w_exp_gate_up'][l // 2], p['w_exp_down'][l // 2])
        x = x + gt_f * f
        sr_l.append(sr)
        si_l.append(si)
        sg_l.append(sg)
        sc_l.append(cb)
    y = rmsnorm(x, p['g_final'])
    return y, jnp.stack(sr_l), jnp.stack(si_l), jnp.stack(sg_l), jnp.stack(sc_l)


def setup_inputs(seed: int = 0) -> dict:
    key = jax.random.key(seed)
    ks = iter(jax.random.split(key, 64))
    f32 = jnp.float32

    def nrm(shape, scale):
        return scale * jax.random.normal(next(ks), shape, f32)

    n_idx = jnp.arange(S5_STATE, dtype=f32)
    gdn_dt = jnp.exp(jax.random.uniform(next(ks), (DEPTH, GDN_HEADS), f32, math.log(1e-3), math.log(1e-1)))
    return {
        'x_prompt': nrm((BATCH, SEQ, D_MODEL), 1.0),
        'x_sample': nrm((DEC_BATCH, DEC_SEQ, D_MODEL), 1.0),
        'c_prompt': nrm((BATCH, D_MODEL), 1.0),
        'c_sample': nrm((DEC_BATCH, D_MODEL), 1.0),
        'state_s5_re': nrm((DEPTH, DEC_BATCH, S5_GROUPS, S5_STATE), 0.5),
        'state_s5_im': nrm((DEPTH, DEC_BATCH, S5_GROUPS, S5_STATE), 0.5),
        'state_gdn': nrm((DEPTH, DEC_BATCH, GDN_HEADS, GDN_HEAD_DIM, GDN_HEAD_DIM), 0.5),
        'state_conv': nrm((DEPTH, DEC_BATCH, GDN_CONV - 1, QKV_WIDTH), 1.0),
        'g_mix': 1.0 + nrm((DEPTH, D_MODEL), 0.02),
        'g_ffn': 1.0 + nrm((DEPTH, D_MODEL), 0.02),
        'g_final': 1.0 + nrm((D_MODEL,), 0.02),
        'w_ada': nrm((DEPTH, D_MODEL, N_MOD * D_MODEL), 0.5 * D_MODEL ** -0.5),
        'b_ada': nrm((DEPTH, N_MOD * D_MODEL), 0.02),
        'w_in': nrm((DEPTH, D_MODEL, IN_COLS), D_MODEL ** -0.5),
        's5_lambda_re': -0.5 + nrm((DEPTH, S5_GROUPS, S5_STATE), 0.01),
        's5_lambda_im': math.pi * n_idx + nrm((DEPTH, S5_GROUPS, S5_STATE), 0.01),
        's5_log_dt': jax.random.uniform(next(ks), (DEPTH, S5_GROUPS), f32, math.log(1e-3), math.log(1e-1)),
        's5_b_re': nrm((DEPTH, S5_GROUPS, S5_STATE, S5_GROUP), (2 * S5_GROUP) ** -0.5),
        's5_b_im': nrm((DEPTH, S5_GROUPS, S5_STATE, S5_GROUP), (2 * S5_GROUP) ** -0.5),
        's5_c_re': nrm((DEPTH, S5_GROUPS, S5_GROUP, S5_STATE), (2 * S5_STATE) ** -0.5),
        's5_c_im': nrm((DEPTH, S5_GROUPS, S5_GROUP, S5_STATE), (2 * S5_STATE) ** -0.5),
        's5_d': nrm((DEPTH, S5_WIDTH), 1.0),
        'w_s5_glu': nrm((DEPTH, S5_WIDTH, 2 * D_MODEL), S5_WIDTH ** -0.5),
        'gdn_conv_w': nrm((DEPTH, GDN_CONV, QKV_WIDTH), GDN_CONV ** -0.5),
        'gdn_a_log': jnp.log(jax.random.uniform(next(ks), (DEPTH, GDN_HEADS), f32, 1.0, 16.0)),
        'gdn_dt_bias': gdn_dt + jnp.log(-jnp.expm1(-gdn_dt)),
        'gdn_norm_w': 1.0 + nrm((DEPTH, GDN_HEAD_DIM), 0.02),
        'w_gdn_out': nrm((DEPTH, GDN_WIDTH, D_MODEL), GDN_WIDTH ** -0.5),
        'w_out': nrm((DEPTH, D_MODEL, D_MODEL), D_MODEL ** -0.5),
        'w_ffn_gate_up': nrm((N_DENSE, D_MODEL, 2 * D_FF), D_MODEL ** -0.5),
        'w_ffn_down': nrm((N_DENSE, D_FF, D_MODEL), D_FF ** -0.5),
        'w_router': nrm((N_MOE, D_MODEL, N_EXPERTS), D_MODEL ** -0.5),
        'b_router': nrm((N_MOE, N_EXPERTS), 0.01),
        'w_exp_gate_up': nrm((N_MOE, N_EXPERTS, D_MODEL, 2 * D_FF), D_MODEL ** -0.5),
        'w_exp_down': nrm((N_MOE, N_EXPERTS, D_FF, D_MODEL), D_FF ** -0.5),
    }


def reference(x_prompt, x_sample, c_prompt, c_sample, state_s5_re, state_s5_im, state_gdn, state_conv,
              g_mix, g_ffn, g_final, w_ada, b_ada, w_in, s5_lambda_re, s5_lambda_im, s5_log_dt,
              s5_b_re, s5_b_im, s5_c_re, s5_c_im, s5_d, w_s5_glu, gdn_conv_w, gdn_a_log, gdn_dt_bias,
              gdn_norm_w, w_gdn_out, w_out, w_ffn_gate_up, w_ffn_down, w_router, b_router,
              w_exp_gate_up, w_exp_down):
    p = dict(g_mix=g_mix, g_ffn=g_ffn, g_final=g_final, w_ada=w_ada, b_ada=b_ada, w_in=w_in,
             s5_lambda_re=s5_lambda_re, s5_lambda_im=s5_lambda_im, s5_log_dt=s5_log_dt,
             s5_b_re=s5_b_re, s5_b_im=s5_b_im, s5_c_re=s5_c_re, s5_c_im=s5_c_im, s5_d=s5_d,
             w_s5_glu=w_s5_glu, gdn_conv_w=gdn_conv_w, gdn_a_log=gdn_a_log, gdn_dt_bias=gdn_dt_bias,
             gdn_norm_w=gdn_norm_w, w_gdn_out=w_gdn_out, w_out=w_out, w_ffn_gate_up=w_ffn_gate_up,
             w_ffn_down=w_ffn_down, w_router=w_router, b_router=b_router,
             w_exp_gate_up=w_exp_gate_up, w_exp_down=w_exp_down)
    nbp = x_prompt.shape[0]
    z_sr = jnp.zeros((DEPTH, nbp, S5_GROUPS, S5_STATE), state_s5_re.dtype)
    z_si = jnp.zeros((DEPTH, nbp, S5_GROUPS, S5_STATE), state_s5_im.dtype)
    z_sg = jnp.zeros((DEPTH, nbp, GDN_HEADS, GDN_HEAD_DIM, GDN_HEAD_DIM), state_gdn.dtype)
    z_sc = jnp.zeros((DEPTH, nbp, GDN_CONV - 1, QKV_WIDTH), state_conv.dtype)
    y_prompt, sr_p, si_p, sg_p, sc_p = trunk(x_prompt, c_prompt, z_sr, z_si, z_sg, z_sc, p)
    y_sample, sr_s, si_s, sg_s, sc_s = trunk(x_sample, c_sample, state_s5_re, state_s5_im,
                                             state_gdn, state_conv, p)
    return (y_prompt, y_sample, sr_p, si_p, sg_p, sc_p, sr_s, si_s, sg_s, sc_s)
```

```python
import functools

import jax
import jax.numpy as jnp
from jax import lax
from jax.experimental import pallas as pl
from jax.experimental.pallas import tpu as pltpu

F32 = jnp.float32
BF16 = jnp.bfloat16
HI = lax.Precision.HIGHEST

D_MODEL = 1024
DEPTH = 2
S5_WIDTH = 512
S5_GROUP = 16
S5_GROUPS = 32
S5_STATE = 64
GDN_HEADS = 4
GDN_DK = 128
GDN_WIDTH = 512
GDN_CONV = 4
QKV_WIDTH = 1536
D_FF = 3584
N_EXPERTS = 8
NORM_EPS = 1e-6
L2_EPS = 1e-6

LANES = 128
SLABS = S5_WIDTH // LANES
SLAB_STATE = (S5_GROUPS // SLABS) * S5_STATE
S5_T = 8
GDN_C = 64
VMEM_LIMIT = 56 * 1024 * 1024


def _cp(sem, vmem=VMEM_LIMIT):
    return pltpu.CompilerParams(dimension_semantics=sem, vmem_limit_bytes=vmem)


def _dot(a, b, prec=None):
    return jnp.dot(a, b, precision=prec, preferred_element_type=F32)


def _dotb(a, b):
    return jnp.dot(a.astype(BF16), b.astype(BF16), preferred_element_type=F32)


def _dot_nt(a, b, prec=None):
    return lax.dot_general(a, b, (((1,), (1,)), ((), ())), precision=prec,
                           preferred_element_type=F32)


def _dot_tn(a, b, prec=None):
    return lax.dot_general(a, b, (((0,), (0,)), ((), ())), precision=prec,
                           preferred_element_type=F32)


def _silu(x):
    return x * jax.nn.sigmoid(x)


def _ada_kernel(c_ref, w_ref, b_ref, o_ref):
    cs = _silu(c_ref[...])
    o_ref[0] = _dot(cs, w_ref[0], HI) + b_ref[0]


def _ada(c_all, w_ada, b_ada):
    n = c_all.shape[0]
    tn = 1536
    return pl.pallas_call(
        _ada_kernel,
        out_shape=jax.ShapeDtypeStruct((DEPTH, n, 6 * D_MODEL), F32),
        grid=(DEPTH, 6 * D_MODEL // tn),
        in_specs=[pl.BlockSpec((n, D_MODEL), lambda l, j: (0, 0)),
                  pl.BlockSpec((1, D_MODEL, tn), lambda l, j: (l, 0, j)),
                  pl.BlockSpec((1, 1, tn), lambda l, j: (l, 0, j))],
        out_specs=pl.BlockSpec((1, n, tn), lambda l, j: (l, 0, j)),
        compiler_params=_cp(("parallel", "parallel")),
        name="ada_mod",
    )(c_all, w_ada, b_ada.reshape(DEPTH, 1, 6 * D_MODEL))


def _proj_kernel(x_ref, g_ref, sc_ref, sh_ref, w_ref, wg_ref, wab_ref,
                 u_ref, qkv_ref, z_ref, ga_ref, gb_ref, ab_ref, h_scr, *, hi):
    j = pl.program_id(2)

    @pl.when(j == 0)
    def _():
        x = x_ref[0]
        ms = jnp.mean(x * x, axis=-1, keepdims=True)
        xn = x * lax.rsqrt(ms + NORM_EPS) * g_ref[...]
        h_scr[...] = (xn * (1.0 + sc_ref[0]) + sh_ref[0]).astype(h_scr.dtype)

    def mm(w):
        if hi:
            return _dot(h_scr[...], w, HI)
        return _dot(h_scr[...], w.astype(BF16))

    @pl.when(j == 0)
    def _():
        res = mm(w_ref[...])
        for k in range(SLABS):
            u_ref[k, 0] = res[:, k * LANES:(k + 1) * LANES]

    @pl.when((j >= 1) & (j <= 3))
    def _():
        qkv_ref[0] = mm(w_ref[...])

    @pl.when(j == 4)
    def _():
        z_ref[0] = mm(w_ref[...])

    @pl.when((j == 5) | (j == 6))
    def _():
        ga_ref[0] = jax.nn.sigmoid(mm(wg_ref[...]))

    @pl.when((j == 7) | (j == 8))
    def _():
        gb_ref[0] = jax.nn.sigmoid(mm(wg_ref[...]))

    @pl.when(j == 9)
    def _():
        ab_ref[0] = mm(wab_ref[...])


def _proj(x, g, sc, sh, w_in, w_gates, w_ab, *, tm, hi):
    B, L, D = x.shape
    lm = sc.shape[1]
    tmm = 1 if lm == 1 else tm
    mod_map = (lambda b, i, j: (b, 0, 0)) if lm == 1 else (lambda b, i, j: (b, i, 0))
    tn = 512
    clampi = lambda j, lo, n: jnp.clip(j - lo, 0, n - 1)
    outs = pl.pallas_call(
        functools.partial(_proj_kernel, hi=hi),
        out_shape=(jax.ShapeDtypeStruct((SLABS, B, L, LANES), F32),
                   jax.ShapeDtypeStruct((B, L, QKV_WIDTH), F32),
                   jax.ShapeDtypeStruct((B, L, GDN_WIDTH), F32),
                   jax.ShapeDtypeStruct((B, L, D), F32),
                   jax.ShapeDtypeStruct((B, L, D), F32),
                   jax.ShapeDtypeStruct((B, L, LANES), F32)),
        grid=(B, L // tm, 10),
        in_specs=[pl.BlockSpec((1, tm, D), lambda b, i, j: (b, i, 0)),
                  pl.BlockSpec((1, D), lambda b, i, j: (0, 0)),
                  pl.BlockSpec((1, tmm, D), mod_map),
                  pl.BlockSpec((1, tmm, D), mod_map),
                  pl.BlockSpec((D, tn), lambda b, i, j: (0, jnp.minimum(j, 4))),
                  pl.BlockSpec((D, tn), lambda b, i, j: (0, clampi(j, 5, 4))),
                  pl.BlockSpec((D, LANES), lambda b, i, j: (0, 0))],
        out_specs=(pl.BlockSpec((SLABS, 1, tm, LANES), lambda b, i, j: (0, b, i, 0)),
                   pl.BlockSpec((1, tm, tn), lambda b, i, j: (b, i, clampi(j, 1, 3))),
                   pl.BlockSpec((1, tm, tn), lambda b, i, j: (b, i, 0)),
                   pl.BlockSpec((1, tm, tn), lambda b, i, j: (b, i, clampi(j, 5, 2))),
                   pl.BlockSpec((1, tm, tn), lambda b, i, j: (b, i, clampi(j, 7, 2))),
                   pl.BlockSpec((1, tm, LANES), lambda b, i, j: (b, i, 0))),
        scratch_shapes=[pltpu.VMEM((tm, D), F32 if hi else BF16)],
        compiler_params=_cp(("parallel", "parallel", "arbitrary")),
        name="norm_in_proj",
    )(x, g, sc, sh, w_in, w_gates, w_ab)
    return outs


def _s5_prep_kernel(lrb, lib, dtb, bre, bim, lrt, lit, dtt, cre, cim, lrn, lin, dtn,
                    be_ref, cpw_ref, pt_ref, a1_ref, *, seg):
    def disc(lr, li, ldt):
        dt = jnp.exp(ldt)
        mag = jnp.exp(lr * dt)
        return mag * jnp.cos(li * dt), mag * jnp.sin(li * dt)

    def cmul(xr, xi, yr, yi):
        return xr * yr - xi * yi, xr * yi + xi * yr

    lr, li = lrb[0], lib[0]
    ar, ai = disc(lr, li, dtb[0])
    den = lr * lr + li * li
    nr = ar - 1.0
    kr = (nr * lr + ai * li) / den
    ki = (ai * lr - nr * li) / den
    br, bi = bre[0], bim[0]
    bbr = kr * br - ki * bi
    bbi = kr * bi + ki * br
    pr, pi = jnp.ones_like(ar), jnp.zeros_like(ar)
    for d in range(S5_T):
        t = S5_T - 1 - d
        vr, vi = cmul(pr, pi, bbr, bbi)
        be_ref[0, t, 0] = vr
        be_ref[0, t, 1] = vi
        pr, pi = cmul(pr, pi, ar, ai)

    ar, ai = disc(lrt[0], lit[0], dtt[0])
    cr, ci = cre[0], cim[0]
    pr, pi = jnp.ones_like(ar), jnp.zeros_like(ar)
    for d in range(S5_T + 1):
        vr, vi = cmul(cr, ci, pr, pi)
        cpw_ref[0, d, 0] = vr
        cpw_ref[0, d, 1] = -vi
        pr, pi = cmul(pr, pi, ar, ai)

    ar, ai = disc(lrn[0], lin[0], dtn[0])
    a1_ref[0, 0] = ar
    a1_ref[0, 1] = ai
    tr, ti = ar, ai
    for _ in range(S5_T - 1):
        tr, ti = cmul(tr, ti, ar, ai)
    pr, pi = jnp.ones_like(ar), jnp.zeros_like(ar)
    for i in range(seg + 1):
        pt_ref[0, i, 0] = pr
        pt_ref[0, i, 1] = pi
        pr, pi = cmul(pr, pi, tr, ti)


def _s5_prep(lam_re, lam_im, log_dt, b_re, b_im, c_re, c_im, seg):
    G, P, C = S5_GROUPS, S5_STATE, S5_GROUP
    flat = P * C
    rep = lambda a: jnp.repeat(a, C, axis=-1)
    til = lambda a: jnp.tile(a, (1, 1, C))
    dt3 = jnp.broadcast_to(log_dt[:, :, None], (DEPTH, G, P))
    nat = lambda a: a.reshape(DEPTH, SLABS, SLAB_STATE)
    args = (rep(lam_re), rep(lam_im), rep(dt3), b_re.reshape(DEPTH, G, flat), b_im.reshape(DEPTH, G, flat),
            til(lam_re), til(lam_im), til(dt3), c_re.reshape(DEPTH, G, flat), c_im.reshape(DEPTH, G, flat),
            nat(lam_re), nat(lam_im), nat(dt3))
    big = pl.BlockSpec((1, G, flat), lambda l: (l, 0, 0))
    small = pl.BlockSpec((1, SLABS, SLAB_STATE), lambda l: (l, 0, 0))
    return pl.pallas_call(
        functools.partial(_s5_prep_kernel, seg=seg),
        out_shape=(jax.ShapeDtypeStruct((DEPTH, S5_T, 2, G, flat), F32),
                   jax.ShapeDtypeStruct((DEPTH, S5_T + 1, 2, G, flat), F32),
                   jax.ShapeDtypeStruct((DEPTH, seg + 1, 2, SLABS, SLAB_STATE), F32),
                   jax.ShapeDtypeStruct((DEPTH, 2, SLABS, SLAB_STATE), F32)),
        grid=(DEPTH,),
        in_specs=[big] * 10 + [small] * 3,
        out_specs=(pl.BlockSpec((1, S5_T, 2, G, flat), lambda l: (l, 0, 0, 0, 0)),
                   pl.BlockSpec((1, S5_T + 1, 2, G, flat), lambda l: (l, 0, 0, 0, 0)),
                   pl.BlockSpec((1, seg + 1, 2, SLABS, SLAB_STATE), lambda l: (l, 0, 0, 0, 0)),
                   pl.BlockSpec((1, 2, SLABS, SLAB_STATE), lambda l: (l, 0, 0, 0))),
        compiler_params=_cp(("parallel",)),
        name="s5_discretize",
    )(*args)


def _embed_be(be):
    T = be.shape[0]
    x = be.reshape(T, 2, SLABS, 8, S5_STATE, S5_GROUP).transpose(2, 0, 3, 5, 1, 4)
    eye = jnp.eye(8, dtype=bool)
    x = jnp.where(eye[None, None, :, None, None, :, None], x[:, :, :, :, :, None, :], 0.0)
    return x.reshape(SLABS, T * LANES, 2 * SLAB_STATE)


def _embed_cp(cpw):
    Dn = cpw.shape[0]
    x = cpw.reshape(Dn, 2, SLABS, 8, S5_GROUP, S5_STATE).transpose(2, 0, 1, 3, 5, 4)
    eye = jnp.eye(8, dtype=bool)
    x = jnp.where(eye[None, None, None, :, None, :, None], x[:, :, :, :, :, None, :], 0.0)
    return x.reshape(SLABS, Dn, 2 * SLAB_STATE, LANES)


def _toep_kernel(b_ref, c_ref, o_ref):
    o_ref[0, 0] = _dot(b_ref[0], c_ref[0, 0], HI)


def _toep(bst, cpe):
    return pl.pallas_call(
        _toep_kernel,
        out_shape=jax.ShapeDtypeStruct((SLABS, S5_T, LANES, LANES), F32),
        grid=(SLABS, S5_T),
        in_specs=[pl.BlockSpec((1, LANES, 2 * SLAB_STATE), lambda k, d: (k, 0, 0)),
                  pl.BlockSpec((1, 1, 2 * SLAB_STATE, LANES), lambda k, d: (k, d, 0, 0))],
        out_specs=pl.BlockSpec((1, 1, LANES, LANES), lambda k, d: (k, d, 0, 0)),
        compiler_params=_cp(("parallel", "parallel")),
        name="s5_conv_blocks",
    )(bst, cpe)


def _pair_tiles(K):
    Z = jnp.zeros_like(K[:, 0])
    kd = lambda d: K[:, d] if d >= 0 else Z
    tiles = []
    for dd in range(S5_T // 2):
        top = jnp.concatenate([kd(2 * dd), kd(2 * dd + 1)], axis=-1)
        bot = jnp.concatenate([kd(2 * dd - 1), kd(2 * dd)], axis=-1)
        tiles.append(jnp.concatenate([top, bot], axis=-2))
    return jnp.stack(tiles, axis=1)


def _s5_seq_kernel(up_ref, be_ref, tp_ref, cpm_ref, pt_ref, s0_ref, dsk_ref,
                   yg_ref, sfin_ref, e_scr, sx_scr, *, nc):
    seg = nc // 8
    W = SLAB_STATE
    nt = W // LANES
    u = up_ref[0, 0]
    ub = u.astype(BF16)
    e = _dot(ub, be_ref[0])
    for c in range(2 * nt):
        e_scr[c] = e[:, c * LANES:(c + 1) * LANES]

    def tiles(row):
        return [(row[:, c * LANES:(c + 1) * LANES], row[:, W + c * LANES:W + (c + 1) * LANES])
                for c in range(nt)]

    a8 = [(jnp.broadcast_to(r, (8, LANES)), jnp.broadcast_to(i, (8, LANES)))
          for r, i in tiles(pt_ref[0, 1:2, :])]

    def step(i, carry):
        rows = pl.ds(i, 8, stride=seg)
        new = []
        for c in range(nt):
            sr, si = carry[c]
            ar, ai = a8[c]
            sx_scr[c, rows, :] = sr
            sx_scr[nt + c, rows, :] = si
            new.append((ar * sr - ai * si + e_scr[c, rows, :],
                        ar * si + ai * sr + e_scr[nt + c, rows, :]))
        return tuple(new)

    zero = jnp.zeros((8, LANES), F32)
    ends = lax.fori_loop(0, seg, step, tuple((zero, zero) for _ in range(nt)))

    al = tiles(pt_ref[0, seg:seg + 1, :])
    cur = tiles(s0_ref[0, 0])
    car = []
    for c in range(nt):
        alr, ali = al[c]
        cr, ci = cur[c]
        sr, si = ends[c]
        crs, cis = [], []
        for j in range(8):
            crs.append(cr)
            cis.append(ci)
            cr, ci = (alr * cr - ali * ci + sr[j:j + 1], alr * ci + ali * cr + si[j:j + 1])
        sfin_ref[0, 0, :, c * LANES:(c + 1) * LANES] = cr
        sfin_ref[0, 0, :, W + c * LANES:W + (c + 1) * LANES] = ci
        car.append((jnp.concatenate(crs, axis=0), jnp.concatenate(cis, axis=0)))

    def corr(i, _):
        rows = pl.ds(i, 8, stride=seg)
        pw = tiles(pt_ref[0, pl.ds(i, 1), :])
        for c in range(nt):
            pr, pi = pw[c]
            cr, ci = car[c]
            sx_scr[c, rows, :] = sx_scr[c, rows, :] + (pr * cr - pi * ci)
            sx_scr[nt + c, rows, :] = sx_scr[nt + c, rows, :] + (pr * ci + pi * cr)
        return 0

    lax.fori_loop(0, seg, corr, 0)

    sx = jnp.concatenate([sx_scr[c] for c in range(2 * nt)], axis=-1)
    y = _dot(sx.astype(BF16), cpm_ref[0])
    TW = 2 * LANES
    for tq in range(S5_T // 2):
        acc = y[:, tq * TW:(tq + 1) * TW]
        for tpi in range(tq + 1):
            acc = acc + _dot(ub[:, tpi * TW:(tpi + 1) * TW], tp_ref[0, tq - tpi])
        acc = acc + dsk_ref[0, :, tq * TW:(tq + 1) * TW] * u[:, tq * TW:(tq + 1) * TW]
        yg_ref[0, 0, :, tq * TW:(tq + 1) * TW] = jax.nn.gelu(acc)


def _s5_seq(u_slab, be_emb, tp, cpm, pt, s0, dsk):
    _, B, L, _ = u_slab.shape
    nc = L // S5_T
    seg = nc // 8
    W2 = 2 * SLAB_STATE
    up = u_slab.reshape(SLABS, B, nc, S5_T * LANES)
    yg, sfin = pl.pallas_call(
        functools.partial(_s5_seq_kernel, nc=nc),
        out_shape=(jax.ShapeDtypeStruct((SLABS, B, nc, S5_T * LANES), F32),
                   jax.ShapeDtypeStruct((SLABS, B, 1, W2), F32)),
        grid=(SLABS, B),
        in_specs=[pl.BlockSpec((1, 1, nc, S5_T * LANES), lambda k, b: (k, b, 0, 0)),
                  pl.BlockSpec((1, S5_T * LANES, W2), lambda k, b: (k, 0, 0)),
                  pl.BlockSpec((1, S5_T // 2, 2 * LANES, 2 * LANES), lambda k, b: (k, 0, 0, 0)),
                  pl.BlockSpec((1, W2, S5_T * LANES), lambda k, b: (k, 0, 0)),
                  pl.BlockSpec((1, seg + 1, W2), lambda k, b: (k, 0, 0)),
                  pl.BlockSpec((1, 1, 1, W2), lambda k, b: (k, b, 0, 0)),
                  pl.BlockSpec((1, 1, S5_T * LANES), lambda k, b: (k, 0, 0))],
        out_specs=(pl.BlockSpec((1, 1, nc, S5_T * LANES), lambda k, b: (k, b, 0, 0)),
                   pl.BlockSpec((1, 1, 1, W2), lambda k, b: (k, b, 0, 0))),
        scratch_shapes=[pltpu.VMEM((W2 // LANES, nc, LANES), F32),
                        pltpu.VMEM((W2 // LANES, nc, LANES), F32)],
        compiler_params=_cp(("parallel", "parallel")),
        name="s5_seq",
    )(up, be_emb, tp, cpm, pt, s0, dsk)
    return yg.reshape(SLABS, B, L, LANES), sfin


def _s5_step_kernel(u_ref, b_ref, c_ref, a_ref, s0_ref, d_ref, yg_ref, s1_ref):
    W = SLAB_STATE
    u = u_ref[0]
    bu = _dot(u, b_ref[0], HI)
    ar = a_ref[0, :, 0:W]
    ai = a_ref[0, :, W:2 * W]
    sr = s0_ref[0, :, 0:W]
    si = s0_ref[0, :, W:2 * W]
    nr = ar * sr - ai * si + bu[:, 0:W]
    ni = ar * si + ai * sr + bu[:, W:2 * W]
    s1_ref[0, :, 0:W] = nr
    s1_ref[0, :, W:2 * W] = ni
    s1 = jnp.concatenate([nr, ni], axis=-1)
    y = _dot(s1, c_ref[0], HI) + d_ref[0] * u
    yg_ref[0] = jax.nn.gelu(y)


def _s5_step(u_slab, bst, c0, a1, s0, d1):
    _, N, _ = u_slab.shape
    W2 = 2 * SLAB_STATE
    return pl.pallas_call(
        _s5_step_kernel,
        out_shape=(jax.ShapeDtypeStruct((SLABS, N, LANES), F32),
                   jax.ShapeDtypeStruct((SLABS, N, W2), F32)),
        grid=(SLABS,),
        in_specs=[pl.BlockSpec((1, N, LANES), lambda k: (k, 0, 0)),
                  pl.BlockSpec((1, LANES, W2), lambda k: (k, 0, 0)),
                  pl.BlockSpec((1, W2, LANES), lambda k: (k, 0, 0)),
                  pl.BlockSpec((1, 1, W2), lambda k: (k, 0, 0)),
                  pl.BlockSpec((1, N, W2), lambda k: (k, 0, 0)),
                  pl.BlockSpec((1, 1, LANES), lambda k: (k, 0, 0))],
        out_specs=(pl.BlockSpec((1, N, LANES), lambda k: (k, 0, 0)),
                   pl.BlockSpec((1, N, W2), lambda k: (k, 0, 0))),
        compiler_params=_cp(("parallel",)),
        name="s5_step",
    )(u_slab, bst, c0, a1, s0, d1)


def _l2n(x):
    return x * lax.rsqrt(jnp.sum(x * x, axis=-1, keepdims=True) + L2_EPS)


def _unit_lower_solve(A, rhs):
    n = GDN_C
    row = lax.broadcasted_iota(jnp.int32, (n, n), 0)
    col = lax.broadcasted_iota(jnp.int32, (n, n), 1)
    eye = (row == col).astype(F32)
    same8 = (row // 8) == (col // 8)
    N8 = jnp.where(same8, -A, 0.0)
    inv = eye + N8
    Q = N8
    for _ in range(2):
        Q = _dotb(Q, Q)
        inv = inv + _dotb(inv, Q)
    s = 8
    while s < n:
        off = jnp.where(((row // (2 * s)) == (col // (2 * s))) & ((row // s) != (col // s)), A, 0.0)
        inv = inv - _dotb(inv, _dotb(off, inv))
        s *= 2
    x0 = _dotb(inv, rhs)
    r = rhs - x0 - _dot(A, x0, HI)
    return x0 + _dotb(inv, r)


def _gdn_chunk(q, k, v, gcol, bcol, S):
    n = GDN_C
    row = lax.broadcasted_iota(jnp.int32, (n, n), 0)
    col = lax.broadcasted_iota(jnp.int32, (n, n), 1)
    tri = row >= col
    strict = row > col
    Gb = jnp.broadcast_to(gcol, (n, GDN_DK))
    gc = _dot(tri.astype(F32), Gb, HI)
    R = _dot(jnp.ones((n, n), F32), jnp.where(row <= col, Gb[:, 0:n], 0.0), HI)
    diff = gc[:, 0:n] - R
    decay = jnp.where(tri, jnp.exp(jnp.where(tri, diff, 0.0)), 0.0)
    Bb = jnp.broadcast_to(bcol, (n, GDN_DK))
    kb = k * Bb
    A = jnp.where(strict, _dot_nt(kb.astype(BF16), k.astype(BF16)) * decay, 0.0)
    egc = jnp.exp(gc)
    rhs = jnp.concatenate([v * Bb, kb * egc], axis=-1)
    sol = _unit_lower_solve(A, rhs)
    u_c = sol[:, 0:GDN_DK]
    w_c = sol[:, GDN_DK:]
    qs = q * (GDN_DK ** -0.5)
    attn = jnp.where(tri, _dot_nt(qs.astype(BF16), k.astype(BF16)) * decay, 0.0)
    qg = qs * egc
    glast = gc[n - 1:n, :]
    kg = k * jnp.exp(glast - gc)
    v_new = u_c - _dotb(w_c, S)
    o = _dotb(qg, S) + _dotb(attn, v_new)
    S_new = S * jnp.exp(glast) + _dot_tn(kg.astype(BF16), v_new.astype(BF16))
    return o, S_new


def _gdn_seq_kernel(qkv_ref, z_ref, ab_ref, cw_ref, alog_ref, dtb_ref, nw_ref, conv0_ref, s0_ref,
                    o_ref, sfin_ref, xp_scr, qc_scr, gb_scr, s_scr, *, tl):
    lt = pl.program_id(1)

    @pl.when(lt == 0)
    def _():
        xp_scr[0:8, :] = jnp.zeros((8, QKV_WIDTH), F32)
        xp_scr[8 - (GDN_CONV - 1):8, :] = conv0_ref[0]
        s_scr[...] = s0_ref[0]

    xp_scr[8:8 + tl, :] = qkv_ref[0]
    conv = cw_ref[0:1, :] * xp_scr[5:5 + tl, :]
    for j in range(1, GDN_CONV):
        conv = conv + cw_ref[j:j + 1, :] * xp_scr[5 + j:5 + j + tl, :]
    xp_scr[0:8, :] = xp_scr[tl:tl + 8, :]
    qc_scr[...] = _silu(conv)

    ab = ab_ref[0]
    gb_scr[:, 0:LANES] = -jnp.exp(alog_ref[...]) * jax.nn.softplus(ab + dtb_ref[...])
    gb_scr[:, LANES:2 * LANES] = jax.nn.sigmoid(ab)

    def chunk(c, _):
        r0 = pl.multiple_of(c * GDN_C, GDN_C)
        rows = pl.ds(r0, GDN_C)
        gbv = gb_scr[rows, :]
        for h in range(GDN_HEADS):
            q = _l2n(qc_scr[rows, h * GDN_DK:(h + 1) * GDN_DK])
            k = _l2n(qc_scr[rows, GDN_WIDTH + h * GDN_DK:GDN_WIDTH + (h + 1) * GDN_DK])
            v = qc_scr[rows, 2 * GDN_WIDTH + h * GDN_DK:2 * GDN_WIDTH + (h + 1) * GDN_DK]
            gcol = gbv[:, h:h + 1]
            bcol = gbv[:, LANES + GDN_HEADS + h:LANES + GDN_HEADS + h + 1]
            o, s_new = _gdn_chunk(q, k, v, gcol, bcol, s_scr[h])
            s_scr[h] = s_new
            zh = z_ref[0, rows, h * GDN_DK:(h + 1) * GDN_DK]
            on = o * lax.rsqrt(jnp.mean(o * o, axis=-1, keepdims=True) + NORM_EPS) * nw_ref[...]
            o_ref[0, rows, h * GDN_DK:(h + 1) * GDN_DK] = on * _silu(zh)
        return 0

    lax.fori_loop(0, tl // GDN_C, chunk, 0)

    @pl.when(lt == pl.num_programs(1) - 1)
    def _():
        sfin_ref[0] = s_scr[...]


def _gdn_seq(qkv, z, ab, conv_w, alog, dtb, nw, conv0, s0):
    B, L, _ = qkv.shape
    tl = min(256, L)
    return pl.pallas_call(
        functools.partial(_gdn_seq_kernel, tl=tl),
        out_shape=(jax.ShapeDtypeStruct((B, L, GDN_WIDTH), F32),
                   jax.ShapeDtypeStruct((B, GDN_HEADS, GDN_DK, GDN_DK), F32)),
        grid=(B, L // tl),
        in_specs=[pl.BlockSpec((1, tl, QKV_WIDTH), lambda b, i: (b, i, 0)),
                  pl.BlockSpec((1, tl, GDN_WIDTH), lambda b, i: (b, i, 0)),
                  pl.BlockSpec((1, tl, LANES), lambda b, i: (b, i, 0)),
                  pl.BlockSpec((GDN_CONV, QKV_WIDTH), lambda b, i: (0, 0)),
                  pl.BlockSpec((1, LANES), lambda b, i: (0, 0)),
                  pl.BlockSpec((1, LANES), lambda b, i: (0, 0)),
                  pl.BlockSpec((1, GDN_DK), lambda b, i: (0, 0)),
                  pl.BlockSpec((1, GDN_CONV - 1, QKV_WIDTH), lambda b, i: (b, 0, 0)),
                  pl.BlockSpec((1, GDN_HEADS, GDN_DK, GDN_DK), lambda b, i: (b, 0, 0, 0))],
        out_specs=(pl.BlockSpec((1, tl, GDN_WIDTH), lambda b, i: (b, i, 0)),
                   pl.BlockSpec((1, GDN_HEADS, GDN_DK, GDN_DK), lambda b, i: (b, 0, 0, 0))),
        scratch_shapes=[pltpu.VMEM((tl + 8, QKV_WIDTH), F32),
                        pltpu.VMEM((tl, QKV_WIDTH), F32),
                        pltpu.VMEM((tl, 2 * LANES), F32),
                        pltpu.VMEM((GDN_HEADS, GDN_DK, GDN_DK), F32)],
        compiler_params=_cp(("parallel", "arbitrary")),
        name="gdn_seq",
    )(qkv, z, ab, conv_w, alog, dtb, nw, conv0, s0)


GDN_STEP_ROWS = 8


def _gdn_step_kernel(qkv_ref, z_ref, ab_ref, cw_ref, alog_ref, dtb_ref, nw_ref, conv0_ref, s0_ref,
                     o_ref, s1_ref):
    nb = GDN_STEP_ROWS
    W = QKV_WIDTH
    conv = cw_ref[0:1, :] * conv0_ref[:, 0:W]
    conv = conv + cw_ref[1:2, :] * conv0_ref[:, W:2 * W]
    conv = conv + cw_ref[2:3, :] * conv0_ref[:, 2 * W:3 * W]
    conv = conv + cw_ref[3:4, :] * qkv_ref[...]
    qc = _silu(conv)
    ab = ab_ref[...]
    eg = jnp.exp(-jnp.exp(alog_ref[...]) * jax.nn.softplus(ab + dtb_ref[...]))
    beta = jax.nn.sigmoid(ab)
    eye = (lax.broadcasted_iota(jnp.int32, (GDN_DK, GDN_DK), 0)
           == lax.broadcasted_iota(jnp.int32, (GDN_DK, GDN_DK), 1)).astype(F32)
    for h in range(GDN_HEADS):
        q = _l2n(qc[:, h * GDN_DK:(h + 1) * GDN_DK]) * (GDN_DK ** -0.5)
        k = _l2n(qc[:, GDN_WIDTH + h * GDN_DK:GDN_WIDTH + (h + 1) * GDN_DK])
        v = qc[:, 2 * GDN_WIDTH + h * GDN_DK:2 * GDN_WIDTH + (h + 1) * GDN_DK]
        kT = _dot_nt(eye, k, HI)
        qT = _dot_nt(eye, q, HI)
        qk = jnp.sum(q * k, axis=-1, keepdims=True)
        for j in range(nb):
            S = s0_ref[j, h]
            kc = jnp.broadcast_to(kT[:, j:j + 1], (GDN_DK, GDN_DK))
            qcb = jnp.broadcast_to(qT[:, j:j + 1], (GDN_DK, GDN_DK))
            kS = jnp.sum(kc * S, axis=0, keepdims=True)
            qS = jnp.sum(qcb * S, axis=0, keepdims=True)
            egj = eg[j:j + 1, h:h + 1]
            bj = beta[j:j + 1, GDN_HEADS + h:GDN_HEADS + h + 1]
            v_new = bj * v[j:j + 1, :] - (bj * egj) * kS
            o = egj * qS + qk[j:j + 1, :] * v_new
            s1_ref[j, h] = S * egj + kc * v_new
            zh = z_ref[j:j + 1, h * GDN_DK:(h + 1) * GDN_DK]
            on = o * lax.rsqrt(jnp.mean(o * o, axis=-1, keepdims=True) + NORM_EPS) * nw_ref[...]
            o_ref[j:j + 1, h * GDN_DK:(h + 1) * GDN_DK] = on * _silu(zh)


def _gdn_step(qkv, z, ab, conv_w, alog, dtb, nw, conv0, s0):
    N = qkv.shape[0]
    nb = GDN_STEP_ROWS
    row = lambda w: pl.BlockSpec((nb, w), lambda i: (i, 0))
    const = lambda r, w: pl.BlockSpec((r, w), lambda i: (0, 0))
    return pl.pallas_call(
        _gdn_step_kernel,
        out_shape=(jax.ShapeDtypeStruct((N, GDN_WIDTH), F32),
                   jax.ShapeDtypeStruct((N, GDN_HEADS, GDN_DK, GDN_DK), F32)),
        grid=(N // nb,),
        in_specs=[row(QKV_WIDTH), row(GDN_WIDTH), row(LANES), const(GDN_CONV, QKV_WIDTH),
                  const(1, LANES), const(1, LANES), const(1, GDN_DK), row(3 * QKV_WIDTH),
                  pl.BlockSpec((nb, GDN_HEADS, GDN_DK, GDN_DK), lambda i: (i, 0, 0, 0))],
        out_specs=(row(GDN_WIDTH),
                   pl.BlockSpec((nb, GDN_HEADS, GDN_DK, GDN_DK), lambda i: (i, 0, 0, 0))),
        compiler_params=_cp(("parallel",)),
        name="gdn_step",
    )(qkv, z, ab, conv_w, alog, dtb, nw, conv0, s0)


def _merge_kernel(yg_ref, og_ref, ga_ref, gb_ref, x_ref, gt_ref, wglu_ref, wgo_ref, wout_ref,
                  gf_ref, scf_ref, shf_ref, wr_ref,
                  xo_ref, h_ref, lg_ref, *scr, hi):
    if hi:
        wglu, wgo, wout = wglu_ref[...], wgo_ref[...], wout_ref[...]
        mm = lambda a, w: _dot(a, w, HI)
    else:
        wglu_s, wgo_s, wout_s = scr

        @pl.when((pl.program_id(0) == 0) & (pl.program_id(1) == 0))
        def _():
            wglu_s[...] = wglu_ref[...].astype(BF16)
            wgo_s[...] = wgo_ref[...].astype(BF16)
            wout_s[...] = wout_ref[...].astype(BF16)

        wglu, wgo, wout = wglu_s[...], wgo_s[...], wout_s[...]
        mm = lambda a, w: _dot(a.astype(BF16), w)

    y = jnp.concatenate([yg_ref[k, 0] for k in range(SLABS)], axis=-1)
    glu = mm(y, wglu)
    branch_a = glu[:, 0:D_MODEL] * jax.nn.sigmoid(glu[:, D_MODEL:])
    branch_b = mm(og_ref[0], wgo)
    merged = ga_ref[0] * branch_a + gb_ref[0] * branch_b
    out = mm(merged, wout)
    x = x_ref[0] + gt_ref[0] * out
    xo_ref[0] = x
    ms = jnp.mean(x * x, axis=-1, keepdims=True)
    h = x * lax.rsqrt(ms + NORM_EPS) * gf_ref[...]
    h = h * (1.0 + scf_ref[0]) + shf_ref[0]
    h_ref[0] = h
    lg_ref[0] = _dot(h, wr_ref[...], HI)


def _merge(yg, og, ga, gb, x, gt, wglu, wgo, wout, gf, scf, shf, wr, *, tm, hi):
    B, L, D = x.shape
    lm = gt.shape[1]
    tmm = 1 if lm == 1 else tm
    mod_map = (lambda b, i: (b, 0, 0)) if lm == 1 else (lambda b, i: (b, i, 0))
    row = lambda w: pl.BlockSpec((1, tm, w), lambda b, i: (b, i, 0))
    const = lambda r, w: pl.BlockSpec((r, w), lambda b, i: (0, 0))
    mod = pl.BlockSpec((1, tmm, D), mod_map)
    scratch = [] if hi else [pltpu.VMEM((S5_WIDTH, 2 * D), BF16), pltpu.VMEM((GDN_WIDTH, D), BF16),
                             pltpu.VMEM((D, D), BF16)]
    return pl.pallas_call(
        functools.partial(_merge_kernel, hi=hi),
        out_shape=(jax.ShapeDtypeStruct((B, L, D), F32),
                   jax.ShapeDtypeStruct((B, L, D), F32),
                   jax.ShapeDtypeStruct((B, L, N_EXPERTS), F32)),
        grid=(B, L // tm),
        in_specs=[pl.BlockSpec((SLABS, 1, tm, LANES), lambda b, i: (0, b, i, 0)),
                  row(GDN_WIDTH), row(D), row(D), row(D), mod,
                  const(S5_WIDTH, 2 * D), const(GDN_WIDTH, D), const(D, D),
                  const(1, D), mod, mod, const(D, N_EXPERTS)],
        out_specs=(row(D), row(D), row(N_EXPERTS)),
        scratch_shapes=scratch,
        compiler_params=_cp(("arbitrary", "arbitrary")),
        name="merge_out_proj",
    )(yg, og, ga, gb, x, gt, wglu, wgo, wout, gf, scf, shf, wr)


FF_TILE = 512


def _finish(x, gfin_ref, final):
    if not final:
        return x
    ms = jnp.mean(x * x, axis=-1, keepdims=True)
    return x * lax.rsqrt(ms + NORM_EPS) * gfin_ref[...]


def _ffn_kernel(h_ref, x_ref, gt_ref, wg_ref, wu_ref, wd_ref, gfin_ref, o_ref, acc_scr, *, hi, final):
    j = pl.program_id(2)
    if hi:
        h = h_ref[0]
        mm = lambda a, w: _dot(a, w, HI)
    else:
        h = h_ref[0].astype(BF16)
        mm = lambda a, w: _dot(a.astype(BF16), w.astype(BF16))
    act = _silu(mm(h, wg_ref[...])) * mm(h, wu_ref[...])
    part = mm(act, wd_ref[...])

    @pl.when(j == 0)
    def _():
        acc_scr[...] = part

    @pl.when(j > 0)
    def _():
        acc_scr[...] = acc_scr[...] + part

    @pl.when(j == pl.num_programs(2) - 1)
    def _():
        o_ref[0] = _finish(x_ref[0] + gt_ref[0] * acc_scr[...], gfin_ref, final)


def _ffn(h, x, gt, w_gu, w_down, gfin, *, tm, hi, final):
    B, L, D = x.shape
    lm = gt.shape[1]
    tmm = 1 if lm == 1 else tm
    mod_map = (lambda b, i, j: (b, 0, 0)) if lm == 1 else (lambda b, i, j: (b, i, 0))
    nj = D_FF // FF_TILE
    row = pl.BlockSpec((1, tm, D), lambda b, i, j: (b, i, 0))
    return pl.pallas_call(
        functools.partial(_ffn_kernel, hi=hi, final=final),
        out_shape=jax.ShapeDtypeStruct((B, L, D), F32),
        grid=(B, L // tm, nj),
        in_specs=[row, row, pl.BlockSpec((1, tmm, D), mod_map),
                  pl.BlockSpec((D, FF_TILE), lambda b, i, j: (0, j)),
                  pl.BlockSpec((D, FF_TILE), lambda b, i, j: (0, nj + j)),
                  pl.BlockSpec((FF_TILE, D), lambda b, i, j: (j, 0)),
                  pl.BlockSpec((1, D), lambda b, i, j: (0, 0))],
        out_specs=row,
        scratch_shapes=[pltpu.VMEM((tm, D), F32)],
        compiler_params=_cp(("parallel", "parallel", "arbitrary")),
        name="ffn_dense",
    )(h, x, gt, w_gu, w_gu, w_down, gfin)


def _route(lg):
    idx = lax.broadcasted_iota(jnp.int32, lg.shape, 1)
    m1 = jnp.max(lg, axis=-1, keepdims=True)
    i1 = jnp.min(jnp.where(lg == m1, idx, N_EXPERTS), axis=-1, keepdims=True)
    lg2 = jnp.where(idx == i1, -jnp.inf, lg)
    m2 = jnp.max(lg2, axis=-1, keepdims=True)
    i2 = jnp.min(jnp.where(lg2 == m2, idx, N_EXPERTS), axis=-1, keepdims=True)
    e2 = jnp.exp(m2 - m1)
    w1 = 1.0 / (1.0 + e2)
    w2 = e2 / (1.0 + e2)
    return jnp.where(idx == i1, w1, 0.0) + jnp.where(idx == i2, w2, 0.0)


def _moe_kernel(h_ref, x_ref, gt_ref, lg_ref, br_ref, wg_ref, wu_ref, wd_ref, gfin_ref, o_ref,
                acc_scr, comb_scr, *, final):
    e = pl.program_id(2)
    j = pl.program_id(3)

    @pl.when((e == 0) & (j == 0))
    def _():
        comb_scr[...] = _route(lg_ref[0] + br_ref[...])
        acc_scr[...] = jnp.zeros_like(acc_scr)

    idx = lax.broadcasted_iota(jnp.int32, comb_scr.shape, 1)
    ce = jnp.sum(jnp.where(idx == e, comb_scr[...], 0.0), axis=-1, keepdims=True)
    h = h_ref[0].astype(BF16)
    act = _silu(_dot(h, wg_ref[0].astype(BF16))) * _dot(h, wu_ref[0].astype(BF16))
    acc_scr[...] = acc_scr[...] + ce * _dot(act.astype(BF16), wd_ref[0].astype(BF16))

    @pl.when((e == pl.num_programs(2) - 1) & (j == pl.num_programs(3) - 1))
    def _():
        o_ref[0] = _finish(x_ref[0] + gt_ref[0] * acc_scr[...], gfin_ref, final)


def _moe(h, x, gt, lg, b_r, w_gu, w_down, gfin, *, tm, final):
    B, L, D = x.shape
    lm = gt.shape[1]
    tmm = 1 if lm == 1 else tm
    mod_map = (lambda b, i, e, j: (b, 0, 0)) if lm == 1 else (lambda b, i, e, j: (b, i, 0))
    nj = D_FF // FF_TILE
    row = pl.BlockSpec((1, tm, D), lambda b, i, e, j: (b, i, 0))
    return pl.pallas_call(
        functools.partial(_moe_kernel, final=final),
        out_shape=jax.ShapeDtypeStruct((B, L, D), F32),
        grid=(B, L // tm, N_EXPERTS, nj),
        in_specs=[row, row, pl.BlockSpec((1, tmm, D), mod_map),
                  pl.BlockSpec((1, tm, N_EXPERTS), lambda b, i, e, j: (b, i, 0)),
                  pl.BlockSpec((1, N_EXPERTS), lambda b, i, e, j: (0, 0)),
                  pl.BlockSpec((1, D, FF_TILE), lambda b, i, e, j: (e, 0, j)),
                  pl.BlockSpec((1, D, FF_TILE), lambda b, i, e, j: (e, 0, nj + j)),
                  pl.BlockSpec((1, FF_TILE, D), lambda b, i, e, j: (e, j, 0)),
                  pl.BlockSpec((1, D), lambda b, i, e, j: (0, 0))],
        out_specs=row,
        scratch_shapes=[pltpu.VMEM((tm, D), F32), pltpu.VMEM((tm, N_EXPERTS), F32)],
        compiler_params=_cp(("parallel", "parallel", "arbitrary", "arbitrary")),
        name="moe_dense",
    )(h, x, gt, lg, b_r, w_gu, w_gu, w_down, gfin)


def _pad_lanes(v):
    return jnp.pad(v.reshape(1, -1), ((0, 0), (0, LANES - v.shape[-1])))


def _trunk(x, mod, states, p, s5m, *, seq):
    B, L, D = x.shape
    hi = not seq
    tm = min(256, L)
    tm_ffn = min(1024, L)
    s5r0, s5i0, sg0, sc0 = states
    out_states = []
    for l in range(DEPTH):
        m = [mod[l][..., i * D:(i + 1) * D] for i in range(6)]
        sh_m, sc_m, gt_m, sh_f, sc_f, gt_f = m
        w_in = p['w_in'][l]
        w_gates = w_in[:, 2568:]
        w_ab = jnp.pad(w_in[:, 2560:2568], ((0, 0), (0, LANES - 8)))
        u, qkv, z, ga, gb, ab = _proj(x, p['g_mix'][l].reshape(1, D), sc_m, sh_m, w_in, w_gates, w_ab,
                                      tm=tm, hi=hi)
        alog = _pad_lanes(p['gdn_a_log'][l])
        dtb = _pad_lanes(p['gdn_dt_bias'][l])
        nw = p['gdn_norm_w'][l].reshape(1, GDN_DK)
        if seq:
            yg, sfin = _s5_seq(u, s5m['be'][l], s5m['tp'][l], s5m['cpm'][l], s5m['pt'][l],
                               jnp.zeros((SLABS, B, 1, 2 * SLAB_STATE), F32), s5m['dsk'][l])
            sfin = sfin.reshape(SLABS, B, 2, SLAB_STATE).transpose(2, 1, 0, 3)
            sr = sfin[0].reshape(B, S5_GROUPS, S5_STATE)
            si = sfin[1].reshape(B, S5_GROUPS, S5_STATE)
            og, sg = _gdn_seq(qkv, z, ab, p['gdn_conv_w'][l], alog, dtb, nw,
                              jnp.zeros((B, GDN_CONV - 1, QKV_WIDTH), F32),
                              jnp.zeros((B, GDN_HEADS, GDN_DK, GDN_DK), F32))
            cb = qkv[:, L - (GDN_CONV - 1):, :]
        else:
            n = L
            s0 = jnp.concatenate([s5r0[l].reshape(n, SLABS, SLAB_STATE),
                                  s5i0[l].reshape(n, SLABS, SLAB_STATE)], axis=-1).transpose(1, 0, 2)
            yg, s1 = _s5_step(u.reshape(SLABS, n, LANES), s5m['bst'][l], s5m['c0'][l], s5m['a1'][l],
                              s0, s5m['d1'][l])
            yg = yg.reshape(SLABS, 1, n, LANES)
            s1 = s1.transpose(1, 0, 2)
            sr = s1[:, :, :SLAB_STATE].reshape(n, S5_GROUPS, S5_STATE)
            si = s1[:, :, SLAB_STATE:].reshape(n, S5_GROUPS, S5_STATE)
            og, sg = _gdn_step(qkv.reshape(n, QKV_WIDTH), z.reshape(n, GDN_WIDTH), ab.reshape(n, LANES),
                               p['gdn_conv_w'][l], alog, dtb, nw,
                               sc0[l].reshape(n, (GDN_CONV - 1) * QKV_WIDTH), sg0[l])
            og = og.reshape(1, n, GDN_WIDTH)
            cb = jnp.concatenate([sc0[l][:, 1:, :], qkv.reshape(n, 1, QKV_WIDTH)], axis=1)
        x, h, lg = _merge(yg, og, ga, gb, x, gt_m, p['w_s5_glu'][l], p['w_gdn_out'][l], p['w_out'][l],
                          p['g_ffn'][l].reshape(1, D), sc_f, sh_f, p['w_router'][l // 2], tm=tm, hi=hi)
        final = l == DEPTH - 1
        gfin = p['g_final'].reshape(1, D)
        if l % 2 == 0:
            x = _ffn(h, x, gt_f, p['w_ffn_gate_up'][l // 2], p['w_ffn_down'][l // 2], gfin,
                     tm=tm_ffn, hi=hi, final=final)
        else:
            x = _moe(h, x, gt_f, lg, p['b_router'][l // 2].reshape(1, N_EXPERTS),
                     p['w_exp_gate_up'][l // 2], p['w_exp_down'][l // 2], gfin, tm=tm_ffn, final=final)
        out_states.append((sr, si, sg, cb))
    st = [jnp.stack([o[i] for o in out_states]) for i in range(4)]
    return x, st


def kernel(x_prompt, x_sample, c_prompt, c_sample, state_s5_re, state_s5_im, state_gdn, state_conv,
           g_mix, g_ffn, g_final, w_ada, b_ada, w_in, s5_lambda_re, s5_lambda_im, s5_log_dt,
           s5_b_re, s5_b_im, s5_c_re, s5_c_im, s5_d, w_s5_glu, gdn_conv_w, gdn_a_log, gdn_dt_bias,
           gdn_norm_w, w_gdn_out, w_out, w_ffn_gate_up, w_ffn_down, w_router, b_router,
           w_exp_gate_up, w_exp_down):
    p = dict(g_mix=g_mix, g_ffn=g_ffn, g_final=g_final, w_in=w_in, w_s5_glu=w_s5_glu,
             gdn_conv_w=gdn_conv_w, gdn_a_log=gdn_a_log, gdn_dt_bias=gdn_dt_bias,
             gdn_norm_w=gdn_norm_w, w_gdn_out=w_gdn_out, w_out=w_out, w_ffn_gate_up=w_ffn_gate_up,
             w_ffn_down=w_ffn_down, w_router=w_router, b_router=b_router,
             w_exp_gate_up=w_exp_gate_up, w_exp_down=w_exp_down)
    nbp, L, D = x_prompt.shape
    nbs = x_sample.shape[0]

    mod = _ada(jnp.concatenate([c_prompt, c_sample], axis=0), w_ada, b_ada)
    mod_p = mod[:, :nbp].reshape(DEPTH, nbp, 1, 6 * D)
    mod_s = mod[:, nbp:].reshape(DEPTH, 1, nbs, 6 * D)

    seg = L // S5_T // 8
    be, cpw, pt, a1 = _s5_prep(s5_lambda_re, s5_lambda_im, s5_log_dt, s5_b_re, s5_b_im,
                               s5_c_re, s5_c_im, seg)
    W2 = 2 * SLAB_STATE
    s5m = dict(be=[], tp=[], cpm=[], pt=[], dsk=[], bst=[], c0=[], a1=[], d1=[])
    for l in range(DEPTH):
        be_e = _embed_be(be[l])
        cp_e = _embed_cp(cpw[l])
        bst = be_e[:, (S5_T - 1) * LANES:, :]
        s5m['be'].append(be_e.astype(BF16))
        s5m['tp'].append(_pair_tiles(_toep(bst, cp_e)).astype(BF16))
        s5m['cpm'].append(cp_e[:, 1:].transpose(0, 2, 1, 3).reshape(SLABS, W2, S5_T * LANES).astype(BF16))
        s5m['pt'].append(pt[l].transpose(2, 0, 1, 3).reshape(SLABS, seg + 1, W2))
        d1 = s5_d[l].reshape(SLABS, 1, LANES)
        s5m['dsk'].append(jnp.tile(d1, (1, 1, S5_T)))
        s5m['bst'].append(bst)
        s5m['c0'].append(cp_e[:, 0])
        s5m['a1'].append(a1[l].transpose(1, 0, 2).reshape(SLABS, 1, W2))
        s5m['d1'].append(d1)

    y_p, st_p = _trunk(x_prompt, mod_p, (None, None, None, None), p, s5m, seq=True)
    y_s, st_s = _trunk(x_sample.reshape(1, nbs, D), mod_s,
                       (state_s5_re, state_s5_im, state_gdn, state_conv), p, s5m, seq=False)
    return (y_p, y_s.reshape(nbs, 1, D), st_p[0], st_p[1], st_p[2], st_p[3],
            st_s[0], st_s[1], st_s[2], st_s[3])
```

```python
import functools

import jax
import jax.numpy as jnp
from jax import lax
from jax.experimental import pallas as pl
from jax.experimental.pallas import tpu as pltpu

F32 = jnp.float32
BF16 = jnp.bfloat16
HI = lax.Precision.HIGHEST

D_MODEL = 1024
DEPTH = 2
S5_WIDTH = 512
S5_GROUP = 16
S5_GROUPS = 32
S5_STATE = 64
GDN_HEADS = 4
GDN_DK = 128
GDN_WIDTH = 512
GDN_CONV = 4
QKV_WIDTH = 1536
D_FF = 3584
N_EXPERTS = 8
NORM_EPS = 1e-6
L2_EPS = 1e-6

LANES = 128
SLABS = S5_WIDTH // LANES
SLAB_STATE = (S5_GROUPS // SLABS) * S5_STATE
S5_T = 8
GDN_C = 128
VMEM_LIMIT = 56 * 1024 * 1024


def _cp(sem, vmem=VMEM_LIMIT):
    return pltpu.CompilerParams(dimension_semantics=sem, vmem_limit_bytes=vmem)


def _dot(a, b, prec=None):
    return jnp.dot(a, b, precision=prec, preferred_element_type=F32)


def _dotb(a, b):
    return jnp.dot(a.astype(BF16), b.astype(BF16), preferred_element_type=F32)


def _dot_nt(a, b, prec=None):
    return lax.dot_general(a, b, (((1,), (1,)), ((), ())), precision=prec,
                           preferred_element_type=F32)


def _dot_tn(a, b, prec=None):
    return lax.dot_general(a, b, (((0,), (0,)), ((), ())), precision=prec,
                           preferred_element_type=F32)


def _silu(x):
    return x * jax.nn.sigmoid(x)


def _ada_kernel(c_ref, w_ref, b_ref, o_ref):
    cs = _silu(c_ref[...])
    o_ref[0] = _dot(cs, w_ref[0], HI) + b_ref[0]


def _ada(c_all, w_ada, b_ada):
    n = c_all.shape[0]
    tn = 1536
    return pl.pallas_call(
        _ada_kernel,
        out_shape=jax.ShapeDtypeStruct((DEPTH, n, 6 * D_MODEL), F32),
        grid=(DEPTH, 6 * D_MODEL // tn),
        in_specs=[pl.BlockSpec((n, D_MODEL), lambda l, j: (0, 0)),
                  pl.BlockSpec((1, D_MODEL, tn), lambda l, j: (l, 0, j)),
                  pl.BlockSpec((1, 1, tn), lambda l, j: (l, 0, j))],
        out_specs=pl.BlockSpec((1, n, tn), lambda l, j: (l, 0, j)),
        compiler_params=_cp(("parallel", "parallel")),
        name="ada_mod",
    )(c_all, w_ada, b_ada.reshape(DEPTH, 1, 6 * D_MODEL))


def _proj_kernel(x_ref, g_ref, sc_ref, sh_ref, w_ref, wg_ref, wab_ref,
                 u_ref, qkv_ref, z_ref, ga_ref, gb_ref, ab_ref, h_scr, *us_scr, hi):
    j = pl.program_id(2)

    @pl.when(j == 0)
    def _():
        x = x_ref[0]
        ms = jnp.mean(x * x, axis=-1, keepdims=True)
        xn = x * lax.rsqrt(ms + NORM_EPS) * g_ref[...]
        h_scr[...] = (xn * (1.0 + sc_ref[0]) + sh_ref[0]).astype(h_scr.dtype)

    def mm(w):
        if hi:
            return _dot(h_scr[...], w, HI)
        return _dot(h_scr[...], w.astype(BF16))

    @pl.when(j == 0)
    def _():
        res = mm(w_ref[...])
        for k in range(SLABS):
            if not us_scr:
                u_ref[k, 0] = res[:, k * LANES:(k + 1) * LANES]
                continue
            us_scr[0][...] = res[:, k * LANES:(k + 1) * LANES]
            nrow = res.shape[0] // S5_T
            for t in range(S5_T):
                u_ref[k, 0, :, t * LANES:(t + 1) * LANES] = us_scr[0][pl.ds(t, nrow, stride=S5_T), :]

    @pl.when((j >= 1) & (j <= 3))
    def _():
        qkv_ref[0] = mm(w_ref[...])

    @pl.when(j == 4)
    def _():
        z_ref[0] = mm(w_ref[...])

    @pl.when((j == 5) | (j == 6))
    def _():
        ga_ref[0] = jax.nn.sigmoid(mm(wg_ref[...]))

    @pl.when((j == 7) | (j == 8))
    def _():
        gb_ref[0] = jax.nn.sigmoid(mm(wg_ref[...]))

    @pl.when(j == 9)
    def _():
        ab_ref[0] = mm(wab_ref[...])


def _proj(x, g, sc, sh, w_in, w_gates, w_ab, *, tm, hi, chunked):
    B, L, D = x.shape
    lm = sc.shape[1]
    tmm = 1 if lm == 1 else tm
    mod_map = (lambda b, i, j: (b, 0, 0)) if lm == 1 else (lambda b, i, j: (b, i, 0))
    tn = 512
    clampi = lambda j, lo, n: jnp.clip(j - lo, 0, n - 1)
    if chunked:
        u_shape = jax.ShapeDtypeStruct((SLABS, B, L // S5_T, S5_T * LANES), F32)
        u_spec = pl.BlockSpec((SLABS, 1, tm // S5_T, S5_T * LANES), lambda b, i, j: (0, b, i, 0))
    else:
        u_shape = jax.ShapeDtypeStruct((SLABS, B, L, LANES), F32)
        u_spec = pl.BlockSpec((SLABS, 1, tm, LANES), lambda b, i, j: (0, b, i, 0))
    outs = pl.pallas_call(
        functools.partial(_proj_kernel, hi=hi),
        out_shape=(u_shape,
                   jax.ShapeDtypeStruct((B, L, QKV_WIDTH), F32),
                   jax.ShapeDtypeStruct((B, L, GDN_WIDTH), F32),
                   jax.ShapeDtypeStruct((B, L, D), F32),
                   jax.ShapeDtypeStruct((B, L, D), F32),
                   jax.ShapeDtypeStruct((B, L, LANES), F32)),
        grid=(B, L // tm, 10),
        in_specs=[pl.BlockSpec((1, tm, D), lambda b, i, j: (b, i, 0)),
                  pl.BlockSpec((1, D), lambda b, i, j: (0, 0)),
                  pl.BlockSpec((1, tmm, D), mod_map),
                  pl.BlockSpec((1, tmm, D), mod_map),
                  pl.BlockSpec((D, tn), lambda b, i, j: (0, jnp.minimum(j, 4))),
                  pl.BlockSpec((D, tn), lambda b, i, j: (0, clampi(j, 5, 4))),
                  pl.BlockSpec((D, LANES), lambda b, i, j: (0, 0))],
        out_specs=(u_spec,
                   pl.BlockSpec((1, tm, tn), lambda b, i, j: (b, i, clampi(j, 1, 3))),
                   pl.BlockSpec((1, tm, tn), lambda b, i, j: (b, i, 0)),
                   pl.BlockSpec((1, tm, tn), lambda b, i, j: (b, i, clampi(j, 5, 2))),
                   pl.BlockSpec((1, tm, tn), lambda b, i, j: (b, i, clampi(j, 7, 2))),
                   pl.BlockSpec((1, tm, LANES), lambda b, i, j: (b, i, 0))),
        scratch_shapes=[pltpu.VMEM((tm, D), F32 if hi else BF16)]
        + ([pltpu.VMEM((tm, LANES), F32)] if chunked else []),
        compiler_params=_cp(("parallel", "parallel", "arbitrary")),
        name="norm_in_proj",
    )(x, g, sc, sh, w_in, w_gates, w_ab)
    return outs


def _s5_prep_kernel(lrb, lib, dtb, bre, bim, lrt, lit, dtt, cre, cim, lrn, lin, dtn,
                    be_ref, cpw_ref, pt_ref, a1_ref, *, seg):
    def disc(lr, li, ldt):
        dt = jnp.exp(ldt)
        mag = jnp.exp(lr * dt)
        return mag * jnp.cos(li * dt), mag * jnp.sin(li * dt)

    def cmul(xr, xi, yr, yi):
        return xr * yr - xi * yi, xr * yi + xi * yr

    lr, li = lrb[0], lib[0]
    ar, ai = disc(lr, li, dtb[0])
    den = lr * lr + li * li
    nr = ar - 1.0
    kr = (nr * lr + ai * li) / den
    ki = (ai * lr - nr * li) / den
    br, bi = bre[0], bim[0]
    bbr = kr * br - ki * bi
    bbi = kr * bi + ki * br
    pr, pi = jnp.ones_like(ar), jnp.zeros_like(ar)
    for d in range(S5_T):
        t = S5_T - 1 - d
        vr, vi = cmul(pr, pi, bbr, bbi)
        be_ref[0, t, 0] = vr
        be_ref[0, t, 1] = vi
        pr, pi = cmul(pr, pi, ar, ai)

    ar, ai = disc(lrt[0], lit[0], dtt[0])
    cr, ci = cre[0], cim[0]
    pr, pi = jnp.ones_like(ar), jnp.zeros_like(ar)
    for d in range(S5_T + 1):
        vr, vi = cmul(cr, ci, pr, pi)
        cpw_ref[0, d, 0] = vr
        cpw_ref[0, d, 1] = -vi
        pr, pi = cmul(pr, pi, ar, ai)

    ar, ai = disc(lrn[0], lin[0], dtn[0])
    a1_ref[0, 0] = ar
    a1_ref[0, 1] = ai
    tr, ti = ar, ai
    for _ in range(S5_T - 1):
        tr, ti = cmul(tr, ti, ar, ai)
    pr, pi = jnp.ones_like(ar), jnp.zeros_like(ar)
    for i in range(seg + 1):
        pt_ref[0, i, 0] = pr
        pt_ref[0, i, 1] = pi
        pr, pi = cmul(pr, pi, tr, ti)


def _s5_prep(lam_re, lam_im, log_dt, b_re, b_im, c_re, c_im, seg):
    G, P, C = S5_GROUPS, S5_STATE, S5_GROUP
    flat = P * C
    rep = lambda a: jnp.repeat(a, C, axis=-1)
    til = lambda a: jnp.tile(a, (1, 1, C))
    dt3 = jnp.broadcast_to(log_dt[:, :, None], (DEPTH, G, P))
    nat = lambda a: a.reshape(DEPTH, SLABS, SLAB_STATE)
    args = (rep(lam_re), rep(lam_im), rep(dt3), b_re.reshape(DEPTH, G, flat), b_im.reshape(DEPTH, G, flat),
            til(lam_re), til(lam_im), til(dt3), c_re.reshape(DEPTH, G, flat), c_im.reshape(DEPTH, G, flat),
            nat(lam_re), nat(lam_im), nat(dt3))
    big = pl.BlockSpec((1, G, flat), lambda l: (l, 0, 0))
    small = pl.BlockSpec((1, SLABS, SLAB_STATE), lambda l: (l, 0, 0))
    return pl.pallas_call(
        functools.partial(_s5_prep_kernel, seg=seg),
        out_shape=(jax.ShapeDtypeStruct((DEPTH, S5_T, 2, G, flat), F32),
                   jax.ShapeDtypeStruct((DEPTH, S5_T + 1, 2, G, flat), F32),
                   jax.ShapeDtypeStruct((DEPTH, seg + 1, 2, SLABS, SLAB_STATE), F32),
                   jax.ShapeDtypeStruct((DEPTH, 2, SLABS, SLAB_STATE), F32)),
        grid=(DEPTH,),
        in_specs=[big] * 10 + [small] * 3,
        out_specs=(pl.BlockSpec((1, S5_T, 2, G, flat), lambda l: (l, 0, 0, 0, 0)),
                   pl.BlockSpec((1, S5_T + 1, 2, G, flat), lambda l: (l, 0, 0, 0, 0)),
                   pl.BlockSpec((1, seg + 1, 2, SLABS, SLAB_STATE), lambda l: (l, 0, 0, 0, 0)),
                   pl.BlockSpec((1, 2, SLABS, SLAB_STATE), lambda l: (l, 0, 0, 0))),
        compiler_params=_cp(("parallel",)),
        name="s5_discretize",
    )(*args)


def _embed_be(be):
    T = be.shape[0]
    x = be.reshape(T, 2, SLABS, 8, S5_STATE, S5_GROUP).transpose(2, 0, 3, 5, 1, 4)
    eye = jnp.eye(8, dtype=bool)
    x = jnp.where(eye[None, None, :, None, None, :, None], x[:, :, :, :, :, None, :], 0.0)
    return x.reshape(SLABS, T * LANES, 2 * SLAB_STATE)


def _embed_cp(cpw):
    Dn = cpw.shape[0]
    x = cpw.reshape(Dn, 2, SLABS, 8, S5_GROUP, S5_STATE).transpose(2, 0, 1, 3, 5, 4)
    eye = jnp.eye(8, dtype=bool)
    x = jnp.where(eye[None, None, None, :, None, :, None], x[:, :, :, :, :, None, :], 0.0)
    return x.reshape(SLABS, Dn, 2 * SLAB_STATE, LANES)


def _toep_kernel(b_ref, c_ref, o_ref):
    o_ref[0, 0] = _dot(b_ref[0], c_ref[0, 0], HI)


def _toep(bst, cpe):
    return pl.pallas_call(
        _toep_kernel,
        out_shape=jax.ShapeDtypeStruct((SLABS, S5_T, LANES, LANES), F32),
        grid=(SLABS, S5_T),
        in_specs=[pl.BlockSpec((1, LANES, 2 * SLAB_STATE), lambda k, d: (k, 0, 0)),
                  pl.BlockSpec((1, 1, 2 * SLAB_STATE, LANES), lambda k, d: (k, d, 0, 0))],
        out_specs=pl.BlockSpec((1, 1, LANES, LANES), lambda k, d: (k, d, 0, 0)),
        compiler_params=_cp(("parallel", "parallel")),
        name="s5_conv_blocks",
    )(bst, cpe)


def _pair_tiles(K):
    Z = jnp.zeros_like(K[:, 0])
    kd = lambda d: K[:, d] if d >= 0 else Z
    tiles = []
    for dd in range(S5_T // 2):
        top = jnp.concatenate([kd(2 * dd), kd(2 * dd + 1)], axis=-1)
        bot = jnp.concatenate([kd(2 * dd - 1), kd(2 * dd)], axis=-1)
        tiles.append(jnp.concatenate([top, bot], axis=-2))
    return jnp.stack(tiles, axis=1)


def _s5_seq_kernel(up_ref, be_ref, tp_ref, cpm_ref, pt_ref, s0_ref, dsk_ref,
                   yg_ref, sfin_ref, e_scr, sx_scr, *, nc):
    seg = nc // 8
    W = SLAB_STATE
    nt = W // LANES
    u = up_ref[0, 0]
    ub = u.astype(BF16)
    e = _dot(ub, be_ref[0])
    for c in range(2 * nt):
        e_scr[c] = e[:, c * LANES:(c + 1) * LANES]

    def tiles(row):
        return [(row[:, c * LANES:(c + 1) * LANES], row[:, W + c * LANES:W + (c + 1) * LANES])
                for c in range(nt)]

    a8 = [(jnp.broadcast_to(r, (8, LANES)), jnp.broadcast_to(i, (8, LANES)))
          for r, i in tiles(pt_ref[0, 1:2, :])]

    def step(i, carry):
        rows = pl.ds(i, 8, stride=seg)
        new = []
        for c in range(nt):
            sr, si = carry[c]
            ar, ai = a8[c]
            sx_scr[c, rows, :] = sr
            sx_scr[nt + c, rows, :] = si
            new.append((ar * sr - ai * si + e_scr[c, rows, :],
                        ar * si + ai * sr + e_scr[nt + c, rows, :]))
        return tuple(new)

    zero = jnp.zeros((8, LANES), F32)
    ends = lax.fori_loop(0, seg, step, tuple((zero, zero) for _ in range(nt)))

    al = tiles(pt_ref[0, seg:seg + 1, :])
    cur = tiles(s0_ref[0, 0])
    car = []
    for c in range(nt):
        alr, ali = al[c]
        cr, ci = cur[c]
        sr, si = ends[c]
        crs, cis = [], []
        for j in range(8):
            crs.append(cr)
            cis.append(ci)
            cr, ci = (alr * cr - ali * ci + sr[j:j + 1], alr * ci + ali * cr + si[j:j + 1])
        sfin_ref[0, 0, :, c * LANES:(c + 1) * LANES] = cr
        sfin_ref[0, 0, :, W + c * LANES:W + (c + 1) * LANES] = ci
        car.append((jnp.concatenate(crs, axis=0), jnp.concatenate(cis, axis=0)))

    def corr(i, _):
        rows = pl.ds(i, 8, stride=seg)
        pw = tiles(pt_ref[0, pl.ds(i, 1), :])
        for c in range(nt):
            pr, pi = pw[c]
            cr, ci = car[c]
            sx_scr[c, rows, :] = sx_scr[c, rows, :] + (pr * cr - pi * ci)
            sx_scr[nt + c, rows, :] = sx_scr[nt + c, rows, :] + (pr * ci + pi * cr)
        return 0

    lax.fori_loop(0, seg, corr, 0)

    sx = jnp.concatenate([sx_scr[c] for c in range(2 * nt)], axis=-1)
    y = _dot(sx.astype(BF16), cpm_ref[0])
    TW = 2 * LANES
    for tq in range(S5_T // 2):
        acc = y[:, tq * TW:(tq + 1) * TW]
        for tpi in range(tq + 1):
            acc = acc + _dot(ub[:, tpi * TW:(tpi + 1) * TW], tp_ref[0, tq - tpi])
        acc = acc + dsk_ref[0, :, tq * TW:(tq + 1) * TW] * u[:, tq * TW:(tq + 1) * TW]
        yg_ref[0, 0, :, tq * TW:(tq + 1) * TW] = jax.nn.gelu(acc)


def _s5_seq(up, be_emb, tp, cpm, pt, s0, dsk):
    _, B, nc, _ = up.shape
    seg = nc // 8
    W2 = 2 * SLAB_STATE
    yg, sfin = pl.pallas_call(
        functools.partial(_s5_seq_kernel, nc=nc),
        out_shape=(jax.ShapeDtypeStruct((SLABS, B, nc, S5_T * LANES), F32),
                   jax.ShapeDtypeStruct((SLABS, B, 1, W2), F32)),
        grid=(SLABS, B),
        in_specs=[pl.BlockSpec((1, 1, nc, S5_T * LANES), lambda k, b: (k, b, 0, 0)),
                  pl.BlockSpec((1, S5_T * LANES, W2), lambda k, b: (k, 0, 0)),
                  pl.BlockSpec((1, S5_T // 2, 2 * LANES, 2 * LANES), lambda k, b: (k, 0, 0, 0)),
                  pl.BlockSpec((1, W2, S5_T * LANES), lambda k, b: (k, 0, 0)),
                  pl.BlockSpec((1, seg + 1, W2), lambda k, b: (k, 0, 0)),
                  pl.BlockSpec((1, 1, 1, W2), lambda k, b: (k, b, 0, 0)),
                  pl.BlockSpec((1, 1, S5_T * LANES), lambda k, b: (k, 0, 0))],
        out_specs=(pl.BlockSpec((1, 1, nc, S5_T * LANES), lambda k, b: (k, b, 0, 0)),
                   pl.BlockSpec((1, 1, 1, W2), lambda k, b: (k, b, 0, 0))),
        scratch_shapes=[pltpu.VMEM((W2 // LANES, nc, LANES), F32),
                        pltpu.VMEM((W2 // LANES, nc, LANES), F32)],
        compiler_params=_cp(("parallel", "parallel")),
        name="s5_seq",
    )(up, be_emb, tp, cpm, pt, s0, dsk)
    return yg, sfin


def _s5_step_kernel(u_ref, b_ref, c_ref, a_ref, s0_ref, d_ref, yg_ref, s1_ref):
    W = SLAB_STATE
    u = u_ref[0]
    bu = _dot(u, b_ref[0], HI)
    ar = a_ref[0, :, 0:W]
    ai = a_ref[0, :, W:2 * W]
    sr = s0_ref[0, :, 0:W]
    si = s0_ref[0, :, W:2 * W]
    nr = ar * sr - ai * si + bu[:, 0:W]
    ni = ar * si + ai * sr + bu[:, W:2 * W]
    s1_ref[0, :, 0:W] = nr
    s1_ref[0, :, W:2 * W] = ni
    s1 = jnp.concatenate([nr, ni], axis=-1)
    y = _dot(s1, c_ref[0], HI) + d_ref[0] * u
    yg_ref[0] = jax.nn.gelu(y)


def _s5_step(u_slab, bst, c0, a1, s0, d1):
    _, N, _ = u_slab.shape
    W2 = 2 * SLAB_STATE
    return pl.pallas_call(
        _s5_step_kernel,
        out_shape=(jax.ShapeDtypeStruct((SLABS, N, LANES), F32),
                   jax.ShapeDtypeStruct((SLABS, N, W2), F32)),
        grid=(SLABS,),
        in_specs=[pl.BlockSpec((1, N, LANES), lambda k: (k, 0, 0)),
                  pl.BlockSpec((1, LANES, W2), lambda k: (k, 0, 0)),
                  pl.BlockSpec((1, W2, LANES), lambda k: (k, 0, 0)),
                  pl.BlockSpec((1, 1, W2), lambda k: (k, 0, 0)),
                  pl.BlockSpec((1, N, W2), lambda k: (k, 0, 0)),
                  pl.BlockSpec((1, 1, LANES), lambda k: (k, 0, 0))],
        out_specs=(pl.BlockSpec((1, N, LANES), lambda k: (k, 0, 0)),
                   pl.BlockSpec((1, N, W2), lambda k: (k, 0, 0))),
        compiler_params=_cp(("parallel",)),
        name="s5_step",
    )(u_slab, bst, c0, a1, s0, d1)


def _l2n(x):
    return x * lax.rsqrt(jnp.sum(x * x, axis=-1, keepdims=True) + L2_EPS)


def _split_bf16(x):
    hi = x.astype(BF16)
    return hi, (x - hi.astype(F32)).astype(BF16)


def _unit_lower_solve(As, rhss):
    n = GDN_C
    row = lax.broadcasted_iota(jnp.int32, (n, n), 0)
    col = lax.broadcasted_iota(jnp.int32, (n, n), 1)
    eye = (row == col).astype(F32)
    same8 = (row // 8) == (col // 8)
    Qs = [jnp.where(same8, -A, 0.0) for A in As]
    invs = [eye + Q for Q in Qs]
    for _ in range(2):
        Qs = [_dotb(Q, Q) for Q in Qs]
        invs = [inv + _dotb(inv, Q) for inv, Q in zip(invs, Qs)]
    s = 8
    while s < n:
        sib = ((row // (2 * s)) == (col // (2 * s))) & ((row // s) != (col // s))
        offs = [jnp.where(sib, A, 0.0).astype(BF16) for A in As]
        invb = [inv.astype(BF16) for inv in invs]
        tmp = [_dot(off, ib) for off, ib in zip(offs, invb)]
        invs = [inv - _dot(ib, t.astype(BF16)) for inv, ib, t in zip(invs, invb, tmp)]
        s *= 2
    invb = [inv.astype(BF16) for inv in invs]
    x0s = [_dot(ib, rhs.astype(BF16)) for ib, rhs in zip(invb, rhss)]
    res = []
    for A, x0, rhs in zip(As, x0s, rhss):
        ah, al = _split_bf16(A)
        xh, xl = _split_bf16(x0)
        res.append(rhs - x0 - (_dot(ah, xh) + _dot(ah, xl) + _dot(al, xh)))
    return [x0 + _dot(ib, r.astype(BF16)) for x0, ib, r in zip(x0s, invb, res)]


def _gdn_tile(qc_scr, gc, beta, z_ref, nw, o_ref, s_scr, tl):
    C, DK, H = GDN_C, GDN_DK, GDN_HEADS
    nchunk = tl // C
    probs = [(c, h) for c in range(nchunk) for h in range(H)]
    row = lax.broadcasted_iota(jnp.int32, (C, C), 0)
    col = lax.broadcasted_iota(jnp.int32, (C, C), 1)
    tri = row >= col
    strict = row > col

    def blk(c, off):
        return qc_scr[c * C:(c + 1) * C, off:off + DK]

    q = [_l2n(blk(c, h * DK)) * (DK ** -0.5) for c, h in probs]
    k = [_l2n(blk(c, GDN_WIDTH + h * DK)) for c, h in probs]
    v = [blk(c, 2 * GDN_WIDTH + h * DK) for c, h in probs]
    gcb = [jnp.broadcast_to(gc[c * C:(c + 1) * C, h:h + 1], (C, DK)) for c, h in probs]
    bb = [jnp.broadcast_to(beta[c * C:(c + 1) * C, H + h:H + h + 1], (C, DK)) for c, h in probs]
    decay = []
    for g in gcb:
        diff = g - g.T
        decay.append(jnp.where(tri, jnp.exp(jnp.where(tri, diff, 0.0)), 0.0))
    kbf = [x.astype(BF16) for x in k]
    kb = [x * b for x, b in zip(k, bb)]
    A = [jnp.where(strict, _dot_nt(x.astype(BF16), y) * d, 0.0) for x, y, d in zip(kb, kbf, decay)]
    egc = [jnp.exp(g) for g in gcb]
    rhs = [jnp.concatenate([x * b, y * e], axis=-1) for x, b, y, e in zip(v, bb, kb, egc)]
    sol = _unit_lower_solve(A, rhs)
    attn = [jnp.where(tri, _dot_nt(x.astype(BF16), y) * d, 0.0).astype(BF16)
            for x, y, d in zip(q, kbf, decay)]
    glast = [g[C - 1:C, :] for g in gcb]
    wq = [jnp.concatenate([s[:, DK:], x * e], axis=0).astype(BF16) for s, x, e in zip(sol, q, egc)]
    kg = [(x * jnp.exp(gl - g)).astype(BF16) for x, gl, g in zip(k, glast, gcb)]

    for c in range(nchunk):
        ps = [c * H + h for h in range(H)]
        S = [s_scr[h] for h in range(H)]
        ws = [_dot(wq[p], S[h].astype(BF16)) for h, p in enumerate(ps)]
        v_new = [sol[p][:, 0:DK] - w[0:C] for p, w in zip(ps, ws)]
        vb = [x.astype(BF16) for x in v_new]
        o = [w[C:] + _dot(attn[p], x) for p, w, x in zip(ps, ws, vb)]
        for h, p in enumerate(ps):
            s_scr[h] = S[h] * jnp.exp(glast[p]) + _dot_tn(kg[p], vb[h])
            zh = z_ref[0, c * C:(c + 1) * C, h * DK:(h + 1) * DK]
            on = o[h] * lax.rsqrt(jnp.mean(o[h] * o[h], axis=-1, keepdims=True) + NORM_EPS) * nw
            o_ref[0, c * C:(c + 1) * C, h * DK:(h + 1) * DK] = on * _silu(zh)


def _gdn_seq_kernel(qkv_ref, z_ref, ab_ref, cw_ref, alog_ref, dtb_ref, nw_ref, conv0_ref, s0_ref,
                    o_ref, sfin_ref, xp_scr, qc_scr, s_scr, *, tl):
    lt = pl.program_id(1)

    @pl.when(lt == 0)
    def _():
        xp_scr[0:8, :] = jnp.zeros((8, QKV_WIDTH), F32)
        xp_scr[8 - (GDN_CONV - 1):8, :] = conv0_ref[0]
        s_scr[...] = s0_ref[0]

    xp_scr[8:8 + tl, :] = qkv_ref[0]
    conv = cw_ref[0:1, :] * xp_scr[5:5 + tl, :]
    for j in range(1, GDN_CONV):
        conv = conv + cw_ref[j:j + 1, :] * xp_scr[5 + j:5 + j + tl, :]
    xp_scr[0:8, :] = xp_scr[tl:tl + 8, :]
    qc_scr[...] = _silu(conv)

    ab = ab_ref[0]
    g = -jnp.exp(alog_ref[...]) * jax.nn.softplus(ab + dtb_ref[...])
    beta = jax.nn.sigmoid(ab)
    row = lax.broadcasted_iota(jnp.int32, (tl, tl), 0)
    col = lax.broadcasted_iota(jnp.int32, (tl, tl), 1)
    csum = ((row >= col) & ((row // GDN_C) == (col // GDN_C))).astype(F32)
    gc = _dot(csum, g, HI)
    _gdn_tile(qc_scr, gc, beta, z_ref, nw_ref[...], o_ref, s_scr, tl)

    @pl.when(lt == pl.num_programs(1) - 1)
    def _():
        sfin_ref[0] = s_scr[...]


def _gdn_seq(qkv, z, ab, conv_w, alog, dtb, nw, conv0, s0):
    B, L, _ = qkv.shape
    tl = min(256, L)
    return pl.pallas_call(
        functools.partial(_gdn_seq_kernel, tl=tl),
        out_shape=(jax.ShapeDtypeStruct((B, L, GDN_WIDTH), F32),
                   jax.ShapeDtypeStruct((B, GDN_HEADS, GDN_DK, GDN_DK), F32)),
        grid=(B, L // tl),
        in_specs=[pl.BlockSpec((1, tl, QKV_WIDTH), lambda b, i: (b, i, 0)),
                  pl.BlockSpec((1, tl, GDN_WIDTH), lambda b, i: (b, i, 0)),
                  pl.BlockSpec((1, tl, LANES), lambda b, i: (b, i, 0)),
                  pl.BlockSpec((GDN_CONV, QKV_WIDTH), lambda b, i: (0, 0)),
                  pl.BlockSpec((1, LANES), lambda b, i: (0, 0)),
                  pl.BlockSpec((1, LANES), lambda b, i: (0, 0)),
                  pl.BlockSpec((1, GDN_DK), lambda b, i: (0, 0)),
                  pl.BlockSpec((1, GDN_CONV - 1, QKV_WIDTH), lambda b, i: (b, 0, 0)),
                  pl.BlockSpec((1, GDN_HEADS, GDN_DK, GDN_DK), lambda b, i: (b, 0, 0, 0))],
        out_specs=(pl.BlockSpec((1, tl, GDN_WIDTH), lambda b, i: (b, i, 0)),
                   pl.BlockSpec((1, GDN_HEADS, GDN_DK, GDN_DK), lambda b, i: (b, 0, 0, 0))),
        scratch_shapes=[pltpu.VMEM((tl + 8, QKV_WIDTH), F32),
                        pltpu.VMEM((tl, QKV_WIDTH), F32),
                        pltpu.VMEM((GDN_HEADS, GDN_DK, GDN_DK), F32)],
        compiler_params=_cp(("parallel", "arbitrary")),
        name="gdn_seq",
    )(qkv, z, ab, conv_w, alog, dtb, nw, conv0, s0)


GDN_STEP_ROWS = 8


def _gdn_step_kernel(qkv_ref, z_ref, ab_ref, cw_ref, alog_ref, dtb_ref, nw_ref, conv0_ref, s0_ref,
                     o_ref, s1_ref):
    nb = GDN_STEP_ROWS
    W = QKV_WIDTH
    conv = cw_ref[0:1, :] * conv0_ref[:, 0:W]
    conv = conv + cw_ref[1:2, :] * conv0_ref[:, W:2 * W]
    conv = conv + cw_ref[2:3, :] * conv0_ref[:, 2 * W:3 * W]
    conv = conv + cw_ref[3:4, :] * qkv_ref[...]
    qc = _silu(conv)
    ab = ab_ref[...]
    eg = jnp.exp(-jnp.exp(alog_ref[...]) * jax.nn.softplus(ab + dtb_ref[...]))
    beta = jax.nn.sigmoid(ab)
    eye = (lax.broadcasted_iota(jnp.int32, (GDN_DK, GDN_DK), 0)
           == lax.broadcasted_iota(jnp.int32, (GDN_DK, GDN_DK), 1)).astype(F32)
    for h in range(GDN_HEADS):
        q = _l2n(qc[:, h * GDN_DK:(h + 1) * GDN_DK]) * (GDN_DK ** -0.5)
        k = _l2n(qc[:, GDN_WIDTH + h * GDN_DK:GDN_WIDTH + (h + 1) * GDN_DK])
        v = qc[:, 2 * GDN_WIDTH + h * GDN_DK:2 * GDN_WIDTH + (h + 1) * GDN_DK]
        kT = _dot_nt(eye, k, HI)
        qT = _dot_nt(eye, q, HI)
        qk = jnp.sum(q * k, axis=-1, keepdims=True)
        for j in range(nb):
            S = s0_ref[j, h]
            kc = jnp.broadcast_to(kT[:, j:j + 1], (GDN_DK, GDN_DK))
            qcb = jnp.broadcast_to(qT[:, j:j + 1], (GDN_DK, GDN_DK))
            kS = jnp.sum(kc * S, axis=0, keepdims=True)
            qS = jnp.sum(qcb * S, axis=0, keepdims=True)
            egj = eg[j:j + 1, h:h + 1]
            bj = beta[j:j + 1, GDN_HEADS + h:GDN_HEADS + h + 1]
            v_new = bj * v[j:j + 1, :] - (bj * egj) * kS
            o = egj * qS + qk[j:j + 1, :] * v_new
            s1_ref[j, h] = S * egj + kc * v_new
            zh = z_ref[j:j + 1, h * GDN_DK:(h + 1) * GDN_DK]
            on = o * lax.rsqrt(jnp.mean(o * o, axis=-1, keepdims=True) + NORM_EPS) * nw_ref[...]
            o_ref[j:j + 1, h * GDN_DK:(h + 1) * GDN_DK] = on * _silu(zh)


def _gdn_step(qkv, z, ab, conv_w, alog, dtb, nw, conv0, s0):
    N = qkv.shape[0]
    nb = GDN_STEP_ROWS
    row = lambda w: pl.BlockSpec((nb, w), lambda i: (i, 0))
    const = lambda r, w: pl.BlockSpec((r, w), lambda i: (0, 0))
    return pl.pallas_call(
        _gdn_step_kernel,
        out_shape=(jax.ShapeDtypeStruct((N, GDN_WIDTH), F32),
                   jax.ShapeDtypeStruct((N, GDN_HEADS, GDN_DK, GDN_DK), F32)),
        grid=(N // nb,),
        in_specs=[row(QKV_WIDTH), row(GDN_WIDTH), row(LANES), const(GDN_CONV, QKV_WIDTH),
                  const(1, LANES), const(1, LANES), const(1, GDN_DK), row(3 * QKV_WIDTH),
                  pl.BlockSpec((nb, GDN_HEADS, GDN_DK, GDN_DK), lambda i: (i, 0, 0, 0))],
        out_specs=(row(GDN_WIDTH),
                   pl.BlockSpec((nb, GDN_HEADS, GDN_DK, GDN_DK), lambda i: (i, 0, 0, 0))),
        compiler_params=_cp(("parallel",)),
        name="gdn_step",
    )(qkv, z, ab, conv_w, alog, dtb, nw, conv0, s0)


def _merge_kernel(yg_ref, og_ref, ga_ref, gb_ref, x_ref, gt_ref, wglu_ref, wgo_ref, wout_ref,
                  gf_ref, scf_ref, shf_ref, wr_ref,
                  xo_ref, h_ref, lg_ref, *scr, hi, chunked):
    if chunked:
        y_scr = scr[-1]
        scr = scr[:-1]
        nrow = y_scr.shape[1] // S5_T
        for k in range(SLABS):
            for t in range(S5_T):
                y_scr[k, pl.ds(t, nrow, stride=S5_T), :] = yg_ref[k, 0, :, t * LANES:(t + 1) * LANES]
        y = jnp.concatenate([y_scr[k] for k in range(SLABS)], axis=-1)
    else:
        y = jnp.concatenate([yg_ref[k, 0] for k in range(SLABS)], axis=-1)
    if hi:
        wglu, wgo, wout = wglu_ref[...], wgo_ref[...], wout_ref[...]
        mm = lambda a, w: _dot(a, w, HI)
    else:
        wglu_s, wgo_s, wout_s = scr

        @pl.when((pl.program_id(0) == 0) & (pl.program_id(1) == 0))
        def _():
            wglu_s[...] = wglu_ref[...].astype(BF16)
            wgo_s[...] = wgo_ref[...].astype(BF16)
            wout_s[...] = wout_ref[...].astype(BF16)

        wglu, wgo, wout = wglu_s[...], wgo_s[...], wout_s[...]
        mm = lambda a, w: _dot(a.astype(BF16), w)

    glu = mm(y, wglu)
    branch_a = glu[:, 0:D_MODEL] * jax.nn.sigmoid(glu[:, D_MODEL:])
    branch_b = mm(og_ref[0], wgo)
    merged = ga_ref[0] * branch_a + gb_ref[0] * branch_b
    out = mm(merged, wout)
    x = x_ref[0] + gt_ref[0] * out
    xo_ref[0] = x
    ms = jnp.mean(x * x, axis=-1, keepdims=True)
    h = x * lax.rsqrt(ms + NORM_EPS) * gf_ref[...]
    h = h * (1.0 + scf_ref[0]) + shf_ref[0]
    h_ref[0] = h
    lg_ref[0] = _dot(h, wr_ref[...], HI)


def _merge(yg, og, ga, gb, x, gt, wglu, wgo, wout, gf, scf, shf, wr, *, tm, hi, chunked):
    B, L, D = x.shape
    lm = gt.shape[1]
    tmm = 1 if lm == 1 else tm
    mod_map = (lambda b, i: (b, 0, 0)) if lm == 1 else (lambda b, i: (b, i, 0))
    row = lambda w: pl.BlockSpec((1, tm, w), lambda b, i: (b, i, 0))
    const = lambda r, w: pl.BlockSpec((r, w), lambda b, i: (0, 0))
    mod = pl.BlockSpec((1, tmm, D), mod_map)
    scratch = [] if hi else [pltpu.VMEM((S5_WIDTH, 2 * D), BF16), pltpu.VMEM((GDN_WIDTH, D), BF16),
                             pltpu.VMEM((D, D), BF16)]
    if chunked:
        scratch = scratch + [pltpu.VMEM((SLABS, tm, LANES), F32)]
        yg_spec = pl.BlockSpec((SLABS, 1, tm // S5_T, S5_T * LANES), lambda b, i: (0, b, i, 0))
    else:
        yg_spec = pl.BlockSpec((SLABS, 1, tm, LANES), lambda b, i: (0, b, i, 0))
    return pl.pallas_call(
        functools.partial(_merge_kernel, hi=hi, chunked=chunked),
        out_shape=(jax.ShapeDtypeStruct((B, L, D), F32),
                   jax.ShapeDtypeStruct((B, L, D), F32),
                   jax.ShapeDtypeStruct((B, L, N_EXPERTS), F32)),
        grid=(B, L // tm),
        in_specs=[yg_spec,
                  row(GDN_WIDTH), row(D), row(D), row(D), mod,
                  const(S5_WIDTH, 2 * D), const(GDN_WIDTH, D), const(D, D),
                  const(1, D), mod, mod, const(D, N_EXPERTS)],
        out_specs=(row(D), row(D), row(N_EXPERTS)),
        scratch_shapes=scratch,
        compiler_params=_cp(("arbitrary", "arbitrary")),
        name="merge_out_proj",
    )(yg, og, ga, gb, x, gt, wglu, wgo, wout, gf, scf, shf, wr)


FF_TILE = 512


def _finish(x, gfin_ref, final):
    if not final:
        return x
    ms = jnp.mean(x * x, axis=-1, keepdims=True)
    return x * lax.rsqrt(ms + NORM_EPS) * gfin_ref[...]


def _ffn_kernel(h_ref, x_ref, gt_ref, wg_ref, wu_ref, wd_ref, gfin_ref, o_ref, acc_scr, *, hi, final):
    j = pl.program_id(2)
    if hi:
        h = h_ref[0]
        mm = lambda a, w: _dot(a, w, HI)
    else:
        h = h_ref[0].astype(BF16)
        mm = lambda a, w: _dot(a.astype(BF16), w.astype(BF16))
    act = _silu(mm(h, wg_ref[...])) * mm(h, wu_ref[...])
    part = mm(act, wd_ref[...])

    @pl.when(j == 0)
    def _():
        acc_scr[...] = part

    @pl.when(j > 0)
    def _():
        acc_scr[...] = acc_scr[...] + part

    @pl.when(j == pl.num_programs(2) - 1)
    def _():
        o_ref[0] = _finish(x_ref[0] + gt_ref[0] * acc_scr[...], gfin_ref, final)


def _ffn(h, x, gt, w_gu, w_down, gfin, *, tm, hi, final):
    B, L, D = x.shape
    lm = gt.shape[1]
    tmm = 1 if lm == 1 else tm
    mod_map = (lambda b, i, j: (b, 0, 0)) if lm == 1 else (lambda b, i, j: (b, i, 0))
    nj = D_FF // FF_TILE
    row = pl.BlockSpec((1, tm, D), lambda b, i, j: (b, i, 0))
    return pl.pallas_call(
        functools.partial(_ffn_kernel, hi=hi, final=final),
        out_shape=jax.ShapeDtypeStruct((B, L, D), F32),
        grid=(B, L // tm, nj),
        in_specs=[row, row, pl.BlockSpec((1, tmm, D), mod_map),
                  pl.BlockSpec((D, FF_TILE), lambda b, i, j: (0, j)),
                  pl.BlockSpec((D, FF_TILE), lambda b, i, j: (0, nj + j)),
                  pl.BlockSpec((FF_TILE, D), lambda b, i, j: (j, 0)),
                  pl.BlockSpec((1, D), lambda b, i, j: (0, 0))],
        out_specs=row,
        scratch_shapes=[pltpu.VMEM((tm, D), F32)],
        compiler_params=_cp(("parallel", "parallel", "arbitrary")),
        name="ffn_dense",
    )(h, x, gt, w_gu, w_gu, w_down, gfin)


def _route(lg):
    idx = lax.broadcasted_iota(jnp.int32, lg.shape, 1)
    m1 = jnp.max(lg, axis=-1, keepdims=True)
    i1 = jnp.min(jnp.where(lg == m1, idx, N_EXPERTS), axis=-1, keepdims=True)
    lg2 = jnp.where(idx == i1, -jnp.inf, lg)
    m2 = jnp.max(lg2, axis=-1, keepdims=True)
    i2 = jnp.min(jnp.where(lg2 == m2, idx, N_EXPERTS), axis=-1, keepdims=True)
    e2 = jnp.exp(m2 - m1)
    w1 = 1.0 / (1.0 + e2)
    w2 = e2 / (1.0 + e2)
    return jnp.where(idx == i1, w1, 0.0) + jnp.where(idx == i2, w2, 0.0)


def _moe_kernel(h_ref, x_ref, gt_ref, lg_ref, br_ref, wg_ref, wu_ref, wd_ref, gfin_ref, o_ref,
                acc_scr, comb_scr, *, final):
    e = pl.program_id(2)
    j = pl.program_id(3)

    @pl.when((e == 0) & (j == 0))
    def _():
        comb_scr[...] = _route(lg_ref[0] + br_ref[...])
        acc_scr[...] = jnp.zeros_like(acc_scr)

    idx = lax.broadcasted_iota(jnp.int32, comb_scr.shape, 1)
    ce = jnp.sum(jnp.where(idx == e, comb_scr[...], 0.0), axis=-1, keepdims=True)
    h = h_ref[0].astype(BF16)
    act = _silu(_dot(h, wg_ref[0].astype(BF16))) * _dot(h, wu_ref[0].astype(BF16))
    acc_scr[...] = acc_scr[...] + ce * _dot(act.astype(BF16), wd_ref[0].astype(BF16))

    @pl.when((e == pl.num_programs(2) - 1) & (j == pl.num_programs(3) - 1))
    def _():
        o_ref[0] = _finish(x_ref[0] + gt_ref[0] * acc_scr[...], gfin_ref, final)


def _moe(h, x, gt, lg, b_r, w_gu, w_down, gfin, *, tm, final):
    B, L, D = x.shape
    lm = gt.shape[1]
    tmm = 1 if lm == 1 else tm
    mod_map = (lambda b, i, e, j: (b, 0, 0)) if lm == 1 else (lambda b, i, e, j: (b, i, 0))
    nj = D_FF // FF_TILE
    row = pl.BlockSpec((1, tm, D), lambda b, i, e, j: (b, i, 0))
    return pl.pallas_call(
        functools.partial(_moe_kernel, final=final),
        out_shape=jax.ShapeDtypeStruct((B, L, D), F32),
        grid=(B, L // tm, N_EXPERTS, nj),
        in_specs=[row, row, pl.BlockSpec((1, tmm, D), mod_map),
                  pl.BlockSpec((1, tm, N_EXPERTS), lambda b, i, e, j: (b, i, 0)),
                  pl.BlockSpec((1, N_EXPERTS), lambda b, i, e, j: (0, 0)),
                  pl.BlockSpec((1, D, FF_TILE), lambda b, i, e, j: (e, 0, j)),
                  pl.BlockSpec((1, D, FF_TILE), lambda b, i, e, j: (e, 0, nj + j)),
                  pl.BlockSpec((1, FF_TILE, D), lambda b, i, e, j: (e, j, 0)),
                  pl.BlockSpec((1, D), lambda b, i, e, j: (0, 0))],
        out_specs=row,
        scratch_shapes=[pltpu.VMEM((tm, D), F32), pltpu.VMEM((tm, N_EXPERTS), F32)],
        compiler_params=_cp(("parallel", "parallel", "arbitrary", "arbitrary")),
        name="moe_dense",
    )(h, x, gt, lg, b_r, w_gu, w_gu, w_down, gfin)


def _pad_lanes(v):
    return jnp.pad(v.reshape(1, -1), ((0, 0), (0, LANES - v.shape[-1])))


def _trunk(x, mod, states, p, s5m, *, seq):
    B, L, D = x.shape
    hi = not seq
    tm_proj = min(1024, L)
    tm = min(512, L)
    tm_ffn = min(1024, L)
    s5r0, s5i0, sg0, sc0 = states
    out_states = []
    for l in range(DEPTH):
        m = [mod[l][..., i * D:(i + 1) * D] for i in range(6)]
        sh_m, sc_m, gt_m, sh_f, sc_f, gt_f = m
        w_in = p['w_in'][l]
        w_gates = w_in[:, 2568:]
        w_ab = jnp.pad(w_in[:, 2560:2568], ((0, 0), (0, LANES - 8)))
        u, qkv, z, ga, gb, ab = _proj(x, p['g_mix'][l].reshape(1, D), sc_m, sh_m, w_in, w_gates, w_ab,
                                      tm=tm_proj, hi=hi, chunked=seq)
        alog = _pad_lanes(p['gdn_a_log'][l])
        dtb = _pad_lanes(p['gdn_dt_bias'][l])
        nw = p['gdn_norm_w'][l].reshape(1, GDN_DK)
        if seq:
            yg, sfin = _s5_seq(u, s5m['be'][l], s5m['tp'][l], s5m['cpm'][l], s5m['pt'][l],
                               jnp.zeros((SLABS, B, 1, 2 * SLAB_STATE), F32), s5m['dsk'][l])
            sfin = sfin.reshape(SLABS, B, 2, SLAB_STATE).transpose(2, 1, 0, 3)
            sr = sfin[0].reshape(B, S5_GROUPS, S5_STATE)
            si = sfin[1].reshape(B, S5_GROUPS, S5_STATE)
            og, sg = _gdn_seq(qkv, z, ab, p['gdn_conv_w'][l], alog, dtb, nw,
                              jnp.zeros((B, GDN_CONV - 1, QKV_WIDTH), F32),
                              jnp.zeros((B, GDN_HEADS, GDN_DK, GDN_DK), F32))
            cb = qkv[:, L - (GDN_CONV - 1):, :]
        else:
            n = L
            s0 = jnp.concatenate([s5r0[l].reshape(n, SLABS, SLAB_STATE),
                                  s5i0[l].reshape(n, SLABS, SLAB_STATE)], axis=-1).transpose(1, 0, 2)
            yg, s1 = _s5_step(u.reshape(SLABS, n, LANES), s5m['bst'][l], s5m['c0'][l], s5m['a1'][l],
                              s0, s5m['d1'][l])
            yg = yg.reshape(SLABS, 1, n, LANES)
            s1 = s1.transpose(1, 0, 2)
            sr = s1[:, :, :SLAB_STATE].reshape(n, S5_GROUPS, S5_STATE)
            si = s1[:, :, SLAB_STATE:].reshape(n, S5_GROUPS, S5_STATE)
            og, sg = _gdn_step(qkv.reshape(n, QKV_WIDTH), z.reshape(n, GDN_WIDTH), ab.reshape(n, LANES),
                               p['gdn_conv_w'][l], alog, dtb, nw,
                               sc0[l].reshape(n, (GDN_CONV - 1) * QKV_WIDTH), sg0[l])
            og = og.reshape(1, n, GDN_WIDTH)
            cb = jnp.concatenate([sc0[l][:, 1:, :], qkv.reshape(n, 1, QKV_WIDTH)], axis=1)
        x, h, lg = _merge(yg, og, ga, gb, x, gt_m, p['w_s5_glu'][l], p['w_gdn_out'][l], p['w_out'][l],
                          p['g_ffn'][l].reshape(1, D), sc_f, sh_f, p['w_router'][l // 2], tm=tm, hi=hi,
                          chunked=seq)
        final = l == DEPTH - 1
        gfin = p['g_final'].reshape(1, D)
        if l % 2 == 0:
            x = _ffn(h, x, gt_f, p['w_ffn_gate_up'][l // 2], p['w_ffn_down'][l // 2], gfin,
                     tm=tm_ffn, hi=hi, final=final)
        else:
            x = _moe(h, x, gt_f, lg, p['b_router'][l // 2].reshape(1, N_EXPERTS),
                     p['w_exp_gate_up'][l // 2], p['w_exp_down'][l // 2], gfin, tm=tm_ffn, final=final)
        out_states.append((sr, si, sg, cb))
    st = [jnp.stack([o[i] for o in out_states]) for i in range(4)]
    return x, st


def kernel(x_prompt, x_sample, c_prompt, c_sample, state_s5_re, state_s5_im, state_gdn, state_conv,
           g_mix, g_ffn, g_final, w_ada, b_ada, w_in, s5_lambda_re, s5_lambda_im, s5_log_dt,
           s5_b_re, s5_b_im, s5_c_re, s5_c_im, s5_d, w_s5_glu, gdn_conv_w, gdn_a_log, gdn_dt_bias,
           gdn_norm_w, w_gdn_out, w_out, w_ffn_gate_up, w_ffn_down, w_router, b_router,
           w_exp_gate_up, w_exp_down):
    p = dict(g_mix=g_mix, g_ffn=g_ffn, g_final=g_final, w_in=w_in, w_s5_glu=w_s5_glu,
             gdn_conv_w=gdn_conv_w, gdn_a_log=gdn_a_log, gdn_dt_bias=gdn_dt_bias,
             gdn_norm_w=gdn_norm_w, w_gdn_out=w_gdn_out, w_out=w_out, w_ffn_gate_up=w_ffn_gate_up,
             w_ffn_down=w_ffn_down, w_router=w_router, b_router=b_router,
             w_exp_gate_up=w_exp_gate_up, w_exp_down=w_exp_down)
    nbp, L, D = x_prompt.shape
    nbs = x_sample.shape[0]

    mod = _ada(jnp.concatenate([c_prompt, c_sample], axis=0), w_ada, b_ada)
    mod_p = mod[:, :nbp].reshape(DEPTH, nbp, 1, 6 * D)
    mod_s = mod[:, nbp:].reshape(DEPTH, 1, nbs, 6 * D)

    seg = L // S5_T // 8
    be, cpw, pt, a1 = _s5_prep(s5_lambda_re, s5_lambda_im, s5_log_dt, s5_b_re, s5_b_im,
                               s5_c_re, s5_c_im, seg)
    W2 = 2 * SLAB_STATE
    s5m = dict(be=[], tp=[], cpm=[], pt=[], dsk=[], bst=[], c0=[], a1=[], d1=[])
    for l in range(DEPTH):
        be_e = _embed_be(be[l])
        cp_e = _embed_cp(cpw[l])
        bst = be_e[:, (S5_T - 1) * LANES:, :]
        s5m['be'].append(be_e.astype(BF16))
        s5m['tp'].append(_pair_tiles(_toep(bst, cp_e)).astype(BF16))
        s5m['cpm'].append(cp_e[:, 1:].transpose(0, 2, 1, 3).reshape(SLABS, W2, S5_T * LANES).astype(BF16))
        s5m['pt'].append(pt[l].transpose(2, 0, 1, 3).reshape(SLABS, seg + 1, W2))
        d1 = s5_d[l].reshape(SLABS, 1, LANES)
        s5m['dsk'].append(jnp.tile(d1, (1, 1, S5_T)))
        s5m['bst'].append(bst)
        s5m['c0'].append(cp_e[:, 0])
        s5m['a1'].append(a1[l].transpose(1, 0, 2).reshape(SLABS, 1, W2))
        s5m['d1'].append(d1)

    y_p, st_p = _trunk(x_prompt, mod_p, (None, None, None, None), p, s5m, seq=True)
    y_s, st_s = _trunk(x_sample.reshape(1, nbs, D), mod_s,
                       (state_s5_re, state_s5_im, state_gdn, state_conv), p, s5m, seq=False)
    return (y_p, y_s.reshape(nbs, 1, D), st_p[0], st_p[1], st_p[2], st_p[3],
            st_s[0], st_s[1], st_s[2], st_s[3])
```

```python
import functools

import jax
import jax.numpy as jnp
from jax import lax
from jax.experimental import pallas as pl
from jax.experimental.pallas import tpu as pltpu

F32 = jnp.float32
BF16 = jnp.bfloat16
HI = lax.Precision.HIGHEST

D_MODEL = 1024
DEPTH = 2
S5_WIDTH = 512
S5_GROUP = 16
S5_GROUPS = 32
S5_STATE = 64
GDN_HEADS = 4
GDN_DK = 128
GDN_WIDTH = 512
GDN_CONV = 4
QKV_WIDTH = 1536
D_FF = 3584
N_EXPERTS = 8
NORM_EPS = 1e-6
L2_EPS = 1e-6

LANES = 128
SLABS = S5_WIDTH // LANES
SLAB_STATE = (S5_GROUPS // SLABS) * S5_STATE
S5_T = 8
GDN_C = 128
VMEM_LIMIT = 56 * 1024 * 1024


def _cp(sem, vmem=VMEM_LIMIT):
    return pltpu.CompilerParams(dimension_semantics=sem, vmem_limit_bytes=vmem)


def _dot(a, b, prec=None):
    return jnp.dot(a, b, precision=prec, preferred_element_type=F32)


def _dotb(a, b):
    return jnp.dot(a.astype(BF16), b.astype(BF16), preferred_element_type=F32)


def _dot_nt(a, b, prec=None):
    return lax.dot_general(a, b, (((1,), (1,)), ((), ())), precision=prec,
                           preferred_element_type=F32)


def _dot_tn(a, b, prec=None):
    return lax.dot_general(a, b, (((0,), (0,)), ((), ())), precision=prec,
                           preferred_element_type=F32)


def _silu(x):
    return x * jax.nn.sigmoid(x)


def _ada_kernel(c_ref, w_ref, b_ref, o_ref):
    cs = _silu(c_ref[...])
    o_ref[0] = _dot(cs, w_ref[0], HI) + b_ref[0]


def _ada(c_all, w_ada, b_ada):
    n = c_all.shape[0]
    tn = 1536
    return pl.pallas_call(
        _ada_kernel,
        out_shape=jax.ShapeDtypeStruct((DEPTH, n, 6 * D_MODEL), F32),
        grid=(DEPTH, 6 * D_MODEL // tn),
        in_specs=[pl.BlockSpec((n, D_MODEL), lambda l, j: (0, 0)),
                  pl.BlockSpec((1, D_MODEL, tn), lambda l, j: (l, 0, j)),
                  pl.BlockSpec((1, 1, tn), lambda l, j: (l, 0, j))],
        out_specs=pl.BlockSpec((1, n, tn), lambda l, j: (l, 0, j)),
        compiler_params=_cp(("parallel", "parallel")),
        name="ada_mod",
    )(c_all, w_ada, b_ada.reshape(DEPTH, 1, 6 * D_MODEL))


def _proj_kernel(x_ref, g_ref, sc_ref, sh_ref, w_ref, wg_ref, wab_ref,
                 u_ref, qkv_ref, z_ref, ga_ref, gb_ref, ab_ref, h_scr, *us_scr, hi):
    j = pl.program_id(2)

    @pl.when(j == 0)
    def _():
        x = x_ref[0]
        ms = jnp.mean(x * x, axis=-1, keepdims=True)
        xn = x * lax.rsqrt(ms + NORM_EPS) * g_ref[...]
        h_scr[...] = (xn * (1.0 + sc_ref[0]) + sh_ref[0]).astype(h_scr.dtype)

    def mm(w):
        if hi:
            return _dot(h_scr[...], w, HI)
        return _dot(h_scr[...], w.astype(BF16))

    @pl.when(j == 0)
    def _():
        res = mm(w_ref[...])
        for k in range(SLABS):
            if not us_scr:
                u_ref[k, 0] = res[:, k * LANES:(k + 1) * LANES]
                continue
            us_scr[0][...] = res[:, k * LANES:(k + 1) * LANES]
            nrow = res.shape[0] // S5_T
            for t in range(S5_T):
                u_ref[k, 0, :, t * LANES:(t + 1) * LANES] = us_scr[0][pl.ds(t, nrow, stride=S5_T), :]

    @pl.when((j >= 1) & (j <= 3))
    def _():
        qkv_ref[0] = mm(w_ref[...])

    @pl.when(j == 4)
    def _():
        z_ref[0] = mm(w_ref[...])

    @pl.when((j == 5) | (j == 6))
    def _():
        ga_ref[0] = jax.nn.sigmoid(mm(wg_ref[...]))

    @pl.when((j == 7) | (j == 8))
    def _():
        gb_ref[0] = jax.nn.sigmoid(mm(wg_ref[...]))

    @pl.when(j == 9)
    def _():
        ab_ref[0] = mm(wab_ref[...])


def _proj(x, g, sc, sh, w_in, w_gates, w_ab, *, tm, hi, chunked):
    B, L, D = x.shape
    lm = sc.shape[1]
    tmm = 1 if lm == 1 else tm
    mod_map = (lambda b, i, j: (b, 0, 0)) if lm == 1 else (lambda b, i, j: (b, i, 0))
    tn = 512
    clampi = lambda j, lo, n: jnp.clip(j - lo, 0, n - 1)
    if chunked:
        u_shape = jax.ShapeDtypeStruct((SLABS, B, L // S5_T, S5_T * LANES), F32)
        u_spec = pl.BlockSpec((SLABS, 1, tm // S5_T, S5_T * LANES), lambda b, i, j: (0, b, i, 0))
    else:
        u_shape = jax.ShapeDtypeStruct((SLABS, B, L, LANES), F32)
        u_spec = pl.BlockSpec((SLABS, 1, tm, LANES), lambda b, i, j: (0, b, i, 0))
    outs = pl.pallas_call(
        functools.partial(_proj_kernel, hi=hi),
        out_shape=(u_shape,
                   jax.ShapeDtypeStruct((B, L, QKV_WIDTH), F32),
                   jax.ShapeDtypeStruct((B, L, GDN_WIDTH), F32),
                   jax.ShapeDtypeStruct((B, L, D), F32),
                   jax.ShapeDtypeStruct((B, L, D), F32),
                   jax.ShapeDtypeStruct((B, L, LANES), F32)),
        grid=(B, L // tm, 10),
        in_specs=[pl.BlockSpec((1, tm, D), lambda b, i, j: (b, i, 0)),
                  pl.BlockSpec((1, D), lambda b, i, j: (0, 0)),
                  pl.BlockSpec((1, tmm, D), mod_map),
                  pl.BlockSpec((1, tmm, D), mod_map),
                  pl.BlockSpec((D, tn), lambda b, i, j: (0, jnp.minimum(j, 4))),
                  pl.BlockSpec((D, tn), lambda b, i, j: (0, clampi(j, 5, 4))),
                  pl.BlockSpec((D, LANES), lambda b, i, j: (0, 0))],
        out_specs=(u_spec,
                   pl.BlockSpec((1, tm, tn), lambda b, i, j: (b, i, clampi(j, 1, 3))),
                   pl.BlockSpec((1, tm, tn), lambda b, i, j: (b, i, 0)),
                   pl.BlockSpec((1, tm, tn), lambda b, i, j: (b, i, clampi(j, 5, 2))),
                   pl.BlockSpec((1, tm, tn), lambda b, i, j: (b, i, clampi(j, 7, 2))),
                   pl.BlockSpec((1, tm, LANES), lambda b, i, j: (b, i, 0))),
        scratch_shapes=[pltpu.VMEM((tm, D), F32 if hi else BF16)]
        + ([pltpu.VMEM((tm, LANES), F32)] if chunked else []),
        compiler_params=_cp(("parallel", "parallel", "arbitrary")),
        name="norm_in_proj",
    )(x, g, sc, sh, w_in, w_gates, w_ab)
    return outs


def _s5_prep_kernel(lrb, lib, dtb, bre, bim, lrt, lit, dtt, cre, cim, lrn, lin, dtn,
                    be_ref, cpw_ref, pt_ref, a1_ref, *, seg):
    def disc(lr, li, ldt):
        dt = jnp.exp(ldt)
        mag = jnp.exp(lr * dt)
        return mag * jnp.cos(li * dt), mag * jnp.sin(li * dt)

    def cmul(xr, xi, yr, yi):
        return xr * yr - xi * yi, xr * yi + xi * yr

    lr, li = lrb[0], lib[0]
    ar, ai = disc(lr, li, dtb[0])
    den = lr * lr + li * li
    nr = ar - 1.0
    kr = (nr * lr + ai * li) / den
    ki = (ai * lr - nr * li) / den
    br, bi = bre[0], bim[0]
    bbr = kr * br - ki * bi
    bbi = kr * bi + ki * br
    pr, pi = jnp.ones_like(ar), jnp.zeros_like(ar)
    for d in range(S5_T):
        t = S5_T - 1 - d
        vr, vi = cmul(pr, pi, bbr, bbi)
        be_ref[0, t, 0] = vr
        be_ref[0, t, 1] = vi
        pr, pi = cmul(pr, pi, ar, ai)

    ar, ai = disc(lrt[0], lit[0], dtt[0])
    cr, ci = cre[0], cim[0]
    pr, pi = jnp.ones_like(ar), jnp.zeros_like(ar)
    for d in range(S5_T + 1):
        vr, vi = cmul(cr, ci, pr, pi)
        cpw_ref[0, d, 0] = vr
        cpw_ref[0, d, 1] = -vi
        pr, pi = cmul(pr, pi, ar, ai)

    ar, ai = disc(lrn[0], lin[0], dtn[0])
    a1_ref[0, 0] = ar
    a1_ref[0, 1] = ai
    tr, ti = ar, ai
    for _ in range(S5_T - 1):
        tr, ti = cmul(tr, ti, ar, ai)
    pr, pi = jnp.ones_like(ar), jnp.zeros_like(ar)
    for i in range(seg + 1):
        pt_ref[0, i, 0] = pr
        pt_ref[0, i, 1] = pi
        pr, pi = cmul(pr, pi, tr, ti)


def _s5_prep(lam_re, lam_im, log_dt, b_re, b_im, c_re, c_im, seg):
    G, P, C = S5_GROUPS, S5_STATE, S5_GROUP
    flat = P * C
    rep = lambda a: jnp.repeat(a, C, axis=-1)
    til = lambda a: jnp.tile(a, (1, 1, C))
    dt3 = jnp.broadcast_to(log_dt[:, :, None], (DEPTH, G, P))
    nat = lambda a: a.reshape(DEPTH, SLABS, SLAB_STATE)
    args = (rep(lam_re), rep(lam_im), rep(dt3), b_re.reshape(DEPTH, G, flat), b_im.reshape(DEPTH, G, flat),
            til(lam_re), til(lam_im), til(dt3), c_re.reshape(DEPTH, G, flat), c_im.reshape(DEPTH, G, flat),
            nat(lam_re), nat(lam_im), nat(dt3))
    big = pl.BlockSpec((1, G, flat), lambda l: (l, 0, 0))
    small = pl.BlockSpec((1, SLABS, SLAB_STATE), lambda l: (l, 0, 0))
    return pl.pallas_call(
        functools.partial(_s5_prep_kernel, seg=seg),
        out_shape=(jax.ShapeDtypeStruct((DEPTH, S5_T, 2, G, flat), F32),
                   jax.ShapeDtypeStruct((DEPTH, S5_T + 1, 2, G, flat), F32),
                   jax.ShapeDtypeStruct((DEPTH, seg + 1, 2, SLABS, SLAB_STATE), F32),
                   jax.ShapeDtypeStruct((DEPTH, 2, SLABS, SLAB_STATE), F32)),
        grid=(DEPTH,),
        in_specs=[big] * 10 + [small] * 3,
        out_specs=(pl.BlockSpec((1, S5_T, 2, G, flat), lambda l: (l, 0, 0, 0, 0)),
                   pl.BlockSpec((1, S5_T + 1, 2, G, flat), lambda l: (l, 0, 0, 0, 0)),
                   pl.BlockSpec((1, seg + 1, 2, SLABS, SLAB_STATE), lambda l: (l, 0, 0, 0, 0)),
                   pl.BlockSpec((1, 2, SLABS, SLAB_STATE), lambda l: (l, 0, 0, 0))),
        compiler_params=_cp(("parallel",)),
        name="s5_discretize",
    )(*args)


def _embed_be(be):
    T = be.shape[0]
    x = be.reshape(T, 2, SLABS, 8, S5_STATE, S5_GROUP).transpose(2, 0, 3, 5, 1, 4)
    eye = jnp.eye(8, dtype=bool)
    x = jnp.where(eye[None, None, :, None, None, :, None], x[:, :, :, :, :, None, :], 0.0)
    return x.reshape(SLABS, T * LANES, 2 * SLAB_STATE)


def _embed_cp(cpw):
    Dn = cpw.shape[0]
    x = cpw.reshape(Dn, 2, SLABS, 8, S5_GROUP, S5_STATE).transpose(2, 0, 1, 3, 5, 4)
    eye = jnp.eye(8, dtype=bool)
    x = jnp.where(eye[None, None, None, :, None, :, None], x[:, :, :, :, :, None, :], 0.0)
    return x.reshape(SLABS, Dn, 2 * SLAB_STATE, LANES)


def _toep_kernel(b_ref, c_ref, o_ref):
    o_ref[0, 0] = _dot(b_ref[0], c_ref[0, 0], HI)


def _toep(bst, cpe):
    return pl.pallas_call(
        _toep_kernel,
        out_shape=jax.ShapeDtypeStruct((SLABS, S5_T, LANES, LANES), F32),
        grid=(SLABS, S5_T),
        in_specs=[pl.BlockSpec((1, LANES, 2 * SLAB_STATE), lambda k, d: (k, 0, 0)),
                  pl.BlockSpec((1, 1, 2 * SLAB_STATE, LANES), lambda k, d: (k, d, 0, 0))],
        out_specs=pl.BlockSpec((1, 1, LANES, LANES), lambda k, d: (k, d, 0, 0)),
        compiler_params=_cp(("parallel", "parallel")),
        name="s5_conv_blocks",
    )(bst, cpe)


def _pair_tiles(K):
    Z = jnp.zeros_like(K[:, 0])
    kd = lambda d: K[:, d] if d >= 0 else Z
    tiles = []
    for dd in range(S5_T // 2):
        top = jnp.concatenate([kd(2 * dd), kd(2 * dd + 1)], axis=-1)
        bot = jnp.concatenate([kd(2 * dd - 1), kd(2 * dd)], axis=-1)
        tiles.append(jnp.concatenate([top, bot], axis=-2))
    return jnp.stack(tiles, axis=1)


def _s5_seq_kernel(up_ref, be_ref, tp_ref, cpm_ref, pt_ref, s0_ref, dsk_ref,
                   yg_ref, sfin_ref, e_scr, sx_scr, *, nc):
    seg = nc // 8
    W = SLAB_STATE
    nt = W // LANES
    u = up_ref[0, 0]
    ub = u.astype(BF16)
    e = _dot(ub, be_ref[0])
    for c in range(2 * nt):
        e_scr[c] = e[:, c * LANES:(c + 1) * LANES]

    def tiles(row):
        return [(row[:, c * LANES:(c + 1) * LANES], row[:, W + c * LANES:W + (c + 1) * LANES])
                for c in range(nt)]

    a8 = [(jnp.broadcast_to(r, (8, LANES)), jnp.broadcast_to(i, (8, LANES)))
          for r, i in tiles(pt_ref[0, 1:2, :])]

    def step(i, carry):
        rows = pl.ds(i, 8, stride=seg)
        new = []
        for c in range(nt):
            sr, si = carry[c]
            ar, ai = a8[c]
            sx_scr[c, rows, :] = sr
            sx_scr[nt + c, rows, :] = si
            new.append((ar * sr - ai * si + e_scr[c, rows, :],
                        ar * si + ai * sr + e_scr[nt + c, rows, :]))
        return tuple(new)

    zero = jnp.zeros((8, LANES), F32)
    ends = lax.fori_loop(0, seg, step, tuple((zero, zero) for _ in range(nt)))

    al = tiles(pt_ref[0, seg:seg + 1, :])
    cur = tiles(s0_ref[0, 0])
    car = []
    for c in range(nt):
        alr, ali = al[c]
        cr, ci = cur[c]
        sr, si = ends[c]
        crs, cis = [], []
        for j in range(8):
            crs.append(cr)
            cis.append(ci)
            cr, ci = (alr * cr - ali * ci + sr[j:j + 1], alr * ci + ali * cr + si[j:j + 1])
        sfin_ref[0, 0, :, c * LANES:(c + 1) * LANES] = cr
        sfin_ref[0, 0, :, W + c * LANES:W + (c + 1) * LANES] = ci
        car.append((jnp.concatenate(crs, axis=0), jnp.concatenate(cis, axis=0)))

    def corr(i, _):
        rows = pl.ds(i, 8, stride=seg)
        pw = tiles(pt_ref[0, pl.ds(i, 1), :])
        for c in range(nt):
            pr, pi = pw[c]
            cr, ci = car[c]
            sx_scr[c, rows, :] = sx_scr[c, rows, :] + (pr * cr - pi * ci)
            sx_scr[nt + c, rows, :] = sx_scr[nt + c, rows, :] + (pr * ci + pi * cr)
        return 0

    lax.fori_loop(0, seg, corr, 0)

    sx = jnp.concatenate([sx_scr[c] for c in range(2 * nt)], axis=-1)
    y = _dot(sx.astype(BF16), cpm_ref[0])
    TW = 2 * LANES
    for tq in range(S5_T // 2):
        acc = y[:, tq * TW:(tq + 1) * TW]
        for tpi in range(tq + 1):
            acc = acc + _dot(ub[:, tpi * TW:(tpi + 1) * TW], tp_ref[0, tq - tpi])
        acc = acc + dsk_ref[0, :, tq * TW:(tq + 1) * TW] * u[:, tq * TW:(tq + 1) * TW]
        yg_ref[0, 0, :, tq * TW:(tq + 1) * TW] = jax.nn.gelu(acc)


def _s5_seq(up, be_emb, tp, cpm, pt, s0, dsk):
    _, B, nc, _ = up.shape
    seg = nc // 8
    W2 = 2 * SLAB_STATE
    yg, sfin = pl.pallas_call(
        functools.partial(_s5_seq_kernel, nc=nc),
        out_shape=(jax.ShapeDtypeStruct((SLABS, B, nc, S5_T * LANES), F32),
                   jax.ShapeDtypeStruct((SLABS, B, 1, W2), F32)),
        grid=(SLABS, B),
        in_specs=[pl.BlockSpec((1, 1, nc, S5_T * LANES), lambda k, b: (k, b, 0, 0)),
                  pl.BlockSpec((1, S5_T * LANES, W2), lambda k, b: (k, 0, 0)),
                  pl.BlockSpec((1, S5_T // 2, 2 * LANES, 2 * LANES), lambda k, b: (k, 0, 0, 0)),
                  pl.BlockSpec((1, W2, S5_T * LANES), lambda k, b: (k, 0, 0)),
                  pl.BlockSpec((1, seg + 1, W2), lambda k, b: (k, 0, 0)),
                  pl.BlockSpec((1, 1, 1, W2), lambda k, b: (k, b, 0, 0)),
                  pl.BlockSpec((1, 1, S5_T * LANES), lambda k, b: (k, 0, 0))],
        out_specs=(pl.BlockSpec((1, 1, nc, S5_T * LANES), lambda k, b: (k, b, 0, 0)),
                   pl.BlockSpec((1, 1, 1, W2), lambda k, b: (k, b, 0, 0))),
        scratch_shapes=[pltpu.VMEM((W2 // LANES, nc, LANES), F32),
                        pltpu.VMEM((W2 // LANES, nc, LANES), F32)],
        compiler_params=_cp(("parallel", "parallel")),
        name="s5_seq",
    )(up, be_emb, tp, cpm, pt, s0, dsk)
    return yg, sfin


def _s5_step_kernel(u_ref, b_ref, c_ref, a_ref, s0_ref, d_ref, yg_ref, s1_ref):
    W = SLAB_STATE
    u = u_ref[0]
    bu = _dot(u, b_ref[0], HI)
    ar = a_ref[0, :, 0:W]
    ai = a_ref[0, :, W:2 * W]
    sr = s0_ref[0, :, 0:W]
    si = s0_ref[0, :, W:2 * W]
    nr = ar * sr - ai * si + bu[:, 0:W]
    ni = ar * si + ai * sr + bu[:, W:2 * W]
    s1_ref[0, :, 0:W] = nr
    s1_ref[0, :, W:2 * W] = ni
    s1 = jnp.concatenate([nr, ni], axis=-1)
    y = _dot(s1, c_ref[0], HI) + d_ref[0] * u
    yg_ref[0] = jax.nn.gelu(y)


def _s5_step(u_slab, bst, c0, a1, s0, d1):
    _, N, _ = u_slab.shape
    W2 = 2 * SLAB_STATE
    return pl.pallas_call(
        _s5_step_kernel,
        out_shape=(jax.ShapeDtypeStruct((SLABS, N, LANES), F32),
                   jax.ShapeDtypeStruct((SLABS, N, W2), F32)),
        grid=(SLABS,),
        in_specs=[pl.BlockSpec((1, N, LANES), lambda k: (k, 0, 0)),
                  pl.BlockSpec((1, LANES, W2), lambda k: (k, 0, 0)),
                  pl.BlockSpec((1, W2, LANES), lambda k: (k, 0, 0)),
                  pl.BlockSpec((1, 1, W2), lambda k: (k, 0, 0)),
                  pl.BlockSpec((1, N, W2), lambda k: (k, 0, 0)),
                  pl.BlockSpec((1, 1, LANES), lambda k: (k, 0, 0))],
        out_specs=(pl.BlockSpec((1, N, LANES), lambda k: (k, 0, 0)),
                   pl.BlockSpec((1, N, W2), lambda k: (k, 0, 0))),
        compiler_params=_cp(("parallel",)),
        name="s5_step",
    )(u_slab, bst, c0, a1, s0, d1)


def _l2n(x):
    return x * lax.rsqrt(jnp.sum(x * x, axis=-1, keepdims=True) + L2_EPS)


def _split_bf16(x):
    hi = x.astype(BF16)
    return hi, (x - hi.astype(F32)).astype(BF16)


def _unit_lower_solve(As, rhss):
    n = GDN_C
    row = lax.broadcasted_iota(jnp.int32, (n, n), 0)
    col = lax.broadcasted_iota(jnp.int32, (n, n), 1)
    eye = (row == col).astype(F32)
    same8 = (row // 8) == (col // 8)
    Qs = [jnp.where(same8, -A, 0.0) for A in As]
    invs = [eye + Q for Q in Qs]
    for _ in range(2):
        Qs = [_dotb(Q, Q) for Q in Qs]
        invs = [inv + _dotb(inv, Q) for inv, Q in zip(invs, Qs)]
    s = 8
    while s < n:
        sib = ((row // (2 * s)) == (col // (2 * s))) & ((row // s) != (col // s))
        offs = [jnp.where(sib, A, 0.0).astype(BF16) for A in As]
        invb = [inv.astype(BF16) for inv in invs]
        tmp = [_dot(off, ib) for off, ib in zip(offs, invb)]
        invs = [inv - _dot(ib, t.astype(BF16)) for inv, ib, t in zip(invs, invb, tmp)]
        s *= 2
    invb = [inv.astype(BF16) for inv in invs]
    x0s = [_dot(ib, rhs.astype(BF16)) for ib, rhs in zip(invb, rhss)]
    res = []
    for A, x0, rhs in zip(As, x0s, rhss):
        ah, al = _split_bf16(A)
        xh, xl = _split_bf16(x0)
        res.append(rhs - x0 - (_dot(ah, xh) + _dot(ah, xl) + _dot(al, xh)))
    return [x0 + _dot(ib, r.astype(BF16)) for x0, ib, r in zip(x0s, invb, res)]


def _gdn_tile(qc_scr, gc, beta, z_ref, nw, o_ref, s_scr, tl):
    C, DK, H = GDN_C, GDN_DK, GDN_HEADS
    nchunk = tl // C
    probs = [(c, h) for c in range(nchunk) for h in range(H)]
    row = lax.broadcasted_iota(jnp.int32, (C, C), 0)
    col = lax.broadcasted_iota(jnp.int32, (C, C), 1)
    tri = row >= col
    strict = row > col

    def blk(c, off):
        return qc_scr[c * C:(c + 1) * C, off:off + DK]

    q = [_l2n(blk(c, h * DK)) * (DK ** -0.5) for c, h in probs]
    k = [_l2n(blk(c, GDN_WIDTH + h * DK)) for c, h in probs]
    v = [blk(c, 2 * GDN_WIDTH + h * DK) for c, h in probs]
    gcb = [jnp.broadcast_to(gc[c * C:(c + 1) * C, h:h + 1], (C, DK)) for c, h in probs]
    bb = [jnp.broadcast_to(beta[c * C:(c + 1) * C, H + h:H + h + 1], (C, DK)) for c, h in probs]
    decay = []
    for g in gcb:
        diff = g - g.T
        decay.append(jnp.where(tri, jnp.exp(jnp.where(tri, diff, 0.0)), 0.0))
    kbf = [x.astype(BF16) for x in k]
    kb = [x * b for x, b in zip(k, bb)]
    A = [jnp.where(strict, _dot_nt(x.astype(BF16), y) * d, 0.0) for x, y, d in zip(kb, kbf, decay)]
    egc = [jnp.exp(g) for g in gcb]
    rhs = [jnp.concatenate([x * b, y * e], axis=-1) for x, b, y, e in zip(v, bb, kb, egc)]
    sol = _unit_lower_solve(A, rhs)
    attn = [jnp.where(tri, _dot_nt(x.astype(BF16), y) * d, 0.0).astype(BF16)
            for x, y, d in zip(q, kbf, decay)]
    glast = [g[C - 1:C, :] for g in gcb]
    wq = [jnp.concatenate([s[:, DK:], x * e], axis=0).astype(BF16) for s, x, e in zip(sol, q, egc)]
    kg = [(x * jnp.exp(gl - g)).astype(BF16) for x, gl, g in zip(k, glast, gcb)]

    for c in range(nchunk):
        ps = [c * H + h for h in range(H)]
        S = [s_scr[h] for h in range(H)]
        ws = [_dot(wq[p], S[h].astype(BF16)) for h, p in enumerate(ps)]
        v_new = [sol[p][:, 0:DK] - w[0:C] for p, w in zip(ps, ws)]
        vb = [x.astype(BF16) for x in v_new]
        o = [w[C:] + _dot(attn[p], x) for p, w, x in zip(ps, ws, vb)]
        for h, p in enumerate(ps):
            s_scr[h] = S[h] * jnp.exp(glast[p]) + _dot_tn(kg[p], vb[h])
            zh = z_ref[0, c * C:(c + 1) * C, h * DK:(h + 1) * DK]
            on = o[h] * lax.rsqrt(jnp.mean(o[h] * o[h], axis=-1, keepdims=True) + NORM_EPS) * nw
            o_ref[0, c * C:(c + 1) * C, h * DK:(h + 1) * DK] = on * _silu(zh)


def _gdn_seq_kernel(qkv_ref, z_ref, ab_ref, cw_ref, alog_ref, dtb_ref, nw_ref, conv0_ref, s0_ref,
                    o_ref, sfin_ref, xp_scr, qc_scr, s_scr, *, tl):
    lt = pl.program_id(1)

    @pl.when(lt == 0)
    def _():
        xp_scr[0:8, :] = jnp.zeros((8, QKV_WIDTH), F32)
        xp_scr[8 - (GDN_CONV - 1):8, :] = conv0_ref[0]
        s_scr[...] = s0_ref[0]

    xp_scr[8:8 + tl, :] = qkv_ref[0]
    conv = cw_ref[0:1, :] * xp_scr[5:5 + tl, :]
    for j in range(1, GDN_CONV):
        conv = conv + cw_ref[j:j + 1, :] * xp_scr[5 + j:5 + j + tl, :]
    xp_scr[0:8, :] = xp_scr[tl:tl + 8, :]
    qc_scr[...] = _silu(conv)

    ab = ab_ref[0]
    g = -jnp.exp(alog_ref[...]) * jax.nn.softplus(ab + dtb_ref[...])
    beta = jax.nn.sigmoid(ab)
    row = lax.broadcasted_iota(jnp.int32, (tl, tl), 0)
    col = lax.broadcasted_iota(jnp.int32, (tl, tl), 1)
    csum = ((row >= col) & ((row // GDN_C) == (col // GDN_C))).astype(F32)
    gc = _dot(csum, g, HI)
    _gdn_tile(qc_scr, gc, beta, z_ref, nw_ref[...], o_ref, s_scr, tl)

    @pl.when(lt == pl.num_programs(1) - 1)
    def _():
        sfin_ref[0] = s_scr[...]


def _gdn_seq(qkv, z, ab, conv_w, alog, dtb, nw, conv0, s0):
    B, L, _ = qkv.shape
    tl = min(256, L)
    return pl.pallas_call(
        functools.partial(_gdn_seq_kernel, tl=tl),
        out_shape=(jax.ShapeDtypeStruct((B, L, GDN_WIDTH), F32),
                   jax.ShapeDtypeStruct((B, GDN_HEADS, GDN_DK, GDN_DK), F32)),
        grid=(B, L // tl),
        in_specs=[pl.BlockSpec((1, tl, QKV_WIDTH), lambda b, i: (b, i, 0)),
                  pl.BlockSpec((1, tl, GDN_WIDTH), lambda b, i: (b, i, 0)),
                  pl.BlockSpec((1, tl, LANES), lambda b, i: (b, i, 0)),
                  pl.BlockSpec((GDN_CONV, QKV_WIDTH), lambda b, i: (0, 0)),
                  pl.BlockSpec((1, LANES), lambda b, i: (0, 0)),
                  pl.BlockSpec((1, LANES), lambda b, i: (0, 0)),
                  pl.BlockSpec((1, GDN_DK), lambda b, i: (0, 0)),
                  pl.BlockSpec((1, GDN_CONV - 1, QKV_WIDTH), lambda b, i: (b, 0, 0)),
                  pl.BlockSpec((1, GDN_HEADS, GDN_DK, GDN_DK), lambda b, i: (b, 0, 0, 0))],
        out_specs=(pl.BlockSpec((1, tl, GDN_WIDTH), lambda b, i: (b, i, 0)),
                   pl.BlockSpec((1, GDN_HEADS, GDN_DK, GDN_DK), lambda b, i: (b, 0, 0, 0))),
        scratch_shapes=[pltpu.VMEM((tl + 8, QKV_WIDTH), F32),
                        pltpu.VMEM((tl, QKV_WIDTH), F32),
                        pltpu.VMEM((GDN_HEADS, GDN_DK, GDN_DK), F32)],
        compiler_params=_cp(("parallel", "arbitrary")),
        name="gdn_seq",
    )(qkv, z, ab, conv_w, alog, dtb, nw, conv0, s0)


GDN_STEP_ROWS = 8


def _gdn_step_kernel(qkv_ref, z_ref, ab_ref, cw_ref, alog_ref, dtb_ref, nw_ref, conv0_ref, s0_ref,
                     o_ref, s1_ref):
    nb = GDN_STEP_ROWS
    W = QKV_WIDTH
    conv = cw_ref[0:1, :] * conv0_ref[:, 0:W]
    conv = conv + cw_ref[1:2, :] * conv0_ref[:, W:2 * W]
    conv = conv + cw_ref[2:3, :] * conv0_ref[:, 2 * W:3 * W]
    conv = conv + cw_ref[3:4, :] * qkv_ref[...]
    qc = _silu(conv)
    ab = ab_ref[...]
    eg = jnp.exp(-jnp.exp(alog_ref[...]) * jax.nn.softplus(ab + dtb_ref[...]))
    beta = jax.nn.sigmoid(ab)
    eye = (lax.broadcasted_iota(jnp.int32, (GDN_DK, GDN_DK), 0)
           == lax.broadcasted_iota(jnp.int32, (GDN_DK, GDN_DK), 1)).astype(F32)
    for h in range(GDN_HEADS):
        q = _l2n(qc[:, h * GDN_DK:(h + 1) * GDN_DK]) * (GDN_DK ** -0.5)
        k = _l2n(qc[:, GDN_WIDTH + h * GDN_DK:GDN_WIDTH + (h + 1) * GDN_DK])
        v = qc[:, 2 * GDN_WIDTH + h * GDN_DK:2 * GDN_WIDTH + (h + 1) * GDN_DK]
        kT = _dot_nt(eye, k, HI)
        qT = _dot_nt(eye, q, HI)
        qk = jnp.sum(q * k, axis=-1, keepdims=True)
        for j in range(nb):
            S = s0_ref[j, h]
            kc = jnp.broadcast_to(kT[:, j:j + 1], (GDN_DK, GDN_DK))
            qcb = jnp.broadcast_to(qT[:, j:j + 1], (GDN_DK, GDN_DK))
            kS = jnp.sum(kc * S, axis=0, keepdims=True)
            qS = jnp.sum(qcb * S, axis=0, keepdims=True)
            egj = eg[j:j + 1, h:h + 1]
            bj = beta[j:j + 1, GDN_HEADS + h:GDN_HEADS + h + 1]
            v_new = bj * v[j:j + 1, :] - (bj * egj) * kS
            o = egj * qS + qk[j:j + 1, :] * v_new
            s1_ref[j, h] = S * egj + kc * v_new
            zh = z_ref[j:j + 1, h * GDN_DK:(h + 1) * GDN_DK]
            on = o * lax.rsqrt(jnp.mean(o * o, axis=-1, keepdims=True) + NORM_EPS) * nw_ref[...]
            o_ref[j:j + 1, h * GDN_DK:(h + 1) * GDN_DK] = on * _silu(zh)


def _gdn_step(qkv, z, ab, conv_w, alog, dtb, nw, conv0, s0):
    N = qkv.shape[0]
    nb = GDN_STEP_ROWS
    row = lambda w: pl.BlockSpec((nb, w), lambda i: (i, 0))
    const = lambda r, w: pl.BlockSpec((r, w), lambda i: (0, 0))
    return pl.pallas_call(
        _gdn_step_kernel,
        out_shape=(jax.ShapeDtypeStruct((N, GDN_WIDTH), F32),
                   jax.ShapeDtypeStruct((N, GDN_HEADS, GDN_DK, GDN_DK), F32)),
        grid=(N // nb,),
        in_specs=[row(QKV_WIDTH), row(GDN_WIDTH), row(LANES), const(GDN_CONV, QKV_WIDTH),
                  const(1, LANES), const(1, LANES), const(1, GDN_DK), row(3 * QKV_WIDTH),
                  pl.BlockSpec((nb, GDN_HEADS, GDN_DK, GDN_DK), lambda i: (i, 0, 0, 0))],
        out_specs=(row(GDN_WIDTH),
                   pl.BlockSpec((nb, GDN_HEADS, GDN_DK, GDN_DK), lambda i: (i, 0, 0, 0))),
        compiler_params=_cp(("parallel",)),
        name="gdn_step",
    )(qkv, z, ab, conv_w, alog, dtb, nw, conv0, s0)


def _merge_kernel(yg_ref, og_ref, ga_ref, gb_ref, x_ref, gt_ref, wglu_ref, wgo_ref, wout_ref,
                  gf_ref, scf_ref, shf_ref, wr_ref,
                  xo_ref, h_ref, lg_ref, *scr, hi, chunked):
    if chunked:
        y_scr = scr[-1]
        scr = scr[:-1]
        nrow = y_scr.shape[1] // S5_T
        for k in range(SLABS):
            for t in range(S5_T):
                y_scr[k, pl.ds(t, nrow, stride=S5_T), :] = yg_ref[k, 0, :, t * LANES:(t + 1) * LANES]
        y = jnp.concatenate([y_scr[k] for k in range(SLABS)], axis=-1)
    else:
        y = jnp.concatenate([yg_ref[k, 0] for k in range(SLABS)], axis=-1)
    if hi:
        wglu, wgo, wout = wglu_ref[...], wgo_ref[...], wout_ref[...]
        mm = lambda a, w: _dot(a, w, HI)
    else:
        wglu_s, wgo_s, wout_s = scr

        @pl.when((pl.program_id(0) == 0) & (pl.program_id(1) == 0))
        def _():
            wglu_s[...] = wglu_ref[...].astype(BF16)
            wgo_s[...] = wgo_ref[...].astype(BF16)
            wout_s[...] = wout_ref[...].astype(BF16)

        wglu, wgo, wout = wglu_s[...], wgo_s[...], wout_s[...]
        mm = lambda a, w: _dot(a.astype(BF16), w)

    glu = mm(y, wglu)
    branch_a = glu[:, 0:D_MODEL] * jax.nn.sigmoid(glu[:, D_MODEL:])
    branch_b = mm(og_ref[0], wgo)
    merged = ga_ref[0] * branch_a + gb_ref[0] * branch_b
    out = mm(merged, wout)
    x = x_ref[0] + gt_ref[0] * out
    xo_ref[0] = x
    ms = jnp.mean(x * x, axis=-1, keepdims=True)
    h = x * lax.rsqrt(ms + NORM_EPS) * gf_ref[...]
    h = h * (1.0 + scf_ref[0]) + shf_ref[0]
    h_ref[0] = h
    if chunked:
        lg_ref[0] = _dot_nt(wr_ref[...], h, HI)
    else:
        lg_ref[0] = _dot(h, wr_ref[...], HI)


def _merge(yg, og, ga, gb, x, gt, wglu, wgo, wout, gf, scf, shf, wr, *, tm, hi, chunked):
    B, L, D = x.shape
    lm = gt.shape[1]
    tmm = 1 if lm == 1 else tm
    mod_map = (lambda b, i: (b, 0, 0)) if lm == 1 else (lambda b, i: (b, i, 0))
    row = lambda w: pl.BlockSpec((1, tm, w), lambda b, i: (b, i, 0))
    const = lambda r, w: pl.BlockSpec((r, w), lambda b, i: (0, 0))
    mod = pl.BlockSpec((1, tmm, D), mod_map)
    scratch = [] if hi else [pltpu.VMEM((S5_WIDTH, 2 * D), BF16), pltpu.VMEM((GDN_WIDTH, D), BF16),
                             pltpu.VMEM((D, D), BF16)]
    if chunked:
        scratch = scratch + [pltpu.VMEM((SLABS, tm, LANES), F32)]
        yg_spec = pl.BlockSpec((SLABS, 1, tm // S5_T, S5_T * LANES), lambda b, i: (0, b, i, 0))
        lg_shape = jax.ShapeDtypeStruct((B, N_EXPERTS, L), F32)
        lg_spec = pl.BlockSpec((1, N_EXPERTS, tm), lambda b, i: (b, 0, i))
        wr = wr.T
    else:
        yg_spec = pl.BlockSpec((SLABS, 1, tm, LANES), lambda b, i: (0, b, i, 0))
        lg_shape = jax.ShapeDtypeStruct((B, L, N_EXPERTS), F32)
        lg_spec = row(N_EXPERTS)
    return pl.pallas_call(
        functools.partial(_merge_kernel, hi=hi, chunked=chunked),
        out_shape=(jax.ShapeDtypeStruct((B, L, D), F32),
                   jax.ShapeDtypeStruct((B, L, D), F32),
                   lg_shape),
        grid=(B, L // tm),
        in_specs=[yg_spec,
                  row(GDN_WIDTH), row(D), row(D), row(D), mod,
                  const(S5_WIDTH, 2 * D), const(GDN_WIDTH, D), const(D, D),
                  const(1, D), mod, mod, const(*wr.shape)],
        out_specs=(row(D), row(D), lg_spec),
        scratch_shapes=scratch,
        compiler_params=_cp(("arbitrary", "arbitrary")),
        name="merge_out_proj",
    )(yg, og, ga, gb, x, gt, wglu, wgo, wout, gf, scf, shf, wr)


FF_TILE = 512


def _finish(x, gfin_ref, final):
    if not final:
        return x
    ms = jnp.mean(x * x, axis=-1, keepdims=True)
    return x * lax.rsqrt(ms + NORM_EPS) * gfin_ref[...]


def _ffn_kernel(h_ref, x_ref, gt_ref, wg_ref, wu_ref, wd_ref, gfin_ref, o_ref, acc_scr, *, hi, final):
    j = pl.program_id(2)
    if hi:
        h = h_ref[0]
        mm = lambda a, w: _dot(a, w, HI)
    else:
        h = h_ref[0].astype(BF16)
        mm = lambda a, w: _dot(a.astype(BF16), w.astype(BF16))
    act = _silu(mm(h, wg_ref[...])) * mm(h, wu_ref[...])
    part = mm(act, wd_ref[...])

    @pl.when(j == 0)
    def _():
        acc_scr[...] = part

    @pl.when(j > 0)
    def _():
        acc_scr[...] = acc_scr[...] + part

    @pl.when(j == pl.num_programs(2) - 1)
    def _():
        o_ref[0] = _finish(x_ref[0] + gt_ref[0] * acc_scr[...], gfin_ref, final)


def _ffn(h, x, gt, w_gu, w_down, gfin, *, tm, hi, final):
    B, L, D = x.shape
    lm = gt.shape[1]
    tmm = 1 if lm == 1 else tm
    mod_map = (lambda b, i, j: (b, 0, 0)) if lm == 1 else (lambda b, i, j: (b, i, 0))
    nj = D_FF // FF_TILE
    row = pl.BlockSpec((1, tm, D), lambda b, i, j: (b, i, 0))
    return pl.pallas_call(
        functools.partial(_ffn_kernel, hi=hi, final=final),
        out_shape=jax.ShapeDtypeStruct((B, L, D), F32),
        grid=(B, L // tm, nj),
        in_specs=[row, row, pl.BlockSpec((1, tmm, D), mod_map),
                  pl.BlockSpec((D, FF_TILE), lambda b, i, j: (0, j)),
                  pl.BlockSpec((D, FF_TILE), lambda b, i, j: (0, nj + j)),
                  pl.BlockSpec((FF_TILE, D), lambda b, i, j: (j, 0)),
                  pl.BlockSpec((1, D), lambda b, i, j: (0, 0))],
        out_specs=row,
        scratch_shapes=[pltpu.VMEM((tm, D), F32)],
        compiler_params=_cp(("parallel", "parallel", "arbitrary")),
        name="ffn_dense",
    )(h, x, gt, w_gu, w_gu, w_down, gfin)


def _route(lg):
    idx = lax.broadcasted_iota(jnp.int32, lg.shape, 1)
    m1 = jnp.max(lg, axis=-1, keepdims=True)
    i1 = jnp.min(jnp.where(lg == m1, idx, N_EXPERTS), axis=-1, keepdims=True)
    lg2 = jnp.where(idx == i1, -jnp.inf, lg)
    m2 = jnp.max(lg2, axis=-1, keepdims=True)
    i2 = jnp.min(jnp.where(lg2 == m2, idx, N_EXPERTS), axis=-1, keepdims=True)
    e2 = jnp.exp(m2 - m1)
    w1 = 1.0 / (1.0 + e2)
    w2 = e2 / (1.0 + e2)
    return jnp.where(idx == i1, w1, 0.0) + jnp.where(idx == i2, w2, 0.0)


def _moe_kernel(h_ref, x_ref, gt_ref, lg_ref, br_ref, wg_ref, wu_ref, wd_ref, gfin_ref, o_ref,
                acc_scr, comb_scr, *, final):
    e = pl.program_id(2)
    j = pl.program_id(3)

    @pl.when((e == 0) & (j == 0))
    def _():
        comb_scr[...] = _route(lg_ref[0] + br_ref[...])
        acc_scr[...] = jnp.zeros_like(acc_scr)

    idx = lax.broadcasted_iota(jnp.int32, comb_scr.shape, 1)
    ce = jnp.sum(jnp.where(idx == e, comb_scr[...], 0.0), axis=-1, keepdims=True)
    h = h_ref[0].astype(BF16)
    act = _silu(_dot(h, wg_ref[0].astype(BF16))) * _dot(h, wu_ref[0].astype(BF16))
    acc_scr[...] = acc_scr[...] + ce * _dot(act.astype(BF16), wd_ref[0].astype(BF16))

    @pl.when((e == pl.num_programs(2) - 1) & (j == pl.num_programs(3) - 1))
    def _():
        o_ref[0] = _finish(x_ref[0] + gt_ref[0] * acc_scr[...], gfin_ref, final)


def _moe(h, x, gt, lg, b_r, w_gu, w_down, gfin, *, tm, final):
    B, L, D = x.shape
    lm = gt.shape[1]
    tmm = 1 if lm == 1 else tm
    mod_map = (lambda b, i, e, j: (b, 0, 0)) if lm == 1 else (lambda b, i, e, j: (b, i, 0))
    nj = D_FF // FF_TILE
    row = pl.BlockSpec((1, tm, D), lambda b, i, e, j: (b, i, 0))
    return pl.pallas_call(
        functools.partial(_moe_kernel, final=final),
        out_shape=jax.ShapeDtypeStruct((B, L, D), F32),
        grid=(B, L // tm, N_EXPERTS, nj),
        in_specs=[row, row, pl.BlockSpec((1, tmm, D), mod_map),
                  pl.BlockSpec((1, tm, N_EXPERTS), lambda b, i, e, j: (b, i, 0)),
                  pl.BlockSpec((1, N_EXPERTS), lambda b, i, e, j: (0, 0)),
                  pl.BlockSpec((1, D, FF_TILE), lambda b, i, e, j: (e, 0, j)),
                  pl.BlockSpec((1, D, FF_TILE), lambda b, i, e, j: (e, 0, nj + j)),
                  pl.BlockSpec((1, FF_TILE, D), lambda b, i, e, j: (e, j, 0)),
                  pl.BlockSpec((1, D), lambda b, i, e, j: (0, 0))],
        out_specs=row,
        scratch_shapes=[pltpu.VMEM((tm, D), F32), pltpu.VMEM((tm, N_EXPERTS), F32)],
        compiler_params=_cp(("parallel", "parallel", "arbitrary", "arbitrary")),
        name="moe_dense",
    )(h, x, gt, lg, b_r, w_gu, w_gu, w_down, gfin)


ROUTE_TM = 512
ROW_DMA_TM = 256
MOE_SUP = 2048
MOE_SUB = 512


def _route_kernel(lg_ref, br_ref, slot_ref, wt_ref, cnt_ref, carry_scr, *, cap):
    @pl.when((pl.program_id(0) == 0) & (pl.program_id(1) == 0))
    def _():
        carry_scr[...] = jnp.zeros_like(carry_scr)

    lg = lg_ref[0] + br_ref[...]
    tm = lg.shape[1]
    eidx = lax.broadcasted_iota(jnp.int32, lg.shape, 0)
    m1 = jnp.max(lg, axis=0, keepdims=True)
    i1 = jnp.min(jnp.where(lg == m1, eidx, N_EXPERTS), axis=0, keepdims=True)
    lg2 = jnp.where(eidx == i1, -jnp.inf, lg)
    m2 = jnp.max(lg2, axis=0, keepdims=True)
    i2 = jnp.min(jnp.where(lg2 == m2, eidx, N_EXPERTS), axis=0, keepdims=True)
    e2 = jnp.exp(m2 - m1)
    wt_ref[0, 0:1, :] = 1.0 / (1.0 + e2)
    wt_ref[0, 1:2, :] = e2 / (1.0 + e2)
    sel1 = eidx == i1
    sel2 = eidx == i2
    oh = jnp.where(sel1 | sel2, 1.0, 0.0)
    before = (lax.broadcasted_iota(jnp.int32, (tm, tm), 0)
              < lax.broadcasted_iota(jnp.int32, (tm, tm), 1)).astype(BF16)
    rank = carry_scr[:, 0:1] + _dot(oh.astype(BF16), before)
    r1 = jnp.sum(jnp.where(sel1, rank, 0.0), axis=0, keepdims=True).astype(jnp.int32)
    r2 = jnp.sum(jnp.where(sel2, rank, 0.0), axis=0, keepdims=True).astype(jnp.int32)
    slot_ref[0, 0:1, :] = i1 * cap + r1
    slot_ref[0, 1:2, :] = i2 * cap + r2
    carry_scr[...] = carry_scr[...] + jnp.sum(oh, axis=1, keepdims=True)
    cnt_ref[...] = carry_scr[...]


def _route_slots(lgT, b_r, cap):
    B, E, L = lgT.shape
    tm = min(ROUTE_TM, L)
    return pl.pallas_call(
        functools.partial(_route_kernel, cap=cap),
        out_shape=(jax.ShapeDtypeStruct((B, 2, L), jnp.int32),
                   jax.ShapeDtypeStruct((B, 2, L), F32),
                   jax.ShapeDtypeStruct((E, LANES), F32)),
        grid=(B, L // tm),
        in_specs=[pl.BlockSpec((1, E, tm), lambda b, i: (b, 0, i)),
                  pl.BlockSpec((E, 1), lambda b, i: (0, 0))],
        out_specs=(pl.BlockSpec((1, 2, tm), lambda b, i: (b, 0, i)),
                   pl.BlockSpec((1, 2, tm), lambda b, i: (b, 0, i)),
                   pl.BlockSpec((E, LANES), lambda b, i: (0, 0))),
        scratch_shapes=[pltpu.VMEM((E, LANES), F32)],
        compiler_params=_cp(("arbitrary", "arbitrary")),
        name="moe_route",
    )(lgT, b_r.reshape(E, 1))


def _row_copy(src, dst, sem):
    return pltpu.make_async_copy(src, dst, sem)


def _slot_row(code, start_ref, cap):
    shift = cap.bit_length() - 1
    return start_ref[lax.shift_right_logical(code, shift)] + (code & (cap - 1))


def _zeros_kernel(o_ref):
    o_ref[...] = jnp.zeros_like(o_ref)


def _zero_rows(n_rows, width):
    return pl.pallas_call(
        _zeros_kernel,
        out_shape=jax.ShapeDtypeStruct((n_rows, width), F32),
        grid=(n_rows // MOE_SUP,),
        out_specs=pl.BlockSpec((MOE_SUP, width), lambda i: (i, 0)),
        compiler_params=_cp(("parallel",)),
        name="moe_zero_rows",
    )()


def _dispatch_kernel(start_ref, code_ref, h_ref, xs_in_ref, xs_ref, sem, *, cap):
    del xs_in_ref
    tm = h_ref.shape[1]

    def issue(r, _):
        for k in range(2):
            row = _slot_row(code_ref[0, k, r], start_ref, cap)
            _row_copy(h_ref.at[0, pl.ds(r, 1), :], xs_ref.at[pl.ds(row, 1), :], sem).start()
        return 0

    lax.fori_loop(0, tm, issue, 0, unroll=8)
    for k in range(2):
        _row_copy(h_ref.at[0], xs_ref.at[pl.ds(0, tm), :], sem).wait()


def _dispatch(start, codes, h, n_rows, cap):
    B, L, D = h.shape
    tm = min(ROW_DMA_TM, L)
    return pl.pallas_call(
        functools.partial(_dispatch_kernel, cap=cap),
        out_shape=jax.ShapeDtypeStruct((n_rows, D), F32),
        grid_spec=pltpu.PrefetchScalarGridSpec(
            num_scalar_prefetch=1,
            grid=(B, L // tm),
            in_specs=[pl.BlockSpec((1, 2, tm), lambda b, i, st: (b, 0, i), memory_space=pltpu.SMEM),
                      pl.BlockSpec((1, tm, D), lambda b, i, st: (b, i, 0)),
                      pl.BlockSpec(memory_space=pl.ANY)],
            out_specs=pl.BlockSpec(memory_space=pl.ANY),
            scratch_shapes=[pltpu.SemaphoreType.DMA(())],
        ),
        input_output_aliases={3: 0},
        compiler_params=_cp(("arbitrary", "arbitrary")),
        name="moe_dispatch",
    )(start, codes, h, _zero_rows(n_rows, D))


def _moe_grp_kernel(ge_ref, gn_ref, x_ref, wg_ref, wu_ref, wd_ref, y_ref):
    g = pl.program_id(0)
    j = pl.program_id(1)
    nsub = gn_ref[g]
    wg = wg_ref[0].astype(BF16)
    wu = wu_ref[0].astype(BF16)
    wd = wd_ref[0].astype(BF16)
    for s in range(MOE_SUP // MOE_SUB):
        rows = slice(s * MOE_SUB, (s + 1) * MOE_SUB)

        @pl.when(s < nsub)
        def _():
            xb = x_ref[rows, :].astype(BF16)
            act = _silu(_dot(xb, wg)) * _dot(xb, wu)
            part = _dot(act.astype(BF16), wd)

            @pl.when(j == 0)
            def _():
                y_ref[rows, :] = part

            @pl.when(j > 0)
            def _():
                y_ref[rows, :] = y_ref[rows, :] + part

        @pl.when((s >= nsub) & (j == 0))
        def _():
            y_ref[rows, :] = jnp.zeros((MOE_SUB, y_ref.shape[1]), F32)


def _moe_groups(counts, n_groups):
    nsup = (counts + MOE_SUP - 1) // MOE_SUP
    ends = jnp.cumsum(nsup)
    first = ends - nsup
    total = ends[-1]
    g = jnp.arange(n_groups, dtype=jnp.int32)
    gc = jnp.minimum(g, total - 1)
    e_of = jnp.minimum(jnp.searchsorted(ends, gc, side='right'), N_EXPERTS - 1).astype(jnp.int32)
    left = counts[e_of] - (gc - first[e_of]) * MOE_SUP
    nsub = jnp.clip((left + MOE_SUB - 1) // MOE_SUB, 0, MOE_SUP // MOE_SUB)
    gn = jnp.where(g < total, nsub, 0).astype(jnp.int32)
    return e_of, gn, (first * MOE_SUP).astype(jnp.int32)


def _moe_grouped(xs, ge, gn, w_gu, w_down):
    D = xs.shape[1]
    nj = D_FF // FF_TILE
    ng = xs.shape[0] // MOE_SUP
    jj = lambda j, gn, g: jnp.where(gn[g] > 0, j, nj - 1)
    return pl.pallas_call(
        _moe_grp_kernel,
        out_shape=jax.ShapeDtypeStruct(xs.shape, F32),
        grid_spec=pltpu.PrefetchScalarGridSpec(
            num_scalar_prefetch=2,
            grid=(ng, nj),
            in_specs=[pl.BlockSpec((MOE_SUP, D), lambda g, j, ge, gn: (g, 0)),
                      pl.BlockSpec((1, D, FF_TILE), lambda g, j, ge, gn: (ge[g], 0, jj(j, gn, g))),
                      pl.BlockSpec((1, D, FF_TILE), lambda g, j, ge, gn: (ge[g], 0, nj + jj(j, gn, g))),
                      pl.BlockSpec((1, FF_TILE, D), lambda g, j, ge, gn: (ge[g], jj(j, gn, g), 0))],
            out_specs=pl.BlockSpec((MOE_SUP, D), lambda g, j, ge, gn: (g, 0)),
        ),
        compiler_params=_cp(("arbitrary", "arbitrary")),
        name="moe_experts",
    )(ge, gn, xs, w_gu, w_gu, w_down)


def _combine_kernel(start_ref, code_ref, w_ref, x_ref, gt_ref, gfin_ref, ys_ref, o_ref, g_scr, sem,
                    *, cap, final):
    tm = x_ref.shape[1]

    def issue(r, _):
        for k in range(2):
            row = _slot_row(code_ref[0, k, r], start_ref, cap)
            _row_copy(ys_ref.at[pl.ds(row, 1), :], g_scr.at[k, pl.ds(r, 1), :], sem).start()
        return 0

    lax.fori_loop(0, tm, issue, 0, unroll=8)
    for k in range(2):
        _row_copy(ys_ref.at[pl.ds(0, tm), :], g_scr.at[k], sem).wait()
    w = w_ref[0]
    f = w[:, 0:1] * g_scr[0] + w[:, 1:2] * g_scr[1]
    o_ref[0] = _finish(x_ref[0] + gt_ref[0] * f, gfin_ref, final)


def _combine(start, codes, wts, x, gt, gfin, ys, *, cap, final):
    B, L, D = x.shape
    tm = min(ROW_DMA_TM, L)
    row = pl.BlockSpec((1, tm, D), lambda b, i, st: (b, i, 0))
    return pl.pallas_call(
        functools.partial(_combine_kernel, cap=cap, final=final),
        out_shape=jax.ShapeDtypeStruct((B, L, D), F32),
        grid_spec=pltpu.PrefetchScalarGridSpec(
            num_scalar_prefetch=1,
            grid=(B, L // tm),
            in_specs=[pl.BlockSpec((1, 2, tm), lambda b, i, st: (b, 0, i), memory_space=pltpu.SMEM),
                      pl.BlockSpec((1, tm, 2), lambda b, i, st: (b, i, 0)),
                      row, pl.BlockSpec((1, 1, D), lambda b, i, st: (b, 0, 0)),
                      pl.BlockSpec((1, D), lambda b, i, st: (0, 0)),
                      pl.BlockSpec(memory_space=pl.ANY)],
            out_specs=row,
            scratch_shapes=[pltpu.VMEM((2, tm, D), F32), pltpu.SemaphoreType.DMA(())],
        ),
        compiler_params=_cp(("arbitrary", "arbitrary")),
        name="moe_combine",
    )(start, codes, wts.transpose(0, 2, 1), x, gt, gfin, ys)


def _moe_seq(h, x, gt, lgT, b_r, w_gu, w_down, gfin, *, final):
    B, L, D = x.shape
    n_tok = B * L
    cap = 1 << (n_tok - 1).bit_length()
    n_groups = 2 * n_tok // MOE_SUP + N_EXPERTS
    codes, wts, cnt = _route_slots(lgT, b_r, cap)
    ge, gn, start = _moe_groups(cnt[:, 0].astype(jnp.int32), n_groups)
    xs = _dispatch(start, codes, h, n_groups * MOE_SUP, cap)
    ys = _moe_grouped(xs, ge, gn, w_gu, w_down)
    return _combine(start, codes, wts, x, gt, gfin, ys, cap=cap, final=final)


def _pad_lanes(v):
    return jnp.pad(v.reshape(1, -1), ((0, 0), (0, LANES - v.shape[-1])))


def _trunk(x, mod, states, p, s5m, *, seq):
    B, L, D = x.shape
    hi = not seq
    tm_proj = min(1024, L)
    tm = min(512, L)
    tm_ffn = min(1024, L)
    s5r0, s5i0, sg0, sc0 = states
    out_states = []
    for l in range(DEPTH):
        m = [mod[l][..., i * D:(i + 1) * D] for i in range(6)]
        sh_m, sc_m, gt_m, sh_f, sc_f, gt_f = m
        w_in = p['w_in'][l]
        w_gates = w_in[:, 2568:]
        w_ab = jnp.pad(w_in[:, 2560:2568], ((0, 0), (0, LANES - 8)))
        u, qkv, z, ga, gb, ab = _proj(x, p['g_mix'][l].reshape(1, D), sc_m, sh_m, w_in, w_gates, w_ab,
                                      tm=tm_proj, hi=hi, chunked=seq)
        alog = _pad_lanes(p['gdn_a_log'][l])
        dtb = _pad_lanes(p['gdn_dt_bias'][l])
        nw = p['gdn_norm_w'][l].reshape(1, GDN_DK)
        if seq:
            yg, sfin = _s5_seq(u, s5m['be'][l], s5m['tp'][l], s5m['cpm'][l], s5m['pt'][l],
                               jnp.zeros((SLABS, B, 1, 2 * SLAB_STATE), F32), s5m['dsk'][l])
            sfin = sfin.reshape(SLABS, B, 2, SLAB_STATE).transpose(2, 1, 0, 3)
            sr = sfin[0].reshape(B, S5_GROUPS, S5_STATE)
            si = sfin[1].reshape(B, S5_GROUPS, S5_STATE)
            og, sg = _gdn_seq(qkv, z, ab, p['gdn_conv_w'][l], alog, dtb, nw,
                              jnp.zeros((B, GDN_CONV - 1, QKV_WIDTH), F32),
                              jnp.zeros((B, GDN_HEADS, GDN_DK, GDN_DK), F32))
            cb = qkv[:, L - (GDN_CONV - 1):, :]
        else:
            n = L
            s0 = jnp.concatenate([s5r0[l].reshape(n, SLABS, SLAB_STATE),
                                  s5i0[l].reshape(n, SLABS, SLAB_STATE)], axis=-1).transpose(1, 0, 2)
            yg, s1 = _s5_step(u.reshape(SLABS, n, LANES), s5m['bst'][l], s5m['c0'][l], s5m['a1'][l],
                              s0, s5m['d1'][l])
            yg = yg.reshape(SLABS, 1, n, LANES)
            s1 = s1.transpose(1, 0, 2)
            sr = s1[:, :, :SLAB_STATE].reshape(n, S5_GROUPS, S5_STATE)
            si = s1[:, :, SLAB_STATE:].reshape(n, S5_GROUPS, S5_STATE)
            og, sg = _gdn_step(qkv.reshape(n, QKV_WIDTH), z.reshape(n, GDN_WIDTH), ab.reshape(n, LANES),
                               p['gdn_conv_w'][l], alog, dtb, nw,
                               sc0[l].reshape(n, (GDN_CONV - 1) * QKV_WIDTH), sg0[l])
            og = og.reshape(1, n, GDN_WIDTH)
            cb = jnp.concatenate([sc0[l][:, 1:, :], qkv.reshape(n, 1, QKV_WIDTH)], axis=1)
        x, h, lg = _merge(yg, og, ga, gb, x, gt_m, p['w_s5_glu'][l], p['w_gdn_out'][l], p['w_out'][l],
                          p['g_ffn'][l].reshape(1, D), sc_f, sh_f, p['w_router'][l // 2], tm=tm, hi=hi,
                          chunked=seq)
        final = l == DEPTH - 1
        gfin = p['g_final'].reshape(1, D)
        if l % 2 == 0:
            x = _ffn(h, x, gt_f, p['w_ffn_gate_up'][l // 2], p['w_ffn_down'][l // 2], gfin,
                     tm=tm_ffn, hi=hi, final=final)
        elif seq:
            x = _moe_seq(h, x, gt_f, lg, p['b_router'][l // 2], p['w_exp_gate_up'][l // 2],
                         p['w_exp_down'][l // 2], gfin, final=final)
        else:
            x = _moe(h, x, gt_f, lg, p['b_router'][l // 2].reshape(1, N_EXPERTS),
                     p['w_exp_gate_up'][l // 2], p['w_exp_down'][l // 2], gfin, tm=tm_ffn, final=final)
        out_states.append((sr, si, sg, cb))
    st = [jnp.stack([o[i] for o in out_states]) for i in range(4)]
    return x, st


def kernel(x_prompt, x_sample, c_prompt, c_sample, state_s5_re, state_s5_im, state_gdn, state_conv,
           g_mix, g_ffn, g_final, w_ada, b_ada, w_in, s5_lambda_re, s5_lambda_im, s5_log_dt,
           s5_b_re, s5_b_im, s5_c_re, s5_c_im, s5_d, w_s5_glu, gdn_conv_w, gdn_a_log, gdn_dt_bias,
           gdn_norm_w, w_gdn_out, w_out, w_ffn_gate_up, w_ffn_down, w_router, b_router,
           w_exp_gate_up, w_exp_down):
    p = dict(g_mix=g_mix, g_ffn=g_ffn, g_final=g_final, w_in=w_in, w_s5_glu=w_s5_glu,
             gdn_conv_w=gdn_conv_w, gdn_a_log=gdn_a_log, gdn_dt_bias=gdn_dt_bias,
             gdn_norm_w=gdn_norm_w, w_gdn_out=w_gdn_out, w_out=w_out, w_ffn_gate_up=w_ffn_gate_up,
             w_ffn_down=w_ffn_down, w_router=w_router, b_router=b_router,
             w_exp_gate_up=w_exp_gate_up, w_exp_down=w_exp_down)
    nbp, L, D = x_prompt.shape
    nbs = x_sample.shape[0]

    mod = _ada(jnp.concatenate([c_prompt, c_sample], axis=0), w_ada, b_ada)
    mod_p = mod[:, :nbp].reshape(DEPTH, nbp, 1, 6 * D)
    mod_s = mod[:, nbp:].reshape(DEPTH, 1, nbs, 6 * D)

    seg = L // S5_T // 8
    be, cpw, pt, a1 = _s5_prep(s5_lambda_re, s5_lambda_im, s5_log_dt, s5_b_re, s5_b_im,
                               s5_c_re, s5_c_im, seg)
    W2 = 2 * SLAB_STATE
    s5m = dict(be=[], tp=[], cpm=[], pt=[], dsk=[], bst=[], c0=[], a1=[], d1=[])
    for l in range(DEPTH):
        be_e = _embed_be(be[l])
        cp_e = _embed_cp(cpw[l])
        bst = be_e[:, (S5_T - 1) * LANES:, :]
        s5m['be'].append(be_e.astype(BF16))
        s5m['tp'].append(_pair_tiles(_toep(bst, cp_e)).astype(BF16))
        s5m['cpm'].append(cp_e[:, 1:].transpose(0, 2, 1, 3).reshape(SLABS, W2, S5_T * LANES).astype(BF16))
        s5m['pt'].append(pt[l].transpose(2, 0, 1, 3).reshape(SLABS, seg + 1, W2))
        d1 = s5_d[l].reshape(SLABS, 1, LANES)
        s5m['dsk'].append(jnp.tile(d1, (1, 1, S5_T)))
        s5m['bst'].append(bst)
        s5m['c0'].append(cp_e[:, 0])
        s5m['a1'].append(a1[l].transpose(1, 0, 2).reshape(SLABS, 1, W2))
        s5m['d1'].append(d1)

    y_p, st_p = _trunk(x_prompt, mod_p, (None, None, None, None), p, s5m, seq=True)
    y_s, st_s = _trunk(x_sample.reshape(1, nbs, D), mod_s,
                       (state_s5_re, state_s5_im, state_gdn, state_conv), p, s5m, seq=False)
    return (y_p, y_s.reshape(nbs, 1, D), st_p[0], st_p[1], st_p[2], st_p[3],
            st_s[0], st_s[1], st_s[2], st_s[3])
```

```python
import functools

import jax
import jax.numpy as jnp
from jax import lax
from jax.experimental import pallas as pl
from jax.experimental.pallas import tpu as pltpu

F32 = jnp.float32
BF16 = jnp.bfloat16
HI = lax.Precision.HIGHEST

D_MODEL = 1024
DEPTH = 2
S5_WIDTH = 512
S5_GROUP = 16
S5_GROUPS = 32
S5_STATE = 64
GDN_HEADS = 4
GDN_DK = 128
GDN_WIDTH = 512
GDN_CONV = 4
QKV_WIDTH = 1536
D_FF = 3584
N_EXPERTS = 8
NORM_EPS = 1e-6
L2_EPS = 1e-6

LANES = 128
SLABS = S5_WIDTH // LANES
SLAB_STATE = (S5_GROUPS // SLABS) * S5_STATE
S5_T = 8
GDN_C = 128
VMEM_LIMIT = 56 * 1024 * 1024


def _cp(sem, vmem=VMEM_LIMIT):
    return pltpu.CompilerParams(dimension_semantics=sem, vmem_limit_bytes=vmem)


def _dot(a, b, prec=None):
    return jnp.dot(a, b, precision=prec, preferred_element_type=F32)


def _dotb(a, b):
    return jnp.dot(a.astype(BF16), b.astype(BF16), preferred_element_type=F32)


def _dot_nt(a, b, prec=None):
    return lax.dot_general(a, b, (((1,), (1,)), ((), ())), precision=prec,
                           preferred_element_type=F32)


def _dot_tn(a, b, prec=None):
    return lax.dot_general(a, b, (((0,), (0,)), ((), ())), precision=prec,
                           preferred_element_type=F32)


def _silu(x):
    return x * jax.nn.sigmoid(x)


def _ada_kernel(c_ref, w_ref, b_ref, o_ref):
    cs = _silu(c_ref[...])
    o_ref[0] = _dot(cs, w_ref[0], HI) + b_ref[0]


def _ada(c_all, w_ada, b_ada):
    n = c_all.shape[0]
    tn = 1536
    return pl.pallas_call(
        _ada_kernel,
        out_shape=jax.ShapeDtypeStruct((DEPTH, n, 6 * D_MODEL), F32),
        grid=(DEPTH, 6 * D_MODEL // tn),
        in_specs=[pl.BlockSpec((n, D_MODEL), lambda l, j: (0, 0)),
                  pl.BlockSpec((1, D_MODEL, tn), lambda l, j: (l, 0, j)),
                  pl.BlockSpec((1, 1, tn), lambda l, j: (l, 0, j))],
        out_specs=pl.BlockSpec((1, n, tn), lambda l, j: (l, 0, j)),
        compiler_params=_cp(("parallel", "parallel")),
        name="ada_mod",
    )(c_all, w_ada, b_ada.reshape(DEPTH, 1, 6 * D_MODEL))


def _proj_kernel(x_ref, g_ref, sc_ref, sh_ref, w_ref, wg_ref, wab_ref,
                 u_ref, qkv_ref, z_ref, ga_ref, gb_ref, ab_ref, h_scr, *us_scr, hi):
    j = pl.program_id(2)

    @pl.when(j == 0)
    def _():
        x = x_ref[0]
        ms = jnp.mean(x * x, axis=-1, keepdims=True)
        xn = x * lax.rsqrt(ms + NORM_EPS) * g_ref[...]
        h_scr[...] = (xn * (1.0 + sc_ref[0]) + sh_ref[0]).astype(h_scr.dtype)

    def mm(w):
        if hi:
            return _dot(h_scr[...], w, HI)
        return _dot(h_scr[...], w.astype(BF16))

    @pl.when(j == 0)
    def _():
        res = mm(w_ref[...])
        for k in range(SLABS):
            if not us_scr:
                u_ref[k, 0] = res[:, k * LANES:(k + 1) * LANES]
                continue
            us_scr[0][...] = res[:, k * LANES:(k + 1) * LANES]
            nrow = res.shape[0] // S5_T
            for t in range(S5_T):
                u_ref[k, 0, :, t * LANES:(t + 1) * LANES] = us_scr[0][pl.ds(t, nrow, stride=S5_T), :]

    @pl.when((j >= 1) & (j <= 3))
    def _():
        qkv_ref[0] = mm(w_ref[...])

    @pl.when(j == 4)
    def _():
        z_ref[0] = mm(w_ref[...])

    @pl.when((j == 5) | (j == 6))
    def _():
        ga_ref[0] = jax.nn.sigmoid(mm(wg_ref[...]))

    @pl.when((j == 7) | (j == 8))
    def _():
        gb_ref[0] = jax.nn.sigmoid(mm(wg_ref[...]))

    @pl.when(j == 9)
    def _():
        ab_ref[0] = mm(wab_ref[...])


def _proj(x, g, sc, sh, w_in, w_gates, w_ab, *, tm, hi, chunked):
    B, L, D = x.shape
    lm = sc.shape[1]
    tmm = 1 if lm == 1 else tm
    mod_map = (lambda b, i, j: (b, 0, 0)) if lm == 1 else (lambda b, i, j: (b, i, 0))
    tn = 512
    clampi = lambda j, lo, n: jnp.clip(j - lo, 0, n - 1)
    if chunked:
        u_shape = jax.ShapeDtypeStruct((SLABS, B, L // S5_T, S5_T * LANES), F32)
        u_spec = pl.BlockSpec((SLABS, 1, tm // S5_T, S5_T * LANES), lambda b, i, j: (0, b, i, 0))
    else:
        u_shape = jax.ShapeDtypeStruct((SLABS, B, L, LANES), F32)
        u_spec = pl.BlockSpec((SLABS, 1, tm, LANES), lambda b, i, j: (0, b, i, 0))
    outs = pl.pallas_call(
        functools.partial(_proj_kernel, hi=hi),
        out_shape=(u_shape,
                   jax.ShapeDtypeStruct((B, L, QKV_WIDTH), F32),
                   jax.ShapeDtypeStruct((B, L, GDN_WIDTH), F32),
                   jax.ShapeDtypeStruct((B, L, D), F32),
                   jax.ShapeDtypeStruct((B, L, D), F32),
                   jax.ShapeDtypeStruct((B, L, LANES), F32)),
        grid=(B, L // tm, 10),
        in_specs=[pl.BlockSpec((1, tm, D), lambda b, i, j: (b, i, 0)),
                  pl.BlockSpec((1, D), lambda b, i, j: (0, 0)),
                  pl.BlockSpec((1, tmm, D), mod_map),
                  pl.BlockSpec((1, tmm, D), mod_map),
                  pl.BlockSpec((D, tn), lambda b, i, j: (0, jnp.minimum(j, 4))),
                  pl.BlockSpec((D, tn), lambda b, i, j: (0, clampi(j, 5, 4))),
                  pl.BlockSpec((D, LANES), lambda b, i, j: (0, 0))],
        out_specs=(u_spec,
                   pl.BlockSpec((1, tm, tn), lambda b, i, j: (b, i, clampi(j, 1, 3))),
                   pl.BlockSpec((1, tm, tn), lambda b, i, j: (b, i, 0)),
                   pl.BlockSpec((1, tm, tn), lambda b, i, j: (b, i, clampi(j, 5, 2))),
                   pl.BlockSpec((1, tm, tn), lambda b, i, j: (b, i, clampi(j, 7, 2))),
                   pl.BlockSpec((1, tm, LANES), lambda b, i, j: (b, i, 0))),
        scratch_shapes=[pltpu.VMEM((tm, D), F32 if hi else BF16)]
        + ([pltpu.VMEM((tm, LANES), F32)] if chunked else []),
        compiler_params=_cp(("parallel", "parallel", "arbitrary")),
        name="norm_in_proj",
    )(x, g, sc, sh, w_in, w_gates, w_ab)
    return outs


GROUPS_PER_SLAB = S5_GROUPS // SLABS


def _s5_prep_kernel(lrb, lib, dtb, bre, bim, lrc, lic, dtc, cre, cim, lrn, lin, dtn,
                    be_ref, bst_ref, cpe_ref, cpm_ref, pt_ref, a1_ref, *, seg):
    W = SLAB_STATE

    def disc(lr, li, ldt):
        dt = jnp.exp(ldt)
        mag = jnp.exp(lr * dt)
        return mag * jnp.cos(li * dt), mag * jnp.sin(li * dt)

    def cmul(xr, xi, yr, yi):
        return xr * yr - xi * yi, xr * yi + xi * yr

    lr, li = lrb[0], lib[0]
    ar, ai = disc(lr, li, dtb[0])
    den = lr * lr + li * li
    nr = ar - 1.0
    kr = (nr * lr + ai * li) / den
    ki = (ai * lr - nr * li) / den
    br, bi = bre[0], bim[0]
    bbr = kr * br - ki * bi
    bbi = kr * bi + ki * br
    rgrp = lax.broadcasted_iota(jnp.int32, (LANES, LANES), 0) // S5_GROUP
    lane_hi = lax.broadcasted_iota(jnp.int32, (LANES, LANES), 1) // S5_STATE
    pr, pi = jnp.ones_like(ar), jnp.zeros_like(ar)
    for d in range(S5_T):
        t = S5_T - 1 - d
        for ri, val in enumerate(cmul(pr, pi, bbr, bbi)):
            two = jnp.concatenate([val, val], axis=1)
            for m in range(GROUPS_PER_SLAB // 2):
                tile = jnp.where(rgrp == 2 * m + lane_hi, two, 0.0)
                c0 = ri * W + m * LANES
                be_ref[0, 0, t * LANES:(t + 1) * LANES, c0:c0 + LANES] = tile.astype(BF16)
                if d == 0:
                    bst_ref[0, 0, :, c0:c0 + LANES] = tile
        pr, pi = cmul(pr, pi, ar, ai)

    ar, ai = disc(lrc[0], lic[0], dtc[0])
    cr, ci = cre[0], cim[0]
    own = (lax.broadcasted_iota(jnp.int32, (W, LANES), 0) // S5_STATE
           == lax.broadcasted_iota(jnp.int32, (W, LANES), 1) // S5_GROUP)
    pr, pi = jnp.ones_like(ar), jnp.zeros_like(ar)
    for d in range(S5_T + 1):
        vr, vi = cmul(cr, ci, pr, pi)
        for ri, val in enumerate((vr, -vi)):
            tile = jnp.where(own, val, 0.0)
            cpe_ref[0, 0, d, ri * W:(ri + 1) * W, :] = tile
            if d >= 1:
                cpm_ref[0, 0, ri * W:(ri + 1) * W, (d - 1) * LANES:d * LANES] = tile.astype(BF16)
        pr, pi = cmul(pr, pi, ar, ai)

    ar, ai = disc(lrn[0, 0], lin[0, 0], dtn[0, 0])
    a1_ref[0, 0, :, 0:W] = ar
    a1_ref[0, 0, :, W:2 * W] = ai
    tr, ti = ar, ai
    for _ in range(S5_T - 1):
        tr, ti = cmul(tr, ti, ar, ai)
    pr, pi = jnp.ones_like(ar), jnp.zeros_like(ar)
    for i in range(seg + 1):
        pt_ref[0, 0, i:i + 1, 0:W] = pr
        pt_ref[0, 0, i:i + 1, W:2 * W] = pi
        pr, pi = cmul(pr, pi, tr, ti)


def _s5_prep(lam_re, lam_im, log_dt, b_re, b_im, c_re, c_im, seg):
    G, P, C = S5_GROUPS, S5_STATE, S5_GROUP
    W2 = 2 * SLAB_STATE
    dt3 = jnp.broadcast_to(log_dt[:, :, None], (DEPTH, G, P))
    rows_b = lambda a: jnp.repeat(a, C, axis=1)
    bt = lambda a: a.transpose(0, 1, 3, 2).reshape(DEPTH, G * C, P)
    rows_c = lambda a: jnp.broadcast_to(a.reshape(DEPTH, G * P, 1), (DEPTH, G * P, LANES))
    ct = lambda a: jnp.tile(a.transpose(0, 1, 3, 2).reshape(DEPTH, G * P, C), (1, 1, LANES // C))
    nat = lambda a: a.reshape(DEPTH, SLABS, 1, SLAB_STATE)
    args = (rows_b(lam_re), rows_b(lam_im), rows_b(dt3), bt(b_re), bt(b_im),
            rows_c(lam_re), rows_c(lam_im), rows_c(dt3), ct(c_re), ct(c_im),
            nat(lam_re), nat(lam_im), nat(dt3))
    bspec = pl.BlockSpec((1, LANES, P), lambda l, k: (l, k, 0))
    cspec = pl.BlockSpec((1, SLAB_STATE, LANES), lambda l, k: (l, k, 0))
    nspec = pl.BlockSpec((1, 1, 1, SLAB_STATE), lambda l, k: (l, k, 0, 0))
    return pl.pallas_call(
        functools.partial(_s5_prep_kernel, seg=seg),
        out_shape=(jax.ShapeDtypeStruct((DEPTH, SLABS, S5_T * LANES, W2), BF16),
                   jax.ShapeDtypeStruct((DEPTH, SLABS, LANES, W2), F32),
                   jax.ShapeDtypeStruct((DEPTH, SLABS, S5_T + 1, W2, LANES), F32),
                   jax.ShapeDtypeStruct((DEPTH, SLABS, W2, S5_T * LANES), BF16),
                   jax.ShapeDtypeStruct((DEPTH, SLABS, seg + 1, W2), F32),
                   jax.ShapeDtypeStruct((DEPTH, SLABS, 1, W2), F32)),
        grid=(DEPTH, SLABS),
        in_specs=[bspec] * 5 + [cspec] * 5 + [nspec] * 3,
        out_specs=(pl.BlockSpec((1, 1, S5_T * LANES, W2), lambda l, k: (l, k, 0, 0)),
                   pl.BlockSpec((1, 1, LANES, W2), lambda l, k: (l, k, 0, 0)),
                   pl.BlockSpec((1, 1, S5_T + 1, W2, LANES), lambda l, k: (l, k, 0, 0, 0)),
                   pl.BlockSpec((1, 1, W2, S5_T * LANES), lambda l, k: (l, k, 0, 0)),
                   pl.BlockSpec((1, 1, seg + 1, W2), lambda l, k: (l, k, 0, 0)),
                   pl.BlockSpec((1, 1, 1, W2), lambda l, k: (l, k, 0, 0))),
        compiler_params=_cp(("parallel", "parallel")),
        name="s5_discretize",
    )(*args)


def _toep_kernel(b_ref, c_ref, o_ref):
    dd = pl.program_id(2)
    bst = b_ref[0, 0]
    lag = lambda d: _dot(bst, c_ref[0, 0, d], HI)
    k0 = lag(2 * dd)
    o_ref[0, 0, 0, 0:LANES, 0:LANES] = k0.astype(BF16)
    o_ref[0, 0, 0, LANES:, LANES:] = k0.astype(BF16)
    o_ref[0, 0, 0, 0:LANES, LANES:] = lag(2 * dd + 1).astype(BF16)
    km = lag(jnp.maximum(2 * dd - 1, 0))
    o_ref[0, 0, 0, LANES:, 0:LANES] = jnp.where(dd > 0, km, 0.0).astype(BF16)


def _toep(bst, cpe):
    W2 = 2 * SLAB_STATE
    return pl.pallas_call(
        _toep_kernel,
        out_shape=jax.ShapeDtypeStruct((DEPTH, SLABS, S5_T // 2, 2 * LANES, 2 * LANES), BF16),
        grid=(DEPTH, SLABS, S5_T // 2),
        in_specs=[pl.BlockSpec((1, 1, LANES, W2), lambda l, k, d: (l, k, 0, 0)),
                  pl.BlockSpec((1, 1, S5_T + 1, W2, LANES), lambda l, k, d: (l, k, 0, 0, 0))],
        out_specs=pl.BlockSpec((1, 1, 1, 2 * LANES, 2 * LANES), lambda l, k, d: (l, k, d, 0, 0)),
        compiler_params=_cp(("parallel", "parallel", "parallel")),
        name="s5_conv_blocks",
    )(bst, cpe)


def _s5_seq_kernel(up_ref, be_ref, tp_ref, cpm_ref, pt_ref, s0_ref, dsk_ref,
                   yg_ref, sfin_ref, e_scr, sx_scr, *, nc):
    seg = nc // 8
    W = SLAB_STATE
    nt = W // LANES
    u = up_ref[0, 0]
    ub = u.astype(BF16)
    e = _dot(ub, be_ref[0, 0])
    for c in range(2 * nt):
        e_scr[c] = e[:, c * LANES:(c + 1) * LANES]

    def tiles(row):
        return [(row[:, c * LANES:(c + 1) * LANES], row[:, W + c * LANES:W + (c + 1) * LANES])
                for c in range(nt)]

    a8 = [(jnp.broadcast_to(r, (8, LANES)), jnp.broadcast_to(i, (8, LANES)))
          for r, i in tiles(pt_ref[0, 0, 1:2, :])]

    def step(i, carry):
        rows = pl.ds(i, 8, stride=seg)
        new = []
        for c in range(nt):
            sr, si = carry[c]
            ar, ai = a8[c]
            sx_scr[c, rows, :] = sr
            sx_scr[nt + c, rows, :] = si
            new.append((ar * sr - ai * si + e_scr[c, rows, :],
                        ar * si + ai * sr + e_scr[nt + c, rows, :]))
        return tuple(new)

    zero = jnp.zeros((8, LANES), F32)
    ends = lax.fori_loop(0, seg, step, tuple((zero, zero) for _ in range(nt)))

    al = tiles(pt_ref[0, 0, seg:seg + 1, :])
    cur = tiles(s0_ref[0, 0])
    car = []
    for c in range(nt):
        alr, ali = al[c]
        cr, ci = cur[c]
        sr, si = ends[c]
        crs, cis = [], []
        for j in range(8):
            crs.append(cr)
            cis.append(ci)
            cr, ci = (alr * cr - ali * ci + sr[j:j + 1], alr * ci + ali * cr + si[j:j + 1])
        sfin_ref[0, 0, :, c * LANES:(c + 1) * LANES] = cr
        sfin_ref[0, 0, :, W + c * LANES:W + (c + 1) * LANES] = ci
        car.append((jnp.concatenate(crs, axis=0), jnp.concatenate(cis, axis=0)))

    def corr(i, _):
        rows = pl.ds(i, 8, stride=seg)
        pw = tiles(pt_ref[0, 0, pl.ds(i, 1), :])
        for c in range(nt):
            pr, pi = pw[c]
            cr, ci = car[c]
            sx_scr[c, rows, :] = sx_scr[c, rows, :] + (pr * cr - pi * ci)
            sx_scr[nt + c, rows, :] = sx_scr[nt + c, rows, :] + (pr * ci + pi * cr)
        return 0

    lax.fori_loop(0, seg, corr, 0)

    sx = jnp.concatenate([sx_scr[c] for c in range(2 * nt)], axis=-1)
    y = _dot(sx.astype(BF16), cpm_ref[0, 0])
    TW = 2 * LANES
    for tq in range(S5_T // 2):
        acc = y[:, tq * TW:(tq + 1) * TW]
        for tpi in range(tq + 1):
            acc = acc + _dot(ub[:, tpi * TW:(tpi + 1) * TW], tp_ref[0, 0, tq - tpi])
        acc = acc + dsk_ref[0, :, tq * TW:(tq + 1) * TW] * u[:, tq * TW:(tq + 1) * TW]
        yg_ref[0, 0, :, tq * TW:(tq + 1) * TW] = jax.nn.gelu(acc)


def _s5_seq(up, be_emb, tp, cpm, pt, s0, dsk, l):
    _, B, nc, _ = up.shape
    seg = nc // 8
    W2 = 2 * SLAB_STATE
    yg, sfin = pl.pallas_call(
        functools.partial(_s5_seq_kernel, nc=nc),
        out_shape=(jax.ShapeDtypeStruct((SLABS, B, nc, S5_T * LANES), F32),
                   jax.ShapeDtypeStruct((SLABS, B, 1, W2), F32)),
        grid=(SLABS, B),
        in_specs=[pl.BlockSpec((1, 1, nc, S5_T * LANES), lambda k, b: (k, b, 0, 0)),
                  pl.BlockSpec((1, 1, S5_T * LANES, W2), lambda k, b: (l, k, 0, 0)),
                  pl.BlockSpec((1, 1, S5_T // 2, 2 * LANES, 2 * LANES), lambda k, b: (l, k, 0, 0, 0)),
                  pl.BlockSpec((1, 1, W2, S5_T * LANES), lambda k, b: (l, k, 0, 0)),
                  pl.BlockSpec((1, 1, seg + 1, W2), lambda k, b: (l, k, 0, 0)),
                  pl.BlockSpec((1, 1, 1, W2), lambda k, b: (k, b, 0, 0)),
                  pl.BlockSpec((1, 1, S5_T * LANES), lambda k, b: (k, 0, 0))],
        out_specs=(pl.BlockSpec((1, 1, nc, S5_T * LANES), lambda k, b: (k, b, 0, 0)),
                   pl.BlockSpec((1, 1, 1, W2), lambda k, b: (k, b, 0, 0))),
        scratch_shapes=[pltpu.VMEM((W2 // LANES, nc, LANES), F32),
                        pltpu.VMEM((W2 // LANES, nc, LANES), F32)],
        compiler_params=_cp(("parallel", "parallel")),
        name="s5_seq",
    )(up, be_emb, tp, cpm, pt, s0, dsk)
    return yg, sfin


def _s5_step_kernel(u_ref, b_ref, c_ref, a_ref, s0_ref, d_ref, yg_ref, s1_ref):
    W = SLAB_STATE
    u = u_ref[0]
    bu = _dot(u, b_ref[0, 0], HI)
    ar = a_ref[0, 0, :, 0:W]
    ai = a_ref[0, 0, :, W:2 * W]
    sr = s0_ref[0, :, 0:W]
    si = s0_ref[0, :, W:2 * W]
    nr = ar * sr - ai * si + bu[:, 0:W]
    ni = ar * si + ai * sr + bu[:, W:2 * W]
    s1_ref[0, :, 0:W] = nr
    s1_ref[0, :, W:2 * W] = ni
    s1 = jnp.concatenate([nr, ni], axis=-1)
    y = _dot(s1, c_ref[0, 0, 0], HI) + d_ref[0] * u
    yg_ref[0] = jax.nn.gelu(y)


def _s5_step(u_slab, bst, cpe, a1, s0, d1, l):
    _, N, _ = u_slab.shape
    W2 = 2 * SLAB_STATE
    return pl.pallas_call(
        _s5_step_kernel,
        out_shape=(jax.ShapeDtypeStruct((SLABS, N, LANES), F32),
                   jax.ShapeDtypeStruct((SLABS, N, W2), F32)),
        grid=(SLABS,),
        in_specs=[pl.BlockSpec((1, N, LANES), lambda k: (k, 0, 0)),
                  pl.BlockSpec((1, 1, LANES, W2), lambda k: (l, k, 0, 0)),
                  pl.BlockSpec((1, 1, 1, W2, LANES), lambda k: (l, k, 0, 0, 0)),
                  pl.BlockSpec((1, 1, 1, W2), lambda k: (l, k, 0, 0)),
                  pl.BlockSpec((1, N, W2), lambda k: (k, 0, 0)),
                  pl.BlockSpec((1, 1, LANES), lambda k: (k, 0, 0))],
        out_specs=(pl.BlockSpec((1, N, LANES), lambda k: (k, 0, 0)),
                   pl.BlockSpec((1, N, W2), lambda k: (k, 0, 0))),
        compiler_params=_cp(("parallel",)),
        name="s5_step",
    )(u_slab, bst, cpe, a1, s0, d1)


def _l2n(x):
    return x * lax.rsqrt(jnp.sum(x * x, axis=-1, keepdims=True) + L2_EPS)


def _split_bf16(x):
    hi = x.astype(BF16)
    return hi, (x - hi.astype(F32)).astype(BF16)


def _unit_lower_solve(As, rhss):
    n = GDN_C
    row = lax.broadcasted_iota(jnp.int32, (n, n), 0)
    col = lax.broadcasted_iota(jnp.int32, (n, n), 1)
    eye = (row == col).astype(F32)
    same8 = (row // 8) == (col // 8)
    Qs = [jnp.where(same8, -A, 0.0) for A in As]
    invs = [eye + Q for Q in Qs]
    for _ in range(2):
        Qs = [_dotb(Q, Q) for Q in Qs]
        invs = [inv + _dotb(inv, Q) for inv, Q in zip(invs, Qs)]
    s = 8
    while s < n:
        sib = ((row // (2 * s)) == (col // (2 * s))) & ((row // s) != (col // s))
        offs = [jnp.where(sib, A, 0.0).astype(BF16) for A in As]
        invb = [inv.astype(BF16) for inv in invs]
        tmp = [_dot(off, ib) for off, ib in zip(offs, invb)]
        invs = [inv - _dot(ib, t.astype(BF16)) for inv, ib, t in zip(invs, invb, tmp)]
        s *= 2
    invb = [inv.astype(BF16) for inv in invs]
    x0s = [_dot(ib, rhs.astype(BF16)) for ib, rhs in zip(invb, rhss)]
    res = []
    for A, x0, rhs in zip(As, x0s, rhss):
        ah, al = _split_bf16(A)
        xh, xl = _split_bf16(x0)
        res.append(rhs - x0 - (_dot(ah, xh) + _dot(ah, xl) + _dot(al, xh)))
    return [x0 + _dot(ib, r.astype(BF16)) for x0, ib, r in zip(x0s, invb, res)]


def _gdn_tile(qc_scr, gc, beta, z_ref, nw, o_ref, s_scr, tl):
    C, DK, H = GDN_C, GDN_DK, GDN_HEADS
    nchunk = tl // C
    probs = [(c, h) for c in range(nchunk) for h in range(H)]
    row = lax.broadcasted_iota(jnp.int32, (C, C), 0)
    col = lax.broadcasted_iota(jnp.int32, (C, C), 1)
    tri = row >= col
    strict = row > col

    def blk(c, off):
        return qc_scr[c * C:(c + 1) * C, off:off + DK]

    q = [_l2n(blk(c, h * DK)) * (DK ** -0.5) for c, h in probs]
    k = [_l2n(blk(c, GDN_WIDTH + h * DK)) for c, h in probs]
    v = [blk(c, 2 * GDN_WIDTH + h * DK) for c, h in probs]
    gcb = [jnp.broadcast_to(gc[c * C:(c + 1) * C, h:h + 1], (C, DK)) for c, h in probs]
    bb = [jnp.broadcast_to(beta[c * C:(c + 1) * C, H + h:H + h + 1], (C, DK)) for c, h in probs]
    decay = []
    for g in gcb:
        diff = g - g.T
        decay.append(jnp.where(tri, jnp.exp(jnp.where(tri, diff, 0.0)), 0.0))
    kbf = [x.astype(BF16) for x in k]
    kb = [x * b for x, b in zip(k, bb)]
    A = [jnp.where(strict, _dot_nt(x.astype(BF16), y) * d, 0.0) for x, y, d in zip(kb, kbf, decay)]
    egc = [jnp.exp(g) for g in gcb]
    rhs = [jnp.concatenate([x * b, y * e], axis=-1) for x, b, y, e in zip(v, bb, kb, egc)]
    sol = _unit_lower_solve(A, rhs)
    attn = [jnp.where(tri, _dot_nt(x.astype(BF16), y) * d, 0.0).astype(BF16)
            for x, y, d in zip(q, kbf, decay)]
    glast = [g[C - 1:C, :] for g in gcb]
    wq = [jnp.concatenate([s[:, DK:], x * e], axis=0).astype(BF16) for s, x, e in zip(sol, q, egc)]
    kg = [(x * jnp.exp(gl - g)).astype(BF16) for x, gl, g in zip(k, glast, gcb)]

    for c in range(nchunk):
        ps = [c * H + h for h in range(H)]
        S = [s_scr[h] for h in range(H)]
        ws = [_dot(wq[p], S[h].astype(BF16)) for h, p in enumerate(ps)]
        v_new = [sol[p][:, 0:DK] - w[0:C] for p, w in zip(ps, ws)]
        vb = [x.astype(BF16) for x in v_new]
        o = [w[C:] + _dot(attn[p], x) for p, w, x in zip(ps, ws, vb)]
        for h, p in enumerate(ps):
            s_scr[h] = S[h] * jnp.exp(glast[p]) + _dot_tn(kg[p], vb[h])
            zh = z_ref[0, c * C:(c + 1) * C, h * DK:(h + 1) * DK]
            on = o[h] * lax.rsqrt(jnp.mean(o[h] * o[h], axis=-1, keepdims=True) + NORM_EPS) * nw
            o_ref[0, c * C:(c + 1) * C, h * DK:(h + 1) * DK] = on * _silu(zh)


def _gdn_seq_kernel(qkv_ref, z_ref, ab_ref, cw_ref, alog_ref, dtb_ref, nw_ref, conv0_ref, s0_ref,
                    o_ref, sfin_ref, xp_scr, qc_scr, s_scr, *, tl):
    lt = pl.program_id(1)

    @pl.when(lt == 0)
    def _():
        xp_scr[0:8, :] = jnp.zeros((8, QKV_WIDTH), F32)
        xp_scr[8 - (GDN_CONV - 1):8, :] = conv0_ref[0]
        s_scr[...] = s0_ref[0]

    xp_scr[8:8 + tl, :] = qkv_ref[0]
    conv = cw_ref[0:1, :] * xp_scr[5:5 + tl, :]
    for j in range(1, GDN_CONV):
        conv = conv + cw_ref[j:j + 1, :] * xp_scr[5 + j:5 + j + tl, :]
    xp_scr[0:8, :] = xp_scr[tl:tl + 8, :]
    qc_scr[...] = _silu(conv)

    ab = ab_ref[0]
    g = -jnp.exp(alog_ref[...]) * jax.nn.softplus(ab + dtb_ref[...])
    beta = jax.nn.sigmoid(ab)
    row = lax.broadcasted_iota(jnp.int32, (tl, tl), 0)
    col = lax.broadcasted_iota(jnp.int32, (tl, tl), 1)
    csum = ((row >= col) & ((row // GDN_C) == (col // GDN_C))).astype(F32)
    gc = _dot(csum, g, HI)
    _gdn_tile(qc_scr, gc, beta, z_ref, nw_ref[...], o_ref, s_scr, tl)

    @pl.when(lt == pl.num_programs(1) - 1)
    def _():
        sfin_ref[0] = s_scr[...]


def _gdn_seq(qkv, z, ab, conv_w, alog, dtb, nw, conv0, s0):
    B, L, _ = qkv.shape
    tl = min(256, L)
    return pl.pallas_call(
        functools.partial(_gdn_seq_kernel, tl=tl),
        out_shape=(jax.ShapeDtypeStruct((B, L, GDN_WIDTH), F32),
                   jax.ShapeDtypeStruct((B, GDN_HEADS, GDN_DK, GDN_DK), F32)),
        grid=(B, L // tl),
        in_specs=[pl.BlockSpec((1, tl, QKV_WIDTH), lambda b, i: (b, i, 0)),
                  pl.BlockSpec((1, tl, GDN_WIDTH), lambda b, i: (b, i, 0)),
                  pl.BlockSpec((1, tl, LANES), lambda b, i: (b, i, 0)),
                  pl.BlockSpec((GDN_CONV, QKV_WIDTH), lambda b, i: (0, 0)),
                  pl.BlockSpec((1, LANES), lambda b, i: (0, 0)),
                  pl.BlockSpec((1, LANES), lambda b, i: (0, 0)),
                  pl.BlockSpec((1, GDN_DK), lambda b, i: (0, 0)),
                  pl.BlockSpec((1, GDN_CONV - 1, QKV_WIDTH), lambda b, i: (b, 0, 0)),
                  pl.BlockSpec((1, GDN_HEADS, GDN_DK, GDN_DK), lambda b, i: (b, 0, 0, 0))],
        out_specs=(pl.BlockSpec((1, tl, GDN_WIDTH), lambda b, i: (b, i, 0)),
                   pl.BlockSpec((1, GDN_HEADS, GDN_DK, GDN_DK), lambda b, i: (b, 0, 0, 0))),
        scratch_shapes=[pltpu.VMEM((tl + 8, QKV_WIDTH), F32),
                        pltpu.VMEM((tl, QKV_WIDTH), F32),
                        pltpu.VMEM((GDN_HEADS, GDN_DK, GDN_DK), F32)],
        compiler_params=_cp(("parallel", "arbitrary")),
        name="gdn_seq",
    )(qkv, z, ab, conv_w, alog, dtb, nw, conv0, s0)


GDN_STEP_ROWS = 8


def _gdn_step_kernel(qkv_ref, z_ref, ab_ref, cw_ref, alog_ref, dtb_ref, nw_ref, conv0_ref, s0_ref,
                     *rest):
    if len(rest) == 3:
        prev_ref, o_ref, s1_all = rest
        s1_all[0] = prev_ref[...]
        s1_ref = s1_all.at[1]
    else:
        o_ref, s1_all = rest
        s1_ref = s1_all
    nb = GDN_STEP_ROWS
    W = QKV_WIDTH
    conv = cw_ref[0:1, :] * conv0_ref[:, 0:W]
    conv = conv + cw_ref[1:2, :] * conv0_ref[:, W:2 * W]
    conv = conv + cw_ref[2:3, :] * conv0_ref[:, 2 * W:3 * W]
    conv = conv + cw_ref[3:4, :] * qkv_ref[...]
    qc = _silu(conv)
    ab = ab_ref[...]
    eg = jnp.exp(-jnp.exp(alog_ref[...]) * jax.nn.softplus(ab + dtb_ref[...]))
    beta = jax.nn.sigmoid(ab)
    eye = (lax.broadcasted_iota(jnp.int32, (GDN_DK, GDN_DK), 0)
           == lax.broadcasted_iota(jnp.int32, (GDN_DK, GDN_DK), 1)).astype(F32)
    for h in range(GDN_HEADS):
        q = _l2n(qc[:, h * GDN_DK:(h + 1) * GDN_DK]) * (GDN_DK ** -0.5)
        k = _l2n(qc[:, GDN_WIDTH + h * GDN_DK:GDN_WIDTH + (h + 1) * GDN_DK])
        v = qc[:, 2 * GDN_WIDTH + h * GDN_DK:2 * GDN_WIDTH + (h + 1) * GDN_DK]
        kT = _dot_nt(eye, k, HI)
        qT = _dot_nt(eye, q, HI)
        qk = jnp.sum(q * k, axis=-1, keepdims=True)
        for j in range(nb):
            S = s0_ref[0, j, h]
            kc = jnp.broadcast_to(kT[:, j:j + 1], (GDN_DK, GDN_DK))
            qcb = jnp.broadcast_to(qT[:, j:j + 1], (GDN_DK, GDN_DK))
            kS = jnp.sum(kc * S, axis=0, keepdims=True)
            qS = jnp.sum(qcb * S, axis=0, keepdims=True)
            egj = eg[j:j + 1, h:h + 1]
            bj = beta[j:j + 1, GDN_HEADS + h:GDN_HEADS + h + 1]
            v_new = bj * v[j:j + 1, :] - (bj * egj) * kS
            o = egj * qS + qk[j:j + 1, :] * v_new
            s1_ref[j, h] = S * egj + kc * v_new
            zh = z_ref[j:j + 1, h * GDN_DK:(h + 1) * GDN_DK]
            on = o * lax.rsqrt(jnp.mean(o * o, axis=-1, keepdims=True) + NORM_EPS) * nw_ref[...]
            o_ref[j:j + 1, h * GDN_DK:(h + 1) * GDN_DK] = on * _silu(zh)


def _gdn_step(qkv, z, ab, conv_w, alog, dtb, nw, conv0, s_all, l, prev):
    N = qkv.shape[0]
    nb = GDN_STEP_ROWS
    row = lambda w: pl.BlockSpec((nb, w), lambda i: (i, 0))
    const = lambda r, w: pl.BlockSpec((r, w), lambda i: (0, 0))
    sblk = (nb, GDN_HEADS, GDN_DK, GDN_DK)
    one = pl.BlockSpec(sblk, lambda i: (i, 0, 0, 0))
    ins = [qkv, z, ab, conv_w, alog, dtb, nw, conv0, s_all]
    in_specs = [row(QKV_WIDTH), row(GDN_WIDTH), row(LANES), const(GDN_CONV, QKV_WIDTH),
                const(1, LANES), const(1, LANES), const(1, GDN_DK), row(3 * QKV_WIDTH),
                pl.BlockSpec((1,) + sblk, lambda i: (l, i, 0, 0, 0))]
    if prev is None:
        s_shape, s_spec = jax.ShapeDtypeStruct((N,) + sblk[1:], F32), one
    else:
        assert DEPTH == 2 and l == 1
        ins.append(prev)
        in_specs.append(one)
        s_shape = jax.ShapeDtypeStruct((DEPTH, N) + sblk[1:], F32)
        s_spec = pl.BlockSpec((DEPTH,) + sblk, lambda i: (0, i, 0, 0, 0))
    return pl.pallas_call(
        _gdn_step_kernel,
        out_shape=(jax.ShapeDtypeStruct((N, GDN_WIDTH), F32), s_shape),
        grid=(N // nb,),
        in_specs=in_specs,
        out_specs=(row(GDN_WIDTH), s_spec),
        compiler_params=_cp(("parallel",)),
        name="gdn_step",
    )(*ins)


def _merge_kernel(yg_ref, og_ref, ga_ref, gb_ref, x_ref, gt_ref, wglu_ref, wgo_ref, wout_ref,
                  gf_ref, scf_ref, shf_ref, wr_ref,
                  xo_ref, h_ref, lg_ref, *scr, hi, chunked):
    if chunked:
        y_scr = scr[-1]
        scr = scr[:-1]
        nrow = y_scr.shape[1] // S5_T
        for k in range(SLABS):
            for t in range(S5_T):
                y_scr[k, pl.ds(t, nrow, stride=S5_T), :] = yg_ref[k, 0, :, t * LANES:(t + 1) * LANES]
        y = jnp.concatenate([y_scr[k] for k in range(SLABS)], axis=-1)
    else:
        y = jnp.concatenate([yg_ref[k, 0] for k in range(SLABS)], axis=-1)
    if hi:
        wglu, wgo, wout = wglu_ref[...], wgo_ref[...], wout_ref[...]
        mm = lambda a, w: _dot(a, w, HI)
    else:
        wglu_s, wgo_s, wout_s = scr

        @pl.when((pl.program_id(0) == 0) & (pl.program_id(1) == 0))
        def _():
            wglu_s[...] = wglu_ref[...].astype(BF16)
            wgo_s[...] = wgo_ref[...].astype(BF16)
            wout_s[...] = wout_ref[...].astype(BF16)

        wglu, wgo, wout = wglu_s[...], wgo_s[...], wout_s[...]
        mm = lambda a, w: _dot(a.astype(BF16), w)

    glu = mm(y, wglu)
    branch_a = glu[:, 0:D_MODEL] * jax.nn.sigmoid(glu[:, D_MODEL:])
    branch_b = mm(og_ref[0], wgo)
    merged = ga_ref[0] * branch_a + gb_ref[0] * branch_b
    out = mm(merged, wout)
    x = x_ref[0] + gt_ref[0] * out
    xo_ref[0] = x
    ms = jnp.mean(x * x, axis=-1, keepdims=True)
    h = x * lax.rsqrt(ms + NORM_EPS) * gf_ref[...]
    h = h * (1.0 + scf_ref[0]) + shf_ref[0]
    h_ref[0] = h
    lg_ref[0] = _dot_nt(wr_ref[...], h, HI)


def _merge(yg, og, ga, gb, x, gt, wglu, wgo, wout, gf, scf, shf, wr, *, tm, hi, chunked):
    B, L, D = x.shape
    lm = gt.shape[1]
    tmm = 1 if lm == 1 else tm
    mod_map = (lambda b, i: (b, 0, 0)) if lm == 1 else (lambda b, i: (b, i, 0))
    row = lambda w: pl.BlockSpec((1, tm, w), lambda b, i: (b, i, 0))
    const = lambda r, w: pl.BlockSpec((r, w), lambda b, i: (0, 0))
    mod = pl.BlockSpec((1, tmm, D), mod_map)
    scratch = [] if hi else [pltpu.VMEM((S5_WIDTH, 2 * D), BF16), pltpu.VMEM((GDN_WIDTH, D), BF16),
                             pltpu.VMEM((D, D), BF16)]
    if chunked:
        scratch = scratch + [pltpu.VMEM((SLABS, tm, LANES), F32)]
        yg_spec = pl.BlockSpec((SLABS, 1, tm // S5_T, S5_T * LANES), lambda b, i: (0, b, i, 0))
    else:
        yg_spec = pl.BlockSpec((SLABS, 1, tm, LANES), lambda b, i: (0, b, i, 0))
    lg_shape = jax.ShapeDtypeStruct((B, N_EXPERTS, L), F32)
    lg_spec = pl.BlockSpec((1, N_EXPERTS, tm), lambda b, i: (b, 0, i))
    wr = wr.T
    return pl.pallas_call(
        functools.partial(_merge_kernel, hi=hi, chunked=chunked),
        out_shape=(jax.ShapeDtypeStruct((B, L, D), F32),
                   jax.ShapeDtypeStruct((B, L, D), F32),
                   lg_shape),
        grid=(B, L // tm),
        in_specs=[yg_spec,
                  row(GDN_WIDTH), row(D), row(D), row(D), mod,
                  const(S5_WIDTH, 2 * D), const(GDN_WIDTH, D), const(D, D),
                  const(1, D), mod, mod, const(*wr.shape)],
        out_specs=(row(D), row(D), lg_spec),
        scratch_shapes=scratch,
        compiler_params=_cp(("arbitrary", "arbitrary")),
        name="merge_out_proj",
    )(yg, og, ga, gb, x, gt, wglu, wgo, wout, gf, scf, shf, wr)


FF_TILE = 512


def _finish(x, gfin_ref, final):
    if not final:
        return x
    ms = jnp.mean(x * x, axis=-1, keepdims=True)
    return x * lax.rsqrt(ms + NORM_EPS) * gfin_ref[...]


def _ffn_kernel(h_ref, x_ref, gt_ref, wg_ref, wu_ref, wd_ref, gfin_ref, o_ref, acc_scr, *, hi, final):
    j = pl.program_id(2)
    if hi:
        h = h_ref[0]
        mm = lambda a, w: _dot(a, w, HI)
    else:
        h = h_ref[0].astype(BF16)
        mm = lambda a, w: _dot(a.astype(BF16), w.astype(BF16))
    act = _silu(mm(h, wg_ref[...])) * mm(h, wu_ref[...])
    part = mm(act, wd_ref[...])

    @pl.when(j == 0)
    def _():
        acc_scr[...] = part

    @pl.when(j > 0)
    def _():
        acc_scr[...] = acc_scr[...] + part

    @pl.when(j == pl.num_programs(2) - 1)
    def _():
        o_ref[0] = _finish(x_ref[0] + gt_ref[0] * acc_scr[...], gfin_ref, final)


def _ffn(h, x, gt, w_gu, w_down, gfin, *, tm, hi, final):
    B, L, D = x.shape
    lm = gt.shape[1]
    tmm = 1 if lm == 1 else tm
    mod_map = (lambda b, i, j: (b, 0, 0)) if lm == 1 else (lambda b, i, j: (b, i, 0))
    nj = D_FF // FF_TILE
    row = pl.BlockSpec((1, tm, D), lambda b, i, j: (b, i, 0))
    return pl.pallas_call(
        functools.partial(_ffn_kernel, hi=hi, final=final),
        out_shape=jax.ShapeDtypeStruct((B, L, D), F32),
        grid=(B, L // tm, nj),
        in_specs=[row, row, pl.BlockSpec((1, tmm, D), mod_map),
                  pl.BlockSpec((D, FF_TILE), lambda b, i, j: (0, j)),
                  pl.BlockSpec((D, FF_TILE), lambda b, i, j: (0, nj + j)),
                  pl.BlockSpec((FF_TILE, D), lambda b, i, j: (j, 0)),
                  pl.BlockSpec((1, D), lambda b, i, j: (0, 0))],
        out_specs=row,
        scratch_shapes=[pltpu.VMEM((tm, D), F32)],
        compiler_params=_cp(("parallel", "parallel", "arbitrary")),
        name="ffn_dense",
    )(h, x, gt, w_gu, w_gu, w_down, gfin)


ROUTE_TM = 512
ROW_DMA_TM = 256
MOE_SUP = 2048
MOE_SUB = 512


def _route_kernel(lg_ref, br_ref, cnt0_ref, slot_ref, wt_ref, cnt_ref, carry_scr, *, cap):
    @pl.when((pl.program_id(0) == 0) & (pl.program_id(1) == 0))
    def _():
        carry_scr[...] = cnt0_ref[...]

    lg = lg_ref[0] + br_ref[...]
    tm = lg.shape[1]
    eidx = lax.broadcasted_iota(jnp.int32, lg.shape, 0)
    m1 = jnp.max(lg, axis=0, keepdims=True)
    i1 = jnp.min(jnp.where(lg == m1, eidx, N_EXPERTS), axis=0, keepdims=True)
    lg2 = jnp.where(eidx == i1, -jnp.inf, lg)
    m2 = jnp.max(lg2, axis=0, keepdims=True)
    i2 = jnp.min(jnp.where(lg2 == m2, eidx, N_EXPERTS), axis=0, keepdims=True)
    e2 = jnp.exp(m2 - m1)
    wt_ref[0, 0:1, :] = 1.0 / (1.0 + e2)
    wt_ref[0, 1:2, :] = e2 / (1.0 + e2)
    sel1 = eidx == i1
    sel2 = eidx == i2
    oh = jnp.where(sel1 | sel2, 1.0, 0.0)
    before = (lax.broadcasted_iota(jnp.int32, (tm, tm), 0)
              < lax.broadcasted_iota(jnp.int32, (tm, tm), 1)).astype(BF16)
    rank = carry_scr[:, 0:1] + _dot(oh.astype(BF16), before)
    r1 = jnp.sum(jnp.where(sel1, rank, 0.0), axis=0, keepdims=True).astype(jnp.int32)
    r2 = jnp.sum(jnp.where(sel2, rank, 0.0), axis=0, keepdims=True).astype(jnp.int32)
    slot_ref[0, 0:1, :] = i1 * cap + r1
    slot_ref[0, 1:2, :] = i2 * cap + r2
    carry_scr[...] = carry_scr[...] + jnp.sum(oh, axis=1, keepdims=True)
    cnt_ref[...] = carry_scr[...]


def _route_slots(lgT, b_r, cnt0, cap):
    B, E, L = lgT.shape
    tm = min(ROUTE_TM, L)
    return pl.pallas_call(
        functools.partial(_route_kernel, cap=cap),
        out_shape=(jax.ShapeDtypeStruct((B, 2, L), jnp.int32),
                   jax.ShapeDtypeStruct((B, 2, L), F32),
                   jax.ShapeDtypeStruct((E, LANES), F32)),
        grid=(B, L // tm),
        in_specs=[pl.BlockSpec((1, E, tm), lambda b, i: (b, 0, i)),
                  pl.BlockSpec((E, 1), lambda b, i: (0, 0)),
                  pl.BlockSpec((E, LANES), lambda b, i: (0, 0))],
        out_specs=(pl.BlockSpec((1, 2, tm), lambda b, i: (b, 0, i)),
                   pl.BlockSpec((1, 2, tm), lambda b, i: (b, 0, i)),
                   pl.BlockSpec((E, LANES), lambda b, i: (0, 0))),
        scratch_shapes=[pltpu.VMEM((E, LANES), F32)],
        compiler_params=_cp(("arbitrary", "arbitrary")),
        name="moe_route",
    )(lgT, b_r.reshape(E, 1), cnt0)


def _row_copy(src, dst, sem):
    return pltpu.make_async_copy(src, dst, sem)


def _slot_row(code, start_ref, cap):
    shift = cap.bit_length() - 1
    return start_ref[lax.shift_right_logical(code, shift)] + (code & (cap - 1))


def _zeros_kernel(o_ref):
    o_ref[...] = jnp.zeros_like(o_ref)


def _zero_rows(n_rows, width):
    return pl.pallas_call(
        _zeros_kernel,
        out_shape=jax.ShapeDtypeStruct((n_rows, width), F32),
        grid=(n_rows // MOE_SUP,),
        out_specs=pl.BlockSpec((MOE_SUP, width), lambda i: (i, 0)),
        compiler_params=_cp(("parallel",)),
        name="moe_zero_rows",
    )()


def _dispatch_kernel(start_ref, code_ref, h_ref, xs_in_ref, xs_ref, sem, *, cap):
    del xs_in_ref
    tm = h_ref.shape[1]

    def issue(r, _):
        for k in range(2):
            row = _slot_row(code_ref[0, k, r], start_ref, cap)
            _row_copy(h_ref.at[0, pl.ds(r, 1), :], xs_ref.at[pl.ds(row, 1), :], sem).start()
        return 0

    lax.fori_loop(0, tm, issue, 0, unroll=8)
    for k in range(2):
        _row_copy(h_ref.at[0], xs_ref.at[pl.ds(0, tm), :], sem).wait()


def _dispatch(start, codes, h, xs, cap):
    B, L, D = h.shape
    n_rows = xs.shape[0]
    tm = min(ROW_DMA_TM, L)
    return pl.pallas_call(
        functools.partial(_dispatch_kernel, cap=cap),
        out_shape=jax.ShapeDtypeStruct((n_rows, D), F32),
        grid_spec=pltpu.PrefetchScalarGridSpec(
            num_scalar_prefetch=1,
            grid=(B, L // tm),
            in_specs=[pl.BlockSpec((1, 2, tm), lambda b, i, st: (b, 0, i), memory_space=pltpu.SMEM),
                      pl.BlockSpec((1, tm, D), lambda b, i, st: (b, i, 0)),
                      pl.BlockSpec(memory_space=pl.ANY)],
            out_specs=pl.BlockSpec(memory_space=pl.ANY),
            scratch_shapes=[pltpu.SemaphoreType.DMA(())],
        ),
        input_output_aliases={3: 0},
        compiler_params=_cp(("arbitrary", "arbitrary")),
        name="moe_dispatch",
    )(start, codes, h, xs)


def _moe_grp_kernel(ge_ref, gn_ref, x_ref, wg_ref, wu_ref, wd_ref, y_ref):
    g = pl.program_id(0)
    j = pl.program_id(1)
    nsub = gn_ref[g]
    wg = wg_ref[0].astype(BF16)
    wu = wu_ref[0].astype(BF16)
    wd = wd_ref[0].astype(BF16)
    for s in range(MOE_SUP // MOE_SUB):
        rows = slice(s * MOE_SUB, (s + 1) * MOE_SUB)

        @pl.when(s < nsub)
        def _():
            xb = x_ref[rows, :].astype(BF16)
            act = _silu(_dot(xb, wg)) * _dot(xb, wu)
            part = _dot(act.astype(BF16), wd)

            @pl.when(j == 0)
            def _():
                y_ref[rows, :] = part

            @pl.when(j > 0)
            def _():
                y_ref[rows, :] = y_ref[rows, :] + part

        @pl.when((s >= nsub) & (j == 0))
        def _():
            y_ref[rows, :] = jnp.zeros((MOE_SUB, y_ref.shape[1]), F32)


def _moe_groups(counts, n_groups):
    nsup = (counts + MOE_SUP - 1) // MOE_SUP
    ends = jnp.cumsum(nsup)
    first = ends - nsup
    total = ends[-1]
    g = jnp.arange(n_groups, dtype=jnp.int32)
    gc = jnp.minimum(g, total - 1)
    e_of = jnp.minimum(jnp.sum((gc[:, None] >= ends[None, :]).astype(jnp.int32), axis=1), N_EXPERTS - 1)
    left = counts[e_of] - (gc - first[e_of]) * MOE_SUP
    nsub = jnp.clip((left + MOE_SUB - 1) // MOE_SUB, 0, MOE_SUP // MOE_SUB)
    gn = jnp.where(g < total, nsub, 0).astype(jnp.int32)
    return e_of, gn, (first * MOE_SUP).astype(jnp.int32)


def _moe_grouped(xs, ge, gn, w_gu, w_down):
    D = xs.shape[1]
    nj = D_FF // FF_TILE
    ng = xs.shape[0] // MOE_SUP
    jj = lambda j, gn, g: jnp.where(gn[g] > 0, j, nj - 1)
    return pl.pallas_call(
        _moe_grp_kernel,
        out_shape=jax.ShapeDtypeStruct(xs.shape, F32),
        grid_spec=pltpu.PrefetchScalarGridSpec(
            num_scalar_prefetch=2,
            grid=(ng, nj),
            in_specs=[pl.BlockSpec((MOE_SUP, D), lambda g, j, ge, gn: (g, 0)),
                      pl.BlockSpec((1, D, FF_TILE), lambda g, j, ge, gn: (ge[g], 0, jj(j, gn, g))),
                      pl.BlockSpec((1, D, FF_TILE), lambda g, j, ge, gn: (ge[g], 0, nj + jj(j, gn, g))),
                      pl.BlockSpec((1, FF_TILE, D), lambda g, j, ge, gn: (ge[g], jj(j, gn, g), 0))],
            out_specs=pl.BlockSpec((MOE_SUP, D), lambda g, j, ge, gn: (g, 0)),
        ),
        compiler_params=_cp(("arbitrary", "arbitrary")),
        name="moe_experts",
    )(ge, gn, xs, w_gu, w_gu, w_down)


def _combine_kernel(start_ref, code_ref, w_ref, x_ref, gt_ref, gfin_ref, ys_ref, o_ref, g_scr, sem,
                    *, cap, final):
    tm = x_ref.shape[1]

    def issue(r, _):
        for k in range(2):
            row = _slot_row(code_ref[0, k, r], start_ref, cap)
            _row_copy(ys_ref.at[pl.ds(row, 1), :], g_scr.at[k, pl.ds(r, 1), :], sem).start()
        return 0

    lax.fori_loop(0, tm, issue, 0, unroll=8)
    for k in range(2):
        _row_copy(ys_ref.at[pl.ds(0, tm), :], g_scr.at[k], sem).wait()
    w = w_ref[0]
    f = w[:, 0:1] * g_scr[0] + w[:, 1:2] * g_scr[1]
    o_ref[0] = _finish(x_ref[0] + gt_ref[0] * f, gfin_ref, final)


def _combine(start, codes, wts, x, gt, gfin, ys, *, cap, final):
    B, L, D = x.shape
    tm = min(ROW_DMA_TM, L)
    gt_spec = (pl.BlockSpec((1, 1, D), lambda b, i, st: (b, 0, 0)) if gt.shape[1] == 1
               else pl.BlockSpec((1, tm, D), lambda b, i, st: (b, i, 0)))
    row = pl.BlockSpec((1, tm, D), lambda b, i, st: (b, i, 0))
    return pl.pallas_call(
        functools.partial(_combine_kernel, cap=cap, final=final),
        out_shape=jax.ShapeDtypeStruct((B, L, D), F32),
        grid_spec=pltpu.PrefetchScalarGridSpec(
            num_scalar_prefetch=1,
            grid=(B, L // tm),
            in_specs=[pl.BlockSpec((1, 2, tm), lambda b, i, st: (b, 0, i), memory_space=pltpu.SMEM),
                      pl.BlockSpec((1, tm, 2), lambda b, i, st: (b, i, 0)),
                      row, gt_spec,
                      pl.BlockSpec((1, D), lambda b, i, st: (0, 0)),
                      pl.BlockSpec(memory_space=pl.ANY)],
            out_specs=row,
            scratch_shapes=[pltpu.VMEM((2, tm, D), F32), pltpu.SemaphoreType.DMA(())],
        ),
        compiler_params=_cp(("arbitrary", "arbitrary")),
        name="moe_combine",
    )(start, codes, wts.transpose(0, 2, 1), x, gt, gfin, ys)


def _moe_routed(groups, b_r, w_gu, w_down, gfin, *, final):
    D = groups[0][1].shape[-1]
    n_tok = sum(g[1].shape[0] * g[1].shape[1] for g in groups)
    cap = 1 << (n_tok - 1).bit_length()
    n_groups = 2 * n_tok // MOE_SUP + N_EXPERTS
    cnt = jnp.zeros((N_EXPERTS, LANES), F32)
    routed = []
    for _, _, _, lgT in groups:
        codes, wts, cnt = _route_slots(lgT, b_r, cnt, cap)
        routed.append((codes, wts))
    ge, gn, start = _moe_groups(cnt[:, 0].astype(jnp.int32), n_groups)
    xs = _zero_rows(n_groups * MOE_SUP, D)
    for (h, _, _, _), (codes, _) in zip(groups, routed):
        xs = _dispatch(start, codes, h, xs, cap)
    ys = _moe_grouped(xs, ge, gn, w_gu, w_down)
    return [_combine(start, codes, wts, x, gt, gfin, ys, cap=cap, final=final)
            for (_, x, gt, _), (codes, wts) in zip(groups, routed)]


def _pad_lanes(v):
    return jnp.pad(v.reshape(1, -1), ((0, 0), (0, LANES - v.shape[-1])))


def _mixer_layer(x, mod, states, p, s5m, l, prev_sg, *, seq):
    B, L, D = x.shape
    hi = not seq
    s5r0, s5i0, sg0, sc0 = states
    sh_m, sc_m, gt_m, sh_f, sc_f, gt_f = [mod[l][..., i * D:(i + 1) * D] for i in range(6)]
    w_in = p['w_in'][l]
    w_gates = w_in[:, 2568:]
    w_ab = jnp.pad(w_in[:, 2560:2568], ((0, 0), (0, LANES - 8)))
    u, qkv, z, ga, gb, ab = _proj(x, p['g_mix'][l].reshape(1, D), sc_m, sh_m, w_in, w_gates, w_ab,
                                  tm=min(1024, L), hi=hi, chunked=seq)
    alog = _pad_lanes(p['gdn_a_log'][l])
    dtb = _pad_lanes(p['gdn_dt_bias'][l])
    nw = p['gdn_norm_w'][l].reshape(1, GDN_DK)
    if seq:
        yg, sfin = _s5_seq(u, s5m['be'], s5m['tp'], s5m['cpm'], s5m['pt'],
                           jnp.zeros((SLABS, B, 1, 2 * SLAB_STATE), F32), s5m['dsk'][l], l)
        sfin = sfin.reshape(SLABS, B, 2, SLAB_STATE).transpose(2, 1, 0, 3)
        sr = sfin[0].reshape(B, S5_GROUPS, S5_STATE)
        si = sfin[1].reshape(B, S5_GROUPS, S5_STATE)
        og, sg = _gdn_seq(qkv, z, ab, p['gdn_conv_w'][l], alog, dtb, nw,
                          jnp.zeros((B, GDN_CONV - 1, QKV_WIDTH), F32),
                          jnp.zeros((B, GDN_HEADS, GDN_DK, GDN_DK), F32))
        cb = qkv[:, L - (GDN_CONV - 1):, :]
    else:
        n = L
        s0 = jnp.concatenate([s5r0[l].reshape(n, SLABS, SLAB_STATE),
                              s5i0[l].reshape(n, SLABS, SLAB_STATE)], axis=-1).transpose(1, 0, 2)
        yg, s1 = _s5_step(u.reshape(SLABS, n, LANES), s5m['bst'], s5m['cpe'], s5m['a1'],
                          s0, s5m['d1'][l], l)
        yg = yg.reshape(SLABS, 1, n, LANES)
        s1 = s1.transpose(1, 0, 2)
        sr = s1[:, :, :SLAB_STATE].reshape(n, S5_GROUPS, S5_STATE)
        si = s1[:, :, SLAB_STATE:].reshape(n, S5_GROUPS, S5_STATE)
        og, sg = _gdn_step(qkv.reshape(n, QKV_WIDTH), z.reshape(n, GDN_WIDTH), ab.reshape(n, LANES),
                           p['gdn_conv_w'][l], alog, dtb, nw,
                           sc0[l].reshape(n, (GDN_CONV - 1) * QKV_WIDTH), sg0, l, prev_sg)
        og = og.reshape(1, n, GDN_WIDTH)
        cb = jnp.concatenate([sc0[l][:, 1:, :], qkv.reshape(n, 1, QKV_WIDTH)], axis=1)
    x, h, lgT = _merge(yg, og, ga, gb, x, gt_m, p['w_s5_glu'][l], p['w_gdn_out'][l], p['w_out'][l],
                       p['g_ffn'][l].reshape(1, D), sc_f, sh_f, p['w_router'][l // 2],
                       tm=min(512, L), hi=hi, chunked=seq)
    return x, h, gt_f, lgT, (sr, si, sg, cb)


def kernel(x_prompt, x_sample, c_prompt, c_sample, state_s5_re, state_s5_im, state_gdn, state_conv,
           g_mix, g_ffn, g_final, w_ada, b_ada, w_in, s5_lambda_re, s5_lambda_im, s5_log_dt,
           s5_b_re, s5_b_im, s5_c_re, s5_c_im, s5_d, w_s5_glu, gdn_conv_w, gdn_a_log, gdn_dt_bias,
           gdn_norm_w, w_gdn_out, w_out, w_ffn_gate_up, w_ffn_down, w_router, b_router,
           w_exp_gate_up, w_exp_down):
    p = dict(g_mix=g_mix, g_ffn=g_ffn, w_in=w_in, w_s5_glu=w_s5_glu,
             gdn_conv_w=gdn_conv_w, gdn_a_log=gdn_a_log, gdn_dt_bias=gdn_dt_bias,
             gdn_norm_w=gdn_norm_w, w_gdn_out=w_gdn_out, w_out=w_out, w_router=w_router)
    nbp, L, D = x_prompt.shape
    nbs = x_sample.shape[0]

    mod = _ada(jnp.concatenate([c_prompt, c_sample], axis=0), w_ada, b_ada)
    mod_p = mod[:, :nbp].reshape(DEPTH, nbp, 1, 6 * D)
    mod_s = mod[:, nbp:].reshape(DEPTH, 1, nbs, 6 * D)

    seg = L // S5_T // 8
    be, bst, cpe, cpm, pt, a1 = _s5_prep(s5_lambda_re, s5_lambda_im, s5_log_dt, s5_b_re, s5_b_im,
                                         s5_c_re, s5_c_im, seg)
    d1 = [s5_d[l].reshape(SLABS, 1, LANES) for l in range(DEPTH)]
    s5m = dict(be=be, bst=bst, cpe=cpe, cpm=cpm, pt=pt, a1=a1, tp=_toep(bst, cpe), d1=d1,
               dsk=[jnp.tile(d, (1, 1, S5_T)) for d in d1])

    xs_ = [x_prompt, x_sample.reshape(1, nbs, D)]
    mods = [mod_p, mod_s]
    states = [(None, None, None, None), (state_s5_re, state_s5_im, state_gdn, state_conv)]
    outs = [[], []]
    gfin = g_final.reshape(1, D)
    for l in range(DEPTH):
        final = l == DEPTH - 1
        mixed = []
        for gi, seq in enumerate((True, False)):
            prev_sg = outs[gi][0][2] if (not seq and final and DEPTH == 2) else None
            x, h, gt_f, lgT, st = _mixer_layer(xs_[gi], mods[gi], states[gi], p, s5m, l, prev_sg, seq=seq)
            outs[gi].append(st)
            mixed.append((h, x, gt_f, lgT))
        if l % 2 == 0:
            xs_ = [_ffn(h, x, gt_f, w_ffn_gate_up[l // 2], w_ffn_down[l // 2], gfin,
                        tm=min(1024, x.shape[1]), hi=(gi == 1), final=final)
                   for gi, (h, x, gt_f, _) in enumerate(mixed)]
        else:
            xs_ = _moe_routed(mixed, b_router[l // 2], w_exp_gate_up[l // 2], w_exp_down[l // 2], gfin,
                              final=final)
    y_p, y_s = xs_
    st_p = [jnp.stack([o[i] for o in outs[0]]) for i in range(4)]
    st_s = [outs[1][-1][2] if (i == 2 and DEPTH == 2) else jnp.stack([o[i] for o in outs[1]])
            for i in range(4)]
    return (y_p, y_s.reshape(nbs, 1, D), st_p[0], st_p[1], st_p[2], st_p[3],
            st_s[0], st_s[1], st_s[2], st_s[3])
```

```python
import functools

import jax
import jax.numpy as jnp
from jax import lax
from jax.experimental import pallas as pl
from jax.experimental.pallas import tpu as pltpu

F32 = jnp.float32
BF16 = jnp.bfloat16
HI = lax.Precision.HIGHEST

D_MODEL = 1024
DEPTH = 2
S5_WIDTH = 512
S5_GROUP = 16
S5_GROUPS = 32
S5_STATE = 64
GDN_HEADS = 4
GDN_DK = 128
GDN_WIDTH = 512
GDN_CONV = 4
QKV_WIDTH = 1536
D_FF = 3584
N_EXPERTS = 8
NORM_EPS = 1e-6
L2_EPS = 1e-6

LANES = 128
SLABS = S5_WIDTH // LANES
SLAB_STATE = (S5_GROUPS // SLABS) * S5_STATE
S5_T = 8
GDN_C = 128
VMEM_LIMIT = 56 * 1024 * 1024


def _cp(sem, vmem=VMEM_LIMIT):
    return pltpu.CompilerParams(dimension_semantics=sem, vmem_limit_bytes=vmem)


def _dot(a, b, prec=None):
    return jnp.dot(a, b, precision=prec, preferred_element_type=F32)


def _dotb(a, b):
    return jnp.dot(a.astype(BF16), b.astype(BF16), preferred_element_type=F32)


def _dot_nt(a, b, prec=None):
    return lax.dot_general(a, b, (((1,), (1,)), ((), ())), precision=prec,
                           preferred_element_type=F32)


def _dot_tn(a, b, prec=None):
    return lax.dot_general(a, b, (((0,), (0,)), ((), ())), precision=prec,
                           preferred_element_type=F32)


def _silu(x):
    return x * jax.nn.sigmoid(x)


def _ada_kernel(c_ref, w_ref, b_ref, o_ref):
    cs = _silu(c_ref[...])
    o_ref[0] = _dot(cs, w_ref[0], HI) + b_ref[0]


def _ada(c_all, w_ada, b_ada):
    n = c_all.shape[0]
    tn = 1536
    return pl.pallas_call(
        _ada_kernel,
        out_shape=jax.ShapeDtypeStruct((DEPTH, n, 6 * D_MODEL), F32),
        grid=(DEPTH, 6 * D_MODEL // tn),
        in_specs=[pl.BlockSpec((n, D_MODEL), lambda l, j: (0, 0)),
                  pl.BlockSpec((1, D_MODEL, tn), lambda l, j: (l, 0, j)),
                  pl.BlockSpec((1, 1, tn), lambda l, j: (l, 0, j))],
        out_specs=pl.BlockSpec((1, n, tn), lambda l, j: (l, 0, j)),
        compiler_params=_cp(("parallel", "parallel")),
        name="ada_mod",
    )(c_all, w_ada, b_ada.reshape(DEPTH, 1, 6 * D_MODEL))


def _proj_kernel(x_ref, g_ref, sc_ref, sh_ref, w_ref, wg_ref, wab_ref,
                 u_ref, qkv_ref, z_ref, ga_ref, gb_ref, ab_ref, h_scr, *us_scr, hi):
    j = pl.program_id(2)

    @pl.when(j == 0)
    def _():
        x = x_ref[0]
        ms = jnp.mean(x * x, axis=-1, keepdims=True)
        xn = x * lax.rsqrt(ms + NORM_EPS) * g_ref[...]
        h_scr[...] = (xn * (1.0 + sc_ref[0]) + sh_ref[0]).astype(h_scr.dtype)

    def mm(w):
        if hi:
            return _dot(h_scr[...], w, HI)
        return _dot(h_scr[...], w.astype(BF16))

    @pl.when(j == 0)
    def _():
        res = mm(w_ref[...])
        for k in range(SLABS):
            if not us_scr:
                u_ref[k, 0] = res[:, k * LANES:(k + 1) * LANES]
                continue
            us_scr[0][...] = res[:, k * LANES:(k + 1) * LANES]
            nrow = res.shape[0] // S5_T
            for t in range(S5_T):
                u_ref[k, 0, :, t * LANES:(t + 1) * LANES] = us_scr[0][pl.ds(t, nrow, stride=S5_T), :]

    @pl.when((j >= 1) & (j <= 3))
    def _():
        qkv_ref[0] = mm(w_ref[...])

    @pl.when(j == 4)
    def _():
        z_ref[0] = mm(w_ref[...])

    @pl.when((j == 5) | (j == 6))
    def _():
        ga_ref[0] = jax.nn.sigmoid(mm(wg_ref[...]))

    @pl.when((j == 7) | (j == 8))
    def _():
        gb_ref[0] = jax.nn.sigmoid(mm(wg_ref[...]))

    @pl.when(j == 9)
    def _():
        ab_ref[0] = mm(wab_ref[...])


def _proj(x, g, sc, sh, w_in, w_gates, w_ab, *, tm, hi, chunked):
    B, L, D = x.shape
    lm = sc.shape[1]
    tmm = 1 if lm == 1 else tm
    mod_map = (lambda b, i, j: (b, 0, 0)) if lm == 1 else (lambda b, i, j: (b, i, 0))
    tn = 512
    clampi = lambda j, lo, n: jnp.clip(j - lo, 0, n - 1)
    if chunked:
        u_shape = jax.ShapeDtypeStruct((SLABS, B, L // S5_T, S5_T * LANES), F32)
        u_spec = pl.BlockSpec((SLABS, 1, tm // S5_T, S5_T * LANES), lambda b, i, j: (0, b, i, 0))
    else:
        u_shape = jax.ShapeDtypeStruct((SLABS, B, L, LANES), F32)
        u_spec = pl.BlockSpec((SLABS, 1, tm, LANES), lambda b, i, j: (0, b, i, 0))
    outs = pl.pallas_call(
        functools.partial(_proj_kernel, hi=hi),
        out_shape=(u_shape,
                   jax.ShapeDtypeStruct((B, L, QKV_WIDTH), F32),
                   jax.ShapeDtypeStruct((B, L, GDN_WIDTH), F32),
                   jax.ShapeDtypeStruct((B, L, D), F32),
                   jax.ShapeDtypeStruct((B, L, D), F32),
                   jax.ShapeDtypeStruct((B, L, LANES), F32)),
        grid=(B, L // tm, 10),
        in_specs=[pl.BlockSpec((1, tm, D), lambda b, i, j: (b, i, 0)),
                  pl.BlockSpec((1, D), lambda b, i, j: (0, 0)),
                  pl.BlockSpec((1, tmm, D), mod_map),
                  pl.BlockSpec((1, tmm, D), mod_map),
                  pl.BlockSpec((D, tn), lambda b, i, j: (0, jnp.minimum(j, 4))),
                  pl.BlockSpec((D, tn), lambda b, i, j: (0, clampi(j, 5, 4))),
                  pl.BlockSpec((D, LANES), lambda b, i, j: (0, 0))],
        out_specs=(u_spec,
                   pl.BlockSpec((1, tm, tn), lambda b, i, j: (b, i, clampi(j, 1, 3))),
                   pl.BlockSpec((1, tm, tn), lambda b, i, j: (b, i, 0)),
                   pl.BlockSpec((1, tm, tn), lambda b, i, j: (b, i, clampi(j, 5, 2))),
                   pl.BlockSpec((1, tm, tn), lambda b, i, j: (b, i, clampi(j, 7, 2))),
                   pl.BlockSpec((1, tm, LANES), lambda b, i, j: (b, i, 0))),
        scratch_shapes=[pltpu.VMEM((tm, D), F32 if hi else BF16)]
        + ([pltpu.VMEM((tm, LANES), F32)] if chunked else []),
        compiler_params=_cp(("parallel", "parallel", "arbitrary")),
        name="norm_in_proj",
    )(x, g, sc, sh, w_in, w_gates, w_ab)
    return outs


GROUPS_PER_SLAB = S5_GROUPS // SLABS


def _s5_prep_kernel(lrb, lib, dtb, bre, bim, lrc, lic, dtc, cre, cim, lrn, lin, dtn,
                    be_ref, bst_ref, cpe_ref, cpm_ref, pt_ref, a1_ref, *, seg):
    W = SLAB_STATE

    def disc(lr, li, ldt):
        dt = jnp.exp(ldt)
        mag = jnp.exp(lr * dt)
        return mag * jnp.cos(li * dt), mag * jnp.sin(li * dt)

    def cmul(xr, xi, yr, yi):
        return xr * yr - xi * yi, xr * yi + xi * yr

    lr, li = lrb[0], lib[0]
    ar, ai = disc(lr, li, dtb[0])
    den = lr * lr + li * li
    nr = ar - 1.0
    kr = (nr * lr + ai * li) / den
    ki = (ai * lr - nr * li) / den
    br, bi = bre[0], bim[0]
    bbr = kr * br - ki * bi
    bbi = kr * bi + ki * br
    rgrp = lax.broadcasted_iota(jnp.int32, (LANES, LANES), 0) // S5_GROUP
    lane_hi = lax.broadcasted_iota(jnp.int32, (LANES, LANES), 1) // S5_STATE
    pr, pi = jnp.ones_like(ar), jnp.zeros_like(ar)
    for d in range(S5_T):
        t = S5_T - 1 - d
        for ri, val in enumerate(cmul(pr, pi, bbr, bbi)):
            two = jnp.concatenate([val, val], axis=1)
            for m in range(GROUPS_PER_SLAB // 2):
                tile = jnp.where(rgrp == 2 * m + lane_hi, two, 0.0)
                c0 = ri * W + m * LANES
                be_ref[0, 0, t * LANES:(t + 1) * LANES, c0:c0 + LANES] = tile.astype(BF16)
                if d == 0:
                    bst_ref[0, 0, :, c0:c0 + LANES] = tile
        pr, pi = cmul(pr, pi, ar, ai)

    ar, ai = disc(lrc[0], lic[0], dtc[0])
    cr, ci = cre[0], cim[0]
    own = (lax.broadcasted_iota(jnp.int32, (W, LANES), 0) // S5_STATE
           == lax.broadcasted_iota(jnp.int32, (W, LANES), 1) // S5_GROUP)
    pr, pi = jnp.ones_like(ar), jnp.zeros_like(ar)
    for d in range(S5_T + 1):
        vr, vi = cmul(cr, ci, pr, pi)
        for ri, val in enumerate((vr, -vi)):
            tile = jnp.where(own, val, 0.0)
            cpe_ref[0, 0, d, ri * W:(ri + 1) * W, :] = tile
            if d >= 1:
                cpm_ref[0, 0, ri * W:(ri + 1) * W, (d - 1) * LANES:d * LANES] = tile.astype(BF16)
        pr, pi = cmul(pr, pi, ar, ai)

    ar, ai = disc(lrn[0, 0], lin[0, 0], dtn[0, 0])
    a1_ref[0, 0, :, 0:W] = ar
    a1_ref[0, 0, :, W:2 * W] = ai
    tr, ti = ar, ai
    for _ in range(S5_T - 1):
        tr, ti = cmul(tr, ti, ar, ai)
    pr, pi = jnp.ones_like(ar), jnp.zeros_like(ar)
    for i in range(seg + 1):
        pt_ref[0, 0, i:i + 1, 0:W] = pr
        pt_ref[0, 0, i:i + 1, W:2 * W] = pi
        pr, pi = cmul(pr, pi, tr, ti)


def _s5_prep(lam_re, lam_im, log_dt, b_re, b_im, c_re, c_im, seg):
    G, P, C = S5_GROUPS, S5_STATE, S5_GROUP
    W2 = 2 * SLAB_STATE
    dt3 = jnp.broadcast_to(log_dt[:, :, None], (DEPTH, G, P))
    rows_b = lambda a: jnp.repeat(a, C, axis=1)
    bt = lambda a: a.transpose(0, 1, 3, 2).reshape(DEPTH, G * C, P)
    rows_c = lambda a: jnp.broadcast_to(a.reshape(DEPTH, G * P, 1), (DEPTH, G * P, LANES))
    ct = lambda a: jnp.tile(a.transpose(0, 1, 3, 2).reshape(DEPTH, G * P, C), (1, 1, LANES // C))
    nat = lambda a: a.reshape(DEPTH, SLABS, 1, SLAB_STATE)
    args = (rows_b(lam_re), rows_b(lam_im), rows_b(dt3), bt(b_re), bt(b_im),
            rows_c(lam_re), rows_c(lam_im), rows_c(dt3), ct(c_re), ct(c_im),
            nat(lam_re), nat(lam_im), nat(dt3))
    bspec = pl.BlockSpec((1, LANES, P), lambda l, k: (l, k, 0))
    cspec = pl.BlockSpec((1, SLAB_STATE, LANES), lambda l, k: (l, k, 0))
    nspec = pl.BlockSpec((1, 1, 1, SLAB_STATE), lambda l, k: (l, k, 0, 0))
    return pl.pallas_call(
        functools.partial(_s5_prep_kernel, seg=seg),
        out_shape=(jax.ShapeDtypeStruct((DEPTH, SLABS, S5_T * LANES, W2), BF16),
                   jax.ShapeDtypeStruct((DEPTH, SLABS, LANES, W2), F32),
                   jax.ShapeDtypeStruct((DEPTH, SLABS, S5_T + 1, W2, LANES), F32),
                   jax.ShapeDtypeStruct((DEPTH, SLABS, W2, S5_T * LANES), BF16),
                   jax.ShapeDtypeStruct((DEPTH, SLABS, seg + 1, W2), F32),
                   jax.ShapeDtypeStruct((DEPTH, SLABS, 1, W2), F32)),
        grid=(DEPTH, SLABS),
        in_specs=[bspec] * 5 + [cspec] * 5 + [nspec] * 3,
        out_specs=(pl.BlockSpec((1, 1, S5_T * LANES, W2), lambda l, k: (l, k, 0, 0)),
                   pl.BlockSpec((1, 1, LANES, W2), lambda l, k: (l, k, 0, 0)),
                   pl.BlockSpec((1, 1, S5_T + 1, W2, LANES), lambda l, k: (l, k, 0, 0, 0)),
                   pl.BlockSpec((1, 1, W2, S5_T * LANES), lambda l, k: (l, k, 0, 0)),
                   pl.BlockSpec((1, 1, seg + 1, W2), lambda l, k: (l, k, 0, 0)),
                   pl.BlockSpec((1, 1, 1, W2), lambda l, k: (l, k, 0, 0))),
        compiler_params=_cp(("parallel", "parallel")),
        name="s5_discretize",
    )(*args)


def _toep_kernel(b_ref, c_ref, o_ref):
    dd = pl.program_id(2)
    bst = b_ref[0, 0]
    lag = lambda d: _dot(bst, c_ref[0, 0, d], HI)
    k0 = lag(2 * dd)
    o_ref[0, 0, 0, 0:LANES, 0:LANES] = k0.astype(BF16)
    o_ref[0, 0, 0, LANES:, LANES:] = k0.astype(BF16)
    o_ref[0, 0, 0, 0:LANES, LANES:] = lag(2 * dd + 1).astype(BF16)
    km = lag(jnp.maximum(2 * dd - 1, 0))
    o_ref[0, 0, 0, LANES:, 0:LANES] = jnp.where(dd > 0, km, 0.0).astype(BF16)


def _toep(bst, cpe):
    W2 = 2 * SLAB_STATE
    return pl.pallas_call(
        _toep_kernel,
        out_shape=jax.ShapeDtypeStruct((DEPTH, SLABS, S5_T // 2, 2 * LANES, 2 * LANES), BF16),
        grid=(DEPTH, SLABS, S5_T // 2),
        in_specs=[pl.BlockSpec((1, 1, LANES, W2), lambda l, k, d: (l, k, 0, 0)),
                  pl.BlockSpec((1, 1, S5_T + 1, W2, LANES), lambda l, k, d: (l, k, 0, 0, 0))],
        out_specs=pl.BlockSpec((1, 1, 1, 2 * LANES, 2 * LANES), lambda l, k, d: (l, k, d, 0, 0)),
        compiler_params=_cp(("parallel", "parallel", "parallel")),
        name="s5_conv_blocks",
    )(bst, cpe)


def _s5_seq_kernel(up_ref, be_ref, tp_ref, cpm_ref, pt_ref, s0_ref, dsk_ref,
                   yg_ref, sfin_ref, e_scr, sx_scr, *, nc):
    seg = nc // 8
    W = SLAB_STATE
    nt = W // LANES
    u = up_ref[0, 0]
    ub = u.astype(BF16)
    e = _dot(ub, be_ref[0, 0])
    for c in range(2 * nt):
        e_scr[c] = e[:, c * LANES:(c + 1) * LANES]

    def tiles(row):
        return [(row[:, c * LANES:(c + 1) * LANES], row[:, W + c * LANES:W + (c + 1) * LANES])
                for c in range(nt)]

    a8 = [(jnp.broadcast_to(r, (8, LANES)), jnp.broadcast_to(i, (8, LANES)))
          for r, i in tiles(pt_ref[0, 0, 1:2, :])]

    def step(i, carry):
        rows = pl.ds(i, 8, stride=seg)
        new = []
        for c in range(nt):
            sr, si = carry[c]
            ar, ai = a8[c]
            sx_scr[c, rows, :] = sr
            sx_scr[nt + c, rows, :] = si
            new.append((ar * sr - ai * si + e_scr[c, rows, :],
                        ar * si + ai * sr + e_scr[nt + c, rows, :]))
        return tuple(new)

    zero = jnp.zeros((8, LANES), F32)
    ends = lax.fori_loop(0, seg, step, tuple((zero, zero) for _ in range(nt)))

    al = tiles(pt_ref[0, 0, seg:seg + 1, :])
    cur = tiles(s0_ref[0, 0])
    car = []
    for c in range(nt):
        alr, ali = al[c]
        cr, ci = cur[c]
        sr, si = ends[c]
        crs, cis = [], []
        for j in range(8):
            crs.append(cr)
            cis.append(ci)
            cr, ci = (alr * cr - ali * ci + sr[j:j + 1], alr * ci + ali * cr + si[j:j + 1])
        sfin_ref[0, 0, :, c * LANES:(c + 1) * LANES] = cr
        sfin_ref[0, 0, :, W + c * LANES:W + (c + 1) * LANES] = ci
        car.append((jnp.concatenate(crs, axis=0), jnp.concatenate(cis, axis=0)))

    def corr(i, _):
        rows = pl.ds(i, 8, stride=seg)
        pw = tiles(pt_ref[0, 0, pl.ds(i, 1), :])
        for c in range(nt):
            pr, pi = pw[c]
            cr, ci = car[c]
            sx_scr[c, rows, :] = sx_scr[c, rows, :] + (pr * cr - pi * ci)
            sx_scr[nt + c, rows, :] = sx_scr[nt + c, rows, :] + (pr * ci + pi * cr)
        return 0

    lax.fori_loop(0, seg, corr, 0)

    sx = jnp.concatenate([sx_scr[c] for c in range(2 * nt)], axis=-1)
    y = _dot(sx.astype(BF16), cpm_ref[0, 0])
    TW = 2 * LANES
    for tq in range(S5_T // 2):
        acc = y[:, tq * TW:(tq + 1) * TW]
        for tpi in range(tq + 1):
            acc = acc + _dot(ub[:, tpi * TW:(tpi + 1) * TW], tp_ref[0, 0, tq - tpi])
        acc = acc + dsk_ref[0, :, tq * TW:(tq + 1) * TW] * u[:, tq * TW:(tq + 1) * TW]
        yg_ref[0, 0, :, tq * TW:(tq + 1) * TW] = jax.nn.gelu(acc).astype(yg_ref.dtype)


def _s5_seq(up, be_emb, tp, cpm, pt, s0, dsk, l):
    _, B, nc, _ = up.shape
    seg = nc // 8
    W2 = 2 * SLAB_STATE
    yg, sfin = pl.pallas_call(
        functools.partial(_s5_seq_kernel, nc=nc),
        out_shape=(jax.ShapeDtypeStruct((SLABS, B, nc, S5_T * LANES), BF16),
                   jax.ShapeDtypeStruct((SLABS, B, 1, W2), F32)),
        grid=(SLABS, B),
        in_specs=[pl.BlockSpec((1, 1, nc, S5_T * LANES), lambda k, b: (k, b, 0, 0)),
                  pl.BlockSpec((1, 1, S5_T * LANES, W2), lambda k, b: (l, k, 0, 0)),
                  pl.BlockSpec((1, 1, S5_T // 2, 2 * LANES, 2 * LANES), lambda k, b: (l, k, 0, 0, 0)),
                  pl.BlockSpec((1, 1, W2, S5_T * LANES), lambda k, b: (l, k, 0, 0)),
                  pl.BlockSpec((1, 1, seg + 1, W2), lambda k, b: (l, k, 0, 0)),
                  pl.BlockSpec((1, 1, 1, W2), lambda k, b: (k, b, 0, 0)),
                  pl.BlockSpec((1, 1, S5_T * LANES), lambda k, b: (k, 0, 0))],
        out_specs=(pl.BlockSpec((1, 1, nc, S5_T * LANES), lambda k, b: (k, b, 0, 0)),
                   pl.BlockSpec((1, 1, 1, W2), lambda k, b: (k, b, 0, 0))),
        scratch_shapes=[pltpu.VMEM((W2 // LANES, nc, LANES), F32),
                        pltpu.VMEM((W2 // LANES, nc, LANES), F32)],
        compiler_params=_cp(("parallel", "parallel")),
        name="s5_seq",
    )(up, be_emb, tp, cpm, pt, s0, dsk)
    return yg, sfin


def _s5_step_kernel(u_ref, b_ref, c_ref, a_ref, s0_ref, d_ref, yg_ref, s1_ref):
    W = SLAB_STATE
    u = u_ref[0]
    bu = _dot(u, b_ref[0, 0], HI)
    ar = a_ref[0, 0, :, 0:W]
    ai = a_ref[0, 0, :, W:2 * W]
    sr = s0_ref[0, :, 0:W]
    si = s0_ref[0, :, W:2 * W]
    nr = ar * sr - ai * si + bu[:, 0:W]
    ni = ar * si + ai * sr + bu[:, W:2 * W]
    s1_ref[0, :, 0:W] = nr
    s1_ref[0, :, W:2 * W] = ni
    s1 = jnp.concatenate([nr, ni], axis=-1)
    y = _dot(s1, c_ref[0, 0, 0], HI) + d_ref[0] * u
    yg_ref[0] = jax.nn.gelu(y)


def _s5_step(u_slab, bst, cpe, a1, s0, d1, l):
    _, N, _ = u_slab.shape
    W2 = 2 * SLAB_STATE
    return pl.pallas_call(
        _s5_step_kernel,
        out_shape=(jax.ShapeDtypeStruct((SLABS, N, LANES), F32),
                   jax.ShapeDtypeStruct((SLABS, N, W2), F32)),
        grid=(SLABS,),
        in_specs=[pl.BlockSpec((1, N, LANES), lambda k: (k, 0, 0)),
                  pl.BlockSpec((1, 1, LANES, W2), lambda k: (l, k, 0, 0)),
                  pl.BlockSpec((1, 1, 1, W2, LANES), lambda k: (l, k, 0, 0, 0)),
                  pl.BlockSpec((1, 1, 1, W2), lambda k: (l, k, 0, 0)),
                  pl.BlockSpec((1, N, W2), lambda k: (k, 0, 0)),
                  pl.BlockSpec((1, 1, LANES), lambda k: (k, 0, 0))],
        out_specs=(pl.BlockSpec((1, N, LANES), lambda k: (k, 0, 0)),
                   pl.BlockSpec((1, N, W2), lambda k: (k, 0, 0))),
        compiler_params=_cp(("parallel",)),
        name="s5_step",
    )(u_slab, bst, cpe, a1, s0, d1)


def _l2n(x):
    return x * lax.rsqrt(jnp.sum(x * x, axis=-1, keepdims=True) + L2_EPS)


def _split_bf16(x):
    hi = x.astype(BF16)
    return hi, (x - hi.astype(F32)).astype(BF16)


def _unit_lower_solve(As, rhss):
    n = GDN_C
    row = lax.broadcasted_iota(jnp.int32, (n, n), 0)
    col = lax.broadcasted_iota(jnp.int32, (n, n), 1)
    eye = (row == col).astype(F32)
    same8 = (row // 8) == (col // 8)
    Qs = [jnp.where(same8, -A, 0.0) for A in As]
    invs = [eye + Q for Q in Qs]
    for _ in range(2):
        Qs = [_dotb(Q, Q) for Q in Qs]
        invs = [inv + _dotb(inv, Q) for inv, Q in zip(invs, Qs)]
    s = 8
    while s < n:
        sib = ((row // (2 * s)) == (col // (2 * s))) & ((row // s) != (col // s))
        offs = [jnp.where(sib, A, 0.0).astype(BF16) for A in As]
        invb = [inv.astype(BF16) for inv in invs]
        tmp = [_dot(off, ib) for off, ib in zip(offs, invb)]
        invs = [inv - _dot(ib, t.astype(BF16)) for inv, ib, t in zip(invs, invb, tmp)]
        s *= 2
    invb = [inv.astype(BF16) for inv in invs]
    x0s = [_dot(ib, rhs.astype(BF16)) for ib, rhs in zip(invb, rhss)]
    res = []
    for A, x0, rhs in zip(As, x0s, rhss):
        ah, al = _split_bf16(A)
        xh, xl = _split_bf16(x0)
        res.append(rhs - x0 - (_dot(ah, xh) + _dot(ah, xl) + _dot(al, xh)))
    return [x0 + _dot(ib, r.astype(BF16)) for x0, ib, r in zip(x0s, invb, res)]


def _gdn_tile(qc_scr, gc, beta, z_ref, nw, o_ref, s_scr, tl):
    C, DK, H = GDN_C, GDN_DK, GDN_HEADS
    nchunk = tl // C
    probs = [(c, h) for c in range(nchunk) for h in range(H)]
    row = lax.broadcasted_iota(jnp.int32, (C, C), 0)
    col = lax.broadcasted_iota(jnp.int32, (C, C), 1)
    tri = row >= col
    strict = row > col

    def blk(c, off):
        return qc_scr[c * C:(c + 1) * C, off:off + DK]

    q = [_l2n(blk(c, h * DK)) * (DK ** -0.5) for c, h in probs]
    k = [_l2n(blk(c, GDN_WIDTH + h * DK)) for c, h in probs]
    v = [blk(c, 2 * GDN_WIDTH + h * DK) for c, h in probs]
    gcb = [jnp.broadcast_to(gc[c * C:(c + 1) * C, h:h + 1], (C, DK)) for c, h in probs]
    bb = [jnp.broadcast_to(beta[c * C:(c + 1) * C, H + h:H + h + 1], (C, DK)) for c, h in probs]
    decay = []
    for g in gcb:
        diff = g - g.T
        decay.append(jnp.where(tri, jnp.exp(jnp.where(tri, diff, 0.0)), 0.0))
    kbf = [x.astype(BF16) for x in k]
    kb = [x * b for x, b in zip(k, bb)]
    A = [jnp.where(strict, _dot_nt(x.astype(BF16), y) * d, 0.0) for x, y, d in zip(kb, kbf, decay)]
    egc = [jnp.exp(g) for g in gcb]
    rhs = [jnp.concatenate([x * b, y * e], axis=-1) for x, b, y, e in zip(v, bb, kb, egc)]
    sol = _unit_lower_solve(A, rhs)
    attn = [jnp.where(tri, _dot_nt(x.astype(BF16), y) * d, 0.0).astype(BF16)
            for x, y, d in zip(q, kbf, decay)]
    glast = [g[C - 1:C, :] for g in gcb]
    wq = [jnp.concatenate([s[:, DK:], x * e], axis=0).astype(BF16) for s, x, e in zip(sol, q, egc)]
    kg = [(x * jnp.exp(gl - g)).astype(BF16) for x, gl, g in zip(k, glast, gcb)]

    for c in range(nchunk):
        ps = [c * H + h for h in range(H)]
        S = [s_scr[h] for h in range(H)]
        ws = [_dot(wq[p], S[h].astype(BF16)) for h, p in enumerate(ps)]
        v_new = [sol[p][:, 0:DK] - w[0:C] for p, w in zip(ps, ws)]
        vb = [x.astype(BF16) for x in v_new]
        o = [w[C:] + _dot(attn[p], x) for p, w, x in zip(ps, ws, vb)]
        for h, p in enumerate(ps):
            s_scr[h] = S[h] * jnp.exp(glast[p]) + _dot_tn(kg[p], vb[h])
            zh = z_ref[0, c * C:(c + 1) * C, h * DK:(h + 1) * DK]
            on = o[h] * lax.rsqrt(jnp.mean(o[h] * o[h], axis=-1, keepdims=True) + NORM_EPS) * nw
            o_ref[0, c * C:(c + 1) * C, h * DK:(h + 1) * DK] = (on * _silu(zh)).astype(o_ref.dtype)


def _gdn_seq_kernel(qkv_ref, z_ref, ab_ref, cw_ref, alog_ref, dtb_ref, nw_ref, conv0_ref, s0_ref,
                    o_ref, sfin_ref, xp_scr, qc_scr, s_scr, *, tl):
    lt = pl.program_id(1)

    @pl.when(lt == 0)
    def _():
        xp_scr[0:8, :] = jnp.zeros((8, QKV_WIDTH), F32)
        xp_scr[8 - (GDN_CONV - 1):8, :] = conv0_ref[0]
        s_scr[...] = s0_ref[0]

    xp_scr[8:8 + tl, :] = qkv_ref[0]
    conv = cw_ref[0:1, :] * xp_scr[5:5 + tl, :]
    for j in range(1, GDN_CONV):
        conv = conv + cw_ref[j:j + 1, :] * xp_scr[5 + j:5 + j + tl, :]
    xp_scr[0:8, :] = xp_scr[tl:tl + 8, :]
    qc_scr[...] = _silu(conv)

    ab = ab_ref[0]
    g = -jnp.exp(alog_ref[...]) * jax.nn.softplus(ab + dtb_ref[...])
    beta = jax.nn.sigmoid(ab)
    row = lax.broadcasted_iota(jnp.int32, (tl, tl), 0)
    col = lax.broadcasted_iota(jnp.int32, (tl, tl), 1)
    csum = ((row >= col) & ((row // GDN_C) == (col // GDN_C))).astype(F32)
    gc = _dot(csum, g, HI)
    _gdn_tile(qc_scr, gc, beta, z_ref, nw_ref[...], o_ref, s_scr, tl)

    @pl.when(lt == pl.num_programs(1) - 1)
    def _():
        sfin_ref[0] = s_scr[...]


def _gdn_seq(qkv, z, ab, conv_w, alog, dtb, nw, conv0, s0):
    B, L, _ = qkv.shape
    tl = min(256, L)
    return pl.pallas_call(
        functools.partial(_gdn_seq_kernel, tl=tl),
        out_shape=(jax.ShapeDtypeStruct((B, L, GDN_WIDTH), BF16),
                   jax.ShapeDtypeStruct((B, GDN_HEADS, GDN_DK, GDN_DK), F32)),
        grid=(B, L // tl),
        in_specs=[pl.BlockSpec((1, tl, QKV_WIDTH), lambda b, i: (b, i, 0)),
                  pl.BlockSpec((1, tl, GDN_WIDTH), lambda b, i: (b, i, 0)),
                  pl.BlockSpec((1, tl, LANES), lambda b, i: (b, i, 0)),
                  pl.BlockSpec((GDN_CONV, QKV_WIDTH), lambda b, i: (0, 0)),
                  pl.BlockSpec((1, LANES), lambda b, i: (0, 0)),
                  pl.BlockSpec((1, LANES), lambda b, i: (0, 0)),
                  pl.BlockSpec((1, GDN_DK), lambda b, i: (0, 0)),
                  pl.BlockSpec((1, GDN_CONV - 1, QKV_WIDTH), lambda b, i: (b, 0, 0)),
                  pl.BlockSpec((1, GDN_HEADS, GDN_DK, GDN_DK), lambda b, i: (b, 0, 0, 0))],
        out_specs=(pl.BlockSpec((1, tl, GDN_WIDTH), lambda b, i: (b, i, 0)),
                   pl.BlockSpec((1, GDN_HEADS, GDN_DK, GDN_DK), lambda b, i: (b, 0, 0, 0))),
        scratch_shapes=[pltpu.VMEM((tl + 8, QKV_WIDTH), F32),
                        pltpu.VMEM((tl, QKV_WIDTH), F32),
                        pltpu.VMEM((GDN_HEADS, GDN_DK, GDN_DK), F32)],
        compiler_params=_cp(("parallel", "arbitrary")),
        name="gdn_seq",
    )(qkv, z, ab, conv_w, alog, dtb, nw, conv0, s0)


GDN_STEP_ROWS = 8


def _gdn_step_kernel(qkv_ref, z_ref, ab_ref, cw_ref, alog_ref, dtb_ref, nw_ref, conv0_ref, s0_ref,
                     *rest):
    if len(rest) == 3:
        prev_ref, o_ref, s1_all = rest
        s1_all[0] = prev_ref[...]
        s1_ref = s1_all.at[1]
    else:
        o_ref, s1_all = rest
        s1_ref = s1_all
    nb = GDN_STEP_ROWS
    W = QKV_WIDTH
    conv = cw_ref[0:1, :] * conv0_ref[:, 0:W]
    conv = conv + cw_ref[1:2, :] * conv0_ref[:, W:2 * W]
    conv = conv + cw_ref[2:3, :] * conv0_ref[:, 2 * W:3 * W]
    conv = conv + cw_ref[3:4, :] * qkv_ref[...]
    qc = _silu(conv)
    ab = ab_ref[...]
    eg = jnp.exp(-jnp.exp(alog_ref[...]) * jax.nn.softplus(ab + dtb_ref[...]))
    beta = jax.nn.sigmoid(ab)
    eye = (lax.broadcasted_iota(jnp.int32, (GDN_DK, GDN_DK), 0)
           == lax.broadcasted_iota(jnp.int32, (GDN_DK, GDN_DK), 1)).astype(F32)
    for h in range(GDN_HEADS):
        q = _l2n(qc[:, h * GDN_DK:(h + 1) * GDN_DK]) * (GDN_DK ** -0.5)
        k = _l2n(qc[:, GDN_WIDTH + h * GDN_DK:GDN_WIDTH + (h + 1) * GDN_DK])
        v = qc[:, 2 * GDN_WIDTH + h * GDN_DK:2 * GDN_WIDTH + (h + 1) * GDN_DK]
        kT = _dot_nt(eye, k, HI)
        qT = _dot_nt(eye, q, HI)
        qk = jnp.sum(q * k, axis=-1, keepdims=True)
        for j in range(nb):
            S = s0_ref[0, j, h]
            kc = jnp.broadcast_to(kT[:, j:j + 1], (GDN_DK, GDN_DK))
            qcb = jnp.broadcast_to(qT[:, j:j + 1], (GDN_DK, GDN_DK))
            kS = jnp.sum(kc * S, axis=0, keepdims=True)
            qS = jnp.sum(qcb * S, axis=0, keepdims=True)
            egj = eg[j:j + 1, h:h + 1]
            bj = beta[j:j + 1, GDN_HEADS + h:GDN_HEADS + h + 1]
            v_new = bj * v[j:j + 1, :] - (bj * egj) * kS
            o = egj * qS + qk[j:j + 1, :] * v_new
            s1_ref[j, h] = S * egj + kc * v_new
            zh = z_ref[j:j + 1, h * GDN_DK:(h + 1) * GDN_DK]
            on = o * lax.rsqrt(jnp.mean(o * o, axis=-1, keepdims=True) + NORM_EPS) * nw_ref[...]
            o_ref[j:j + 1, h * GDN_DK:(h + 1) * GDN_DK] = on * _silu(zh)


def _gdn_step(qkv, z, ab, conv_w, alog, dtb, nw, conv0, s_all, l, prev):
    N = qkv.shape[0]
    nb = GDN_STEP_ROWS
    row = lambda w: pl.BlockSpec((nb, w), lambda i: (i, 0))
    const = lambda r, w: pl.BlockSpec((r, w), lambda i: (0, 0))
    sblk = (nb, GDN_HEADS, GDN_DK, GDN_DK)
    one = pl.BlockSpec(sblk, lambda i: (i, 0, 0, 0))
    ins = [qkv, z, ab, conv_w, alog, dtb, nw, conv0, s_all]
    in_specs = [row(QKV_WIDTH), row(GDN_WIDTH), row(LANES), const(GDN_CONV, QKV_WIDTH),
                const(1, LANES), const(1, LANES), const(1, GDN_DK), row(3 * QKV_WIDTH),
                pl.BlockSpec((1,) + sblk, lambda i: (l, i, 0, 0, 0))]
    if prev is None:
        s_shape, s_spec = jax.ShapeDtypeStruct((N,) + sblk[1:], F32), one
    else:
        assert DEPTH == 2 and l == 1
        ins.append(prev)
        in_specs.append(one)
        s_shape = jax.ShapeDtypeStruct((DEPTH, N) + sblk[1:], F32)
        s_spec = pl.BlockSpec((DEPTH,) + sblk, lambda i: (0, i, 0, 0, 0))
    return pl.pallas_call(
        _gdn_step_kernel,
        out_shape=(jax.ShapeDtypeStruct((N, GDN_WIDTH), F32), s_shape),
        grid=(N // nb,),
        in_specs=in_specs,
        out_specs=(row(GDN_WIDTH), s_spec),
        compiler_params=_cp(("parallel",)),
        name="gdn_step",
    )(*ins)


def _merge_kernel(yg_ref, og_ref, ga_ref, gb_ref, x_ref, gt_ref, wglu_ref, wgo_ref, wout_ref,
                  gf_ref, scf_ref, shf_ref, wr_ref,
                  xo_ref, h_ref, lg_ref, *scr, hi, chunked):
    if chunked:
        y_scr = scr[-1]
        scr = scr[:-1]
        nrow = y_scr.shape[1] // S5_T
        for k in range(SLABS):
            for t in range(S5_T):
                y_scr[k, pl.ds(t, nrow, stride=S5_T), :] = (
                    yg_ref[k, 0, :, t * LANES:(t + 1) * LANES].astype(F32))
        y = jnp.concatenate([y_scr[k] for k in range(SLABS)], axis=-1)
    else:
        y = jnp.concatenate([yg_ref[k, 0] for k in range(SLABS)], axis=-1)
    if hi:
        wglu, wgo, wout = wglu_ref[...], wgo_ref[...], wout_ref[...]
        mm = lambda a, w: _dot(a, w, HI)
    else:
        wglu_s, wgo_s, wout_s = scr

        @pl.when((pl.program_id(0) == 0) & (pl.program_id(1) == 0))
        def _():
            wglu_s[...] = wglu_ref[...].astype(BF16)
            wgo_s[...] = wgo_ref[...].astype(BF16)
            wout_s[...] = wout_ref[...].astype(BF16)

        wglu, wgo, wout = wglu_s[...], wgo_s[...], wout_s[...]
        mm = lambda a, w: _dot(a.astype(BF16), w)

    glu = mm(y, wglu)
    branch_a = glu[:, 0:D_MODEL] * jax.nn.sigmoid(glu[:, D_MODEL:])
    branch_b = mm(og_ref[0], wgo)
    merged = ga_ref[0] * branch_a + gb_ref[0] * branch_b
    out = mm(merged, wout)
    x = x_ref[0] + gt_ref[0] * out
    xo_ref[0] = x
    ms = jnp.mean(x * x, axis=-1, keepdims=True)
    h = x * lax.rsqrt(ms + NORM_EPS) * gf_ref[...]
    h = h * (1.0 + scf_ref[0]) + shf_ref[0]
    h_ref[0] = h.astype(h_ref.dtype)
    lg_ref[0] = _dot_nt(wr_ref[...], h, HI)


def _merge(yg, og, ga, gb, x, gt, wglu, wgo, wout, gf, scf, shf, wr, *, tm, hi, chunked, h_dtype):
    B, L, D = x.shape
    lm = gt.shape[1]
    tmm = 1 if lm == 1 else tm
    mod_map = (lambda b, i: (b, 0, 0)) if lm == 1 else (lambda b, i: (b, i, 0))
    row = lambda w: pl.BlockSpec((1, tm, w), lambda b, i: (b, i, 0))
    const = lambda r, w: pl.BlockSpec((r, w), lambda b, i: (0, 0))
    mod = pl.BlockSpec((1, tmm, D), mod_map)
    scratch = [] if hi else [pltpu.VMEM((S5_WIDTH, 2 * D), BF16), pltpu.VMEM((GDN_WIDTH, D), BF16),
                             pltpu.VMEM((D, D), BF16)]
    if chunked:
        scratch = scratch + [pltpu.VMEM((SLABS, tm, LANES), F32)]
        yg_spec = pl.BlockSpec((SLABS, 1, tm // S5_T, S5_T * LANES), lambda b, i: (0, b, i, 0))
    else:
        yg_spec = pl.BlockSpec((SLABS, 1, tm, LANES), lambda b, i: (0, b, i, 0))
    lg_shape = jax.ShapeDtypeStruct((B, N_EXPERTS, L), F32)
    lg_spec = pl.BlockSpec((1, N_EXPERTS, tm), lambda b, i: (b, 0, i))
    wr = wr.T
    return pl.pallas_call(
        functools.partial(_merge_kernel, hi=hi, chunked=chunked),
        out_shape=(jax.ShapeDtypeStruct((B, L, D), F32),
                   jax.ShapeDtypeStruct((B, L, D), h_dtype),
                   lg_shape),
        grid=(B, L // tm),
        in_specs=[yg_spec,
                  row(GDN_WIDTH), row(D), row(D), row(D), mod,
                  const(S5_WIDTH, 2 * D), const(GDN_WIDTH, D), const(D, D),
                  const(1, D), mod, mod, const(*wr.shape)],
        out_specs=(row(D), row(D), lg_spec),
        scratch_shapes=scratch,
        compiler_params=_cp(("arbitrary", "arbitrary")),
        name="merge_out_proj",
    )(yg, og, ga, gb, x, gt, wglu, wgo, wout, gf, scf, shf, wr)


FF_TILE = 512


def _finish(x, gfin_ref, final):
    if not final:
        return x
    ms = jnp.mean(x * x, axis=-1, keepdims=True)
    return x * lax.rsqrt(ms + NORM_EPS) * gfin_ref[...]


def _ffn_kernel(h_ref, x_ref, gt_ref, wg_ref, wu_ref, wd_ref, gfin_ref, o_ref, acc_scr, *, hi, final):
    j = pl.program_id(2)
    if hi:
        h = h_ref[0]
        mm = lambda a, w: _dot(a, w, HI)
    else:
        h = h_ref[0].astype(BF16)
        mm = lambda a, w: _dot(a.astype(BF16), w.astype(BF16))
    act = _silu(mm(h, wg_ref[...])) * mm(h, wu_ref[...])
    part = mm(act, wd_ref[...])

    @pl.when(j == 0)
    def _():
        acc_scr[...] = part

    @pl.when(j > 0)
    def _():
        acc_scr[...] = acc_scr[...] + part

    @pl.when(j == pl.num_programs(2) - 1)
    def _():
        o_ref[0] = _finish(x_ref[0] + gt_ref[0] * acc_scr[...], gfin_ref, final)


def _ffn(h, x, gt, w_gu, w_down, gfin, *, tm, hi, final):
    B, L, D = x.shape
    lm = gt.shape[1]
    tmm = 1 if lm == 1 else tm
    mod_map = (lambda b, i, j: (b, 0, 0)) if lm == 1 else (lambda b, i, j: (b, i, 0))
    nj = D_FF // FF_TILE
    row = pl.BlockSpec((1, tm, D), lambda b, i, j: (b, i, 0))
    return pl.pallas_call(
        functools.partial(_ffn_kernel, hi=hi, final=final),
        out_shape=jax.ShapeDtypeStruct((B, L, D), F32),
        grid=(B, L // tm, nj),
        in_specs=[row, row, pl.BlockSpec((1, tmm, D), mod_map),
                  pl.BlockSpec((D, FF_TILE), lambda b, i, j: (0, j)),
                  pl.BlockSpec((D, FF_TILE), lambda b, i, j: (0, nj + j)),
                  pl.BlockSpec((FF_TILE, D), lambda b, i, j: (j, 0)),
                  pl.BlockSpec((1, D), lambda b, i, j: (0, 0))],
        out_specs=row,
        scratch_shapes=[pltpu.VMEM((tm, D), F32)],
        compiler_params=_cp(("parallel", "parallel", "arbitrary")),
        name="ffn_dense",
    )(h, x, gt, w_gu, w_gu, w_down, gfin)


ROUTE_TM = 512
ROW_DMA_TM = 256
MOE_SUP = 2048
MOE_SUB = 512


def _route_kernel(lg_ref, br_ref, cnt0_ref, slot_ref, wt_ref, cnt_ref, carry_scr, *, cap):
    @pl.when((pl.program_id(0) == 0) & (pl.program_id(1) == 0))
    def _():
        carry_scr[...] = cnt0_ref[...]

    lg = lg_ref[0] + br_ref[...]
    tm = lg.shape[1]
    eidx = lax.broadcasted_iota(jnp.int32, lg.shape, 0)
    m1 = jnp.max(lg, axis=0, keepdims=True)
    i1 = jnp.min(jnp.where(lg == m1, eidx, N_EXPERTS), axis=0, keepdims=True)
    lg2 = jnp.where(eidx == i1, -jnp.inf, lg)
    m2 = jnp.max(lg2, axis=0, keepdims=True)
    i2 = jnp.min(jnp.where(lg2 == m2, eidx, N_EXPERTS), axis=0, keepdims=True)
    e2 = jnp.exp(m2 - m1)
    wt_ref[0, 0:1, :] = 1.0 / (1.0 + e2)
    wt_ref[0, 1:2, :] = e2 / (1.0 + e2)
    sel1 = eidx == i1
    sel2 = eidx == i2
    oh = jnp.where(sel1 | sel2, 1.0, 0.0)
    before = (lax.broadcasted_iota(jnp.int32, (tm, tm), 0)
              < lax.broadcasted_iota(jnp.int32, (tm, tm), 1)).astype(BF16)
    rank = carry_scr[:, 0:1] + _dot(oh.astype(BF16), before)
    r1 = jnp.sum(jnp.where(sel1, rank, 0.0), axis=0, keepdims=True).astype(jnp.int32)
    r2 = jnp.sum(jnp.where(sel2, rank, 0.0), axis=0, keepdims=True).astype(jnp.int32)
    slot_ref[0, 0:1, :] = i1 * cap + r1
    slot_ref[0, 1:2, :] = i2 * cap + r2
    carry_scr[...] = carry_scr[...] + jnp.sum(oh, axis=1, keepdims=True)
    cnt_ref[...] = carry_scr[...]


def _route_slots(lgT, b_r, cnt0, cap):
    B, E, L = lgT.shape
    tm = min(ROUTE_TM, L)
    return pl.pallas_call(
        functools.partial(_route_kernel, cap=cap),
        out_shape=(jax.ShapeDtypeStruct((B, 2, L), jnp.int32),
                   jax.ShapeDtypeStruct((B, 2, L), F32),
                   jax.ShapeDtypeStruct((E, LANES), F32)),
        grid=(B, L // tm),
        in_specs=[pl.BlockSpec((1, E, tm), lambda b, i: (b, 0, i)),
                  pl.BlockSpec((E, 1), lambda b, i: (0, 0)),
                  pl.BlockSpec((E, LANES), lambda b, i: (0, 0))],
        out_specs=(pl.BlockSpec((1, 2, tm), lambda b, i: (b, 0, i)),
                   pl.BlockSpec((1, 2, tm), lambda b, i: (b, 0, i)),
                   pl.BlockSpec((E, LANES), lambda b, i: (0, 0))),
        scratch_shapes=[pltpu.VMEM((E, LANES), F32)],
        compiler_params=_cp(("arbitrary", "arbitrary")),
        name="moe_route",
    )(lgT, b_r.reshape(E, 1), cnt0)


def _row_copy(src, dst, sem):
    return pltpu.make_async_copy(src, dst, sem)


def _slot_row(code, start_ref, cap):
    shift = cap.bit_length() - 1
    return start_ref[lax.shift_right_logical(code, shift)] + (code & (cap - 1))


def _zeros_kernel(o_ref):
    o_ref[...] = jnp.zeros_like(o_ref)


def _zero_rows(n_rows, width):
    return pl.pallas_call(
        _zeros_kernel,
        out_shape=jax.ShapeDtypeStruct((n_rows, width), F32),
        grid=(n_rows // MOE_SUP,),
        out_specs=pl.BlockSpec((MOE_SUP, width), lambda i: (i, 0)),
        compiler_params=_cp(("parallel",)),
        name="moe_zero_rows",
    )()


def _dispatch_kernel(start_ref, code_ref, h_ref, xs_in_ref, xs_ref, sem, *, cap):
    del xs_in_ref
    tm = h_ref.shape[1]

    def issue(r, _):
        for k in range(2):
            row = _slot_row(code_ref[0, k, r], start_ref, cap)
            _row_copy(h_ref.at[0, pl.ds(r, 1), :], xs_ref.at[pl.ds(row, 1), :], sem).start()
        return 0

    lax.fori_loop(0, tm, issue, 0, unroll=8)
    for k in range(2):
        _row_copy(h_ref.at[0], xs_ref.at[pl.ds(0, tm), :], sem).wait()


def _dispatch(start, codes, h, xs, cap):
    B, L, D = h.shape
    n_rows = xs.shape[0]
    tm = min(ROW_DMA_TM, L)
    return pl.pallas_call(
        functools.partial(_dispatch_kernel, cap=cap),
        out_shape=jax.ShapeDtypeStruct((n_rows, D), F32),
        grid_spec=pltpu.PrefetchScalarGridSpec(
            num_scalar_prefetch=1,
            grid=(B, L // tm),
            in_specs=[pl.BlockSpec((1, 2, tm), lambda b, i, st: (b, 0, i), memory_space=pltpu.SMEM),
                      pl.BlockSpec((1, tm, D), lambda b, i, st: (b, i, 0)),
                      pl.BlockSpec(memory_space=pl.ANY)],
            out_specs=pl.BlockSpec(memory_space=pl.ANY),
            scratch_shapes=[pltpu.SemaphoreType.DMA(())],
        ),
        input_output_aliases={3: 0},
        compiler_params=_cp(("arbitrary", "arbitrary")),
        name="moe_dispatch",
    )(start, codes, h, xs)


def _moe_grp_kernel(ge_ref, gn_ref, x_ref, wg_ref, wu_ref, wd_ref, y_ref, xb_scr):
    g = pl.program_id(0)
    j = pl.program_id(1)
    nsub = gn_ref[g]
    wg = wg_ref[0].astype(BF16)
    wu = wu_ref[0].astype(BF16)
    wd = wd_ref[0].astype(BF16)
    for s in range(MOE_SUP // MOE_SUB):
        rows = slice(s * MOE_SUB, (s + 1) * MOE_SUB)

        @pl.when(s < nsub)
        def _():
            @pl.when(j == 0)
            def _():
                xb_scr[rows, :] = x_ref[rows, :].astype(BF16)

            xb = xb_scr[rows, :]
            act = _silu(_dot(xb, wg)) * _dot(xb, wu)
            part = _dot(act.astype(BF16), wd)

            @pl.when(j == 0)
            def _():
                y_ref[rows, :] = part

            @pl.when(j > 0)
            def _():
                y_ref[rows, :] = y_ref[rows, :] + part

        @pl.when((s >= nsub) & (j == 0))
        def _():
            y_ref[rows, :] = jnp.zeros((MOE_SUB, y_ref.shape[1]), F32)


def _moe_groups(counts, n_groups):
    nsup = (counts + MOE_SUP - 1) // MOE_SUP
    ends = jnp.cumsum(nsup)
    first = ends - nsup
    total = ends[-1]
    g = jnp.arange(n_groups, dtype=jnp.int32)
    gc = jnp.minimum(g, total - 1)
    e_of = jnp.minimum(jnp.sum((gc[:, None] >= ends[None, :]).astype(jnp.int32), axis=1), N_EXPERTS - 1)
    left = counts[e_of] - (gc - first[e_of]) * MOE_SUP
    nsub = jnp.clip((left + MOE_SUB - 1) // MOE_SUB, 0, MOE_SUP // MOE_SUB)
    gn = jnp.where(g < total, nsub, 0).astype(jnp.int32)
    return e_of, gn, (first * MOE_SUP).astype(jnp.int32)


def _moe_grouped(xs, ge, gn, w_gu, w_down):
    D = xs.shape[1]
    nj = D_FF // FF_TILE
    ng = xs.shape[0] // MOE_SUP
    jj = lambda j, gn, g: jnp.where(gn[g] > 0, j, nj - 1)
    return pl.pallas_call(
        _moe_grp_kernel,
        out_shape=jax.ShapeDtypeStruct(xs.shape, F32),
        grid_spec=pltpu.PrefetchScalarGridSpec(
            num_scalar_prefetch=2,
            grid=(ng, nj),
            in_specs=[pl.BlockSpec((MOE_SUP, D), lambda g, j, ge, gn: (g, 0)),
                      pl.BlockSpec((1, D, FF_TILE), lambda g, j, ge, gn: (ge[g], 0, jj(j, gn, g))),
                      pl.BlockSpec((1, D, FF_TILE), lambda g, j, ge, gn: (ge[g], 0, nj + jj(j, gn, g))),
                      pl.BlockSpec((1, FF_TILE, D), lambda g, j, ge, gn: (ge[g], jj(j, gn, g), 0))],
            out_specs=pl.BlockSpec((MOE_SUP, D), lambda g, j, ge, gn: (g, 0)),
            scratch_shapes=[pltpu.VMEM((MOE_SUP, D), BF16)],
        ),
        compiler_params=_cp(("arbitrary", "arbitrary")),
        name="moe_experts",
    )(ge, gn, xs, w_gu, w_gu, w_down)


def _combine_kernel(start_ref, code_ref, w_ref, x_ref, gt_ref, gfin_ref, ys_ref, o_ref, g_scr, sem,
                    *, cap, final):
    tm = x_ref.shape[1]

    def issue(r, _):
        for k in range(2):
            row = _slot_row(code_ref[0, k, r], start_ref, cap)
            _row_copy(ys_ref.at[pl.ds(row, 1), :], g_scr.at[k, pl.ds(r, 1), :], sem).start()
        return 0

    lax.fori_loop(0, tm, issue, 0, unroll=8)
    for k in range(2):
        _row_copy(ys_ref.at[pl.ds(0, tm), :], g_scr.at[k], sem).wait()
    w = w_ref[0]
    f = w[:, 0:1] * g_scr[0] + w[:, 1:2] * g_scr[1]
    o_ref[0] = _finish(x_ref[0] + gt_ref[0] * f, gfin_ref, final)


def _combine(start, codes, wts, x, gt, gfin, ys, *, cap, final):
    B, L, D = x.shape
    tm = min(ROW_DMA_TM, L)
    gt_spec = (pl.BlockSpec((1, 1, D), lambda b, i, st: (b, 0, 0)) if gt.shape[1] == 1
               else pl.BlockSpec((1, tm, D), lambda b, i, st: (b, i, 0)))
    row = pl.BlockSpec((1, tm, D), lambda b, i, st: (b, i, 0))
    return pl.pallas_call(
        functools.partial(_combine_kernel, cap=cap, final=final),
        out_shape=jax.ShapeDtypeStruct((B, L, D), F32),
        grid_spec=pltpu.PrefetchScalarGridSpec(
            num_scalar_prefetch=1,
            grid=(B, L // tm),
            in_specs=[pl.BlockSpec((1, 2, tm), lambda b, i, st: (b, 0, i), memory_space=pltpu.SMEM),
                      pl.BlockSpec((1, tm, 2), lambda b, i, st: (b, i, 0)),
                      row, gt_spec,
                      pl.BlockSpec((1, D), lambda b, i, st: (0, 0)),
                      pl.BlockSpec(memory_space=pl.ANY)],
            out_specs=row,
            scratch_shapes=[pltpu.VMEM((2, tm, D), F32), pltpu.SemaphoreType.DMA(())],
        ),
        compiler_params=_cp(("arbitrary", "arbitrary")),
        name="moe_combine",
    )(start, codes, wts.transpose(0, 2, 1), x, gt, gfin, ys)


def _moe_routed(groups, b_r, w_gu, w_down, gfin, *, final):
    D = groups[0][1].shape[-1]
    n_tok = sum(g[1].shape[0] * g[1].shape[1] for g in groups)
    cap = 1 << (n_tok - 1).bit_length()
    n_groups = 2 * n_tok // MOE_SUP + N_EXPERTS
    cnt = jnp.zeros((N_EXPERTS, LANES), F32)
    routed = []
    for _, _, _, lgT in groups:
        codes, wts, cnt = _route_slots(lgT, b_r, cnt, cap)
        routed.append((codes, wts))
    ge, gn, start = _moe_groups(cnt[:, 0].astype(jnp.int32), n_groups)
    xs = _zero_rows(n_groups * MOE_SUP, D)
    for (h, _, _, _), (codes, _) in zip(groups, routed):
        xs = _dispatch(start, codes, h, xs, cap)
    ys = _moe_grouped(xs, ge, gn, w_gu, w_down)
    return [_combine(start, codes, wts, x, gt, gfin, ys, cap=cap, final=final)
            for (_, x, gt, _), (codes, wts) in zip(groups, routed)]


def _pad_lanes(v):
    return jnp.pad(v.reshape(1, -1), ((0, 0), (0, LANES - v.shape[-1])))


def _mixer_layer(x, mod, states, p, s5m, l, prev_sg, *, seq):
    B, L, D = x.shape
    hi = not seq
    s5r0, s5i0, sg0, sc0 = states
    sh_m, sc_m, gt_m, sh_f, sc_f, gt_f = [mod[l][..., i * D:(i + 1) * D] for i in range(6)]
    w_in = p['w_in_seq' if seq else 'w_in'][l]
    w_gates = w_in[:, 2568:]
    w_ab = jnp.pad(w_in[:, 2560:2568], ((0, 0), (0, LANES - 8)))
    u, qkv, z, ga, gb, ab = _proj(x, p['g_mix'][l].reshape(1, D), sc_m, sh_m, w_in, w_gates, w_ab,
                                  tm=min(1024, L), hi=hi, chunked=seq)
    alog = _pad_lanes(p['gdn_a_log'][l])
    dtb = _pad_lanes(p['gdn_dt_bias'][l])
    nw = p['gdn_norm_w'][l].reshape(1, GDN_DK)
    if seq:
        yg, sfin = _s5_seq(u, s5m['be'], s5m['tp'], s5m['cpm'], s5m['pt'],
                           jnp.zeros((SLABS, B, 1, 2 * SLAB_STATE), F32), s5m['dsk'][l], l)
        sfin = sfin.reshape(SLABS, B, 2, SLAB_STATE).transpose(2, 1, 0, 3)
        sr = sfin[0].reshape(B, S5_GROUPS, S5_STATE)
        si = sfin[1].reshape(B, S5_GROUPS, S5_STATE)
        og, sg = _gdn_seq(qkv, z, ab, p['gdn_conv_w'][l], alog, dtb, nw,
                          jnp.zeros((B, GDN_CONV - 1, QKV_WIDTH), F32),
                          jnp.zeros((B, GDN_HEADS, GDN_DK, GDN_DK), F32))
        cb = qkv[:, L - (GDN_CONV - 1):, :]
    else:
        n = L
        s0 = jnp.concatenate([s5r0[l].reshape(n, SLABS, SLAB_STATE),
                              s5i0[l].reshape(n, SLABS, SLAB_STATE)], axis=-1).transpose(1, 0, 2)
        yg, s1 = _s5_step(u.reshape(SLABS, n, LANES), s5m['bst'], s5m['cpe'], s5m['a1'],
                          s0, s5m['d1'][l], l)
        yg = yg.reshape(SLABS, 1, n, LANES)
        s1 = s1.transpose(1, 0, 2)
        sr = s1[:, :, :SLAB_STATE].reshape(n, S5_GROUPS, S5_STATE)
        si = s1[:, :, SLAB_STATE:].reshape(n, S5_GROUPS, S5_STATE)
        og, sg = _gdn_step(qkv.reshape(n, QKV_WIDTH), z.reshape(n, GDN_WIDTH), ab.reshape(n, LANES),
                           p['gdn_conv_w'][l], alog, dtb, nw,
                           sc0[l].reshape(n, (GDN_CONV - 1) * QKV_WIDTH), sg0, l, prev_sg)
        og = og.reshape(1, n, GDN_WIDTH)
        cb = jnp.concatenate([sc0[l][:, 1:, :], qkv.reshape(n, 1, QKV_WIDTH)], axis=1)
    x, h, lgT = _merge(yg, og, ga, gb, x, gt_m, p['w_s5_glu'][l], p['w_gdn_out'][l], p['w_out'][l],
                       p['g_ffn'][l].reshape(1, D), sc_f, sh_f, p['w_router'][l // 2],
                       tm=min(512, L), hi=hi, chunked=seq,
                       h_dtype=BF16 if (seq and l % 2 == 0) else F32)
    return x, h, gt_f, lgT, (sr, si, sg, cb)


def kernel(x_prompt, x_sample, c_prompt, c_sample, state_s5_re, state_s5_im, state_gdn, state_conv,
           g_mix, g_ffn, g_final, w_ada, b_ada, w_in, s5_lambda_re, s5_lambda_im, s5_log_dt,
           s5_b_re, s5_b_im, s5_c_re, s5_c_im, s5_d, w_s5_glu, gdn_conv_w, gdn_a_log, gdn_dt_bias,
           gdn_norm_w, w_gdn_out, w_out, w_ffn_gate_up, w_ffn_down, w_router, b_router,
           w_exp_gate_up, w_exp_down):
    p = dict(g_mix=g_mix, g_ffn=g_ffn, w_in=w_in, w_s5_glu=w_s5_glu,
             gdn_conv_w=gdn_conv_w, gdn_a_log=gdn_a_log, gdn_dt_bias=gdn_dt_bias,
             gdn_norm_w=gdn_norm_w, w_gdn_out=w_gdn_out, w_out=w_out, w_router=w_router,
             w_in_seq=w_in.astype(BF16))
    nbp, L, D = x_prompt.shape
    nbs = x_sample.shape[0]

    mod = _ada(jnp.concatenate([c_prompt, c_sample], axis=0), w_ada, b_ada)
    mod_p = mod[:, :nbp].reshape(DEPTH, nbp, 1, 6 * D)
    mod_s = mod[:, nbp:].reshape(DEPTH, 1, nbs, 6 * D)

    seg = L // S5_T // 8
    be, bst, cpe, cpm, pt, a1 = _s5_prep(s5_lambda_re, s5_lambda_im, s5_log_dt, s5_b_re, s5_b_im,
                                         s5_c_re, s5_c_im, seg)
    d1 = [s5_d[l].reshape(SLABS, 1, LANES) for l in range(DEPTH)]
    s5m = dict(be=be, bst=bst, cpe=cpe, cpm=cpm, pt=pt, a1=a1, tp=_toep(bst, cpe), d1=d1,
               dsk=[jnp.tile(d, (1, 1, S5_T)) for d in d1])

    xs_ = [x_prompt, x_sample.reshape(1, nbs, D)]
    mods = [mod_p, mod_s]
    states = [(None, None, None, None), (state_s5_re, state_s5_im, state_gdn, state_conv)]
    outs = [[], []]
    gfin = g_final.reshape(1, D)
    for l in range(DEPTH):
        final = l == DEPTH - 1
        mixed = []
        for gi, seq in enumerate((True, False)):
            prev_sg = outs[gi][0][2] if (not seq and final and DEPTH == 2) else None
            x, h, gt_f, lgT, st = _mixer_layer(xs_[gi], mods[gi], states[gi], p, s5m, l, prev_sg, seq=seq)
            outs[gi].append(st)
            mixed.append((h, x, gt_f, lgT))
        if l % 2 == 0:
            wgu, wdn = w_ffn_gate_up[l // 2], w_ffn_down[l // 2]
            xs_ = [_ffn(h, x, gt_f, wgu if gi else wgu.astype(BF16), wdn if gi else wdn.astype(BF16), gfin,
                        tm=min(1024, x.shape[1]), hi=(gi == 1), final=final)
                   for gi, (h, x, gt_f, _) in enumerate(mixed)]
        else:
            xs_ = _moe_routed(mixed, b_router[l // 2], w_exp_gate_up[l // 2], w_exp_down[l // 2], gfin,
                              final=final)
    y_p, y_s = xs_
    st_p = [jnp.stack([o[i] for o in outs[0]]) for i in range(4)]
    st_s = [outs[1][-1][2] if (i == 2 and DEPTH == 2) else jnp.stack([o[i] for o in outs[1]])
            for i in range(4)]
    return (y_p, y_s.reshape(nbs, 1, D), st_p[0], st_p[1], st_p[2], st_p[3],
            st_s[0], st_s[1], st_s[2], st_s[3])
```

```python
import functools

import jax
import jax.numpy as jnp
from jax import lax
from jax.experimental import pallas as pl
from jax.experimental.pallas import tpu as pltpu

F32 = jnp.float32
BF16 = jnp.bfloat16
HI = lax.Precision.HIGHEST

D_MODEL = 1024
DEPTH = 2
S5_WIDTH = 512
S5_GROUP = 16
S5_GROUPS = 32
S5_STATE = 64
GDN_HEADS = 4
GDN_DK = 128
GDN_WIDTH = 512
GDN_CONV = 4
QKV_WIDTH = 1536
D_FF = 3584
N_EXPERTS = 8
NORM_EPS = 1e-6
L2_EPS = 1e-6

LANES = 128
SLABS = S5_WIDTH // LANES
SLAB_STATE = (S5_GROUPS // SLABS) * S5_STATE
S5_T = 8
GDN_C = 128
VMEM_LIMIT = 56 * 1024 * 1024


def _cp(sem, vmem=VMEM_LIMIT):
    return pltpu.CompilerParams(dimension_semantics=sem, vmem_limit_bytes=vmem)


def _dot(a, b, prec=None):
    return jnp.dot(a, b, precision=prec, preferred_element_type=F32)


def _dotb(a, b):
    return jnp.dot(a.astype(BF16), b.astype(BF16), preferred_element_type=F32)


def _dot_nt(a, b, prec=None):
    return lax.dot_general(a, b, (((1,), (1,)), ((), ())), precision=prec,
                           preferred_element_type=F32)


def _dot_tn(a, b, prec=None):
    return lax.dot_general(a, b, (((0,), (0,)), ((), ())), precision=prec,
                           preferred_element_type=F32)


def _silu(x):
    return x * jax.nn.sigmoid(x)


def _ada_kernel(c_ref, w_ref, b_ref, o_ref):
    cs = _silu(c_ref[...])
    o_ref[0] = _dot(cs, w_ref[0], HI) + b_ref[0]


def _ada(c_all, w_ada, b_ada):
    n = c_all.shape[0]
    tn = 1536
    return pl.pallas_call(
        _ada_kernel,
        out_shape=jax.ShapeDtypeStruct((DEPTH, n, 6 * D_MODEL), F32),
        grid=(DEPTH, 6 * D_MODEL // tn),
        in_specs=[pl.BlockSpec((n, D_MODEL), lambda l, j: (0, 0)),
                  pl.BlockSpec((1, D_MODEL, tn), lambda l, j: (l, 0, j)),
                  pl.BlockSpec((1, 1, tn), lambda l, j: (l, 0, j))],
        out_specs=pl.BlockSpec((1, n, tn), lambda l, j: (l, 0, j)),
        compiler_params=_cp(("parallel", "parallel")),
        name="ada_mod",
    )(c_all, w_ada, b_ada.reshape(DEPTH, 1, 6 * D_MODEL))


def _proj_kernel(x_ref, g_ref, sc_ref, sh_ref, w_ref, wg_ref, wab_ref,
                 u_ref, qkv_ref, z_ref, ga_ref, gb_ref, ab_ref, h_scr, *us_scr, hi):
    j = pl.program_id(2)

    @pl.when(j == 0)
    def _():
        x = x_ref[0]
        ms = jnp.mean(x * x, axis=-1, keepdims=True)
        xn = x * lax.rsqrt(ms + NORM_EPS) * g_ref[...]
        h_scr[...] = (xn * (1.0 + sc_ref[0]) + sh_ref[0]).astype(h_scr.dtype)

    def mm(w):
        if hi:
            return _dot(h_scr[...], w, HI)
        return _dot(h_scr[...], w.astype(BF16))

    @pl.when(j == 0)
    def _():
        res = mm(w_ref[...])
        for k in range(SLABS):
            if not us_scr:
                u_ref[k, 0] = res[:, k * LANES:(k + 1) * LANES]
                continue
            us_scr[0][...] = res[:, k * LANES:(k + 1) * LANES]
            nrow = res.shape[0] // S5_T
            for t in range(S5_T):
                u_ref[k, 0, :, t * LANES:(t + 1) * LANES] = us_scr[0][pl.ds(t, nrow, stride=S5_T), :]

    @pl.when((j >= 1) & (j <= 3))
    def _():
        qkv_ref[0] = mm(w_ref[...])

    @pl.when(j == 4)
    def _():
        z_ref[0] = mm(w_ref[...])

    @pl.when((j == 5) | (j == 6))
    def _():
        ga_ref[0] = jax.nn.sigmoid(mm(wg_ref[...]))

    @pl.when((j == 7) | (j == 8))
    def _():
        gb_ref[0] = jax.nn.sigmoid(mm(wg_ref[...]))

    @pl.when(j == 9)
    def _():
        ab_ref[0] = mm(wab_ref[...])


def _proj(x, g, sc, sh, w_in, w_gates, w_ab, *, tm, hi, chunked):
    B, L, D = x.shape
    lm = sc.shape[1]
    tmm = 1 if lm == 1 else tm
    mod_map = (lambda b, i, j: (b, 0, 0)) if lm == 1 else (lambda b, i, j: (b, i, 0))
    tn = 512
    clampi = lambda j, lo, n: jnp.clip(j - lo, 0, n - 1)
    if chunked:
        u_shape = jax.ShapeDtypeStruct((SLABS, B, L // S5_T, S5_T * LANES), F32)
        u_spec = pl.BlockSpec((SLABS, 1, tm // S5_T, S5_T * LANES), lambda b, i, j: (0, b, i, 0))
    else:
        u_shape = jax.ShapeDtypeStruct((SLABS, B, L, LANES), F32)
        u_spec = pl.BlockSpec((SLABS, 1, tm, LANES), lambda b, i, j: (0, b, i, 0))
    outs = pl.pallas_call(
        functools.partial(_proj_kernel, hi=hi),
        out_shape=(u_shape,
                   jax.ShapeDtypeStruct((B, L, QKV_WIDTH), F32),
                   jax.ShapeDtypeStruct((B, L, GDN_WIDTH), F32),
                   jax.ShapeDtypeStruct((B, L, D), F32),
                   jax.ShapeDtypeStruct((B, L, D), F32),
                   jax.ShapeDtypeStruct((B, L, LANES), F32)),
        grid=(B, L // tm, 10),
        in_specs=[pl.BlockSpec((1, tm, D), lambda b, i, j: (b, i, 0)),
                  pl.BlockSpec((1, D), lambda b, i, j: (0, 0)),
                  pl.BlockSpec((1, tmm, D), mod_map),
                  pl.BlockSpec((1, tmm, D), mod_map),
                  pl.BlockSpec((D, tn), lambda b, i, j: (0, jnp.minimum(j, 4))),
                  pl.BlockSpec((D, tn), lambda b, i, j: (0, clampi(j, 5, 4))),
                  pl.BlockSpec((D, LANES), lambda b, i, j: (0, 0))],
        out_specs=(u_spec,
                   pl.BlockSpec((1, tm, tn), lambda b, i, j: (b, i, clampi(j, 1, 3))),
                   pl.BlockSpec((1, tm, tn), lambda b, i, j: (b, i, 0)),
                   pl.BlockSpec((1, tm, tn), lambda b, i, j: (b, i, clampi(j, 5, 2))),
                   pl.BlockSpec((1, tm, tn), lambda b, i, j: (b, i, clampi(j, 7, 2))),
                   pl.BlockSpec((1, tm, LANES), lambda b, i, j: (b, i, 0))),
        scratch_shapes=[pltpu.VMEM((tm, D), F32 if hi else BF16)]
        + ([pltpu.VMEM((tm, LANES), F32)] if chunked else []),
        compiler_params=_cp(("parallel", "parallel", "arbitrary")),
        name="norm_in_proj",
    )(x, g, sc, sh, w_in, w_gates, w_ab)
    return outs


GROUPS_PER_SLAB = S5_GROUPS // SLABS


def _s5_prep_kernel(lrb, lib, dtb, bre, bim, lrc, lic, dtc, cre, cim, lrn, lin, dtn,
                    be_ref, bst_ref, cpe_ref, cpm_ref, pt_ref, a1_ref, *, seg):
    W = SLAB_STATE

    def disc(lr, li, ldt):
        dt = jnp.exp(ldt)
        mag = jnp.exp(lr * dt)
        return mag * jnp.cos(li * dt), mag * jnp.sin(li * dt)

    def cmul(xr, xi, yr, yi):
        return xr * yr - xi * yi, xr * yi + xi * yr

    lr, li = lrb[0], lib[0]
    ar, ai = disc(lr, li, dtb[0])
    den = lr * lr + li * li
    nr = ar - 1.0
    kr = (nr * lr + ai * li) / den
    ki = (ai * lr - nr * li) / den
    br, bi = bre[0], bim[0]
    bbr = kr * br - ki * bi
    bbi = kr * bi + ki * br
    rgrp = lax.broadcasted_iota(jnp.int32, (LANES, LANES), 0) // S5_GROUP
    lane_hi = lax.broadcasted_iota(jnp.int32, (LANES, LANES), 1) // S5_STATE
    pr, pi = jnp.ones_like(ar), jnp.zeros_like(ar)
    for d in range(S5_T):
        t = S5_T - 1 - d
        for ri, val in enumerate(cmul(pr, pi, bbr, bbi)):
            two = jnp.concatenate([val, val], axis=1)
            for m in range(GROUPS_PER_SLAB // 2):
                tile = jnp.where(rgrp == 2 * m + lane_hi, two, 0.0)
                c0 = ri * W + m * LANES
                be_ref[0, 0, t * LANES:(t + 1) * LANES, c0:c0 + LANES] = tile.astype(BF16)
                if d == 0:
                    bst_ref[0, 0, :, c0:c0 + LANES] = tile
        pr, pi = cmul(pr, pi, ar, ai)

    ar, ai = disc(lrc[0], lic[0], dtc[0])
    cr, ci = cre[0], cim[0]
    own = (lax.broadcasted_iota(jnp.int32, (W, LANES), 0) // S5_STATE
           == lax.broadcasted_iota(jnp.int32, (W, LANES), 1) // S5_GROUP)
    pr, pi = jnp.ones_like(ar), jnp.zeros_like(ar)
    for d in range(S5_T + 1):
        vr, vi = cmul(cr, ci, pr, pi)
        for ri, val in enumerate((vr, -vi)):
            tile = jnp.where(own, val, 0.0)
            cpe_ref[0, 0, d, ri * W:(ri + 1) * W, :] = tile
            if d >= 1:
                cpm_ref[0, 0, ri * W:(ri + 1) * W, (d - 1) * LANES:d * LANES] = tile.astype(BF16)
        pr, pi = cmul(pr, pi, ar, ai)

    ar, ai = disc(lrn[0, 0], lin[0, 0], dtn[0, 0])
    a1_ref[0, 0, :, 0:W] = ar
    a1_ref[0, 0, :, W:2 * W] = ai
    tr, ti = ar, ai
    for _ in range(S5_T - 1):
        tr, ti = cmul(tr, ti, ar, ai)
    pr, pi = jnp.ones_like(ar), jnp.zeros_like(ar)
    for i in range(seg + 1):
        pt_ref[0, 0, i:i + 1, 0:W] = pr
        pt_ref[0, 0, i:i + 1, W:2 * W] = pi
        pr, pi = cmul(pr, pi, tr, ti)


def _s5_prep(lam_re, lam_im, log_dt, b_re, b_im, c_re, c_im, seg):
    G, P, C = S5_GROUPS, S5_STATE, S5_GROUP
    W2 = 2 * SLAB_STATE
    dt3 = jnp.broadcast_to(log_dt[:, :, None], (DEPTH, G, P))
    rows_b = lambda a: jnp.repeat(a, C, axis=1)
    bt = lambda a: a.transpose(0, 1, 3, 2).reshape(DEPTH, G * C, P)
    rows_c = lambda a: jnp.broadcast_to(a.reshape(DEPTH, G * P, 1), (DEPTH, G * P, LANES))
    ct = lambda a: jnp.tile(a.transpose(0, 1, 3, 2).reshape(DEPTH, G * P, C), (1, 1, LANES // C))
    nat = lambda a: a.reshape(DEPTH, SLABS, 1, SLAB_STATE)
    args = (rows_b(lam_re), rows_b(lam_im), rows_b(dt3), bt(b_re), bt(b_im),
            rows_c(lam_re), rows_c(lam_im), rows_c(dt3), ct(c_re), ct(c_im),
            nat(lam_re), nat(lam_im), nat(dt3))
    bspec = pl.BlockSpec((1, LANES, P), lambda l, k: (l, k, 0))
    cspec = pl.BlockSpec((1, SLAB_STATE, LANES), lambda l, k: (l, k, 0))
    nspec = pl.BlockSpec((1, 1, 1, SLAB_STATE), lambda l, k: (l, k, 0, 0))
    return pl.pallas_call(
        functools.partial(_s5_prep_kernel, seg=seg),
        out_shape=(jax.ShapeDtypeStruct((DEPTH, SLABS, S5_T * LANES, W2), BF16),
                   jax.ShapeDtypeStruct((DEPTH, SLABS, LANES, W2), F32),
                   jax.ShapeDtypeStruct((DEPTH, SLABS, S5_T + 1, W2, LANES), F32),
                   jax.ShapeDtypeStruct((DEPTH, SLABS, W2, S5_T * LANES), BF16),
                   jax.ShapeDtypeStruct((DEPTH, SLABS, seg + 1, W2), F32),
                   jax.ShapeDtypeStruct((DEPTH, SLABS, 1, W2), F32)),
        grid=(DEPTH, SLABS),
        in_specs=[bspec] * 5 + [cspec] * 5 + [nspec] * 3,
        out_specs=(pl.BlockSpec((1, 1, S5_T * LANES, W2), lambda l, k: (l, k, 0, 0)),
                   pl.BlockSpec((1, 1, LANES, W2), lambda l, k: (l, k, 0, 0)),
                   pl.BlockSpec((1, 1, S5_T + 1, W2, LANES), lambda l, k: (l, k, 0, 0, 0)),
                   pl.BlockSpec((1, 1, W2, S5_T * LANES), lambda l, k: (l, k, 0, 0)),
                   pl.BlockSpec((1, 1, seg + 1, W2), lambda l, k: (l, k, 0, 0)),
                   pl.BlockSpec((1, 1, 1, W2), lambda l, k: (l, k, 0, 0))),
        compiler_params=_cp(("parallel", "parallel")),
        name="s5_discretize",
    )(*args)


def _toep_kernel(b_ref, c_ref, o_ref):
    dd = pl.program_id(2)
    bst = b_ref[0, 0]
    lag = lambda d: _dot(bst, c_ref[0, 0, d], HI)
    k0 = lag(2 * dd)
    o_ref[0, 0, 0, 0:LANES, 0:LANES] = k0.astype(BF16)
    o_ref[0, 0, 0, LANES:, LANES:] = k0.astype(BF16)
    o_ref[0, 0, 0, 0:LANES, LANES:] = lag(2 * dd + 1).astype(BF16)
    km = lag(jnp.maximum(2 * dd - 1, 0))
    o_ref[0, 0, 0, LANES:, 0:LANES] = jnp.where(dd > 0, km, 0.0).astype(BF16)


def _toep(bst, cpe):
    W2 = 2 * SLAB_STATE
    return pl.pallas_call(
        _toep_kernel,
        out_shape=jax.ShapeDtypeStruct((DEPTH, SLABS, S5_T // 2, 2 * LANES, 2 * LANES), BF16),
        grid=(DEPTH, SLABS, S5_T // 2),
        in_specs=[pl.BlockSpec((1, 1, LANES, W2), lambda l, k, d: (l, k, 0, 0)),
                  pl.BlockSpec((1, 1, S5_T + 1, W2, LANES), lambda l, k, d: (l, k, 0, 0, 0))],
        out_specs=pl.BlockSpec((1, 1, 1, 2 * LANES, 2 * LANES), lambda l, k, d: (l, k, d, 0, 0)),
        compiler_params=_cp(("parallel", "parallel", "parallel")),
        name="s5_conv_blocks",
    )(bst, cpe)


def _s5_seq_kernel(up_ref, be_ref, tp_ref, cpm_ref, pt_ref, s0_ref, dsk_ref,
                   yg_ref, sfin_ref, e_scr, sx_scr, *, nc):
    seg = nc // 8
    W = SLAB_STATE
    nt = W // LANES
    u = up_ref[0, 0]
    ub = u.astype(BF16)
    e = _dot(ub, be_ref[0, 0])
    for c in range(2 * nt):
        e_scr[c] = e[:, c * LANES:(c + 1) * LANES]

    def tiles(row):
        return [(row[:, c * LANES:(c + 1) * LANES], row[:, W + c * LANES:W + (c + 1) * LANES])
                for c in range(nt)]

    a8 = [(jnp.broadcast_to(r, (8, LANES)), jnp.broadcast_to(i, (8, LANES)))
          for r, i in tiles(pt_ref[0, 0, 1:2, :])]

    def step(i, carry):
        rows = pl.ds(i, 8, stride=seg)
        new = []
        for c in range(nt):
            sr, si = carry[c]
            ar, ai = a8[c]
            sx_scr[c, rows, :] = sr
            sx_scr[nt + c, rows, :] = si
            new.append((ar * sr - ai * si + e_scr[c, rows, :],
                        ar * si + ai * sr + e_scr[nt + c, rows, :]))
        return tuple(new)

    zero = jnp.zeros((8, LANES), F32)
    ends = lax.fori_loop(0, seg, step, tuple((zero, zero) for _ in range(nt)))

    al = tiles(pt_ref[0, 0, seg:seg + 1, :])
    cur = tiles(s0_ref[0, 0])
    car = []
    for c in range(nt):
        alr, ali = al[c]
        cr, ci = cur[c]
        sr, si = ends[c]
        crs, cis = [], []
        for j in range(8):
            crs.append(cr)
            cis.append(ci)
            cr, ci = (alr * cr - ali * ci + sr[j:j + 1], alr * ci + ali * cr + si[j:j + 1])
        sfin_ref[0, 0, :, c * LANES:(c + 1) * LANES] = cr
        sfin_ref[0, 0, :, W + c * LANES:W + (c + 1) * LANES] = ci
        car.append((jnp.concatenate(crs, axis=0), jnp.concatenate(cis, axis=0)))

    def corr(i, _):
        rows = pl.ds(i, 8, stride=seg)
        pw = tiles(pt_ref[0, 0, pl.ds(i, 1), :])
        for c in range(nt):
            pr, pi = pw[c]
            cr, ci = car[c]
            sx_scr[c, rows, :] = sx_scr[c, rows, :] + (pr * cr - pi * ci)
            sx_scr[nt + c, rows, :] = sx_scr[nt + c, rows, :] + (pr * ci + pi * cr)
        return 0

    lax.fori_loop(0, seg, corr, 0)

    sx = jnp.concatenate([sx_scr[c] for c in range(2 * nt)], axis=-1)
    y = _dot(sx.astype(BF16), cpm_ref[0, 0])
    TW = 2 * LANES
    for tq in range(S5_T // 2):
        acc = y[:, tq * TW:(tq + 1) * TW]
        for tpi in range(tq + 1):
            acc = acc + _dot(ub[:, tpi * TW:(tpi + 1) * TW], tp_ref[0, 0, tq - tpi])
        acc = acc + dsk_ref[0, :, tq * TW:(tq + 1) * TW] * u[:, tq * TW:(tq + 1) * TW]
        yg_ref[0, 0, :, tq * TW:(tq + 1) * TW] = jax.nn.gelu(acc).astype(yg_ref.dtype)


def _s5_seq(up, be_emb, tp, cpm, pt, s0, dsk, l):
    _, B, nc, _ = up.shape
    seg = nc // 8
    W2 = 2 * SLAB_STATE
    yg, sfin = pl.pallas_call(
        functools.partial(_s5_seq_kernel, nc=nc),
        out_shape=(jax.ShapeDtypeStruct((SLABS, B, nc, S5_T * LANES), BF16),
                   jax.ShapeDtypeStruct((SLABS, B, 1, W2), F32)),
        grid=(SLABS, B),
        in_specs=[pl.BlockSpec((1, 1, nc, S5_T * LANES), lambda k, b: (k, b, 0, 0)),
                  pl.BlockSpec((1, 1, S5_T * LANES, W2), lambda k, b: (l, k, 0, 0)),
                  pl.BlockSpec((1, 1, S5_T // 2, 2 * LANES, 2 * LANES), lambda k, b: (l, k, 0, 0, 0)),
                  pl.BlockSpec((1, 1, W2, S5_T * LANES), lambda k, b: (l, k, 0, 0)),
                  pl.BlockSpec((1, 1, seg + 1, W2), lambda k, b: (l, k, 0, 0)),
                  pl.BlockSpec((1, 1, 1, W2), lambda k, b: (k, b, 0, 0)),
                  pl.BlockSpec((1, 1, S5_T * LANES), lambda k, b: (k, 0, 0))],
        out_specs=(pl.BlockSpec((1, 1, nc, S5_T * LANES), lambda k, b: (k, b, 0, 0)),
                   pl.BlockSpec((1, 1, 1, W2), lambda k, b: (k, b, 0, 0))),
        scratch_shapes=[pltpu.VMEM((W2 // LANES, nc, LANES), F32),
                        pltpu.VMEM((W2 // LANES, nc, LANES), F32)],
        compiler_params=_cp(("parallel", "parallel")),
        name="s5_seq",
    )(up, be_emb, tp, cpm, pt, s0, dsk)
    return yg, sfin


def _s5_step_kernel(u_ref, b_ref, c_ref, a_ref, s0_ref, d_ref, yg_ref, s1_ref):
    W = SLAB_STATE
    u = u_ref[0]
    bu = _dot(u, b_ref[0, 0], HI)
    ar = a_ref[0, 0, :, 0:W]
    ai = a_ref[0, 0, :, W:2 * W]
    sr = s0_ref[0, :, 0:W]
    si = s0_ref[0, :, W:2 * W]
    nr = ar * sr - ai * si + bu[:, 0:W]
    ni = ar * si + ai * sr + bu[:, W:2 * W]
    s1_ref[0, :, 0:W] = nr
    s1_ref[0, :, W:2 * W] = ni
    s1 = jnp.concatenate([nr, ni], axis=-1)
    y = _dot(s1, c_ref[0, 0, 0], HI) + d_ref[0] * u
    yg_ref[0] = jax.nn.gelu(y)


def _s5_step(u_slab, bst, cpe, a1, s0, d1, l):
    _, N, _ = u_slab.shape
    W2 = 2 * SLAB_STATE
    return pl.pallas_call(
        _s5_step_kernel,
        out_shape=(jax.ShapeDtypeStruct((SLABS, N, LANES), F32),
                   jax.ShapeDtypeStruct((SLABS, N, W2), F32)),
        grid=(SLABS,),
        in_specs=[pl.BlockSpec((1, N, LANES), lambda k: (k, 0, 0)),
                  pl.BlockSpec((1, 1, LANES, W2), lambda k: (l, k, 0, 0)),
                  pl.BlockSpec((1, 1, 1, W2, LANES), lambda k: (l, k, 0, 0, 0)),
                  pl.BlockSpec((1, 1, 1, W2), lambda k: (l, k, 0, 0)),
                  pl.BlockSpec((1, N, W2), lambda k: (k, 0, 0)),
                  pl.BlockSpec((1, 1, LANES), lambda k: (k, 0, 0))],
        out_specs=(pl.BlockSpec((1, N, LANES), lambda k: (k, 0, 0)),
                   pl.BlockSpec((1, N, W2), lambda k: (k, 0, 0))),
        compiler_params=_cp(("parallel",)),
        name="s5_step",
    )(u_slab, bst, cpe, a1, s0, d1)


def _l2n(x):
    return x * lax.rsqrt(jnp.sum(x * x, axis=-1, keepdims=True) + L2_EPS)


def _split_bf16(x):
    hi = x.astype(BF16)
    return hi, (x - hi.astype(F32)).astype(BF16)


def _unit_lower_solve(As, rhss):
    n = GDN_C
    row = lax.broadcasted_iota(jnp.int32, (n, n), 0)
    col = lax.broadcasted_iota(jnp.int32, (n, n), 1)
    eye = (row == col).astype(F32)
    same8 = (row // 8) == (col // 8)
    Qs = [jnp.where(same8, -A, 0.0) for A in As]
    invs = [eye + Q for Q in Qs]
    for _ in range(2):
        Qs = [_dotb(Q, Q) for Q in Qs]
        invs = [inv + _dotb(inv, Q) for inv, Q in zip(invs, Qs)]
    s = 8
    while s < n:
        sib = ((row // (2 * s)) == (col // (2 * s))) & ((row // s) != (col // s))
        offs = [jnp.where(sib, A, 0.0).astype(BF16) for A in As]
        invb = [inv.astype(BF16) for inv in invs]
        tmp = [_dot(off, ib) for off, ib in zip(offs, invb)]
        invs = [inv - _dot(ib, t.astype(BF16)) for inv, ib, t in zip(invs, invb, tmp)]
        s *= 2
    invb = [inv.astype(BF16) for inv in invs]
    x0s = [_dot(ib, rhs.astype(BF16)) for ib, rhs in zip(invb, rhss)]
    res = []
    for A, x0, rhs in zip(As, x0s, rhss):
        ah, al = _split_bf16(A)
        xh, xl = _split_bf16(x0)
        res.append(rhs - x0 - (_dot(ah, xh) + _dot(ah, xl) + _dot(al, xh)))
    return [x0 + _dot(ib, r.astype(BF16)) for x0, ib, r in zip(x0s, invb, res)]


def _gdn_tile(qc_scr, gc, beta, z_ref, nw, o_ref, s_scr, tl):
    C, DK, H = GDN_C, GDN_DK, GDN_HEADS
    nchunk = tl // C
    probs = [(c, h) for c in range(nchunk) for h in range(H)]
    row = lax.broadcasted_iota(jnp.int32, (C, C), 0)
    col = lax.broadcasted_iota(jnp.int32, (C, C), 1)
    tri = row >= col
    strict = row > col

    def blk(c, off):
        return qc_scr[c * C:(c + 1) * C, off:off + DK]

    q = [_l2n(blk(c, h * DK)) * (DK ** -0.5) for c, h in probs]
    k = [_l2n(blk(c, GDN_WIDTH + h * DK)) for c, h in probs]
    v = [blk(c, 2 * GDN_WIDTH + h * DK) for c, h in probs]
    gcb = [jnp.broadcast_to(gc[c * C:(c + 1) * C, h:h + 1], (C, DK)) for c, h in probs]
    bb = [jnp.broadcast_to(beta[c * C:(c + 1) * C, H + h:H + h + 1], (C, DK)) for c, h in probs]
    decay = []
    for g in gcb:
        diff = g - g.T
        decay.append(jnp.where(tri, jnp.exp(jnp.where(tri, diff, 0.0)), 0.0))
    kbf = [x.astype(BF16) for x in k]
    kb = [x * b for x, b in zip(k, bb)]
    A = [jnp.where(strict, _dot_nt(x.astype(BF16), y) * d, 0.0) for x, y, d in zip(kb, kbf, decay)]
    egc = [jnp.exp(g) for g in gcb]
    rhs = [jnp.concatenate([x * b, y * e], axis=-1) for x, b, y, e in zip(v, bb, kb, egc)]
    sol = _unit_lower_solve(A, rhs)
    attn = [jnp.where(tri, _dot_nt(x.astype(BF16), y) * d, 0.0).astype(BF16)
            for x, y, d in zip(q, kbf, decay)]
    glast = [g[C - 1:C, :] for g in gcb]
    wq = [jnp.concatenate([s[:, DK:], x * e], axis=0).astype(BF16) for s, x, e in zip(sol, q, egc)]
    kg = [(x * jnp.exp(gl - g)).astype(BF16) for x, gl, g in zip(k, glast, gcb)]

    for c in range(nchunk):
        ps = [c * H + h for h in range(H)]
        S = [s_scr[h] for h in range(H)]
        ws = [_dot(wq[p], S[h].astype(BF16)) for h, p in enumerate(ps)]
        v_new = [sol[p][:, 0:DK] - w[0:C] for p, w in zip(ps, ws)]
        vb = [x.astype(BF16) for x in v_new]
        o = [w[C:] + _dot(attn[p], x) for p, w, x in zip(ps, ws, vb)]
        for h, p in enumerate(ps):
            s_scr[h] = S[h] * jnp.exp(glast[p]) + _dot_tn(kg[p], vb[h])
            zh = z_ref[0, c * C:(c + 1) * C, h * DK:(h + 1) * DK]
            on = o[h] * lax.rsqrt(jnp.mean(o[h] * o[h], axis=-1, keepdims=True) + NORM_EPS) * nw
            o_ref[0, c * C:(c + 1) * C, h * DK:(h + 1) * DK] = (on * _silu(zh)).astype(o_ref.dtype)


def _gdn_seq_kernel(qkv_ref, z_ref, ab_ref, cw_ref, alog_ref, dtb_ref, nw_ref, conv0_ref, s0_ref,
                    o_ref, sfin_ref, xp_scr, qc_scr, s_scr, *, tl):
    lt = pl.program_id(1)

    @pl.when(lt == 0)
    def _():
        xp_scr[0:8, :] = jnp.zeros((8, QKV_WIDTH), F32)
        xp_scr[8 - (GDN_CONV - 1):8, :] = conv0_ref[0]
        s_scr[...] = s0_ref[0]

    xp_scr[8:8 + tl, :] = qkv_ref[0]
    conv = cw_ref[0:1, :] * xp_scr[5:5 + tl, :]
    for j in range(1, GDN_CONV):
        conv = conv + cw_ref[j:j + 1, :] * xp_scr[5 + j:5 + j + tl, :]
    xp_scr[0:8, :] = xp_scr[tl:tl + 8, :]
    qc_scr[...] = _silu(conv)

    ab = ab_ref[0]
    g = -jnp.exp(alog_ref[...]) * jax.nn.softplus(ab + dtb_ref[...])
    beta = jax.nn.sigmoid(ab)
    row = lax.broadcasted_iota(jnp.int32, (tl, tl), 0)
    col = lax.broadcasted_iota(jnp.int32, (tl, tl), 1)
    csum = ((row >= col) & ((row // GDN_C) == (col // GDN_C))).astype(F32)
    gc = _dot(csum, g, HI)
    _gdn_tile(qc_scr, gc, beta, z_ref, nw_ref[...], o_ref, s_scr, tl)

    @pl.when(lt == pl.num_programs(1) - 1)
    def _():
        sfin_ref[0] = s_scr[...]


def _gdn_seq(qkv, z, ab, conv_w, alog, dtb, nw, conv0, s0):
    B, L, _ = qkv.shape
    tl = min(256, L)
    return pl.pallas_call(
        functools.partial(_gdn_seq_kernel, tl=tl),
        out_shape=(jax.ShapeDtypeStruct((B, L, GDN_WIDTH), BF16),
                   jax.ShapeDtypeStruct((B, GDN_HEADS, GDN_DK, GDN_DK), F32)),
        grid=(B, L // tl),
        in_specs=[pl.BlockSpec((1, tl, QKV_WIDTH), lambda b, i: (b, i, 0)),
                  pl.BlockSpec((1, tl, GDN_WIDTH), lambda b, i: (b, i, 0)),
                  pl.BlockSpec((1, tl, LANES), lambda b, i: (b, i, 0)),
                  pl.BlockSpec((GDN_CONV, QKV_WIDTH), lambda b, i: (0, 0)),
                  pl.BlockSpec((1, LANES), lambda b, i: (0, 0)),
                  pl.BlockSpec((1, LANES), lambda b, i: (0, 0)),
                  pl.BlockSpec((1, GDN_DK), lambda b, i: (0, 0)),
                  pl.BlockSpec((1, GDN_CONV - 1, QKV_WIDTH), lambda b, i: (b, 0, 0)),
                  pl.BlockSpec((1, GDN_HEADS, GDN_DK, GDN_DK), lambda b, i: (b, 0, 0, 0))],
        out_specs=(pl.BlockSpec((1, tl, GDN_WIDTH), lambda b, i: (b, i, 0)),
                   pl.BlockSpec((1, GDN_HEADS, GDN_DK, GDN_DK), lambda b, i: (b, 0, 0, 0))),
        scratch_shapes=[pltpu.VMEM((tl + 8, QKV_WIDTH), F32),
                        pltpu.VMEM((tl, QKV_WIDTH), F32),
                        pltpu.VMEM((GDN_HEADS, GDN_DK, GDN_DK), F32)],
        compiler_params=_cp(("parallel", "arbitrary")),
        name="gdn_seq",
    )(qkv, z, ab, conv_w, alog, dtb, nw, conv0, s0)


GDN_STEP_ROWS = 8


def _gdn_step_kernel(qkv_ref, z_ref, ab_ref, cw_ref, alog_ref, dtb_ref, nw_ref, conv0_ref, s0_ref,
                     *rest):
    if len(rest) == 3:
        prev_ref, o_ref, s1_all = rest
        s1_all[0] = prev_ref[...]
        s1_ref = s1_all.at[1]
    else:
        o_ref, s1_all = rest
        s1_ref = s1_all
    nb = GDN_STEP_ROWS
    W = QKV_WIDTH
    conv = cw_ref[0:1, :] * conv0_ref[:, 0:W]
    conv = conv + cw_ref[1:2, :] * conv0_ref[:, W:2 * W]
    conv = conv + cw_ref[2:3, :] * conv0_ref[:, 2 * W:3 * W]
    conv = conv + cw_ref[3:4, :] * qkv_ref[...]
    qc = _silu(conv)
    ab = ab_ref[...]
    eg = jnp.exp(-jnp.exp(alog_ref[...]) * jax.nn.softplus(ab + dtb_ref[...]))
    beta = jax.nn.sigmoid(ab)
    eye = (lax.broadcasted_iota(jnp.int32, (GDN_DK, GDN_DK), 0)
           == lax.broadcasted_iota(jnp.int32, (GDN_DK, GDN_DK), 1)).astype(F32)
    for h in range(GDN_HEADS):
        q = _l2n(qc[:, h * GDN_DK:(h + 1) * GDN_DK]) * (GDN_DK ** -0.5)
        k = _l2n(qc[:, GDN_WIDTH + h * GDN_DK:GDN_WIDTH + (h + 1) * GDN_DK])
        v = qc[:, 2 * GDN_WIDTH + h * GDN_DK:2 * GDN_WIDTH + (h + 1) * GDN_DK]
        kT = _dot_nt(eye, k, HI)
        qT = _dot_nt(eye, q, HI)
        qk = jnp.sum(q * k, axis=-1, keepdims=True)
        for j in range(nb):
            S = s0_ref[0, j, h]
            kc = jnp.broadcast_to(kT[:, j:j + 1], (GDN_DK, GDN_DK))
            qcb = jnp.broadcast_to(qT[:, j:j + 1], (GDN_DK, GDN_DK))
            kS = jnp.sum(kc * S, axis=0, keepdims=True)
            qS = jnp.sum(qcb * S, axis=0, keepdims=True)
            egj = eg[j:j + 1, h:h + 1]
            bj = beta[j:j + 1, GDN_HEADS + h:GDN_HEADS + h + 1]
            v_new = bj * v[j:j + 1, :] - (bj * egj) * kS
            o = egj * qS + qk[j:j + 1, :] * v_new
            s1_ref[j, h] = S * egj + kc * v_new
            zh = z_ref[j:j + 1, h * GDN_DK:(h + 1) * GDN_DK]
            on = o * lax.rsqrt(jnp.mean(o * o, axis=-1, keepdims=True) + NORM_EPS) * nw_ref[...]
            o_ref[j:j + 1, h * GDN_DK:(h + 1) * GDN_DK] = on * _silu(zh)


def _gdn_step(qkv, z, ab, conv_w, alog, dtb, nw, conv0, s_all, l, prev):
    N = qkv.shape[0]
    nb = GDN_STEP_ROWS
    row = lambda w: pl.BlockSpec((nb, w), lambda i: (i, 0))
    const = lambda r, w: pl.BlockSpec((r, w), lambda i: (0, 0))
    sblk = (nb, GDN_HEADS, GDN_DK, GDN_DK)
    one = pl.BlockSpec(sblk, lambda i: (i, 0, 0, 0))
    ins = [qkv, z, ab, conv_w, alog, dtb, nw, conv0, s_all]
    in_specs = [row(QKV_WIDTH), row(GDN_WIDTH), row(LANES), const(GDN_CONV, QKV_WIDTH),
                const(1, LANES), const(1, LANES), const(1, GDN_DK), row(3 * QKV_WIDTH),
                pl.BlockSpec((1,) + sblk, lambda i: (l, i, 0, 0, 0))]
    if prev is None:
        s_shape, s_spec = jax.ShapeDtypeStruct((N,) + sblk[1:], F32), one
    else:
        assert DEPTH == 2 and l == 1
        ins.append(prev)
        in_specs.append(one)
        s_shape = jax.ShapeDtypeStruct((DEPTH, N) + sblk[1:], F32)
        s_spec = pl.BlockSpec((DEPTH,) + sblk, lambda i: (0, i, 0, 0, 0))
    return pl.pallas_call(
        _gdn_step_kernel,
        out_shape=(jax.ShapeDtypeStruct((N, GDN_WIDTH), F32), s_shape),
        grid=(N // nb,),
        in_specs=in_specs,
        out_specs=(row(GDN_WIDTH), s_spec),
        compiler_params=_cp(("parallel",)),
        name="gdn_step",
    )(*ins)


def _merge_kernel(yg_ref, og_ref, ga_ref, gb_ref, x_ref, gt_ref, wglu_ref, wgo_ref, wout_ref,
                  gf_ref, scf_ref, shf_ref, wr_ref,
                  xo_ref, h_ref, lg_ref, *scr, hi, chunked):
    if chunked:
        y_scr = scr[-1]
        scr = scr[:-1]
        nrow = y_scr.shape[1] // S5_T
        for k in range(SLABS):
            for t in range(S5_T):
                y_scr[k, pl.ds(t, nrow, stride=S5_T), :] = (
                    yg_ref[k, 0, :, t * LANES:(t + 1) * LANES].astype(F32))
        y = jnp.concatenate([y_scr[k] for k in range(SLABS)], axis=-1)
    else:
        y = jnp.concatenate([yg_ref[k, 0] for k in range(SLABS)], axis=-1)
    if hi:
        wglu, wgo, wout = wglu_ref[...], wgo_ref[...], wout_ref[...]
        mm = lambda a, w: _dot(a, w, HI)
    else:
        wglu_s, wgo_s, wout_s = scr

        @pl.when((pl.program_id(0) == 0) & (pl.program_id(1) == 0))
        def _():
            wglu_s[...] = wglu_ref[...].astype(BF16)
            wgo_s[...] = wgo_ref[...].astype(BF16)
            wout_s[...] = wout_ref[...].astype(BF16)

        wglu, wgo, wout = wglu_s[...], wgo_s[...], wout_s[...]
        mm = lambda a, w: _dot(a.astype(BF16), w)

    glu = mm(y, wglu)
    branch_a = glu[:, 0:D_MODEL] * jax.nn.sigmoid(glu[:, D_MODEL:])
    branch_b = mm(og_ref[0], wgo)
    merged = ga_ref[0] * branch_a + gb_ref[0] * branch_b
    out = mm(merged, wout)
    x = x_ref[0] + gt_ref[0] * out
    xo_ref[0] = x
    ms = jnp.mean(x * x, axis=-1, keepdims=True)
    h = x * lax.rsqrt(ms + NORM_EPS) * gf_ref[...]
    h = h * (1.0 + scf_ref[0]) + shf_ref[0]
    h_ref[0] = h.astype(h_ref.dtype)
    lg_ref[0] = _dot_nt(wr_ref[...], h, HI)


def _merge(yg, og, ga, gb, x, gt, wglu, wgo, wout, gf, scf, shf, wr, *, tm, hi, chunked, h_dtype):
    B, L, D = x.shape
    lm = gt.shape[1]
    tmm = 1 if lm == 1 else tm
    mod_map = (lambda b, i: (b, 0, 0)) if lm == 1 else (lambda b, i: (b, i, 0))
    row = lambda w: pl.BlockSpec((1, tm, w), lambda b, i: (b, i, 0))
    const = lambda r, w: pl.BlockSpec((r, w), lambda b, i: (0, 0))
    mod = pl.BlockSpec((1, tmm, D), mod_map)
    scratch = [] if hi else [pltpu.VMEM((S5_WIDTH, 2 * D), BF16), pltpu.VMEM((GDN_WIDTH, D), BF16),
                             pltpu.VMEM((D, D), BF16)]
    if chunked:
        scratch = scratch + [pltpu.VMEM((SLABS, tm, LANES), F32)]
        yg_spec = pl.BlockSpec((SLABS, 1, tm // S5_T, S5_T * LANES), lambda b, i: (0, b, i, 0))
    else:
        yg_spec = pl.BlockSpec((SLABS, 1, tm, LANES), lambda b, i: (0, b, i, 0))
    lg_shape = jax.ShapeDtypeStruct((B, N_EXPERTS, L), F32)
    lg_spec = pl.BlockSpec((1, N_EXPERTS, tm), lambda b, i: (b, 0, i))
    wr = wr.T
    return pl.pallas_call(
        functools.partial(_merge_kernel, hi=hi, chunked=chunked),
        out_shape=(jax.ShapeDtypeStruct((B, L, D), F32),
                   jax.ShapeDtypeStruct((B, L, D), h_dtype),
                   lg_shape),
        grid=(B, L // tm),
        in_specs=[yg_spec,
                  row(GDN_WIDTH), row(D), row(D), row(D), mod,
                  const(S5_WIDTH, 2 * D), const(GDN_WIDTH, D), const(D, D),
                  const(1, D), mod, mod, const(*wr.shape)],
        out_specs=(row(D), row(D), lg_spec),
        scratch_shapes=scratch,
        compiler_params=_cp(("arbitrary", "arbitrary")),
        name="merge_out_proj",
    )(yg, og, ga, gb, x, gt, wglu, wgo, wout, gf, scf, shf, wr)


FF_TILE = 512


def _finish(x, gfin_ref, final):
    if not final:
        return x
    ms = jnp.mean(x * x, axis=-1, keepdims=True)
    return x * lax.rsqrt(ms + NORM_EPS) * gfin_ref[...]


def _ffn_kernel(h_ref, x_ref, gt_ref, wg_ref, wu_ref, wd_ref, gfin_ref, o_ref, acc_scr, *, hi, final):
    j = pl.program_id(2)
    if hi:
        h = h_ref[0]
        mm = lambda a, w: _dot(a, w, HI)
    else:
        h = h_ref[0].astype(BF16)
        mm = lambda a, w: _dot(a.astype(BF16), w.astype(BF16))
    act = _silu(mm(h, wg_ref[...])) * mm(h, wu_ref[...])
    part = mm(act, wd_ref[...])

    @pl.when(j == 0)
    def _():
        acc_scr[...] = part

    @pl.when(j > 0)
    def _():
        acc_scr[...] = acc_scr[...] + part

    @pl.when(j == pl.num_programs(2) - 1)
    def _():
        o_ref[0] = _finish(x_ref[0] + gt_ref[0] * acc_scr[...], gfin_ref, final)


def _ffn(h, x, gt, w_gu, w_down, gfin, *, tm, hi, final):
    B, L, D = x.shape
    lm = gt.shape[1]
    tmm = 1 if lm == 1 else tm
    mod_map = (lambda b, i, j: (b, 0, 0)) if lm == 1 else (lambda b, i, j: (b, i, 0))
    nj = D_FF // FF_TILE
    row = pl.BlockSpec((1, tm, D), lambda b, i, j: (b, i, 0))
    return pl.pallas_call(
        functools.partial(_ffn_kernel, hi=hi, final=final),
        out_shape=jax.ShapeDtypeStruct((B, L, D), F32),
        grid=(B, L // tm, nj),
        in_specs=[row, row, pl.BlockSpec((1, tmm, D), mod_map),
                  pl.BlockSpec((D, FF_TILE), lambda b, i, j: (0, j)),
                  pl.BlockSpec((D, FF_TILE), lambda b, i, j: (0, nj + j)),
                  pl.BlockSpec((FF_TILE, D), lambda b, i, j: (j, 0)),
                  pl.BlockSpec((1, D), lambda b, i, j: (0, 0))],
        out_specs=row,
        scratch_shapes=[pltpu.VMEM((tm, D), F32)],
        compiler_params=_cp(("parallel", "parallel", "arbitrary")),
        name="ffn_dense",
    )(h, x, gt, w_gu, w_gu, w_down, gfin)


ROUTE_TM = 512
ROW_DMA_TM = 256
MOE_SUP = 2048
MOE_SUB = 512


def _route_kernel(lg_ref, br_ref, cnt0_ref, slot_ref, wt_ref, cnt_ref, carry_scr, *, cap):
    @pl.when((pl.program_id(0) == 0) & (pl.program_id(1) == 0))
    def _():
        carry_scr[...] = cnt0_ref[...]

    lg = lg_ref[0] + br_ref[...]
    tm = lg.shape[1]
    eidx = lax.broadcasted_iota(jnp.int32, lg.shape, 0)
    m1 = jnp.max(lg, axis=0, keepdims=True)
    i1 = jnp.min(jnp.where(lg == m1, eidx, N_EXPERTS), axis=0, keepdims=True)
    lg2 = jnp.where(eidx == i1, -jnp.inf, lg)
    m2 = jnp.max(lg2, axis=0, keepdims=True)
    i2 = jnp.min(jnp.where(lg2 == m2, eidx, N_EXPERTS), axis=0, keepdims=True)
    e2 = jnp.exp(m2 - m1)
    wt_ref[0, 0:1, :] = 1.0 / (1.0 + e2)
    wt_ref[0, 1:2, :] = e2 / (1.0 + e2)
    sel1 = eidx == i1
    sel2 = eidx == i2
    oh = jnp.where(sel1 | sel2, 1.0, 0.0)
    before = (lax.broadcasted_iota(jnp.int32, (tm, tm), 0)
              < lax.broadcasted_iota(jnp.int32, (tm, tm), 1)).astype(BF16)
    rank = carry_scr[:, 0:1] + _dot(oh.astype(BF16), before)
    r1 = jnp.sum(jnp.where(sel1, rank, 0.0), axis=0, keepdims=True).astype(jnp.int32)
    r2 = jnp.sum(jnp.where(sel2, rank, 0.0), axis=0, keepdims=True).astype(jnp.int32)
    slot_ref[0, 0:1, :] = i1 * cap + r1
    slot_ref[0, 1:2, :] = i2 * cap + r2
    carry_scr[...] = carry_scr[...] + jnp.sum(oh, axis=1, keepdims=True)
    cnt_ref[...] = carry_scr[...]


def _route_slots(lgT, b_r, cnt0, cap):
    B, E, L = lgT.shape
    tm = min(ROUTE_TM, L)
    return pl.pallas_call(
        functools.partial(_route_kernel, cap=cap),
        out_shape=(jax.ShapeDtypeStruct((B, 2, L), jnp.int32),
                   jax.ShapeDtypeStruct((B, 2, L), F32),
                   jax.ShapeDtypeStruct((E, LANES), F32)),
        grid=(B, L // tm),
        in_specs=[pl.BlockSpec((1, E, tm), lambda b, i: (b, 0, i)),
                  pl.BlockSpec((E, 1), lambda b, i: (0, 0)),
                  pl.BlockSpec((E, LANES), lambda b, i: (0, 0))],
        out_specs=(pl.BlockSpec((1, 2, tm), lambda b, i: (b, 0, i)),
                   pl.BlockSpec((1, 2, tm), lambda b, i: (b, 0, i)),
                   pl.BlockSpec((E, LANES), lambda b, i: (0, 0))),
        scratch_shapes=[pltpu.VMEM((E, LANES), F32)],
        compiler_params=_cp(("arbitrary", "arbitrary")),
        name="moe_route",
    )(lgT, b_r.reshape(E, 1), cnt0)


def _row_copy(src, dst, sem):
    return pltpu.make_async_copy(src, dst, sem)


def _slot_row(code, start_ref, cap):
    shift = cap.bit_length() - 1
    return start_ref[lax.shift_right_logical(code, shift)] + (code & (cap - 1))


def _zeros_kernel(o_ref):
    o_ref[...] = jnp.zeros_like(o_ref)


def _zero_rows(n_rows, width):
    return pl.pallas_call(
        _zeros_kernel,
        out_shape=jax.ShapeDtypeStruct((n_rows, width), F32),
        grid=(n_rows // MOE_SUP,),
        out_specs=pl.BlockSpec((MOE_SUP, width), lambda i: (i, 0)),
        compiler_params=_cp(("parallel",)),
        name="moe_zero_rows",
    )()


def _dispatch_kernel(start_ref, code_ref, h_ref, xs_in_ref, xs_ref, hbuf, sem, *, cap):
    del xs_in_ref
    tm = h_ref.shape[1]
    t = pl.program_id(0) * pl.num_programs(1) + pl.program_id(1)
    last = pl.num_programs(0) * pl.num_programs(1) - 1
    slot = t % 2
    hbuf[slot] = h_ref[0]

    def issue(r, _):
        for k in range(2):
            row = _slot_row(code_ref[0, k, r], start_ref, cap)
            _row_copy(hbuf.at[slot, pl.ds(r, 1), :], xs_ref.at[pl.ds(row, 1), :], sem.at[slot]).start()
        return 0

    lax.fori_loop(0, tm, issue, 0, unroll=8)

    def drain(sl):
        for k in range(2):
            _row_copy(hbuf.at[sl], xs_ref.at[pl.ds(0, tm), :], sem.at[sl]).wait()

    @pl.when(t > 0)
    def _():
        drain(1 - slot)

    @pl.when(t == last)
    def _():
        drain(slot)


def _dispatch(start, codes, h, xs, cap):
    B, L, D = h.shape
    n_rows = xs.shape[0]
    tm = min(ROW_DMA_TM, L)
    return pl.pallas_call(
        functools.partial(_dispatch_kernel, cap=cap),
        out_shape=jax.ShapeDtypeStruct((n_rows, D), F32),
        grid_spec=pltpu.PrefetchScalarGridSpec(
            num_scalar_prefetch=1,
            grid=(B, L // tm),
            in_specs=[pl.BlockSpec((1, 2, tm), lambda b, i, st: (b, 0, i), memory_space=pltpu.SMEM),
                      pl.BlockSpec((1, tm, D), lambda b, i, st: (b, i, 0)),
                      pl.BlockSpec(memory_space=pl.ANY)],
            out_specs=pl.BlockSpec(memory_space=pl.ANY),
            scratch_shapes=[pltpu.VMEM((2, tm, D), F32), pltpu.SemaphoreType.DMA((2,))],
        ),
        input_output_aliases={3: 0},
        compiler_params=_cp(("arbitrary", "arbitrary")),
        name="moe_dispatch",
    )(start, codes, h, xs)


def _moe_grp_kernel(ge_ref, gn_ref, x_ref, wg_ref, wu_ref, wd_ref, y_ref, xb_scr):
    g = pl.program_id(0)
    j = pl.program_id(1)
    nsub = gn_ref[g]
    wg = wg_ref[0].astype(BF16)
    wu = wu_ref[0].astype(BF16)
    wd = wd_ref[0].astype(BF16)
    nblk = MOE_SUP // MOE_SUB

    @pl.when(j == 0)
    def _():
        xb_scr[...] = x_ref[...].astype(BF16)
        y_ref[...] = jnp.zeros_like(y_ref)

    def block(s):
        rows = slice(s * MOE_SUB, (s + 1) * MOE_SUB)
        xb = xb_scr[rows, :]
        act = _silu(_dot(xb, wg)) * _dot(xb, wu)
        y_ref[rows, :] = y_ref[rows, :] + _dot(act.astype(BF16), wd)

    @pl.when(nsub == nblk)
    def _():
        for s in range(nblk):
            block(s)

    @pl.when(nsub < nblk)
    def _():
        for s in range(nblk - 1):
            pl.when(s < nsub)(functools.partial(block, s))


def _moe_groups(counts, n_groups):
    nsup = (counts + MOE_SUP - 1) // MOE_SUP
    ends = jnp.cumsum(nsup)
    first = ends - nsup
    total = ends[-1]
    g = jnp.arange(n_groups, dtype=jnp.int32)
    gc = jnp.minimum(g, total - 1)
    e_of = jnp.minimum(jnp.sum((gc[:, None] >= ends[None, :]).astype(jnp.int32), axis=1), N_EXPERTS - 1)
    left = counts[e_of] - (gc - first[e_of]) * MOE_SUP
    nsub = jnp.clip((left + MOE_SUB - 1) // MOE_SUB, 0, MOE_SUP // MOE_SUB)
    gn = jnp.where(g < total, nsub, 0).astype(jnp.int32)
    return e_of, gn, (first * MOE_SUP).astype(jnp.int32)


def _moe_grouped(xs, ge, gn, w_gu, w_down):
    D = xs.shape[1]
    nj = D_FF // FF_TILE
    ng = xs.shape[0] // MOE_SUP
    jj = lambda j, gn, g: jnp.where(gn[g] > 0, j, nj - 1)
    return pl.pallas_call(
        _moe_grp_kernel,
        out_shape=jax.ShapeDtypeStruct(xs.shape, F32),
        grid_spec=pltpu.PrefetchScalarGridSpec(
            num_scalar_prefetch=2,
            grid=(ng, nj),
            in_specs=[pl.BlockSpec((MOE_SUP, D), lambda g, j, ge, gn: (g, 0)),
                      pl.BlockSpec((1, D, FF_TILE), lambda g, j, ge, gn: (ge[g], 0, jj(j, gn, g))),
                      pl.BlockSpec((1, D, FF_TILE), lambda g, j, ge, gn: (ge[g], 0, nj + jj(j, gn, g))),
                      pl.BlockSpec((1, FF_TILE, D), lambda g, j, ge, gn: (ge[g], jj(j, gn, g), 0))],
            out_specs=pl.BlockSpec((MOE_SUP, D), lambda g, j, ge, gn: (g, 0)),
            scratch_shapes=[pltpu.VMEM((MOE_SUP, D), BF16)],
        ),
        compiler_params=_cp(("arbitrary", "arbitrary")),
        name="moe_experts",
    )(ge, gn, xs, w_gu, w_gu, w_down)


def _combine_kernel(start_ref, code_ref, next_ref, w_ref, x_ref, gt_ref, gfin_ref, ys_ref, o_ref, g_scr, sem,
                    *, cap, final):
    tm = x_ref.shape[1]
    t = pl.program_id(0) * pl.num_programs(1) + pl.program_id(1)
    last = pl.num_programs(0) * pl.num_programs(1) - 1
    slot = t % 2

    def gather(codes, sl):
        def issue(r, _):
            for k in range(2):
                row = _slot_row(codes[0, k, r], start_ref, cap)
                _row_copy(ys_ref.at[pl.ds(row, 1), :], g_scr.at[sl, k, pl.ds(r, 1), :], sem.at[sl]).start()
            return 0

        lax.fori_loop(0, tm, issue, 0, unroll=8)

    @pl.when(t == 0)
    def _():
        gather(code_ref, slot)

    @pl.when(t < last)
    def _():
        gather(next_ref, 1 - slot)

    for k in range(2):
        _row_copy(ys_ref.at[pl.ds(0, tm), :], g_scr.at[slot, k], sem.at[slot]).wait()
    w = w_ref[0]
    f = w[:, 0:1] * g_scr[slot, 0] + w[:, 1:2] * g_scr[slot, 1]
    o_ref[0] = _finish(x_ref[0] + gt_ref[0] * f, gfin_ref, final)


def _combine(start, codes, wts, x, gt, gfin, ys, *, cap, final):
    B, L, D = x.shape
    tm = min(ROW_DMA_TM, L)
    gt_spec = (pl.BlockSpec((1, 1, D), lambda b, i, st: (b, 0, 0)) if gt.shape[1] == 1
               else pl.BlockSpec((1, tm, D), lambda b, i, st: (b, i, 0)))
    row = pl.BlockSpec((1, tm, D), lambda b, i, st: (b, i, 0))
    nl = L // tm

    def next_block(b, i, st):
        t1 = jnp.minimum(b * nl + i + 1, B * nl - 1)
        return (t1 // nl, 0, t1 % nl)

    return pl.pallas_call(
        functools.partial(_combine_kernel, cap=cap, final=final),
        out_shape=jax.ShapeDtypeStruct((B, L, D), F32),
        grid_spec=pltpu.PrefetchScalarGridSpec(
            num_scalar_prefetch=1,
            grid=(B, L // tm),
            in_specs=[pl.BlockSpec((1, 2, tm), lambda b, i, st: (b, 0, i), memory_space=pltpu.SMEM),
                      pl.BlockSpec((1, 2, tm), next_block, memory_space=pltpu.SMEM),
                      pl.BlockSpec((1, tm, 2), lambda b, i, st: (b, i, 0)),
                      row, gt_spec,
                      pl.BlockSpec((1, D), lambda b, i, st: (0, 0)),
                      pl.BlockSpec(memory_space=pl.ANY)],
            out_specs=row,
            scratch_shapes=[pltpu.VMEM((2, 2, tm, D), F32), pltpu.SemaphoreType.DMA((2,))],
        ),
        compiler_params=_cp(("arbitrary", "arbitrary")),
        name="moe_combine",
    )(start, codes, codes, wts.transpose(0, 2, 1), x, gt, gfin, ys)


def _moe_routed(groups, b_r, w_gu, w_down, gfin, *, final):
    D = groups[0][1].shape[-1]
    n_tok = sum(g[1].shape[0] * g[1].shape[1] for g in groups)
    cap = 1 << (n_tok - 1).bit_length()
    n_groups = 2 * n_tok // MOE_SUP + N_EXPERTS
    cnt = jnp.zeros((N_EXPERTS, LANES), F32)
    routed = []
    for _, _, _, lgT in groups:
        codes, wts, cnt = _route_slots(lgT, b_r, cnt, cap)
        routed.append((codes, wts))
    ge, gn, start = _moe_groups(cnt[:, 0].astype(jnp.int32), n_groups)
    xs = _zero_rows(n_groups * MOE_SUP, D)
    for (h, _, _, _), (codes, _) in zip(groups, routed):
        xs = _dispatch(start, codes, h, xs, cap)
    ys = _moe_grouped(xs, ge, gn, w_gu, w_down)
    return [_combine(start, codes, wts, x, gt, gfin, ys, cap=cap, final=final)
            for (_, x, gt, _), (codes, wts) in zip(groups, routed)]


def _pad_lanes(v):
    return jnp.pad(v.reshape(1, -1), ((0, 0), (0, LANES - v.shape[-1])))


def _mixer_layer(x, mod, states, p, s5m, l, prev_sg, *, seq):
    B, L, D = x.shape
    hi = not seq
    s5r0, s5i0, sg0, sc0 = states
    sh_m, sc_m, gt_m, sh_f, sc_f, gt_f = [mod[l][..., i * D:(i + 1) * D] for i in range(6)]
    w_in = p['w_in_seq' if seq else 'w_in'][l]
    w_gates = w_in[:, 2568:]
    w_ab = jnp.pad(w_in[:, 2560:2568], ((0, 0), (0, LANES - 8)))
    u, qkv, z, ga, gb, ab = _proj(x, p['g_mix'][l].reshape(1, D), sc_m, sh_m, w_in, w_gates, w_ab,
                                  tm=min(1024, L), hi=hi, chunked=seq)
    alog = _pad_lanes(p['gdn_a_log'][l])
    dtb = _pad_lanes(p['gdn_dt_bias'][l])
    nw = p['gdn_norm_w'][l].reshape(1, GDN_DK)
    if seq:
        yg, sfin = _s5_seq(u, s5m['be'], s5m['tp'], s5m['cpm'], s5m['pt'],
                           jnp.zeros((SLABS, B, 1, 2 * SLAB_STATE), F32), s5m['dsk'][l], l)
        sfin = sfin.reshape(SLABS, B, 2, SLAB_STATE).transpose(2, 1, 0, 3)
        sr = sfin[0].reshape(B, S5_GROUPS, S5_STATE)
        si = sfin[1].reshape(B, S5_GROUPS, S5_STATE)
        og, sg = _gdn_seq(qkv, z, ab, p['gdn_conv_w'][l], alog, dtb, nw,
                          jnp.zeros((B, GDN_CONV - 1, QKV_WIDTH), F32),
                          jnp.zeros((B, GDN_HEADS, GDN_DK, GDN_DK), F32))
        cb = qkv[:, L - (GDN_CONV - 1):, :]
    else:
        n = L
        s0 = jnp.concatenate([s5r0[l].reshape(n, SLABS, SLAB_STATE),
                              s5i0[l].reshape(n, SLABS, SLAB_STATE)], axis=-1).transpose(1, 0, 2)
        yg, s1 = _s5_step(u.reshape(SLABS, n, LANES), s5m['bst'], s5m['cpe'], s5m['a1'],
                          s0, s5m['d1'][l], l)
        yg = yg.reshape(SLABS, 1, n, LANES)
        s1 = s1.transpose(1, 0, 2)
        sr = s1[:, :, :SLAB_STATE].reshape(n, S5_GROUPS, S5_STATE)
        si = s1[:, :, SLAB_STATE:].reshape(n, S5_GROUPS, S5_STATE)
        og, sg = _gdn_step(qkv.reshape(n, QKV_WIDTH), z.reshape(n, GDN_WIDTH), ab.reshape(n, LANES),
                           p['gdn_conv_w'][l], alog, dtb, nw,
                           sc0[l].reshape(n, (GDN_CONV - 1) * QKV_WIDTH), sg0, l, prev_sg)
        og = og.reshape(1, n, GDN_WIDTH)
        cb = jnp.concatenate([sc0[l][:, 1:, :], qkv.reshape(n, 1, QKV_WIDTH)], axis=1)
    x, h, lgT = _merge(yg, og, ga, gb, x, gt_m, p['w_s5_glu'][l], p['w_gdn_out'][l], p['w_out'][l],
                       p['g_ffn'][l].reshape(1, D), sc_f, sh_f, p['w_router'][l // 2],
                       tm=min(512, L), hi=hi, chunked=seq,
                       h_dtype=BF16 if (seq and l % 2 == 0) else F32)
    return x, h, gt_f, lgT, (sr, si, sg, cb)


def kernel(x_prompt, x_sample, c_prompt, c_sample, state_s5_re, state_s5_im, state_gdn, state_conv,
           g_mix, g_ffn, g_final, w_ada, b_ada, w_in, s5_lambda_re, s5_lambda_im, s5_log_dt,
           s5_b_re, s5_b_im, s5_c_re, s5_c_im, s5_d, w_s5_glu, gdn_conv_w, gdn_a_log, gdn_dt_bias,
           gdn_norm_w, w_gdn_out, w_out, w_ffn_gate_up, w_ffn_down, w_router, b_router,
           w_exp_gate_up, w_exp_down):
    p = dict(g_mix=g_mix, g_ffn=g_ffn, w_in=w_in, w_s5_glu=w_s5_glu,
             gdn_conv_w=gdn_conv_w, gdn_a_log=gdn_a_log, gdn_dt_bias=gdn_dt_bias,
             gdn_norm_w=gdn_norm_w, w_gdn_out=w_gdn_out, w_out=w_out, w_router=w_router,
             w_in_seq=w_in.astype(BF16))
    nbp, L, D = x_prompt.shape
    nbs = x_sample.shape[0]

    mod = _ada(jnp.concatenate([c_prompt, c_sample], axis=0), w_ada, b_ada)
    mod_p = mod[:, :nbp].reshape(DEPTH, nbp, 1, 6 * D)
    mod_s = mod[:, nbp:].reshape(DEPTH, 1, nbs, 6 * D)

    seg = L // S5_T // 8
    be, bst, cpe, cpm, pt, a1 = _s5_prep(s5_lambda_re, s5_lambda_im, s5_log_dt, s5_b_re, s5_b_im,
                                         s5_c_re, s5_c_im, seg)
    d1 = [s5_d[l].reshape(SLABS, 1, LANES) for l in range(DEPTH)]
    s5m = dict(be=be, bst=bst, cpe=cpe, cpm=cpm, pt=pt, a1=a1, tp=_toep(bst, cpe), d1=d1,
               dsk=[jnp.tile(d, (1, 1, S5_T)) for d in d1])

    xs_ = [x_prompt, x_sample.reshape(1, nbs, D)]
    mods = [mod_p, mod_s]
    states = [(None, None, None, None), (state_s5_re, state_s5_im, state_gdn, state_conv)]
    outs = [[], []]
    gfin = g_final.reshape(1, D)
    for l in range(DEPTH):
        final = l == DEPTH - 1
        mixed = []
        for gi, seq in enumerate((True, False)):
            prev_sg = outs[gi][0][2] if (not seq and final and DEPTH == 2) else None
            x, h, gt_f, lgT, st = _mixer_layer(xs_[gi], mods[gi], states[gi], p, s5m, l, prev_sg, seq=seq)
            outs[gi].append(st)
            mixed.append((h, x, gt_f, lgT))
        if l % 2 == 0:
            wgu, wdn = w_ffn_gate_up[l // 2], w_ffn_down[l // 2]
            xs_ = [_ffn(h, x, gt_f, wgu if gi else wgu.astype(BF16), wdn if gi else wdn.astype(BF16), gfin,
                        tm=min(1024, x.shape[1]), hi=(gi == 1), final=final)
                   for gi, (h, x, gt_f, _) in enumerate(mixed)]
        else:
            xs_ = _moe_routed(mixed, b_router[l // 2], w_exp_gate_up[l // 2], w_exp_down[l // 2], gfin,
                              final=final)
    y_p, y_s = xs_
    st_p = [jnp.stack([o[i] for o in outs[0]]) for i in range(4)]
    st_s = [outs[1][-1][2] if (i == 2 and DEPTH == 2) else jnp.stack([o[i] for o in outs[1]])
            for i in range(4)]
    return (y_p, y_s.reshape(nbs, 1, D), st_p[0], st_p[1], st_p[2], st_p[3],
            st_s[0], st_s[1], st_s[2], st_s[3])
```

```python
import functools

import jax
import jax.numpy as jnp
from jax import lax
from jax.experimental import pallas as pl
from jax.experimental.pallas import tpu as pltpu

F32 = jnp.float32
BF16 = jnp.bfloat16
HI = lax.Precision.HIGHEST

D_MODEL = 1024
DEPTH = 2
S5_WIDTH = 512
S5_GROUP = 16
S5_GROUPS = 32
S5_STATE = 64
GDN_HEADS = 4
GDN_DK = 128
GDN_WIDTH = 512
GDN_CONV = 4
QKV_WIDTH = 1536
D_FF = 3584
N_EXPERTS = 8
NORM_EPS = 1e-6
L2_EPS = 1e-6

LANES = 128
SLABS = S5_WIDTH // LANES
SLAB_STATE = (S5_GROUPS // SLABS) * S5_STATE
S5_T = 8
GDN_C = 128
VMEM_LIMIT = 56 * 1024 * 1024


def _cp(sem, vmem=VMEM_LIMIT):
    return pltpu.CompilerParams(dimension_semantics=sem, vmem_limit_bytes=vmem)


def _dot(a, b, prec=None):
    return jnp.dot(a, b, precision=prec, preferred_element_type=F32)


def _dotb(a, b):
    return jnp.dot(a.astype(BF16), b.astype(BF16), preferred_element_type=F32)


def _dot_nt(a, b, prec=None):
    return lax.dot_general(a, b, (((1,), (1,)), ((), ())), precision=prec,
                           preferred_element_type=F32)


def _dot_tn(a, b, prec=None):
    return lax.dot_general(a, b, (((0,), (0,)), ((), ())), precision=prec,
                           preferred_element_type=F32)


def _silu(x):
    return x * jax.nn.sigmoid(x)


def _ada_kernel(c_ref, w_ref, b_ref, o_ref):
    cs = _silu(c_ref[...])
    o_ref[0] = _dot(cs, w_ref[0], HI) + b_ref[0]


def _ada(c_all, w_ada, b_ada):
    n = c_all.shape[0]
    tn = 1536
    return pl.pallas_call(
        _ada_kernel,
        out_shape=jax.ShapeDtypeStruct((DEPTH, n, 6 * D_MODEL), F32),
        grid=(DEPTH, 6 * D_MODEL // tn),
        in_specs=[pl.BlockSpec((n, D_MODEL), lambda l, j: (0, 0)),
                  pl.BlockSpec((1, D_MODEL, tn), lambda l, j: (l, 0, j)),
                  pl.BlockSpec((1, 1, tn), lambda l, j: (l, 0, j))],
        out_specs=pl.BlockSpec((1, n, tn), lambda l, j: (l, 0, j)),
        compiler_params=_cp(("parallel", "parallel")),
        name="ada_mod",
    )(c_all, w_ada, b_ada.reshape(DEPTH, 1, 6 * D_MODEL))


def _proj_kernel(x_ref, g_ref, sc_ref, sh_ref, w_ref, wg_ref, wab_ref, *rest, hi, chunked):
    if chunked:
        (cw_ref, conv0_ref, u_ref, qkv_ref, z_ref, ga_ref, gb_ref, ab_ref, tail_ref,
         h_scr, us, cv_scr) = rest
        us_scr = (us,)
    else:
        u_ref, qkv_ref, z_ref, ga_ref, gb_ref, ab_ref, h_scr = rest
        us_scr = ()
    j = pl.program_id(2)

    @pl.when(j == 0)
    def _():
        x = x_ref[0]
        ms = jnp.mean(x * x, axis=-1, keepdims=True)
        xn = x * lax.rsqrt(ms + NORM_EPS) * g_ref[0]
        h_scr[...] = (xn * (1.0 + sc_ref[0, 0]) + sh_ref[0, 0]).astype(h_scr.dtype)

    def mm(w):
        if hi:
            return _dot(h_scr[...], w, HI)
        return _dot(h_scr[...], w.astype(BF16))

    @pl.when(j == 0)
    def _():
        res = mm(w_ref[0])
        for k in range(SLABS):
            if not us_scr:
                u_ref[k, 0] = res[:, k * LANES:(k + 1) * LANES]
                continue
            us_scr[0][...] = res[:, k * LANES:(k + 1) * LANES]
            nrow = res.shape[0] // S5_T
            for t in range(S5_T):
                u_ref[k, 0, :, t * LANES:(t + 1) * LANES] = us_scr[0][pl.ds(t, nrow, stride=S5_T), :]

    @pl.when((j >= 1) & (j <= 3))
    def _():
        res = mm(w_ref[0])
        if not chunked:
            qkv_ref[0] = res
            return
        tm = res.shape[0]
        c = j - 1
        pad = 8 - (GDN_CONV - 1)

        @pl.when(pl.program_id(1) == 0)
        def _():
            cv_scr[c, 0:pad, :] = jnp.zeros((pad, res.shape[1]), F32)
            cv_scr[c, pad:8, :] = conv0_ref[0]

        cv_scr[c, 8:8 + tm, :] = res
        tail_ref[0, 0] = res[tm - 8:tm, :]
        conv = cw_ref[0, 0:1, :] * cv_scr[c, pad:pad + tm, :]
        for jj in range(1, GDN_CONV):
            conv = conv + cw_ref[0, jj:jj + 1, :] * cv_scr[c, pad + jj:pad + jj + tm, :]
        cv_scr[c, 0:8, :] = cv_scr[c, tm:tm + 8, :]
        qkv_ref[0] = _silu(conv)

    @pl.when(j == 4)
    def _():
        z_ref[0] = mm(w_ref[0])

    @pl.when((j == 5) | (j == 6))
    def _():
        ga_ref[0] = jax.nn.sigmoid(mm(wg_ref[0]))

    @pl.when((j == 7) | (j == 8))
    def _():
        gb_ref[0] = jax.nn.sigmoid(mm(wg_ref[0]))

    @pl.when(j == 9)
    def _():
        ab_ref[0] = mm(wab_ref[0])


def _mod_spec(mod, l, chunk, tm):
    per_row = mod.shape[2] != 1
    D = mod.shape[3] // 6

    def index(b, i, *_):
        return (l, b, i if per_row else 0, chunk)

    return pl.BlockSpec((1, 1, tm if per_row else 1, D), index)


def _proj(x, g, mod, w_in, w_gates, w_ab, conv_w=None, conv0=None, *, l, tm, hi, chunked):
    B, L, D = x.shape
    tn = 512
    clampi = lambda j, lo, n: jnp.clip(j - lo, 0, n - 1)
    if chunked:
        u_shape = jax.ShapeDtypeStruct((SLABS, B, L // S5_T, S5_T * LANES), F32)
        u_spec = pl.BlockSpec((SLABS, 1, tm // S5_T, S5_T * LANES), lambda b, i, j: (0, b, i, 0))
    else:
        u_shape = jax.ShapeDtypeStruct((SLABS, B, L, LANES), F32)
        u_spec = pl.BlockSpec((SLABS, 1, tm, LANES), lambda b, i, j: (0, b, i, 0))
    ins = [x, g, mod, mod, w_in, w_gates, w_ab]
    in_extra, out_extra_shape, out_extra_spec, scratch_extra = [], (), (), []
    if chunked:
        ins += [conv_w, conv0]
        in_extra = [pl.BlockSpec((1, GDN_CONV, tn), lambda b, i, j: (l, 0, clampi(j, 1, 3))),
                    pl.BlockSpec((1, GDN_CONV - 1, tn), lambda b, i, j: (b, 0, clampi(j, 1, 3)))]
        out_extra_shape = (jax.ShapeDtypeStruct((B, L // tm, 8, QKV_WIDTH), F32),)
        out_extra_spec = (pl.BlockSpec((1, 1, 8, tn), lambda b, i, j: (b, i, 0, clampi(j, 1, 3))),)
        scratch_extra = [pltpu.VMEM((tm, LANES), F32), pltpu.VMEM((QKV_WIDTH // tn, tm + 8, tn), F32)]
    outs = pl.pallas_call(
        functools.partial(_proj_kernel, hi=hi, chunked=chunked),
        out_shape=(u_shape,
                   jax.ShapeDtypeStruct((B, L, QKV_WIDTH), F32),
                   jax.ShapeDtypeStruct((B, L, GDN_WIDTH), F32),
                   jax.ShapeDtypeStruct((B, L, D), F32),
                   jax.ShapeDtypeStruct((B, L, D), F32),
                   jax.ShapeDtypeStruct((B, L, LANES), F32)) + out_extra_shape,
        grid=(B, L // tm, 10),
        in_specs=[pl.BlockSpec((1, tm, D), lambda b, i, j: (b, i, 0)),
                  pl.BlockSpec((1, 1, D), lambda b, i, j: (l, 0, 0)),
                  _mod_spec(mod, l, 1, tm),
                  _mod_spec(mod, l, 0, tm),
                  pl.BlockSpec((1, D, tn), lambda b, i, j: (l, 0, jnp.minimum(j, 4))),
                  pl.BlockSpec((1, D, tn), lambda b, i, j: (l, 0, clampi(j, 5, 4))),
                  pl.BlockSpec((1, D, LANES), lambda b, i, j: (l, 0, 0))] + in_extra,
        out_specs=(u_spec,
                   pl.BlockSpec((1, tm, tn), lambda b, i, j: (b, i, clampi(j, 1, 3))),
                   pl.BlockSpec((1, tm, tn), lambda b, i, j: (b, i, 0)),
                   pl.BlockSpec((1, tm, tn), lambda b, i, j: (b, i, clampi(j, 5, 2))),
                   pl.BlockSpec((1, tm, tn), lambda b, i, j: (b, i, clampi(j, 7, 2))),
                   pl.BlockSpec((1, tm, LANES), lambda b, i, j: (b, i, 0))) + out_extra_spec,
        scratch_shapes=[pltpu.VMEM((tm, D), F32 if hi else BF16)] + scratch_extra,
        compiler_params=_cp(("parallel", "arbitrary", "arbitrary")),
        name="norm_in_proj",
    )(*ins)
    return outs


GROUPS_PER_SLAB = S5_GROUPS // SLABS


def _s5_prep_kernel(lrb, lib, dtb, bre, bim, lrc, lic, dtc, cre, cim, lrn, lin, dtn,
                    be_ref, bst_ref, cpe_ref, cpm_ref, pt_ref, a1_ref, *, seg):
    W = SLAB_STATE

    def disc(lr, li, ldt):
        dt = jnp.exp(ldt)
        mag = jnp.exp(lr * dt)
        return mag * jnp.cos(li * dt), mag * jnp.sin(li * dt)

    def cmul(xr, xi, yr, yi):
        return xr * yr - xi * yi, xr * yi + xi * yr

    lr, li = lrb[0], lib[0]
    ar, ai = disc(lr, li, dtb[0])
    den = lr * lr + li * li
    nr = ar - 1.0
    kr = (nr * lr + ai * li) / den
    ki = (ai * lr - nr * li) / den
    br, bi = bre[0], bim[0]
    bbr = kr * br - ki * bi
    bbi = kr * bi + ki * br
    rgrp = lax.broadcasted_iota(jnp.int32, (LANES, LANES), 0) // S5_GROUP
    lane_hi = lax.broadcasted_iota(jnp.int32, (LANES, LANES), 1) // S5_STATE
    pr, pi = jnp.ones_like(ar), jnp.zeros_like(ar)
    for d in range(S5_T):
        t = S5_T - 1 - d
        for ri, val in enumerate(cmul(pr, pi, bbr, bbi)):
            two = jnp.concatenate([val, val], axis=1)
            for m in range(GROUPS_PER_SLAB // 2):
                tile = jnp.where(rgrp == 2 * m + lane_hi, two, 0.0)
                c0 = ri * W + m * LANES
                be_ref[0, 0, t * LANES:(t + 1) * LANES, c0:c0 + LANES] = tile.astype(BF16)
                if d == 0:
                    bst_ref[0, 0, :, c0:c0 + LANES] = tile
        pr, pi = cmul(pr, pi, ar, ai)

    ar, ai = disc(lrc[0], lic[0], dtc[0])
    cr, ci = cre[0], cim[0]
    own = (lax.broadcasted_iota(jnp.int32, (W, LANES), 0) // S5_STATE
           == lax.broadcasted_iota(jnp.int32, (W, LANES), 1) // S5_GROUP)
    pr, pi = jnp.ones_like(ar), jnp.zeros_like(ar)
    for d in range(S5_T + 1):
        vr, vi = cmul(cr, ci, pr, pi)
        for ri, val in enumerate((vr, -vi)):
            tile = jnp.where(own, val, 0.0)
            cpe_ref[0, 0, d, ri * W:(ri + 1) * W, :] = tile
            if d >= 1:
                cpm_ref[0, 0, ri * W:(ri + 1) * W, (d - 1) * LANES:d * LANES] = tile.astype(BF16)
        pr, pi = cmul(pr, pi, ar, ai)

    ar, ai = disc(lrn[0, 0], lin[0, 0], dtn[0, 0])
    a1_ref[0, 0, :, 0:W] = ar
    a1_ref[0, 0, :, W:2 * W] = ai
    tr, ti = ar, ai
    for _ in range(S5_T - 1):
        tr, ti = cmul(tr, ti, ar, ai)
    pr, pi = jnp.ones_like(ar), jnp.zeros_like(ar)
    for i in range(seg + 1):
        pt_ref[0, 0, i:i + 1, 0:W] = pr
        pt_ref[0, 0, i:i + 1, W:2 * W] = pi
        pr, pi = cmul(pr, pi, tr, ti)


def _s5_prep(lam_re, lam_im, log_dt, b_re, b_im, c_re, c_im, seg):
    G, P, C = S5_GROUPS, S5_STATE, S5_GROUP
    W2 = 2 * SLAB_STATE
    dt3 = jnp.broadcast_to(log_dt[:, :, None], (DEPTH, G, P))
    rows_b = lambda a: jnp.repeat(a, C, axis=1)
    bt = lambda a: a.transpose(0, 1, 3, 2).reshape(DEPTH, G * C, P)
    rows_c = lambda a: jnp.broadcast_to(a.reshape(DEPTH, G * P, 1), (DEPTH, G * P, LANES))
    ct = lambda a: jnp.tile(a.transpose(0, 1, 3, 2).reshape(DEPTH, G * P, C), (1, 1, LANES // C))
    nat = lambda a: a.reshape(DEPTH, SLABS, 1, SLAB_STATE)
    args = (rows_b(lam_re), rows_b(lam_im), rows_b(dt3), bt(b_re), bt(b_im),
            rows_c(lam_re), rows_c(lam_im), rows_c(dt3), ct(c_re), ct(c_im),
            nat(lam_re), nat(lam_im), nat(dt3))
    bspec = pl.BlockSpec((1, LANES, P), lambda l, k: (l, k, 0))
    cspec = pl.BlockSpec((1, SLAB_STATE, LANES), lambda l, k: (l, k, 0))
    nspec = pl.BlockSpec((1, 1, 1, SLAB_STATE), lambda l, k: (l, k, 0, 0))
    return pl.pallas_call(
        functools.partial(_s5_prep_kernel, seg=seg),
        out_shape=(jax.ShapeDtypeStruct((DEPTH, SLABS, S5_T * LANES, W2), BF16),
                   jax.ShapeDtypeStruct((DEPTH, SLABS, LANES, W2), F32),
                   jax.ShapeDtypeStruct((DEPTH, SLABS, S5_T + 1, W2, LANES), F32),
                   jax.ShapeDtypeStruct((DEPTH, SLABS, W2, S5_T * LANES), BF16),
                   jax.ShapeDtypeStruct((DEPTH, SLABS, seg + 1, W2), F32),
                   jax.ShapeDtypeStruct((DEPTH, SLABS, 1, W2), F32)),
        grid=(DEPTH, SLABS),
        in_specs=[bspec] * 5 + [cspec] * 5 + [nspec] * 3,
        out_specs=(pl.BlockSpec((1, 1, S5_T * LANES, W2), lambda l, k: (l, k, 0, 0)),
                   pl.BlockSpec((1, 1, LANES, W2), lambda l, k: (l, k, 0, 0)),
                   pl.BlockSpec((1, 1, S5_T + 1, W2, LANES), lambda l, k: (l, k, 0, 0, 0)),
                   pl.BlockSpec((1, 1, W2, S5_T * LANES), lambda l, k: (l, k, 0, 0)),
                   pl.BlockSpec((1, 1, seg + 1, W2), lambda l, k: (l, k, 0, 0)),
                   pl.BlockSpec((1, 1, 1, W2), lambda l, k: (l, k, 0, 0))),
        compiler_params=_cp(("parallel", "parallel")),
        name="s5_discretize",
    )(*args)


def _toep_kernel(b_ref, c_ref, o_ref):
    dd = pl.program_id(2)
    bst = b_ref[0, 0]
    lag = lambda d: _dot(bst, c_ref[0, 0, d], HI)
    k0 = lag(2 * dd)
    o_ref[0, 0, 0, 0:LANES, 0:LANES] = k0.astype(BF16)
    o_ref[0, 0, 0, LANES:, LANES:] = k0.astype(BF16)
    o_ref[0, 0, 0, 0:LANES, LANES:] = lag(2 * dd + 1).astype(BF16)
    km = lag(jnp.maximum(2 * dd - 1, 0))
    o_ref[0, 0, 0, LANES:, 0:LANES] = jnp.where(dd > 0, km, 0.0).astype(BF16)


def _toep(bst, cpe):
    W2 = 2 * SLAB_STATE
    return pl.pallas_call(
        _toep_kernel,
        out_shape=jax.ShapeDtypeStruct((DEPTH, SLABS, S5_T // 2, 2 * LANES, 2 * LANES), BF16),
        grid=(DEPTH, SLABS, S5_T // 2),
        in_specs=[pl.BlockSpec((1, 1, LANES, W2), lambda l, k, d: (l, k, 0, 0)),
                  pl.BlockSpec((1, 1, S5_T + 1, W2, LANES), lambda l, k, d: (l, k, 0, 0, 0))],
        out_specs=pl.BlockSpec((1, 1, 1, 2 * LANES, 2 * LANES), lambda l, k, d: (l, k, d, 0, 0)),
        compiler_params=_cp(("parallel", "parallel", "parallel")),
        name="s5_conv_blocks",
    )(bst, cpe)


def _s5_seq_kernel(up_ref, be_ref, tp_ref, cpm_ref, pt_ref, s0_ref, dsk_ref,
                   yg_ref, sfin_ref, e_scr, sx_scr, *, nc):
    seg = nc // 8
    W = SLAB_STATE
    nt = W // LANES
    u = up_ref[0, 0]
    ub = u.astype(BF16)
    e = _dot(ub, be_ref[0, 0])
    for c in range(2 * nt):
        e_scr[c] = e[:, c * LANES:(c + 1) * LANES]

    def tiles(row):
        return [(row[:, c * LANES:(c + 1) * LANES], row[:, W + c * LANES:W + (c + 1) * LANES])
                for c in range(nt)]

    a8 = [(jnp.broadcast_to(r, (8, LANES)), jnp.broadcast_to(i, (8, LANES)))
          for r, i in tiles(pt_ref[0, 0, 1:2, :])]

    def step(i, carry):
        rows = pl.ds(i, 8, stride=seg)
        new = []
        for c in range(nt):
            sr, si = carry[c]
            ar, ai = a8[c]
            sx_scr[c, rows, :] = sr
            sx_scr[nt + c, rows, :] = si
            new.append((ar * sr - ai * si + e_scr[c, rows, :],
                        ar * si + ai * sr + e_scr[nt + c, rows, :]))
        return tuple(new)

    zero = jnp.zeros((8, LANES), F32)
    ends = lax.fori_loop(0, seg, step, tuple((zero, zero) for _ in range(nt)))

    al = tiles(pt_ref[0, 0, seg:seg + 1, :])
    cur = tiles(s0_ref[0, 0])
    car = []
    for c in range(nt):
        alr, ali = al[c]
        cr, ci = cur[c]
        sr, si = ends[c]
        crs, cis = [], []
        for j in range(8):
            crs.append(cr)
            cis.append(ci)
            cr, ci = (alr * cr - ali * ci + sr[j:j + 1], alr * ci + ali * cr + si[j:j + 1])
        sfin_ref[0, 0, :, c * LANES:(c + 1) * LANES] = cr
        sfin_ref[0, 0, :, W + c * LANES:W + (c + 1) * LANES] = ci
        car.append((jnp.concatenate(crs, axis=0), jnp.concatenate(cis, axis=0)))

    def corr(i, _):
        rows = pl.ds(i, 8, stride=seg)
        pw = tiles(pt_ref[0, 0, pl.ds(i, 1), :])
        for c in range(nt):
            pr, pi = pw[c]
            cr, ci = car[c]
            sx_scr[c, rows, :] = sx_scr[c, rows, :] + (pr * cr - pi * ci)
            sx_scr[nt + c, rows, :] = sx_scr[nt + c, rows, :] + (pr * ci + pi * cr)
        return 0

    lax.fori_loop(0, seg, corr, 0)

    sx = jnp.concatenate([sx_scr[c] for c in range(2 * nt)], axis=-1)
    y = _dot(sx.astype(BF16), cpm_ref[0, 0])
    TW = 2 * LANES
    for tq in range(S5_T // 2):
        acc = y[:, tq * TW:(tq + 1) * TW]
        for tpi in range(tq + 1):
            acc = acc + _dot(ub[:, tpi * TW:(tpi + 1) * TW], tp_ref[0, 0, tq - tpi])
        acc = acc + dsk_ref[0, :, tq * TW:(tq + 1) * TW] * u[:, tq * TW:(tq + 1) * TW]
        yg_ref[0, 0, :, tq * TW:(tq + 1) * TW] = jax.nn.gelu(acc).astype(yg_ref.dtype)


def _s5_seq(up, be_emb, tp, cpm, pt, s0, dsk, l):
    _, B, nc, _ = up.shape
    seg = nc // 8
    W2 = 2 * SLAB_STATE
    yg, sfin = pl.pallas_call(
        functools.partial(_s5_seq_kernel, nc=nc),
        out_shape=(jax.ShapeDtypeStruct((SLABS, B, nc, S5_T * LANES), BF16),
                   jax.ShapeDtypeStruct((SLABS, B, 1, W2), F32)),
        grid=(SLABS, B),
        in_specs=[pl.BlockSpec((1, 1, nc, S5_T * LANES), lambda k, b: (k, b, 0, 0)),
                  pl.BlockSpec((1, 1, S5_T * LANES, W2), lambda k, b: (l, k, 0, 0)),
                  pl.BlockSpec((1, 1, S5_T // 2, 2 * LANES, 2 * LANES), lambda k, b: (l, k, 0, 0, 0)),
                  pl.BlockSpec((1, 1, W2, S5_T * LANES), lambda k, b: (l, k, 0, 0)),
                  pl.BlockSpec((1, 1, seg + 1, W2), lambda k, b: (l, k, 0, 0)),
                  pl.BlockSpec((1, 1, 1, W2), lambda k, b: (k, b, 0, 0)),
                  pl.BlockSpec((1, 1, S5_T * LANES), lambda k, b: (k, 0, 0))],
        out_specs=(pl.BlockSpec((1, 1, nc, S5_T * LANES), lambda k, b: (k, b, 0, 0)),
                   pl.BlockSpec((1, 1, 1, W2), lambda k, b: (k, b, 0, 0))),
        scratch_shapes=[pltpu.VMEM((W2 // LANES, nc, LANES), F32),
                        pltpu.VMEM((W2 // LANES, nc, LANES), F32)],
        compiler_params=_cp(("parallel", "parallel")),
        name="s5_seq",
    )(up, be_emb, tp, cpm, pt, s0, dsk)
    return yg, sfin


def _s5_step_kernel(u_ref, b_ref, c_ref, a_ref, s0_ref, d_ref, yg_ref, s1_ref):
    W = SLAB_STATE
    u = u_ref[0]
    bu = _dot(u, b_ref[0, 0], HI)
    ar = a_ref[0, 0, :, 0:W]
    ai = a_ref[0, 0, :, W:2 * W]
    sr = s0_ref[0, :, 0:W]
    si = s0_ref[0, :, W:2 * W]
    nr = ar * sr - ai * si + bu[:, 0:W]
    ni = ar * si + ai * sr + bu[:, W:2 * W]
    s1_ref[0, :, 0:W] = nr
    s1_ref[0, :, W:2 * W] = ni
    s1 = jnp.concatenate([nr, ni], axis=-1)
    y = _dot(s1, c_ref[0, 0, 0], HI) + d_ref[0] * u
    yg_ref[0] = jax.nn.gelu(y)


def _s5_step(u_slab, bst, cpe, a1, s0, d1, l):
    _, N, _ = u_slab.shape
    W2 = 2 * SLAB_STATE
    return pl.pallas_call(
        _s5_step_kernel,
        out_shape=(jax.ShapeDtypeStruct((SLABS, N, LANES), F32),
                   jax.ShapeDtypeStruct((SLABS, N, W2), F32)),
        grid=(SLABS,),
        in_specs=[pl.BlockSpec((1, N, LANES), lambda k: (k, 0, 0)),
                  pl.BlockSpec((1, 1, LANES, W2), lambda k: (l, k, 0, 0)),
                  pl.BlockSpec((1, 1, 1, W2, LANES), lambda k: (l, k, 0, 0, 0)),
                  pl.BlockSpec((1, 1, 1, W2), lambda k: (l, k, 0, 0)),
                  pl.BlockSpec((1, N, W2), lambda k: (k, 0, 0)),
                  pl.BlockSpec((1, 1, LANES), lambda k: (k, 0, 0))],
        out_specs=(pl.BlockSpec((1, N, LANES), lambda k: (k, 0, 0)),
                   pl.BlockSpec((1, N, W2), lambda k: (k, 0, 0))),
        compiler_params=_cp(("parallel",)),
        name="s5_step",
    )(u_slab, bst, cpe, a1, s0, d1)


def _l2n(x):
    return x * lax.rsqrt(jnp.sum(x * x, axis=-1, keepdims=True) + L2_EPS)


def _split_bf16(x):
    hi = x.astype(BF16)
    return hi, (x - hi.astype(F32)).astype(BF16)


def _unit_lower_solve(As, rhss):
    n = GDN_C
    row = lax.broadcasted_iota(jnp.int32, (n, n), 0)
    col = lax.broadcasted_iota(jnp.int32, (n, n), 1)
    eye = (row == col).astype(F32)
    same8 = (row // 8) == (col // 8)
    Qs = [jnp.where(same8, -A, 0.0) for A in As]
    invs = [eye + Q for Q in Qs]
    for _ in range(2):
        Qs = [_dotb(Q, Q) for Q in Qs]
        invs = [inv + _dotb(inv, Q) for inv, Q in zip(invs, Qs)]
    s = 8
    while s < n:
        sib = ((row // (2 * s)) == (col // (2 * s))) & ((row // s) != (col // s))
        offs = [jnp.where(sib, A, 0.0).astype(BF16) for A in As]
        invb = [inv.astype(BF16) for inv in invs]
        tmp = [_dot(off, ib) for off, ib in zip(offs, invb)]
        invs = [inv - _dot(ib, t.astype(BF16)) for inv, ib, t in zip(invs, invb, tmp)]
        s *= 2
    invb = [inv.astype(BF16) for inv in invs]
    x0s = [_dot(ib, rhs.astype(BF16)) for ib, rhs in zip(invb, rhss)]
    res = []
    for A, x0, rhs in zip(As, x0s, rhss):
        ah, al = _split_bf16(A)
        xh, xl = _split_bf16(x0)
        res.append(rhs - x0 - (_dot(ah, xh) + _dot(ah, xl) + _dot(al, xh)))
    return [x0 + _dot(ib, r.astype(BF16)) for x0, ib, r in zip(x0s, invb, res)]


def _gdn_tile(qc_scr, gc, beta, z_ref, nw, o_ref, s_scr, tl):
    C, DK, H = GDN_C, GDN_DK, GDN_HEADS
    nchunk = tl // C
    probs = [(c, h) for c in range(nchunk) for h in range(H)]
    row = lax.broadcasted_iota(jnp.int32, (C, C), 0)
    col = lax.broadcasted_iota(jnp.int32, (C, C), 1)
    tri = row >= col
    strict = row > col

    def blk(c, off):
        return qc_scr[c * C:(c + 1) * C, off:off + DK]

    q = [_l2n(blk(c, h * DK)) * (DK ** -0.5) for c, h in probs]
    k = [_l2n(blk(c, GDN_WIDTH + h * DK)) for c, h in probs]
    v = [blk(c, 2 * GDN_WIDTH + h * DK) for c, h in probs]
    gcb = [jnp.broadcast_to(gc[c * C:(c + 1) * C, h:h + 1], (C, DK)) for c, h in probs]
    bb = [jnp.broadcast_to(beta[c * C:(c + 1) * C, H + h:H + h + 1], (C, DK)) for c, h in probs]
    decay = []
    for g in gcb:
        diff = g - g.T
        decay.append(jnp.where(tri, jnp.exp(jnp.where(tri, diff, 0.0)), 0.0))
    kbf = [x.astype(BF16) for x in k]
    kb = [x * b for x, b in zip(k, bb)]
    A = [jnp.where(strict, _dot_nt(x.astype(BF16), y) * d, 0.0) for x, y, d in zip(kb, kbf, decay)]
    egc = [jnp.exp(g) for g in gcb]
    rhs = [jnp.concatenate([x * b, y * e], axis=-1) for x, b, y, e in zip(v, bb, kb, egc)]
    sol = _unit_lower_solve(A, rhs)
    attn = [jnp.where(tri, _dot_nt(x.astype(BF16), y) * d, 0.0).astype(BF16)
            for x, y, d in zip(q, kbf, decay)]
    glast = [g[C - 1:C, :] for g in gcb]
    wq = [jnp.concatenate([s[:, DK:], x * e], axis=0).astype(BF16) for s, x, e in zip(sol, q, egc)]
    kg = [(x * jnp.exp(gl - g)).astype(BF16) for x, gl, g in zip(k, glast, gcb)]

    for c in range(nchunk):
        ps = [c * H + h for h in range(H)]
        S = [s_scr[h] for h in range(H)]
        ws = [_dot(wq[p], S[h].astype(BF16)) for h, p in enumerate(ps)]
        v_new = [sol[p][:, 0:DK] - w[0:C] for p, w in zip(ps, ws)]
        vb = [x.astype(BF16) for x in v_new]
        o = [w[C:] + _dot(attn[p], x) for p, w, x in zip(ps, ws, vb)]
        for h, p in enumerate(ps):
            s_scr[h] = S[h] * jnp.exp(glast[p]) + _dot_tn(kg[p], vb[h])
            zh = z_ref[0, c * C:(c + 1) * C, h * DK:(h + 1) * DK]
            on = o[h] * lax.rsqrt(jnp.mean(o[h] * o[h], axis=-1, keepdims=True) + NORM_EPS) * nw
            o_ref[0, c * C:(c + 1) * C, h * DK:(h + 1) * DK] = (on * _silu(zh)).astype(o_ref.dtype)


def _gdn_seq_kernel(qc_ref, z_ref, ab_ref, alog_ref, dtb_ref, nw_ref, s0_ref,
                    o_ref, sfin_ref, s_scr, *, tl):
    lt = pl.program_id(1)

    @pl.when(lt == 0)
    def _():
        s_scr[...] = s0_ref[0]

    qc_scr = qc_ref.at[0]
    ab = ab_ref[0]
    g = -jnp.exp(alog_ref[...]) * jax.nn.softplus(ab + dtb_ref[...])
    beta = jax.nn.sigmoid(ab)
    row = lax.broadcasted_iota(jnp.int32, (tl, tl), 0)
    col = lax.broadcasted_iota(jnp.int32, (tl, tl), 1)
    csum = ((row >= col) & ((row // GDN_C) == (col // GDN_C))).astype(F32)
    gc = _dot(csum, g, HI)
    _gdn_tile(qc_scr, gc, beta, z_ref, nw_ref[...], o_ref, s_scr, tl)

    @pl.when(lt == pl.num_programs(1) - 1)
    def _():
        sfin_ref[0] = s_scr[...]


def _gdn_seq(qkv, z, ab, alog, dtb, nw, s0):
    B, L, _ = qkv.shape
    tl = min(256, L)
    return pl.pallas_call(
        functools.partial(_gdn_seq_kernel, tl=tl),
        out_shape=(jax.ShapeDtypeStruct((B, L, GDN_WIDTH), BF16),
                   jax.ShapeDtypeStruct((B, GDN_HEADS, GDN_DK, GDN_DK), F32)),
        grid=(B, L // tl),
        in_specs=[pl.BlockSpec((1, tl, QKV_WIDTH), lambda b, i: (b, i, 0)),
                  pl.BlockSpec((1, tl, GDN_WIDTH), lambda b, i: (b, i, 0)),
                  pl.BlockSpec((1, tl, LANES), lambda b, i: (b, i, 0)),
                  pl.BlockSpec((1, LANES), lambda b, i: (0, 0)),
                  pl.BlockSpec((1, LANES), lambda b, i: (0, 0)),
                  pl.BlockSpec((1, GDN_DK), lambda b, i: (0, 0)),
                  pl.BlockSpec((1, GDN_HEADS, GDN_DK, GDN_DK), lambda b, i: (b, 0, 0, 0))],
        out_specs=(pl.BlockSpec((1, tl, GDN_WIDTH), lambda b, i: (b, i, 0)),
                   pl.BlockSpec((1, GDN_HEADS, GDN_DK, GDN_DK), lambda b, i: (b, 0, 0, 0))),
        scratch_shapes=[pltpu.VMEM((GDN_HEADS, GDN_DK, GDN_DK), F32)],
        compiler_params=_cp(("parallel", "arbitrary")),
        name="gdn_seq",
    )(qkv, z, ab, alog, dtb, nw, s0)


GDN_STEP_ROWS = 8


def _gdn_step_kernel(qkv_ref, z_ref, ab_ref, cw_ref, alog_ref, dtb_ref, nw_ref, conv0_ref, s0_ref,
                     *rest):
    if len(rest) == 3:
        prev_ref, o_ref, s1_all = rest
        s1_all[0] = prev_ref[...]
        s1_ref = s1_all.at[1]
    else:
        o_ref, s1_all = rest
        s1_ref = s1_all
    nb = GDN_STEP_ROWS
    W = QKV_WIDTH
    conv = cw_ref[0:1, :] * conv0_ref[:, 0:W]
    conv = conv + cw_ref[1:2, :] * conv0_ref[:, W:2 * W]
    conv = conv + cw_ref[2:3, :] * conv0_ref[:, 2 * W:3 * W]
    conv = conv + cw_ref[3:4, :] * qkv_ref[...]
    qc = _silu(conv)
    ab = ab_ref[...]
    eg = jnp.exp(-jnp.exp(alog_ref[...]) * jax.nn.softplus(ab + dtb_ref[...]))
    beta = jax.nn.sigmoid(ab)
    eye = (lax.broadcasted_iota(jnp.int32, (GDN_DK, GDN_DK), 0)
           == lax.broadcasted_iota(jnp.int32, (GDN_DK, GDN_DK), 1)).astype(F32)
    for h in range(GDN_HEADS):
        q = _l2n(qc[:, h * GDN_DK:(h + 1) * GDN_DK]) * (GDN_DK ** -0.5)
        k = _l2n(qc[:, GDN_WIDTH + h * GDN_DK:GDN_WIDTH + (h + 1) * GDN_DK])
        v = qc[:, 2 * GDN_WIDTH + h * GDN_DK:2 * GDN_WIDTH + (h + 1) * GDN_DK]
        kT = _dot_nt(eye, k, HI)
        qT = _dot_nt(eye, q, HI)
        qk = jnp.sum(q * k, axis=-1, keepdims=True)
        for j in range(nb):
            S = s0_ref[0, j, h]
            kc = jnp.broadcast_to(kT[:, j:j + 1], (GDN_DK, GDN_DK))
            qcb = jnp.broadcast_to(qT[:, j:j + 1], (GDN_DK, GDN_DK))
            kS = jnp.sum(kc * S, axis=0, keepdims=True)
            qS = jnp.sum(qcb * S, axis=0, keepdims=True)
            egj = eg[j:j + 1, h:h + 1]
            bj = beta[j:j + 1, GDN_HEADS + h:GDN_HEADS + h + 1]
            v_new = bj * v[j:j + 1, :] - (bj * egj) * kS
            o = egj * qS + qk[j:j + 1, :] * v_new
            s1_ref[j, h] = S * egj + kc * v_new
            zh = z_ref[j:j + 1, h * GDN_DK:(h + 1) * GDN_DK]
            on = o * lax.rsqrt(jnp.mean(o * o, axis=-1, keepdims=True) + NORM_EPS) * nw_ref[...]
            o_ref[j:j + 1, h * GDN_DK:(h + 1) * GDN_DK] = on * _silu(zh)


def _gdn_step(qkv, z, ab, conv_w, alog, dtb, nw, conv0, s_all, l, prev):
    N = qkv.shape[0]
    nb = GDN_STEP_ROWS
    row = lambda w: pl.BlockSpec((nb, w), lambda i: (i, 0))
    const = lambda r, w: pl.BlockSpec((r, w), lambda i: (0, 0))
    sblk = (nb, GDN_HEADS, GDN_DK, GDN_DK)
    one = pl.BlockSpec(sblk, lambda i: (i, 0, 0, 0))
    ins = [qkv, z, ab, conv_w, alog, dtb, nw, conv0, s_all]
    in_specs = [row(QKV_WIDTH), row(GDN_WIDTH), row(LANES), const(GDN_CONV, QKV_WIDTH),
                const(1, LANES), const(1, LANES), const(1, GDN_DK), row(3 * QKV_WIDTH),
                pl.BlockSpec((1,) + sblk, lambda i: (l, i, 0, 0, 0))]
    if prev is None:
        s_shape, s_spec = jax.ShapeDtypeStruct((N,) + sblk[1:], F32), one
    else:
        assert DEPTH == 2 and l == 1
        ins.append(prev)
        in_specs.append(one)
        s_shape = jax.ShapeDtypeStruct((DEPTH, N) + sblk[1:], F32)
        s_spec = pl.BlockSpec((DEPTH,) + sblk, lambda i: (0, i, 0, 0, 0))
    return pl.pallas_call(
        _gdn_step_kernel,
        out_shape=(jax.ShapeDtypeStruct((N, GDN_WIDTH), F32), s_shape),
        grid=(N // nb,),
        in_specs=in_specs,
        out_specs=(row(GDN_WIDTH), s_spec),
        compiler_params=_cp(("parallel",)),
        name="gdn_step",
    )(*ins)


def _merge_kernel(yg_ref, og_ref, ga_ref, gb_ref, x_ref, gt_ref, wglu_ref, wgo_ref, wout_ref,
                  gf_ref, scf_ref, shf_ref, wr_ref,
                  xo_ref, h_ref, lg_ref, *scr, hi, chunked):
    if chunked:
        y_scr = scr[-1]
        scr = scr[:-1]
        nrow = y_scr.shape[1] // S5_T
        for k in range(SLABS):
            for t in range(S5_T):
                y_scr[k, pl.ds(t, nrow, stride=S5_T), :] = (
                    yg_ref[k, 0, :, t * LANES:(t + 1) * LANES].astype(F32))
        y = jnp.concatenate([y_scr[k] for k in range(SLABS)], axis=-1)
    else:
        y = jnp.concatenate([yg_ref[k, 0] for k in range(SLABS)], axis=-1)
    if hi:
        wglu, wgo, wout = wglu_ref[0], wgo_ref[0], wout_ref[0]
        mm = lambda a, w: _dot(a, w, HI)
    else:
        wglu_s, wgo_s, wout_s = scr

        @pl.when((pl.program_id(0) == 0) & (pl.program_id(1) == 0))
        def _():
            wglu_s[...] = wglu_ref[0].astype(BF16)
            wgo_s[...] = wgo_ref[0].astype(BF16)
            wout_s[...] = wout_ref[0].astype(BF16)

        wglu, wgo, wout = wglu_s[...], wgo_s[...], wout_s[...]
        mm = lambda a, w: _dot(a.astype(BF16), w)

    glu = mm(y, wglu)
    branch_a = glu[:, 0:D_MODEL] * jax.nn.sigmoid(glu[:, D_MODEL:])
    branch_b = mm(og_ref[0], wgo)
    merged = ga_ref[0] * branch_a + gb_ref[0] * branch_b
    out = mm(merged, wout)
    x = x_ref[0] + gt_ref[0, 0] * out
    xo_ref[0] = x
    ms = jnp.mean(x * x, axis=-1, keepdims=True)
    h = x * lax.rsqrt(ms + NORM_EPS) * gf_ref[0]
    h = h * (1.0 + scf_ref[0, 0]) + shf_ref[0, 0]
    h_ref[0] = h.astype(h_ref.dtype)
    lg_ref[0] = _dot_nt(wr_ref[0], h, HI)


def _merge(yg, og, ga, gb, x, mod, wglu, wgo, wout, gf, wr, *, l, tm, hi, chunked, h_dtype):
    B, L, D = x.shape
    row = lambda w: pl.BlockSpec((1, tm, w), lambda b, i: (b, i, 0))
    layer = lambda r, w, ll=l: pl.BlockSpec((1, r, w), lambda b, i: (ll, 0, 0))
    scratch = [] if hi else [pltpu.VMEM((S5_WIDTH, 2 * D), BF16), pltpu.VMEM((GDN_WIDTH, D), BF16),
                             pltpu.VMEM((D, D), BF16)]
    if chunked:
        scratch = scratch + [pltpu.VMEM((SLABS, tm, LANES), F32)]
        yg_spec = pl.BlockSpec((SLABS, 1, tm // S5_T, S5_T * LANES), lambda b, i: (0, b, i, 0))
    else:
        yg_spec = pl.BlockSpec((SLABS, 1, tm, LANES), lambda b, i: (0, b, i, 0))
    lg_shape = jax.ShapeDtypeStruct((B, N_EXPERTS, L), F32)
    lg_spec = pl.BlockSpec((1, N_EXPERTS, tm), lambda b, i: (b, 0, i))
    return pl.pallas_call(
        functools.partial(_merge_kernel, hi=hi, chunked=chunked),
        out_shape=(jax.ShapeDtypeStruct((B, L, D), F32),
                   jax.ShapeDtypeStruct((B, L, D), h_dtype),
                   lg_shape),
        grid=(B, L // tm),
        in_specs=[yg_spec,
                  row(GDN_WIDTH), row(D), row(D), row(D), _mod_spec(mod, l, 2, tm),
                  layer(S5_WIDTH, 2 * D), layer(GDN_WIDTH, D), layer(D, D),
                  layer(1, D), _mod_spec(mod, l, 4, tm), _mod_spec(mod, l, 3, tm),
                  layer(N_EXPERTS, D, l // 2)],
        out_specs=(row(D), row(D), lg_spec),
        scratch_shapes=scratch,
        compiler_params=_cp(("arbitrary", "arbitrary")),
        name="merge_out_proj",
    )(yg, og, ga, gb, x, mod, wglu, wgo, wout, gf, mod, mod, wr)


FF_TILE = 512


def _finish(x, gfin_ref, final):
    if not final:
        return x
    ms = jnp.mean(x * x, axis=-1, keepdims=True)
    return x * lax.rsqrt(ms + NORM_EPS) * gfin_ref[...]


def _ffn_kernel(h_ref, x_ref, gt_ref, wg_ref, wu_ref, wd_ref, gfin_ref, o_ref, acc_scr, *, hi, final):
    j = pl.program_id(2)
    if hi:
        h = h_ref[0]
        mm = lambda a, w: _dot(a, w, HI)
    else:
        h = h_ref[0].astype(BF16)
        mm = lambda a, w: _dot(a.astype(BF16), w.astype(BF16))
    act = _silu(mm(h, wg_ref[...])) * mm(h, wu_ref[...])
    part = mm(act, wd_ref[...])

    @pl.when(j == 0)
    def _():
        acc_scr[...] = part

    @pl.when(j > 0)
    def _():
        acc_scr[...] = acc_scr[...] + part

    @pl.when(j == pl.num_programs(2) - 1)
    def _():
        o_ref[0] = _finish(x_ref[0] + gt_ref[0, 0] * acc_scr[...], gfin_ref, final)


def _ffn(h, x, mod, w_gu, w_down, gfin, *, l, tm, hi, final):
    B, L, D = x.shape
    nj = D_FF // FF_TILE
    row = pl.BlockSpec((1, tm, D), lambda b, i, j: (b, i, 0))
    return pl.pallas_call(
        functools.partial(_ffn_kernel, hi=hi, final=final),
        out_shape=jax.ShapeDtypeStruct((B, L, D), F32),
        grid=(B, L // tm, nj),
        in_specs=[row, row, _mod_spec(mod, l, 5, tm),
                  pl.BlockSpec((D, FF_TILE), lambda b, i, j: (0, j)),
                  pl.BlockSpec((D, FF_TILE), lambda b, i, j: (0, nj + j)),
                  pl.BlockSpec((FF_TILE, D), lambda b, i, j: (j, 0)),
                  pl.BlockSpec((1, D), lambda b, i, j: (0, 0))],
        out_specs=row,
        scratch_shapes=[pltpu.VMEM((tm, D), F32)],
        compiler_params=_cp(("parallel", "parallel", "arbitrary")),
        name="ffn_dense",
    )(h, x, mod, w_gu, w_gu, w_down, gfin)


ROUTE_TM = 512
ROW_DMA_TM = 256
MOE_SUP = 2048
MOE_SUB = 512


def _route_kernel(lg_ref, br_ref, cnt0_ref, slot_ref, wt_ref, cnt_ref, carry_scr, *, cap):
    @pl.when((pl.program_id(0) == 0) & (pl.program_id(1) == 0))
    def _():
        carry_scr[...] = cnt0_ref[...]

    lg = lg_ref[0] + br_ref[...]
    tm = lg.shape[1]
    eidx = lax.broadcasted_iota(jnp.int32, lg.shape, 0)
    m1 = jnp.max(lg, axis=0, keepdims=True)
    i1 = jnp.min(jnp.where(lg == m1, eidx, N_EXPERTS), axis=0, keepdims=True)
    lg2 = jnp.where(eidx == i1, -jnp.inf, lg)
    m2 = jnp.max(lg2, axis=0, keepdims=True)
    i2 = jnp.min(jnp.where(lg2 == m2, eidx, N_EXPERTS), axis=0, keepdims=True)
    e2 = jnp.exp(m2 - m1)
    wt_ref[0, 0:1, :] = 1.0 / (1.0 + e2)
    wt_ref[0, 1:2, :] = e2 / (1.0 + e2)
    sel1 = eidx == i1
    sel2 = eidx == i2
    oh = jnp.where(sel1 | sel2, 1.0, 0.0)
    before = (lax.broadcasted_iota(jnp.int32, (tm, tm), 0)
              < lax.broadcasted_iota(jnp.int32, (tm, tm), 1)).astype(BF16)
    rank = carry_scr[:, 0:1] + _dot(oh.astype(BF16), before)
    r1 = jnp.sum(jnp.where(sel1, rank, 0.0), axis=0, keepdims=True).astype(jnp.int32)
    r2 = jnp.sum(jnp.where(sel2, rank, 0.0), axis=0, keepdims=True).astype(jnp.int32)
    slot_ref[0, 0:1, :] = i1 * cap + r1
    slot_ref[0, 1:2, :] = i2 * cap + r2
    carry_scr[...] = carry_scr[...] + jnp.sum(oh, axis=1, keepdims=True)
    cnt_ref[...] = carry_scr[...]


def _route_slots(lgT, b_r, cnt0, cap):
    B, E, L = lgT.shape
    tm = min(ROUTE_TM, L)
    return pl.pallas_call(
        functools.partial(_route_kernel, cap=cap),
        out_shape=(jax.ShapeDtypeStruct((B, 2, L), jnp.int32),
                   jax.ShapeDtypeStruct((B, 2, L), F32),
                   jax.ShapeDtypeStruct((E, LANES), F32)),
        grid=(B, L // tm),
        in_specs=[pl.BlockSpec((1, E, tm), lambda b, i: (b, 0, i)),
                  pl.BlockSpec((E, 1), lambda b, i: (0, 0)),
                  pl.BlockSpec((E, LANES), lambda b, i: (0, 0))],
        out_specs=(pl.BlockSpec((1, 2, tm), lambda b, i: (b, 0, i)),
                   pl.BlockSpec((1, 2, tm), lambda b, i: (b, 0, i)),
                   pl.BlockSpec((E, LANES), lambda b, i: (0, 0))),
        scratch_shapes=[pltpu.VMEM((E, LANES), F32)],
        compiler_params=_cp(("arbitrary", "arbitrary")),
        name="moe_route",
    )(lgT, b_r.reshape(E, 1), cnt0)


def _row_copy(src, dst, sem):
    return pltpu.make_async_copy(src, dst, sem)


def _slot_row(code, start_ref, cap):
    shift = cap.bit_length() - 1
    return start_ref[lax.shift_right_logical(code, shift)] + (code & (cap - 1))


def _zeros_kernel(o_ref):
    o_ref[...] = jnp.zeros_like(o_ref)


def _zero_rows(n_rows, width):
    return pl.pallas_call(
        _zeros_kernel,
        out_shape=jax.ShapeDtypeStruct((n_rows, width), F32),
        grid=(n_rows // MOE_SUP,),
        out_specs=pl.BlockSpec((MOE_SUP, width), lambda i: (i, 0)),
        compiler_params=_cp(("parallel",)),
        name="moe_zero_rows",
    )()


def _dispatch_kernel(start_ref, code_ref, h_ref, xs_in_ref, xs_ref, hbuf, sem, *, cap):
    del xs_in_ref
    tm = h_ref.shape[1]
    t = pl.program_id(0) * pl.num_programs(1) + pl.program_id(1)
    last = pl.num_programs(0) * pl.num_programs(1) - 1
    slot = t % 2
    hbuf[slot] = h_ref[0]

    def issue(r, _):
        for k in range(2):
            row = _slot_row(code_ref[0, k, r], start_ref, cap)
            _row_copy(hbuf.at[slot, pl.ds(r, 1), :], xs_ref.at[pl.ds(row, 1), :], sem.at[slot]).start()
        return 0

    lax.fori_loop(0, tm, issue, 0, unroll=8)

    def drain(sl):
        for k in range(2):
            _row_copy(hbuf.at[sl], xs_ref.at[pl.ds(0, tm), :], sem.at[sl]).wait()

    @pl.when(t > 0)
    def _():
        drain(1 - slot)

    @pl.when(t == last)
    def _():
        drain(slot)


def _dispatch(start, codes, h, xs, cap):
    B, L, D = h.shape
    n_rows = xs.shape[0]
    tm = min(ROW_DMA_TM, L)
    return pl.pallas_call(
        functools.partial(_dispatch_kernel, cap=cap),
        out_shape=jax.ShapeDtypeStruct((n_rows, D), F32),
        grid_spec=pltpu.PrefetchScalarGridSpec(
            num_scalar_prefetch=1,
            grid=(B, L // tm),
            in_specs=[pl.BlockSpec((1, 2, tm), lambda b, i, st: (b, 0, i), memory_space=pltpu.SMEM),
                      pl.BlockSpec((1, tm, D), lambda b, i, st: (b, i, 0)),
                      pl.BlockSpec(memory_space=pl.ANY)],
            out_specs=pl.BlockSpec(memory_space=pl.ANY),
            scratch_shapes=[pltpu.VMEM((2, tm, D), F32), pltpu.SemaphoreType.DMA((2,))],
        ),
        input_output_aliases={3: 0},
        compiler_params=_cp(("arbitrary", "arbitrary")),
        name="moe_dispatch",
    )(start, codes, h, xs)


def _moe_grp_kernel(ge_ref, gn_ref, x_ref, wg_ref, wu_ref, wd_ref, y_ref, xb_scr):
    g = pl.program_id(0)
    j = pl.program_id(1)
    nsub = gn_ref[g]
    wg = wg_ref[0].astype(BF16)
    wu = wu_ref[0].astype(BF16)
    wd = wd_ref[0].astype(BF16)
    nblk = MOE_SUP // MOE_SUB

    @pl.when(j == 0)
    def _():
        xb_scr[...] = x_ref[...].astype(BF16)
        y_ref[...] = jnp.zeros_like(y_ref)

    def block(s):
        rows = slice(s * MOE_SUB, (s + 1) * MOE_SUB)
        xb = xb_scr[rows, :]
        act = _silu(_dot(xb, wg)) * _dot(xb, wu)
        y_ref[rows, :] = y_ref[rows, :] + _dot(act.astype(BF16), wd)

    @pl.when(nsub == nblk)
    def _():
        for s in range(nblk):
            block(s)

    @pl.when(nsub < nblk)
    def _():
        for s in range(nblk - 1):
            pl.when(s < nsub)(functools.partial(block, s))


def _moe_groups(counts, n_groups):
    nsup = (counts + MOE_SUP - 1) // MOE_SUP
    ends = jnp.cumsum(nsup)
    first = ends - nsup
    total = ends[-1]
    g = jnp.arange(n_groups, dtype=jnp.int32)
    gc = jnp.minimum(g, total - 1)
    e_of = jnp.minimum(jnp.sum((gc[:, None] >= ends[None, :]).astype(jnp.int32), axis=1), N_EXPERTS - 1)
    left = counts[e_of] - (gc - first[e_of]) * MOE_SUP
    nsub = jnp.clip((left + MOE_SUB - 1) // MOE_SUB, 0, MOE_SUP // MOE_SUB)
    gn = jnp.where(g < total, nsub, 0).astype(jnp.int32)
    return e_of, gn, (first * MOE_SUP).astype(jnp.int32)


def _moe_grouped(xs, ge, gn, w_gu, w_down):
    D = xs.shape[1]
    nj = D_FF // FF_TILE
    ng = xs.shape[0] // MOE_SUP
    jj = lambda j, gn, g: jnp.where(gn[g] > 0, j, nj - 1)
    return pl.pallas_call(
        _moe_grp_kernel,
        out_shape=jax.ShapeDtypeStruct(xs.shape, F32),
        grid_spec=pltpu.PrefetchScalarGridSpec(
            num_scalar_prefetch=2,
            grid=(ng, nj),
            in_specs=[pl.BlockSpec((MOE_SUP, D), lambda g, j, ge, gn: (g, 0)),
                      pl.BlockSpec((1, D, FF_TILE), lambda g, j, ge, gn: (ge[g], 0, jj(j, gn, g))),
                      pl.BlockSpec((1, D, FF_TILE), lambda g, j, ge, gn: (ge[g], 0, nj + jj(j, gn, g))),
                      pl.BlockSpec((1, FF_TILE, D), lambda g, j, ge, gn: (ge[g], jj(j, gn, g), 0))],
            out_specs=pl.BlockSpec((MOE_SUP, D), lambda g, j, ge, gn: (g, 0)),
            scratch_shapes=[pltpu.VMEM((MOE_SUP, D), BF16)],
        ),
        compiler_params=_cp(("arbitrary", "arbitrary")),
        name="moe_experts",
    )(ge, gn, xs, w_gu, w_gu, w_down)


def _combine_kernel(start_ref, code_ref, next_ref, w_ref, x_ref, gt_ref, gfin_ref, ys_ref, o_ref, g_scr, sem,
                    *, cap, final):
    tm = x_ref.shape[1]
    t = pl.program_id(0) * pl.num_programs(1) + pl.program_id(1)
    last = pl.num_programs(0) * pl.num_programs(1) - 1
    slot = t % 2

    def gather(codes, sl):
        def issue(r, _):
            for k in range(2):
                row = _slot_row(codes[0, k, r], start_ref, cap)
                _row_copy(ys_ref.at[pl.ds(row, 1), :], g_scr.at[sl, k, pl.ds(r, 1), :], sem.at[sl]).start()
            return 0

        lax.fori_loop(0, tm, issue, 0, unroll=8)

    @pl.when(t == 0)
    def _():
        gather(code_ref, slot)

    @pl.when(t < last)
    def _():
        gather(next_ref, 1 - slot)

    for k in range(2):
        _row_copy(ys_ref.at[pl.ds(0, tm), :], g_scr.at[slot, k], sem.at[slot]).wait()
    w = w_ref[0]
    f = w[:, 0:1] * g_scr[slot, 0] + w[:, 1:2] * g_scr[slot, 1]
    o_ref[0] = _finish(x_ref[0] + gt_ref[0, 0] * f, gfin_ref, final)


def _combine(start, codes, wts, x, mod, gfin, ys, *, l, cap, final):
    B, L, D = x.shape
    tm = min(ROW_DMA_TM, L)
    gt_spec = _mod_spec(mod, l, 5, tm)
    row = pl.BlockSpec((1, tm, D), lambda b, i, st: (b, i, 0))
    nl = L // tm

    def next_block(b, i, st):
        t1 = jnp.minimum(b * nl + i + 1, B * nl - 1)
        return (t1 // nl, 0, t1 % nl)

    return pl.pallas_call(
        functools.partial(_combine_kernel, cap=cap, final=final),
        out_shape=jax.ShapeDtypeStruct((B, L, D), F32),
        grid_spec=pltpu.PrefetchScalarGridSpec(
            num_scalar_prefetch=1,
            grid=(B, L // tm),
            in_specs=[pl.BlockSpec((1, 2, tm), lambda b, i, st: (b, 0, i), memory_space=pltpu.SMEM),
                      pl.BlockSpec((1, 2, tm), next_block, memory_space=pltpu.SMEM),
                      pl.BlockSpec((1, tm, 2), lambda b, i, st: (b, i, 0)),
                      row, gt_spec,
                      pl.BlockSpec((1, D), lambda b, i, st: (0, 0)),
                      pl.BlockSpec(memory_space=pl.ANY)],
            out_specs=row,
            scratch_shapes=[pltpu.VMEM((2, 2, tm, D), F32), pltpu.SemaphoreType.DMA((2,))],
        ),
        compiler_params=_cp(("arbitrary", "arbitrary")),
        name="moe_combine",
    )(start, codes, codes, wts.transpose(0, 2, 1), x, mod, gfin, ys)


def _moe_routed(groups, b_r, w_gu, w_down, gfin, *, l, final):
    D = groups[0][1].shape[-1]
    n_tok = sum(g[1].shape[0] * g[1].shape[1] for g in groups)
    cap = 1 << (n_tok - 1).bit_length()
    n_groups = 2 * n_tok // MOE_SUP + N_EXPERTS
    cnt = jnp.zeros((N_EXPERTS, LANES), F32)
    routed = []
    for _, _, _, lgT in groups:
        codes, wts, cnt = _route_slots(lgT, b_r, cnt, cap)
        routed.append((codes, wts))
    ge, gn, start = _moe_groups(cnt[:, 0].astype(jnp.int32), n_groups)
    xs = _zero_rows(n_groups * MOE_SUP, D)
    for (h, _, _, _), (codes, _) in zip(groups, routed):
        xs = _dispatch(start, codes, h, xs, cap)
    ys = _moe_grouped(xs, ge, gn, w_gu, w_down)
    return [_combine(start, codes, wts, x, mod, gfin, ys, l=l, cap=cap, final=final)
            for (_, x, mod, _), (codes, wts) in zip(groups, routed)]


def _pad_lanes(v):
    return jnp.pad(v.reshape(1, -1), ((0, 0), (0, LANES - v.shape[-1])))


def _mixer_layer(x, mod, states, p, s5m, l, prev_sg, *, seq):
    B, L, D = x.shape
    hi = not seq
    s5r0, s5i0, sg0, sc0 = states
    w_in, w_gates, w_ab = p['w_in_seq' if seq else 'w_in']
    conv = (p['gdn_conv_w'], jnp.zeros((B, GDN_CONV - 1, QKV_WIDTH), F32)) if seq else ()
    u, qkv, z, ga, gb, ab, *tail = _proj(x, p['g_mix'], mod, w_in, w_gates, w_ab, *conv,
                                         l=l, tm=min(1024, L), hi=hi, chunked=seq)
    alog = _pad_lanes(p['gdn_a_log'][l])
    dtb = _pad_lanes(p['gdn_dt_bias'][l])
    nw = p['gdn_norm_w'][l].reshape(1, GDN_DK)
    if seq:
        yg, sfin = _s5_seq(u, s5m['be'], s5m['tp'], s5m['cpm'], s5m['pt'],
                           jnp.zeros((SLABS, B, 1, 2 * SLAB_STATE), F32), s5m['dsk'][l], l)
        sfin = sfin.reshape(SLABS, B, 2, SLAB_STATE).transpose(2, 1, 0, 3)
        sr = sfin[0].reshape(B, S5_GROUPS, S5_STATE)
        si = sfin[1].reshape(B, S5_GROUPS, S5_STATE)
        og, sg = _gdn_seq(qkv, z, ab, alog, dtb, nw, jnp.zeros((B, GDN_HEADS, GDN_DK, GDN_DK), F32))
        cb = tail[0][:, -1, 8 - (GDN_CONV - 1):, :]
    else:
        n = L
        s0 = jnp.concatenate([s5r0[l].reshape(n, SLABS, SLAB_STATE),
                              s5i0[l].reshape(n, SLABS, SLAB_STATE)], axis=-1).transpose(1, 0, 2)
        yg, s1 = _s5_step(u.reshape(SLABS, n, LANES), s5m['bst'], s5m['cpe'], s5m['a1'],
                          s0, s5m['d1'][l], l)
        yg = yg.reshape(SLABS, 1, n, LANES)
        s1 = s1.transpose(1, 0, 2)
        sr = s1[:, :, :SLAB_STATE].reshape(n, S5_GROUPS, S5_STATE)
        si = s1[:, :, SLAB_STATE:].reshape(n, S5_GROUPS, S5_STATE)
        og, sg = _gdn_step(qkv.reshape(n, QKV_WIDTH), z.reshape(n, GDN_WIDTH), ab.reshape(n, LANES),
                           p['gdn_conv_w'][l], alog, dtb, nw,
                           sc0[l].reshape(n, (GDN_CONV - 1) * QKV_WIDTH), sg0, l, prev_sg)
        og = og.reshape(1, n, GDN_WIDTH)
        cb = jnp.concatenate([sc0[l][:, 1:, :], qkv.reshape(n, 1, QKV_WIDTH)], axis=1)
    x, h, lgT = _merge(yg, og, ga, gb, x, mod, p['w_s5_glu'], p['w_gdn_out'], p['w_out'],
                       p['g_ffn'], p['w_router'], l=l, tm=min(512, L), hi=hi, chunked=seq,
                       h_dtype=BF16 if (seq and l % 2 == 0) else F32)
    return x, h, lgT, (sr, si, sg, cb)


def kernel(x_prompt, x_sample, c_prompt, c_sample, state_s5_re, state_s5_im, state_gdn, state_conv,
           g_mix, g_ffn, g_final, w_ada, b_ada, w_in, s5_lambda_re, s5_lambda_im, s5_log_dt,
           s5_b_re, s5_b_im, s5_c_re, s5_c_im, s5_d, w_s5_glu, gdn_conv_w, gdn_a_log, gdn_dt_bias,
           gdn_norm_w, w_gdn_out, w_out, w_ffn_gate_up, w_ffn_down, w_router, b_router,
           w_exp_gate_up, w_exp_down):
    def in_proj_parts(w):
        return w, w[:, :, 2568:], jnp.pad(w[:, :, 2560:2568], ((0, 0), (0, 0), (0, LANES - 8)))

    D_ = x_prompt.shape[-1]
    p = dict(g_mix=g_mix.reshape(DEPTH, 1, D_), g_ffn=g_ffn.reshape(DEPTH, 1, D_), w_s5_glu=w_s5_glu,
             gdn_conv_w=gdn_conv_w, gdn_a_log=gdn_a_log, gdn_dt_bias=gdn_dt_bias,
             gdn_norm_w=gdn_norm_w, w_gdn_out=w_gdn_out, w_out=w_out,
             w_router=w_router.transpose(0, 2, 1), w_in=in_proj_parts(w_in),
             w_in_seq=in_proj_parts(w_in.astype(BF16)))
    nbp, L, D = x_prompt.shape
    nbs = x_sample.shape[0]

    mod = _ada(jnp.concatenate([c_prompt, c_sample], axis=0), w_ada, b_ada)
    mod_p = mod[:, :nbp].reshape(DEPTH, nbp, 1, 6 * D)
    mod_s = mod[:, nbp:].reshape(DEPTH, 1, nbs, 6 * D)

    seg = L // S5_T // 8
    be, bst, cpe, cpm, pt, a1 = _s5_prep(s5_lambda_re, s5_lambda_im, s5_log_dt, s5_b_re, s5_b_im,
                                         s5_c_re, s5_c_im, seg)
    d1 = [s5_d[l].reshape(SLABS, 1, LANES) for l in range(DEPTH)]
    s5m = dict(be=be, bst=bst, cpe=cpe, cpm=cpm, pt=pt, a1=a1, tp=_toep(bst, cpe), d1=d1,
               dsk=[jnp.tile(d, (1, 1, S5_T)) for d in d1])

    xs_ = [x_prompt, x_sample.reshape(1, nbs, D)]
    mods = [mod_p, mod_s]
    states = [(None, None, None, None), (state_s5_re, state_s5_im, state_gdn, state_conv)]
    outs = [[], []]
    gfin = g_final.reshape(1, D)
    for l in range(DEPTH):
        final = l == DEPTH - 1
        mixed = []
        for gi, seq in enumerate((True, False)):
            prev_sg = outs[gi][0][2] if (not seq and final and DEPTH == 2) else None
            x, h, lgT, st = _mixer_layer(xs_[gi], mods[gi], states[gi], p, s5m, l, prev_sg, seq=seq)
            outs[gi].append(st)
            mixed.append((h, x, mods[gi], lgT))
        if l % 2 == 0:
            wgu, wdn = w_ffn_gate_up[l // 2], w_ffn_down[l // 2]
            xs_ = [_ffn(h, x, mod_g, wgu if gi else wgu.astype(BF16), wdn if gi else wdn.astype(BF16), gfin,
                        l=l, tm=min(1024, x.shape[1]), hi=(gi == 1), final=final)
                   for gi, (h, x, mod_g, _) in enumerate(mixed)]
        else:
            xs_ = _moe_routed(mixed, b_router[l // 2], w_exp_gate_up[l // 2], w_exp_down[l // 2], gfin,
                              l=l, final=final)
    y_p, y_s = xs_
    st_p = [jnp.stack([o[i] for o in outs[0]]) for i in range(4)]
    st_s = [outs[1][-1][2] if (i == 2 and DEPTH == 2) else jnp.stack([o[i] for o in outs[1]])
            for i in range(4)]
    return (y_p, y_s.reshape(nbs, 1, D), st_p[0], st_p[1], st_p[2], st_p[3],
            st_s[0], st_s[1], st_s[2], st_s[3])
```

```python
import functools

import jax
import jax.numpy as jnp
from jax import lax
from jax.experimental import pallas as pl
from jax.experimental.pallas import tpu as pltpu

F32 = jnp.float32
BF16 = jnp.bfloat16
HI = lax.Precision.HIGHEST

D_MODEL = 1024
DEPTH = 2
S5_WIDTH = 512
S5_GROUP = 16
S5_GROUPS = 32
S5_STATE = 64
GDN_HEADS = 4
GDN_DK = 128
GDN_WIDTH = 512
GDN_CONV = 4
QKV_WIDTH = 1536
D_FF = 3584
N_EXPERTS = 8
NORM_EPS = 1e-6
L2_EPS = 1e-6

LANES = 128
SLABS = S5_WIDTH // LANES
SLAB_STATE = (S5_GROUPS // SLABS) * S5_STATE
S5_T = 8
GDN_C = 128
VMEM_LIMIT = 56 * 1024 * 1024


def _cp(sem, vmem=VMEM_LIMIT):
    return pltpu.CompilerParams(dimension_semantics=sem, vmem_limit_bytes=vmem)


def _dot(a, b, prec=None):
    return jnp.dot(a, b, precision=prec, preferred_element_type=F32)


def _dotb(a, b):
    return jnp.dot(a.astype(BF16), b.astype(BF16), preferred_element_type=F32)


def _dot_nt(a, b, prec=None):
    return lax.dot_general(a, b, (((1,), (1,)), ((), ())), precision=prec,
                           preferred_element_type=F32)


def _dot_tn(a, b, prec=None):
    return lax.dot_general(a, b, (((0,), (0,)), ((), ())), precision=prec,
                           preferred_element_type=F32)


def _silu(x):
    return x * jax.nn.sigmoid(x)


def _ada_kernel(c_ref, w_ref, b_ref, o_ref):
    cs = _silu(c_ref[...])
    o_ref[0] = _dot(cs, w_ref[0], HI) + b_ref[0]


def _ada(c_all, w_ada, b_ada):
    n = c_all.shape[0]
    tn = 1536
    return pl.pallas_call(
        _ada_kernel,
        out_shape=jax.ShapeDtypeStruct((DEPTH, n, 6 * D_MODEL), F32),
        grid=(DEPTH, 6 * D_MODEL // tn),
        in_specs=[pl.BlockSpec((n, D_MODEL), lambda l, j: (0, 0)),
                  pl.BlockSpec((1, D_MODEL, tn), lambda l, j: (l, 0, j)),
                  pl.BlockSpec((1, 1, tn), lambda l, j: (l, 0, j))],
        out_specs=pl.BlockSpec((1, n, tn), lambda l, j: (l, 0, j)),
        compiler_params=_cp(("parallel", "parallel")),
        name="ada_mod",
    )(c_all, w_ada, b_ada.reshape(DEPTH, 1, 6 * D_MODEL))


def _proj_kernel(x_ref, g_ref, sc_ref, sh_ref, w_ref, wg_ref, wab_ref,
                 u_ref, qkv_ref, z_ref, ga_ref, gb_ref, ab_ref, h_scr, *us_scr, hi):
    j = pl.program_id(2)

    @pl.when(j == 0)
    def _():
        x = x_ref[0]
        ms = jnp.mean(x * x, axis=-1, keepdims=True)
        xn = x * lax.rsqrt(ms + NORM_EPS) * g_ref[0]
        h_scr[...] = (xn * (1.0 + sc_ref[0, 0]) + sh_ref[0, 0]).astype(h_scr.dtype)

    def mm(w):
        if hi:
            return _dot(h_scr[...], w, HI)
        return _dot(h_scr[...], w.astype(BF16))

    @pl.when(j == 0)
    def _():
        res = mm(w_ref[0])
        for k in range(SLABS):
            if not us_scr:
                u_ref[k, 0] = res[:, k * LANES:(k + 1) * LANES]
                continue
            us_scr[0][...] = res[:, k * LANES:(k + 1) * LANES]
            nrow = res.shape[0] // S5_T
            for t in range(S5_T):
                u_ref[k, 0, :, t * LANES:(t + 1) * LANES] = (
                    us_scr[0][pl.ds(t, nrow, stride=S5_T), :].astype(u_ref.dtype))

    @pl.when((j >= 1) & (j <= 3))
    def _():
        qkv_ref[0] = mm(w_ref[0]).astype(qkv_ref.dtype)

    @pl.when(j == 4)
    def _():
        z_ref[0] = mm(w_ref[0]).astype(z_ref.dtype)

    @pl.when((j == 5) | (j == 6))
    def _():
        ga_ref[0] = jax.nn.sigmoid(mm(wg_ref[0])).astype(ga_ref.dtype)

    @pl.when((j == 7) | (j == 8))
    def _():
        gb_ref[0] = jax.nn.sigmoid(mm(wg_ref[0])).astype(gb_ref.dtype)

    @pl.when(j == 9)
    def _():
        ab_ref[0] = mm(wab_ref[0])


def _mod_spec(mod, l, chunk, tm):
    per_row = mod.shape[2] != 1
    D = mod.shape[3] // 6

    def index(b, i, *_):
        return (l, b, i if per_row else 0, chunk)

    return pl.BlockSpec((1, 1, tm if per_row else 1, D), index)


def _proj(x, g, mod, w_in, w_gates, w_ab, *, l, tm, hi, chunked):
    B, L, D = x.shape
    tn = 512
    clampi = lambda j, lo, n: jnp.clip(j - lo, 0, n - 1)
    if chunked:
        u_shape = jax.ShapeDtypeStruct((SLABS, B, L // S5_T, S5_T * LANES), BF16)
        u_spec = pl.BlockSpec((SLABS, 1, tm // S5_T, S5_T * LANES), lambda b, i, j: (0, b, i, 0))
    else:
        u_shape = jax.ShapeDtypeStruct((SLABS, B, L, LANES), F32)
        u_spec = pl.BlockSpec((SLABS, 1, tm, LANES), lambda b, i, j: (0, b, i, 0))
    od = BF16 if chunked else F32
    outs = pl.pallas_call(
        functools.partial(_proj_kernel, hi=hi),
        out_shape=(u_shape,
                   jax.ShapeDtypeStruct((B, L, QKV_WIDTH), od),
                   jax.ShapeDtypeStruct((B, L, GDN_WIDTH), od),
                   jax.ShapeDtypeStruct((B, L, D), od),
                   jax.ShapeDtypeStruct((B, L, D), od),
                   jax.ShapeDtypeStruct((B, L, LANES), F32)),
        grid=(B, L // tm, 10),
        in_specs=[pl.BlockSpec((1, tm, D), lambda b, i, j: (b, i, 0)),
                  pl.BlockSpec((1, 1, D), lambda b, i, j: (l, 0, 0)),
                  _mod_spec(mod, l, 1, tm),
                  _mod_spec(mod, l, 0, tm),
                  pl.BlockSpec((1, D, tn), lambda b, i, j: (l, 0, jnp.minimum(j, 4))),
                  pl.BlockSpec((1, D, tn), lambda b, i, j: (l, 0, clampi(j, 5, 4))),
                  pl.BlockSpec((1, D, LANES), lambda b, i, j: (l, 0, 0))],
        out_specs=(u_spec,
                   pl.BlockSpec((1, tm, tn), lambda b, i, j: (b, i, clampi(j, 1, 3))),
                   pl.BlockSpec((1, tm, tn), lambda b, i, j: (b, i, 0)),
                   pl.BlockSpec((1, tm, tn), lambda b, i, j: (b, i, clampi(j, 5, 2))),
                   pl.BlockSpec((1, tm, tn), lambda b, i, j: (b, i, clampi(j, 7, 2))),
                   pl.BlockSpec((1, tm, LANES), lambda b, i, j: (b, i, 0))),
        scratch_shapes=[pltpu.VMEM((tm, D), F32 if hi else BF16)]
        + ([pltpu.VMEM((tm, LANES), F32)] if chunked else []),
        compiler_params=_cp(("parallel", "parallel", "arbitrary")),
        name="norm_in_proj",
    )(x, g, mod, mod, w_in, w_gates, w_ab)
    return outs


GROUPS_PER_SLAB = S5_GROUPS // SLABS


def _s5_prep_kernel(lrb, lib, dtb, bre, bim, lrc, lic, dtc, cre, cim, lrn, lin, dtn,
                    be_ref, bst_ref, cpe_ref, cpm_ref, pt_ref, a1_ref, *, seg):
    W = SLAB_STATE

    def disc(lr, li, ldt):
        dt = jnp.exp(ldt)
        mag = jnp.exp(lr * dt)
        return mag * jnp.cos(li * dt), mag * jnp.sin(li * dt)

    def cmul(xr, xi, yr, yi):
        return xr * yr - xi * yi, xr * yi + xi * yr

    lr, li = lrb[0], lib[0]
    ar, ai = disc(lr, li, dtb[0])
    den = lr * lr + li * li
    nr = ar - 1.0
    kr = (nr * lr + ai * li) / den
    ki = (ai * lr - nr * li) / den
    br, bi = bre[0], bim[0]
    bbr = kr * br - ki * bi
    bbi = kr * bi + ki * br
    rgrp = lax.broadcasted_iota(jnp.int32, (LANES, LANES), 0) // S5_GROUP
    lane_hi = lax.broadcasted_iota(jnp.int32, (LANES, LANES), 1) // S5_STATE
    pr, pi = jnp.ones_like(ar), jnp.zeros_like(ar)
    for d in range(S5_T):
        t = S5_T - 1 - d
        for ri, val in enumerate(cmul(pr, pi, bbr, bbi)):
            two = jnp.concatenate([val, val], axis=1)
            for m in range(GROUPS_PER_SLAB // 2):
                tile = jnp.where(rgrp == 2 * m + lane_hi, two, 0.0)
                c0 = ri * W + m * LANES
                be_ref[0, 0, t * LANES:(t + 1) * LANES, c0:c0 + LANES] = tile.astype(BF16)
                if d == 0:
                    bst_ref[0, 0, :, c0:c0 + LANES] = tile
        pr, pi = cmul(pr, pi, ar, ai)

    ar, ai = disc(lrc[0], lic[0], dtc[0])
    cr, ci = cre[0], cim[0]
    own = (lax.broadcasted_iota(jnp.int32, (W, LANES), 0) // S5_STATE
           == lax.broadcasted_iota(jnp.int32, (W, LANES), 1) // S5_GROUP)
    pr, pi = jnp.ones_like(ar), jnp.zeros_like(ar)
    for d in range(S5_T + 1):
        vr, vi = cmul(cr, ci, pr, pi)
        for ri, val in enumerate((vr, -vi)):
            tile = jnp.where(own, val, 0.0)
            cpe_ref[0, 0, d, ri * W:(ri + 1) * W, :] = tile
            if d >= 1:
                cpm_ref[0, 0, ri * W:(ri + 1) * W, (d - 1) * LANES:d * LANES] = tile.astype(BF16)
        pr, pi = cmul(pr, pi, ar, ai)

    ar, ai = disc(lrn[0, 0], lin[0, 0], dtn[0, 0])
    a1_ref[0, 0, :, 0:W] = ar
    a1_ref[0, 0, :, W:2 * W] = ai
    tr, ti = ar, ai
    for _ in range(S5_T - 1):
        tr, ti = cmul(tr, ti, ar, ai)
    pr, pi = jnp.ones_like(ar), jnp.zeros_like(ar)
    for i in range(seg + 1):
        pt_ref[0, 0, i:i + 1, 0:W] = pr
        pt_ref[0, 0, i:i + 1, W:2 * W] = pi
        pr, pi = cmul(pr, pi, tr, ti)


def _s5_prep(lam_re, lam_im, log_dt, b_re, b_im, c_re, c_im, seg):
    G, P, C = S5_GROUPS, S5_STATE, S5_GROUP
    W2 = 2 * SLAB_STATE
    dt3 = jnp.broadcast_to(log_dt[:, :, None], (DEPTH, G, P))
    rows_b = lambda a: jnp.repeat(a, C, axis=1)
    bt = lambda a: a.transpose(0, 1, 3, 2).reshape(DEPTH, G * C, P)
    rows_c = lambda a: jnp.broadcast_to(a.reshape(DEPTH, G * P, 1), (DEPTH, G * P, LANES))
    ct = lambda a: jnp.tile(a.transpose(0, 1, 3, 2).reshape(DEPTH, G * P, C), (1, 1, LANES // C))
    nat = lambda a: a.reshape(DEPTH, SLABS, 1, SLAB_STATE)
    args = (rows_b(lam_re), rows_b(lam_im), rows_b(dt3), bt(b_re), bt(b_im),
            rows_c(lam_re), rows_c(lam_im), rows_c(dt3), ct(c_re), ct(c_im),
            nat(lam_re), nat(lam_im), nat(dt3))
    bspec = pl.BlockSpec((1, LANES, P), lambda l, k: (l, k, 0))
    cspec = pl.BlockSpec((1, SLAB_STATE, LANES), lambda l, k: (l, k, 0))
    nspec = pl.BlockSpec((1, 1, 1, SLAB_STATE), lambda l, k: (l, k, 0, 0))
    return pl.pallas_call(
        functools.partial(_s5_prep_kernel, seg=seg),
        out_shape=(jax.ShapeDtypeStruct((DEPTH, SLABS, S5_T * LANES, W2), BF16),
                   jax.ShapeDtypeStruct((DEPTH, SLABS, LANES, W2), F32),
                   jax.ShapeDtypeStruct((DEPTH, SLABS, S5_T + 1, W2, LANES), F32),
                   jax.ShapeDtypeStruct((DEPTH, SLABS, W2, S5_T * LANES), BF16),
                   jax.ShapeDtypeStruct((DEPTH, SLABS, seg + 1, W2), F32),
                   jax.ShapeDtypeStruct((DEPTH, SLABS, 1, W2), F32)),
        grid=(DEPTH, SLABS),
        in_specs=[bspec] * 5 + [cspec] * 5 + [nspec] * 3,
        out_specs=(pl.BlockSpec((1, 1, S5_T * LANES, W2), lambda l, k: (l, k, 0, 0)),
                   pl.BlockSpec((1, 1, LANES, W2), lambda l, k: (l, k, 0, 0)),
                   pl.BlockSpec((1, 1, S5_T + 1, W2, LANES), lambda l, k: (l, k, 0, 0, 0)),
                   pl.BlockSpec((1, 1, W2, S5_T * LANES), lambda l, k: (l, k, 0, 0)),
                   pl.BlockSpec((1, 1, seg + 1, W2), lambda l, k: (l, k, 0, 0)),
                   pl.BlockSpec((1, 1, 1, W2), lambda l, k: (l, k, 0, 0))),
        compiler_params=_cp(("parallel", "parallel")),
        name="s5_discretize",
    )(*args)


def _toep_kernel(b_ref, c_ref, o_ref):
    dd = pl.program_id(2)
    bst = b_ref[0, 0]
    lag = lambda d: _dot(bst, c_ref[0, 0, d], HI)
    k0 = lag(2 * dd)
    o_ref[0, 0, 0, 0:LANES, 0:LANES] = k0.astype(BF16)
    o_ref[0, 0, 0, LANES:, LANES:] = k0.astype(BF16)
    o_ref[0, 0, 0, 0:LANES, LANES:] = lag(2 * dd + 1).astype(BF16)
    km = lag(jnp.maximum(2 * dd - 1, 0))
    o_ref[0, 0, 0, LANES:, 0:LANES] = jnp.where(dd > 0, km, 0.0).astype(BF16)


def _toep(bst, cpe):
    W2 = 2 * SLAB_STATE
    return pl.pallas_call(
        _toep_kernel,
        out_shape=jax.ShapeDtypeStruct((DEPTH, SLABS, S5_T // 2, 2 * LANES, 2 * LANES), BF16),
        grid=(DEPTH, SLABS, S5_T // 2),
        in_specs=[pl.BlockSpec((1, 1, LANES, W2), lambda l, k, d: (l, k, 0, 0)),
                  pl.BlockSpec((1, 1, S5_T + 1, W2, LANES), lambda l, k, d: (l, k, 0, 0, 0))],
        out_specs=pl.BlockSpec((1, 1, 1, 2 * LANES, 2 * LANES), lambda l, k, d: (l, k, d, 0, 0)),
        compiler_params=_cp(("parallel", "parallel", "parallel")),
        name="s5_conv_blocks",
    )(bst, cpe)


def _s5_seq_kernel(up_ref, be_ref, tp_ref, cpm_ref, pt_ref, s0_ref, dsk_ref,
                   yg_ref, sfin_ref, e_scr, sx_scr, *, nc):
    seg = nc // 8
    W = SLAB_STATE
    nt = W // LANES
    ub = up_ref[0, 0]
    u = ub.astype(F32)
    e = _dot(ub, be_ref[0, 0])
    for c in range(2 * nt):
        e_scr[c] = e[:, c * LANES:(c + 1) * LANES]

    def tiles(row):
        return [(row[:, c * LANES:(c + 1) * LANES], row[:, W + c * LANES:W + (c + 1) * LANES])
                for c in range(nt)]

    a8 = [(jnp.broadcast_to(r, (8, LANES)), jnp.broadcast_to(i, (8, LANES)))
          for r, i in tiles(pt_ref[0, 0, 1:2, :])]

    def step(i, carry):
        rows = pl.ds(i, 8, stride=seg)
        new = []
        for c in range(nt):
            sr, si = carry[c]
            ar, ai = a8[c]
            sx_scr[c, rows, :] = sr
            sx_scr[nt + c, rows, :] = si
            new.append((ar * sr - ai * si + e_scr[c, rows, :],
                        ar * si + ai * sr + e_scr[nt + c, rows, :]))
        return tuple(new)

    zero = jnp.zeros((8, LANES), F32)
    ends = lax.fori_loop(0, seg, step, tuple((zero, zero) for _ in range(nt)))

    al = tiles(pt_ref[0, 0, seg:seg + 1, :])
    cur = tiles(s0_ref[0, 0])
    car = []
    for c in range(nt):
        alr, ali = al[c]
        cr, ci = cur[c]
        sr, si = ends[c]
        crs, cis = [], []
        for j in range(8):
            crs.append(cr)
            cis.append(ci)
            cr, ci = (alr * cr - ali * ci + sr[j:j + 1], alr * ci + ali * cr + si[j:j + 1])
        sfin_ref[0, 0, :, c * LANES:(c + 1) * LANES] = cr
        sfin_ref[0, 0, :, W + c * LANES:W + (c + 1) * LANES] = ci
        car.append((jnp.concatenate(crs, axis=0), jnp.concatenate(cis, axis=0)))

    def corr(i, _):
        rows = pl.ds(i, 8, stride=seg)
        pw = tiles(pt_ref[0, 0, pl.ds(i, 1), :])
        for c in range(nt):
            pr, pi = pw[c]
            cr, ci = car[c]
            sx_scr[c, rows, :] = sx_scr[c, rows, :] + (pr * cr - pi * ci)
            sx_scr[nt + c, rows, :] = sx_scr[nt + c, rows, :] + (pr * ci + pi * cr)
        return 0

    lax.fori_loop(0, seg, corr, 0)

    sx = jnp.concatenate([sx_scr[c] for c in range(2 * nt)], axis=-1)
    y = _dot(sx.astype(BF16), cpm_ref[0, 0])
    TW = 2 * LANES
    for tq in range(S5_T // 2):
        acc = y[:, tq * TW:(tq + 1) * TW]
        for tpi in range(tq + 1):
            acc = acc + _dot(ub[:, tpi * TW:(tpi + 1) * TW], tp_ref[0, 0, tq - tpi])
        acc = acc + dsk_ref[0, :, tq * TW:(tq + 1) * TW] * u[:, tq * TW:(tq + 1) * TW]
        yg_ref[0, 0, :, tq * TW:(tq + 1) * TW] = jax.nn.gelu(acc).astype(yg_ref.dtype)


def _s5_seq(up, be_emb, tp, cpm, pt, s0, dsk, l):
    _, B, nc, _ = up.shape
    seg = nc // 8
    W2 = 2 * SLAB_STATE
    yg, sfin = pl.pallas_call(
        functools.partial(_s5_seq_kernel, nc=nc),
        out_shape=(jax.ShapeDtypeStruct((SLABS, B, nc, S5_T * LANES), BF16),
                   jax.ShapeDtypeStruct((SLABS, B, 1, W2), F32)),
        grid=(SLABS, B),
        in_specs=[pl.BlockSpec((1, 1, nc, S5_T * LANES), lambda k, b: (k, b, 0, 0)),
                  pl.BlockSpec((1, 1, S5_T * LANES, W2), lambda k, b: (l, k, 0, 0)),
                  pl.BlockSpec((1, 1, S5_T // 2, 2 * LANES, 2 * LANES), lambda k, b: (l, k, 0, 0, 0)),
                  pl.BlockSpec((1, 1, W2, S5_T * LANES), lambda k, b: (l, k, 0, 0)),
                  pl.BlockSpec((1, 1, seg + 1, W2), lambda k, b: (l, k, 0, 0)),
                  pl.BlockSpec((1, 1, 1, W2), lambda k, b: (k, b, 0, 0)),
                  pl.BlockSpec((1, 1, S5_T * LANES), lambda k, b: (k, 0, 0))],
        out_specs=(pl.BlockSpec((1, 1, nc, S5_T * LANES), lambda k, b: (k, b, 0, 0)),
                   pl.BlockSpec((1, 1, 1, W2), lambda k, b: (k, b, 0, 0))),
        scratch_shapes=[pltpu.VMEM((W2 // LANES, nc, LANES), F32),
                        pltpu.VMEM((W2 // LANES, nc, LANES), F32)],
        compiler_params=_cp(("parallel", "parallel")),
        name="s5_seq",
    )(up, be_emb, tp, cpm, pt, s0, dsk)
    return yg, sfin


def _s5_step_kernel(u_ref, b_ref, c_ref, a_ref, s0_ref, d_ref, yg_ref, s1_ref):
    W = SLAB_STATE
    u = u_ref[0]
    bu = _dot(u, b_ref[0, 0], HI)
    ar = a_ref[0, 0, :, 0:W]
    ai = a_ref[0, 0, :, W:2 * W]
    sr = s0_ref[0, :, 0:W]
    si = s0_ref[0, :, W:2 * W]
    nr = ar * sr - ai * si + bu[:, 0:W]
    ni = ar * si + ai * sr + bu[:, W:2 * W]
    s1_ref[0, :, 0:W] = nr
    s1_ref[0, :, W:2 * W] = ni
    s1 = jnp.concatenate([nr, ni], axis=-1)
    y = _dot(s1, c_ref[0, 0, 0], HI) + d_ref[0] * u
    yg_ref[0] = jax.nn.gelu(y)


def _s5_step(u_slab, bst, cpe, a1, s0, d1, l):
    _, N, _ = u_slab.shape
    W2 = 2 * SLAB_STATE
    return pl.pallas_call(
        _s5_step_kernel,
        out_shape=(jax.ShapeDtypeStruct((SLABS, N, LANES), F32),
                   jax.ShapeDtypeStruct((SLABS, N, W2), F32)),
        grid=(SLABS,),
        in_specs=[pl.BlockSpec((1, N, LANES), lambda k: (k, 0, 0)),
                  pl.BlockSpec((1, 1, LANES, W2), lambda k: (l, k, 0, 0)),
                  pl.BlockSpec((1, 1, 1, W2, LANES), lambda k: (l, k, 0, 0, 0)),
                  pl.BlockSpec((1, 1, 1, W2), lambda k: (l, k, 0, 0)),
                  pl.BlockSpec((1, N, W2), lambda k: (k, 0, 0)),
                  pl.BlockSpec((1, 1, LANES), lambda k: (k, 0, 0))],
        out_specs=(pl.BlockSpec((1, N, LANES), lambda k: (k, 0, 0)),
                   pl.BlockSpec((1, N, W2), lambda k: (k, 0, 0))),
        compiler_params=_cp(("parallel",)),
        name="s5_step",
    )(u_slab, bst, cpe, a1, s0, d1)


def _l2n(x):
    return x * lax.rsqrt(jnp.sum(x * x, axis=-1, keepdims=True) + L2_EPS)


def _split_bf16(x):
    hi = x.astype(BF16)
    return hi, (x - hi.astype(F32)).astype(BF16)


def _unit_lower_solve(As, rhss):
    n = GDN_C
    row = lax.broadcasted_iota(jnp.int32, (n, n), 0)
    col = lax.broadcasted_iota(jnp.int32, (n, n), 1)
    eye = (row == col).astype(F32)
    same8 = (row // 8) == (col // 8)
    Qs = [jnp.where(same8, -A, 0.0) for A in As]
    invs = [eye + Q for Q in Qs]
    for _ in range(2):
        Qs = [_dotb(Q, Q) for Q in Qs]
        invs = [inv + _dotb(inv, Q) for inv, Q in zip(invs, Qs)]
    s = 8
    while s < n:
        sib = ((row // (2 * s)) == (col // (2 * s))) & ((row // s) != (col // s))
        offs = [jnp.where(sib, A, 0.0).astype(BF16) for A in As]
        invb = [inv.astype(BF16) for inv in invs]
        tmp = [_dot(off, ib) for off, ib in zip(offs, invb)]
        invs = [inv - _dot(ib, t.astype(BF16)) for inv, ib, t in zip(invs, invb, tmp)]
        s *= 2
    invb = [inv.astype(BF16) for inv in invs]
    x0s = [_dot(ib, rhs.astype(BF16)) for ib, rhs in zip(invb, rhss)]
    res = []
    for A, x0, rhs in zip(As, x0s, rhss):
        ah, al = _split_bf16(A)
        xh, xl = _split_bf16(x0)
        res.append(rhs - x0 - (_dot(ah, xh) + _dot(ah, xl) + _dot(al, xh)))
    return [x0 + _dot(ib, r.astype(BF16)) for x0, ib, r in zip(x0s, invb, res)]


def _gdn_tile(qc_scr, gc, beta, z_ref, nw, o_ref, s_scr, tl):
    C, DK, H = GDN_C, GDN_DK, GDN_HEADS
    nchunk = tl // C
    probs = [(c, h) for c in range(nchunk) for h in range(H)]
    row = lax.broadcasted_iota(jnp.int32, (C, C), 0)
    col = lax.broadcasted_iota(jnp.int32, (C, C), 1)
    tri = row >= col
    strict = row > col

    def blk(c, off):
        return qc_scr[c * C:(c + 1) * C, off:off + DK]

    q = [_l2n(blk(c, h * DK)) * (DK ** -0.5) for c, h in probs]
    k = [_l2n(blk(c, GDN_WIDTH + h * DK)) for c, h in probs]
    v = [blk(c, 2 * GDN_WIDTH + h * DK) for c, h in probs]
    gcb = [jnp.broadcast_to(gc[c * C:(c + 1) * C, h:h + 1], (C, DK)) for c, h in probs]
    bb = [jnp.broadcast_to(beta[c * C:(c + 1) * C, H + h:H + h + 1], (C, DK)) for c, h in probs]
    decay = []
    for g in gcb:
        diff = g - g.T
        decay.append(jnp.where(tri, jnp.exp(jnp.where(tri, diff, 0.0)), 0.0))
    kbf = [x.astype(BF16) for x in k]
    kb = [x * b for x, b in zip(k, bb)]
    A = [jnp.where(strict, _dot_nt(x.astype(BF16), y) * d, 0.0) for x, y, d in zip(kb, kbf, decay)]
    egc = [jnp.exp(g) for g in gcb]
    rhs = [jnp.concatenate([x * b, y * e], axis=-1) for x, b, y, e in zip(v, bb, kb, egc)]
    sol = _unit_lower_solve(A, rhs)
    attn = [jnp.where(tri, _dot_nt(x.astype(BF16), y) * d, 0.0).astype(BF16)
            for x, y, d in zip(q, kbf, decay)]
    glast = [g[C - 1:C, :] for g in gcb]
    wq = [jnp.concatenate([s[:, DK:], x * e], axis=0).astype(BF16) for s, x, e in zip(sol, q, egc)]
    kg = [(x * jnp.exp(gl - g)).astype(BF16) for x, gl, g in zip(k, glast, gcb)]

    for c in range(nchunk):
        ps = [c * H + h for h in range(H)]
        S = [s_scr[h] for h in range(H)]
        ws = [_dot(wq[p], S[h].astype(BF16)) for h, p in enumerate(ps)]
        v_new = [sol[p][:, 0:DK] - w[0:C] for p, w in zip(ps, ws)]
        vb = [x.astype(BF16) for x in v_new]
        o = [w[C:] + _dot(attn[p], x) for p, w, x in zip(ps, ws, vb)]
        for h, p in enumerate(ps):
            s_scr[h] = S[h] * jnp.exp(glast[p]) + _dot_tn(kg[p], vb[h])
            zh = z_ref[0, c * C:(c + 1) * C, h * DK:(h + 1) * DK].astype(F32)
            on = o[h] * lax.rsqrt(jnp.mean(o[h] * o[h], axis=-1, keepdims=True) + NORM_EPS) * nw
            o_ref[0, c * C:(c + 1) * C, h * DK:(h + 1) * DK] = (on * _silu(zh)).astype(o_ref.dtype)


def _gdn_seq_kernel(qkv_ref, z_ref, ab_ref, cw_ref, alog_ref, dtb_ref, nw_ref, conv0_ref, s0_ref,
                    o_ref, sfin_ref, xp_scr, qc_scr, s_scr, *, tl):
    lt = pl.program_id(1)

    @pl.when(lt == 0)
    def _():
        xp_scr[0:8, :] = jnp.zeros((8, QKV_WIDTH), F32)
        xp_scr[8 - (GDN_CONV - 1):8, :] = conv0_ref[0]
        s_scr[...] = s0_ref[0]

    xp_scr[8:8 + tl, :] = qkv_ref[0].astype(F32)
    conv = cw_ref[0, 0:1, :] * xp_scr[5:5 + tl, :]
    for j in range(1, GDN_CONV):
        conv = conv + cw_ref[0, j:j + 1, :] * xp_scr[5 + j:5 + j + tl, :]
    xp_scr[0:8, :] = xp_scr[tl:tl + 8, :]
    qc_scr[...] = _silu(conv)

    ab = ab_ref[0]
    g = -jnp.exp(alog_ref[...]) * jax.nn.softplus(ab + dtb_ref[...])
    beta = jax.nn.sigmoid(ab)
    row = lax.broadcasted_iota(jnp.int32, (tl, tl), 0)
    col = lax.broadcasted_iota(jnp.int32, (tl, tl), 1)
    csum = ((row >= col) & ((row // GDN_C) == (col // GDN_C))).astype(F32)
    gc = _dot(csum, g, HI)
    _gdn_tile(qc_scr, gc, beta, z_ref, nw_ref[...], o_ref, s_scr, tl)

    @pl.when(lt == pl.num_programs(1) - 1)
    def _():
        sfin_ref[0] = s_scr[...]


def _gdn_seq(qkv, z, ab, conv_w, alog, dtb, nw, conv0, s0, l):
    B, L, _ = qkv.shape
    tl = min(256, L)
    return pl.pallas_call(
        functools.partial(_gdn_seq_kernel, tl=tl),
        out_shape=(jax.ShapeDtypeStruct((B, L, GDN_WIDTH), BF16),
                   jax.ShapeDtypeStruct((B, GDN_HEADS, GDN_DK, GDN_DK), F32)),
        grid=(B, L // tl),
        in_specs=[pl.BlockSpec((1, tl, QKV_WIDTH), lambda b, i: (b, i, 0)),
                  pl.BlockSpec((1, tl, GDN_WIDTH), lambda b, i: (b, i, 0)),
                  pl.BlockSpec((1, tl, LANES), lambda b, i: (b, i, 0)),
                  pl.BlockSpec((1, GDN_CONV, QKV_WIDTH), lambda b, i: (l, 0, 0)),
                  pl.BlockSpec((1, LANES), lambda b, i: (0, 0)),
                  pl.BlockSpec((1, LANES), lambda b, i: (0, 0)),
                  pl.BlockSpec((1, GDN_DK), lambda b, i: (0, 0)),
                  pl.BlockSpec((1, GDN_CONV - 1, QKV_WIDTH), lambda b, i: (b, 0, 0)),
                  pl.BlockSpec((1, GDN_HEADS, GDN_DK, GDN_DK), lambda b, i: (b, 0, 0, 0))],
        out_specs=(pl.BlockSpec((1, tl, GDN_WIDTH), lambda b, i: (b, i, 0)),
                   pl.BlockSpec((1, GDN_HEADS, GDN_DK, GDN_DK), lambda b, i: (b, 0, 0, 0))),
        scratch_shapes=[pltpu.VMEM((tl + 8, QKV_WIDTH), F32),
                        pltpu.VMEM((tl, QKV_WIDTH), F32),
                        pltpu.VMEM((GDN_HEADS, GDN_DK, GDN_DK), F32)],
        compiler_params=_cp(("parallel", "arbitrary")),
        name="gdn_seq",
    )(qkv, z, ab, conv_w, alog, dtb, nw, conv0, s0)


GDN_STEP_ROWS = 8


def _gdn_step_kernel(qkv_ref, z_ref, ab_ref, cw_ref, alog_ref, dtb_ref, nw_ref, conv0_ref, s0_ref,
                     *rest):
    if len(rest) == 3:
        prev_ref, o_ref, s1_all = rest
        s1_all[0] = prev_ref[...]
        s1_ref = s1_all.at[1]
    else:
        o_ref, s1_all = rest
        s1_ref = s1_all
    nb = GDN_STEP_ROWS
    W = QKV_WIDTH
    conv = cw_ref[0:1, :] * conv0_ref[:, 0:W]
    conv = conv + cw_ref[1:2, :] * conv0_ref[:, W:2 * W]
    conv = conv + cw_ref[2:3, :] * conv0_ref[:, 2 * W:3 * W]
    conv = conv + cw_ref[3:4, :] * qkv_ref[...]
    qc = _silu(conv)
    ab = ab_ref[...]
    eg = jnp.exp(-jnp.exp(alog_ref[...]) * jax.nn.softplus(ab + dtb_ref[...]))
    beta = jax.nn.sigmoid(ab)
    eye = (lax.broadcasted_iota(jnp.int32, (GDN_DK, GDN_DK), 0)
           == lax.broadcasted_iota(jnp.int32, (GDN_DK, GDN_DK), 1)).astype(F32)
    for h in range(GDN_HEADS):
        q = _l2n(qc[:, h * GDN_DK:(h + 1) * GDN_DK]) * (GDN_DK ** -0.5)
        k = _l2n(qc[:, GDN_WIDTH + h * GDN_DK:GDN_WIDTH + (h + 1) * GDN_DK])
        v = qc[:, 2 * GDN_WIDTH + h * GDN_DK:2 * GDN_WIDTH + (h + 1) * GDN_DK]
        kT = _dot_nt(eye, k, HI)
        qT = _dot_nt(eye, q, HI)
        qk = jnp.sum(q * k, axis=-1, keepdims=True)
        for j in range(nb):
            S = s0_ref[0, j, h]
            kc = jnp.broadcast_to(kT[:, j:j + 1], (GDN_DK, GDN_DK))
            qcb = jnp.broadcast_to(qT[:, j:j + 1], (GDN_DK, GDN_DK))
            kS = jnp.sum(kc * S, axis=0, keepdims=True)
            qS = jnp.sum(qcb * S, axis=0, keepdims=True)
            egj = eg[j:j + 1, h:h + 1]
            bj = beta[j:j + 1, GDN_HEADS + h:GDN_HEADS + h + 1]
            v_new = bj * v[j:j + 1, :] - (bj * egj) * kS
            o = egj * qS + qk[j:j + 1, :] * v_new
            s1_ref[j, h] = S * egj + kc * v_new
            zh = z_ref[j:j + 1, h * GDN_DK:(h + 1) * GDN_DK]
            on = o * lax.rsqrt(jnp.mean(o * o, axis=-1, keepdims=True) + NORM_EPS) * nw_ref[...]
            o_ref[j:j + 1, h * GDN_DK:(h + 1) * GDN_DK] = on * _silu(zh)


def _gdn_step(qkv, z, ab, conv_w, alog, dtb, nw, conv0, s_all, l, prev):
    N = qkv.shape[0]
    nb = GDN_STEP_ROWS
    row = lambda w: pl.BlockSpec((nb, w), lambda i: (i, 0))
    const = lambda r, w: pl.BlockSpec((r, w), lambda i: (0, 0))
    sblk = (nb, GDN_HEADS, GDN_DK, GDN_DK)
    one = pl.BlockSpec(sblk, lambda i: (i, 0, 0, 0))
    ins = [qkv, z, ab, conv_w, alog, dtb, nw, conv0, s_all]
    in_specs = [row(QKV_WIDTH), row(GDN_WIDTH), row(LANES), const(GDN_CONV, QKV_WIDTH),
                const(1, LANES), const(1, LANES), const(1, GDN_DK), row(3 * QKV_WIDTH),
                pl.BlockSpec((1,) + sblk, lambda i: (l, i, 0, 0, 0))]
    if prev is None:
        s_shape, s_spec = jax.ShapeDtypeStruct((N,) + sblk[1:], F32), one
    else:
        assert DEPTH == 2 and l == 1
        ins.append(prev)
        in_specs.append(one)
        s_shape = jax.ShapeDtypeStruct((DEPTH, N) + sblk[1:], F32)
        s_spec = pl.BlockSpec((DEPTH,) + sblk, lambda i: (0, i, 0, 0, 0))
    return pl.pallas_call(
        _gdn_step_kernel,
        out_shape=(jax.ShapeDtypeStruct((N, GDN_WIDTH), F32), s_shape),
        grid=(N // nb,),
        in_specs=in_specs,
        out_specs=(row(GDN_WIDTH), s_spec),
        compiler_params=_cp(("parallel",)),
        name="gdn_step",
    )(*ins)


def _merge_kernel(yg_ref, og_ref, ga_ref, gb_ref, x_ref, gt_ref, wglu_ref, wgo_ref, wout_ref,
                  gf_ref, scf_ref, shf_ref, wr_ref,
                  xo_ref, h_ref, lg_ref, *scr, hi, chunked):
    if chunked:
        y_scr = scr[-1]
        scr = scr[:-1]
        nrow = y_scr.shape[1] // S5_T
        for k in range(SLABS):
            for t in range(S5_T):
                y_scr[k, pl.ds(t, nrow, stride=S5_T), :] = (
                    yg_ref[k, 0, :, t * LANES:(t + 1) * LANES].astype(F32))
        y = jnp.concatenate([y_scr[k] for k in range(SLABS)], axis=-1)
    else:
        y = jnp.concatenate([yg_ref[k, 0] for k in range(SLABS)], axis=-1)
    if hi:
        wglu, wgo, wout = wglu_ref[0], wgo_ref[0], wout_ref[0]
        mm = lambda a, w: _dot(a, w, HI)
    else:
        wglu_s, wgo_s, wout_s = scr

        @pl.when((pl.program_id(0) == 0) & (pl.program_id(1) == 0))
        def _():
            wglu_s[...] = wglu_ref[0].astype(BF16)
            wgo_s[...] = wgo_ref[0].astype(BF16)
            wout_s[...] = wout_ref[0].astype(BF16)

        wglu, wgo, wout = wglu_s[...], wgo_s[...], wout_s[...]
        mm = lambda a, w: _dot(a.astype(BF16), w)

    glu = mm(y, wglu)
    branch_a = glu[:, 0:D_MODEL] * jax.nn.sigmoid(glu[:, D_MODEL:])
    branch_b = mm(og_ref[0], wgo)
    merged = ga_ref[0].astype(F32) * branch_a + gb_ref[0].astype(F32) * branch_b
    out = mm(merged, wout)
    x = x_ref[0] + gt_ref[0, 0] * out
    xo_ref[0] = x
    ms = jnp.mean(x * x, axis=-1, keepdims=True)
    h = x * lax.rsqrt(ms + NORM_EPS) * gf_ref[0]
    h = h * (1.0 + scf_ref[0, 0]) + shf_ref[0, 0]
    h_ref[0] = h.astype(h_ref.dtype)
    lg_ref[0] = _dot_nt(wr_ref[0], h, HI)


def _merge(yg, og, ga, gb, x, mod, wglu, wgo, wout, gf, wr, *, l, tm, hi, chunked, h_dtype):
    B, L, D = x.shape
    row = lambda w: pl.BlockSpec((1, tm, w), lambda b, i: (b, i, 0))
    layer = lambda r, w, ll=l: pl.BlockSpec((1, r, w), lambda b, i: (ll, 0, 0))
    scratch = [] if hi else [pltpu.VMEM((S5_WIDTH, 2 * D), BF16), pltpu.VMEM((GDN_WIDTH, D), BF16),
                             pltpu.VMEM((D, D), BF16)]
    if chunked:
        scratch = scratch + [pltpu.VMEM((SLABS, tm, LANES), F32)]
        yg_spec = pl.BlockSpec((SLABS, 1, tm // S5_T, S5_T * LANES), lambda b, i: (0, b, i, 0))
    else:
        yg_spec = pl.BlockSpec((SLABS, 1, tm, LANES), lambda b, i: (0, b, i, 0))
    lg_shape = jax.ShapeDtypeStruct((B, N_EXPERTS, L), F32)
    lg_spec = pl.BlockSpec((1, N_EXPERTS, tm), lambda b, i: (b, 0, i))
    return pl.pallas_call(
        functools.partial(_merge_kernel, hi=hi, chunked=chunked),
        out_shape=(jax.ShapeDtypeStruct((B, L, D), F32),
                   jax.ShapeDtypeStruct((B, L, D), h_dtype),
                   lg_shape),
        grid=(B, L // tm),
        in_specs=[yg_spec,
                  row(GDN_WIDTH), row(D), row(D), row(D), _mod_spec(mod, l, 2, tm),
                  layer(S5_WIDTH, 2 * D), layer(GDN_WIDTH, D), layer(D, D),
                  layer(1, D), _mod_spec(mod, l, 4, tm), _mod_spec(mod, l, 3, tm),
                  layer(N_EXPERTS, D, l // 2)],
        out_specs=(row(D), row(D), lg_spec),
        scratch_shapes=scratch,
        compiler_params=_cp(("arbitrary", "arbitrary")),
        name="merge_out_proj",
    )(yg, og, ga, gb, x, mod, wglu, wgo, wout, gf, mod, mod, wr)


FF_TILE = 512


def _finish(x, gfin_ref, final):
    if not final:
        return x
    ms = jnp.mean(x * x, axis=-1, keepdims=True)
    return x * lax.rsqrt(ms + NORM_EPS) * gfin_ref[...]


def _ffn_kernel(h_ref, x_ref, gt_ref, wg_ref, wu_ref, wd_ref, gfin_ref, o_ref, acc_scr, *, hi, final):
    j = pl.program_id(2)
    if hi:
        h = h_ref[0]
        mm = lambda a, w: _dot(a, w, HI)
    else:
        h = h_ref[0].astype(BF16)
        mm = lambda a, w: _dot(a.astype(BF16), w.astype(BF16))
    act = _silu(mm(h, wg_ref[...])) * mm(h, wu_ref[...])
    part = mm(act, wd_ref[...])

    @pl.when(j == 0)
    def _():
        acc_scr[...] = part

    @pl.when(j > 0)
    def _():
        acc_scr[...] = acc_scr[...] + part

    @pl.when(j == pl.num_programs(2) - 1)
    def _():
        o_ref[0] = _finish(x_ref[0] + gt_ref[0, 0] * acc_scr[...], gfin_ref, final)


def _ffn(h, x, mod, w_gu, w_down, gfin, *, l, tm, hi, final):
    B, L, D = x.shape
    nj = D_FF // FF_TILE
    row = pl.BlockSpec((1, tm, D), lambda b, i, j: (b, i, 0))
    return pl.pallas_call(
        functools.partial(_ffn_kernel, hi=hi, final=final),
        out_shape=jax.ShapeDtypeStruct((B, L, D), F32),
        grid=(B, L // tm, nj),
        in_specs=[row, row, _mod_spec(mod, l, 5, tm),
                  pl.BlockSpec((D, FF_TILE), lambda b, i, j: (0, j)),
                  pl.BlockSpec((D, FF_TILE), lambda b, i, j: (0, nj + j)),
                  pl.BlockSpec((FF_TILE, D), lambda b, i, j: (j, 0)),
                  pl.BlockSpec((1, D), lambda b, i, j: (0, 0))],
        out_specs=row,
        scratch_shapes=[pltpu.VMEM((tm, D), F32)],
        compiler_params=_cp(("parallel", "parallel", "arbitrary")),
        name="ffn_dense",
    )(h, x, mod, w_gu, w_gu, w_down, gfin)


ROUTE_TM = 512
ROW_DMA_TM = 256
MOE_SUP = 2048
MOE_SUB = 512


def _route_kernel(lg_ref, br_ref, cnt0_ref, slot_ref, wt_ref, cnt_ref, carry_scr, *, cap):
    @pl.when((pl.program_id(0) == 0) & (pl.program_id(1) == 0))
    def _():
        carry_scr[...] = cnt0_ref[...]

    lg = lg_ref[0] + br_ref[...]
    tm = lg.shape[1]
    eidx = lax.broadcasted_iota(jnp.int32, lg.shape, 0)
    m1 = jnp.max(lg, axis=0, keepdims=True)
    i1 = jnp.min(jnp.where(lg == m1, eidx, N_EXPERTS), axis=0, keepdims=True)
    lg2 = jnp.where(eidx == i1, -jnp.inf, lg)
    m2 = jnp.max(lg2, axis=0, keepdims=True)
    i2 = jnp.min(jnp.where(lg2 == m2, eidx, N_EXPERTS), axis=0, keepdims=True)
    e2 = jnp.exp(m2 - m1)
    wt_ref[0, 0:1, :] = 1.0 / (1.0 + e2)
    wt_ref[0, 1:2, :] = e2 / (1.0 + e2)
    sel1 = eidx == i1
    sel2 = eidx == i2
    oh = jnp.where(sel1 | sel2, 1.0, 0.0)
    before = (lax.broadcasted_iota(jnp.int32, (tm, tm), 0)
              < lax.broadcasted_iota(jnp.int32, (tm, tm), 1)).astype(BF16)
    rank = carry_scr[:, 0:1] + _dot(oh.astype(BF16), before)
    r1 = jnp.sum(jnp.where(sel1, rank, 0.0), axis=0, keepdims=True).astype(jnp.int32)
    r2 = jnp.sum(jnp.where(sel2, rank, 0.0), axis=0, keepdims=True).astype(jnp.int32)
    slot_ref[0, 0:1, :] = i1 * cap + r1
    slot_ref[0, 1:2, :] = i2 * cap + r2
    carry_scr[...] = carry_scr[...] + jnp.sum(oh, axis=1, keepdims=True)
    cnt_ref[...] = carry_scr[...]


def _route_slots(lgT, b_r, cnt0, cap):
    B, E, L = lgT.shape
    tm = min(ROUTE_TM, L)
    return pl.pallas_call(
        functools.partial(_route_kernel, cap=cap),
        out_shape=(jax.ShapeDtypeStruct((B, 2, L), jnp.int32),
                   jax.ShapeDtypeStruct((B, 2, L), F32),
                   jax.ShapeDtypeStruct((E, LANES), F32)),
        grid=(B, L // tm),
        in_specs=[pl.BlockSpec((1, E, tm), lambda b, i: (b, 0, i)),
                  pl.BlockSpec((E, 1), lambda b, i: (0, 0)),
                  pl.BlockSpec((E, LANES), lambda b, i: (0, 0))],
        out_specs=(pl.BlockSpec((1, 2, tm), lambda b, i: (b, 0, i)),
                   pl.BlockSpec((1, 2, tm), lambda b, i: (b, 0, i)),
                   pl.BlockSpec((E, LANES), lambda b, i: (0, 0))),
        scratch_shapes=[pltpu.VMEM((E, LANES), F32)],
        compiler_params=_cp(("arbitrary", "arbitrary")),
        name="moe_route",
    )(lgT, b_r.reshape(E, 1), cnt0)


def _row_copy(src, dst, sem):
    return pltpu.make_async_copy(src, dst, sem)


def _slot_row(code, start_ref, cap):
    shift = cap.bit_length() - 1
    return start_ref[lax.shift_right_logical(code, shift)] + (code & (cap - 1))


def _zeros_kernel(o_ref):
    o_ref[...] = jnp.zeros_like(o_ref)


def _zero_rows(n_rows, width):
    return pl.pallas_call(
        _zeros_kernel,
        out_shape=jax.ShapeDtypeStruct((n_rows, width), F32),
        grid=(n_rows // MOE_SUP,),
        out_specs=pl.BlockSpec((MOE_SUP, width), lambda i: (i, 0)),
        compiler_params=_cp(("parallel",)),
        name="moe_zero_rows",
    )()


def _dispatch_kernel(start_ref, code_ref, h_ref, xs_in_ref, xs_ref, hbuf, sem, *, cap):
    del xs_in_ref
    tm = h_ref.shape[1]
    t = pl.program_id(0) * pl.num_programs(1) + pl.program_id(1)
    last = pl.num_programs(0) * pl.num_programs(1) - 1
    slot = t % 2
    hbuf[slot] = h_ref[0]

    def issue(r, _):
        for k in range(2):
            row = _slot_row(code_ref[0, k, r], start_ref, cap)
            _row_copy(hbuf.at[slot, pl.ds(r, 1), :], xs_ref.at[pl.ds(row, 1), :], sem.at[slot]).start()
        return 0

    lax.fori_loop(0, tm, issue, 0, unroll=8)

    def drain(sl):
        for k in range(2):
            _row_copy(hbuf.at[sl], xs_ref.at[pl.ds(0, tm), :], sem.at[sl]).wait()

    @pl.when(t > 0)
    def _():
        drain(1 - slot)

    @pl.when(t == last)
    def _():
        drain(slot)


def _dispatch(start, codes, h, xs, cap):
    B, L, D = h.shape
    n_rows = xs.shape[0]
    tm = min(ROW_DMA_TM, L)
    return pl.pallas_call(
        functools.partial(_dispatch_kernel, cap=cap),
        out_shape=jax.ShapeDtypeStruct((n_rows, D), F32),
        grid_spec=pltpu.PrefetchScalarGridSpec(
            num_scalar_prefetch=1,
            grid=(B, L // tm),
            in_specs=[pl.BlockSpec((1, 2, tm), lambda b, i, st: (b, 0, i), memory_space=pltpu.SMEM),
                      pl.BlockSpec((1, tm, D), lambda b, i, st: (b, i, 0)),
                      pl.BlockSpec(memory_space=pl.ANY)],
            out_specs=pl.BlockSpec(memory_space=pl.ANY),
            scratch_shapes=[pltpu.VMEM((2, tm, D), F32), pltpu.SemaphoreType.DMA((2,))],
        ),
        input_output_aliases={3: 0},
        compiler_params=_cp(("arbitrary", "arbitrary")),
        name="moe_dispatch",
    )(start, codes, h, xs)


def _moe_grp_kernel(ge_ref, gn_ref, x_ref, wg_ref, wu_ref, wd_ref, y_ref, xb_scr):
    g = pl.program_id(0)
    j = pl.program_id(1)
    nsub = gn_ref[g]
    wg = wg_ref[0].astype(BF16)
    wu = wu_ref[0].astype(BF16)
    wd = wd_ref[0].astype(BF16)
    nblk = MOE_SUP // MOE_SUB

    @pl.when(j == 0)
    def _():
        xb_scr[...] = x_ref[...].astype(BF16)
        y_ref[...] = jnp.zeros_like(y_ref)

    def block(s):
        rows = slice(s * MOE_SUB, (s + 1) * MOE_SUB)
        xb = xb_scr[rows, :]
        act = _silu(_dot(xb, wg)) * _dot(xb, wu)
        y_ref[rows, :] = y_ref[rows, :] + _dot(act.astype(BF16), wd)

    @pl.when(nsub == nblk)
    def _():
        for s in range(nblk):
            block(s)

    @pl.when(nsub < nblk)
    def _():
        for s in range(nblk - 1):
            pl.when(s < nsub)(functools.partial(block, s))


def _moe_groups(counts, n_groups):
    nsup = (counts + MOE_SUP - 1) // MOE_SUP
    ends = jnp.cumsum(nsup)
    first = ends - nsup
    total = ends[-1]
    g = jnp.arange(n_groups, dtype=jnp.int32)
    gc = jnp.minimum(g, total - 1)
    e_of = jnp.minimum(jnp.sum((gc[:, None] >= ends[None, :]).astype(jnp.int32), axis=1), N_EXPERTS - 1)
    left = counts[e_of] - (gc - first[e_of]) * MOE_SUP
    nsub = jnp.clip((left + MOE_SUB - 1) // MOE_SUB, 0, MOE_SUP // MOE_SUB)
    gn = jnp.where(g < total, nsub, 0).astype(jnp.int32)
    return e_of, gn, (first * MOE_SUP).astype(jnp.int32)


def _moe_grouped(xs, ge, gn, w_gu, w_down):
    D = xs.shape[1]
    nj = D_FF // FF_TILE
    ng = xs.shape[0] // MOE_SUP
    jj = lambda j, gn, g: jnp.where(gn[g] > 0, j, nj - 1)
    return pl.pallas_call(
        _moe_grp_kernel,
        out_shape=jax.ShapeDtypeStruct(xs.shape, F32),
        grid_spec=pltpu.PrefetchScalarGridSpec(
            num_scalar_prefetch=2,
            grid=(ng, nj),
            in_specs=[pl.BlockSpec((MOE_SUP, D), lambda g, j, ge, gn: (g, 0)),
                      pl.BlockSpec((1, D, FF_TILE), lambda g, j, ge, gn: (ge[g], 0, jj(j, gn, g))),
                      pl.BlockSpec((1, D, FF_TILE), lambda g, j, ge, gn: (ge[g], 0, nj + jj(j, gn, g))),
                      pl.BlockSpec((1, FF_TILE, D), lambda g, j, ge, gn: (ge[g], jj(j, gn, g), 0))],
            out_specs=pl.BlockSpec((MOE_SUP, D), lambda g, j, ge, gn: (g, 0)),
            scratch_shapes=[pltpu.VMEM((MOE_SUP, D), BF16)],
        ),
        compiler_params=_cp(("arbitrary", "arbitrary")),
        name="moe_experts",
    )(ge, gn, xs, w_gu, w_gu, w_down)


def _combine_kernel(start_ref, code_ref, next_ref, w_ref, x_ref, gt_ref, gfin_ref, ys_ref, o_ref, g_scr, sem,
                    *, cap, final):
    tm = x_ref.shape[1]
    t = pl.program_id(0) * pl.num_programs(1) + pl.program_id(1)
    last = pl.num_programs(0) * pl.num_programs(1) - 1
    slot = t % 2

    def gather(codes, sl):
        def issue(r, _):
            for k in range(2):
                row = _slot_row(codes[0, k, r], start_ref, cap)
                _row_copy(ys_ref.at[pl.ds(row, 1), :], g_scr.at[sl, k, pl.ds(r, 1), :], sem.at[sl]).start()
            return 0

        lax.fori_loop(0, tm, issue, 0, unroll=8)

    @pl.when(t == 0)
    def _():
        gather(code_ref, slot)

    @pl.when(t < last)
    def _():
        gather(next_ref, 1 - slot)

    for k in range(2):
        _row_copy(ys_ref.at[pl.ds(0, tm), :], g_scr.at[slot, k], sem.at[slot]).wait()
    w = w_ref[0]
    f = w[:, 0:1] * g_scr[slot, 0] + w[:, 1:2] * g_scr[slot, 1]
    o_ref[0] = _finish(x_ref[0] + gt_ref[0, 0] * f, gfin_ref, final)


def _combine(start, codes, wts, x, mod, gfin, ys, *, l, cap, final):
    B, L, D = x.shape
    tm = min(ROW_DMA_TM, L)
    gt_spec = _mod_spec(mod, l, 5, tm)
    row = pl.BlockSpec((1, tm, D), lambda b, i, st: (b, i, 0))
    nl = L // tm

    def next_block(b, i, st):
        t1 = jnp.minimum(b * nl + i + 1, B * nl - 1)
        return (t1 // nl, 0, t1 % nl)

    return pl.pallas_call(
        functools.partial(_combine_kernel, cap=cap, final=final),
        out_shape=jax.ShapeDtypeStruct((B, L, D), F32),
        grid_spec=pltpu.PrefetchScalarGridSpec(
            num_scalar_prefetch=1,
            grid=(B, L // tm),
            in_specs=[pl.BlockSpec((1, 2, tm), lambda b, i, st: (b, 0, i), memory_space=pltpu.SMEM),
                      pl.BlockSpec((1, 2, tm), next_block, memory_space=pltpu.SMEM),
                      pl.BlockSpec((1, tm, 2), lambda b, i, st: (b, i, 0)),
                      row, gt_spec,
                      pl.BlockSpec((1, D), lambda b, i, st: (0, 0)),
                      pl.BlockSpec(memory_space=pl.ANY)],
            out_specs=row,
            scratch_shapes=[pltpu.VMEM((2, 2, tm, D), F32), pltpu.SemaphoreType.DMA((2,))],
        ),
        compiler_params=_cp(("arbitrary", "arbitrary")),
        name="moe_combine",
    )(start, codes, codes, wts.transpose(0, 2, 1), x, mod, gfin, ys)


def _moe_routed(groups, b_r, w_gu, w_down, gfin, *, l, final):
    D = groups[0][1].shape[-1]
    n_tok = sum(g[1].shape[0] * g[1].shape[1] for g in groups)
    cap = 1 << (n_tok - 1).bit_length()
    n_groups = 2 * n_tok // MOE_SUP + N_EXPERTS
    cnt = jnp.zeros((N_EXPERTS, LANES), F32)
    routed = []
    for _, _, _, lgT in groups:
        codes, wts, cnt = _route_slots(lgT, b_r, cnt, cap)
        routed.append((codes, wts))
    ge, gn, start = _moe_groups(cnt[:, 0].astype(jnp.int32), n_groups)
    xs = _zero_rows(n_groups * MOE_SUP, D)
    for (h, _, _, _), (codes, _) in zip(groups, routed):
        xs = _dispatch(start, codes, h, xs, cap)
    ys = _moe_grouped(xs, ge, gn, w_gu, w_down)
    return [_combine(start, codes, wts, x, mod, gfin, ys, l=l, cap=cap, final=final)
            for (_, x, mod, _), (codes, wts) in zip(groups, routed)]


def _pad_lanes(v):
    return jnp.pad(v.reshape(1, -1), ((0, 0), (0, LANES - v.shape[-1])))


def _mixer_layer(x, mod, states, p, s5m, l, prev_sg, *, seq):
    B, L, D = x.shape
    hi = not seq
    s5r0, s5i0, sg0, sc0 = states
    w_in, w_gates, w_ab = p['w_in_seq' if seq else 'w_in']
    u, qkv, z, ga, gb, ab = _proj(x, p['g_mix'], mod, w_in, w_gates, w_ab,
                                  l=l, tm=min(1024, L), hi=hi, chunked=seq)
    alog = _pad_lanes(p['gdn_a_log'][l])
    dtb = _pad_lanes(p['gdn_dt_bias'][l])
    nw = p['gdn_norm_w'][l].reshape(1, GDN_DK)
    if seq:
        yg, sfin = _s5_seq(u, s5m['be'], s5m['tp'], s5m['cpm'], s5m['pt'],
                           jnp.zeros((SLABS, B, 1, 2 * SLAB_STATE), F32), s5m['dsk'][l], l)
        sfin = sfin.reshape(SLABS, B, 2, SLAB_STATE).transpose(2, 1, 0, 3)
        sr = sfin[0].reshape(B, S5_GROUPS, S5_STATE)
        si = sfin[1].reshape(B, S5_GROUPS, S5_STATE)
        og, sg = _gdn_seq(qkv, z, ab, p['gdn_conv_w'], alog, dtb, nw,
                          jnp.zeros((B, GDN_CONV - 1, QKV_WIDTH), F32),
                          jnp.zeros((B, GDN_HEADS, GDN_DK, GDN_DK), F32), l)
        cb = qkv[:, L - (GDN_CONV - 1):, :].astype(F32)
    else:
        n = L
        s0 = jnp.concatenate([s5r0[l].reshape(n, SLABS, SLAB_STATE),
                              s5i0[l].reshape(n, SLABS, SLAB_STATE)], axis=-1).transpose(1, 0, 2)
        yg, s1 = _s5_step(u.reshape(SLABS, n, LANES), s5m['bst'], s5m['cpe'], s5m['a1'],
                          s0, s5m['d1'][l], l)
        yg = yg.reshape(SLABS, 1, n, LANES)
        s1 = s1.transpose(1, 0, 2)
        sr = s1[:, :, :SLAB_STATE].reshape(n, S5_GROUPS, S5_STATE)
        si = s1[:, :, SLAB_STATE:].reshape(n, S5_GROUPS, S5_STATE)
        og, sg = _gdn_step(qkv.reshape(n, QKV_WIDTH), z.reshape(n, GDN_WIDTH), ab.reshape(n, LANES),
                           p['gdn_conv_w'][l], alog, dtb, nw,
                           sc0[l].reshape(n, (GDN_CONV - 1) * QKV_WIDTH), sg0, l, prev_sg)
        og = og.reshape(1, n, GDN_WIDTH)
        cb = jnp.concatenate([sc0[l][:, 1:, :], qkv.reshape(n, 1, QKV_WIDTH)], axis=1)
    x, h, lgT = _merge(yg, og, ga, gb, x, mod, p['w_s5_glu'], p['w_gdn_out'], p['w_out'],
                       p['g_ffn'], p['w_router'], l=l, tm=min(512, L), hi=hi, chunked=seq,
                       h_dtype=BF16 if (seq and l % 2 == 0) else F32)
    return x, h, lgT, (sr, si, sg, cb)


def kernel(x_prompt, x_sample, c_prompt, c_sample, state_s5_re, state_s5_im, state_gdn, state_conv,
           g_mix, g_ffn, g_final, w_ada, b_ada, w_in, s5_lambda_re, s5_lambda_im, s5_log_dt,
           s5_b_re, s5_b_im, s5_c_re, s5_c_im, s5_d, w_s5_glu, gdn_conv_w, gdn_a_log, gdn_dt_bias,
           gdn_norm_w, w_gdn_out, w_out, w_ffn_gate_up, w_ffn_down, w_router, b_router,
           w_exp_gate_up, w_exp_down):
    def in_proj_parts(w):
        return w, w[:, :, 2568:], jnp.pad(w[:, :, 2560:2568], ((0, 0), (0, 0), (0, LANES - 8)))

    D_ = x_prompt.shape[-1]
    p = dict(g_mix=g_mix.reshape(DEPTH, 1, D_), g_ffn=g_ffn.reshape(DEPTH, 1, D_), w_s5_glu=w_s5_glu,
             gdn_conv_w=gdn_conv_w, gdn_a_log=gdn_a_log, gdn_dt_bias=gdn_dt_bias,
             gdn_norm_w=gdn_norm_w, w_gdn_out=w_gdn_out, w_out=w_out,
             w_router=w_router.transpose(0, 2, 1), w_in=in_proj_parts(w_in),
             w_in_seq=in_proj_parts(w_in.astype(BF16)))
    nbp, L, D = x_prompt.shape
    nbs = x_sample.shape[0]

    mod = _ada(jnp.concatenate([c_prompt, c_sample], axis=0), w_ada, b_ada)
    mod_p = mod[:, :nbp].reshape(DEPTH, nbp, 1, 6 * D)
    mod_s = mod[:, nbp:].reshape(DEPTH, 1, nbs, 6 * D)

    seg = L // S5_T // 8
    be, bst, cpe, cpm, pt, a1 = _s5_prep(s5_lambda_re, s5_lambda_im, s5_log_dt, s5_b_re, s5_b_im,
                                         s5_c_re, s5_c_im, seg)
    d1 = [s5_d[l].reshape(SLABS, 1, LANES) for l in range(DEPTH)]
    s5m = dict(be=be, bst=bst, cpe=cpe, cpm=cpm, pt=pt, a1=a1, tp=_toep(bst, cpe), d1=d1,
               dsk=[jnp.tile(d, (1, 1, S5_T)) for d in d1])

    xs_ = [x_prompt, x_sample.reshape(1, nbs, D)]
    mods = [mod_p, mod_s]
    states = [(None, None, None, None), (state_s5_re, state_s5_im, state_gdn, state_conv)]
    outs = [[], []]
    gfin = g_final.reshape(1, D)
    for l in range(DEPTH):
        final = l == DEPTH - 1
        mixed = []
        for gi, seq in enumerate((True, False)):
            prev_sg = outs[gi][0][2] if (not seq and final and DEPTH == 2) else None
            x, h, lgT, st = _mixer_layer(xs_[gi], mods[gi], states[gi], p, s5m, l, prev_sg, seq=seq)
            outs[gi].append(st)
            mixed.append((h, x, mods[gi], lgT))
        if l % 2 == 0:
            wgu, wdn = w_ffn_gate_up[l // 2], w_ffn_down[l // 2]
            xs_ = [_ffn(h, x, mod_g, wgu if gi else wgu.astype(BF16), wdn if gi else wdn.astype(BF16), gfin,
                        l=l, tm=min(1024, x.shape[1]), hi=(gi == 1), final=final)
                   for gi, (h, x, mod_g, _) in enumerate(mixed)]
        else:
            xs_ = _moe_routed(mixed, b_router[l // 2], w_exp_gate_up[l // 2], w_exp_down[l // 2], gfin,
                              l=l, final=final)
    y_p, y_s = xs_
    st_p = [jnp.stack([o[i] for o in outs[0]]) for i in range(4)]
    st_s = [outs[1][-1][2] if (i == 2 and DEPTH == 2) else jnp.stack([o[i] for o in outs[1]])
            for i in range(4)]
    return (y_p, y_s.reshape(nbs, 1, D), st_p[0], st_p[1], st_p[2], st_p[3],
            st_s[0], st_s[1], st_s[2], st_s[3])
```

```python
import functools

import jax
import jax.numpy as jnp
from jax import lax
from jax.experimental import pallas as pl
from jax.experimental.pallas import tpu as pltpu

F32 = jnp.float32
BF16 = jnp.bfloat16
HI = lax.Precision.HIGHEST

D_MODEL = 1024
DEPTH = 2
S5_WIDTH = 512
S5_GROUP = 16
S5_GROUPS = 32
S5_STATE = 64
GDN_HEADS = 4
GDN_DK = 128
GDN_WIDTH = 512
GDN_CONV = 4
QKV_WIDTH = 1536
D_FF = 3584
N_EXPERTS = 8
NORM_EPS = 1e-6
L2_EPS = 1e-6

LANES = 128
SLABS = S5_WIDTH // LANES
SLAB_STATE = (S5_GROUPS // SLABS) * S5_STATE
S5_T = 8
GDN_C = 128
VMEM_LIMIT = 56 * 1024 * 1024


def _cp(sem, vmem=VMEM_LIMIT):
    return pltpu.CompilerParams(dimension_semantics=sem, vmem_limit_bytes=vmem)


def _dot(a, b, prec=None):
    return jnp.dot(a, b, precision=prec, preferred_element_type=F32)


def _dotb(a, b):
    return jnp.dot(a.astype(BF16), b.astype(BF16), preferred_element_type=F32)


def _dot_nt(a, b, prec=None):
    return lax.dot_general(a, b, (((1,), (1,)), ((), ())), precision=prec,
                           preferred_element_type=F32)


def _dot_tn(a, b, prec=None):
    return lax.dot_general(a, b, (((0,), (0,)), ((), ())), precision=prec,
                           preferred_element_type=F32)


def _silu(x):
    return x * jax.nn.sigmoid(x)


def _ada_kernel(c_ref, w_ref, b_ref, o_ref):
    cs = _silu(c_ref[...])
    o_ref[0] = _dot(cs, w_ref[0], HI) + b_ref[0]


def _ada(c_all, w_ada, b_ada):
    n = c_all.shape[0]
    tn = 1536
    return pl.pallas_call(
        _ada_kernel,
        out_shape=jax.ShapeDtypeStruct((DEPTH, n, 6 * D_MODEL), F32),
        grid=(DEPTH, 6 * D_MODEL // tn),
        in_specs=[pl.BlockSpec((n, D_MODEL), lambda l, j: (0, 0)),
                  pl.BlockSpec((1, D_MODEL, tn), lambda l, j: (l, 0, j)),
                  pl.BlockSpec((1, 1, tn), lambda l, j: (l, 0, j))],
        out_specs=pl.BlockSpec((1, n, tn), lambda l, j: (l, 0, j)),
        compiler_params=_cp(("parallel", "parallel")),
        name="ada_mod",
    )(c_all, w_ada, b_ada.reshape(DEPTH, 1, 6 * D_MODEL))


def _proj_kernel(x_ref, g_ref, sc_ref, sh_ref, w_ref, wg_ref, wab_ref,
                 u_ref, qkv_ref, z_ref, ga_ref, gb_ref, ab_ref, h_scr):
    j = pl.program_id(2)

    @pl.when(j == 0)
    def _():
        x = x_ref[0]
        ms = jnp.mean(x * x, axis=-1, keepdims=True)
        xn = x * lax.rsqrt(ms + NORM_EPS) * g_ref[0]
        h_scr[...] = (xn * (1.0 + sc_ref[0, 0]) + sh_ref[0, 0]).astype(h_scr.dtype)

    def mm(w):
        return _dot(h_scr[...], w, HI)

    @pl.when(j == 0)
    def _():
        res = mm(w_ref[0])
        for k in range(SLABS):
            u_ref[k, 0] = res[:, k * LANES:(k + 1) * LANES]

    @pl.when((j >= 1) & (j <= 3))
    def _():
        qkv_ref[0] = mm(w_ref[0])

    @pl.when(j == 4)
    def _():
        z_ref[0] = mm(w_ref[0])

    @pl.when((j == 5) | (j == 6))
    def _():
        ga_ref[0] = jax.nn.sigmoid(mm(wg_ref[0]))

    @pl.when((j == 7) | (j == 8))
    def _():
        gb_ref[0] = jax.nn.sigmoid(mm(wg_ref[0]))

    @pl.when(j == 9)
    def _():
        ab_ref[0] = mm(wab_ref[0])


def _mod_spec(mod, l, chunk, tm):
    per_row = mod.shape[2] != 1
    D = mod.shape[3] // 6

    def index(b, i, *_):
        return (l, b, i if per_row else 0, chunk)

    return pl.BlockSpec((1, 1, tm if per_row else 1, D), index)


def _proj(x, g, mod, w_in, w_gates, w_ab, *, l, tm):
    B, L, D = x.shape
    tn = 512
    clampi = lambda j, lo, n: jnp.clip(j - lo, 0, n - 1)
    outs = pl.pallas_call(
        _proj_kernel,
        out_shape=(jax.ShapeDtypeStruct((SLABS, B, L, LANES), F32),
                   jax.ShapeDtypeStruct((B, L, QKV_WIDTH), F32),
                   jax.ShapeDtypeStruct((B, L, GDN_WIDTH), F32),
                   jax.ShapeDtypeStruct((B, L, D), F32),
                   jax.ShapeDtypeStruct((B, L, D), F32),
                   jax.ShapeDtypeStruct((B, L, LANES), F32)),
        grid=(B, L // tm, 10),
        in_specs=[pl.BlockSpec((1, tm, D), lambda b, i, j: (b, i, 0)),
                  pl.BlockSpec((1, 1, D), lambda b, i, j: (l, 0, 0)),
                  _mod_spec(mod, l, 1, tm),
                  _mod_spec(mod, l, 0, tm),
                  pl.BlockSpec((1, D, tn), lambda b, i, j: (l, 0, jnp.minimum(j, 4))),
                  pl.BlockSpec((1, D, tn), lambda b, i, j: (l, 0, clampi(j, 5, 4))),
                  pl.BlockSpec((1, D, LANES), lambda b, i, j: (l, 0, 0))],
        out_specs=(pl.BlockSpec((SLABS, 1, tm, LANES), lambda b, i, j: (0, b, i, 0)),
                   pl.BlockSpec((1, tm, tn), lambda b, i, j: (b, i, clampi(j, 1, 3))),
                   pl.BlockSpec((1, tm, tn), lambda b, i, j: (b, i, 0)),
                   pl.BlockSpec((1, tm, tn), lambda b, i, j: (b, i, clampi(j, 5, 2))),
                   pl.BlockSpec((1, tm, tn), lambda b, i, j: (b, i, clampi(j, 7, 2))),
                   pl.BlockSpec((1, tm, LANES), lambda b, i, j: (b, i, 0))),
        scratch_shapes=[pltpu.VMEM((tm, D), F32)],
        compiler_params=_cp(("parallel", "parallel", "arbitrary")),
        name="norm_in_proj",
    )(x, g, mod, mod, w_in, w_gates, w_ab)
    return outs


def _proj_seq_kernel(x_ref, g_ref, sc_ref, sh_ref, w_ref, wg_ref, wab_ref,
                     u_ref, qkv_ref, z_ref, ga_ref, gb_ref, ab_ref, us_scr):
    x = x_ref[0]
    ms = jnp.mean(x * x, axis=-1, keepdims=True)
    xn = x * lax.rsqrt(ms + NORM_EPS) * g_ref[0]
    h = (xn * (1.0 + sc_ref[0, 0]) + sh_ref[0, 0]).astype(BF16)
    res = _dot(h, w_ref[0, :, 0:S5_WIDTH])
    nrow = res.shape[0] // S5_T
    for k in range(SLABS):
        us_scr[...] = res[:, k * LANES:(k + 1) * LANES]
        for t in range(S5_T):
            u_ref[k, 0, :, t * LANES:(t + 1) * LANES] = (
                us_scr[pl.ds(t, nrow, stride=S5_T), :].astype(u_ref.dtype))
    c0 = S5_WIDTH
    qkv_ref[0] = _dot(h, w_ref[0, :, c0:c0 + QKV_WIDTH]).astype(qkv_ref.dtype)
    c0 += QKV_WIDTH
    z_ref[0] = _dot(h, w_ref[0, :, c0:c0 + GDN_WIDTH]).astype(z_ref.dtype)
    D = x.shape[-1]
    ga_ref[0] = jax.nn.sigmoid(_dot(h, wg_ref[0, :, 0:D])).astype(ga_ref.dtype)
    gb_ref[0] = jax.nn.sigmoid(_dot(h, wg_ref[0, :, D:2 * D])).astype(gb_ref.dtype)
    ab_ref[0] = _dot(h, wab_ref[0])


def _proj_seq(x, g, mod, w_in, w_gates, w_ab, *, l, tm):
    B, L, D = x.shape
    n_main = S5_WIDTH + QKV_WIDTH + GDN_WIDTH
    row = lambda w: pl.BlockSpec((1, tm, w), lambda b, i: (b, i, 0))
    return pl.pallas_call(
        _proj_seq_kernel,
        out_shape=(jax.ShapeDtypeStruct((SLABS, B, L // S5_T, S5_T * LANES), BF16),
                   jax.ShapeDtypeStruct((B, L, QKV_WIDTH), BF16),
                   jax.ShapeDtypeStruct((B, L, GDN_WIDTH), BF16),
                   jax.ShapeDtypeStruct((B, L, D), BF16),
                   jax.ShapeDtypeStruct((B, L, D), BF16),
                   jax.ShapeDtypeStruct((B, L, LANES), F32)),
        grid=(B, L // tm),
        in_specs=[row(D),
                  pl.BlockSpec((1, 1, D), lambda b, i: (l, 0, 0)),
                  _mod_spec(mod, l, 1, tm),
                  _mod_spec(mod, l, 0, tm),
                  pl.BlockSpec((1, D, n_main), lambda b, i: (l, 0, 0)),
                  pl.BlockSpec((1, D, 2 * D), lambda b, i: (l, 0, 0)),
                  pl.BlockSpec((1, D, LANES), lambda b, i: (l, 0, 0))],
        out_specs=(pl.BlockSpec((SLABS, 1, tm // S5_T, S5_T * LANES), lambda b, i: (0, b, i, 0)),
                   row(QKV_WIDTH), row(GDN_WIDTH), row(D), row(D), row(LANES)),
        scratch_shapes=[pltpu.VMEM((tm, LANES), F32)],
        compiler_params=_cp(("parallel", "parallel")),
        name="norm_in_proj_seq",
    )(x, g, mod, mod, w_in, w_gates, w_ab)


GROUPS_PER_SLAB = S5_GROUPS // SLABS


def _s5_prep_kernel(lrb, lib, dtb, bre, bim, lrc, lic, dtc, cre, cim, lrn, lin, dtn,
                    be_ref, bst_ref, cpe_ref, cpm_ref, pt_ref, a1_ref, *, seg):
    W = SLAB_STATE

    def disc(lr, li, ldt):
        dt = jnp.exp(ldt)
        mag = jnp.exp(lr * dt)
        return mag * jnp.cos(li * dt), mag * jnp.sin(li * dt)

    def cmul(xr, xi, yr, yi):
        return xr * yr - xi * yi, xr * yi + xi * yr

    lr, li = lrb[0], lib[0]
    ar, ai = disc(lr, li, dtb[0])
    den = lr * lr + li * li
    nr = ar - 1.0
    kr = (nr * lr + ai * li) / den
    ki = (ai * lr - nr * li) / den
    br, bi = bre[0], bim[0]
    bbr = kr * br - ki * bi
    bbi = kr * bi + ki * br
    rgrp = lax.broadcasted_iota(jnp.int32, (LANES, LANES), 0) // S5_GROUP
    lane_hi = lax.broadcasted_iota(jnp.int32, (LANES, LANES), 1) // S5_STATE
    pr, pi = jnp.ones_like(ar), jnp.zeros_like(ar)
    for d in range(S5_T):
        t = S5_T - 1 - d
        for ri, val in enumerate(cmul(pr, pi, bbr, bbi)):
            two = jnp.concatenate([val, val], axis=1)
            for m in range(GROUPS_PER_SLAB // 2):
                tile = jnp.where(rgrp == 2 * m + lane_hi, two, 0.0)
                c0 = ri * W + m * LANES
                be_ref[0, 0, t * LANES:(t + 1) * LANES, c0:c0 + LANES] = tile.astype(BF16)
                if d == 0:
                    bst_ref[0, 0, :, c0:c0 + LANES] = tile
        pr, pi = cmul(pr, pi, ar, ai)

    ar, ai = disc(lrc[0], lic[0], dtc[0])
    cr, ci = cre[0], cim[0]
    own = (lax.broadcasted_iota(jnp.int32, (W, LANES), 0) // S5_STATE
           == lax.broadcasted_iota(jnp.int32, (W, LANES), 1) // S5_GROUP)
    pr, pi = jnp.ones_like(ar), jnp.zeros_like(ar)
    for d in range(S5_T + 1):
        vr, vi = cmul(cr, ci, pr, pi)
        for ri, val in enumerate((vr, -vi)):
            tile = jnp.where(own, val, 0.0)
            cpe_ref[0, 0, d, ri * W:(ri + 1) * W, :] = tile
            if d >= 1:
                cpm_ref[0, 0, ri * W:(ri + 1) * W, (d - 1) * LANES:d * LANES] = tile.astype(BF16)
        pr, pi = cmul(pr, pi, ar, ai)

    ar, ai = disc(lrn[0, 0], lin[0, 0], dtn[0, 0])
    a1_ref[0, 0, :, 0:W] = ar
    a1_ref[0, 0, :, W:2 * W] = ai
    tr, ti = ar, ai
    for _ in range(S5_T - 1):
        tr, ti = cmul(tr, ti, ar, ai)
    pr, pi = jnp.ones_like(ar), jnp.zeros_like(ar)
    for i in range(seg + 1):
        pt_ref[0, 0, i:i + 1, 0:W] = pr
        pt_ref[0, 0, i:i + 1, W:2 * W] = pi
        pr, pi = cmul(pr, pi, tr, ti)


def _s5_prep(lam_re, lam_im, log_dt, b_re, b_im, c_re, c_im, seg):
    G, P, C = S5_GROUPS, S5_STATE, S5_GROUP
    W2 = 2 * SLAB_STATE
    dt3 = jnp.broadcast_to(log_dt[:, :, None], (DEPTH, G, P))
    rows_b = lambda a: jnp.repeat(a, C, axis=1)
    bt = lambda a: a.transpose(0, 1, 3, 2).reshape(DEPTH, G * C, P)
    rows_c = lambda a: jnp.broadcast_to(a.reshape(DEPTH, G * P, 1), (DEPTH, G * P, LANES))
    ct = lambda a: jnp.tile(a.transpose(0, 1, 3, 2).reshape(DEPTH, G * P, C), (1, 1, LANES // C))
    nat = lambda a: a.reshape(DEPTH, SLABS, 1, SLAB_STATE)
    args = (rows_b(lam_re), rows_b(lam_im), rows_b(dt3), bt(b_re), bt(b_im),
            rows_c(lam_re), rows_c(lam_im), rows_c(dt3), ct(c_re), ct(c_im),
            nat(lam_re), nat(lam_im), nat(dt3))
    bspec = pl.BlockSpec((1, LANES, P), lambda l, k: (l, k, 0))
    cspec = pl.BlockSpec((1, SLAB_STATE, LANES), lambda l, k: (l, k, 0))
    nspec = pl.BlockSpec((1, 1, 1, SLAB_STATE), lambda l, k: (l, k, 0, 0))
    return pl.pallas_call(
        functools.partial(_s5_prep_kernel, seg=seg),
        out_shape=(jax.ShapeDtypeStruct((DEPTH, SLABS, S5_T * LANES, W2), BF16),
                   jax.ShapeDtypeStruct((DEPTH, SLABS, LANES, W2), F32),
                   jax.ShapeDtypeStruct((DEPTH, SLABS, S5_T + 1, W2, LANES), F32),
                   jax.ShapeDtypeStruct((DEPTH, SLABS, W2, S5_T * LANES), BF16),
                   jax.ShapeDtypeStruct((DEPTH, SLABS, seg + 1, W2), F32),
                   jax.ShapeDtypeStruct((DEPTH, SLABS, 1, W2), F32)),
        grid=(DEPTH, SLABS),
        in_specs=[bspec] * 5 + [cspec] * 5 + [nspec] * 3,
        out_specs=(pl.BlockSpec((1, 1, S5_T * LANES, W2), lambda l, k: (l, k, 0, 0)),
                   pl.BlockSpec((1, 1, LANES, W2), lambda l, k: (l, k, 0, 0)),
                   pl.BlockSpec((1, 1, S5_T + 1, W2, LANES), lambda l, k: (l, k, 0, 0, 0)),
                   pl.BlockSpec((1, 1, W2, S5_T * LANES), lambda l, k: (l, k, 0, 0)),
                   pl.BlockSpec((1, 1, seg + 1, W2), lambda l, k: (l, k, 0, 0)),
                   pl.BlockSpec((1, 1, 1, W2), lambda l, k: (l, k, 0, 0))),
        compiler_params=_cp(("parallel", "parallel")),
        name="s5_discretize",
    )(*args)


def _toep_kernel(b_ref, c_ref, o_ref):
    dd = pl.program_id(2)
    bst = b_ref[0, 0]
    lag = lambda d: _dot(bst, c_ref[0, 0, d], HI)
    k0 = lag(2 * dd)
    o_ref[0, 0, 0, 0:LANES, 0:LANES] = k0.astype(BF16)
    o_ref[0, 0, 0, LANES:, LANES:] = k0.astype(BF16)
    o_ref[0, 0, 0, 0:LANES, LANES:] = lag(2 * dd + 1).astype(BF16)
    km = lag(jnp.maximum(2 * dd - 1, 0))
    o_ref[0, 0, 0, LANES:, 0:LANES] = jnp.where(dd > 0, km, 0.0).astype(BF16)


def _toep(bst, cpe):
    W2 = 2 * SLAB_STATE
    return pl.pallas_call(
        _toep_kernel,
        out_shape=jax.ShapeDtypeStruct((DEPTH, SLABS, S5_T // 2, 2 * LANES, 2 * LANES), BF16),
        grid=(DEPTH, SLABS, S5_T // 2),
        in_specs=[pl.BlockSpec((1, 1, LANES, W2), lambda l, k, d: (l, k, 0, 0)),
                  pl.BlockSpec((1, 1, S5_T + 1, W2, LANES), lambda l, k, d: (l, k, 0, 0, 0))],
        out_specs=pl.BlockSpec((1, 1, 1, 2 * LANES, 2 * LANES), lambda l, k, d: (l, k, d, 0, 0)),
        compiler_params=_cp(("parallel", "parallel", "parallel")),
        name="s5_conv_blocks",
    )(bst, cpe)


def _s5_seq_kernel(up_ref, be_ref, tp_ref, cpm_ref, pt_ref, s0_ref, dsk_ref,
                   yg_ref, sfin_ref, e_scr, sx_scr, *, nc):
    seg = nc // 8
    W = SLAB_STATE
    nt = W // LANES
    ub = up_ref[0, 0]
    u = ub.astype(F32)
    e = _dot(ub, be_ref[0, 0])
    for c in range(2 * nt):
        e_scr[c] = e[:, c * LANES:(c + 1) * LANES]

    def tiles(row):
        return [(row[:, c * LANES:(c + 1) * LANES], row[:, W + c * LANES:W + (c + 1) * LANES])
                for c in range(nt)]

    a8 = [(jnp.broadcast_to(r, (8, LANES)), jnp.broadcast_to(i, (8, LANES)))
          for r, i in tiles(pt_ref[0, 0, 1:2, :])]

    def step(i, carry):
        rows = pl.ds(i, 8, stride=seg)
        new = []
        for c in range(nt):
            sr, si = carry[c]
            ar, ai = a8[c]
            sx_scr[c, rows, :] = sr
            sx_scr[nt + c, rows, :] = si
            new.append((ar * sr - ai * si + e_scr[c, rows, :],
                        ar * si + ai * sr + e_scr[nt + c, rows, :]))
        return tuple(new)

    zero = jnp.zeros((8, LANES), F32)
    ends = lax.fori_loop(0, seg, step, tuple((zero, zero) for _ in range(nt)))

    al = tiles(pt_ref[0, 0, seg:seg + 1, :])
    cur = tiles(s0_ref[0, 0])
    car = []
    for c in range(nt):
        alr, ali = al[c]
        cr, ci = cur[c]
        sr, si = ends[c]
        crs, cis = [], []
        for j in range(8):
            crs.append(cr)
            cis.append(ci)
            cr, ci = (alr * cr - ali * ci + sr[j:j + 1], alr * ci + ali * cr + si[j:j + 1])
        sfin_ref[0, 0, :, c * LANES:(c + 1) * LANES] = cr
        sfin_ref[0, 0, :, W + c * LANES:W + (c + 1) * LANES] = ci
        car.append((jnp.concatenate(crs, axis=0), jnp.concatenate(cis, axis=0)))

    def corr(i, _):
        rows = pl.ds(i, 8, stride=seg)
        pw = tiles(pt_ref[0, 0, pl.ds(i, 1), :])
        for c in range(nt):
            pr, pi = pw[c]
            cr, ci = car[c]
            sx_scr[c, rows, :] = sx_scr[c, rows, :] + (pr * cr - pi * ci)
            sx_scr[nt + c, rows, :] = sx_scr[nt + c, rows, :] + (pr * ci + pi * cr)
        return 0

    lax.fori_loop(0, seg, corr, 0)

    sx = jnp.concatenate([sx_scr[c] for c in range(2 * nt)], axis=-1)
    y = _dot(sx.astype(BF16), cpm_ref[0, 0])
    TW = 2 * LANES
    for tq in range(S5_T // 2):
        acc = y[:, tq * TW:(tq + 1) * TW]
        for tpi in range(tq + 1):
            acc = acc + _dot(ub[:, tpi * TW:(tpi + 1) * TW], tp_ref[0, 0, tq - tpi])
        acc = acc + dsk_ref[0, :, tq * TW:(tq + 1) * TW] * u[:, tq * TW:(tq + 1) * TW]
        yg_ref[0, 0, :, tq * TW:(tq + 1) * TW] = jax.nn.gelu(acc).astype(yg_ref.dtype)


def _s5_seq(up, be_emb, tp, cpm, pt, s0, dsk, l):
    _, B, nc, _ = up.shape
    seg = nc // 8
    W2 = 2 * SLAB_STATE
    yg, sfin = pl.pallas_call(
        functools.partial(_s5_seq_kernel, nc=nc),
        out_shape=(jax.ShapeDtypeStruct((SLABS, B, nc, S5_T * LANES), BF16),
                   jax.ShapeDtypeStruct((SLABS, B, 1, W2), F32)),
        grid=(SLABS, B),
        in_specs=[pl.BlockSpec((1, 1, nc, S5_T * LANES), lambda k, b: (k, b, 0, 0)),
                  pl.BlockSpec((1, 1, S5_T * LANES, W2), lambda k, b: (l, k, 0, 0)),
                  pl.BlockSpec((1, 1, S5_T // 2, 2 * LANES, 2 * LANES), lambda k, b: (l, k, 0, 0, 0)),
                  pl.BlockSpec((1, 1, W2, S5_T * LANES), lambda k, b: (l, k, 0, 0)),
                  pl.BlockSpec((1, 1, seg + 1, W2), lambda k, b: (l, k, 0, 0)),
                  pl.BlockSpec((1, 1, 1, W2), lambda k, b: (k, b, 0, 0)),
                  pl.BlockSpec((1, 1, S5_T * LANES), lambda k, b: (k, 0, 0))],
        out_specs=(pl.BlockSpec((1, 1, nc, S5_T * LANES), lambda k, b: (k, b, 0, 0)),
                   pl.BlockSpec((1, 1, 1, W2), lambda k, b: (k, b, 0, 0))),
        scratch_shapes=[pltpu.VMEM((W2 // LANES, nc, LANES), F32),
                        pltpu.VMEM((W2 // LANES, nc, LANES), F32)],
        compiler_params=_cp(("parallel", "parallel")),
        name="s5_seq",
    )(up, be_emb, tp, cpm, pt, s0, dsk)
    return yg, sfin


def _s5_step_kernel(u_ref, b_ref, c_ref, a_ref, s0_ref, d_ref, yg_ref, s1_ref):
    W = SLAB_STATE
    u = u_ref[0]
    bu = _dot(u, b_ref[0, 0], HI)
    ar = a_ref[0, 0, :, 0:W]
    ai = a_ref[0, 0, :, W:2 * W]
    sr = s0_ref[0, :, 0:W]
    si = s0_ref[0, :, W:2 * W]
    nr = ar * sr - ai * si + bu[:, 0:W]
    ni = ar * si + ai * sr + bu[:, W:2 * W]
    s1_ref[0, :, 0:W] = nr
    s1_ref[0, :, W:2 * W] = ni
    s1 = jnp.concatenate([nr, ni], axis=-1)
    y = _dot(s1, c_ref[0, 0, 0], HI) + d_ref[0] * u
    yg_ref[0] = jax.nn.gelu(y)


def _s5_step(u_slab, bst, cpe, a1, s0, d1, l):
    _, N, _ = u_slab.shape
    W2 = 2 * SLAB_STATE
    return pl.pallas_call(
        _s5_step_kernel,
        out_shape=(jax.ShapeDtypeStruct((SLABS, N, LANES), F32),
                   jax.ShapeDtypeStruct((SLABS, N, W2), F32)),
        grid=(SLABS,),
        in_specs=[pl.BlockSpec((1, N, LANES), lambda k: (k, 0, 0)),
                  pl.BlockSpec((1, 1, LANES, W2), lambda k: (l, k, 0, 0)),
                  pl.BlockSpec((1, 1, 1, W2, LANES), lambda k: (l, k, 0, 0, 0)),
                  pl.BlockSpec((1, 1, 1, W2), lambda k: (l, k, 0, 0)),
                  pl.BlockSpec((1, N, W2), lambda k: (k, 0, 0)),
                  pl.BlockSpec((1, 1, LANES), lambda k: (k, 0, 0))],
        out_specs=(pl.BlockSpec((1, N, LANES), lambda k: (k, 0, 0)),
                   pl.BlockSpec((1, N, W2), lambda k: (k, 0, 0))),
        compiler_params=_cp(("parallel",)),
        name="s5_step",
    )(u_slab, bst, cpe, a1, s0, d1)


def _l2n(x):
    return x * lax.rsqrt(jnp.sum(x * x, axis=-1, keepdims=True) + L2_EPS)


def _split_bf16(x):
    hi = x.astype(BF16)
    return hi, (x - hi.astype(F32)).astype(BF16)


def _unit_lower_solve(As, rhss):
    n = GDN_C
    row = lax.broadcasted_iota(jnp.int32, (n, n), 0)
    col = lax.broadcasted_iota(jnp.int32, (n, n), 1)
    eye = (row == col).astype(F32)
    same8 = (row // 8) == (col // 8)
    Qs = [jnp.where(same8, -A, 0.0) for A in As]
    invs = [eye + Q for Q in Qs]
    for _ in range(2):
        Qs = [_dotb(Q, Q) for Q in Qs]
        invs = [inv + _dotb(inv, Q) for inv, Q in zip(invs, Qs)]
    s = 8
    while s < n:
        sib = ((row // (2 * s)) == (col // (2 * s))) & ((row // s) != (col // s))
        offs = [jnp.where(sib, A, 0.0).astype(BF16) for A in As]
        invb = [inv.astype(BF16) for inv in invs]
        tmp = [_dot(off, ib) for off, ib in zip(offs, invb)]
        invs = [inv - _dot(ib, t.astype(BF16)) for inv, ib, t in zip(invs, invb, tmp)]
        s *= 2
    invb = [inv.astype(BF16) for inv in invs]
    x0s = [_dot(ib, rhs.astype(BF16)) for ib, rhs in zip(invb, rhss)]
    res = []
    for A, x0, rhs in zip(As, x0s, rhss):
        ah, al = _split_bf16(A)
        xh, xl = _split_bf16(x0)
        res.append(rhs - x0 - (_dot(ah, xh) + _dot(ah, xl) + _dot(al, xh)))
    return [x0 + _dot(ib, r.astype(BF16)) for x0, ib, r in zip(x0s, invb, res)]


def _gdn_tile(qc_scr, gc, beta, z_ref, nw, o_ref, s_scr, tl):
    C, DK, H = GDN_C, GDN_DK, GDN_HEADS
    nchunk = tl // C
    probs = [(c, h) for c in range(nchunk) for h in range(H)]
    row = lax.broadcasted_iota(jnp.int32, (C, C), 0)
    col = lax.broadcasted_iota(jnp.int32, (C, C), 1)
    tri = row >= col
    strict = row > col

    def blk(c, off):
        return qc_scr[c * C:(c + 1) * C, off:off + DK]

    q = [_l2n(blk(c, h * DK)) * (DK ** -0.5) for c, h in probs]
    k = [_l2n(blk(c, GDN_WIDTH + h * DK)) for c, h in probs]
    v = [blk(c, 2 * GDN_WIDTH + h * DK) for c, h in probs]
    gcb = [jnp.broadcast_to(gc[c * C:(c + 1) * C, h:h + 1], (C, DK)) for c, h in probs]
    bb = [jnp.broadcast_to(beta[c * C:(c + 1) * C, H + h:H + h + 1], (C, DK)) for c, h in probs]
    decay = []
    for g in gcb:
        diff = g - g.T
        decay.append(jnp.where(tri, jnp.exp(jnp.where(tri, diff, 0.0)), 0.0))
    kbf = [x.astype(BF16) for x in k]
    kb = [x * b for x, b in zip(k, bb)]
    A = [jnp.where(strict, _dot_nt(x.astype(BF16), y) * d, 0.0) for x, y, d in zip(kb, kbf, decay)]
    egc = [jnp.exp(g) for g in gcb]
    rhs = [jnp.concatenate([x * b, y * e], axis=-1) for x, b, y, e in zip(v, bb, kb, egc)]
    sol = _unit_lower_solve(A, rhs)
    attn = [jnp.where(tri, _dot_nt(x.astype(BF16), y) * d, 0.0).astype(BF16)
            for x, y, d in zip(q, kbf, decay)]
    glast = [g[C - 1:C, :] for g in gcb]
    wq = [jnp.concatenate([s[:, DK:], x * e], axis=0).astype(BF16) for s, x, e in zip(sol, q, egc)]
    kg = [(x * jnp.exp(gl - g)).astype(BF16) for x, gl, g in zip(k, glast, gcb)]

    for c in range(nchunk):
        ps = [c * H + h for h in range(H)]
        S = [s_scr[h] for h in range(H)]
        ws = [_dot(wq[p], S[h].astype(BF16)) for h, p in enumerate(ps)]
        v_new = [sol[p][:, 0:DK] - w[0:C] for p, w in zip(ps, ws)]
        vb = [x.astype(BF16) for x in v_new]
        o = [w[C:] + _dot(attn[p], x) for p, w, x in zip(ps, ws, vb)]
        for h, p in enumerate(ps):
            s_scr[h] = S[h] * jnp.exp(glast[p]) + _dot_tn(kg[p], vb[h])
            zh = z_ref[0, c * C:(c + 1) * C, h * DK:(h + 1) * DK].astype(F32)
            on = o[h] * lax.rsqrt(jnp.mean(o[h] * o[h], axis=-1, keepdims=True) + NORM_EPS) * nw
            o_ref[0, c * C:(c + 1) * C, h * DK:(h + 1) * DK] = (on * _silu(zh)).astype(o_ref.dtype)


def _gdn_seq_kernel(qkv_ref, z_ref, ab_ref, cw_ref, alog_ref, dtb_ref, nw_ref, conv0_ref, s0_ref,
                    o_ref, sfin_ref, xp_scr, qc_scr, s_scr, *, tl):
    lt = pl.program_id(1)

    @pl.when(lt == 0)
    def _():
        xp_scr[0:8, :] = jnp.zeros((8, QKV_WIDTH), F32)
        xp_scr[8 - (GDN_CONV - 1):8, :] = conv0_ref[0]
        s_scr[...] = s0_ref[0]

    xp_scr[8:8 + tl, :] = qkv_ref[0].astype(F32)
    conv = cw_ref[0, 0:1, :] * xp_scr[5:5 + tl, :]
    for j in range(1, GDN_CONV):
        conv = conv + cw_ref[0, j:j + 1, :] * xp_scr[5 + j:5 + j + tl, :]
    xp_scr[0:8, :] = xp_scr[tl:tl + 8, :]
    qc_scr[...] = _silu(conv)

    ab = ab_ref[0]
    g = -jnp.exp(alog_ref[...]) * jax.nn.softplus(ab + dtb_ref[...])
    beta = jax.nn.sigmoid(ab)
    row = lax.broadcasted_iota(jnp.int32, (tl, tl), 0)
    col = lax.broadcasted_iota(jnp.int32, (tl, tl), 1)
    csum = ((row >= col) & ((row // GDN_C) == (col // GDN_C))).astype(F32)
    gc = _dot(csum, g, HI)
    _gdn_tile(qc_scr, gc, beta, z_ref, nw_ref[...], o_ref, s_scr, tl)

    @pl.when(lt == pl.num_programs(1) - 1)
    def _():
        sfin_ref[0] = s_scr[...]


def _gdn_seq(qkv, z, ab, conv_w, alog, dtb, nw, conv0, s0, l):
    B, L, _ = qkv.shape
    tl = min(256, L)
    return pl.pallas_call(
        functools.partial(_gdn_seq_kernel, tl=tl),
        out_shape=(jax.ShapeDtypeStruct((B, L, GDN_WIDTH), BF16),
                   jax.ShapeDtypeStruct((B, GDN_HEADS, GDN_DK, GDN_DK), F32)),
        grid=(B, L // tl),
        in_specs=[pl.BlockSpec((1, tl, QKV_WIDTH), lambda b, i: (b, i, 0)),
                  pl.BlockSpec((1, tl, GDN_WIDTH), lambda b, i: (b, i, 0)),
                  pl.BlockSpec((1, tl, LANES), lambda b, i: (b, i, 0)),
                  pl.BlockSpec((1, GDN_CONV, QKV_WIDTH), lambda b, i: (l, 0, 0)),
                  pl.BlockSpec((1, LANES), lambda b, i: (0, 0)),
                  pl.BlockSpec((1, LANES), lambda b, i: (0, 0)),
                  pl.BlockSpec((1, GDN_DK), lambda b, i: (0, 0)),
                  pl.BlockSpec((1, GDN_CONV - 1, QKV_WIDTH), lambda b, i: (b, 0, 0)),
                  pl.BlockSpec((1, GDN_HEADS, GDN_DK, GDN_DK), lambda b, i: (b, 0, 0, 0))],
        out_specs=(pl.BlockSpec((1, tl, GDN_WIDTH), lambda b, i: (b, i, 0)),
                   pl.BlockSpec((1, GDN_HEADS, GDN_DK, GDN_DK), lambda b, i: (b, 0, 0, 0))),
        scratch_shapes=[pltpu.VMEM((tl + 8, QKV_WIDTH), F32),
                        pltpu.VMEM((tl, QKV_WIDTH), F32),
                        pltpu.VMEM((GDN_HEADS, GDN_DK, GDN_DK), F32)],
        compiler_params=_cp(("parallel", "arbitrary")),
        name="gdn_seq",
    )(qkv, z, ab, conv_w, alog, dtb, nw, conv0, s0)


GDN_STEP_ROWS = 8


def _gdn_step_kernel(qkv_ref, z_ref, ab_ref, cw_ref, alog_ref, dtb_ref, nw_ref, conv0_ref, s0_ref,
                     *rest):
    if len(rest) == 3:
        prev_ref, o_ref, s1_all = rest
        s1_all[0] = prev_ref[...]
        s1_ref = s1_all.at[1]
    else:
        o_ref, s1_all = rest
        s1_ref = s1_all
    nb = GDN_STEP_ROWS
    W = QKV_WIDTH
    conv = cw_ref[0:1, :] * conv0_ref[:, 0:W]
    conv = conv + cw_ref[1:2, :] * conv0_ref[:, W:2 * W]
    conv = conv + cw_ref[2:3, :] * conv0_ref[:, 2 * W:3 * W]
    conv = conv + cw_ref[3:4, :] * qkv_ref[...]
    qc = _silu(conv)
    ab = ab_ref[...]
    eg = jnp.exp(-jnp.exp(alog_ref[...]) * jax.nn.softplus(ab + dtb_ref[...]))
    beta = jax.nn.sigmoid(ab)
    eye = (lax.broadcasted_iota(jnp.int32, (GDN_DK, GDN_DK), 0)
           == lax.broadcasted_iota(jnp.int32, (GDN_DK, GDN_DK), 1)).astype(F32)
    for h in range(GDN_HEADS):
        q = _l2n(qc[:, h * GDN_DK:(h + 1) * GDN_DK]) * (GDN_DK ** -0.5)
        k = _l2n(qc[:, GDN_WIDTH + h * GDN_DK:GDN_WIDTH + (h + 1) * GDN_DK])
        v = qc[:, 2 * GDN_WIDTH + h * GDN_DK:2 * GDN_WIDTH + (h + 1) * GDN_DK]
        kT = _dot_nt(eye, k, HI)
        qT = _dot_nt(eye, q, HI)
        qk = jnp.sum(q * k, axis=-1, keepdims=True)
        for j in range(nb):
            S = s0_ref[0, j, h]
            kc = jnp.broadcast_to(kT[:, j:j + 1], (GDN_DK, GDN_DK))
            qcb = jnp.broadcast_to(qT[:, j:j + 1], (GDN_DK, GDN_DK))
            kS = jnp.sum(kc * S, axis=0, keepdims=True)
            qS = jnp.sum(qcb * S, axis=0, keepdims=True)
            egj = eg[j:j + 1, h:h + 1]
            bj = beta[j:j + 1, GDN_HEADS + h:GDN_HEADS + h + 1]
            v_new = bj * v[j:j + 1, :] - (bj * egj) * kS
            o = egj * qS + qk[j:j + 1, :] * v_new
            s1_ref[j, h] = S * egj + kc * v_new
            zh = z_ref[j:j + 1, h * GDN_DK:(h + 1) * GDN_DK]
            on = o * lax.rsqrt(jnp.mean(o * o, axis=-1, keepdims=True) + NORM_EPS) * nw_ref[...]
            o_ref[j:j + 1, h * GDN_DK:(h + 1) * GDN_DK] = on * _silu(zh)


def _gdn_step(qkv, z, ab, conv_w, alog, dtb, nw, conv0, s_all, l, prev):
    N = qkv.shape[0]
    nb = GDN_STEP_ROWS
    row = lambda w: pl.BlockSpec((nb, w), lambda i: (i, 0))
    const = lambda r, w: pl.BlockSpec((r, w), lambda i: (0, 0))
    sblk = (nb, GDN_HEADS, GDN_DK, GDN_DK)
    one = pl.BlockSpec(sblk, lambda i: (i, 0, 0, 0))
    ins = [qkv, z, ab, conv_w, alog, dtb, nw, conv0, s_all]
    in_specs = [row(QKV_WIDTH), row(GDN_WIDTH), row(LANES), const(GDN_CONV, QKV_WIDTH),
                const(1, LANES), const(1, LANES), const(1, GDN_DK), row(3 * QKV_WIDTH),
                pl.BlockSpec((1,) + sblk, lambda i: (l, i, 0, 0, 0))]
    if prev is None:
        s_shape, s_spec = jax.ShapeDtypeStruct((N,) + sblk[1:], F32), one
    else:
        assert DEPTH == 2 and l == 1
        ins.append(prev)
        in_specs.append(one)
        s_shape = jax.ShapeDtypeStruct((DEPTH, N) + sblk[1:], F32)
        s_spec = pl.BlockSpec((DEPTH,) + sblk, lambda i: (0, i, 0, 0, 0))
    return pl.pallas_call(
        _gdn_step_kernel,
        out_shape=(jax.ShapeDtypeStruct((N, GDN_WIDTH), F32), s_shape),
        grid=(N // nb,),
        in_specs=in_specs,
        out_specs=(row(GDN_WIDTH), s_spec),
        compiler_params=_cp(("parallel",)),
        name="gdn_step",
    )(*ins)


def _merge_kernel(yg_ref, og_ref, ga_ref, gb_ref, x_ref, gt_ref, wglu_ref, wgo_ref, wout_ref,
                  gf_ref, scf_ref, shf_ref, wr_ref,
                  xo_ref, h_ref, lg_ref, *scr, hi, chunked):
    if chunked:
        y_scr = scr[-1]
        scr = scr[:-1]
        nrow = y_scr.shape[1] // S5_T
        for k in range(SLABS):
            for t in range(S5_T):
                y_scr[k, pl.ds(t, nrow, stride=S5_T), :] = (
                    yg_ref[k, 0, :, t * LANES:(t + 1) * LANES].astype(F32))
        y = jnp.concatenate([y_scr[k] for k in range(SLABS)], axis=-1)
    else:
        y = jnp.concatenate([yg_ref[k, 0] for k in range(SLABS)], axis=-1)
    if hi:
        wglu, wgo, wout = wglu_ref[0], wgo_ref[0], wout_ref[0]
        mm = lambda a, w: _dot(a, w, HI)
    else:
        wglu_s, wgo_s, wout_s = scr

        @pl.when((pl.program_id(0) == 0) & (pl.program_id(1) == 0))
        def _():
            wglu_s[...] = wglu_ref[0].astype(BF16)
            wgo_s[...] = wgo_ref[0].astype(BF16)
            wout_s[...] = wout_ref[0].astype(BF16)

        wglu, wgo, wout = wglu_s[...], wgo_s[...], wout_s[...]
        mm = lambda a, w: _dot(a.astype(BF16), w)

    glu = mm(y, wglu)
    branch_a = glu[:, 0:D_MODEL] * jax.nn.sigmoid(glu[:, D_MODEL:])
    branch_b = mm(og_ref[0], wgo)
    merged = ga_ref[0].astype(F32) * branch_a + gb_ref[0].astype(F32) * branch_b
    out = mm(merged, wout)
    x = x_ref[0] + gt_ref[0, 0] * out
    xo_ref[0] = x
    ms = jnp.mean(x * x, axis=-1, keepdims=True)
    h = x * lax.rsqrt(ms + NORM_EPS) * gf_ref[0]
    h = h * (1.0 + scf_ref[0, 0]) + shf_ref[0, 0]
    h_ref[0] = h.astype(h_ref.dtype)
    lg_ref[0] = _dot_nt(wr_ref[0], h, HI)


def _merge(yg, og, ga, gb, x, mod, wglu, wgo, wout, gf, wr, *, l, tm, hi, chunked, h_dtype):
    B, L, D = x.shape
    row = lambda w: pl.BlockSpec((1, tm, w), lambda b, i: (b, i, 0))
    layer = lambda r, w, ll=l: pl.BlockSpec((1, r, w), lambda b, i: (ll, 0, 0))
    scratch = [] if hi else [pltpu.VMEM((S5_WIDTH, 2 * D), BF16), pltpu.VMEM((GDN_WIDTH, D), BF16),
                             pltpu.VMEM((D, D), BF16)]
    if chunked:
        scratch = scratch + [pltpu.VMEM((SLABS, tm, LANES), F32)]
        yg_spec = pl.BlockSpec((SLABS, 1, tm // S5_T, S5_T * LANES), lambda b, i: (0, b, i, 0))
    else:
        yg_spec = pl.BlockSpec((SLABS, 1, tm, LANES), lambda b, i: (0, b, i, 0))
    lg_shape = jax.ShapeDtypeStruct((B, N_EXPERTS, L), F32)
    lg_spec = pl.BlockSpec((1, N_EXPERTS, tm), lambda b, i: (b, 0, i))
    return pl.pallas_call(
        functools.partial(_merge_kernel, hi=hi, chunked=chunked),
        out_shape=(jax.ShapeDtypeStruct((B, L, D), F32),
                   jax.ShapeDtypeStruct((B, L, D), h_dtype),
                   lg_shape),
        grid=(B, L // tm),
        in_specs=[yg_spec,
                  row(GDN_WIDTH), row(D), row(D), row(D), _mod_spec(mod, l, 2, tm),
                  layer(S5_WIDTH, 2 * D), layer(GDN_WIDTH, D), layer(D, D),
                  layer(1, D), _mod_spec(mod, l, 4, tm), _mod_spec(mod, l, 3, tm),
                  layer(N_EXPERTS, D, l // 2)],
        out_specs=(row(D), row(D), lg_spec),
        scratch_shapes=scratch,
        compiler_params=_cp(("arbitrary", "arbitrary")),
        name="merge_out_proj",
    )(yg, og, ga, gb, x, mod, wglu, wgo, wout, gf, mod, mod, wr)


FF_TILE = 512


def _finish(x, gfin_ref, final):
    if not final:
        return x
    ms = jnp.mean(x * x, axis=-1, keepdims=True)
    return x * lax.rsqrt(ms + NORM_EPS) * gfin_ref[...]


def _ffn_kernel(h_ref, x_ref, gt_ref, wg_ref, wu_ref, wd_ref, gfin_ref, o_ref, acc_scr, *, hi, final):
    j = pl.program_id(2)
    if hi:
        h = h_ref[0]
        mm = lambda a, w: _dot(a, w, HI)
    else:
        h = h_ref[0].astype(BF16)
        mm = lambda a, w: _dot(a.astype(BF16), w.astype(BF16))
    act = _silu(mm(h, wg_ref[...])) * mm(h, wu_ref[...])
    part = mm(act, wd_ref[...])

    @pl.when(j == 0)
    def _():
        acc_scr[...] = part

    @pl.when(j > 0)
    def _():
        acc_scr[...] = acc_scr[...] + part

    @pl.when(j == pl.num_programs(2) - 1)
    def _():
        o_ref[0] = _finish(x_ref[0] + gt_ref[0, 0] * acc_scr[...], gfin_ref, final)


def _ffn(h, x, mod, w_gu, w_down, gfin, *, l, tm, hi, final):
    B, L, D = x.shape
    nj = D_FF // FF_TILE
    row = pl.BlockSpec((1, tm, D), lambda b, i, j: (b, i, 0))
    return pl.pallas_call(
        functools.partial(_ffn_kernel, hi=hi, final=final),
        out_shape=jax.ShapeDtypeStruct((B, L, D), F32),
        grid=(B, L // tm, nj),
        in_specs=[row, row, _mod_spec(mod, l, 5, tm),
                  pl.BlockSpec((D, FF_TILE), lambda b, i, j: (0, j)),
                  pl.BlockSpec((D, FF_TILE), lambda b, i, j: (0, nj + j)),
                  pl.BlockSpec((FF_TILE, D), lambda b, i, j: (j, 0)),
                  pl.BlockSpec((1, D), lambda b, i, j: (0, 0))],
        out_specs=row,
        scratch_shapes=[pltpu.VMEM((tm, D), F32)],
        compiler_params=_cp(("parallel", "parallel", "arbitrary")),
        name="ffn_dense",
    )(h, x, mod, w_gu, w_gu, w_down, gfin)


ROUTE_TM = 512
ROW_DMA_TM = 256
MOE_SUP = 2048
MOE_SUB = 512


def _route_kernel(lg_ref, br_ref, cnt0_ref, slot_ref, wt_ref, cnt_ref, carry_scr, *, cap):
    @pl.when((pl.program_id(0) == 0) & (pl.program_id(1) == 0))
    def _():
        carry_scr[...] = cnt0_ref[...]

    lg = lg_ref[0] + br_ref[...]
    tm = lg.shape[1]
    eidx = lax.broadcasted_iota(jnp.int32, lg.shape, 0)
    m1 = jnp.max(lg, axis=0, keepdims=True)
    i1 = jnp.min(jnp.where(lg == m1, eidx, N_EXPERTS), axis=0, keepdims=True)
    lg2 = jnp.where(eidx == i1, -jnp.inf, lg)
    m2 = jnp.max(lg2, axis=0, keepdims=True)
    i2 = jnp.min(jnp.where(lg2 == m2, eidx, N_EXPERTS), axis=0, keepdims=True)
    e2 = jnp.exp(m2 - m1)
    wt_ref[0, 0:1, :] = 1.0 / (1.0 + e2)
    wt_ref[0, 1:2, :] = e2 / (1.0 + e2)
    sel1 = eidx == i1
    sel2 = eidx == i2
    oh = jnp.where(sel1 | sel2, 1.0, 0.0)
    before = (lax.broadcasted_iota(jnp.int32, (tm, tm), 0)
              < lax.broadcasted_iota(jnp.int32, (tm, tm), 1)).astype(BF16)
    rank = carry_scr[:, 0:1] + _dot(oh.astype(BF16), before)
    r1 = jnp.sum(jnp.where(sel1, rank, 0.0), axis=0, keepdims=True).astype(jnp.int32)
    r2 = jnp.sum(jnp.where(sel2, rank, 0.0), axis=0, keepdims=True).astype(jnp.int32)
    slot_ref[0, 0:1, :] = i1 * cap + r1
    slot_ref[0, 1:2, :] = i2 * cap + r2
    carry_scr[...] = carry_scr[...] + jnp.sum(oh, axis=1, keepdims=True)
    cnt_ref[...] = carry_scr[...]


def _route_slots(lgT, b_r, cnt0, cap):
    B, E, L = lgT.shape
    tm = min(ROUTE_TM, L)
    return pl.pallas_call(
        functools.partial(_route_kernel, cap=cap),
        out_shape=(jax.ShapeDtypeStruct((B, 2, L), jnp.int32),
                   jax.ShapeDtypeStruct((B, 2, L), F32),
                   jax.ShapeDtypeStruct((E, LANES), F32)),
        grid=(B, L // tm),
        in_specs=[pl.BlockSpec((1, E, tm), lambda b, i: (b, 0, i)),
                  pl.BlockSpec((E, 1), lambda b, i: (0, 0)),
                  pl.BlockSpec((E, LANES), lambda b, i: (0, 0))],
        out_specs=(pl.BlockSpec((1, 2, tm), lambda b, i: (b, 0, i)),
                   pl.BlockSpec((1, 2, tm), lambda b, i: (b, 0, i)),
                   pl.BlockSpec((E, LANES), lambda b, i: (0, 0))),
        scratch_shapes=[pltpu.VMEM((E, LANES), F32)],
        compiler_params=_cp(("arbitrary", "arbitrary")),
        name="moe_route",
    )(lgT, b_r.reshape(E, 1), cnt0)


def _row_copy(src, dst, sem):
    return pltpu.make_async_copy(src, dst, sem)


def _slot_row(code, start_ref, cap):
    shift = cap.bit_length() - 1
    return start_ref[lax.shift_right_logical(code, shift)] + (code & (cap - 1))


def _zeros_kernel(o_ref):
    o_ref[...] = jnp.zeros_like(o_ref)


def _zero_rows(n_rows, width):
    return pl.pallas_call(
        _zeros_kernel,
        out_shape=jax.ShapeDtypeStruct((n_rows, width), F32),
        grid=(n_rows // MOE_SUP,),
        out_specs=pl.BlockSpec((MOE_SUP, width), lambda i: (i, 0)),
        compiler_params=_cp(("parallel",)),
        name="moe_zero_rows",
    )()


def _dispatch_kernel(start_ref, code_ref, h_ref, xs_in_ref, xs_ref, hbuf, sem, *, cap):
    del xs_in_ref
    tm = h_ref.shape[1]
    t = pl.program_id(0) * pl.num_programs(1) + pl.program_id(1)
    last = pl.num_programs(0) * pl.num_programs(1) - 1
    slot = t % 2
    hbuf[slot] = h_ref[0]

    def issue(r, _):
        for k in range(2):
            row = _slot_row(code_ref[0, k, r], start_ref, cap)
            _row_copy(hbuf.at[slot, pl.ds(r, 1), :], xs_ref.at[pl.ds(row, 1), :], sem.at[slot]).start()
        return 0

    lax.fori_loop(0, tm, issue, 0, unroll=8)

    def drain(sl):
        for k in range(2):
            _row_copy(hbuf.at[sl], xs_ref.at[pl.ds(0, tm), :], sem.at[sl]).wait()

    @pl.when(t > 0)
    def _():
        drain(1 - slot)

    @pl.when(t == last)
    def _():
        drain(slot)


def _dispatch(start, codes, h, xs, cap):
    B, L, D = h.shape
    n_rows = xs.shape[0]
    tm = min(ROW_DMA_TM, L)
    return pl.pallas_call(
        functools.partial(_dispatch_kernel, cap=cap),
        out_shape=jax.ShapeDtypeStruct((n_rows, D), F32),
        grid_spec=pltpu.PrefetchScalarGridSpec(
            num_scalar_prefetch=1,
            grid=(B, L // tm),
            in_specs=[pl.BlockSpec((1, 2, tm), lambda b, i, st: (b, 0, i), memory_space=pltpu.SMEM),
                      pl.BlockSpec((1, tm, D), lambda b, i, st: (b, i, 0)),
                      pl.BlockSpec(memory_space=pl.ANY)],
            out_specs=pl.BlockSpec(memory_space=pl.ANY),
            scratch_shapes=[pltpu.VMEM((2, tm, D), F32), pltpu.SemaphoreType.DMA((2,))],
        ),
        input_output_aliases={3: 0},
        compiler_params=_cp(("arbitrary", "arbitrary")),
        name="moe_dispatch",
    )(start, codes, h, xs)


def _moe_grp_kernel(ge_ref, gn_ref, x_ref, wg_ref, wu_ref, wd_ref, y_ref, xb_scr):
    g = pl.program_id(0)
    j = pl.program_id(1)
    nsub = gn_ref[g]
    wg = wg_ref[0].astype(BF16)
    wu = wu_ref[0].astype(BF16)
    wd = wd_ref[0].astype(BF16)
    nblk = MOE_SUP // MOE_SUB

    @pl.when(j == 0)
    def _():
        xb_scr[...] = x_ref[...].astype(BF16)
        y_ref[...] = jnp.zeros_like(y_ref)

    def block(s):
        rows = slice(s * MOE_SUB, (s + 1) * MOE_SUB)
        xb = xb_scr[rows, :]
        act = _silu(_dot(xb, wg)) * _dot(xb, wu)
        y_ref[rows, :] = y_ref[rows, :] + _dot(act.astype(BF16), wd)

    @pl.when(nsub == nblk)
    def _():
        for s in range(nblk):
            block(s)

    @pl.when(nsub < nblk)
    def _():
        for s in range(nblk - 1):
            pl.when(s < nsub)(functools.partial(block, s))


def _moe_groups(counts, n_groups):
    nsup = (counts + MOE_SUP - 1) // MOE_SUP
    ends = jnp.cumsum(nsup)
    first = ends - nsup
    total = ends[-1]
    g = jnp.arange(n_groups, dtype=jnp.int32)
    gc = jnp.minimum(g, total - 1)
    e_of = jnp.minimum(jnp.sum((gc[:, None] >= ends[None, :]).astype(jnp.int32), axis=1), N_EXPERTS - 1)
    left = counts[e_of] - (gc - first[e_of]) * MOE_SUP
    nsub = jnp.clip((left + MOE_SUB - 1) // MOE_SUB, 0, MOE_SUP // MOE_SUB)
    gn = jnp.where(g < total, nsub, 0).astype(jnp.int32)
    return e_of, gn, (first * MOE_SUP).astype(jnp.int32)


def _moe_grouped(xs, ge, gn, w_gu, w_down):
    D = xs.shape[1]
    nj = D_FF // FF_TILE
    ng = xs.shape[0] // MOE_SUP
    jj = lambda j, gn, g: jnp.where(gn[g] > 0, j, nj - 1)
    return pl.pallas_call(
        _moe_grp_kernel,
        out_shape=jax.ShapeDtypeStruct(xs.shape, F32),
        grid_spec=pltpu.PrefetchScalarGridSpec(
            num_scalar_prefetch=2,
            grid=(ng, nj),
            in_specs=[pl.BlockSpec((MOE_SUP, D), lambda g, j, ge, gn: (g, 0)),
                      pl.BlockSpec((1, D, FF_TILE), lambda g, j, ge, gn: (ge[g], 0, jj(j, gn, g))),
                      pl.BlockSpec((1, D, FF_TILE), lambda g, j, ge, gn: (ge[g], 0, nj + jj(j, gn, g))),
                      pl.BlockSpec((1, FF_TILE, D), lambda g, j, ge, gn: (ge[g], jj(j, gn, g), 0))],
            out_specs=pl.BlockSpec((MOE_SUP, D), lambda g, j, ge, gn: (g, 0)),
            scratch_shapes=[pltpu.VMEM((MOE_SUP, D), BF16)],
        ),
        compiler_params=_cp(("arbitrary", "arbitrary")),
        name="moe_experts",
    )(ge, gn, xs, w_gu, w_gu, w_down)


def _combine_kernel(start_ref, code_ref, next_ref, w_ref, x_ref, gt_ref, gfin_ref, ys_ref, o_ref, g_scr, sem,
                    *, cap, final):
    tm = x_ref.shape[1]
    t = pl.program_id(0) * pl.num_programs(1) + pl.program_id(1)
    last = pl.num_programs(0) * pl.num_programs(1) - 1
    slot = t % 2

    def gather(codes, sl):
        def issue(r, _):
            for k in range(2):
                row = _slot_row(codes[0, k, r], start_ref, cap)
                _row_copy(ys_ref.at[pl.ds(row, 1), :], g_scr.at[sl, k, pl.ds(r, 1), :], sem.at[sl]).start()
            return 0

        lax.fori_loop(0, tm, issue, 0, unroll=8)

    @pl.when(t == 0)
    def _():
        gather(code_ref, slot)

    @pl.when(t < last)
    def _():
        gather(next_ref, 1 - slot)

    for k in range(2):
        _row_copy(ys_ref.at[pl.ds(0, tm), :], g_scr.at[slot, k], sem.at[slot]).wait()
    w = w_ref[0]
    f = w[:, 0:1] * g_scr[slot, 0] + w[:, 1:2] * g_scr[slot, 1]
    o_ref[0] = _finish(x_ref[0] + gt_ref[0, 0] * f, gfin_ref, final)


def _combine(start, codes, wts, x, mod, gfin, ys, *, l, cap, final):
    B, L, D = x.shape
    tm = min(ROW_DMA_TM, L)
    gt_spec = _mod_spec(mod, l, 5, tm)
    row = pl.BlockSpec((1, tm, D), lambda b, i, st: (b, i, 0))
    nl = L // tm

    def next_block(b, i, st):
        t1 = jnp.minimum(b * nl + i + 1, B * nl - 1)
        return (t1 // nl, 0, t1 % nl)

    return pl.pallas_call(
        functools.partial(_combine_kernel, cap=cap, final=final),
        out_shape=jax.ShapeDtypeStruct((B, L, D), F32),
        grid_spec=pltpu.PrefetchScalarGridSpec(
            num_scalar_prefetch=1,
            grid=(B, L // tm),
            in_specs=[pl.BlockSpec((1, 2, tm), lambda b, i, st: (b, 0, i), memory_space=pltpu.SMEM),
                      pl.BlockSpec((1, 2, tm), next_block, memory_space=pltpu.SMEM),
                      pl.BlockSpec((1, tm, 2), lambda b, i, st: (b, i, 0)),
                      row, gt_spec,
                      pl.BlockSpec((1, D), lambda b, i, st: (0, 0)),
                      pl.BlockSpec(memory_space=pl.ANY)],
            out_specs=row,
            scratch_shapes=[pltpu.VMEM((2, 2, tm, D), F32), pltpu.SemaphoreType.DMA((2,))],
        ),
        compiler_params=_cp(("arbitrary", "arbitrary")),
        name="moe_combine",
    )(start, codes, codes, wts.transpose(0, 2, 1), x, mod, gfin, ys)


def _moe_routed(groups, b_r, w_gu, w_down, gfin, *, l, final):
    D = groups[0][1].shape[-1]
    n_tok = sum(g[1].shape[0] * g[1].shape[1] for g in groups)
    cap = 1 << (n_tok - 1).bit_length()
    n_groups = 2 * n_tok // MOE_SUP + N_EXPERTS
    cnt = jnp.zeros((N_EXPERTS, LANES), F32)
    routed = []
    for _, _, _, lgT in groups:
        codes, wts, cnt = _route_slots(lgT, b_r, cnt, cap)
        routed.append((codes, wts))
    ge, gn, start = _moe_groups(cnt[:, 0].astype(jnp.int32), n_groups)
    xs = _zero_rows(n_groups * MOE_SUP, D)
    for (h, _, _, _), (codes, _) in zip(groups, routed):
        xs = _dispatch(start, codes, h, xs, cap)
    ys = _moe_grouped(xs, ge, gn, w_gu, w_down)
    return [_combine(start, codes, wts, x, mod, gfin, ys, l=l, cap=cap, final=final)
            for (_, x, mod, _), (codes, wts) in zip(groups, routed)]


def _pad_lanes(v):
    return jnp.pad(v.reshape(1, -1), ((0, 0), (0, LANES - v.shape[-1])))


def _mixer_layer(x, mod, states, p, s5m, l, prev_sg, *, seq):
    B, L, D = x.shape
    hi = not seq
    s5r0, s5i0, sg0, sc0 = states
    w_in, w_gates, w_ab = p['w_in_seq' if seq else 'w_in']
    if seq:
        u, qkv, z, ga, gb, ab = _proj_seq(x, p['g_mix'], mod, w_in, w_gates, w_ab, l=l, tm=min(512, L))
    else:
        u, qkv, z, ga, gb, ab = _proj(x, p['g_mix'], mod, w_in, w_gates, w_ab, l=l, tm=L)
    alog = _pad_lanes(p['gdn_a_log'][l])
    dtb = _pad_lanes(p['gdn_dt_bias'][l])
    nw = p['gdn_norm_w'][l].reshape(1, GDN_DK)
    if seq:
        yg, sfin = _s5_seq(u, s5m['be'], s5m['tp'], s5m['cpm'], s5m['pt'],
                           jnp.zeros((SLABS, B, 1, 2 * SLAB_STATE), F32), s5m['dsk'][l], l)
        sfin = sfin.reshape(SLABS, B, 2, SLAB_STATE).transpose(2, 1, 0, 3)
        sr = sfin[0].reshape(B, S5_GROUPS, S5_STATE)
        si = sfin[1].reshape(B, S5_GROUPS, S5_STATE)
        og, sg = _gdn_seq(qkv, z, ab, p['gdn_conv_w'], alog, dtb, nw,
                          jnp.zeros((B, GDN_CONV - 1, QKV_WIDTH), F32),
                          jnp.zeros((B, GDN_HEADS, GDN_DK, GDN_DK), F32), l)
        cb = qkv[:, L - (GDN_CONV - 1):, :].astype(F32)
    else:
        n = L
        s0 = jnp.concatenate([s5r0[l].reshape(n, SLABS, SLAB_STATE),
                              s5i0[l].reshape(n, SLABS, SLAB_STATE)], axis=-1).transpose(1, 0, 2)
        yg, s1 = _s5_step(u.reshape(SLABS, n, LANES), s5m['bst'], s5m['cpe'], s5m['a1'],
                          s0, s5m['d1'][l], l)
        yg = yg.reshape(SLABS, 1, n, LANES)
        s1 = s1.transpose(1, 0, 2)
        sr = s1[:, :, :SLAB_STATE].reshape(n, S5_GROUPS, S5_STATE)
        si = s1[:, :, SLAB_STATE:].reshape(n, S5_GROUPS, S5_STATE)
        og, sg = _gdn_step(qkv.reshape(n, QKV_WIDTH), z.reshape(n, GDN_WIDTH), ab.reshape(n, LANES),
                           p['gdn_conv_w'][l], alog, dtb, nw,
                           sc0[l].reshape(n, (GDN_CONV - 1) * QKV_WIDTH), sg0, l, prev_sg)
        og = og.reshape(1, n, GDN_WIDTH)
        cb = jnp.concatenate([sc0[l][:, 1:, :], qkv.reshape(n, 1, QKV_WIDTH)], axis=1)
    x, h, lgT = _merge(yg, og, ga, gb, x, mod, p['w_s5_glu'], p['w_gdn_out'], p['w_out'],
                       p['g_ffn'], p['w_router'], l=l, tm=min(512, L), hi=hi, chunked=seq,
                       h_dtype=BF16 if (seq and l % 2 == 0) else F32)
    return x, h, lgT, (sr, si, sg, cb)


def kernel(x_prompt, x_sample, c_prompt, c_sample, state_s5_re, state_s5_im, state_gdn, state_conv,
           g_mix, g_ffn, g_final, w_ada, b_ada, w_in, s5_lambda_re, s5_lambda_im, s5_log_dt,
           s5_b_re, s5_b_im, s5_c_re, s5_c_im, s5_d, w_s5_glu, gdn_conv_w, gdn_a_log, gdn_dt_bias,
           gdn_norm_w, w_gdn_out, w_out, w_ffn_gate_up, w_ffn_down, w_router, b_router,
           w_exp_gate_up, w_exp_down):
    def in_proj_parts(w):
        return w, w[:, :, 2568:], jnp.pad(w[:, :, 2560:2568], ((0, 0), (0, 0), (0, LANES - 8)))

    D_ = x_prompt.shape[-1]
    p = dict(g_mix=g_mix.reshape(DEPTH, 1, D_), g_ffn=g_ffn.reshape(DEPTH, 1, D_), w_s5_glu=w_s5_glu,
             gdn_conv_w=gdn_conv_w, gdn_a_log=gdn_a_log, gdn_dt_bias=gdn_dt_bias,
             gdn_norm_w=gdn_norm_w, w_gdn_out=w_gdn_out, w_out=w_out,
             w_router=w_router.transpose(0, 2, 1), w_in=in_proj_parts(w_in),
             w_in_seq=in_proj_parts(w_in.astype(BF16)))
    nbp, L, D = x_prompt.shape
    nbs = x_sample.shape[0]

    mod = _ada(jnp.concatenate([c_prompt, c_sample], axis=0), w_ada, b_ada)
    mod_p = mod[:, :nbp].reshape(DEPTH, nbp, 1, 6 * D)
    mod_s = mod[:, nbp:].reshape(DEPTH, 1, nbs, 6 * D)

    seg = L // S5_T // 8
    be, bst, cpe, cpm, pt, a1 = _s5_prep(s5_lambda_re, s5_lambda_im, s5_log_dt, s5_b_re, s5_b_im,
                                         s5_c_re, s5_c_im, seg)
    d1 = [s5_d[l].reshape(SLABS, 1, LANES) for l in range(DEPTH)]
    s5m = dict(be=be, bst=bst, cpe=cpe, cpm=cpm, pt=pt, a1=a1, tp=_toep(bst, cpe), d1=d1,
               dsk=[jnp.tile(d, (1, 1, S5_T)) for d in d1])

    xs_ = [x_prompt, x_sample.reshape(1, nbs, D)]
    mods = [mod_p, mod_s]
    states = [(None, None, None, None), (state_s5_re, state_s5_im, state_gdn, state_conv)]
    outs = [[], []]
    gfin = g_final.reshape(1, D)
    for l in range(DEPTH):
        final = l == DEPTH - 1
        mixed = []
        for gi, seq in enumerate((True, False)):
            prev_sg = outs[gi][0][2] if (not seq and final and DEPTH == 2) else None
            x, h, lgT, st = _mixer_layer(xs_[gi], mods[gi], states[gi], p, s5m, l, prev_sg, seq=seq)
            outs[gi].append(st)
            mixed.append((h, x, mods[gi], lgT))
        if l % 2 == 0:
            wgu, wdn = w_ffn_gate_up[l // 2], w_ffn_down[l // 2]
            xs_ = [_ffn(h, x, mod_g, wgu if gi else wgu.astype(BF16), wdn if gi else wdn.astype(BF16), gfin,
                        l=l, tm=min(1024, x.shape[1]), hi=(gi == 1), final=final)
                   for gi, (h, x, mod_g, _) in enumerate(mixed)]
        else:
            xs_ = _moe_routed(mixed, b_router[l // 2], w_exp_gate_up[l // 2], w_exp_down[l // 2], gfin,
                              l=l, final=final)
    y_p, y_s = xs_
    st_p = [jnp.stack([o[i] for o in outs[0]]) for i in range(4)]
    st_s = [outs[1][-1][2] if (i == 2 and DEPTH == 2) else jnp.stack([o[i] for o in outs[1]])
            for i in range(4)]
    return (y_p, y_s.reshape(nbs, 1, D), st_p[0], st_p[1], st_p[2], st_p[3],
            st_s[0], st_s[1], st_s[2], st_s[3])
```

```python
import functools

import jax
import jax.numpy as jnp
from jax import lax
from jax.experimental import pallas as pl
from jax.experimental.pallas import tpu as pltpu

F32 = jnp.float32
BF16 = jnp.bfloat16
HI = lax.Precision.HIGHEST

D_MODEL = 1024
DEPTH = 2
S5_WIDTH = 512
S5_GROUP = 16
S5_GROUPS = 32
S5_STATE = 64
GDN_HEADS = 4
GDN_DK = 128
GDN_WIDTH = 512
GDN_CONV = 4
QKV_WIDTH = 1536
D_FF = 3584
N_EXPERTS = 8
NORM_EPS = 1e-6
L2_EPS = 1e-6

LANES = 128
SLABS = S5_WIDTH // LANES
SLAB_STATE = (S5_GROUPS // SLABS) * S5_STATE
S5_T = 8
GDN_C = 128
VMEM_LIMIT = 56 * 1024 * 1024


def _cp(sem, vmem=VMEM_LIMIT):
    return pltpu.CompilerParams(dimension_semantics=sem, vmem_limit_bytes=vmem)


def _dot(a, b, prec=None):
    return jnp.dot(a, b, precision=prec, preferred_element_type=F32)


def _dotb(a, b):
    return jnp.dot(a.astype(BF16), b.astype(BF16), preferred_element_type=F32)


def _dot_nt(a, b, prec=None):
    return lax.dot_general(a, b, (((1,), (1,)), ((), ())), precision=prec,
                           preferred_element_type=F32)


def _dot_tn(a, b, prec=None):
    return lax.dot_general(a, b, (((0,), (0,)), ((), ())), precision=prec,
                           preferred_element_type=F32)


def _silu(x):
    return x * jax.nn.sigmoid(x)


def _ada_kernel(c_ref, w_ref, b_ref, o_ref):
    cs = _silu(c_ref[...])
    o_ref[0] = _dot(cs, w_ref[0], HI) + b_ref[0]


def _ada(c_all, w_ada, b_ada):
    n = c_all.shape[0]
    tn = 1536
    return pl.pallas_call(
        _ada_kernel,
        out_shape=jax.ShapeDtypeStruct((DEPTH, n, 6 * D_MODEL), F32),
        grid=(DEPTH, 6 * D_MODEL // tn),
        in_specs=[pl.BlockSpec((n, D_MODEL), lambda l, j: (0, 0)),
                  pl.BlockSpec((1, D_MODEL, tn), lambda l, j: (l, 0, j)),
                  pl.BlockSpec((1, 1, tn), lambda l, j: (l, 0, j))],
        out_specs=pl.BlockSpec((1, n, tn), lambda l, j: (l, 0, j)),
        compiler_params=_cp(("parallel", "parallel")),
        name="ada_mod",
    )(c_all, w_ada, b_ada.reshape(DEPTH, 1, 6 * D_MODEL))


def _proj_kernel(x_ref, g_ref, sc_ref, sh_ref, w_ref, wg_ref, wab_ref,
                 u_ref, qkv_ref, z_ref, ga_ref, gb_ref, ab_ref, h_scr):
    j = pl.program_id(2)

    @pl.when(j == 0)
    def _():
        x = x_ref[0]
        ms = jnp.mean(x * x, axis=-1, keepdims=True)
        xn = x * lax.rsqrt(ms + NORM_EPS) * g_ref[0]
        h_scr[...] = (xn * (1.0 + sc_ref[0, 0]) + sh_ref[0, 0]).astype(h_scr.dtype)

    def mm(w):
        return _dot(h_scr[...], w, HI)

    @pl.when(j == 0)
    def _():
        res = mm(w_ref[0])
        for k in range(SLABS):
            u_ref[k, 0] = res[:, k * LANES:(k + 1) * LANES]

    @pl.when((j >= 1) & (j <= 3))
    def _():
        qkv_ref[0] = mm(w_ref[0])

    @pl.when(j == 4)
    def _():
        z_ref[0] = mm(w_ref[0])

    @pl.when((j == 5) | (j == 6))
    def _():
        ga_ref[0] = jax.nn.sigmoid(mm(wg_ref[0]))

    @pl.when((j == 7) | (j == 8))
    def _():
        gb_ref[0] = jax.nn.sigmoid(mm(wg_ref[0]))

    @pl.when(j == 9)
    def _():
        ab_ref[0] = mm(wab_ref[0])


def _mod_spec(mod, l, chunk, tm):
    per_row = mod.shape[2] != 1
    D = mod.shape[3] // 6

    def index(b, i, *_):
        return (l, b, i if per_row else 0, chunk)

    return pl.BlockSpec((1, 1, tm if per_row else 1, D), index)


def _proj(x, g, mod, w_in, w_gates, w_ab, *, l, tm):
    B, L, D = x.shape
    tn = 512
    clampi = lambda j, lo, n: jnp.clip(j - lo, 0, n - 1)
    outs = pl.pallas_call(
        _proj_kernel,
        out_shape=(jax.ShapeDtypeStruct((SLABS, B, L, LANES), F32),
                   jax.ShapeDtypeStruct((B, L, QKV_WIDTH), F32),
                   jax.ShapeDtypeStruct((B, L, GDN_WIDTH), F32),
                   jax.ShapeDtypeStruct((B, L, D), F32),
                   jax.ShapeDtypeStruct((B, L, D), F32),
                   jax.ShapeDtypeStruct((B, L, LANES), F32)),
        grid=(B, L // tm, 10),
        in_specs=[pl.BlockSpec((1, tm, D), lambda b, i, j: (b, i, 0)),
                  pl.BlockSpec((1, 1, D), lambda b, i, j: (l, 0, 0)),
                  _mod_spec(mod, l, 1, tm),
                  _mod_spec(mod, l, 0, tm),
                  pl.BlockSpec((1, D, tn), lambda b, i, j: (l, 0, jnp.minimum(j, 4))),
                  pl.BlockSpec((1, D, tn), lambda b, i, j: (l, 0, clampi(j, 5, 4))),
                  pl.BlockSpec((1, D, LANES), lambda b, i, j: (l, 0, 0))],
        out_specs=(pl.BlockSpec((SLABS, 1, tm, LANES), lambda b, i, j: (0, b, i, 0)),
                   pl.BlockSpec((1, tm, tn), lambda b, i, j: (b, i, clampi(j, 1, 3))),
                   pl.BlockSpec((1, tm, tn), lambda b, i, j: (b, i, 0)),
                   pl.BlockSpec((1, tm, tn), lambda b, i, j: (b, i, clampi(j, 5, 2))),
                   pl.BlockSpec((1, tm, tn), lambda b, i, j: (b, i, clampi(j, 7, 2))),
                   pl.BlockSpec((1, tm, LANES), lambda b, i, j: (b, i, 0))),
        scratch_shapes=[pltpu.VMEM((tm, D), F32)],
        compiler_params=_cp(("parallel", "parallel", "arbitrary")),
        name="norm_in_proj",
    )(x, g, mod, mod, w_in, w_gates, w_ab)
    return outs


def _proj_seq_kernel(x_ref, g_ref, sc_ref, sh_ref, w_ref, wg_ref, wab_ref,
                     u_ref, qkv_ref, z_ref, ga_ref, gb_ref, ab_ref, us_scr):
    x = x_ref[0]
    ms = jnp.mean(x * x, axis=-1, keepdims=True)
    xn = x * lax.rsqrt(ms + NORM_EPS) * g_ref[0]
    h = (xn * (1.0 + sc_ref[0, 0]) + sh_ref[0, 0]).astype(BF16)
    res = _dot(h, w_ref[0, :, 0:S5_WIDTH])
    nrow = res.shape[0] // S5_T
    for k in range(SLABS):
        us_scr[...] = res[:, k * LANES:(k + 1) * LANES]
        for t in range(S5_T):
            u_ref[k, 0, :, t * LANES:(t + 1) * LANES] = (
                us_scr[pl.ds(t, nrow, stride=S5_T), :].astype(u_ref.dtype))
    c0 = S5_WIDTH
    qkv_ref[0] = _dot(h, w_ref[0, :, c0:c0 + QKV_WIDTH]).astype(qkv_ref.dtype)
    c0 += QKV_WIDTH
    z_ref[0] = _dot(h, w_ref[0, :, c0:c0 + GDN_WIDTH]).astype(z_ref.dtype)
    D = x.shape[-1]
    ga_ref[0] = jax.nn.sigmoid(_dot(h, wg_ref[0, :, 0:D])).astype(ga_ref.dtype)
    gb_ref[0] = jax.nn.sigmoid(_dot(h, wg_ref[0, :, D:2 * D])).astype(gb_ref.dtype)
    ab_ref[0] = _dot(h, wab_ref[0])


def _proj_seq(x, g, mod, w_in, w_gates, w_ab, *, l, tm):
    B, L, D = x.shape
    n_main = S5_WIDTH + QKV_WIDTH + GDN_WIDTH
    row = lambda w: pl.BlockSpec((1, tm, w), lambda b, i: (b, i, 0))
    return pl.pallas_call(
        _proj_seq_kernel,
        out_shape=(jax.ShapeDtypeStruct((SLABS, B, L // S5_T, S5_T * LANES), BF16),
                   jax.ShapeDtypeStruct((B, L, QKV_WIDTH), BF16),
                   jax.ShapeDtypeStruct((B, L, GDN_WIDTH), BF16),
                   jax.ShapeDtypeStruct((B, L, D), BF16),
                   jax.ShapeDtypeStruct((B, L, D), BF16),
                   jax.ShapeDtypeStruct((B, L, LANES), F32)),
        grid=(B, L // tm),
        in_specs=[row(D),
                  pl.BlockSpec((1, 1, D), lambda b, i: (l, 0, 0)),
                  _mod_spec(mod, l, 1, tm),
                  _mod_spec(mod, l, 0, tm),
                  pl.BlockSpec((1, D, n_main), lambda b, i: (l, 0, 0)),
                  pl.BlockSpec((1, D, 2 * D), lambda b, i: (l, 0, 0)),
                  pl.BlockSpec((1, D, LANES), lambda b, i: (l, 0, 0))],
        out_specs=(pl.BlockSpec((SLABS, 1, tm // S5_T, S5_T * LANES), lambda b, i: (0, b, i, 0)),
                   row(QKV_WIDTH), row(GDN_WIDTH), row(D), row(D), row(LANES)),
        scratch_shapes=[pltpu.VMEM((tm, LANES), F32)],
        compiler_params=_cp(("parallel", "parallel")),
        name="norm_in_proj_seq",
    )(x, g, mod, mod, w_in, w_gates, w_ab)


GROUPS_PER_SLAB = S5_GROUPS // SLABS


def _s5_prep_kernel(lrb, lib, dtb, bre, bim, lrc, lic, dtc, cre, cim, lrn, lin, dtn,
                    be_ref, bst_ref, cpe_ref, cpm_ref, pt_ref, a1_ref, *, seg):
    W = SLAB_STATE

    def disc(lr, li, ldt):
        dt = jnp.exp(ldt)
        mag = jnp.exp(lr * dt)
        return mag * jnp.cos(li * dt), mag * jnp.sin(li * dt)

    def cmul(xr, xi, yr, yi):
        return xr * yr - xi * yi, xr * yi + xi * yr

    lr, li = lrb[0], lib[0]
    ar, ai = disc(lr, li, dtb[0])
    den = lr * lr + li * li
    nr = ar - 1.0
    kr = (nr * lr + ai * li) / den
    ki = (ai * lr - nr * li) / den
    br, bi = bre[0], bim[0]
    bbr = kr * br - ki * bi
    bbi = kr * bi + ki * br
    rgrp = lax.broadcasted_iota(jnp.int32, (LANES, LANES), 0) // S5_GROUP
    lane_hi = lax.broadcasted_iota(jnp.int32, (LANES, LANES), 1) // S5_STATE
    pr, pi = jnp.ones_like(ar), jnp.zeros_like(ar)
    for d in range(S5_T):
        t = S5_T - 1 - d
        for ri, val in enumerate(cmul(pr, pi, bbr, bbi)):
            two = jnp.concatenate([val, val], axis=1)
            for m in range(GROUPS_PER_SLAB // 2):
                tile = jnp.where(rgrp == 2 * m + lane_hi, two, 0.0)
                c0 = ri * W + m * LANES
                be_ref[0, 0, t * LANES:(t + 1) * LANES, c0:c0 + LANES] = tile.astype(BF16)
                if d == 0:
                    bst_ref[0, 0, :, c0:c0 + LANES] = tile
        pr, pi = cmul(pr, pi, ar, ai)

    ar, ai = disc(lrc[0], lic[0], dtc[0])
    cr, ci = cre[0], cim[0]
    own = (lax.broadcasted_iota(jnp.int32, (W, LANES), 0) // S5_STATE
           == lax.broadcasted_iota(jnp.int32, (W, LANES), 1) // S5_GROUP)
    pr, pi = jnp.ones_like(ar), jnp.zeros_like(ar)
    for d in range(S5_T + 1):
        vr, vi = cmul(cr, ci, pr, pi)
        for ri, val in enumerate((vr, -vi)):
            tile = jnp.where(own, val, 0.0)
            cpe_ref[0, 0, d, ri * W:(ri + 1) * W, :] = tile
            if d >= 1:
                cpm_ref[0, 0, ri * W:(ri + 1) * W, (d - 1) * LANES:d * LANES] = tile.astype(BF16)
        pr, pi = cmul(pr, pi, ar, ai)

    ar, ai = disc(lrn[0, 0], lin[0, 0], dtn[0, 0])
    a1_ref[0, 0, :, 0:W] = ar
    a1_ref[0, 0, :, W:2 * W] = ai
    tr, ti = ar, ai
    for _ in range(S5_T - 1):
        tr, ti = cmul(tr, ti, ar, ai)
    pr, pi = jnp.ones_like(ar), jnp.zeros_like(ar)
    for i in range(seg + 1):
        pt_ref[0, 0, i:i + 1, 0:W] = pr
        pt_ref[0, 0, i:i + 1, W:2 * W] = pi
        pr, pi = cmul(pr, pi, tr, ti)


def _s5_prep(lam_re, lam_im, log_dt, b_re, b_im, c_re, c_im, seg):
    G, P, C = S5_GROUPS, S5_STATE, S5_GROUP
    W2 = 2 * SLAB_STATE
    dt3 = jnp.broadcast_to(log_dt[:, :, None], (DEPTH, G, P))
    rows_b = lambda a: jnp.repeat(a, C, axis=1)
    bt = lambda a: a.transpose(0, 1, 3, 2).reshape(DEPTH, G * C, P)
    rows_c = lambda a: jnp.broadcast_to(a.reshape(DEPTH, G * P, 1), (DEPTH, G * P, LANES))
    ct = lambda a: jnp.tile(a.transpose(0, 1, 3, 2).reshape(DEPTH, G * P, C), (1, 1, LANES // C))
    nat = lambda a: a.reshape(DEPTH, SLABS, 1, SLAB_STATE)
    args = (rows_b(lam_re), rows_b(lam_im), rows_b(dt3), bt(b_re), bt(b_im),
            rows_c(lam_re), rows_c(lam_im), rows_c(dt3), ct(c_re), ct(c_im),
            nat(lam_re), nat(lam_im), nat(dt3))
    bspec = pl.BlockSpec((1, LANES, P), lambda l, k: (l, k, 0))
    cspec = pl.BlockSpec((1, SLAB_STATE, LANES), lambda l, k: (l, k, 0))
    nspec = pl.BlockSpec((1, 1, 1, SLAB_STATE), lambda l, k: (l, k, 0, 0))
    return pl.pallas_call(
        functools.partial(_s5_prep_kernel, seg=seg),
        out_shape=(jax.ShapeDtypeStruct((DEPTH, SLABS, S5_T * LANES, W2), BF16),
                   jax.ShapeDtypeStruct((DEPTH, SLABS, LANES, W2), F32),
                   jax.ShapeDtypeStruct((DEPTH, SLABS, S5_T + 1, W2, LANES), F32),
                   jax.ShapeDtypeStruct((DEPTH, SLABS, W2, S5_T * LANES), BF16),
                   jax.ShapeDtypeStruct((DEPTH, SLABS, seg + 1, W2), F32),
                   jax.ShapeDtypeStruct((DEPTH, SLABS, 1, W2), F32)),
        grid=(DEPTH, SLABS),
        in_specs=[bspec] * 5 + [cspec] * 5 + [nspec] * 3,
        out_specs=(pl.BlockSpec((1, 1, S5_T * LANES, W2), lambda l, k: (l, k, 0, 0)),
                   pl.BlockSpec((1, 1, LANES, W2), lambda l, k: (l, k, 0, 0)),
                   pl.BlockSpec((1, 1, S5_T + 1, W2, LANES), lambda l, k: (l, k, 0, 0, 0)),
                   pl.BlockSpec((1, 1, W2, S5_T * LANES), lambda l, k: (l, k, 0, 0)),
                   pl.BlockSpec((1, 1, seg + 1, W2), lambda l, k: (l, k, 0, 0)),
                   pl.BlockSpec((1, 1, 1, W2), lambda l, k: (l, k, 0, 0))),
        compiler_params=_cp(("parallel", "parallel")),
        name="s5_discretize",
    )(*args)


def _toep_kernel(b_ref, c_ref, o_ref):
    dd = pl.program_id(2)
    bst = b_ref[0, 0]
    lag = lambda d: _dot(bst, c_ref[0, 0, d], HI)
    k0 = lag(2 * dd)
    o_ref[0, 0, 0, 0:LANES, 0:LANES] = k0.astype(BF16)
    o_ref[0, 0, 0, LANES:, LANES:] = k0.astype(BF16)
    o_ref[0, 0, 0, 0:LANES, LANES:] = lag(2 * dd + 1).astype(BF16)
    km = lag(jnp.maximum(2 * dd - 1, 0))
    o_ref[0, 0, 0, LANES:, 0:LANES] = jnp.where(dd > 0, km, 0.0).astype(BF16)


def _toep(bst, cpe):
    W2 = 2 * SLAB_STATE
    return pl.pallas_call(
        _toep_kernel,
        out_shape=jax.ShapeDtypeStruct((DEPTH, SLABS, S5_T // 2, 2 * LANES, 2 * LANES), BF16),
        grid=(DEPTH, SLABS, S5_T // 2),
        in_specs=[pl.BlockSpec((1, 1, LANES, W2), lambda l, k, d: (l, k, 0, 0)),
                  pl.BlockSpec((1, 1, S5_T + 1, W2, LANES), lambda l, k, d: (l, k, 0, 0, 0))],
        out_specs=pl.BlockSpec((1, 1, 1, 2 * LANES, 2 * LANES), lambda l, k, d: (l, k, d, 0, 0)),
        compiler_params=_cp(("parallel", "parallel", "parallel")),
        name="s5_conv_blocks",
    )(bst, cpe)


def _s5_seq_kernel(up_ref, be_ref, tp_ref, cpm_ref, pt_ref, s0_ref, dsk_ref,
                   yg_ref, sfin_ref, e_scr, sx_scr, *, nc):
    seg = nc // 8
    W = SLAB_STATE
    nt = W // LANES
    ub = up_ref[0, 0]
    u = ub.astype(F32)
    e = _dot(ub, be_ref[0, 0])
    for c in range(2 * nt):
        e_scr[c] = e[:, c * LANES:(c + 1) * LANES]

    def tiles(row):
        return [(row[:, c * LANES:(c + 1) * LANES], row[:, W + c * LANES:W + (c + 1) * LANES])
                for c in range(nt)]

    a8 = [(jnp.broadcast_to(r, (8, LANES)), jnp.broadcast_to(i, (8, LANES)))
          for r, i in tiles(pt_ref[0, 0, 1:2, :])]

    def step(i, carry):
        rows = pl.ds(i, 8, stride=seg)
        new = []
        for c in range(nt):
            sr, si = carry[c]
            ar, ai = a8[c]
            sx_scr[c, rows, :] = sr
            sx_scr[nt + c, rows, :] = si
            new.append((ar * sr - ai * si + e_scr[c, rows, :],
                        ar * si + ai * sr + e_scr[nt + c, rows, :]))
        return tuple(new)

    zero = jnp.zeros((8, LANES), F32)
    ends = lax.fori_loop(0, seg, step, tuple((zero, zero) for _ in range(nt)))

    al = tiles(pt_ref[0, 0, seg:seg + 1, :])
    cur = tiles(s0_ref[0, 0])
    car = []
    for c in range(nt):
        alr, ali = al[c]
        cr, ci = cur[c]
        sr, si = ends[c]
        crs, cis = [], []
        for j in range(8):
            crs.append(cr)
            cis.append(ci)
            cr, ci = (alr * cr - ali * ci + sr[j:j + 1], alr * ci + ali * cr + si[j:j + 1])
        sfin_ref[0, 0, :, c * LANES:(c + 1) * LANES] = cr
        sfin_ref[0, 0, :, W + c * LANES:W + (c + 1) * LANES] = ci
        car.append((jnp.concatenate(crs, axis=0), jnp.concatenate(cis, axis=0)))

    def corr(i, _):
        rows = pl.ds(i, 8, stride=seg)
        pw = tiles(pt_ref[0, 0, pl.ds(i, 1), :])
        for c in range(nt):
            pr, pi = pw[c]
            cr, ci = car[c]
            sx_scr[c, rows, :] = sx_scr[c, rows, :] + (pr * cr - pi * ci)
            sx_scr[nt + c, rows, :] = sx_scr[nt + c, rows, :] + (pr * ci + pi * cr)
        return 0

    lax.fori_loop(0, seg, corr, 0)

    sx = jnp.concatenate([sx_scr[c] for c in range(2 * nt)], axis=-1)
    y = _dot(sx.astype(BF16), cpm_ref[0, 0])
    TW = 2 * LANES
    for tq in range(S5_T // 2):
        acc = y[:, tq * TW:(tq + 1) * TW]
        for tpi in range(tq + 1):
            acc = acc + _dot(ub[:, tpi * TW:(tpi + 1) * TW], tp_ref[0, 0, tq - tpi])
        acc = acc + dsk_ref[0, :, tq * TW:(tq + 1) * TW] * u[:, tq * TW:(tq + 1) * TW]
        yg_ref[0, 0, :, tq * TW:(tq + 1) * TW] = jax.nn.gelu(acc).astype(yg_ref.dtype)


def _s5_seq(up, be_emb, tp, cpm, pt, s0, dsk, l):
    _, B, nc, _ = up.shape
    seg = nc // 8
    W2 = 2 * SLAB_STATE
    yg, sfin = pl.pallas_call(
        functools.partial(_s5_seq_kernel, nc=nc),
        out_shape=(jax.ShapeDtypeStruct((SLABS, B, nc, S5_T * LANES), BF16),
                   jax.ShapeDtypeStruct((SLABS, B, 1, W2), F32)),
        grid=(SLABS, B),
        in_specs=[pl.BlockSpec((1, 1, nc, S5_T * LANES), lambda k, b: (k, b, 0, 0)),
                  pl.BlockSpec((1, 1, S5_T * LANES, W2), lambda k, b: (l, k, 0, 0)),
                  pl.BlockSpec((1, 1, S5_T // 2, 2 * LANES, 2 * LANES), lambda k, b: (l, k, 0, 0, 0)),
                  pl.BlockSpec((1, 1, W2, S5_T * LANES), lambda k, b: (l, k, 0, 0)),
                  pl.BlockSpec((1, 1, seg + 1, W2), lambda k, b: (l, k, 0, 0)),
                  pl.BlockSpec((1, 1, 1, W2), lambda k, b: (k, b, 0, 0)),
                  pl.BlockSpec((1, 1, S5_T * LANES), lambda k, b: (k, 0, 0))],
        out_specs=(pl.BlockSpec((1, 1, nc, S5_T * LANES), lambda k, b: (k, b, 0, 0)),
                   pl.BlockSpec((1, 1, 1, W2), lambda k, b: (k, b, 0, 0))),
        scratch_shapes=[pltpu.VMEM((W2 // LANES, nc, LANES), F32),
                        pltpu.VMEM((W2 // LANES, nc, LANES), F32)],
        compiler_params=_cp(("parallel", "parallel")),
        name="s5_seq",
    )(up, be_emb, tp, cpm, pt, s0, dsk)
    return yg, sfin


def _s5_step_kernel(u_ref, b_ref, c_ref, a_ref, s0_ref, d_ref, yg_ref, s1_ref):
    W = SLAB_STATE
    u = u_ref[0]
    bu = _dot(u, b_ref[0, 0], HI)
    ar = a_ref[0, 0, :, 0:W]
    ai = a_ref[0, 0, :, W:2 * W]
    sr = s0_ref[0, :, 0:W]
    si = s0_ref[0, :, W:2 * W]
    nr = ar * sr - ai * si + bu[:, 0:W]
    ni = ar * si + ai * sr + bu[:, W:2 * W]
    s1_ref[0, :, 0:W] = nr
    s1_ref[0, :, W:2 * W] = ni
    s1 = jnp.concatenate([nr, ni], axis=-1)
    y = _dot(s1, c_ref[0, 0, 0], HI) + d_ref[0] * u
    yg_ref[0] = jax.nn.gelu(y)


def _s5_step(u_slab, bst, cpe, a1, s0, d1, l):
    _, N, _ = u_slab.shape
    W2 = 2 * SLAB_STATE
    return pl.pallas_call(
        _s5_step_kernel,
        out_shape=(jax.ShapeDtypeStruct((SLABS, N, LANES), F32),
                   jax.ShapeDtypeStruct((SLABS, N, W2), F32)),
        grid=(SLABS,),
        in_specs=[pl.BlockSpec((1, N, LANES), lambda k: (k, 0, 0)),
                  pl.BlockSpec((1, 1, LANES, W2), lambda k: (l, k, 0, 0)),
                  pl.BlockSpec((1, 1, 1, W2, LANES), lambda k: (l, k, 0, 0, 0)),
                  pl.BlockSpec((1, 1, 1, W2), lambda k: (l, k, 0, 0)),
                  pl.BlockSpec((1, N, W2), lambda k: (k, 0, 0)),
                  pl.BlockSpec((1, 1, LANES), lambda k: (k, 0, 0))],
        out_specs=(pl.BlockSpec((1, N, LANES), lambda k: (k, 0, 0)),
                   pl.BlockSpec((1, N, W2), lambda k: (k, 0, 0))),
        compiler_params=_cp(("parallel",)),
        name="s5_step",
    )(u_slab, bst, cpe, a1, s0, d1)


def _l2n(x):
    return x * lax.rsqrt(jnp.sum(x * x, axis=-1, keepdims=True) + L2_EPS)


def _split_bf16(x):
    hi = x.astype(BF16)
    return hi, (x - hi.astype(F32)).astype(BF16)


def _unit_lower_solve(As, rhss):
    n = GDN_C
    row = lax.broadcasted_iota(jnp.int32, (n, n), 0)
    col = lax.broadcasted_iota(jnp.int32, (n, n), 1)
    eye = (row == col).astype(F32)
    same8 = (row // 8) == (col // 8)
    Qs = [jnp.where(same8, -A, 0.0) for A in As]
    invs = [eye + Q for Q in Qs]
    for _ in range(2):
        Qs = [_dotb(Q, Q) for Q in Qs]
        invs = [inv + _dotb(inv, Q) for inv, Q in zip(invs, Qs)]
    s = 8
    while s < n:
        sib = ((row // (2 * s)) == (col // (2 * s))) & ((row // s) != (col // s))
        offs = [jnp.where(sib, A, 0.0).astype(BF16) for A in As]
        invb = [inv.astype(BF16) for inv in invs]
        tmp = [_dot(off, ib) for off, ib in zip(offs, invb)]
        invs = [inv - _dot(ib, t.astype(BF16)) for inv, ib, t in zip(invs, invb, tmp)]
        s *= 2
    invb = [inv.astype(BF16) for inv in invs]
    x0s = [_dot(ib, rhs.astype(BF16)) for ib, rhs in zip(invb, rhss)]
    res = []
    for A, x0, rhs in zip(As, x0s, rhss):
        ah, al = _split_bf16(A)
        xh, xl = _split_bf16(x0)
        res.append(rhs - x0 - (_dot(ah, xh) + _dot(ah, xl) + _dot(al, xh)))
    return [x0 + _dot(ib, r.astype(BF16)) for x0, ib, r in zip(x0s, invb, res)]


def _gdn_tile(qc_scr, gc, beta, z_ref, nw, o_ref, s_scr, tl):
    C, DK, H = GDN_C, GDN_DK, GDN_HEADS
    nchunk = tl // C
    probs = [(c, h) for c in range(nchunk) for h in range(H)]
    row = lax.broadcasted_iota(jnp.int32, (C, C), 0)
    col = lax.broadcasted_iota(jnp.int32, (C, C), 1)
    tri = row >= col
    strict = row > col

    def blk(c, off):
        return qc_scr[c * C:(c + 1) * C, off:off + DK]

    q = [_l2n(blk(c, h * DK)) * (DK ** -0.5) for c, h in probs]
    k = [_l2n(blk(c, GDN_WIDTH + h * DK)) for c, h in probs]
    v = [blk(c, 2 * GDN_WIDTH + h * DK) for c, h in probs]
    gcb = [jnp.broadcast_to(gc[c * C:(c + 1) * C, h:h + 1], (C, DK)) for c, h in probs]
    bb = [jnp.broadcast_to(beta[c * C:(c + 1) * C, H + h:H + h + 1], (C, DK)) for c, h in probs]
    decay = []
    for g in gcb:
        diff = g - g.T
        decay.append(jnp.where(tri, jnp.exp(jnp.where(tri, diff, 0.0)), 0.0))
    kbf = [x.astype(BF16) for x in k]
    kb = [x * b for x, b in zip(k, bb)]
    A = [jnp.where(strict, _dot_nt(x.astype(BF16), y) * d, 0.0) for x, y, d in zip(kb, kbf, decay)]
    egc = [jnp.exp(g) for g in gcb]
    rhs = [jnp.concatenate([x * b, y * e], axis=-1) for x, b, y, e in zip(v, bb, kb, egc)]
    sol = _unit_lower_solve(A, rhs)
    attn = [jnp.where(tri, _dot_nt(x.astype(BF16), y) * d, 0.0).astype(BF16)
            for x, y, d in zip(q, kbf, decay)]
    glast = [g[C - 1:C, :] for g in gcb]
    wq = [jnp.concatenate([s[:, DK:], x * e], axis=0).astype(BF16) for s, x, e in zip(sol, q, egc)]
    kg = [(x * jnp.exp(gl - g)).astype(BF16) for x, gl, g in zip(k, glast, gcb)]

    for c in range(nchunk):
        ps = [c * H + h for h in range(H)]
        S = [s_scr[h] for h in range(H)]
        ws = [_dot(wq[p], S[h].astype(BF16)) for h, p in enumerate(ps)]
        v_new = [sol[p][:, 0:DK] - w[0:C] for p, w in zip(ps, ws)]
        vb = [x.astype(BF16) for x in v_new]
        o = [w[C:] + _dot(attn[p], x) for p, w, x in zip(ps, ws, vb)]
        for h, p in enumerate(ps):
            s_scr[h] = S[h] * jnp.exp(glast[p]) + _dot_tn(kg[p], vb[h])
            zh = z_ref[0, c * C:(c + 1) * C, h * DK:(h + 1) * DK].astype(F32)
            on = o[h] * lax.rsqrt(jnp.mean(o[h] * o[h], axis=-1, keepdims=True) + NORM_EPS) * nw
            o_ref[0, c * C:(c + 1) * C, h * DK:(h + 1) * DK] = (on * _silu(zh)).astype(o_ref.dtype)


def _gdn_seq_kernel(qkv_ref, z_ref, ab_ref, cw_ref, alog_ref, dtb_ref, nw_ref, conv0_ref, s0_ref,
                    o_ref, sfin_ref, xp_scr, qc_scr, s_scr, *, tl):
    lt = pl.program_id(1)

    @pl.when(lt == 0)
    def _():
        xp_scr[0:8, :] = jnp.zeros((8, QKV_WIDTH), F32)
        xp_scr[8 - (GDN_CONV - 1):8, :] = conv0_ref[0]
        s_scr[...] = s0_ref[0]

    xp_scr[8:8 + tl, :] = qkv_ref[0].astype(F32)
    conv = cw_ref[0, 0:1, :] * xp_scr[5:5 + tl, :]
    for j in range(1, GDN_CONV):
        conv = conv + cw_ref[0, j:j + 1, :] * xp_scr[5 + j:5 + j + tl, :]
    xp_scr[0:8, :] = xp_scr[tl:tl + 8, :]
    qc_scr[...] = _silu(conv)

    ab = ab_ref[0]
    g = -jnp.exp(alog_ref[...]) * jax.nn.softplus(ab + dtb_ref[...])
    beta = jax.nn.sigmoid(ab)
    row = lax.broadcasted_iota(jnp.int32, (tl, tl), 0)
    col = lax.broadcasted_iota(jnp.int32, (tl, tl), 1)
    csum = ((row >= col) & ((row // GDN_C) == (col // GDN_C))).astype(F32)
    gc = _dot(csum, g, HI)
    _gdn_tile(qc_scr, gc, beta, z_ref, nw_ref[...], o_ref, s_scr, tl)

    @pl.when(lt == pl.num_programs(1) - 1)
    def _():
        sfin_ref[0] = s_scr[...]


def _gdn_seq(qkv, z, ab, conv_w, alog, dtb, nw, conv0, s0, l):
    B, L, _ = qkv.shape
    tl = min(256, L)
    return pl.pallas_call(
        functools.partial(_gdn_seq_kernel, tl=tl),
        out_shape=(jax.ShapeDtypeStruct((B, L, GDN_WIDTH), BF16),
                   jax.ShapeDtypeStruct((B, GDN_HEADS, GDN_DK, GDN_DK), F32)),
        grid=(B, L // tl),
        in_specs=[pl.BlockSpec((1, tl, QKV_WIDTH), lambda b, i: (b, i, 0)),
                  pl.BlockSpec((1, tl, GDN_WIDTH), lambda b, i: (b, i, 0)),
                  pl.BlockSpec((1, tl, LANES), lambda b, i: (b, i, 0)),
                  pl.BlockSpec((1, GDN_CONV, QKV_WIDTH), lambda b, i: (l, 0, 0)),
                  pl.BlockSpec((1, LANES), lambda b, i: (0, 0)),
                  pl.BlockSpec((1, LANES), lambda b, i: (0, 0)),
                  pl.BlockSpec((1, GDN_DK), lambda b, i: (0, 0)),
                  pl.BlockSpec((1, GDN_CONV - 1, QKV_WIDTH), lambda b, i: (b, 0, 0)),
                  pl.BlockSpec((1, GDN_HEADS, GDN_DK, GDN_DK), lambda b, i: (b, 0, 0, 0))],
        out_specs=(pl.BlockSpec((1, tl, GDN_WIDTH), lambda b, i: (b, i, 0)),
                   pl.BlockSpec((1, GDN_HEADS, GDN_DK, GDN_DK), lambda b, i: (b, 0, 0, 0))),
        scratch_shapes=[pltpu.VMEM((tl + 8, QKV_WIDTH), F32),
                        pltpu.VMEM((tl, QKV_WIDTH), F32),
                        pltpu.VMEM((GDN_HEADS, GDN_DK, GDN_DK), F32)],
        compiler_params=_cp(("parallel", "arbitrary")),
        name="gdn_seq",
    )(qkv, z, ab, conv_w, alog, dtb, nw, conv0, s0)


GDN_STEP_ROWS = 8


def _gdn_step_kernel(qkv_ref, z_ref, ab_ref, cw_ref, alog_ref, dtb_ref, nw_ref, conv0_ref, s0_ref,
                     *rest):
    if len(rest) == 3:
        prev_ref, o_ref, s1_all = rest
        s1_all[0] = prev_ref[...]
        s1_ref = s1_all.at[1]
    else:
        o_ref, s1_all = rest
        s1_ref = s1_all
    nb = GDN_STEP_ROWS
    W = QKV_WIDTH
    conv = cw_ref[0:1, :] * conv0_ref[:, 0:W]
    conv = conv + cw_ref[1:2, :] * conv0_ref[:, W:2 * W]
    conv = conv + cw_ref[2:3, :] * conv0_ref[:, 2 * W:3 * W]
    conv = conv + cw_ref[3:4, :] * qkv_ref[...]
    qc = _silu(conv)
    ab = ab_ref[...]
    eg = jnp.exp(-jnp.exp(alog_ref[...]) * jax.nn.softplus(ab + dtb_ref[...]))
    beta = jax.nn.sigmoid(ab)
    eye = (lax.broadcasted_iota(jnp.int32, (GDN_DK, GDN_DK), 0)
           == lax.broadcasted_iota(jnp.int32, (GDN_DK, GDN_DK), 1)).astype(F32)
    for h in range(GDN_HEADS):
        q = _l2n(qc[:, h * GDN_DK:(h + 1) * GDN_DK]) * (GDN_DK ** -0.5)
        k = _l2n(qc[:, GDN_WIDTH + h * GDN_DK:GDN_WIDTH + (h + 1) * GDN_DK])
        v = qc[:, 2 * GDN_WIDTH + h * GDN_DK:2 * GDN_WIDTH + (h + 1) * GDN_DK]
        kT = _dot_nt(eye, k, HI)
        qT = _dot_nt(eye, q, HI)
        qk = jnp.sum(q * k, axis=-1, keepdims=True)
        for j in range(nb):
            S = s0_ref[0, j, h]
            kc = jnp.broadcast_to(kT[:, j:j + 1], (GDN_DK, GDN_DK))
            qcb = jnp.broadcast_to(qT[:, j:j + 1], (GDN_DK, GDN_DK))
            kS = jnp.sum(kc * S, axis=0, keepdims=True)
            qS = jnp.sum(qcb * S, axis=0, keepdims=True)
            egj = eg[j:j + 1, h:h + 1]
            bj = beta[j:j + 1, GDN_HEADS + h:GDN_HEADS + h + 1]
            v_new = bj * v[j:j + 1, :] - (bj * egj) * kS
            o = egj * qS + qk[j:j + 1, :] * v_new
            s1_ref[j, h] = S * egj + kc * v_new
            zh = z_ref[j:j + 1, h * GDN_DK:(h + 1) * GDN_DK]
            on = o * lax.rsqrt(jnp.mean(o * o, axis=-1, keepdims=True) + NORM_EPS) * nw_ref[...]
            o_ref[j:j + 1, h * GDN_DK:(h + 1) * GDN_DK] = on * _silu(zh)


def _gdn_step(qkv, z, ab, conv_w, alog, dtb, nw, conv0, s_all, l, prev):
    N = qkv.shape[0]
    nb = GDN_STEP_ROWS
    row = lambda w: pl.BlockSpec((nb, w), lambda i: (i, 0))
    const = lambda r, w: pl.BlockSpec((r, w), lambda i: (0, 0))
    sblk = (nb, GDN_HEADS, GDN_DK, GDN_DK)
    one = pl.BlockSpec(sblk, lambda i: (i, 0, 0, 0))
    ins = [qkv, z, ab, conv_w, alog, dtb, nw, conv0, s_all]
    in_specs = [row(QKV_WIDTH), row(GDN_WIDTH), row(LANES), const(GDN_CONV, QKV_WIDTH),
                const(1, LANES), const(1, LANES), const(1, GDN_DK), row(3 * QKV_WIDTH),
                pl.BlockSpec((1,) + sblk, lambda i: (l, i, 0, 0, 0))]
    if prev is None:
        s_shape, s_spec = jax.ShapeDtypeStruct((N,) + sblk[1:], F32), one
    else:
        assert DEPTH == 2 and l == 1
        ins.append(prev)
        in_specs.append(one)
        s_shape = jax.ShapeDtypeStruct((DEPTH, N) + sblk[1:], F32)
        s_spec = pl.BlockSpec((DEPTH,) + sblk, lambda i: (0, i, 0, 0, 0))
    return pl.pallas_call(
        _gdn_step_kernel,
        out_shape=(jax.ShapeDtypeStruct((N, GDN_WIDTH), F32), s_shape),
        grid=(N // nb,),
        in_specs=in_specs,
        out_specs=(row(GDN_WIDTH), s_spec),
        compiler_params=_cp(("parallel",)),
        name="gdn_step",
    )(*ins)


def _merge_kernel(yg_ref, og_ref, ga_ref, gb_ref, x_ref, gt_ref, wglu_ref, wgo_ref, wout_ref,
                  gf_ref, scf_ref, shf_ref, wr_ref,
                  xo_ref, h_ref, lg_ref, *scr, hi, chunked):
    if chunked:
        y_scr = scr[-1]
        scr = scr[:-1]
        nrow = y_scr.shape[1] // S5_T
        for k in range(SLABS):
            for t in range(S5_T):
                y_scr[k, pl.ds(t, nrow, stride=S5_T), :] = (
                    yg_ref[k, 0, :, t * LANES:(t + 1) * LANES].astype(F32))
        y = jnp.concatenate([y_scr[k] for k in range(SLABS)], axis=-1)
    else:
        y = jnp.concatenate([yg_ref[k, 0] for k in range(SLABS)], axis=-1)
    if hi:
        wglu, wgo, wout = wglu_ref[0], wgo_ref[0], wout_ref[0]
        mm = lambda a, w: _dot(a, w, HI)
    else:
        wglu_s, wgo_s, wout_s = scr

        @pl.when((pl.program_id(0) == 0) & (pl.program_id(1) == 0))
        def _():
            wglu_s[...] = wglu_ref[0].astype(BF16)
            wgo_s[...] = wgo_ref[0].astype(BF16)
            wout_s[...] = wout_ref[0].astype(BF16)

        wglu, wgo, wout = wglu_s[...], wgo_s[...], wout_s[...]
        mm = lambda a, w: _dot(a.astype(BF16), w)

    glu = mm(y, wglu)
    branch_a = glu[:, 0:D_MODEL] * jax.nn.sigmoid(glu[:, D_MODEL:])
    branch_b = mm(og_ref[0], wgo)
    merged = ga_ref[0].astype(F32) * branch_a + gb_ref[0].astype(F32) * branch_b
    out = mm(merged, wout)
    x = x_ref[0] + gt_ref[0, 0] * out
    xo_ref[0] = x
    ms = jnp.mean(x * x, axis=-1, keepdims=True)
    h = x * lax.rsqrt(ms + NORM_EPS) * gf_ref[0]
    h = h * (1.0 + scf_ref[0, 0]) + shf_ref[0, 0]
    h_ref[0] = h.astype(h_ref.dtype)
    lg_ref[0] = _dot_nt(wr_ref[0], h, HI)


def _merge(yg, og, ga, gb, x, mod, wglu, wgo, wout, gf, wr, *, l, tm, hi, chunked, h_dtype):
    B, L, D = x.shape
    row = lambda w: pl.BlockSpec((1, tm, w), lambda b, i: (b, i, 0))
    layer = lambda r, w, ll=l: pl.BlockSpec((1, r, w), lambda b, i: (ll, 0, 0))
    scratch = [] if hi else [pltpu.VMEM((S5_WIDTH, 2 * D), BF16), pltpu.VMEM((GDN_WIDTH, D), BF16),
                             pltpu.VMEM((D, D), BF16)]
    if chunked:
        scratch = scratch + [pltpu.VMEM((SLABS, tm, LANES), F32)]
        yg_spec = pl.BlockSpec((SLABS, 1, tm // S5_T, S5_T * LANES), lambda b, i: (0, b, i, 0))
    else:
        yg_spec = pl.BlockSpec((SLABS, 1, tm, LANES), lambda b, i: (0, b, i, 0))
    lg_shape = jax.ShapeDtypeStruct((B, N_EXPERTS, L), F32)
    lg_spec = pl.BlockSpec((1, N_EXPERTS, tm), lambda b, i: (b, 0, i))
    return pl.pallas_call(
        functools.partial(_merge_kernel, hi=hi, chunked=chunked),
        out_shape=(jax.ShapeDtypeStruct((B, L, D), F32),
                   jax.ShapeDtypeStruct((B, L, D), h_dtype),
                   lg_shape),
        grid=(B, L // tm),
        in_specs=[yg_spec,
                  row(GDN_WIDTH), row(D), row(D), row(D), _mod_spec(mod, l, 2, tm),
                  layer(S5_WIDTH, 2 * D), layer(GDN_WIDTH, D), layer(D, D),
                  layer(1, D), _mod_spec(mod, l, 4, tm), _mod_spec(mod, l, 3, tm),
                  layer(N_EXPERTS, D, l // 2)],
        out_specs=(row(D), row(D), lg_spec),
        scratch_shapes=scratch,
        compiler_params=_cp(("arbitrary", "arbitrary")),
        name="merge_out_proj",
    )(yg, og, ga, gb, x, mod, wglu, wgo, wout, gf, mod, mod, wr)


FF_TILE = 896


def _finish(x, gfin_ref, final):
    if not final:
        return x
    ms = jnp.mean(x * x, axis=-1, keepdims=True)
    return x * lax.rsqrt(ms + NORM_EPS) * gfin_ref[...]


def _ffn_kernel(h_ref, x_ref, gt_ref, wg_ref, wu_ref, wd_ref, gfin_ref, o_ref, acc_scr, *, hi, final):
    j = pl.program_id(2)
    if hi:
        h = h_ref[0]
        mm = lambda a, w: _dot(a, w, HI)
    else:
        h = h_ref[0].astype(BF16)
        mm = lambda a, w: _dot(a.astype(BF16), w.astype(BF16))
    act = _silu(mm(h, wg_ref[...])) * mm(h, wu_ref[...])
    part = mm(act, wd_ref[...])

    @pl.when(j == 0)
    def _():
        acc_scr[...] = part

    @pl.when(j > 0)
    def _():
        acc_scr[...] = acc_scr[...] + part

    @pl.when(j == pl.num_programs(2) - 1)
    def _():
        o_ref[0] = _finish(x_ref[0] + gt_ref[0, 0] * acc_scr[...], gfin_ref, final)


def _ffn(h, x, mod, w_gu, w_down, gfin, *, l, tm, hi, final):
    B, L, D = x.shape
    nj = D_FF // FF_TILE
    row = pl.BlockSpec((1, tm, D), lambda b, i, j: (b, i, 0))
    return pl.pallas_call(
        functools.partial(_ffn_kernel, hi=hi, final=final),
        out_shape=jax.ShapeDtypeStruct((B, L, D), F32),
        grid=(B, L // tm, nj),
        in_specs=[row, row, _mod_spec(mod, l, 5, tm),
                  pl.BlockSpec((D, FF_TILE), lambda b, i, j: (0, j)),
                  pl.BlockSpec((D, FF_TILE), lambda b, i, j: (0, nj + j)),
                  pl.BlockSpec((FF_TILE, D), lambda b, i, j: (j, 0)),
                  pl.BlockSpec((1, D), lambda b, i, j: (0, 0))],
        out_specs=row,
        scratch_shapes=[pltpu.VMEM((tm, D), F32)],
        compiler_params=_cp(("parallel", "parallel", "arbitrary")),
        name="ffn_dense",
    )(h, x, mod, w_gu, w_gu, w_down, gfin)


ROUTE_TM = 512
ROW_DMA_TM = 256
MOE_SUP = 2048
MOE_SUB = 512
MOE_FF_TILE = 512


def _route_kernel(lg_ref, br_ref, cnt0_ref, slot_ref, wt_ref, cnt_ref, carry_scr, *, cap):
    @pl.when((pl.program_id(0) == 0) & (pl.program_id(1) == 0))
    def _():
        carry_scr[...] = cnt0_ref[...]

    lg = lg_ref[0] + br_ref[...]
    tm = lg.shape[1]
    eidx = lax.broadcasted_iota(jnp.int32, lg.shape, 0)
    m1 = jnp.max(lg, axis=0, keepdims=True)
    i1 = jnp.min(jnp.where(lg == m1, eidx, N_EXPERTS), axis=0, keepdims=True)
    lg2 = jnp.where(eidx == i1, -jnp.inf, lg)
    m2 = jnp.max(lg2, axis=0, keepdims=True)
    i2 = jnp.min(jnp.where(lg2 == m2, eidx, N_EXPERTS), axis=0, keepdims=True)
    e2 = jnp.exp(m2 - m1)
    wt_ref[0, 0:1, :] = 1.0 / (1.0 + e2)
    wt_ref[0, 1:2, :] = e2 / (1.0 + e2)
    sel1 = eidx == i1
    sel2 = eidx == i2
    oh = jnp.where(sel1 | sel2, 1.0, 0.0)
    before = (lax.broadcasted_iota(jnp.int32, (tm, tm), 0)
              < lax.broadcasted_iota(jnp.int32, (tm, tm), 1)).astype(BF16)
    rank = carry_scr[:, 0:1] + _dot(oh.astype(BF16), before)
    r1 = jnp.sum(jnp.where(sel1, rank, 0.0), axis=0, keepdims=True).astype(jnp.int32)
    r2 = jnp.sum(jnp.where(sel2, rank, 0.0), axis=0, keepdims=True).astype(jnp.int32)
    slot_ref[0, 0:1, :] = i1 * cap + r1
    slot_ref[0, 1:2, :] = i2 * cap + r2
    carry_scr[...] = carry_scr[...] + jnp.sum(oh, axis=1, keepdims=True)
    cnt_ref[...] = carry_scr[...]


def _route_slots(lgT, b_r, cnt0, cap):
    B, E, L = lgT.shape
    tm = min(ROUTE_TM, L)
    return pl.pallas_call(
        functools.partial(_route_kernel, cap=cap),
        out_shape=(jax.ShapeDtypeStruct((B, 2, L), jnp.int32),
                   jax.ShapeDtypeStruct((B, 2, L), F32),
                   jax.ShapeDtypeStruct((E, LANES), F32)),
        grid=(B, L // tm),
        in_specs=[pl.BlockSpec((1, E, tm), lambda b, i: (b, 0, i)),
                  pl.BlockSpec((E, 1), lambda b, i: (0, 0)),
                  pl.BlockSpec((E, LANES), lambda b, i: (0, 0))],
        out_specs=(pl.BlockSpec((1, 2, tm), lambda b, i: (b, 0, i)),
                   pl.BlockSpec((1, 2, tm), lambda b, i: (b, 0, i)),
                   pl.BlockSpec((E, LANES), lambda b, i: (0, 0))),
        scratch_shapes=[pltpu.VMEM((E, LANES), F32)],
        compiler_params=_cp(("arbitrary", "arbitrary")),
        name="moe_route",
    )(lgT, b_r.reshape(E, 1), cnt0)


def _row_copy(src, dst, sem):
    return pltpu.make_async_copy(src, dst, sem)


def _slot_rows_kernel(start_ref, per_ref, code_ref, row_ref, *, cap):
    code = code_ref[...]
    shift = cap.bit_length() - 1
    e = lax.shift_right_logical(code, shift)
    r = code & (cap - 1)
    start = jnp.zeros_like(code)
    per = jnp.ones_like(code)
    for k in range(N_EXPERTS):
        start = jnp.where(e == k, start_ref[k], start)
        per = jnp.where(e == k, per_ref[k], per)
    q = jnp.floor((r.astype(F32) + 0.5) / per.astype(F32)).astype(jnp.int32)
    row_ref[...] = start + q * MOE_SUP + (r - q * per)


def _slot_rows(start, per, codes, cap):
    B, _, L = codes.shape
    tm = min(ROUTE_TM, L)
    spec = pl.BlockSpec((1, 2, tm), lambda b, i, st, pe: (b, 0, i))
    return pl.pallas_call(
        functools.partial(_slot_rows_kernel, cap=cap),
        out_shape=jax.ShapeDtypeStruct(codes.shape, jnp.int32),
        grid_spec=pltpu.PrefetchScalarGridSpec(
            num_scalar_prefetch=2, grid=(B, L // tm), in_specs=[spec], out_specs=spec),
        compiler_params=_cp(("parallel", "parallel")),
        name="moe_slot_rows",
    )(start, per, codes)


def _zeros_kernel(o_ref):
    o_ref[...] = jnp.zeros_like(o_ref)


def _zero_rows(n_rows, width):
    return pl.pallas_call(
        _zeros_kernel,
        out_shape=jax.ShapeDtypeStruct((n_rows, width), F32),
        grid=(n_rows // MOE_SUP,),
        out_specs=pl.BlockSpec((MOE_SUP, width), lambda i: (i, 0)),
        compiler_params=_cp(("parallel",)),
        name="moe_zero_rows",
    )()


def _dispatch_kernel(row_ref, h_ref, xs_in_ref, xs_ref, hbuf, sem):
    del xs_in_ref
    tm = h_ref.shape[1]
    t = pl.program_id(0) * pl.num_programs(1) + pl.program_id(1)
    last = pl.num_programs(0) * pl.num_programs(1) - 1
    slot = t % 2
    hbuf[slot] = h_ref[0]

    def issue(r, _):
        for k in range(2):
            row = row_ref[0, k, r]
            _row_copy(hbuf.at[slot, pl.ds(r, 1), :], xs_ref.at[pl.ds(row, 1), :], sem.at[slot]).start()
        return 0

    lax.fori_loop(0, tm, issue, 0, unroll=8)

    def drain(sl):
        for k in range(2):
            _row_copy(hbuf.at[sl], xs_ref.at[pl.ds(0, tm), :], sem.at[sl]).wait()

    @pl.when(t > 0)
    def _():
        drain(1 - slot)

    @pl.when(t == last)
    def _():
        drain(slot)


def _dispatch(rows, h, xs):
    B, L, D = h.shape
    n_rows = xs.shape[0]
    tm = min(ROW_DMA_TM, L)
    return pl.pallas_call(
        _dispatch_kernel,
        out_shape=jax.ShapeDtypeStruct((n_rows, D), F32),
        grid=(B, L // tm),
        in_specs=[pl.BlockSpec((1, 2, tm), lambda b, i: (b, 0, i), memory_space=pltpu.SMEM),
                  pl.BlockSpec((1, tm, D), lambda b, i: (b, i, 0)),
                  pl.BlockSpec(memory_space=pl.ANY)],
        out_specs=pl.BlockSpec(memory_space=pl.ANY),
        scratch_shapes=[pltpu.VMEM((2, tm, D), F32), pltpu.SemaphoreType.DMA((2,))],
        input_output_aliases={2: 0},
        compiler_params=_cp(("arbitrary", "arbitrary")),
        name="moe_dispatch",
    )(rows, h, xs)


def _moe_grp_kernel(ge_ref, gn_ref, x_ref, wg_ref, wu_ref, wd_ref, y_ref, xb_scr):
    g = pl.program_id(0)
    j = pl.program_id(1)
    nsub = gn_ref[g]
    wg = wg_ref[0].astype(BF16)
    wu = wu_ref[0].astype(BF16)
    wd = wd_ref[0].astype(BF16)
    nblk = MOE_SUP // MOE_SUB

    @pl.when(j == 0)
    def _():
        xb_scr[...] = x_ref[...].astype(BF16)
        y_ref[...] = jnp.zeros_like(y_ref)

    def block(s):
        rows = slice(s * MOE_SUB, (s + 1) * MOE_SUB)
        xb = xb_scr[rows, :]
        act = _silu(_dot(xb, wg)) * _dot(xb, wu)
        y_ref[rows, :] = y_ref[rows, :] + _dot(act.astype(BF16), wd)

    for n in range(1, nblk + 1):
        @pl.when(nsub == n)
        def _():
            for s in range(n):
                block(s)


def _moe_groups(counts, n_groups):
    nsup = (counts + MOE_SUP - 1) // MOE_SUP
    div = jnp.maximum(nsup, 1)
    per = jnp.maximum(((counts + div - 1) // div + MOE_SUB - 1) // MOE_SUB * MOE_SUB, MOE_SUB)
    ends = jnp.cumsum(nsup)
    first = ends - nsup
    total = ends[-1]
    g = jnp.arange(n_groups, dtype=jnp.int32)
    gc = jnp.minimum(g, total - 1)
    e_of = jnp.minimum(jnp.sum((gc[:, None] >= ends[None, :]).astype(jnp.int32), axis=1), N_EXPERTS - 1)
    left = jnp.minimum(counts[e_of] - (gc - first[e_of]) * per[e_of], per[e_of])
    nsub = jnp.clip((left + MOE_SUB - 1) // MOE_SUB, 0, MOE_SUP // MOE_SUB)
    gn = jnp.where(g < total, nsub, 0).astype(jnp.int32)
    return e_of, gn, (first * MOE_SUP).astype(jnp.int32), per.astype(jnp.int32)


def _moe_grouped(xs, ge, gn, w_gu, w_down):
    D = xs.shape[1]
    nj = D_FF // MOE_FF_TILE
    ng = xs.shape[0] // MOE_SUP
    jj = lambda j, gn, g: jnp.where(gn[g] > 0, j, nj - 1)
    return pl.pallas_call(
        _moe_grp_kernel,
        out_shape=jax.ShapeDtypeStruct(xs.shape, F32),
        grid_spec=pltpu.PrefetchScalarGridSpec(
            num_scalar_prefetch=2,
            grid=(ng, nj),
            in_specs=[pl.BlockSpec((MOE_SUP, D), lambda g, j, ge, gn: (g, 0)),
                      pl.BlockSpec((1, D, MOE_FF_TILE), lambda g, j, ge, gn: (ge[g], 0, jj(j, gn, g))),
                      pl.BlockSpec((1, D, MOE_FF_TILE), lambda g, j, ge, gn: (ge[g], 0, nj + jj(j, gn, g))),
                      pl.BlockSpec((1, MOE_FF_TILE, D), lambda g, j, ge, gn: (ge[g], jj(j, gn, g), 0))],
            out_specs=pl.BlockSpec((MOE_SUP, D), lambda g, j, ge, gn: (g, 0)),
            scratch_shapes=[pltpu.VMEM((MOE_SUP, D), BF16)],
        ),
        compiler_params=_cp(("arbitrary", "arbitrary")),
        name="moe_experts",
    )(ge, gn, xs, w_gu, w_gu, w_down)


def _combine_kernel(row_ref, next_ref, w_ref, x_ref, gt_ref, gfin_ref, ys_ref, o_ref, g_scr, sem, *, final):
    tm = x_ref.shape[1]
    t = pl.program_id(0) * pl.num_programs(1) + pl.program_id(1)
    last = pl.num_programs(0) * pl.num_programs(1) - 1
    slot = t % 2

    def gather(rows, sl):
        def issue(r, _):
            for k in range(2):
                row = rows[0, k, r]
                _row_copy(ys_ref.at[pl.ds(row, 1), :], g_scr.at[sl, k, pl.ds(r, 1), :], sem.at[sl]).start()
            return 0

        lax.fori_loop(0, tm, issue, 0, unroll=8)

    @pl.when(t == 0)
    def _():
        gather(row_ref, slot)

    @pl.when(t < last)
    def _():
        gather(next_ref, 1 - slot)

    for k in range(2):
        _row_copy(ys_ref.at[pl.ds(0, tm), :], g_scr.at[slot, k], sem.at[slot]).wait()
    w = w_ref[0]
    f = w[:, 0:1] * g_scr[slot, 0] + w[:, 1:2] * g_scr[slot, 1]
    o_ref[0] = _finish(x_ref[0] + gt_ref[0, 0] * f, gfin_ref, final)


def _combine(rows, wts, x, mod, gfin, ys, *, l, final):
    B, L, D = x.shape
    tm = min(ROW_DMA_TM, L)
    gt_spec = _mod_spec(mod, l, 5, tm)
    row = pl.BlockSpec((1, tm, D), lambda b, i: (b, i, 0))
    nl = L // tm

    def next_block(b, i):
        t1 = jnp.minimum(b * nl + i + 1, B * nl - 1)
        return (t1 // nl, 0, t1 % nl)

    return pl.pallas_call(
        functools.partial(_combine_kernel, final=final),
        out_shape=jax.ShapeDtypeStruct((B, L, D), F32),
        grid=(B, L // tm),
        in_specs=[pl.BlockSpec((1, 2, tm), lambda b, i: (b, 0, i), memory_space=pltpu.SMEM),
                  pl.BlockSpec((1, 2, tm), next_block, memory_space=pltpu.SMEM),
                  pl.BlockSpec((1, tm, 2), lambda b, i: (b, i, 0)),
                  row, gt_spec,
                  pl.BlockSpec((1, D), lambda b, i: (0, 0)),
                  pl.BlockSpec(memory_space=pl.ANY)],
        out_specs=row,
        scratch_shapes=[pltpu.VMEM((2, 2, tm, D), F32), pltpu.SemaphoreType.DMA((2,))],
        compiler_params=_cp(("arbitrary", "arbitrary")),
        name="moe_combine",
    )(rows, rows, wts.transpose(0, 2, 1), x, mod, gfin, ys)


def _moe_routed(groups, b_r, w_gu, w_down, gfin, *, l, final):
    D = groups[0][1].shape[-1]
    n_tok = sum(g[1].shape[0] * g[1].shape[1] for g in groups)
    cap = 1 << (n_tok - 1).bit_length()
    n_groups = 2 * n_tok // MOE_SUP + N_EXPERTS
    cnt = jnp.zeros((N_EXPERTS, LANES), F32)
    routed = []
    for _, _, _, lgT in groups:
        codes, wts, cnt = _route_slots(lgT, b_r, cnt, cap)
        routed.append((codes, wts))
    ge, gn, start, per = _moe_groups(cnt[:, 0].astype(jnp.int32), n_groups)
    rows = [_slot_rows(start, per, codes, cap) for codes, _ in routed]
    xs = _zero_rows(n_groups * MOE_SUP, D)
    for (h, _, _, _), r in zip(groups, rows):
        xs = _dispatch(r, h, xs)
    ys = _moe_grouped(xs, ge, gn, w_gu, w_down)
    return [_combine(r, wts, x, mod, gfin, ys, l=l, final=final)
            for (_, x, mod, _), (_, wts), r in zip(groups, routed, rows)]


def _pad_lanes(v):
    return jnp.pad(v.reshape(1, -1), ((0, 0), (0, LANES - v.shape[-1])))


def _mixer_layer(x, mod, states, p, s5m, l, prev_sg, *, seq):
    B, L, D = x.shape
    hi = not seq
    s5r0, s5i0, sg0, sc0 = states
    w_in, w_gates, w_ab = p['w_in_seq' if seq else 'w_in']
    if seq:
        u, qkv, z, ga, gb, ab = _proj_seq(x, p['g_mix'], mod, w_in, w_gates, w_ab, l=l, tm=min(512, L))
    else:
        u, qkv, z, ga, gb, ab = _proj(x, p['g_mix'], mod, w_in, w_gates, w_ab, l=l, tm=L)
    alog = _pad_lanes(p['gdn_a_log'][l])
    dtb = _pad_lanes(p['gdn_dt_bias'][l])
    nw = p['gdn_norm_w'][l].reshape(1, GDN_DK)
    if seq:
        yg, sfin = _s5_seq(u, s5m['be'], s5m['tp'], s5m['cpm'], s5m['pt'],
                           jnp.zeros((SLABS, B, 1, 2 * SLAB_STATE), F32), s5m['dsk'][l], l)
        sfin = sfin.reshape(SLABS, B, 2, SLAB_STATE).transpose(2, 1, 0, 3)
        sr = sfin[0].reshape(B, S5_GROUPS, S5_STATE)
        si = sfin[1].reshape(B, S5_GROUPS, S5_STATE)
        og, sg = _gdn_seq(qkv, z, ab, p['gdn_conv_w'], alog, dtb, nw,
                          jnp.zeros((B, GDN_CONV - 1, QKV_WIDTH), F32),
                          jnp.zeros((B, GDN_HEADS, GDN_DK, GDN_DK), F32), l)
        cb = qkv[:, L - (GDN_CONV - 1):, :].astype(F32)
    else:
        n = L
        s0 = jnp.concatenate([s5r0[l].reshape(n, SLABS, SLAB_STATE),
                              s5i0[l].reshape(n, SLABS, SLAB_STATE)], axis=-1).transpose(1, 0, 2)
        yg, s1 = _s5_step(u.reshape(SLABS, n, LANES), s5m['bst'], s5m['cpe'], s5m['a1'],
                          s0, s5m['d1'][l], l)
        yg = yg.reshape(SLABS, 1, n, LANES)
        s1 = s1.transpose(1, 0, 2)
        sr = s1[:, :, :SLAB_STATE].reshape(n, S5_GROUPS, S5_STATE)
        si = s1[:, :, SLAB_STATE:].reshape(n, S5_GROUPS, S5_STATE)
        og, sg = _gdn_step(qkv.reshape(n, QKV_WIDTH), z.reshape(n, GDN_WIDTH), ab.reshape(n, LANES),
                           p['gdn_conv_w'][l], alog, dtb, nw,
                           sc0[l].reshape(n, (GDN_CONV - 1) * QKV_WIDTH), sg0, l, prev_sg)
        og = og.reshape(1, n, GDN_WIDTH)
        cb = jnp.concatenate([sc0[l][:, 1:, :], qkv.reshape(n, 1, QKV_WIDTH)], axis=1)
    x, h, lgT = _merge(yg, og, ga, gb, x, mod, p['w_s5_glu'], p['w_gdn_out'], p['w_out'],
                       p['g_ffn'], p['w_router'], l=l, tm=min(512, L), hi=hi, chunked=seq,
                       h_dtype=BF16 if (seq and l % 2 == 0) else F32)
    return x, h, lgT, (sr, si, sg, cb)


def kernel(x_prompt, x_sample, c_prompt, c_sample, state_s5_re, state_s5_im, state_gdn, state_conv,
           g_mix, g_ffn, g_final, w_ada, b_ada, w_in, s5_lambda_re, s5_lambda_im, s5_log_dt,
           s5_b_re, s5_b_im, s5_c_re, s5_c_im, s5_d, w_s5_glu, gdn_conv_w, gdn_a_log, gdn_dt_bias,
           gdn_norm_w, w_gdn_out, w_out, w_ffn_gate_up, w_ffn_down, w_router, b_router,
           w_exp_gate_up, w_exp_down):
    def in_proj_parts(w):
        return w, w[:, :, 2568:], jnp.pad(w[:, :, 2560:2568], ((0, 0), (0, 0), (0, LANES - 8)))

    D_ = x_prompt.shape[-1]
    p = dict(g_mix=g_mix.reshape(DEPTH, 1, D_), g_ffn=g_ffn.reshape(DEPTH, 1, D_), w_s5_glu=w_s5_glu,
             gdn_conv_w=gdn_conv_w, gdn_a_log=gdn_a_log, gdn_dt_bias=gdn_dt_bias,
             gdn_norm_w=gdn_norm_w, w_gdn_out=w_gdn_out, w_out=w_out,
             w_router=w_router.transpose(0, 2, 1), w_in=in_proj_parts(w_in),
             w_in_seq=in_proj_parts(w_in.astype(BF16)))
    nbp, L, D = x_prompt.shape
    nbs = x_sample.shape[0]

    mod = _ada(jnp.concatenate([c_prompt, c_sample], axis=0), w_ada, b_ada)
    mod_p = mod[:, :nbp].reshape(DEPTH, nbp, 1, 6 * D)
    mod_s = mod[:, nbp:].reshape(DEPTH, 1, nbs, 6 * D)

    seg = L // S5_T // 8
    be, bst, cpe, cpm, pt, a1 = _s5_prep(s5_lambda_re, s5_lambda_im, s5_log_dt, s5_b_re, s5_b_im,
                                         s5_c_re, s5_c_im, seg)
    d1 = [s5_d[l].reshape(SLABS, 1, LANES) for l in range(DEPTH)]
    s5m = dict(be=be, bst=bst, cpe=cpe, cpm=cpm, pt=pt, a1=a1, tp=_toep(bst, cpe), d1=d1,
               dsk=[jnp.tile(d, (1, 1, S5_T)) for d in d1])

    xs_ = [x_prompt, x_sample.reshape(1, nbs, D)]
    mods = [mod_p, mod_s]
    states = [(None, None, None, None), (state_s5_re, state_s5_im, state_gdn, state_conv)]
    outs = [[], []]
    gfin = g_final.reshape(1, D)
    for l in range(DEPTH):
        final = l == DEPTH - 1
        mixed = []
        for gi, seq in enumerate((True, False)):
            prev_sg = outs[gi][0][2] if (not seq and final and DEPTH == 2) else None
            x, h, lgT, st = _mixer_layer(xs_[gi], mods[gi], states[gi], p, s5m, l, prev_sg, seq=seq)
            outs[gi].append(st)
            mixed.append((h, x, mods[gi], lgT))
        if l % 2 == 0:
            wgu, wdn = w_ffn_gate_up[l // 2], w_ffn_down[l // 2]
            xs_ = [_ffn(h, x, mod_g, wgu if gi else wgu.astype(BF16), wdn if gi else wdn.astype(BF16), gfin,
                        l=l, tm=min(1024, x.shape[1]), hi=(gi == 1), final=final)
                   for gi, (h, x, mod_g, _) in enumerate(mixed)]
        else:
            xs_ = _moe_routed(mixed, b_router[l // 2], w_exp_gate_up[l // 2], w_exp_down[l // 2], gfin,
                              l=l, final=final)
    y_p, y_s = xs_
    st_p = [jnp.stack([o[i] for o in outs[0]]) for i in range(4)]
    st_s = [outs[1][-1][2] if (i == 2 and DEPTH == 2) else jnp.stack([o[i] for o in outs[1]])
            for i in range(4)]
    return (y_p, y_s.reshape(nbs, 1, D), st_p[0], st_p[1], st_p[2], st_p[3],
            st_s[0], st_s[1], st_s[2], st_s[3])
```

```python
import functools

import jax
import jax.numpy as jnp
from jax import lax
from jax.experimental import pallas as pl
from jax.experimental.pallas import tpu as pltpu

F32 = jnp.float32
BF16 = jnp.bfloat16
HI = lax.Precision.HIGHEST

D_MODEL = 1024
DEPTH = 2
S5_WIDTH = 512
S5_GROUP = 16
S5_GROUPS = 32
S5_STATE = 64
GDN_HEADS = 4
GDN_DK = 128
GDN_WIDTH = 512
GDN_CONV = 4
QKV_WIDTH = 1536
D_FF = 3584
N_EXPERTS = 8
NORM_EPS = 1e-6
L2_EPS = 1e-6

LANES = 128
SLABS = S5_WIDTH // LANES
SLAB_STATE = (S5_GROUPS // SLABS) * S5_STATE
S5_T = 8
GDN_C = 128
VMEM_LIMIT = 56 * 1024 * 1024


def _cp(sem, vmem=VMEM_LIMIT):
    return pltpu.CompilerParams(dimension_semantics=sem, vmem_limit_bytes=vmem)


def _dot(a, b, prec=None):
    return jnp.dot(a, b, precision=prec, preferred_element_type=F32)


def _dotb(a, b):
    return jnp.dot(a.astype(BF16), b.astype(BF16), preferred_element_type=F32)


def _dot_nt(a, b, prec=None):
    return lax.dot_general(a, b, (((1,), (1,)), ((), ())), precision=prec,
                           preferred_element_type=F32)


def _dot_tn(a, b, prec=None):
    return lax.dot_general(a, b, (((0,), (0,)), ((), ())), precision=prec,
                           preferred_element_type=F32)


def _silu(x):
    return x * jax.nn.sigmoid(x)


def _ada_kernel(c_ref, w_ref, b_ref, o_ref):
    cs = _silu(c_ref[...])
    o_ref[0] = _dot(cs, w_ref[0], HI) + b_ref[0]


def _ada(c_all, w_ada, b_ada):
    n = c_all.shape[0]
    tn = 1536
    return pl.pallas_call(
        _ada_kernel,
        out_shape=jax.ShapeDtypeStruct((DEPTH, n, 6 * D_MODEL), F32),
        grid=(DEPTH, 6 * D_MODEL // tn),
        in_specs=[pl.BlockSpec((n, D_MODEL), lambda l, j: (0, 0)),
                  pl.BlockSpec((1, D_MODEL, tn), lambda l, j: (l, 0, j)),
                  pl.BlockSpec((1, 1, tn), lambda l, j: (l, 0, j))],
        out_specs=pl.BlockSpec((1, n, tn), lambda l, j: (l, 0, j)),
        compiler_params=_cp(("parallel", "parallel")),
        name="ada_mod",
    )(c_all, w_ada, b_ada.reshape(DEPTH, 1, 6 * D_MODEL))


def _proj_kernel(x_ref, g_ref, sc_ref, sh_ref, w_ref, wg_ref, wab_ref,
                 u_ref, qkv_ref, z_ref, ga_ref, gb_ref, ab_ref, h_scr):
    j = pl.program_id(2)

    @pl.when(j == 0)
    def _():
        x = x_ref[0]
        ms = jnp.mean(x * x, axis=-1, keepdims=True)
        xn = x * lax.rsqrt(ms + NORM_EPS) * g_ref[0]
        h_scr[...] = (xn * (1.0 + sc_ref[0, 0]) + sh_ref[0, 0]).astype(h_scr.dtype)

    def mm(w):
        return _dot(h_scr[...], w, HI)

    @pl.when(j == 0)
    def _():
        res = mm(w_ref[0])
        for k in range(SLABS):
            u_ref[k, 0] = res[:, k * LANES:(k + 1) * LANES]

    @pl.when((j >= 1) & (j <= 3))
    def _():
        qkv_ref[0] = mm(w_ref[0])

    @pl.when(j == 4)
    def _():
        z_ref[0] = mm(w_ref[0])

    @pl.when((j == 5) | (j == 6))
    def _():
        ga_ref[0] = jax.nn.sigmoid(mm(wg_ref[0]))

    @pl.when((j == 7) | (j == 8))
    def _():
        gb_ref[0] = jax.nn.sigmoid(mm(wg_ref[0]))

    @pl.when(j == 9)
    def _():
        ab_ref[0] = mm(wab_ref[0])


def _mod_spec(mod, l, chunk, tm):
    per_row = mod.shape[2] != 1
    D = mod.shape[3] // 6

    def index(b, i, *_):
        return (l, b, i if per_row else 0, chunk)

    return pl.BlockSpec((1, 1, tm if per_row else 1, D), index)


def _proj(x, g, mod, w_in, w_gates, w_ab, *, l, tm):
    B, L, D = x.shape
    tn = 512
    clampi = lambda j, lo, n: jnp.clip(j - lo, 0, n - 1)
    outs = pl.pallas_call(
        _proj_kernel,
        out_shape=(jax.ShapeDtypeStruct((SLABS, B, L, LANES), F32),
                   jax.ShapeDtypeStruct((B, L, QKV_WIDTH), F32),
                   jax.ShapeDtypeStruct((B, L, GDN_WIDTH), F32),
                   jax.ShapeDtypeStruct((B, L, D), F32),
                   jax.ShapeDtypeStruct((B, L, D), F32),
                   jax.ShapeDtypeStruct((B, L, LANES), F32)),
        grid=(B, L // tm, 10),
        in_specs=[pl.BlockSpec((1, tm, D), lambda b, i, j: (b, i, 0)),
                  pl.BlockSpec((1, 1, D), lambda b, i, j: (l, 0, 0)),
                  _mod_spec(mod, l, 1, tm),
                  _mod_spec(mod, l, 0, tm),
                  pl.BlockSpec((1, D, tn), lambda b, i, j: (l, 0, jnp.minimum(j, 4))),
                  pl.BlockSpec((1, D, tn), lambda b, i, j: (l, 0, clampi(j, 5, 4))),
                  pl.BlockSpec((1, D, LANES), lambda b, i, j: (l, 0, 0))],
        out_specs=(pl.BlockSpec((SLABS, 1, tm, LANES), lambda b, i, j: (0, b, i, 0)),
                   pl.BlockSpec((1, tm, tn), lambda b, i, j: (b, i, clampi(j, 1, 3))),
                   pl.BlockSpec((1, tm, tn), lambda b, i, j: (b, i, 0)),
                   pl.BlockSpec((1, tm, tn), lambda b, i, j: (b, i, clampi(j, 5, 2))),
                   pl.BlockSpec((1, tm, tn), lambda b, i, j: (b, i, clampi(j, 7, 2))),
                   pl.BlockSpec((1, tm, LANES), lambda b, i, j: (b, i, 0))),
        scratch_shapes=[pltpu.VMEM((tm, D), F32)],
        compiler_params=_cp(("parallel", "parallel", "arbitrary")),
        name="norm_in_proj",
    )(x, g, mod, mod, w_in, w_gates, w_ab)
    return outs


def _proj_seq_kernel(x_ref, g_ref, sc_ref, sh_ref, w_ref, wg_ref, wab_ref,
                     u_ref, qkv_ref, z_ref, ga_ref, gb_ref, ab_ref, us_scr):
    x = x_ref[0]
    ms = jnp.mean(x * x, axis=-1, keepdims=True)
    xn = x * lax.rsqrt(ms + NORM_EPS) * g_ref[0]
    h = (xn * (1.0 + sc_ref[0, 0]) + sh_ref[0, 0]).astype(BF16)
    res = _dot(h, w_ref[0, :, 0:S5_WIDTH])
    nrow = res.shape[0] // S5_T
    for k in range(SLABS):
        us_scr[...] = res[:, k * LANES:(k + 1) * LANES]
        for t in range(S5_T):
            u_ref[k, 0, :, t * LANES:(t + 1) * LANES] = (
                us_scr[pl.ds(t, nrow, stride=S5_T), :].astype(u_ref.dtype))
    c0 = S5_WIDTH
    qkv_ref[0] = _dot(h, w_ref[0, :, c0:c0 + QKV_WIDTH]).astype(qkv_ref.dtype)
    c0 += QKV_WIDTH
    z_ref[0] = _dot(h, w_ref[0, :, c0:c0 + GDN_WIDTH]).astype(z_ref.dtype)
    D = x.shape[-1]
    ga_ref[0] = jax.nn.sigmoid(_dot(h, wg_ref[0, :, 0:D])).astype(ga_ref.dtype)
    gb_ref[0] = jax.nn.sigmoid(_dot(h, wg_ref[0, :, D:2 * D])).astype(gb_ref.dtype)
    ab_ref[0] = _dot(h, wab_ref[0])


def _proj_seq(x, g, mod, w_in, w_gates, w_ab, *, l, tm):
    B, L, D = x.shape
    n_main = S5_WIDTH + QKV_WIDTH + GDN_WIDTH
    row = lambda w: pl.BlockSpec((1, tm, w), lambda b, i: (b, i, 0))
    return pl.pallas_call(
        _proj_seq_kernel,
        out_shape=(jax.ShapeDtypeStruct((SLABS, B, L // S5_T, S5_T * LANES), BF16),
                   jax.ShapeDtypeStruct((B, L, QKV_WIDTH), BF16),
                   jax.ShapeDtypeStruct((B, L, GDN_WIDTH), BF16),
                   jax.ShapeDtypeStruct((B, L, D), BF16),
                   jax.ShapeDtypeStruct((B, L, D), BF16),
                   jax.ShapeDtypeStruct((B, L, LANES), F32)),
        grid=(B, L // tm),
        in_specs=[row(D),
                  pl.BlockSpec((1, 1, D), lambda b, i: (l, 0, 0)),
                  _mod_spec(mod, l, 1, tm),
                  _mod_spec(mod, l, 0, tm),
                  pl.BlockSpec((1, D, n_main), lambda b, i: (l, 0, 0)),
                  pl.BlockSpec((1, D, 2 * D), lambda b, i: (l, 0, 0)),
                  pl.BlockSpec((1, D, LANES), lambda b, i: (l, 0, 0))],
        out_specs=(pl.BlockSpec((SLABS, 1, tm // S5_T, S5_T * LANES), lambda b, i: (0, b, i, 0)),
                   row(QKV_WIDTH), row(GDN_WIDTH), row(D), row(D), row(LANES)),
        scratch_shapes=[pltpu.VMEM((tm, LANES), F32)],
        compiler_params=_cp(("parallel", "parallel")),
        name="norm_in_proj_seq",
    )(x, g, mod, mod, w_in, w_gates, w_ab)


GROUPS_PER_SLAB = S5_GROUPS // SLABS


def _s5_prep_kernel(lrb, lib, dtb, bre, bim, lrc, lic, dtc, cre, cim, lrn, lin, dtn,
                    be_ref, bst_ref, cpe_ref, cpm_ref, pt_ref, a1_ref, *, seg):
    W = SLAB_STATE

    def disc(lr, li, ldt):
        dt = jnp.exp(ldt)
        mag = jnp.exp(lr * dt)
        return mag * jnp.cos(li * dt), mag * jnp.sin(li * dt)

    def cmul(xr, xi, yr, yi):
        return xr * yr - xi * yi, xr * yi + xi * yr

    lr, li = lrb[0], lib[0]
    ar, ai = disc(lr, li, dtb[0])
    den = lr * lr + li * li
    nr = ar - 1.0
    kr = (nr * lr + ai * li) / den
    ki = (ai * lr - nr * li) / den
    br, bi = bre[0], bim[0]
    bbr = kr * br - ki * bi
    bbi = kr * bi + ki * br
    rgrp = lax.broadcasted_iota(jnp.int32, (LANES, LANES), 0) // S5_GROUP
    lane_hi = lax.broadcasted_iota(jnp.int32, (LANES, LANES), 1) // S5_STATE
    pr, pi = jnp.ones_like(ar), jnp.zeros_like(ar)
    for d in range(S5_T):
        t = S5_T - 1 - d
        for ri, val in enumerate(cmul(pr, pi, bbr, bbi)):
            two = jnp.concatenate([val, val], axis=1)
            for m in range(GROUPS_PER_SLAB // 2):
                tile = jnp.where(rgrp == 2 * m + lane_hi, two, 0.0)
                c0 = ri * W + m * LANES
                be_ref[0, 0, t * LANES:(t + 1) * LANES, c0:c0 + LANES] = tile.astype(BF16)
                if d == 0:
                    bst_ref[0, 0, :, c0:c0 + LANES] = tile
        pr, pi = cmul(pr, pi, ar, ai)

    ar, ai = disc(lrc[0], lic[0], dtc[0])
    cr, ci = cre[0], cim[0]
    own = (lax.broadcasted_iota(jnp.int32, (W, LANES), 0) // S5_STATE
           == lax.broadcasted_iota(jnp.int32, (W, LANES), 1) // S5_GROUP)
    pr, pi = jnp.ones_like(ar), jnp.zeros_like(ar)
    for d in range(S5_T + 1):
        vr, vi = cmul(cr, ci, pr, pi)
        for ri, val in enumerate((vr, -vi)):
            tile = jnp.where(own, val, 0.0)
            cpe_ref[0, 0, d, ri * W:(ri + 1) * W, :] = tile
            if d >= 1:
                cpm_ref[0, 0, ri * W:(ri + 1) * W, (d - 1) * LANES:d * LANES] = tile.astype(BF16)
        pr, pi = cmul(pr, pi, ar, ai)

    ar, ai = disc(lrn[0, 0], lin[0, 0], dtn[0, 0])
    a1_ref[0, 0, :, 0:W] = ar
    a1_ref[0, 0, :, W:2 * W] = ai
    tr, ti = ar, ai
    for _ in range(S5_T - 1):
        tr, ti = cmul(tr, ti, ar, ai)
    pr, pi = jnp.ones_like(ar), jnp.zeros_like(ar)
    for i in range(seg + 1):
        pt_ref[0, 0, i:i + 1, 0:W] = pr
        pt_ref[0, 0, i:i + 1, W:2 * W] = pi
        pr, pi = cmul(pr, pi, tr, ti)


def _s5_prep(lam_re, lam_im, log_dt, b_re, b_im, c_re, c_im, seg):
    G, P, C = S5_GROUPS, S5_STATE, S5_GROUP
    W2 = 2 * SLAB_STATE
    dt3 = jnp.broadcast_to(log_dt[:, :, None], (DEPTH, G, P))
    rows_b = lambda a: jnp.repeat(a, C, axis=1)
    bt = lambda a: a.transpose(0, 1, 3, 2).reshape(DEPTH, G * C, P)
    rows_c = lambda a: jnp.broadcast_to(a.reshape(DEPTH, G * P, 1), (DEPTH, G * P, LANES))
    ct = lambda a: jnp.tile(a.transpose(0, 1, 3, 2).reshape(DEPTH, G * P, C), (1, 1, LANES // C))
    nat = lambda a: a.reshape(DEPTH, SLABS, 1, SLAB_STATE)
    args = (rows_b(lam_re), rows_b(lam_im), rows_b(dt3), bt(b_re), bt(b_im),
            rows_c(lam_re), rows_c(lam_im), rows_c(dt3), ct(c_re), ct(c_im),
            nat(lam_re), nat(lam_im), nat(dt3))
    bspec = pl.BlockSpec((1, LANES, P), lambda l, k: (l, k, 0))
    cspec = pl.BlockSpec((1, SLAB_STATE, LANES), lambda l, k: (l, k, 0))
    nspec = pl.BlockSpec((1, 1, 1, SLAB_STATE), lambda l, k: (l, k, 0, 0))
    return pl.pallas_call(
        functools.partial(_s5_prep_kernel, seg=seg),
        out_shape=(jax.ShapeDtypeStruct((DEPTH, SLABS, S5_T * LANES, W2), BF16),
                   jax.ShapeDtypeStruct((DEPTH, SLABS, LANES, W2), F32),
                   jax.ShapeDtypeStruct((DEPTH, SLABS, S5_T + 1, W2, LANES), F32),
                   jax.ShapeDtypeStruct((DEPTH, SLABS, W2, S5_T * LANES), BF16),
                   jax.ShapeDtypeStruct((DEPTH, SLABS, seg + 1, W2), F32),
                   jax.ShapeDtypeStruct((DEPTH, SLABS, 1, W2), F32)),
        grid=(DEPTH, SLABS),
        in_specs=[bspec] * 5 + [cspec] * 5 + [nspec] * 3,
        out_specs=(pl.BlockSpec((1, 1, S5_T * LANES, W2), lambda l, k: (l, k, 0, 0)),
                   pl.BlockSpec((1, 1, LANES, W2), lambda l, k: (l, k, 0, 0)),
                   pl.BlockSpec((1, 1, S5_T + 1, W2, LANES), lambda l, k: (l, k, 0, 0, 0)),
                   pl.BlockSpec((1, 1, W2, S5_T * LANES), lambda l, k: (l, k, 0, 0)),
                   pl.BlockSpec((1, 1, seg + 1, W2), lambda l, k: (l, k, 0, 0)),
                   pl.BlockSpec((1, 1, 1, W2), lambda l, k: (l, k, 0, 0))),
        compiler_params=_cp(("parallel", "parallel")),
        name="s5_discretize",
    )(*args)


def _toep_kernel(b_ref, c_ref, o_ref):
    dd = pl.program_id(2)
    bst = b_ref[0, 0]
    lag = lambda d: _dot(bst, c_ref[0, 0, d], HI)
    k0 = lag(2 * dd)
    o_ref[0, 0, 0, 0:LANES, 0:LANES] = k0.astype(BF16)
    o_ref[0, 0, 0, LANES:, LANES:] = k0.astype(BF16)
    o_ref[0, 0, 0, 0:LANES, LANES:] = lag(2 * dd + 1).astype(BF16)
    km = lag(jnp.maximum(2 * dd - 1, 0))
    o_ref[0, 0, 0, LANES:, 0:LANES] = jnp.where(dd > 0, km, 0.0).astype(BF16)


def _toep(bst, cpe):
    W2 = 2 * SLAB_STATE
    return pl.pallas_call(
        _toep_kernel,
        out_shape=jax.ShapeDtypeStruct((DEPTH, SLABS, S5_T // 2, 2 * LANES, 2 * LANES), BF16),
        grid=(DEPTH, SLABS, S5_T // 2),
        in_specs=[pl.BlockSpec((1, 1, LANES, W2), lambda l, k, d: (l, k, 0, 0)),
                  pl.BlockSpec((1, 1, S5_T + 1, W2, LANES), lambda l, k, d: (l, k, 0, 0, 0))],
        out_specs=pl.BlockSpec((1, 1, 1, 2 * LANES, 2 * LANES), lambda l, k, d: (l, k, d, 0, 0)),
        compiler_params=_cp(("parallel", "parallel", "parallel")),
        name="s5_conv_blocks",
    )(bst, cpe)


def _s5_seq_kernel(up_ref, be_ref, tp_ref, cpm_ref, pt_ref, s0_ref, dsk_ref,
                   yg_ref, sfin_ref, e_scr, sx_scr, *, nc):
    seg = nc // 8
    W = SLAB_STATE
    nt = W // LANES
    ub = up_ref[0, 0]
    u = ub.astype(F32)
    e = _dot(ub, be_ref[0, 0])
    for c in range(2 * nt):
        e_scr[c] = e[:, c * LANES:(c + 1) * LANES]

    def tiles(row):
        return [(row[:, c * LANES:(c + 1) * LANES], row[:, W + c * LANES:W + (c + 1) * LANES])
                for c in range(nt)]

    a8 = [(jnp.broadcast_to(r, (8, LANES)), jnp.broadcast_to(i, (8, LANES)))
          for r, i in tiles(pt_ref[0, 0, 1:2, :])]

    def step(i, carry):
        rows = pl.ds(i, 8, stride=seg)
        new = []
        for c in range(nt):
            sr, si = carry[c]
            ar, ai = a8[c]
            sx_scr[c, rows, :] = sr
            sx_scr[nt + c, rows, :] = si
            new.append((ar * sr - ai * si + e_scr[c, rows, :],
                        ar * si + ai * sr + e_scr[nt + c, rows, :]))
        return tuple(new)

    zero = jnp.zeros((8, LANES), F32)
    ends = lax.fori_loop(0, seg, step, tuple((zero, zero) for _ in range(nt)))

    al = tiles(pt_ref[0, 0, seg:seg + 1, :])
    cur = tiles(s0_ref[0, 0])
    car = []
    for c in range(nt):
        alr, ali = al[c]
        cr, ci = cur[c]
        sr, si = ends[c]
        crs, cis = [], []
        for j in range(8):
            crs.append(cr)
            cis.append(ci)
            cr, ci = (alr * cr - ali * ci + sr[j:j + 1], alr * ci + ali * cr + si[j:j + 1])
        sfin_ref[0, 0, :, c * LANES:(c + 1) * LANES] = cr
        sfin_ref[0, 0, :, W + c * LANES:W + (c + 1) * LANES] = ci
        car.append((jnp.concatenate(crs, axis=0), jnp.concatenate(cis, axis=0)))

    def corr(i, _):
        rows = pl.ds(i, 8, stride=seg)
        pw = tiles(pt_ref[0, 0, pl.ds(i, 1), :])
        for c in range(nt):
            pr, pi = pw[c]
            cr, ci = car[c]
            sx_scr[c, rows, :] = sx_scr[c, rows, :] + (pr * cr - pi * ci)
            sx_scr[nt + c, rows, :] = sx_scr[nt + c, rows, :] + (pr * ci + pi * cr)
        return 0

    lax.fori_loop(0, seg, corr, 0)

    sx = jnp.concatenate([sx_scr[c] for c in range(2 * nt)], axis=-1)
    y = _dot(sx.astype(BF16), cpm_ref[0, 0])
    TW = 2 * LANES
    for tq in range(S5_T // 2):
        acc = y[:, tq * TW:(tq + 1) * TW]
        for tpi in range(tq + 1):
            acc = acc + _dot(ub[:, tpi * TW:(tpi + 1) * TW], tp_ref[0, 0, tq - tpi])
        acc = acc + dsk_ref[0, :, tq * TW:(tq + 1) * TW] * u[:, tq * TW:(tq + 1) * TW]
        yg_ref[0, 0, :, tq * TW:(tq + 1) * TW] = jax.nn.gelu(acc).astype(yg_ref.dtype)


def _s5_seq(up, be_emb, tp, cpm, pt, s0, dsk, l):
    _, B, nc, _ = up.shape
    seg = nc // 8
    W2 = 2 * SLAB_STATE
    yg, sfin = pl.pallas_call(
        functools.partial(_s5_seq_kernel, nc=nc),
        out_shape=(jax.ShapeDtypeStruct((SLABS, B, nc, S5_T * LANES), BF16),
                   jax.ShapeDtypeStruct((SLABS, B, 1, W2), F32)),
        grid=(SLABS, B),
        in_specs=[pl.BlockSpec((1, 1, nc, S5_T * LANES), lambda k, b: (k, b, 0, 0)),
                  pl.BlockSpec((1, 1, S5_T * LANES, W2), lambda k, b: (l, k, 0, 0)),
                  pl.BlockSpec((1, 1, S5_T // 2, 2 * LANES, 2 * LANES), lambda k, b: (l, k, 0, 0, 0)),
                  pl.BlockSpec((1, 1, W2, S5_T * LANES), lambda k, b: (l, k, 0, 0)),
                  pl.BlockSpec((1, 1, seg + 1, W2), lambda k, b: (l, k, 0, 0)),
                  pl.BlockSpec((1, 1, 1, W2), lambda k, b: (k, b, 0, 0)),
                  pl.BlockSpec((1, 1, S5_T * LANES), lambda k, b: (k, 0, 0))],
        out_specs=(pl.BlockSpec((1, 1, nc, S5_T * LANES), lambda k, b: (k, b, 0, 0)),
                   pl.BlockSpec((1, 1, 1, W2), lambda k, b: (k, b, 0, 0))),
        scratch_shapes=[pltpu.VMEM((W2 // LANES, nc, LANES), F32),
                        pltpu.VMEM((W2 // LANES, nc, LANES), F32)],
        compiler_params=_cp(("parallel", "parallel")),
        name="s5_seq",
    )(up, be_emb, tp, cpm, pt, s0, dsk)
    return yg, sfin


def _s5_step_kernel(u_ref, b_ref, c_ref, a_ref, s0_ref, d_ref, yg_ref, s1_ref):
    W = SLAB_STATE
    u = u_ref[0]
    bu = _dot(u, b_ref[0, 0], HI)
    ar = a_ref[0, 0, :, 0:W]
    ai = a_ref[0, 0, :, W:2 * W]
    sr = s0_ref[0, :, 0:W]
    si = s0_ref[0, :, W:2 * W]
    nr = ar * sr - ai * si + bu[:, 0:W]
    ni = ar * si + ai * sr + bu[:, W:2 * W]
    s1_ref[0, :, 0:W] = nr
    s1_ref[0, :, W:2 * W] = ni
    s1 = jnp.concatenate([nr, ni], axis=-1)
    y = _dot(s1, c_ref[0, 0, 0], HI) + d_ref[0] * u
    yg_ref[0] = jax.nn.gelu(y)


def _s5_step(u_slab, bst, cpe, a1, s0, d1, l):
    _, N, _ = u_slab.shape
    W2 = 2 * SLAB_STATE
    return pl.pallas_call(
        _s5_step_kernel,
        out_shape=(jax.ShapeDtypeStruct((SLABS, N, LANES), F32),
                   jax.ShapeDtypeStruct((SLABS, N, W2), F32)),
        grid=(SLABS,),
        in_specs=[pl.BlockSpec((1, N, LANES), lambda k: (k, 0, 0)),
                  pl.BlockSpec((1, 1, LANES, W2), lambda k: (l, k, 0, 0)),
                  pl.BlockSpec((1, 1, 1, W2, LANES), lambda k: (l, k, 0, 0, 0)),
                  pl.BlockSpec((1, 1, 1, W2), lambda k: (l, k, 0, 0)),
                  pl.BlockSpec((1, N, W2), lambda k: (k, 0, 0)),
                  pl.BlockSpec((1, 1, LANES), lambda k: (k, 0, 0))],
        out_specs=(pl.BlockSpec((1, N, LANES), lambda k: (k, 0, 0)),
                   pl.BlockSpec((1, N, W2), lambda k: (k, 0, 0))),
        compiler_params=_cp(("parallel",)),
        name="s5_step",
    )(u_slab, bst, cpe, a1, s0, d1)


def _l2n(x):
    return x * lax.rsqrt(jnp.sum(x * x, axis=-1, keepdims=True) + L2_EPS)


def _split_bf16(x):
    hi = x.astype(BF16)
    return hi, (x - hi.astype(F32)).astype(BF16)


def _unit_lower_solve(As, rhss):
    n = GDN_C
    row = lax.broadcasted_iota(jnp.int32, (n, n), 0)
    col = lax.broadcasted_iota(jnp.int32, (n, n), 1)
    eye = (row == col).astype(F32)
    same8 = (row // 8) == (col // 8)
    Qs = [jnp.where(same8, -A, 0.0) for A in As]
    invs = [eye + Q for Q in Qs]
    for _ in range(2):
        Qs = [_dotb(Q, Q) for Q in Qs]
        invs = [inv + _dotb(inv, Q) for inv, Q in zip(invs, Qs)]
    s = 8
    while s < n:
        sib = ((row // (2 * s)) == (col // (2 * s))) & ((row // s) != (col // s))
        offs = [jnp.where(sib, A, 0.0).astype(BF16) for A in As]
        invb = [inv.astype(BF16) for inv in invs]
        tmp = [_dot(off, ib) for off, ib in zip(offs, invb)]
        invs = [inv - _dot(ib, t.astype(BF16)) for inv, ib, t in zip(invs, invb, tmp)]
        s *= 2
    invb = [inv.astype(BF16) for inv in invs]
    x0s = [_dot(ib, rhs.astype(BF16)) for ib, rhs in zip(invb, rhss)]
    res = []
    for A, x0, rhs in zip(As, x0s, rhss):
        ah, al = _split_bf16(A)
        xh, xl = _split_bf16(x0)
        res.append(rhs - x0 - (_dot(ah, xh) + _dot(ah, xl) + _dot(al, xh)))
    return [x0 + _dot(ib, r.astype(BF16)) for x0, ib, r in zip(x0s, invb, res)]


def _gdn_tile(qc_scr, gc, beta, z_ref, nw, o_ref, s_scr, tl):
    C, DK, H = GDN_C, GDN_DK, GDN_HEADS
    nchunk = tl // C
    probs = [(c, h) for c in range(nchunk) for h in range(H)]
    row = lax.broadcasted_iota(jnp.int32, (C, C), 0)
    col = lax.broadcasted_iota(jnp.int32, (C, C), 1)
    tri = row >= col
    strict = row > col

    def blk(c, off):
        return qc_scr[c * C:(c + 1) * C, off:off + DK]

    q = [_l2n(blk(c, h * DK)) * (DK ** -0.5) for c, h in probs]
    k = [_l2n(blk(c, GDN_WIDTH + h * DK)) for c, h in probs]
    v = [blk(c, 2 * GDN_WIDTH + h * DK) for c, h in probs]
    gcb = [jnp.broadcast_to(gc[c * C:(c + 1) * C, h:h + 1], (C, DK)) for c, h in probs]
    bb = [jnp.broadcast_to(beta[c * C:(c + 1) * C, H + h:H + h + 1], (C, DK)) for c, h in probs]
    decay = []
    for g in gcb:
        diff = g - g.T
        decay.append(jnp.where(tri, jnp.exp(jnp.where(tri, diff, 0.0)), 0.0))
    kbf = [x.astype(BF16) for x in k]
    kb = [x * b for x, b in zip(k, bb)]
    A = [jnp.where(strict, _dot_nt(x.astype(BF16), y) * d, 0.0) for x, y, d in zip(kb, kbf, decay)]
    egc = [jnp.exp(g) for g in gcb]
    rhs = [jnp.concatenate([x * b, y * e], axis=-1) for x, b, y, e in zip(v, bb, kb, egc)]
    sol = _unit_lower_solve(A, rhs)
    attn = [jnp.where(tri, _dot_nt(x.astype(BF16), y) * d, 0.0).astype(BF16)
            for x, y, d in zip(q, kbf, decay)]
    glast = [g[C - 1:C, :] for g in gcb]
    wq = [jnp.concatenate([s[:, DK:], x * e], axis=0).astype(BF16) for s, x, e in zip(sol, q, egc)]
    kg = [(x * jnp.exp(gl - g)).astype(BF16) for x, gl, g in zip(k, glast, gcb)]

    for c in range(nchunk):
        ps = [c * H + h for h in range(H)]
        S = [s_scr[h] for h in range(H)]
        ws = [_dot(wq[p], S[h].astype(BF16)) for h, p in enumerate(ps)]
        v_new = [sol[p][:, 0:DK] - w[0:C] for p, w in zip(ps, ws)]
        vb = [x.astype(BF16) for x in v_new]
        o = [w[C:] + _dot(attn[p], x) for p, w, x in zip(ps, ws, vb)]
        for h, p in enumerate(ps):
            s_scr[h] = S[h] * jnp.exp(glast[p]) + _dot_tn(kg[p], vb[h])
            zh = z_ref[0, c * C:(c + 1) * C, h * DK:(h + 1) * DK].astype(F32)
            on = o[h] * lax.rsqrt(jnp.mean(o[h] * o[h], axis=-1, keepdims=True) + NORM_EPS) * nw
            o_ref[0, c * C:(c + 1) * C, h * DK:(h + 1) * DK] = (on * _silu(zh)).astype(o_ref.dtype)


def _gdn_seq_kernel(qkv_ref, z_ref, ab_ref, cw_ref, alog_ref, dtb_ref, nw_ref, conv0_ref, s0_ref,
                    o_ref, sfin_ref, xp_scr, qc_scr, s_scr, *, tl):
    lt = pl.program_id(1)

    @pl.when(lt == 0)
    def _():
        xp_scr[0:8, :] = jnp.zeros((8, QKV_WIDTH), F32)
        xp_scr[8 - (GDN_CONV - 1):8, :] = conv0_ref[0]
        s_scr[...] = s0_ref[0]

    xp_scr[8:8 + tl, :] = qkv_ref[0].astype(F32)
    conv = cw_ref[0, 0:1, :] * xp_scr[5:5 + tl, :]
    for j in range(1, GDN_CONV):
        conv = conv + cw_ref[0, j:j + 1, :] * xp_scr[5 + j:5 + j + tl, :]
    xp_scr[0:8, :] = xp_scr[tl:tl + 8, :]
    qc_scr[...] = _silu(conv)

    ab = ab_ref[0]
    g = -jnp.exp(alog_ref[...]) * jax.nn.softplus(ab + dtb_ref[...])
    beta = jax.nn.sigmoid(ab)
    row = lax.broadcasted_iota(jnp.int32, (tl, tl), 0)
    col = lax.broadcasted_iota(jnp.int32, (tl, tl), 1)
    csum = ((row >= col) & ((row // GDN_C) == (col // GDN_C))).astype(F32)
    gc = _dot(csum, g, HI)
    _gdn_tile(qc_scr, gc, beta, z_ref, nw_ref[...], o_ref, s_scr, tl)

    @pl.when(lt == pl.num_programs(1) - 1)
    def _():
        sfin_ref[0] = s_scr[...]


def _gdn_seq(qkv, z, ab, conv_w, alog, dtb, nw, conv0, s0, l):
    B, L, _ = qkv.shape
    tl = min(256, L)
    return pl.pallas_call(
        functools.partial(_gdn_seq_kernel, tl=tl),
        out_shape=(jax.ShapeDtypeStruct((B, L, GDN_WIDTH), BF16),
                   jax.ShapeDtypeStruct((B, GDN_HEADS, GDN_DK, GDN_DK), F32)),
        grid=(B, L // tl),
        in_specs=[pl.BlockSpec((1, tl, QKV_WIDTH), lambda b, i: (b, i, 0)),
                  pl.BlockSpec((1, tl, GDN_WIDTH), lambda b, i: (b, i, 0)),
                  pl.BlockSpec((1, tl, LANES), lambda b, i: (b, i, 0)),
                  pl.BlockSpec((1, GDN_CONV, QKV_WIDTH), lambda b, i: (l, 0, 0)),
                  pl.BlockSpec((1, LANES), lambda b, i: (0, 0)),
                  pl.BlockSpec((1, LANES), lambda b, i: (0, 0)),
                  pl.BlockSpec((1, GDN_DK), lambda b, i: (0, 0)),
                  pl.BlockSpec((1, GDN_CONV - 1, QKV_WIDTH), lambda b, i: (b, 0, 0)),
                  pl.BlockSpec((1, GDN_HEADS, GDN_DK, GDN_DK), lambda b, i: (b, 0, 0, 0))],
        out_specs=(pl.BlockSpec((1, tl, GDN_WIDTH), lambda b, i: (b, i, 0)),
                   pl.BlockSpec((1, GDN_HEADS, GDN_DK, GDN_DK), lambda b, i: (b, 0, 0, 0))),
        scratch_shapes=[pltpu.VMEM((tl + 8, QKV_WIDTH), F32),
                        pltpu.VMEM((tl, QKV_WIDTH), F32),
                        pltpu.VMEM((GDN_HEADS, GDN_DK, GDN_DK), F32)],
        compiler_params=_cp(("parallel", "arbitrary")),
        name="gdn_seq",
    )(qkv, z, ab, conv_w, alog, dtb, nw, conv0, s0)


GDN_STEP_ROWS = 8


def _gdn_step_kernel(qkv_ref, z_ref, ab_ref, cw_ref, alog_ref, dtb_ref, nw_ref, conv0_ref, s0_ref,
                     *rest):
    if len(rest) == 3:
        prev_ref, o_ref, s1_all = rest
        s1_all[0] = prev_ref[...]
        s1_ref = s1_all.at[1]
    else:
        o_ref, s1_all = rest
        s1_ref = s1_all
    nb = GDN_STEP_ROWS
    W = QKV_WIDTH
    conv = cw_ref[0:1, :] * conv0_ref[:, 0:W]
    conv = conv + cw_ref[1:2, :] * conv0_ref[:, W:2 * W]
    conv = conv + cw_ref[2:3, :] * conv0_ref[:, 2 * W:3 * W]
    conv = conv + cw_ref[3:4, :] * qkv_ref[...]
    qc = _silu(conv)
    ab = ab_ref[...]
    eg = jnp.exp(-jnp.exp(alog_ref[...]) * jax.nn.softplus(ab + dtb_ref[...]))
    beta = jax.nn.sigmoid(ab)
    eye = (lax.broadcasted_iota(jnp.int32, (GDN_DK, GDN_DK), 0)
           == lax.broadcasted_iota(jnp.int32, (GDN_DK, GDN_DK), 1)).astype(F32)
    for h in range(GDN_HEADS):
        q = _l2n(qc[:, h * GDN_DK:(h + 1) * GDN_DK]) * (GDN_DK ** -0.5)
        k = _l2n(qc[:, GDN_WIDTH + h * GDN_DK:GDN_WIDTH + (h + 1) * GDN_DK])
        v = qc[:, 2 * GDN_WIDTH + h * GDN_DK:2 * GDN_WIDTH + (h + 1) * GDN_DK]
        kT = _dot_nt(eye, k, HI)
        qT = _dot_nt(eye, q, HI)
        qk = jnp.sum(q * k, axis=-1, keepdims=True)
        for j in range(nb):
            S = s0_ref[0, j, h]
            kc = jnp.broadcast_to(kT[:, j:j + 1], (GDN_DK, GDN_DK))
            qcb = jnp.broadcast_to(qT[:, j:j + 1], (GDN_DK, GDN_DK))
            kS = jnp.sum(kc * S, axis=0, keepdims=True)
            qS = jnp.sum(qcb * S, axis=0, keepdims=True)
            egj = eg[j:j + 1, h:h + 1]
            bj = beta[j:j + 1, GDN_HEADS + h:GDN_HEADS + h + 1]
            v_new = bj * v[j:j + 1, :] - (bj * egj) * kS
            o = egj * qS + qk[j:j + 1, :] * v_new
            s1_ref[j, h] = S * egj + kc * v_new
            zh = z_ref[j:j + 1, h * GDN_DK:(h + 1) * GDN_DK]
            on = o * lax.rsqrt(jnp.mean(o * o, axis=-1, keepdims=True) + NORM_EPS) * nw_ref[...]
            o_ref[j:j + 1, h * GDN_DK:(h + 1) * GDN_DK] = on * _silu(zh)


def _gdn_step(qkv, z, ab, conv_w, alog, dtb, nw, conv0, s_all, l, prev):
    N = qkv.shape[0]
    nb = GDN_STEP_ROWS
    row = lambda w: pl.BlockSpec((nb, w), lambda i: (i, 0))
    const = lambda r, w: pl.BlockSpec((r, w), lambda i: (0, 0))
    sblk = (nb, GDN_HEADS, GDN_DK, GDN_DK)
    one = pl.BlockSpec(sblk, lambda i: (i, 0, 0, 0))
    ins = [qkv, z, ab, conv_w, alog, dtb, nw, conv0, s_all]
    in_specs = [row(QKV_WIDTH), row(GDN_WIDTH), row(LANES), const(GDN_CONV, QKV_WIDTH),
                const(1, LANES), const(1, LANES), const(1, GDN_DK), row(3 * QKV_WIDTH),
                pl.BlockSpec((1,) + sblk, lambda i: (l, i, 0, 0, 0))]
    if prev is None:
        s_shape, s_spec = jax.ShapeDtypeStruct((N,) + sblk[1:], F32), one
    else:
        assert DEPTH == 2 and l == 1
        ins.append(prev)
        in_specs.append(one)
        s_shape = jax.ShapeDtypeStruct((DEPTH, N) + sblk[1:], F32)
        s_spec = pl.BlockSpec((DEPTH,) + sblk, lambda i: (0, i, 0, 0, 0))
    return pl.pallas_call(
        _gdn_step_kernel,
        out_shape=(jax.ShapeDtypeStruct((N, GDN_WIDTH), F32), s_shape),
        grid=(N // nb,),
        in_specs=in_specs,
        out_specs=(row(GDN_WIDTH), s_spec),
        compiler_params=_cp(("parallel",)),
        name="gdn_step",
    )(*ins)


def _merge_kernel(yg_ref, og_ref, ga_ref, gb_ref, x_ref, gt_ref, wglu_ref, wgo_ref, wout_ref,
                  gf_ref, scf_ref, shf_ref, wr_ref,
                  xo_ref, h_ref, lg_ref, *scr, hi, chunked):
    if chunked:
        y_scr = scr[-1]
        scr = scr[:-1]
        nrow = y_scr.shape[1] // S5_T
        for k in range(SLABS):
            for t in range(S5_T):
                y_scr[k, pl.ds(t, nrow, stride=S5_T), :] = (
                    yg_ref[k, 0, :, t * LANES:(t + 1) * LANES].astype(F32))
        y = jnp.concatenate([y_scr[k] for k in range(SLABS)], axis=-1)
    else:
        y = jnp.concatenate([yg_ref[k, 0] for k in range(SLABS)], axis=-1)
    if hi:
        wglu, wgo, wout = wglu_ref[0], wgo_ref[0], wout_ref[0]
        mm = lambda a, w: _dot(a, w, HI)
    else:
        wglu_s, wgo_s, wout_s = scr

        @pl.when((pl.program_id(0) == 0) & (pl.program_id(1) == 0))
        def _():
            wglu_s[...] = wglu_ref[0].astype(BF16)
            wgo_s[...] = wgo_ref[0].astype(BF16)
            wout_s[...] = wout_ref[0].astype(BF16)

        wglu, wgo, wout = wglu_s[...], wgo_s[...], wout_s[...]
        mm = lambda a, w: _dot(a.astype(BF16), w)

    glu = mm(y, wglu)
    branch_a = glu[:, 0:D_MODEL] * jax.nn.sigmoid(glu[:, D_MODEL:])
    branch_b = mm(og_ref[0], wgo)
    merged = ga_ref[0].astype(F32) * branch_a + gb_ref[0].astype(F32) * branch_b
    out = mm(merged, wout)
    x = x_ref[0] + gt_ref[0, 0] * out
    xo_ref[0] = x
    ms = jnp.mean(x * x, axis=-1, keepdims=True)
    h = x * lax.rsqrt(ms + NORM_EPS) * gf_ref[0]
    h = h * (1.0 + scf_ref[0, 0]) + shf_ref[0, 0]
    h_ref[0] = h.astype(h_ref.dtype)
    lg_ref[0] = _dot_nt(wr_ref[0], h, HI)


def _merge(yg, og, ga, gb, x, mod, wglu, wgo, wout, gf, wr, *, l, tm, hi, chunked, h_dtype):
    B, L, D = x.shape
    row = lambda w: pl.BlockSpec((1, tm, w), lambda b, i: (b, i, 0))
    layer = lambda r, w, ll=l: pl.BlockSpec((1, r, w), lambda b, i: (ll, 0, 0))
    scratch = [] if hi else [pltpu.VMEM((S5_WIDTH, 2 * D), BF16), pltpu.VMEM((GDN_WIDTH, D), BF16),
                             pltpu.VMEM((D, D), BF16)]
    if chunked:
        scratch = scratch + [pltpu.VMEM((SLABS, tm, LANES), F32)]
        yg_spec = pl.BlockSpec((SLABS, 1, tm // S5_T, S5_T * LANES), lambda b, i: (0, b, i, 0))
    else:
        yg_spec = pl.BlockSpec((SLABS, 1, tm, LANES), lambda b, i: (0, b, i, 0))
    lg_shape = jax.ShapeDtypeStruct((B, N_EXPERTS, L), F32)
    lg_spec = pl.BlockSpec((1, N_EXPERTS, tm), lambda b, i: (b, 0, i))
    return pl.pallas_call(
        functools.partial(_merge_kernel, hi=hi, chunked=chunked),
        out_shape=(jax.ShapeDtypeStruct((B, L, D), F32),
                   jax.ShapeDtypeStruct((B, L, D), h_dtype),
                   lg_shape),
        grid=(B, L // tm),
        in_specs=[yg_spec,
                  row(GDN_WIDTH), row(D), row(D), row(D), _mod_spec(mod, l, 2, tm),
                  layer(S5_WIDTH, 2 * D), layer(GDN_WIDTH, D), layer(D, D),
                  layer(1, D), _mod_spec(mod, l, 4, tm), _mod_spec(mod, l, 3, tm),
                  layer(N_EXPERTS, D, l // 2)],
        out_specs=(row(D), row(D), lg_spec),
        scratch_shapes=scratch,
        compiler_params=_cp(("arbitrary", "arbitrary")),
        name="merge_out_proj",
    )(yg, og, ga, gb, x, mod, wglu, wgo, wout, gf, mod, mod, wr)


FF_TILE = 512
FFN_ROWS = 512


def _finish(x, gfin_ref, final):
    if not final:
        return x
    ms = jnp.mean(x * x, axis=-1, keepdims=True)
    return x * lax.rsqrt(ms + NORM_EPS) * gfin_ref[...]


def _ffn_kernel(h_ref, x_ref, gt_ref, wg_ref, wu_ref, wd_ref, gfin_ref, o_ref, acc_scr, *, hi, final):
    j = pl.program_id(2)
    if hi:
        wg, wu, wd = wg_ref[...], wu_ref[...], wd_ref[...]
        mm = lambda a, w: _dot(a, w, HI)
    else:
        wg, wu, wd = wg_ref[...].astype(BF16), wu_ref[...].astype(BF16), wd_ref[...].astype(BF16)
        mm = lambda a, w: _dot(a.astype(BF16), w)

    @pl.when(j == 0)
    def _():
        acc_scr[...] = jnp.zeros_like(acc_scr)

    tm = acc_scr.shape[0]
    sub = min(FFN_ROWS, tm)
    for s in range(tm // sub):
        rows = slice(s * sub, (s + 1) * sub)
        hb = h_ref[0, rows, :]
        act = _silu(mm(hb, wg)) * mm(hb, wu)
        acc_scr[rows, :] = acc_scr[rows, :] + mm(act, wd)

    @pl.when(j == pl.num_programs(2) - 1)
    def _():
        o_ref[0] = _finish(x_ref[0] + gt_ref[0, 0] * acc_scr[...], gfin_ref, final)


def _ffn(h, x, mod, w_gu, w_down, gfin, *, l, tm, hi, final):
    B, L, D = x.shape
    nj = D_FF // FF_TILE
    row = pl.BlockSpec((1, tm, D), lambda b, i, j: (b, i, 0))
    return pl.pallas_call(
        functools.partial(_ffn_kernel, hi=hi, final=final),
        out_shape=jax.ShapeDtypeStruct((B, L, D), F32),
        grid=(B, L // tm, nj),
        in_specs=[row, row, _mod_spec(mod, l, 5, tm),
                  pl.BlockSpec((D, FF_TILE), lambda b, i, j: (0, j)),
                  pl.BlockSpec((D, FF_TILE), lambda b, i, j: (0, nj + j)),
                  pl.BlockSpec((FF_TILE, D), lambda b, i, j: (j, 0)),
                  pl.BlockSpec((1, D), lambda b, i, j: (0, 0))],
        out_specs=row,
        scratch_shapes=[pltpu.VMEM((tm, D), F32)],
        compiler_params=_cp(("parallel", "parallel", "arbitrary")),
        name="ffn_dense",
    )(h, x, mod, w_gu, w_gu, w_down, gfin)


ROUTE_TM = 512
ROW_DMA_TM = 256
MOE_SUP = 2048
MOE_SUB = 512
MOE_FF_TILE = 512


def _route_kernel(lg_ref, br_ref, cnt0_ref, slot_ref, wt_ref, cnt_ref, carry_scr, *, cap):
    @pl.when((pl.program_id(0) == 0) & (pl.program_id(1) == 0))
    def _():
        carry_scr[...] = cnt0_ref[...]

    lg = lg_ref[0] + br_ref[...]
    tm = lg.shape[1]
    eidx = lax.broadcasted_iota(jnp.int32, lg.shape, 0)
    m1 = jnp.max(lg, axis=0, keepdims=True)
    i1 = jnp.min(jnp.where(lg == m1, eidx, N_EXPERTS), axis=0, keepdims=True)
    lg2 = jnp.where(eidx == i1, -jnp.inf, lg)
    m2 = jnp.max(lg2, axis=0, keepdims=True)
    i2 = jnp.min(jnp.where(lg2 == m2, eidx, N_EXPERTS), axis=0, keepdims=True)
    e2 = jnp.exp(m2 - m1)
    wt_ref[0, 0:1, :] = 1.0 / (1.0 + e2)
    wt_ref[0, 1:2, :] = e2 / (1.0 + e2)
    sel1 = eidx == i1
    sel2 = eidx == i2
    oh = jnp.where(sel1 | sel2, 1.0, 0.0)
    before = (lax.broadcasted_iota(jnp.int32, (tm, tm), 0)
              < lax.broadcasted_iota(jnp.int32, (tm, tm), 1)).astype(BF16)
    rank = carry_scr[:, 0:1] + _dot(oh.astype(BF16), before)
    r1 = jnp.sum(jnp.where(sel1, rank, 0.0), axis=0, keepdims=True).astype(jnp.int32)
    r2 = jnp.sum(jnp.where(sel2, rank, 0.0), axis=0, keepdims=True).astype(jnp.int32)
    slot_ref[0, 0:1, :] = i1 * cap + r1
    slot_ref[0, 1:2, :] = i2 * cap + r2
    carry_scr[...] = carry_scr[...] + jnp.sum(oh, axis=1, keepdims=True)
    cnt_ref[...] = carry_scr[...]


def _route_slots(lgT, b_r, cnt0, cap):
    B, E, L = lgT.shape
    tm = min(ROUTE_TM, L)
    return pl.pallas_call(
        functools.partial(_route_kernel, cap=cap),
        out_shape=(jax.ShapeDtypeStruct((B, 2, L), jnp.int32),
                   jax.ShapeDtypeStruct((B, 2, L), F32),
                   jax.ShapeDtypeStruct((E, LANES), F32)),
        grid=(B, L // tm),
        in_specs=[pl.BlockSpec((1, E, tm), lambda b, i: (b, 0, i)),
                  pl.BlockSpec((E, 1), lambda b, i: (0, 0)),
                  pl.BlockSpec((E, LANES), lambda b, i: (0, 0))],
        out_specs=(pl.BlockSpec((1, 2, tm), lambda b, i: (b, 0, i)),
                   pl.BlockSpec((1, 2, tm), lambda b, i: (b, 0, i)),
                   pl.BlockSpec((E, LANES), lambda b, i: (0, 0))),
        scratch_shapes=[pltpu.VMEM((E, LANES), F32)],
        compiler_params=_cp(("arbitrary", "arbitrary")),
        name="moe_route",
    )(lgT, b_r.reshape(E, 1), cnt0)


def _row_copy(src, dst, sem):
    return pltpu.make_async_copy(src, dst, sem)


def _slot_rows_kernel(start_ref, per_ref, code_ref, row_ref, *, cap):
    code = code_ref[...]
    shift = cap.bit_length() - 1
    e = lax.shift_right_logical(code, shift)
    r = code & (cap - 1)
    start = jnp.zeros_like(code)
    per = jnp.ones_like(code)
    for k in range(N_EXPERTS):
        start = jnp.where(e == k, start_ref[k], start)
        per = jnp.where(e == k, per_ref[k], per)
    q = jnp.floor((r.astype(F32) + 0.5) / per.astype(F32)).astype(jnp.int32)
    row_ref[...] = start + q * MOE_SUP + (r - q * per)


def _slot_rows(start, per, codes, cap):
    B, _, L = codes.shape
    tm = min(ROUTE_TM, L)
    spec = pl.BlockSpec((1, 2, tm), lambda b, i, st, pe: (b, 0, i))
    return pl.pallas_call(
        functools.partial(_slot_rows_kernel, cap=cap),
        out_shape=jax.ShapeDtypeStruct(codes.shape, jnp.int32),
        grid_spec=pltpu.PrefetchScalarGridSpec(
            num_scalar_prefetch=2, grid=(B, L // tm), in_specs=[spec], out_specs=spec),
        compiler_params=_cp(("parallel", "parallel")),
        name="moe_slot_rows",
    )(start, per, codes)


def _zeros_kernel(o_ref):
    o_ref[...] = jnp.zeros_like(o_ref)


def _zero_rows(n_rows, width):
    return pl.pallas_call(
        _zeros_kernel,
        out_shape=jax.ShapeDtypeStruct((n_rows, width), F32),
        grid=(n_rows // MOE_SUP,),
        out_specs=pl.BlockSpec((MOE_SUP, width), lambda i: (i, 0)),
        compiler_params=_cp(("parallel",)),
        name="moe_zero_rows",
    )()


def _dispatch_kernel(row_ref, h_ref, xs_in_ref, xs_ref, hbuf, sem):
    del xs_in_ref
    tm = h_ref.shape[1]
    t = pl.program_id(0) * pl.num_programs(1) + pl.program_id(1)
    last = pl.num_programs(0) * pl.num_programs(1) - 1
    slot = t % 2
    hbuf[slot] = h_ref[0]

    def issue(r, _):
        for k in range(2):
            row = row_ref[0, k, r]
            _row_copy(hbuf.at[slot, pl.ds(r, 1), :], xs_ref.at[pl.ds(row, 1), :], sem.at[slot]).start()
        return 0

    lax.fori_loop(0, tm, issue, 0, unroll=8)

    def drain(sl):
        for k in range(2):
            _row_copy(hbuf.at[sl], xs_ref.at[pl.ds(0, tm), :], sem.at[sl]).wait()

    @pl.when(t > 0)
    def _():
        drain(1 - slot)

    @pl.when(t == last)
    def _():
        drain(slot)


def _dispatch(rows, h, xs):
    B, L, D = h.shape
    n_rows = xs.shape[0]
    tm = min(ROW_DMA_TM, L)
    return pl.pallas_call(
        _dispatch_kernel,
        out_shape=jax.ShapeDtypeStruct((n_rows, D), F32),
        grid=(B, L // tm),
        in_specs=[pl.BlockSpec((1, 2, tm), lambda b, i: (b, 0, i), memory_space=pltpu.SMEM),
                  pl.BlockSpec((1, tm, D), lambda b, i: (b, i, 0)),
                  pl.BlockSpec(memory_space=pl.ANY)],
        out_specs=pl.BlockSpec(memory_space=pl.ANY),
        scratch_shapes=[pltpu.VMEM((2, tm, D), F32), pltpu.SemaphoreType.DMA((2,))],
        input_output_aliases={2: 0},
        compiler_params=_cp(("arbitrary", "arbitrary")),
        name="moe_dispatch",
    )(rows, h, xs)


def _moe_grp_kernel(ge_ref, gn_ref, x_ref, wg_ref, wu_ref, wd_ref, y_ref, xb_scr):
    g = pl.program_id(0)
    j = pl.program_id(1)
    nsub = gn_ref[g]
    wg = wg_ref[0].astype(BF16)
    wu = wu_ref[0].astype(BF16)
    wd = wd_ref[0].astype(BF16)
    nblk = MOE_SUP // MOE_SUB

    @pl.when(j == 0)
    def _():
        xb_scr[...] = x_ref[...].astype(BF16)
        y_ref[...] = jnp.zeros_like(y_ref)

    def block(s):
        rows = slice(s * MOE_SUB, (s + 1) * MOE_SUB)
        xb = xb_scr[rows, :]
        act = _silu(_dot(xb, wg)) * _dot(xb, wu)
        y_ref[rows, :] = y_ref[rows, :] + _dot(act.astype(BF16), wd)

    for n in range(1, nblk + 1):
        @pl.when(nsub == n)
        def _():
            for s in range(n):
                block(s)


def _moe_groups(counts, n_groups):
    nsup = (counts + MOE_SUP - 1) // MOE_SUP
    div = jnp.maximum(nsup, 1)
    per = jnp.maximum(((counts + div - 1) // div + MOE_SUB - 1) // MOE_SUB * MOE_SUB, MOE_SUB)
    ends = jnp.cumsum(nsup)
    first = ends - nsup
    total = ends[-1]
    g = jnp.arange(n_groups, dtype=jnp.int32)
    gc = jnp.minimum(g, total - 1)
    e_of = jnp.minimum(jnp.sum((gc[:, None] >= ends[None, :]).astype(jnp.int32), axis=1), N_EXPERTS - 1)
    left = jnp.minimum(counts[e_of] - (gc - first[e_of]) * per[e_of], per[e_of])
    nsub = jnp.clip((left + MOE_SUB - 1) // MOE_SUB, 0, MOE_SUP // MOE_SUB)
    gn = jnp.where(g < total, nsub, 0).astype(jnp.int32)
    return e_of, gn, (first * MOE_SUP).astype(jnp.int32), per.astype(jnp.int32)


def _moe_grouped(xs, ge, gn, w_gu, w_down):
    D = xs.shape[1]
    nj = D_FF // MOE_FF_TILE
    ng = xs.shape[0] // MOE_SUP
    jj = lambda j, gn, g: jnp.where(gn[g] > 0, j, nj - 1)
    return pl.pallas_call(
        _moe_grp_kernel,
        out_shape=jax.ShapeDtypeStruct(xs.shape, F32),
        grid_spec=pltpu.PrefetchScalarGridSpec(
            num_scalar_prefetch=2,
            grid=(ng, nj),
            in_specs=[pl.BlockSpec((MOE_SUP, D), lambda g, j, ge, gn: (g, 0)),
                      pl.BlockSpec((1, D, MOE_FF_TILE), lambda g, j, ge, gn: (ge[g], 0, jj(j, gn, g))),
                      pl.BlockSpec((1, D, MOE_FF_TILE), lambda g, j, ge, gn: (ge[g], 0, nj + jj(j, gn, g))),
                      pl.BlockSpec((1, MOE_FF_TILE, D), lambda g, j, ge, gn: (ge[g], jj(j, gn, g), 0))],
            out_specs=pl.BlockSpec((MOE_SUP, D), lambda g, j, ge, gn: (g, 0)),
            scratch_shapes=[pltpu.VMEM((MOE_SUP, D), BF16)],
        ),
        compiler_params=_cp(("arbitrary", "arbitrary")),
        name="moe_experts",
    )(ge, gn, xs, w_gu, w_gu, w_down)


def _combine_kernel(row_ref, next_ref, w_ref, x_ref, gt_ref, gfin_ref, ys_ref, o_ref, g_scr, sem, *, final):
    tm = x_ref.shape[1]
    t = pl.program_id(0) * pl.num_programs(1) + pl.program_id(1)
    last = pl.num_programs(0) * pl.num_programs(1) - 1
    slot = t % 2

    def gather(rows, sl):
        def issue(r, _):
            for k in range(2):
                row = rows[0, k, r]
                _row_copy(ys_ref.at[pl.ds(row, 1), :], g_scr.at[sl, k, pl.ds(r, 1), :], sem.at[sl]).start()
            return 0

        lax.fori_loop(0, tm, issue, 0, unroll=8)

    @pl.when(t == 0)
    def _():
        gather(row_ref, slot)

    @pl.when(t < last)
    def _():
        gather(next_ref, 1 - slot)

    for k in range(2):
        _row_copy(ys_ref.at[pl.ds(0, tm), :], g_scr.at[slot, k], sem.at[slot]).wait()
    w = w_ref[0]
    f = w[:, 0:1] * g_scr[slot, 0] + w[:, 1:2] * g_scr[slot, 1]
    o_ref[0] = _finish(x_ref[0] + gt_ref[0, 0] * f, gfin_ref, final)


def _combine(rows, wts, x, mod, gfin, ys, *, l, final):
    B, L, D = x.shape
    tm = min(ROW_DMA_TM, L)
    gt_spec = _mod_spec(mod, l, 5, tm)
    row = pl.BlockSpec((1, tm, D), lambda b, i: (b, i, 0))
    nl = L // tm

    def next_block(b, i):
        t1 = jnp.minimum(b * nl + i + 1, B * nl - 1)
        return (t1 // nl, 0, t1 % nl)

    return pl.pallas_call(
        functools.partial(_combine_kernel, final=final),
        out_shape=jax.ShapeDtypeStruct((B, L, D), F32),
        grid=(B, L // tm),
        in_specs=[pl.BlockSpec((1, 2, tm), lambda b, i: (b, 0, i), memory_space=pltpu.SMEM),
                  pl.BlockSpec((1, 2, tm), next_block, memory_space=pltpu.SMEM),
                  pl.BlockSpec((1, tm, 2), lambda b, i: (b, i, 0)),
                  row, gt_spec,
                  pl.BlockSpec((1, D), lambda b, i: (0, 0)),
                  pl.BlockSpec(memory_space=pl.ANY)],
        out_specs=row,
        scratch_shapes=[pltpu.VMEM((2, 2, tm, D), F32), pltpu.SemaphoreType.DMA((2,))],
        compiler_params=_cp(("arbitrary", "arbitrary")),
        name="moe_combine",
    )(rows, rows, wts.transpose(0, 2, 1), x, mod, gfin, ys)


def _moe_routed(groups, b_r, w_gu, w_down, gfin, *, l, final):
    D = groups[0][1].shape[-1]
    n_tok = sum(g[1].shape[0] * g[1].shape[1] for g in groups)
    cap = 1 << (n_tok - 1).bit_length()
    n_groups = 2 * n_tok // MOE_SUP + N_EXPERTS
    cnt = jnp.zeros((N_EXPERTS, LANES), F32)
    routed = []
    for _, _, _, lgT in groups:
        codes, wts, cnt = _route_slots(lgT, b_r, cnt, cap)
        routed.append((codes, wts))
    ge, gn, start, per = _moe_groups(cnt[:, 0].astype(jnp.int32), n_groups)
    rows = [_slot_rows(start, per, codes, cap) for codes, _ in routed]
    xs = _zero_rows(n_groups * MOE_SUP, D)
    for (h, _, _, _), r in zip(groups, rows):
        xs = _dispatch(r, h, xs)
    ys = _moe_grouped(xs, ge, gn, w_gu, w_down)
    return [_combine(r, wts, x, mod, gfin, ys, l=l, final=final)
            for (_, x, mod, _), (_, wts), r in zip(groups, routed, rows)]


def _pad_lanes(v):
    return jnp.pad(v.reshape(1, -1), ((0, 0), (0, LANES - v.shape[-1])))


def _mixer_layer(x, mod, states, p, s5m, l, prev_sg, *, seq):
    B, L, D = x.shape
    hi = not seq
    s5r0, s5i0, sg0, sc0 = states
    w_in, w_gates, w_ab = p['w_in_seq' if seq else 'w_in']
    if seq:
        u, qkv, z, ga, gb, ab = _proj_seq(x, p['g_mix'], mod, w_in, w_gates, w_ab, l=l, tm=min(512, L))
    else:
        u, qkv, z, ga, gb, ab = _proj(x, p['g_mix'], mod, w_in, w_gates, w_ab, l=l, tm=L)
    alog = _pad_lanes(p['gdn_a_log'][l])
    dtb = _pad_lanes(p['gdn_dt_bias'][l])
    nw = p['gdn_norm_w'][l].reshape(1, GDN_DK)
    if seq:
        yg, sfin = _s5_seq(u, s5m['be'], s5m['tp'], s5m['cpm'], s5m['pt'],
                           jnp.zeros((SLABS, B, 1, 2 * SLAB_STATE), F32), s5m['dsk'][l], l)
        sfin = sfin.reshape(SLABS, B, 2, SLAB_STATE).transpose(2, 1, 0, 3)
        sr = sfin[0].reshape(B, S5_GROUPS, S5_STATE)
        si = sfin[1].reshape(B, S5_GROUPS, S5_STATE)
        og, sg = _gdn_seq(qkv, z, ab, p['gdn_conv_w'], alog, dtb, nw,
                          jnp.zeros((B, GDN_CONV - 1, QKV_WIDTH), F32),
                          jnp.zeros((B, GDN_HEADS, GDN_DK, GDN_DK), F32), l)
        cb = qkv[:, L - (GDN_CONV - 1):, :].astype(F32)
    else:
        n = L
        s0 = jnp.concatenate([s5r0[l].reshape(n, SLABS, SLAB_STATE),
                              s5i0[l].reshape(n, SLABS, SLAB_STATE)], axis=-1).transpose(1, 0, 2)
        yg, s1 = _s5_step(u.reshape(SLABS, n, LANES), s5m['bst'], s5m['cpe'], s5m['a1'],
                          s0, s5m['d1'][l], l)
        yg = yg.reshape(SLABS, 1, n, LANES)
        s1 = s1.transpose(1, 0, 2)
        sr = s1[:, :, :SLAB_STATE].reshape(n, S5_GROUPS, S5_STATE)
        si = s1[:, :, SLAB_STATE:].reshape(n, S5_GROUPS, S5_STATE)
        og, sg = _gdn_step(qkv.reshape(n, QKV_WIDTH), z.reshape(n, GDN_WIDTH), ab.reshape(n, LANES),
                           p['gdn_conv_w'][l], alog, dtb, nw,
                           sc0[l].reshape(n, (GDN_CONV - 1) * QKV_WIDTH), sg0, l, prev_sg)
        og = og.reshape(1, n, GDN_WIDTH)
        cb = jnp.concatenate([sc0[l][:, 1:, :], qkv.reshape(n, 1, QKV_WIDTH)], axis=1)
    x, h, lgT = _merge(yg, og, ga, gb, x, mod, p['w_s5_glu'], p['w_gdn_out'], p['w_out'],
                       p['g_ffn'], p['w_router'], l=l, tm=min(512, L), hi=hi, chunked=seq,
                       h_dtype=BF16 if (seq and l % 2 == 0) else F32)
    return x, h, lgT, (sr, si, sg, cb)


def kernel(x_prompt, x_sample, c_prompt, c_sample, state_s5_re, state_s5_im, state_gdn, state_conv,
           g_mix, g_ffn, g_final, w_ada, b_ada, w_in, s5_lambda_re, s5_lambda_im, s5_log_dt,
           s5_b_re, s5_b_im, s5_c_re, s5_c_im, s5_d, w_s5_glu, gdn_conv_w, gdn_a_log, gdn_dt_bias,
           gdn_norm_w, w_gdn_out, w_out, w_ffn_gate_up, w_ffn_down, w_router, b_router,
           w_exp_gate_up, w_exp_down):
    def in_proj_parts(w):
        return w, w[:, :, 2568:], jnp.pad(w[:, :, 2560:2568], ((0, 0), (0, 0), (0, LANES - 8)))

    D_ = x_prompt.shape[-1]
    p = dict(g_mix=g_mix.reshape(DEPTH, 1, D_), g_ffn=g_ffn.reshape(DEPTH, 1, D_), w_s5_glu=w_s5_glu,
             gdn_conv_w=gdn_conv_w, gdn_a_log=gdn_a_log, gdn_dt_bias=gdn_dt_bias,
             gdn_norm_w=gdn_norm_w, w_gdn_out=w_gdn_out, w_out=w_out,
             w_router=w_router.transpose(0, 2, 1), w_in=in_proj_parts(w_in),
             w_in_seq=in_proj_parts(w_in.astype(BF16)))
    nbp, L, D = x_prompt.shape
    nbs = x_sample.shape[0]

    mod = _ada(jnp.concatenate([c_prompt, c_sample], axis=0), w_ada, b_ada)
    mod_p = mod[:, :nbp].reshape(DEPTH, nbp, 1, 6 * D)
    mod_s = mod[:, nbp:].reshape(DEPTH, 1, nbs, 6 * D)

    seg = L // S5_T // 8
    be, bst, cpe, cpm, pt, a1 = _s5_prep(s5_lambda_re, s5_lambda_im, s5_log_dt, s5_b_re, s5_b_im,
                                         s5_c_re, s5_c_im, seg)
    d1 = [s5_d[l].reshape(SLABS, 1, LANES) for l in range(DEPTH)]
    s5m = dict(be=be, bst=bst, cpe=cpe, cpm=cpm, pt=pt, a1=a1, tp=_toep(bst, cpe), d1=d1,
               dsk=[jnp.tile(d, (1, 1, S5_T)) for d in d1])

    xs_ = [x_prompt, x_sample.reshape(1, nbs, D)]
    mods = [mod_p, mod_s]
    states = [(None, None, None, None), (state_s5_re, state_s5_im, state_gdn, state_conv)]
    outs = [[], []]
    gfin = g_final.reshape(1, D)
    for l in range(DEPTH):
        final = l == DEPTH - 1
        mixed = []
        for gi, seq in enumerate((True, False)):
            prev_sg = outs[gi][0][2] if (not seq and final and DEPTH == 2) else None
            x, h, lgT, st = _mixer_layer(xs_[gi], mods[gi], states[gi], p, s5m, l, prev_sg, seq=seq)
            outs[gi].append(st)
            mixed.append((h, x, mods[gi], lgT))
        if l % 2 == 0:
            wgu, wdn = w_ffn_gate_up[l // 2], w_ffn_down[l // 2]
            xs_ = [_ffn(h, x, mod_g, wgu if gi else wgu.astype(BF16), wdn if gi else wdn.astype(BF16), gfin,
                        l=l, tm=min(1024, x.shape[1]), hi=(gi == 1), final=final)
                   for gi, (h, x, mod_g, _) in enumerate(mixed)]
        else:
            xs_ = _moe_routed(mixed, b_router[l // 2], w_exp_gate_up[l // 2], w_exp_down[l // 2], gfin,
                              l=l, final=final)
    y_p, y_s = xs_
    st_p = [jnp.stack([o[i] for o in outs[0]]) for i in range(4)]
    st_s = [outs[1][-1][2] if (i == 2 and DEPTH == 2) else jnp.stack([o[i] for o in outs[1]])
            for i in range(4)]
    return (y_p, y_s.reshape(nbs, 1, D), st_p[0], st_p[1], st_p[2], st_p[3],
            st_s[0], st_s[1], st_s[2], st_s[3])
```

```python
import functools

import jax
import jax.numpy as jnp
from jax import lax
from jax.experimental import pallas as pl
from jax.experimental.pallas import tpu as pltpu

F32 = jnp.float32
BF16 = jnp.bfloat16
HI = lax.Precision.HIGHEST

D_MODEL = 1024
DEPTH = 2
S5_WIDTH = 512
S5_GROUP = 16
S5_GROUPS = 32
S5_STATE = 64
GDN_HEADS = 4
GDN_DK = 128
GDN_WIDTH = 512
GDN_CONV = 4
QKV_WIDTH = 1536
D_FF = 3584
N_EXPERTS = 8
NORM_EPS = 1e-6
L2_EPS = 1e-6

LANES = 128
SLABS = S5_WIDTH // LANES
SLAB_STATE = (S5_GROUPS // SLABS) * S5_STATE
S5_T = 8
S5_SEG_PAD = 4
GDN_C = 128
VMEM_LIMIT = 56 * 1024 * 1024


def _cp(sem, vmem=VMEM_LIMIT):
    return pltpu.CompilerParams(dimension_semantics=sem, vmem_limit_bytes=vmem)


def _dot(a, b, prec=None):
    return jnp.dot(a, b, precision=prec, preferred_element_type=F32)


def _dotb(a, b):
    return jnp.dot(a.astype(BF16), b.astype(BF16), preferred_element_type=F32)


def _dot_nt(a, b, prec=None):
    return lax.dot_general(a, b, (((1,), (1,)), ((), ())), precision=prec,
                           preferred_element_type=F32)


def _dot_tn(a, b, prec=None):
    return lax.dot_general(a, b, (((0,), (0,)), ((), ())), precision=prec,
                           preferred_element_type=F32)


def _silu(x):
    return x * jax.nn.sigmoid(x)


def _ada_kernel(c_ref, w_ref, b_ref, o_ref):
    cs = _silu(c_ref[...])
    o_ref[0] = _dot(cs, w_ref[0], HI) + b_ref[0]


def _ada(c_all, w_ada, b_ada):
    n = c_all.shape[0]
    tn = 1536
    return pl.pallas_call(
        _ada_kernel,
        out_shape=jax.ShapeDtypeStruct((DEPTH, n, 6 * D_MODEL), F32),
        grid=(DEPTH, 6 * D_MODEL // tn),
        in_specs=[pl.BlockSpec((n, D_MODEL), lambda l, j: (0, 0)),
                  pl.BlockSpec((1, D_MODEL, tn), lambda l, j: (l, 0, j)),
                  pl.BlockSpec((1, 1, tn), lambda l, j: (l, 0, j))],
        out_specs=pl.BlockSpec((1, n, tn), lambda l, j: (l, 0, j)),
        compiler_params=_cp(("parallel", "parallel")),
        name="ada_mod",
    )(c_all, w_ada, b_ada.reshape(DEPTH, 1, 6 * D_MODEL))


def _proj_kernel(x_ref, g_ref, sc_ref, sh_ref, w_ref, wg_ref, wab_ref,
                 u_ref, qkv_ref, z_ref, ga_ref, gb_ref, ab_ref, h_scr):
    j = pl.program_id(2)

    @pl.when(j == 0)
    def _():
        x = x_ref[0]
        ms = jnp.mean(x * x, axis=-1, keepdims=True)
        xn = x * lax.rsqrt(ms + NORM_EPS) * g_ref[0]
        h_scr[...] = (xn * (1.0 + sc_ref[0, 0]) + sh_ref[0, 0]).astype(h_scr.dtype)

    def mm(w):
        return _dot(h_scr[...], w, HI)

    @pl.when(j == 0)
    def _():
        res = mm(w_ref[0])
        for k in range(SLABS):
            u_ref[k, 0] = res[:, k * LANES:(k + 1) * LANES]

    @pl.when((j >= 1) & (j <= 3))
    def _():
        qkv_ref[0] = mm(w_ref[0])

    @pl.when(j == 4)
    def _():
        z_ref[0] = mm(w_ref[0])

    @pl.when((j == 5) | (j == 6))
    def _():
        ga_ref[0] = jax.nn.sigmoid(mm(wg_ref[0]))

    @pl.when((j == 7) | (j == 8))
    def _():
        gb_ref[0] = jax.nn.sigmoid(mm(wg_ref[0]))

    @pl.when(j == 9)
    def _():
        ab_ref[0] = mm(wab_ref[0])


def _mod_spec(mod, l, chunk, tm):
    per_row = mod.shape[2] != 1
    D = mod.shape[3] // 6

    def index(b, i, *_):
        return (l, b, i if per_row else 0, chunk)

    return pl.BlockSpec((1, 1, tm if per_row else 1, D), index)


def _proj(x, g, mod, w_in, w_gates, w_ab, *, l, tm):
    B, L, D = x.shape
    tn = 512
    clampi = lambda j, lo, n: jnp.clip(j - lo, 0, n - 1)
    outs = pl.pallas_call(
        _proj_kernel,
        out_shape=(jax.ShapeDtypeStruct((SLABS, B, L, LANES), F32),
                   jax.ShapeDtypeStruct((B, L, QKV_WIDTH), F32),
                   jax.ShapeDtypeStruct((B, L, GDN_WIDTH), F32),
                   jax.ShapeDtypeStruct((B, L, D), F32),
                   jax.ShapeDtypeStruct((B, L, D), F32),
                   jax.ShapeDtypeStruct((B, L, LANES), F32)),
        grid=(B, L // tm, 10),
        in_specs=[pl.BlockSpec((1, tm, D), lambda b, i, j: (b, i, 0)),
                  pl.BlockSpec((1, 1, D), lambda b, i, j: (l, 0, 0)),
                  _mod_spec(mod, l, 1, tm),
                  _mod_spec(mod, l, 0, tm),
                  pl.BlockSpec((1, D, tn), lambda b, i, j: (l, 0, jnp.minimum(j, 4))),
                  pl.BlockSpec((1, D, tn), lambda b, i, j: (l, 0, clampi(j, 5, 4))),
                  pl.BlockSpec((1, D, LANES), lambda b, i, j: (l, 0, 0))],
        out_specs=(pl.BlockSpec((SLABS, 1, tm, LANES), lambda b, i, j: (0, b, i, 0)),
                   pl.BlockSpec((1, tm, tn), lambda b, i, j: (b, i, clampi(j, 1, 3))),
                   pl.BlockSpec((1, tm, tn), lambda b, i, j: (b, i, 0)),
                   pl.BlockSpec((1, tm, tn), lambda b, i, j: (b, i, clampi(j, 5, 2))),
                   pl.BlockSpec((1, tm, tn), lambda b, i, j: (b, i, clampi(j, 7, 2))),
                   pl.BlockSpec((1, tm, LANES), lambda b, i, j: (b, i, 0))),
        scratch_shapes=[pltpu.VMEM((tm, D), F32)],
        compiler_params=_cp(("parallel", "parallel", "arbitrary")),
        name="norm_in_proj",
    )(x, g, mod, mod, w_in, w_gates, w_ab)
    return outs


def _proj_seq_kernel(x_ref, g_ref, sc_ref, sh_ref, w_ref, wg_ref, wab_ref,
                     u_ref, qkv_ref, z_ref, ga_ref, gb_ref, ab_ref, us_scr):
    x = x_ref[0]
    ms = jnp.mean(x * x, axis=-1, keepdims=True)
    xn = x * lax.rsqrt(ms + NORM_EPS) * g_ref[0]
    h = (xn * (1.0 + sc_ref[0, 0]) + sh_ref[0, 0]).astype(BF16)
    res = _dot(h, w_ref[0, :, 0:S5_WIDTH])
    nrow = res.shape[0] // S5_T
    for k in range(SLABS):
        us_scr[...] = res[:, k * LANES:(k + 1) * LANES]
        for t in range(S5_T):
            u_ref[k, 0, :, t * LANES:(t + 1) * LANES] = (
                us_scr[pl.ds(t, nrow, stride=S5_T), :].astype(u_ref.dtype))
    c0 = S5_WIDTH
    qkv_ref[0] = _dot(h, w_ref[0, :, c0:c0 + QKV_WIDTH]).astype(qkv_ref.dtype)
    c0 += QKV_WIDTH
    z_ref[0] = _dot(h, w_ref[0, :, c0:c0 + GDN_WIDTH]).astype(z_ref.dtype)
    D = x.shape[-1]
    ga_ref[0] = jax.nn.sigmoid(_dot(h, wg_ref[0, :, 0:D])).astype(ga_ref.dtype)
    gb_ref[0] = jax.nn.sigmoid(_dot(h, wg_ref[0, :, D:2 * D])).astype(gb_ref.dtype)
    ab_ref[0] = _dot(h, wab_ref[0])


def _proj_seq(x, g, mod, w_in, w_gates, w_ab, *, l, tm):
    B, L, D = x.shape
    n_main = S5_WIDTH + QKV_WIDTH + GDN_WIDTH
    row = lambda w: pl.BlockSpec((1, tm, w), lambda b, i: (b, i, 0))
    return pl.pallas_call(
        _proj_seq_kernel,
        out_shape=(jax.ShapeDtypeStruct((SLABS, B, L // S5_T, S5_T * LANES), BF16),
                   jax.ShapeDtypeStruct((B, L, QKV_WIDTH), BF16),
                   jax.ShapeDtypeStruct((B, L, GDN_WIDTH), BF16),
                   jax.ShapeDtypeStruct((B, L, D), BF16),
                   jax.ShapeDtypeStruct((B, L, D), BF16),
                   jax.ShapeDtypeStruct((B, L, LANES), F32)),
        grid=(B, L // tm),
        in_specs=[row(D),
                  pl.BlockSpec((1, 1, D), lambda b, i: (l, 0, 0)),
                  _mod_spec(mod, l, 1, tm),
                  _mod_spec(mod, l, 0, tm),
                  pl.BlockSpec((1, D, n_main), lambda b, i: (l, 0, 0)),
                  pl.BlockSpec((1, D, 2 * D), lambda b, i: (l, 0, 0)),
                  pl.BlockSpec((1, D, LANES), lambda b, i: (l, 0, 0))],
        out_specs=(pl.BlockSpec((SLABS, 1, tm // S5_T, S5_T * LANES), lambda b, i: (0, b, i, 0)),
                   row(QKV_WIDTH), row(GDN_WIDTH), row(D), row(D), row(LANES)),
        scratch_shapes=[pltpu.VMEM((tm, LANES), F32)],
        compiler_params=_cp(("parallel", "parallel")),
        name="norm_in_proj_seq",
    )(x, g, mod, mod, w_in, w_gates, w_ab)


GROUPS_PER_SLAB = S5_GROUPS // SLABS


def _s5_prep_kernel(lrb, lib, dtb, bre, bim, lrc, lic, dtc, cre, cim, lrn, lin, dtn,
                    be_ref, bst_ref, cpe_ref, cpm_ref, pt_ref, a1_ref, *, seg):
    W = SLAB_STATE

    def disc(lr, li, ldt):
        dt = jnp.exp(ldt)
        mag = jnp.exp(lr * dt)
        return mag * jnp.cos(li * dt), mag * jnp.sin(li * dt)

    def cmul(xr, xi, yr, yi):
        return xr * yr - xi * yi, xr * yi + xi * yr

    lr, li = lrb[0], lib[0]
    ar, ai = disc(lr, li, dtb[0])
    den = lr * lr + li * li
    nr = ar - 1.0
    kr = (nr * lr + ai * li) / den
    ki = (ai * lr - nr * li) / den
    br, bi = bre[0], bim[0]
    bbr = kr * br - ki * bi
    bbi = kr * bi + ki * br
    rgrp = lax.broadcasted_iota(jnp.int32, (LANES, LANES), 0) // S5_GROUP
    lane_hi = lax.broadcasted_iota(jnp.int32, (LANES, LANES), 1) // S5_STATE
    pr, pi = jnp.ones_like(ar), jnp.zeros_like(ar)
    for d in range(S5_T):
        t = S5_T - 1 - d
        for ri, val in enumerate(cmul(pr, pi, bbr, bbi)):
            two = jnp.concatenate([val, val], axis=1)
            for m in range(GROUPS_PER_SLAB // 2):
                tile = jnp.where(rgrp == 2 * m + lane_hi, two, 0.0)
                c0 = ri * W + m * LANES
                be_ref[0, 0, t * LANES:(t + 1) * LANES, c0:c0 + LANES] = tile.astype(BF16)
                if d == 0:
                    bst_ref[0, 0, :, c0:c0 + LANES] = tile
        pr, pi = cmul(pr, pi, ar, ai)

    ar, ai = disc(lrc[0], lic[0], dtc[0])
    cr, ci = cre[0], cim[0]
    own = (lax.broadcasted_iota(jnp.int32, (W, LANES), 0) // S5_STATE
           == lax.broadcasted_iota(jnp.int32, (W, LANES), 1) // S5_GROUP)
    pr, pi = jnp.ones_like(ar), jnp.zeros_like(ar)
    for d in range(S5_T + 1):
        vr, vi = cmul(cr, ci, pr, pi)
        for ri, val in enumerate((vr, -vi)):
            tile = jnp.where(own, val, 0.0)
            cpe_ref[0, 0, d, ri * W:(ri + 1) * W, :] = tile
            if d >= 1:
                cpm_ref[0, 0, ri * W:(ri + 1) * W, (d - 1) * LANES:d * LANES] = tile.astype(BF16)
        pr, pi = cmul(pr, pi, ar, ai)

    ar, ai = disc(lrn[0, 0], lin[0, 0], dtn[0, 0])
    a1_ref[0, 0, :, 0:W] = ar
    a1_ref[0, 0, :, W:2 * W] = ai
    tr, ti = ar, ai
    for _ in range(S5_T - 1):
        tr, ti = cmul(tr, ti, ar, ai)
    pr, pi = jnp.ones_like(ar), jnp.zeros_like(ar)
    for i in range(seg + 1):
        pt_ref[0, 0, i:i + 1, 0:W] = pr
        pt_ref[0, 0, i:i + 1, W:2 * W] = pi
        pr, pi = cmul(pr, pi, tr, ti)


def _s5_prep(lam_re, lam_im, log_dt, b_re, b_im, c_re, c_im, seg):
    G, P, C = S5_GROUPS, S5_STATE, S5_GROUP
    W2 = 2 * SLAB_STATE
    dt3 = jnp.broadcast_to(log_dt[:, :, None], (DEPTH, G, P))
    rows_b = lambda a: jnp.repeat(a, C, axis=1)
    bt = lambda a: a.transpose(0, 1, 3, 2).reshape(DEPTH, G * C, P)
    rows_c = lambda a: jnp.broadcast_to(a.reshape(DEPTH, G * P, 1), (DEPTH, G * P, LANES))
    ct = lambda a: jnp.tile(a.transpose(0, 1, 3, 2).reshape(DEPTH, G * P, C), (1, 1, LANES // C))
    nat = lambda a: a.reshape(DEPTH, SLABS, 1, SLAB_STATE)
    args = (rows_b(lam_re), rows_b(lam_im), rows_b(dt3), bt(b_re), bt(b_im),
            rows_c(lam_re), rows_c(lam_im), rows_c(dt3), ct(c_re), ct(c_im),
            nat(lam_re), nat(lam_im), nat(dt3))
    bspec = pl.BlockSpec((1, LANES, P), lambda l, k: (l, k, 0))
    cspec = pl.BlockSpec((1, SLAB_STATE, LANES), lambda l, k: (l, k, 0))
    nspec = pl.BlockSpec((1, 1, 1, SLAB_STATE), lambda l, k: (l, k, 0, 0))
    return pl.pallas_call(
        functools.partial(_s5_prep_kernel, seg=seg),
        out_shape=(jax.ShapeDtypeStruct((DEPTH, SLABS, S5_T * LANES, W2), BF16),
                   jax.ShapeDtypeStruct((DEPTH, SLABS, LANES, W2), F32),
                   jax.ShapeDtypeStruct((DEPTH, SLABS, S5_T + 1, W2, LANES), F32),
                   jax.ShapeDtypeStruct((DEPTH, SLABS, W2, S5_T * LANES), BF16),
                   jax.ShapeDtypeStruct((DEPTH, SLABS, seg + 1, W2), F32),
                   jax.ShapeDtypeStruct((DEPTH, SLABS, 1, W2), F32)),
        grid=(DEPTH, SLABS),
        in_specs=[bspec] * 5 + [cspec] * 5 + [nspec] * 3,
        out_specs=(pl.BlockSpec((1, 1, S5_T * LANES, W2), lambda l, k: (l, k, 0, 0)),
                   pl.BlockSpec((1, 1, LANES, W2), lambda l, k: (l, k, 0, 0)),
                   pl.BlockSpec((1, 1, S5_T + 1, W2, LANES), lambda l, k: (l, k, 0, 0, 0)),
                   pl.BlockSpec((1, 1, W2, S5_T * LANES), lambda l, k: (l, k, 0, 0)),
                   pl.BlockSpec((1, 1, seg + 1, W2), lambda l, k: (l, k, 0, 0)),
                   pl.BlockSpec((1, 1, 1, W2), lambda l, k: (l, k, 0, 0))),
        compiler_params=_cp(("parallel", "parallel")),
        name="s5_discretize",
    )(*args)


def _toep_kernel(b_ref, c_ref, o_ref):
    dd = pl.program_id(2)
    bst = b_ref[0, 0]
    lag = lambda d: _dot(bst, c_ref[0, 0, d], HI)
    k0 = lag(2 * dd)
    o_ref[0, 0, 0, 0:LANES, 0:LANES] = k0.astype(BF16)
    o_ref[0, 0, 0, LANES:, LANES:] = k0.astype(BF16)
    o_ref[0, 0, 0, 0:LANES, LANES:] = lag(2 * dd + 1).astype(BF16)
    km = lag(jnp.maximum(2 * dd - 1, 0))
    o_ref[0, 0, 0, LANES:, 0:LANES] = jnp.where(dd > 0, km, 0.0).astype(BF16)


def _toep(bst, cpe):
    W2 = 2 * SLAB_STATE
    return pl.pallas_call(
        _toep_kernel,
        out_shape=jax.ShapeDtypeStruct((DEPTH, SLABS, S5_T // 2, 2 * LANES, 2 * LANES), BF16),
        grid=(DEPTH, SLABS, S5_T // 2),
        in_specs=[pl.BlockSpec((1, 1, LANES, W2), lambda l, k, d: (l, k, 0, 0)),
                  pl.BlockSpec((1, 1, S5_T + 1, W2, LANES), lambda l, k, d: (l, k, 0, 0, 0))],
        out_specs=pl.BlockSpec((1, 1, 1, 2 * LANES, 2 * LANES), lambda l, k, d: (l, k, d, 0, 0)),
        compiler_params=_cp(("parallel", "parallel", "parallel")),
        name="s5_conv_blocks",
    )(bst, cpe)


def _s5_seq_kernel(up_ref, be_ref, tp_ref, cpm_ref, pt_ref, s0_ref, dsk_ref,
                   yg_ref, sfin_ref, e_scr, sx_scr, *, nc):
    seg = nc // 8
    W = SLAB_STATE
    nt = W // LANES
    ub = up_ref[0, 0]
    u = ub.astype(F32)
    e = _dot(ub, be_ref[0, 0])
    pitch = e_scr.shape[1] // 8
    for c in range(2 * nt):
        for j in range(8):
            e_scr[c, j * pitch:j * pitch + seg, :] = e[j * seg:(j + 1) * seg, c * LANES:(c + 1) * LANES]

    def tiles(row):
        return [(row[:, c * LANES:(c + 1) * LANES], row[:, W + c * LANES:W + (c + 1) * LANES])
                for c in range(nt)]

    a8 = [(jnp.broadcast_to(r, (8, LANES)), jnp.broadcast_to(i, (8, LANES)))
          for r, i in tiles(pt_ref[0, 0, 1:2, :])]

    def step(i, carry):
        rows = pl.ds(i, 8, stride=pitch)
        new = []
        for c in range(nt):
            sr, si = carry[c]
            ar, ai = a8[c]
            sx_scr[c, rows, :] = sr
            sx_scr[nt + c, rows, :] = si
            new.append((ar * sr - ai * si + e_scr[c, rows, :],
                        ar * si + ai * sr + e_scr[nt + c, rows, :]))
        return tuple(new)

    zero = jnp.zeros((8, LANES), F32)
    ends = lax.fori_loop(0, seg, step, tuple((zero, zero) for _ in range(nt)))

    al = tiles(pt_ref[0, 0, seg:seg + 1, :])
    cur = tiles(s0_ref[0, 0])
    car = []
    for c in range(nt):
        alr, ali = al[c]
        cr, ci = cur[c]
        sr, si = ends[c]
        crs, cis = [], []
        for j in range(8):
            crs.append(cr)
            cis.append(ci)
            cr, ci = (alr * cr - ali * ci + sr[j:j + 1], alr * ci + ali * cr + si[j:j + 1])
        sfin_ref[0, 0, :, c * LANES:(c + 1) * LANES] = cr
        sfin_ref[0, 0, :, W + c * LANES:W + (c + 1) * LANES] = ci
        car.append((jnp.concatenate(crs, axis=0), jnp.concatenate(cis, axis=0)))

    def corr(i, _):
        rows = pl.ds(i, 8, stride=pitch)
        pw = tiles(pt_ref[0, 0, pl.ds(i, 1), :])
        for c in range(nt):
            pr, pi = pw[c]
            cr, ci = car[c]
            sx_scr[c, rows, :] = sx_scr[c, rows, :] + (pr * cr - pi * ci)
            sx_scr[nt + c, rows, :] = sx_scr[nt + c, rows, :] + (pr * ci + pi * cr)
        return 0

    lax.fori_loop(0, seg, corr, 0)

    sx = jnp.concatenate(
        [jnp.concatenate([sx_scr[c, j * pitch:j * pitch + seg, :] for j in range(8)], axis=0)
         for c in range(2 * nt)], axis=-1)
    y = _dot(sx.astype(BF16), cpm_ref[0, 0])
    TW = 2 * LANES
    for tq in range(S5_T // 2):
        acc = y[:, tq * TW:(tq + 1) * TW]
        for tpi in range(tq + 1):
            acc = acc + _dot(ub[:, tpi * TW:(tpi + 1) * TW], tp_ref[0, 0, tq - tpi])
        acc = acc + dsk_ref[0, :, tq * TW:(tq + 1) * TW] * u[:, tq * TW:(tq + 1) * TW]
        yg_ref[0, 0, :, tq * TW:(tq + 1) * TW] = jax.nn.gelu(acc).astype(yg_ref.dtype)


def _s5_seq(up, be_emb, tp, cpm, pt, s0, dsk, l):
    _, B, nc, _ = up.shape
    seg = nc // 8
    W2 = 2 * SLAB_STATE
    yg, sfin = pl.pallas_call(
        functools.partial(_s5_seq_kernel, nc=nc),
        out_shape=(jax.ShapeDtypeStruct((SLABS, B, nc, S5_T * LANES), BF16),
                   jax.ShapeDtypeStruct((SLABS, B, 1, W2), F32)),
        grid=(SLABS, B),
        in_specs=[pl.BlockSpec((1, 1, nc, S5_T * LANES), lambda k, b: (k, b, 0, 0)),
                  pl.BlockSpec((1, 1, S5_T * LANES, W2), lambda k, b: (l, k, 0, 0)),
                  pl.BlockSpec((1, 1, S5_T // 2, 2 * LANES, 2 * LANES), lambda k, b: (l, k, 0, 0, 0)),
                  pl.BlockSpec((1, 1, W2, S5_T * LANES), lambda k, b: (l, k, 0, 0)),
                  pl.BlockSpec((1, 1, seg + 1, W2), lambda k, b: (l, k, 0, 0)),
                  pl.BlockSpec((1, 1, 1, W2), lambda k, b: (k, b, 0, 0)),
                  pl.BlockSpec((1, 1, S5_T * LANES), lambda k, b: (k, 0, 0))],
        out_specs=(pl.BlockSpec((1, 1, nc, S5_T * LANES), lambda k, b: (k, b, 0, 0)),
                   pl.BlockSpec((1, 1, 1, W2), lambda k, b: (k, b, 0, 0))),
        scratch_shapes=[pltpu.VMEM((W2 // LANES, 8 * (seg + S5_SEG_PAD), LANES), F32),
                        pltpu.VMEM((W2 // LANES, 8 * (seg + S5_SEG_PAD), LANES), F32)],
        compiler_params=_cp(("parallel", "parallel")),
        name="s5_seq",
    )(up, be_emb, tp, cpm, pt, s0, dsk)
    return yg, sfin


def _s5_step_kernel(u_ref, b_ref, c_ref, a_ref, s0_ref, d_ref, yg_ref, s1_ref):
    W = SLAB_STATE
    u = u_ref[0]
    bu = _dot(u, b_ref[0, 0], HI)
    ar = a_ref[0, 0, :, 0:W]
    ai = a_ref[0, 0, :, W:2 * W]
    sr = s0_ref[0, :, 0:W]
    si = s0_ref[0, :, W:2 * W]
    nr = ar * sr - ai * si + bu[:, 0:W]
    ni = ar * si + ai * sr + bu[:, W:2 * W]
    s1_ref[0, :, 0:W] = nr
    s1_ref[0, :, W:2 * W] = ni
    s1 = jnp.concatenate([nr, ni], axis=-1)
    y = _dot(s1, c_ref[0, 0, 0], HI) + d_ref[0] * u
    yg_ref[0] = jax.nn.gelu(y)


def _s5_step(u_slab, bst, cpe, a1, s0, d1, l):
    _, N, _ = u_slab.shape
    W2 = 2 * SLAB_STATE
    return pl.pallas_call(
        _s5_step_kernel,
        out_shape=(jax.ShapeDtypeStruct((SLABS, N, LANES), F32),
                   jax.ShapeDtypeStruct((SLABS, N, W2), F32)),
        grid=(SLABS,),
        in_specs=[pl.BlockSpec((1, N, LANES), lambda k: (k, 0, 0)),
                  pl.BlockSpec((1, 1, LANES, W2), lambda k: (l, k, 0, 0)),
                  pl.BlockSpec((1, 1, 1, W2, LANES), lambda k: (l, k, 0, 0, 0)),
                  pl.BlockSpec((1, 1, 1, W2), lambda k: (l, k, 0, 0)),
                  pl.BlockSpec((1, N, W2), lambda k: (k, 0, 0)),
                  pl.BlockSpec((1, 1, LANES), lambda k: (k, 0, 0))],
        out_specs=(pl.BlockSpec((1, N, LANES), lambda k: (k, 0, 0)),
                   pl.BlockSpec((1, N, W2), lambda k: (k, 0, 0))),
        compiler_params=_cp(("parallel",)),
        name="s5_step",
    )(u_slab, bst, cpe, a1, s0, d1)


def _l2n(x):
    return x * lax.rsqrt(jnp.sum(x * x, axis=-1, keepdims=True) + L2_EPS)


def _split_bf16(x):
    hi = x.astype(BF16)
    return hi, (x - hi.astype(F32)).astype(BF16)


def _unit_lower_solve(As, rhss):
    n = GDN_C
    row = lax.broadcasted_iota(jnp.int32, (n, n), 0)
    col = lax.broadcasted_iota(jnp.int32, (n, n), 1)
    eye = (row == col).astype(F32)
    same8 = (row // 8) == (col // 8)
    Qs = [jnp.where(same8, -A, 0.0) for A in As]
    invs = [eye + Q for Q in Qs]
    for _ in range(2):
        Qs = [_dotb(Q, Q) for Q in Qs]
        invs = [inv + _dotb(inv, Q) for inv, Q in zip(invs, Qs)]
    s = 8
    while s < n:
        sib = ((row // (2 * s)) == (col // (2 * s))) & ((row // s) != (col // s))
        offs = [jnp.where(sib, A, 0.0).astype(BF16) for A in As]
        invb = [inv.astype(BF16) for inv in invs]
        tmp = [_dot(off, ib) for off, ib in zip(offs, invb)]
        invs = [inv - _dot(ib, t.astype(BF16)) for inv, ib, t in zip(invs, invb, tmp)]
        s *= 2
    invb = [inv.astype(BF16) for inv in invs]
    x0s = [_dot(ib, rhs.astype(BF16)) for ib, rhs in zip(invb, rhss)]
    res = []
    for A, x0, rhs in zip(As, x0s, rhss):
        ah, al = _split_bf16(A)
        xh, xl = _split_bf16(x0)
        res.append(rhs - x0 - (_dot(ah, xh) + _dot(ah, xl) + _dot(al, xh)))
    return [x0 + _dot(ib, r.astype(BF16)) for x0, ib, r in zip(x0s, invb, res)]


def _gdn_tile(qc_scr, gc, beta, z_ref, nw, o_ref, s_scr, tl):
    C, DK, H = GDN_C, GDN_DK, GDN_HEADS
    nchunk = tl // C
    probs = [(c, h) for c in range(nchunk) for h in range(H)]
    row = lax.broadcasted_iota(jnp.int32, (C, C), 0)
    col = lax.broadcasted_iota(jnp.int32, (C, C), 1)
    tri = row >= col
    strict = row > col

    def blk(c, off):
        return qc_scr[c * C:(c + 1) * C, off:off + DK]

    q = [_l2n(blk(c, h * DK)) * (DK ** -0.5) for c, h in probs]
    k = [_l2n(blk(c, GDN_WIDTH + h * DK)) for c, h in probs]
    v = [blk(c, 2 * GDN_WIDTH + h * DK) for c, h in probs]
    gcb = [jnp.broadcast_to(gc[c * C:(c + 1) * C, h:h + 1], (C, DK)) for c, h in probs]
    bb = [jnp.broadcast_to(beta[c * C:(c + 1) * C, H + h:H + h + 1], (C, DK)) for c, h in probs]
    decay = []
    for g in gcb:
        diff = g - g.T
        decay.append(jnp.where(tri, jnp.exp(jnp.where(tri, diff, 0.0)), 0.0))
    kbf = [x.astype(BF16) for x in k]
    kb = [x * b for x, b in zip(k, bb)]
    A = [jnp.where(strict, _dot_nt(x.astype(BF16), y) * d, 0.0) for x, y, d in zip(kb, kbf, decay)]
    egc = [jnp.exp(g) for g in gcb]
    rhs = [jnp.concatenate([x * b, y * e], axis=-1) for x, b, y, e in zip(v, bb, kb, egc)]
    sol = _unit_lower_solve(A, rhs)
    attn = [jnp.where(tri, _dot_nt(x.astype(BF16), y) * d, 0.0).astype(BF16)
            for x, y, d in zip(q, kbf, decay)]
    glast = [g[C - 1:C, :] for g in gcb]
    wq = [jnp.concatenate([s[:, DK:], x * e], axis=0).astype(BF16) for s, x, e in zip(sol, q, egc)]
    kg = [(x * jnp.exp(gl - g)).astype(BF16) for x, gl, g in zip(k, glast, gcb)]

    for c in range(nchunk):
        ps = [c * H + h for h in range(H)]
        S = [s_scr[h] for h in range(H)]
        ws = [_dot(wq[p], S[h].astype(BF16)) for h, p in enumerate(ps)]
        v_new = [sol[p][:, 0:DK] - w[0:C] for p, w in zip(ps, ws)]
        vb = [x.astype(BF16) for x in v_new]
        o = [w[C:] + _dot(attn[p], x) for p, w, x in zip(ps, ws, vb)]
        for h, p in enumerate(ps):
            s_scr[h] = S[h] * jnp.exp(glast[p]) + _dot_tn(kg[p], vb[h])
            zh = z_ref[0, c * C:(c + 1) * C, h * DK:(h + 1) * DK].astype(F32)
            on = o[h] * lax.rsqrt(jnp.mean(o[h] * o[h], axis=-1, keepdims=True) + NORM_EPS) * nw
            o_ref[0, c * C:(c + 1) * C, h * DK:(h + 1) * DK] = (on * _silu(zh)).astype(o_ref.dtype)


def _gdn_seq_kernel(qkv_ref, z_ref, ab_ref, cw_ref, alog_ref, dtb_ref, nw_ref, conv0_ref, s0_ref,
                    o_ref, sfin_ref, xp_scr, qc_scr, s_scr, *, tl):
    lt = pl.program_id(1)

    @pl.when(lt == 0)
    def _():
        xp_scr[0:8, :] = jnp.zeros((8, QKV_WIDTH), F32)
        xp_scr[8 - (GDN_CONV - 1):8, :] = conv0_ref[0]
        s_scr[...] = s0_ref[0]

    xp_scr[8:8 + tl, :] = qkv_ref[0].astype(F32)
    conv = cw_ref[0, 0:1, :] * xp_scr[5:5 + tl, :]
    for j in range(1, GDN_CONV):
        conv = conv + cw_ref[0, j:j + 1, :] * xp_scr[5 + j:5 + j + tl, :]
    xp_scr[0:8, :] = xp_scr[tl:tl + 8, :]
    qc_scr[...] = _silu(conv)

    ab = ab_ref[0]
    g = -jnp.exp(alog_ref[...]) * jax.nn.softplus(ab + dtb_ref[...])
    beta = jax.nn.sigmoid(ab)
    row = lax.broadcasted_iota(jnp.int32, (tl, tl), 0)
    col = lax.broadcasted_iota(jnp.int32, (tl, tl), 1)
    csum = ((row >= col) & ((row // GDN_C) == (col // GDN_C))).astype(F32)
    gc = _dot(csum, g, HI)
    _gdn_tile(qc_scr, gc, beta, z_ref, nw_ref[...], o_ref, s_scr, tl)

    @pl.when(lt == pl.num_programs(1) - 1)
    def _():
        sfin_ref[0] = s_scr[...]


def _gdn_seq(qkv, z, ab, conv_w, alog, dtb, nw, conv0, s0, l):
    B, L, _ = qkv.shape
    tl = min(256, L)
    return pl.pallas_call(
        functools.partial(_gdn_seq_kernel, tl=tl),
        out_shape=(jax.ShapeDtypeStruct((B, L, GDN_WIDTH), BF16),
                   jax.ShapeDtypeStruct((B, GDN_HEADS, GDN_DK, GDN_DK), F32)),
        grid=(B, L // tl),
        in_specs=[pl.BlockSpec((1, tl, QKV_WIDTH), lambda b, i: (b, i, 0)),
                  pl.BlockSpec((1, tl, GDN_WIDTH), lambda b, i: (b, i, 0)),
                  pl.BlockSpec((1, tl, LANES), lambda b, i: (b, i, 0)),
                  pl.BlockSpec((1, GDN_CONV, QKV_WIDTH), lambda b, i: (l, 0, 0)),
                  pl.BlockSpec((1, LANES), lambda b, i: (0, 0)),
                  pl.BlockSpec((1, LANES), lambda b, i: (0, 0)),
                  pl.BlockSpec((1, GDN_DK), lambda b, i: (0, 0)),
                  pl.BlockSpec((1, GDN_CONV - 1, QKV_WIDTH), lambda b, i: (b, 0, 0)),
                  pl.BlockSpec((1, GDN_HEADS, GDN_DK, GDN_DK), lambda b, i: (b, 0, 0, 0))],
        out_specs=(pl.BlockSpec((1, tl, GDN_WIDTH), lambda b, i: (b, i, 0)),
                   pl.BlockSpec((1, GDN_HEADS, GDN_DK, GDN_DK), lambda b, i: (b, 0, 0, 0))),
        scratch_shapes=[pltpu.VMEM((tl + 8, QKV_WIDTH), F32),
                        pltpu.VMEM((tl, QKV_WIDTH), F32),
                        pltpu.VMEM((GDN_HEADS, GDN_DK, GDN_DK), F32)],
        compiler_params=_cp(("parallel", "arbitrary")),
        name="gdn_seq",
    )(qkv, z, ab, conv_w, alog, dtb, nw, conv0, s0)


GDN_STEP_ROWS = 8


def _gdn_step_kernel(qkv_ref, z_ref, ab_ref, cw_ref, alog_ref, dtb_ref, nw_ref, conv0_ref, s0_ref,
                     *rest):
    if len(rest) == 3:
        prev_ref, o_ref, s1_all = rest
        s1_all[0] = prev_ref[...]
        s1_ref = s1_all.at[1]
    else:
        o_ref, s1_all = rest
        s1_ref = s1_all
    nb = GDN_STEP_ROWS
    W = QKV_WIDTH
    conv = cw_ref[0:1, :] * conv0_ref[:, 0:W]
    conv = conv + cw_ref[1:2, :] * conv0_ref[:, W:2 * W]
    conv = conv + cw_ref[2:3, :] * conv0_ref[:, 2 * W:3 * W]
    conv = conv + cw_ref[3:4, :] * qkv_ref[...]
    qc = _silu(conv)
    ab = ab_ref[...]
    eg = jnp.exp(-jnp.exp(alog_ref[...]) * jax.nn.softplus(ab + dtb_ref[...]))
    beta = jax.nn.sigmoid(ab)
    eye = (lax.broadcasted_iota(jnp.int32, (GDN_DK, GDN_DK), 0)
           == lax.broadcasted_iota(jnp.int32, (GDN_DK, GDN_DK), 1)).astype(F32)
    for h in range(GDN_HEADS):
        q = _l2n(qc[:, h * GDN_DK:(h + 1) * GDN_DK]) * (GDN_DK ** -0.5)
        k = _l2n(qc[:, GDN_WIDTH + h * GDN_DK:GDN_WIDTH + (h + 1) * GDN_DK])
        v = qc[:, 2 * GDN_WIDTH + h * GDN_DK:2 * GDN_WIDTH + (h + 1) * GDN_DK]
        kT = _dot_nt(eye, k, HI)
        qT = _dot_nt(eye, q, HI)
        qk = jnp.sum(q * k, axis=-1, keepdims=True)
        for j in range(nb):
            S = s0_ref[0, j, h]
            kc = jnp.broadcast_to(kT[:, j:j + 1], (GDN_DK, GDN_DK))
            qcb = jnp.broadcast_to(qT[:, j:j + 1], (GDN_DK, GDN_DK))
            kS = jnp.sum(kc * S, axis=0, keepdims=True)
            qS = jnp.sum(qcb * S, axis=0, keepdims=True)
            egj = eg[j:j + 1, h:h + 1]
            bj = beta[j:j + 1, GDN_HEADS + h:GDN_HEADS + h + 1]
            v_new = bj * v[j:j + 1, :] - (bj * egj) * kS
            o = egj * qS + qk[j:j + 1, :] * v_new
            s1_ref[j, h] = S * egj + kc * v_new
            zh = z_ref[j:j + 1, h * GDN_DK:(h + 1) * GDN_DK]
            on = o * lax.rsqrt(jnp.mean(o * o, axis=-1, keepdims=True) + NORM_EPS) * nw_ref[...]
            o_ref[j:j + 1, h * GDN_DK:(h + 1) * GDN_DK] = on * _silu(zh)


def _gdn_step(qkv, z, ab, conv_w, alog, dtb, nw, conv0, s_all, l, prev):
    N = qkv.shape[0]
    nb = GDN_STEP_ROWS
    row = lambda w: pl.BlockSpec((nb, w), lambda i: (i, 0))
    const = lambda r, w: pl.BlockSpec((r, w), lambda i: (0, 0))
    sblk = (nb, GDN_HEADS, GDN_DK, GDN_DK)
    one = pl.BlockSpec(sblk, lambda i: (i, 0, 0, 0))
    ins = [qkv, z, ab, conv_w, alog, dtb, nw, conv0, s_all]
    in_specs = [row(QKV_WIDTH), row(GDN_WIDTH), row(LANES), const(GDN_CONV, QKV_WIDTH),
                const(1, LANES), const(1, LANES), const(1, GDN_DK), row(3 * QKV_WIDTH),
                pl.BlockSpec((1,) + sblk, lambda i: (l, i, 0, 0, 0))]
    if prev is None:
        s_shape, s_spec = jax.ShapeDtypeStruct((N,) + sblk[1:], F32), one
    else:
        assert DEPTH == 2 and l == 1
        ins.append(prev)
        in_specs.append(one)
        s_shape = jax.ShapeDtypeStruct((DEPTH, N) + sblk[1:], F32)
        s_spec = pl.BlockSpec((DEPTH,) + sblk, lambda i: (0, i, 0, 0, 0))
    return pl.pallas_call(
        _gdn_step_kernel,
        out_shape=(jax.ShapeDtypeStruct((N, GDN_WIDTH), F32), s_shape),
        grid=(N // nb,),
        in_specs=in_specs,
        out_specs=(row(GDN_WIDTH), s_spec),
        compiler_params=_cp(("parallel",)),
        name="gdn_step",
    )(*ins)


def _merge_kernel(yg_ref, og_ref, ga_ref, gb_ref, x_ref, gt_ref, wglu_ref, wgo_ref, wout_ref,
                  gf_ref, scf_ref, shf_ref, wr_ref,
                  xo_ref, h_ref, lg_ref, *scr, hi, chunked):
    if chunked:
        y_scr = scr[-1]
        scr = scr[:-1]
        nrow = y_scr.shape[1] // S5_T
        for k in range(SLABS):
            for t in range(S5_T):
                y_scr[k, pl.ds(t, nrow, stride=S5_T), :] = (
                    yg_ref[k, 0, :, t * LANES:(t + 1) * LANES].astype(F32))
        y = jnp.concatenate([y_scr[k] for k in range(SLABS)], axis=-1)
    else:
        y = jnp.concatenate([yg_ref[k, 0] for k in range(SLABS)], axis=-1)
    if hi:
        wglu, wgo, wout = wglu_ref[0], wgo_ref[0], wout_ref[0]
        mm = lambda a, w: _dot(a, w, HI)
    else:
        wglu_s, wgo_s, wout_s = scr

        @pl.when((pl.program_id(0) == 0) & (pl.program_id(1) == 0))
        def _():
            wglu_s[...] = wglu_ref[0].astype(BF16)
            wgo_s[...] = wgo_ref[0].astype(BF16)
            wout_s[...] = wout_ref[0].astype(BF16)

        wglu, wgo, wout = wglu_s[...], wgo_s[...], wout_s[...]
        mm = lambda a, w: _dot(a.astype(BF16), w)

    glu = mm(y, wglu)
    branch_a = glu[:, 0:D_MODEL] * jax.nn.sigmoid(glu[:, D_MODEL:])
    branch_b = mm(og_ref[0], wgo)
    merged = ga_ref[0].astype(F32) * branch_a + gb_ref[0].astype(F32) * branch_b
    out = mm(merged, wout)
    x = x_ref[0] + gt_ref[0, 0] * out
    xo_ref[0] = x
    ms = jnp.mean(x * x, axis=-1, keepdims=True)
    h = x * lax.rsqrt(ms + NORM_EPS) * gf_ref[0]
    h = h * (1.0 + scf_ref[0, 0]) + shf_ref[0, 0]
    h_ref[0] = h.astype(h_ref.dtype)
    lg_ref[0] = _dot_nt(wr_ref[0], h, HI)


def _merge(yg, og, ga, gb, x, mod, wglu, wgo, wout, gf, wr, *, l, tm, hi, chunked, h_dtype):
    B, L, D = x.shape
    row = lambda w: pl.BlockSpec((1, tm, w), lambda b, i: (b, i, 0))
    layer = lambda r, w, ll=l: pl.BlockSpec((1, r, w), lambda b, i: (ll, 0, 0))
    scratch = [] if hi else [pltpu.VMEM((S5_WIDTH, 2 * D), BF16), pltpu.VMEM((GDN_WIDTH, D), BF16),
                             pltpu.VMEM((D, D), BF16)]
    if chunked:
        scratch = scratch + [pltpu.VMEM((SLABS, tm, LANES), F32)]
        yg_spec = pl.BlockSpec((SLABS, 1, tm // S5_T, S5_T * LANES), lambda b, i: (0, b, i, 0))
    else:
        yg_spec = pl.BlockSpec((SLABS, 1, tm, LANES), lambda b, i: (0, b, i, 0))
    lg_shape = jax.ShapeDtypeStruct((B, N_EXPERTS, L), F32)
    lg_spec = pl.BlockSpec((1, N_EXPERTS, tm), lambda b, i: (b, 0, i))
    return pl.pallas_call(
        functools.partial(_merge_kernel, hi=hi, chunked=chunked),
        out_shape=(jax.ShapeDtypeStruct((B, L, D), F32),
                   jax.ShapeDtypeStruct((B, L, D), h_dtype),
                   lg_shape),
        grid=(B, L // tm),
        in_specs=[yg_spec,
                  row(GDN_WIDTH), row(D), row(D), row(D), _mod_spec(mod, l, 2, tm),
                  layer(S5_WIDTH, 2 * D), layer(GDN_WIDTH, D), layer(D, D),
                  layer(1, D), _mod_spec(mod, l, 4, tm), _mod_spec(mod, l, 3, tm),
                  layer(N_EXPERTS, D, l // 2)],
        out_specs=(row(D), row(D), lg_spec),
        scratch_shapes=scratch,
        compiler_params=_cp(("arbitrary", "arbitrary")),
        name="merge_out_proj",
    )(yg, og, ga, gb, x, mod, wglu, wgo, wout, gf, mod, mod, wr)


FF_TILE = 512
FFN_ROWS = 512


def _finish(x, gfin_ref, final):
    if not final:
        return x
    ms = jnp.mean(x * x, axis=-1, keepdims=True)
    return x * lax.rsqrt(ms + NORM_EPS) * gfin_ref[...]


def _ffn_kernel(h_ref, x_ref, gt_ref, wg_ref, wu_ref, wd_ref, gfin_ref, o_ref, acc_scr, *, hi, final):
    j = pl.program_id(2)
    if hi:
        wg, wu, wd = wg_ref[...], wu_ref[...], wd_ref[...]
        mm = lambda a, w: _dot(a, w, HI)
    else:
        wg, wu, wd = wg_ref[...].astype(BF16), wu_ref[...].astype(BF16), wd_ref[...].astype(BF16)
        mm = lambda a, w: _dot(a.astype(BF16), w)

    @pl.when(j == 0)
    def _():
        acc_scr[...] = jnp.zeros_like(acc_scr)

    tm = acc_scr.shape[0]
    sub = min(FFN_ROWS, tm)
    for s in range(tm // sub):
        rows = slice(s * sub, (s + 1) * sub)
        hb = h_ref[0, rows, :]
        act = _silu(mm(hb, wg)) * mm(hb, wu)
        acc_scr[rows, :] = acc_scr[rows, :] + mm(act, wd)

    @pl.when(j == pl.num_programs(2) - 1)
    def _():
        o_ref[0] = _finish(x_ref[0] + gt_ref[0, 0] * acc_scr[...], gfin_ref, final)


def _ffn(h, x, mod, w_gu, w_down, gfin, *, l, tm, hi, final):
    B, L, D = x.shape
    nj = D_FF // FF_TILE
    row = pl.BlockSpec((1, tm, D), lambda b, i, j: (b, i, 0))
    return pl.pallas_call(
        functools.partial(_ffn_kernel, hi=hi, final=final),
        out_shape=jax.ShapeDtypeStruct((B, L, D), F32),
        grid=(B, L // tm, nj),
        in_specs=[row, row, _mod_spec(mod, l, 5, tm),
                  pl.BlockSpec((D, FF_TILE), lambda b, i, j: (0, j)),
                  pl.BlockSpec((D, FF_TILE), lambda b, i, j: (0, nj + j)),
                  pl.BlockSpec((FF_TILE, D), lambda b, i, j: (j, 0)),
                  pl.BlockSpec((1, D), lambda b, i, j: (0, 0))],
        out_specs=row,
        scratch_shapes=[pltpu.VMEM((tm, D), F32)],
        compiler_params=_cp(("parallel", "parallel", "arbitrary")),
        name="ffn_dense",
    )(h, x, mod, w_gu, w_gu, w_down, gfin)


ROUTE_TM = 512
ROW_DMA_TM = 256
MOE_SUP = 2048
MOE_SUB = 512
MOE_FF_TILE = 512


def _route_kernel(lg_ref, br_ref, cnt0_ref, slot_ref, wt_ref, cnt_ref, carry_scr, *, cap):
    @pl.when((pl.program_id(0) == 0) & (pl.program_id(1) == 0))
    def _():
        carry_scr[...] = cnt0_ref[...]

    lg = lg_ref[0] + br_ref[...]
    tm = lg.shape[1]
    eidx = lax.broadcasted_iota(jnp.int32, lg.shape, 0)
    m1 = jnp.max(lg, axis=0, keepdims=True)
    i1 = jnp.min(jnp.where(lg == m1, eidx, N_EXPERTS), axis=0, keepdims=True)
    lg2 = jnp.where(eidx == i1, -jnp.inf, lg)
    m2 = jnp.max(lg2, axis=0, keepdims=True)
    i2 = jnp.min(jnp.where(lg2 == m2, eidx, N_EXPERTS), axis=0, keepdims=True)
    e2 = jnp.exp(m2 - m1)
    wt_ref[0, 0:1, :] = 1.0 / (1.0 + e2)
    wt_ref[0, 1:2, :] = e2 / (1.0 + e2)
    sel1 = eidx == i1
    sel2 = eidx == i2
    oh = jnp.where(sel1 | sel2, 1.0, 0.0)
    before = (lax.broadcasted_iota(jnp.int32, (tm, tm), 0)
              < lax.broadcasted_iota(jnp.int32, (tm, tm), 1)).astype(BF16)
    rank = carry_scr[:, 0:1] + _dot(oh.astype(BF16), before)
    r1 = jnp.sum(jnp.where(sel1, rank, 0.0), axis=0, keepdims=True).astype(jnp.int32)
    r2 = jnp.sum(jnp.where(sel2, rank, 0.0), axis=0, keepdims=True).astype(jnp.int32)
    slot_ref[0, 0:1, :] = i1 * cap + r1
    slot_ref[0, 1:2, :] = i2 * cap + r2
    carry_scr[...] = carry_scr[...] + jnp.sum(oh, axis=1, keepdims=True)
    cnt_ref[...] = carry_scr[...]


def _route_slots(lgT, b_r, cnt0, cap):
    B, E, L = lgT.shape
    tm = min(ROUTE_TM, L)
    return pl.pallas_call(
        functools.partial(_route_kernel, cap=cap),
        out_shape=(jax.ShapeDtypeStruct((B, 2, L), jnp.int32),
                   jax.ShapeDtypeStruct((B, 2, L), F32),
                   jax.ShapeDtypeStruct((E, LANES), F32)),
        grid=(B, L // tm),
        in_specs=[pl.BlockSpec((1, E, tm), lambda b, i: (b, 0, i)),
                  pl.BlockSpec((E, 1), lambda b, i: (0, 0)),
                  pl.BlockSpec((E, LANES), lambda b, i: (0, 0))],
        out_specs=(pl.BlockSpec((1, 2, tm), lambda b, i: (b, 0, i)),
                   pl.BlockSpec((1, 2, tm), lambda b, i: (b, 0, i)),
                   pl.BlockSpec((E, LANES), lambda b, i: (0, 0))),
        scratch_shapes=[pltpu.VMEM((E, LANES), F32)],
        compiler_params=_cp(("arbitrary", "arbitrary")),
        name="moe_route",
    )(lgT, b_r.reshape(E, 1), cnt0)


def _row_copy(src, dst, sem):
    return pltpu.make_async_copy(src, dst, sem)


def _slot_rows_kernel(start_ref, per_ref, code_ref, row_ref, *, cap):
    code = code_ref[...]
    shift = cap.bit_length() - 1
    e = lax.shift_right_logical(code, shift)
    r = code & (cap - 1)
    start = jnp.zeros_like(code)
    per = jnp.ones_like(code)
    for k in range(N_EXPERTS):
        start = jnp.where(e == k, start_ref[k], start)
        per = jnp.where(e == k, per_ref[k], per)
    q = jnp.floor((r.astype(F32) + 0.5) / per.astype(F32)).astype(jnp.int32)
    row_ref[...] = start + q * MOE_SUP + (r - q * per)


def _slot_rows(start, per, codes, cap):
    B, _, L = codes.shape
    tm = min(ROUTE_TM, L)
    spec = pl.BlockSpec((1, 2, tm), lambda b, i, st, pe: (b, 0, i))
    return pl.pallas_call(
        functools.partial(_slot_rows_kernel, cap=cap),
        out_shape=jax.ShapeDtypeStruct(codes.shape, jnp.int32),
        grid_spec=pltpu.PrefetchScalarGridSpec(
            num_scalar_prefetch=2, grid=(B, L // tm), in_specs=[spec], out_specs=spec),
        compiler_params=_cp(("parallel", "parallel")),
        name="moe_slot_rows",
    )(start, per, codes)


def _zeros_kernel(o_ref):
    o_ref[...] = jnp.zeros_like(o_ref)


def _zero_rows(n_rows, width):
    return pl.pallas_call(
        _zeros_kernel,
        out_shape=jax.ShapeDtypeStruct((n_rows, width), F32),
        grid=(n_rows // MOE_SUP,),
        out_specs=pl.BlockSpec((MOE_SUP, width), lambda i: (i, 0)),
        compiler_params=_cp(("parallel",)),
        name="moe_zero_rows",
    )()


def _dispatch_kernel(row_ref, h_ref, xs_in_ref, xs_ref, hbuf, sem):
    del xs_in_ref
    tm = h_ref.shape[1]
    t = pl.program_id(0) * pl.num_programs(1) + pl.program_id(1)
    last = pl.num_programs(0) * pl.num_programs(1) - 1
    slot = t % 2
    hbuf[slot] = h_ref[0]

    def issue(r, _):
        for k in range(2):
            row = row_ref[0, k, r]
            _row_copy(hbuf.at[slot, pl.ds(r, 1), :], xs_ref.at[pl.ds(row, 1), :], sem.at[slot]).start()
        return 0

    lax.fori_loop(0, tm, issue, 0, unroll=8)

    def drain(sl):
        for k in range(2):
            _row_copy(hbuf.at[sl], xs_ref.at[pl.ds(0, tm), :], sem.at[sl]).wait()

    @pl.when(t > 0)
    def _():
        drain(1 - slot)

    @pl.when(t == last)
    def _():
        drain(slot)


def _dispatch(rows, h, xs):
    B, L, D = h.shape
    n_rows = xs.shape[0]
    tm = min(ROW_DMA_TM, L)
    return pl.pallas_call(
        _dispatch_kernel,
        out_shape=jax.ShapeDtypeStruct((n_rows, D), F32),
        grid=(B, L // tm),
        in_specs=[pl.BlockSpec((1, 2, tm), lambda b, i: (b, 0, i), memory_space=pltpu.SMEM),
                  pl.BlockSpec((1, tm, D), lambda b, i: (b, i, 0)),
                  pl.BlockSpec(memory_space=pl.ANY)],
        out_specs=pl.BlockSpec(memory_space=pl.ANY),
        scratch_shapes=[pltpu.VMEM((2, tm, D), F32), pltpu.SemaphoreType.DMA((2,))],
        input_output_aliases={2: 0},
        compiler_params=_cp(("arbitrary", "arbitrary")),
        name="moe_dispatch",
    )(rows, h, xs)


def _moe_grp_kernel(ge_ref, gn_ref, x_ref, wg_ref, wu_ref, wd_ref, y_ref, xb_scr):
    g = pl.program_id(0)
    j = pl.program_id(1)
    nsub = gn_ref[g]
    wg = wg_ref[0].astype(BF16)
    wu = wu_ref[0].astype(BF16)
    wd = wd_ref[0].astype(BF16)
    nblk = MOE_SUP // MOE_SUB

    @pl.when(j == 0)
    def _():
        xb_scr[...] = x_ref[...].astype(BF16)
        y_ref[...] = jnp.zeros_like(y_ref)

    def block(s):
        rows = slice(s * MOE_SUB, (s + 1) * MOE_SUB)
        xb = xb_scr[rows, :]
        act = _silu(_dot(xb, wg)) * _dot(xb, wu)
        y_ref[rows, :] = y_ref[rows, :] + _dot(act.astype(BF16), wd)

    for n in range(1, nblk + 1):
        @pl.when(nsub == n)
        def _():
            for s in range(n):
                block(s)


def _moe_groups(counts, n_groups):
    nsup = (counts + MOE_SUP - 1) // MOE_SUP
    div = jnp.maximum(nsup, 1)
    per = jnp.maximum(((counts + div - 1) // div + MOE_SUB - 1) // MOE_SUB * MOE_SUB, MOE_SUB)
    ends = jnp.cumsum(nsup)
    first = ends - nsup
    total = ends[-1]
    g = jnp.arange(n_groups, dtype=jnp.int32)
    gc = jnp.minimum(g, total - 1)
    e_of = jnp.minimum(jnp.sum((gc[:, None] >= ends[None, :]).astype(jnp.int32), axis=1), N_EXPERTS - 1)
    left = jnp.minimum(counts[e_of] - (gc - first[e_of]) * per[e_of], per[e_of])
    nsub = jnp.clip((left + MOE_SUB - 1) // MOE_SUB, 0, MOE_SUP // MOE_SUB)
    gn = jnp.where(g < total, nsub, 0).astype(jnp.int32)
    return e_of, gn, (first * MOE_SUP).astype(jnp.int32), per.astype(jnp.int32)


def _moe_grouped(xs, ge, gn, w_gu, w_down):
    D = xs.shape[1]
    nj = D_FF // MOE_FF_TILE
    ng = xs.shape[0] // MOE_SUP
    jj = lambda j, gn, g: jnp.where(gn[g] > 0, j, nj - 1)
    return pl.pallas_call(
        _moe_grp_kernel,
        out_shape=jax.ShapeDtypeStruct(xs.shape, F32),
        grid_spec=pltpu.PrefetchScalarGridSpec(
            num_scalar_prefetch=2,
            grid=(ng, nj),
            in_specs=[pl.BlockSpec((MOE_SUP, D), lambda g, j, ge, gn: (g, 0)),
                      pl.BlockSpec((1, D, MOE_FF_TILE), lambda g, j, ge, gn: (ge[g], 0, jj(j, gn, g))),
                      pl.BlockSpec((1, D, MOE_FF_TILE), lambda g, j, ge, gn: (ge[g], 0, nj + jj(j, gn, g))),
                      pl.BlockSpec((1, MOE_FF_TILE, D), lambda g, j, ge, gn: (ge[g], jj(j, gn, g), 0))],
            out_specs=pl.BlockSpec((MOE_SUP, D), lambda g, j, ge, gn: (g, 0)),
            scratch_shapes=[pltpu.VMEM((MOE_SUP, D), BF16)],
        ),
        compiler_params=_cp(("arbitrary", "arbitrary")),
        name="moe_experts",
    )(ge, gn, xs, w_gu, w_gu, w_down)


def _combine_kernel(row_ref, next_ref, w_ref, x_ref, gt_ref, gfin_ref, ys_ref, o_ref, g_scr, sem, *, final):
    tm = x_ref.shape[1]
    t = pl.program_id(0) * pl.num_programs(1) + pl.program_id(1)
    last = pl.num_programs(0) * pl.num_programs(1) - 1
    slot = t % 2

    def gather(rows, sl):
        def issue(r, _):
            for k in range(2):
                row = rows[0, k, r]
                _row_copy(ys_ref.at[pl.ds(row, 1), :], g_scr.at[sl, k, pl.ds(r, 1), :], sem.at[sl]).start()
            return 0

        lax.fori_loop(0, tm, issue, 0, unroll=8)

    @pl.when(t == 0)
    def _():
        gather(row_ref, slot)

    @pl.when(t < last)
    def _():
        gather(next_ref, 1 - slot)

    for k in range(2):
        _row_copy(ys_ref.at[pl.ds(0, tm), :], g_scr.at[slot, k], sem.at[slot]).wait()
    w = w_ref[0]
    f = w[:, 0:1] * g_scr[slot, 0] + w[:, 1:2] * g_scr[slot, 1]
    o_ref[0] = _finish(x_ref[0] + gt_ref[0, 0] * f, gfin_ref, final)


def _combine(rows, wts, x, mod, gfin, ys, *, l, final):
    B, L, D = x.shape
    tm = min(ROW_DMA_TM, L)
    gt_spec = _mod_spec(mod, l, 5, tm)
    row = pl.BlockSpec((1, tm, D), lambda b, i: (b, i, 0))
    nl = L // tm

    def next_block(b, i):
        t1 = jnp.minimum(b * nl + i + 1, B * nl - 1)
        return (t1 // nl, 0, t1 % nl)

    return pl.pallas_call(
        functools.partial(_combine_kernel, final=final),
        out_shape=jax.ShapeDtypeStruct((B, L, D), F32),
        grid=(B, L // tm),
        in_specs=[pl.BlockSpec((1, 2, tm), lambda b, i: (b, 0, i), memory_space=pltpu.SMEM),
                  pl.BlockSpec((1, 2, tm), next_block, memory_space=pltpu.SMEM),
                  pl.BlockSpec((1, tm, 2), lambda b, i: (b, i, 0)),
                  row, gt_spec,
                  pl.BlockSpec((1, D), lambda b, i: (0, 0)),
                  pl.BlockSpec(memory_space=pl.ANY)],
        out_specs=row,
        scratch_shapes=[pltpu.VMEM((2, 2, tm, D), F32), pltpu.SemaphoreType.DMA((2,))],
        compiler_params=_cp(("arbitrary", "arbitrary")),
        name="moe_combine",
    )(rows, rows, wts.transpose(0, 2, 1), x, mod, gfin, ys)


def _moe_routed(groups, b_r, w_gu, w_down, gfin, *, l, final):
    D = groups[0][1].shape[-1]
    n_tok = sum(g[1].shape[0] * g[1].shape[1] for g in groups)
    cap = 1 << (n_tok - 1).bit_length()
    n_groups = 2 * n_tok // MOE_SUP + N_EXPERTS
    cnt = jnp.zeros((N_EXPERTS, LANES), F32)
    routed = []
    for _, _, _, lgT in groups:
        codes, wts, cnt = _route_slots(lgT, b_r, cnt, cap)
        routed.append((codes, wts))
    ge, gn, start, per = _moe_groups(cnt[:, 0].astype(jnp.int32), n_groups)
    rows = [_slot_rows(start, per, codes, cap) for codes, _ in routed]
    xs = _zero_rows(n_groups * MOE_SUP, D)
    for (h, _, _, _), r in zip(groups, rows):
        xs = _dispatch(r, h, xs)
    ys = _moe_grouped(xs, ge, gn, w_gu, w_down)
    return [_combine(r, wts, x, mod, gfin, ys, l=l, final=final)
            for (_, x, mod, _), (_, wts), r in zip(groups, routed, rows)]


def _pad_lanes(v):
    return jnp.pad(v.reshape(1, -1), ((0, 0), (0, LANES - v.shape[-1])))


def _mixer_layer(x, mod, states, p, s5m, l, prev_sg, *, seq):
    B, L, D = x.shape
    hi = not seq
    s5r0, s5i0, sg0, sc0 = states
    w_in, w_gates, w_ab = p['w_in_seq' if seq else 'w_in']
    if seq:
        u, qkv, z, ga, gb, ab = _proj_seq(x, p['g_mix'], mod, w_in, w_gates, w_ab, l=l, tm=min(512, L))
    else:
        u, qkv, z, ga, gb, ab = _proj(x, p['g_mix'], mod, w_in, w_gates, w_ab, l=l, tm=L)
    alog = _pad_lanes(p['gdn_a_log'][l])
    dtb = _pad_lanes(p['gdn_dt_bias'][l])
    nw = p['gdn_norm_w'][l].reshape(1, GDN_DK)
    if seq:
        yg, sfin = _s5_seq(u, s5m['be'], s5m['tp'], s5m['cpm'], s5m['pt'],
                           jnp.zeros((SLABS, B, 1, 2 * SLAB_STATE), F32), s5m['dsk'][l], l)
        sfin = sfin.reshape(SLABS, B, 2, SLAB_STATE).transpose(2, 1, 0, 3)
        sr = sfin[0].reshape(B, S5_GROUPS, S5_STATE)
        si = sfin[1].reshape(B, S5_GROUPS, S5_STATE)
        og, sg = _gdn_seq(qkv, z, ab, p['gdn_conv_w'], alog, dtb, nw,
                          jnp.zeros((B, GDN_CONV - 1, QKV_WIDTH), F32),
                          jnp.zeros((B, GDN_HEADS, GDN_DK, GDN_DK), F32), l)
        cb = qkv[:, L - (GDN_CONV - 1):, :].astype(F32)
    else:
        n = L
        s0 = jnp.concatenate([s5r0[l].reshape(n, SLABS, SLAB_STATE),
                              s5i0[l].reshape(n, SLABS, SLAB_STATE)], axis=-1).transpose(1, 0, 2)
        yg, s1 = _s5_step(u.reshape(SLABS, n, LANES), s5m['bst'], s5m['cpe'], s5m['a1'],
                          s0, s5m['d1'][l], l)
        yg = yg.reshape(SLABS, 1, n, LANES)
        s1 = s1.transpose(1, 0, 2)
        sr = s1[:, :, :SLAB_STATE].reshape(n, S5_GROUPS, S5_STATE)
        si = s1[:, :, SLAB_STATE:].reshape(n, S5_GROUPS, S5_STATE)
        og, sg = _gdn_step(qkv.reshape(n, QKV_WIDTH), z.reshape(n, GDN_WIDTH), ab.reshape(n, LANES),
                           p['gdn_conv_w'][l], alog, dtb, nw,
                           sc0[l].reshape(n, (GDN_CONV - 1) * QKV_WIDTH), sg0, l, prev_sg)
        og = og.reshape(1, n, GDN_WIDTH)
        cb = jnp.concatenate([sc0[l][:, 1:, :], qkv.reshape(n, 1, QKV_WIDTH)], axis=1)
    x, h, lgT = _merge(yg, og, ga, gb, x, mod, p['w_s5_glu'], p['w_gdn_out'], p['w_out'],
                       p['g_ffn'], p['w_router'], l=l, tm=min(512, L), hi=hi, chunked=seq,
                       h_dtype=BF16 if (seq and l % 2 == 0) else F32)
    return x, h, lgT, (sr, si, sg, cb)


def kernel(x_prompt, x_sample, c_prompt, c_sample, state_s5_re, state_s5_im, state_gdn, state_conv,
           g_mix, g_ffn, g_final, w_ada, b_ada, w_in, s5_lambda_re, s5_lambda_im, s5_log_dt,
           s5_b_re, s5_b_im, s5_c_re, s5_c_im, s5_d, w_s5_glu, gdn_conv_w, gdn_a_log, gdn_dt_bias,
           gdn_norm_w, w_gdn_out, w_out, w_ffn_gate_up, w_ffn_down, w_router, b_router,
           w_exp_gate_up, w_exp_down):
    def in_proj_parts(w):
        return w, w[:, :, 2568:], jnp.pad(w[:, :, 2560:2568], ((0, 0), (0, 0), (0, LANES - 8)))

    D_ = x_prompt.shape[-1]
    p = dict(g_mix=g_mix.reshape(DEPTH, 1, D_), g_ffn=g_ffn.reshape(DEPTH, 1, D_), w_s5_glu=w_s5_glu,
             gdn_conv_w=gdn_conv_w, gdn_a_log=gdn_a_log, gdn_dt_bias=gdn_dt_bias,
             gdn_norm_w=gdn_norm_w, w_gdn_out=w_gdn_out, w_out=w_out,
             w_router=w_router.transpose(0, 2, 1), w_in=in_proj_parts(w_in),
             w_in_seq=in_proj_parts(w_in.astype(BF16)))
    nbp, L, D = x_prompt.shape
    nbs = x_sample.shape[0]

    mod = _ada(jnp.concatenate([c_prompt, c_sample], axis=0), w_ada, b_ada)
    mod_p = mod[:, :nbp].reshape(DEPTH, nbp, 1, 6 * D)
    mod_s = mod[:, nbp:].reshape(DEPTH, 1, nbs, 6 * D)

    seg = L // S5_T // 8
    be, bst, cpe, cpm, pt, a1 = _s5_prep(s5_lambda_re, s5_lambda_im, s5_log_dt, s5_b_re, s5_b_im,
                                         s5_c_re, s5_c_im, seg)
    d1 = [s5_d[l].reshape(SLABS, 1, LANES) for l in range(DEPTH)]
    s5m = dict(be=be, bst=bst, cpe=cpe, cpm=cpm, pt=pt, a1=a1, tp=_toep(bst, cpe), d1=d1,
               dsk=[jnp.tile(d, (1, 1, S5_T)) for d in d1])

    xs_ = [x_prompt, x_sample.reshape(1, nbs, D)]
    mods = [mod_p, mod_s]
    states = [(None, None, None, None), (state_s5_re, state_s5_im, state_gdn, state_conv)]
    outs = [[], []]
    gfin = g_final.reshape(1, D)
    for l in range(DEPTH):
        final = l == DEPTH - 1
        mixed = []
        for gi, seq in enumerate((True, False)):
            prev_sg = outs[gi][0][2] if (not seq and final and DEPTH == 2) else None
            x, h, lgT, st = _mixer_layer(xs_[gi], mods[gi], states[gi], p, s5m, l, prev_sg, seq=seq)
            outs[gi].append(st)
            mixed.append((h, x, mods[gi], lgT))
        if l % 2 == 0:
            wgu, wdn = w_ffn_gate_up[l // 2], w_ffn_down[l // 2]
            xs_ = [_ffn(h, x, mod_g, wgu if gi else wgu.astype(BF16), wdn if gi else wdn.astype(BF16), gfin,
                        l=l, tm=min(1024, x.shape[1]), hi=(gi == 1), final=final)
                   for gi, (h, x, mod_g, _) in enumerate(mixed)]
        else:
            xs_ = _moe_routed(mixed, b_router[l // 2], w_exp_gate_up[l // 2], w_exp_down[l // 2], gfin,
                              l=l, final=final)
    y_p, y_s = xs_
    st_p = [jnp.stack([o[i] for o in outs[0]]) for i in range(4)]
    st_s = [outs[1][-1][2] if (i == 2 and DEPTH == 2) else jnp.stack([o[i] for o in outs[1]])
            for i in range(4)]
    return (y_p, y_s.reshape(nbs, 1, D), st_p[0], st_p[1], st_p[2], st_p[3],
            st_s[0], st_s[1], st_s[2], st_s[3])
```

```python
import functools

import jax
import jax.numpy as jnp
from jax import lax
from jax.experimental import pallas as pl
from jax.experimental.pallas import tpu as pltpu

F32 = jnp.float32
BF16 = jnp.bfloat16
HI = lax.Precision.HIGHEST

D_MODEL = 1024
DEPTH = 2
S5_WIDTH = 512
S5_GROUP = 16
S5_GROUPS = 32
S5_STATE = 64
GDN_HEADS = 4
GDN_DK = 128
GDN_WIDTH = 512
GDN_CONV = 4
QKV_WIDTH = 1536
D_FF = 3584
N_EXPERTS = 8
NORM_EPS = 1e-6
L2_EPS = 1e-6

LANES = 128
SLABS = S5_WIDTH // LANES
SLAB_STATE = (S5_GROUPS // SLABS) * S5_STATE
S5_T = 8
S5_SEG_PAD = 4
GDN_C = 128
GDN_BLOCK = 512
GDN_SUB = 256
VMEM_LIMIT = 56 * 1024 * 1024


def _cp(sem, vmem=VMEM_LIMIT):
    return pltpu.CompilerParams(dimension_semantics=sem, vmem_limit_bytes=vmem)


def _dot(a, b, prec=None):
    return jnp.dot(a, b, precision=prec, preferred_element_type=F32)


def _dotb(a, b):
    return jnp.dot(a.astype(BF16), b.astype(BF16), preferred_element_type=F32)


def _dot_nt(a, b, prec=None):
    return lax.dot_general(a, b, (((1,), (1,)), ((), ())), precision=prec,
                           preferred_element_type=F32)


def _dot_tn(a, b, prec=None):
    return lax.dot_general(a, b, (((0,), (0,)), ((), ())), precision=prec,
                           preferred_element_type=F32)


def _silu(x):
    return x * jax.nn.sigmoid(x)


def _ada_kernel(c_ref, w_ref, b_ref, o_ref):
    cs = _silu(c_ref[...])
    o_ref[0] = _dot(cs, w_ref[0], HI) + b_ref[0]


def _ada(c_all, w_ada, b_ada):
    n = c_all.shape[0]
    tn = 1536
    return pl.pallas_call(
        _ada_kernel,
        out_shape=jax.ShapeDtypeStruct((DEPTH, n, 6 * D_MODEL), F32),
        grid=(DEPTH, 6 * D_MODEL // tn),
        in_specs=[pl.BlockSpec((n, D_MODEL), lambda l, j: (0, 0)),
                  pl.BlockSpec((1, D_MODEL, tn), lambda l, j: (l, 0, j)),
                  pl.BlockSpec((1, 1, tn), lambda l, j: (l, 0, j))],
        out_specs=pl.BlockSpec((1, n, tn), lambda l, j: (l, 0, j)),
        compiler_params=_cp(("parallel", "parallel")),
        name="ada_mod",
    )(c_all, w_ada, b_ada.reshape(DEPTH, 1, 6 * D_MODEL))


def _proj_kernel(x_ref, g_ref, sc_ref, sh_ref, w_ref, wg_ref, wab_ref,
                 u_ref, qkv_ref, z_ref, ga_ref, gb_ref, ab_ref, h_scr):
    j = pl.program_id(2)

    @pl.when(j == 0)
    def _():
        x = x_ref[0]
        ms = jnp.mean(x * x, axis=-1, keepdims=True)
        xn = x * lax.rsqrt(ms + NORM_EPS) * g_ref[0]
        h_scr[...] = (xn * (1.0 + sc_ref[0, 0]) + sh_ref[0, 0]).astype(h_scr.dtype)

    def mm(w):
        return _dot(h_scr[...], w, HI)

    @pl.when(j == 0)
    def _():
        res = mm(w_ref[0])
        for k in range(SLABS):
            u_ref[k, 0] = res[:, k * LANES:(k + 1) * LANES]

    @pl.when((j >= 1) & (j <= 3))
    def _():
        qkv_ref[0] = mm(w_ref[0])

    @pl.when(j == 4)
    def _():
        z_ref[0] = mm(w_ref[0])

    @pl.when((j == 5) | (j == 6))
    def _():
        ga_ref[0] = jax.nn.sigmoid(mm(wg_ref[0]))

    @pl.when((j == 7) | (j == 8))
    def _():
        gb_ref[0] = jax.nn.sigmoid(mm(wg_ref[0]))

    @pl.when(j == 9)
    def _():
        ab_ref[0] = mm(wab_ref[0])


def _mod_spec(mod, l, chunk, tm):
    per_row = mod.shape[2] != 1
    D = mod.shape[3] // 6

    def index(b, i, *_):
        return (l, b, i if per_row else 0, chunk)

    return pl.BlockSpec((1, 1, tm if per_row else 1, D), index)


def _proj(x, g, mod, w_in, w_gates, w_ab, *, l, tm):
    B, L, D = x.shape
    tn = 512
    clampi = lambda j, lo, n: jnp.clip(j - lo, 0, n - 1)
    outs = pl.pallas_call(
        _proj_kernel,
        out_shape=(jax.ShapeDtypeStruct((SLABS, B, L, LANES), F32),
                   jax.ShapeDtypeStruct((B, L, QKV_WIDTH), F32),
                   jax.ShapeDtypeStruct((B, L, GDN_WIDTH), F32),
                   jax.ShapeDtypeStruct((B, L, D), F32),
                   jax.ShapeDtypeStruct((B, L, D), F32),
                   jax.ShapeDtypeStruct((B, L, LANES), F32)),
        grid=(B, L // tm, 10),
        in_specs=[pl.BlockSpec((1, tm, D), lambda b, i, j: (b, i, 0)),
                  pl.BlockSpec((1, 1, D), lambda b, i, j: (l, 0, 0)),
                  _mod_spec(mod, l, 1, tm),
                  _mod_spec(mod, l, 0, tm),
                  pl.BlockSpec((1, D, tn), lambda b, i, j: (l, 0, jnp.minimum(j, 4))),
                  pl.BlockSpec((1, D, tn), lambda b, i, j: (l, 0, clampi(j, 5, 4))),
                  pl.BlockSpec((1, D, LANES), lambda b, i, j: (l, 0, 0))],
        out_specs=(pl.BlockSpec((SLABS, 1, tm, LANES), lambda b, i, j: (0, b, i, 0)),
                   pl.BlockSpec((1, tm, tn), lambda b, i, j: (b, i, clampi(j, 1, 3))),
                   pl.BlockSpec((1, tm, tn), lambda b, i, j: (b, i, 0)),
                   pl.BlockSpec((1, tm, tn), lambda b, i, j: (b, i, clampi(j, 5, 2))),
                   pl.BlockSpec((1, tm, tn), lambda b, i, j: (b, i, clampi(j, 7, 2))),
                   pl.BlockSpec((1, tm, LANES), lambda b, i, j: (b, i, 0))),
        scratch_shapes=[pltpu.VMEM((tm, D), F32)],
        compiler_params=_cp(("parallel", "parallel", "arbitrary")),
        name="norm_in_proj",
    )(x, g, mod, mod, w_in, w_gates, w_ab)
    return outs


def _proj_seq_kernel(x_ref, g_ref, sc_ref, sh_ref, w_ref, wg_ref, wab_ref,
                     u_ref, qkv_ref, z_ref, ga_ref, gb_ref, ab_ref, us_scr):
    x = x_ref[0]
    ms = jnp.mean(x * x, axis=-1, keepdims=True)
    xn = x * lax.rsqrt(ms + NORM_EPS) * g_ref[0]
    h = (xn * (1.0 + sc_ref[0, 0]) + sh_ref[0, 0]).astype(BF16)
    res = _dot(h, w_ref[0, :, 0:S5_WIDTH])
    nrow = res.shape[0] // S5_T
    for k in range(SLABS):
        us_scr[...] = res[:, k * LANES:(k + 1) * LANES]
        for t in range(S5_T):
            u_ref[k, 0, :, t * LANES:(t + 1) * LANES] = (
                us_scr[pl.ds(t, nrow, stride=S5_T), :].astype(u_ref.dtype))
    c0 = S5_WIDTH
    qkv_ref[0] = _dot(h, w_ref[0, :, c0:c0 + QKV_WIDTH]).astype(qkv_ref.dtype)
    c0 += QKV_WIDTH
    z_ref[0] = _dot(h, w_ref[0, :, c0:c0 + GDN_WIDTH]).astype(z_ref.dtype)
    D = x.shape[-1]
    ga_ref[0] = jax.nn.sigmoid(_dot(h, wg_ref[0, :, 0:D])).astype(ga_ref.dtype)
    gb_ref[0] = jax.nn.sigmoid(_dot(h, wg_ref[0, :, D:2 * D])).astype(gb_ref.dtype)
    ab_ref[0] = _dot(h, wab_ref[0])


def _proj_seq(x, g, mod, w_in, w_gates, w_ab, *, l, tm):
    B, L, D = x.shape
    n_main = S5_WIDTH + QKV_WIDTH + GDN_WIDTH
    row = lambda w: pl.BlockSpec((1, tm, w), lambda b, i: (b, i, 0))
    return pl.pallas_call(
        _proj_seq_kernel,
        out_shape=(jax.ShapeDtypeStruct((SLABS, B, L // S5_T, S5_T * LANES), BF16),
                   jax.ShapeDtypeStruct((B, L, QKV_WIDTH), BF16),
                   jax.ShapeDtypeStruct((B, L, GDN_WIDTH), BF16),
                   jax.ShapeDtypeStruct((B, L, D), BF16),
                   jax.ShapeDtypeStruct((B, L, D), BF16),
                   jax.ShapeDtypeStruct((B, L, LANES), F32)),
        grid=(B, L // tm),
        in_specs=[row(D),
                  pl.BlockSpec((1, 1, D), lambda b, i: (l, 0, 0)),
                  _mod_spec(mod, l, 1, tm),
                  _mod_spec(mod, l, 0, tm),
                  pl.BlockSpec((1, D, n_main), lambda b, i: (l, 0, 0)),
                  pl.BlockSpec((1, D, 2 * D), lambda b, i: (l, 0, 0)),
                  pl.BlockSpec((1, D, LANES), lambda b, i: (l, 0, 0))],
        out_specs=(pl.BlockSpec((SLABS, 1, tm // S5_T, S5_T * LANES), lambda b, i: (0, b, i, 0)),
                   row(QKV_WIDTH), row(GDN_WIDTH), row(D), row(D), row(LANES)),
        scratch_shapes=[pltpu.VMEM((tm, LANES), F32)],
        compiler_params=_cp(("parallel", "parallel")),
        name="norm_in_proj_seq",
    )(x, g, mod, mod, w_in, w_gates, w_ab)


GROUPS_PER_SLAB = S5_GROUPS // SLABS


def _s5_prep_kernel(lrb, lib, dtb, bre, bim, lrc, lic, dtc, cre, cim, lrn, lin, dtn,
                    be_ref, bst_ref, cpe_ref, cpm_ref, pt_ref, a1_ref, *, seg):
    W = SLAB_STATE

    def disc(lr, li, ldt):
        dt = jnp.exp(ldt)
        mag = jnp.exp(lr * dt)
        return mag * jnp.cos(li * dt), mag * jnp.sin(li * dt)

    def cmul(xr, xi, yr, yi):
        return xr * yr - xi * yi, xr * yi + xi * yr

    lr, li = lrb[0], lib[0]
    ar, ai = disc(lr, li, dtb[0])
    den = lr * lr + li * li
    nr = ar - 1.0
    kr = (nr * lr + ai * li) / den
    ki = (ai * lr - nr * li) / den
    br, bi = bre[0], bim[0]
    bbr = kr * br - ki * bi
    bbi = kr * bi + ki * br
    rgrp = lax.broadcasted_iota(jnp.int32, (LANES, LANES), 0) // S5_GROUP
    lane_hi = lax.broadcasted_iota(jnp.int32, (LANES, LANES), 1) // S5_STATE
    pr, pi = jnp.ones_like(ar), jnp.zeros_like(ar)
    for d in range(S5_T):
        t = S5_T - 1 - d
        for ri, val in enumerate(cmul(pr, pi, bbr, bbi)):
            two = jnp.concatenate([val, val], axis=1)
            for m in range(GROUPS_PER_SLAB // 2):
                tile = jnp.where(rgrp == 2 * m + lane_hi, two, 0.0)
                c0 = ri * W + m * LANES
                be_ref[0, 0, t * LANES:(t + 1) * LANES, c0:c0 + LANES] = tile.astype(BF16)
                if d == 0:
                    bst_ref[0, 0, :, c0:c0 + LANES] = tile
        pr, pi = cmul(pr, pi, ar, ai)

    ar, ai = disc(lrc[0], lic[0], dtc[0])
    cr, ci = cre[0], cim[0]
    own = (lax.broadcasted_iota(jnp.int32, (W, LANES), 0) // S5_STATE
           == lax.broadcasted_iota(jnp.int32, (W, LANES), 1) // S5_GROUP)
    pr, pi = jnp.ones_like(ar), jnp.zeros_like(ar)
    for d in range(S5_T + 1):
        vr, vi = cmul(cr, ci, pr, pi)
        for ri, val in enumerate((vr, -vi)):
            tile = jnp.where(own, val, 0.0)
            cpe_ref[0, 0, d, ri * W:(ri + 1) * W, :] = tile
            if d >= 1:
                cpm_ref[0, 0, ri * W:(ri + 1) * W, (d - 1) * LANES:d * LANES] = tile.astype(BF16)
        pr, pi = cmul(pr, pi, ar, ai)

    ar, ai = disc(lrn[0, 0], lin[0, 0], dtn[0, 0])
    a1_ref[0, 0, :, 0:W] = ar
    a1_ref[0, 0, :, W:2 * W] = ai
    tr, ti = ar, ai
    for _ in range(S5_T - 1):
        tr, ti = cmul(tr, ti, ar, ai)
    pr, pi = jnp.ones_like(ar), jnp.zeros_like(ar)
    for i in range(seg + 1):
        pt_ref[0, 0, i:i + 1, 0:W] = pr
        pt_ref[0, 0, i:i + 1, W:2 * W] = pi
        pr, pi = cmul(pr, pi, tr, ti)


def _s5_prep(lam_re, lam_im, log_dt, b_re, b_im, c_re, c_im, seg):
    G, P, C = S5_GROUPS, S5_STATE, S5_GROUP
    W2 = 2 * SLAB_STATE
    dt3 = jnp.broadcast_to(log_dt[:, :, None], (DEPTH, G, P))
    rows_b = lambda a: jnp.repeat(a, C, axis=1)
    bt = lambda a: a.transpose(0, 1, 3, 2).reshape(DEPTH, G * C, P)
    rows_c = lambda a: jnp.broadcast_to(a.reshape(DEPTH, G * P, 1), (DEPTH, G * P, LANES))
    ct = lambda a: jnp.tile(a.transpose(0, 1, 3, 2).reshape(DEPTH, G * P, C), (1, 1, LANES // C))
    nat = lambda a: a.reshape(DEPTH, SLABS, 1, SLAB_STATE)
    args = (rows_b(lam_re), rows_b(lam_im), rows_b(dt3), bt(b_re), bt(b_im),
            rows_c(lam_re), rows_c(lam_im), rows_c(dt3), ct(c_re), ct(c_im),
            nat(lam_re), nat(lam_im), nat(dt3))
    bspec = pl.BlockSpec((1, LANES, P), lambda l, k: (l, k, 0))
    cspec = pl.BlockSpec((1, SLAB_STATE, LANES), lambda l, k: (l, k, 0))
    nspec = pl.BlockSpec((1, 1, 1, SLAB_STATE), lambda l, k: (l, k, 0, 0))
    return pl.pallas_call(
        functools.partial(_s5_prep_kernel, seg=seg),
        out_shape=(jax.ShapeDtypeStruct((DEPTH, SLABS, S5_T * LANES, W2), BF16),
                   jax.ShapeDtypeStruct((DEPTH, SLABS, LANES, W2), F32),
                   jax.ShapeDtypeStruct((DEPTH, SLABS, S5_T + 1, W2, LANES), F32),
                   jax.ShapeDtypeStruct((DEPTH, SLABS, W2, S5_T * LANES), BF16),
                   jax.ShapeDtypeStruct((DEPTH, SLABS, seg + 1, W2), F32),
                   jax.ShapeDtypeStruct((DEPTH, SLABS, 1, W2), F32)),
        grid=(DEPTH, SLABS),
        in_specs=[bspec] * 5 + [cspec] * 5 + [nspec] * 3,
        out_specs=(pl.BlockSpec((1, 1, S5_T * LANES, W2), lambda l, k: (l, k, 0, 0)),
                   pl.BlockSpec((1, 1, LANES, W2), lambda l, k: (l, k, 0, 0)),
                   pl.BlockSpec((1, 1, S5_T + 1, W2, LANES), lambda l, k: (l, k, 0, 0, 0)),
                   pl.BlockSpec((1, 1, W2, S5_T * LANES), lambda l, k: (l, k, 0, 0)),
                   pl.BlockSpec((1, 1, seg + 1, W2), lambda l, k: (l, k, 0, 0)),
                   pl.BlockSpec((1, 1, 1, W2), lambda l, k: (l, k, 0, 0))),
        compiler_params=_cp(("parallel", "parallel")),
        name="s5_discretize",
    )(*args)


def _toep_kernel(b_ref, c_ref, o_ref):
    dd = pl.program_id(2)
    bst = b_ref[0, 0]
    lag = lambda d: _dot(bst, c_ref[0, 0, d], HI)
    k0 = lag(2 * dd)
    o_ref[0, 0, 0, 0:LANES, 0:LANES] = k0.astype(BF16)
    o_ref[0, 0, 0, LANES:, LANES:] = k0.astype(BF16)
    o_ref[0, 0, 0, 0:LANES, LANES:] = lag(2 * dd + 1).astype(BF16)
    km = lag(jnp.maximum(2 * dd - 1, 0))
    o_ref[0, 0, 0, LANES:, 0:LANES] = jnp.where(dd > 0, km, 0.0).astype(BF16)


def _toep(bst, cpe):
    W2 = 2 * SLAB_STATE
    return pl.pallas_call(
        _toep_kernel,
        out_shape=jax.ShapeDtypeStruct((DEPTH, SLABS, S5_T // 2, 2 * LANES, 2 * LANES), BF16),
        grid=(DEPTH, SLABS, S5_T // 2),
        in_specs=[pl.BlockSpec((1, 1, LANES, W2), lambda l, k, d: (l, k, 0, 0)),
                  pl.BlockSpec((1, 1, S5_T + 1, W2, LANES), lambda l, k, d: (l, k, 0, 0, 0))],
        out_specs=pl.BlockSpec((1, 1, 1, 2 * LANES, 2 * LANES), lambda l, k, d: (l, k, d, 0, 0)),
        compiler_params=_cp(("parallel", "parallel", "parallel")),
        name="s5_conv_blocks",
    )(bst, cpe)


def _s5_seq_kernel(up_ref, be_ref, tp_ref, cpm_ref, pt_ref, s0_ref, dsk_ref,
                   yg_ref, sfin_ref, e_scr, sx_scr, *, nc):
    seg = nc // 8
    W = SLAB_STATE
    nt = W // LANES
    ub = up_ref[0, 0]
    u = ub.astype(F32)
    e = _dot(ub, be_ref[0, 0])
    pitch = e_scr.shape[1] // 8
    for c in range(2 * nt):
        for j in range(8):
            e_scr[c, j * pitch:j * pitch + seg, :] = e[j * seg:(j + 1) * seg, c * LANES:(c + 1) * LANES]

    def tiles(row):
        return [(row[:, c * LANES:(c + 1) * LANES], row[:, W + c * LANES:W + (c + 1) * LANES])
                for c in range(nt)]

    a8 = [(jnp.broadcast_to(r, (8, LANES)), jnp.broadcast_to(i, (8, LANES)))
          for r, i in tiles(pt_ref[0, 0, 1:2, :])]

    def step(i, carry):
        rows = pl.ds(i, 8, stride=pitch)
        new = []
        for c in range(nt):
            sr, si = carry[c]
            ar, ai = a8[c]
            sx_scr[c, rows, :] = sr
            sx_scr[nt + c, rows, :] = si
            new.append((ar * sr - ai * si + e_scr[c, rows, :],
                        ar * si + ai * sr + e_scr[nt + c, rows, :]))
        return tuple(new)

    zero = jnp.zeros((8, LANES), F32)
    ends = lax.fori_loop(0, seg, step, tuple((zero, zero) for _ in range(nt)))

    al = tiles(pt_ref[0, 0, seg:seg + 1, :])
    cur = tiles(s0_ref[0, 0])
    car = []
    for c in range(nt):
        alr, ali = al[c]
        cr, ci = cur[c]
        sr, si = ends[c]
        crs, cis = [], []
        for j in range(8):
            crs.append(cr)
            cis.append(ci)
            cr, ci = (alr * cr - ali * ci + sr[j:j + 1], alr * ci + ali * cr + si[j:j + 1])
        sfin_ref[0, 0, :, c * LANES:(c + 1) * LANES] = cr
        sfin_ref[0, 0, :, W + c * LANES:W + (c + 1) * LANES] = ci
        car.append((jnp.concatenate(crs, axis=0), jnp.concatenate(cis, axis=0)))

    def corr(i, _):
        rows = pl.ds(i, 8, stride=pitch)
        pw = tiles(pt_ref[0, 0, pl.ds(i, 1), :])
        for c in range(nt):
            pr, pi = pw[c]
            cr, ci = car[c]
            sx_scr[c, rows, :] = sx_scr[c, rows, :] + (pr * cr - pi * ci)
            sx_scr[nt + c, rows, :] = sx_scr[nt + c, rows, :] + (pr * ci + pi * cr)
        return 0

    lax.fori_loop(0, seg, corr, 0)

    sx = jnp.concatenate(
        [jnp.concatenate([sx_scr[c, j * pitch:j * pitch + seg, :] for j in range(8)], axis=0)
         for c in range(2 * nt)], axis=-1)
    y = _dot(sx.astype(BF16), cpm_ref[0, 0])
    TW = 2 * LANES
    for tq in range(S5_T // 2):
        acc = y[:, tq * TW:(tq + 1) * TW]
        for tpi in range(tq + 1):
            acc = acc + _dot(ub[:, tpi * TW:(tpi + 1) * TW], tp_ref[0, 0, tq - tpi])
        acc = acc + dsk_ref[0, :, tq * TW:(tq + 1) * TW] * u[:, tq * TW:(tq + 1) * TW]
        yg_ref[0, 0, :, tq * TW:(tq + 1) * TW] = jax.nn.gelu(acc).astype(yg_ref.dtype)


def _s5_seq(up, be_emb, tp, cpm, pt, s0, dsk, l):
    _, B, nc, _ = up.shape
    seg = nc // 8
    W2 = 2 * SLAB_STATE
    yg, sfin = pl.pallas_call(
        functools.partial(_s5_seq_kernel, nc=nc),
        out_shape=(jax.ShapeDtypeStruct((SLABS, B, nc, S5_T * LANES), BF16),
                   jax.ShapeDtypeStruct((SLABS, B, 1, W2), F32)),
        grid=(SLABS, B),
        in_specs=[pl.BlockSpec((1, 1, nc, S5_T * LANES), lambda k, b: (k, b, 0, 0)),
                  pl.BlockSpec((1, 1, S5_T * LANES, W2), lambda k, b: (l, k, 0, 0)),
                  pl.BlockSpec((1, 1, S5_T // 2, 2 * LANES, 2 * LANES), lambda k, b: (l, k, 0, 0, 0)),
                  pl.BlockSpec((1, 1, W2, S5_T * LANES), lambda k, b: (l, k, 0, 0)),
                  pl.BlockSpec((1, 1, seg + 1, W2), lambda k, b: (l, k, 0, 0)),
                  pl.BlockSpec((1, 1, 1, W2), lambda k, b: (k, b, 0, 0)),
                  pl.BlockSpec((1, 1, S5_T * LANES), lambda k, b: (k, 0, 0))],
        out_specs=(pl.BlockSpec((1, 1, nc, S5_T * LANES), lambda k, b: (k, b, 0, 0)),
                   pl.BlockSpec((1, 1, 1, W2), lambda k, b: (k, b, 0, 0))),
        scratch_shapes=[pltpu.VMEM((W2 // LANES, 8 * (seg + S5_SEG_PAD), LANES), F32),
                        pltpu.VMEM((W2 // LANES, 8 * (seg + S5_SEG_PAD), LANES), F32)],
        compiler_params=_cp(("parallel", "parallel")),
        name="s5_seq",
    )(up, be_emb, tp, cpm, pt, s0, dsk)
    return yg, sfin


def _s5_step_kernel(u_ref, b_ref, c_ref, a_ref, s0_ref, d_ref, yg_ref, s1_ref):
    W = SLAB_STATE
    u = u_ref[0]
    bu = _dot(u, b_ref[0, 0], HI)
    ar = a_ref[0, 0, :, 0:W]
    ai = a_ref[0, 0, :, W:2 * W]
    sr = s0_ref[0, :, 0:W]
    si = s0_ref[0, :, W:2 * W]
    nr = ar * sr - ai * si + bu[:, 0:W]
    ni = ar * si + ai * sr + bu[:, W:2 * W]
    s1_ref[0, :, 0:W] = nr
    s1_ref[0, :, W:2 * W] = ni
    s1 = jnp.concatenate([nr, ni], axis=-1)
    y = _dot(s1, c_ref[0, 0, 0], HI) + d_ref[0] * u
    yg_ref[0] = jax.nn.gelu(y)


def _s5_step(u_slab, bst, cpe, a1, s0, d1, l):
    _, N, _ = u_slab.shape
    W2 = 2 * SLAB_STATE
    return pl.pallas_call(
        _s5_step_kernel,
        out_shape=(jax.ShapeDtypeStruct((SLABS, N, LANES), F32),
                   jax.ShapeDtypeStruct((SLABS, N, W2), F32)),
        grid=(SLABS,),
        in_specs=[pl.BlockSpec((1, N, LANES), lambda k: (k, 0, 0)),
                  pl.BlockSpec((1, 1, LANES, W2), lambda k: (l, k, 0, 0)),
                  pl.BlockSpec((1, 1, 1, W2, LANES), lambda k: (l, k, 0, 0, 0)),
                  pl.BlockSpec((1, 1, 1, W2), lambda k: (l, k, 0, 0)),
                  pl.BlockSpec((1, N, W2), lambda k: (k, 0, 0)),
                  pl.BlockSpec((1, 1, LANES), lambda k: (k, 0, 0))],
        out_specs=(pl.BlockSpec((1, N, LANES), lambda k: (k, 0, 0)),
                   pl.BlockSpec((1, N, W2), lambda k: (k, 0, 0))),
        compiler_params=_cp(("parallel",)),
        name="s5_step",
    )(u_slab, bst, cpe, a1, s0, d1)


def _l2n(x):
    return x * lax.rsqrt(jnp.sum(x * x, axis=-1, keepdims=True) + L2_EPS)


def _split_bf16(x):
    hi = x.astype(BF16)
    return hi, (x - hi.astype(F32)).astype(BF16)


def _unit_lower_solve(As, rhss):
    n = GDN_C
    row = lax.broadcasted_iota(jnp.int32, (n, n), 0)
    col = lax.broadcasted_iota(jnp.int32, (n, n), 1)
    eye = (row == col).astype(F32)
    same8 = (row // 8) == (col // 8)
    Qs = [jnp.where(same8, -A, 0.0) for A in As]
    invs = [eye + Q for Q in Qs]
    for _ in range(2):
        Qs = [_dotb(Q, Q) for Q in Qs]
        invs = [inv + _dotb(inv, Q) for inv, Q in zip(invs, Qs)]
    s = 8
    while s < n:
        sib = ((row // (2 * s)) == (col // (2 * s))) & ((row // s) != (col // s))
        offs = [jnp.where(sib, A, 0.0).astype(BF16) for A in As]
        invb = [inv.astype(BF16) for inv in invs]
        tmp = [_dot(off, ib) for off, ib in zip(offs, invb)]
        invs = [inv - _dot(ib, t.astype(BF16)) for inv, ib, t in zip(invs, invb, tmp)]
        s *= 2
    invb = [inv.astype(BF16) for inv in invs]
    x0s = [_dot(ib, rhs.astype(BF16)) for ib, rhs in zip(invb, rhss)]
    res = []
    for A, x0, rhs in zip(As, x0s, rhss):
        ah, al = _split_bf16(A)
        xh, xl = _split_bf16(x0)
        res.append(rhs - x0 - (_dot(ah, xh) + _dot(ah, xl) + _dot(al, xh)))
    return [x0 + _dot(ib, r.astype(BF16)) for x0, ib, r in zip(x0s, invb, res)]


def _gdn_tile(qc_scr, gc, beta, z_ref, nw, o_ref, s_scr, tl, r0):
    C, DK, H = GDN_C, GDN_DK, GDN_HEADS
    nchunk = tl // C
    probs = [(c, h) for c in range(nchunk) for h in range(H)]
    row = lax.broadcasted_iota(jnp.int32, (C, C), 0)
    col = lax.broadcasted_iota(jnp.int32, (C, C), 1)
    tri = row >= col
    strict = row > col

    def blk(c, off):
        return qc_scr[c * C:(c + 1) * C, off:off + DK]

    q = [_l2n(blk(c, h * DK)) * (DK ** -0.5) for c, h in probs]
    k = [_l2n(blk(c, GDN_WIDTH + h * DK)) for c, h in probs]
    v = [blk(c, 2 * GDN_WIDTH + h * DK) for c, h in probs]
    gcb = [jnp.broadcast_to(gc[c * C:(c + 1) * C, h:h + 1], (C, DK)) for c, h in probs]
    bb = [jnp.broadcast_to(beta[c * C:(c + 1) * C, H + h:H + h + 1], (C, DK)) for c, h in probs]
    decay = []
    for g in gcb:
        diff = g - g.T
        decay.append(jnp.where(tri, jnp.exp(jnp.where(tri, diff, 0.0)), 0.0))
    kbf = [x.astype(BF16) for x in k]
    kb = [x * b for x, b in zip(k, bb)]
    A = [jnp.where(strict, _dot_nt(x.astype(BF16), y) * d, 0.0) for x, y, d in zip(kb, kbf, decay)]
    egc = [jnp.exp(g) for g in gcb]
    rhs = [jnp.concatenate([x * b, y * e], axis=-1) for x, b, y, e in zip(v, bb, kb, egc)]
    sol = _unit_lower_solve(A, rhs)
    attn = [jnp.where(tri, _dot_nt(x.astype(BF16), y) * d, 0.0).astype(BF16)
            for x, y, d in zip(q, kbf, decay)]
    glast = [g[C - 1:C, :] for g in gcb]
    wq = [jnp.concatenate([s[:, DK:], x * e], axis=0).astype(BF16) for s, x, e in zip(sol, q, egc)]
    kg = [(x * jnp.exp(gl - g)).astype(BF16) for x, gl, g in zip(k, glast, gcb)]

    for c in range(nchunk):
        ps = [c * H + h for h in range(H)]
        S = [s_scr[h] for h in range(H)]
        ws = [_dot(wq[p], S[h].astype(BF16)) for h, p in enumerate(ps)]
        v_new = [sol[p][:, 0:DK] - w[0:C] for p, w in zip(ps, ws)]
        vb = [x.astype(BF16) for x in v_new]
        o = [w[C:] + _dot(attn[p], x) for p, w, x in zip(ps, ws, vb)]
        for h, p in enumerate(ps):
            s_scr[h] = S[h] * jnp.exp(glast[p]) + _dot_tn(kg[p], vb[h])
            zh = z_ref[0, r0 + c * C:r0 + (c + 1) * C, h * DK:(h + 1) * DK].astype(F32)
            on = o[h] * lax.rsqrt(jnp.mean(o[h] * o[h], axis=-1, keepdims=True) + NORM_EPS) * nw
            o_ref[0, r0 + c * C:r0 + (c + 1) * C, h * DK:(h + 1) * DK] = (on * _silu(zh)).astype(o_ref.dtype)


def _gdn_seq_kernel(qkv_ref, z_ref, ab_ref, cw_ref, alog_ref, dtb_ref, nw_ref, conv0_ref, s0_ref,
                    o_ref, sfin_ref, xp_scr, qc_scr, s_scr, *, tl):
    lt = pl.program_id(1)

    @pl.when(lt == 0)
    def _():
        xp_scr[0:8, :] = jnp.zeros((8, QKV_WIDTH), F32)
        xp_scr[8 - (GDN_CONV - 1):8, :] = conv0_ref[0]
        s_scr[...] = s0_ref[0]

    sub = qc_scr.shape[0]
    for r0 in range(0, tl, sub):
        xp_scr[8:8 + sub, :] = qkv_ref[0, r0:r0 + sub, :].astype(F32)
        conv = cw_ref[0, 0:1, :] * xp_scr[5:5 + sub, :]
        for j in range(1, GDN_CONV):
            conv = conv + cw_ref[0, j:j + 1, :] * xp_scr[5 + j:5 + j + sub, :]
        xp_scr[0:8, :] = xp_scr[sub:sub + 8, :]
        qc_scr[...] = _silu(conv)

        ab = ab_ref[0, r0:r0 + sub, :]
        g = -jnp.exp(alog_ref[...]) * jax.nn.softplus(ab + dtb_ref[...])
        beta = jax.nn.sigmoid(ab)
        row = lax.broadcasted_iota(jnp.int32, (sub, sub), 0)
        col = lax.broadcasted_iota(jnp.int32, (sub, sub), 1)
        csum = ((row >= col) & ((row // GDN_C) == (col // GDN_C))).astype(F32)
        gc = _dot(csum, g, HI)
        _gdn_tile(qc_scr, gc, beta, z_ref, nw_ref[...], o_ref, s_scr, sub, r0)

    @pl.when(lt == pl.num_programs(1) - 1)
    def _():
        sfin_ref[0] = s_scr[...]


def _gdn_seq(qkv, z, ab, conv_w, alog, dtb, nw, conv0, s0, l):
    B, L, _ = qkv.shape
    tl = min(GDN_BLOCK, L)
    sub = min(GDN_SUB, tl)
    return pl.pallas_call(
        functools.partial(_gdn_seq_kernel, tl=tl),
        out_shape=(jax.ShapeDtypeStruct((B, L, GDN_WIDTH), BF16),
                   jax.ShapeDtypeStruct((B, GDN_HEADS, GDN_DK, GDN_DK), F32)),
        grid=(B, L // tl),
        in_specs=[pl.BlockSpec((1, tl, QKV_WIDTH), lambda b, i: (b, i, 0)),
                  pl.BlockSpec((1, tl, GDN_WIDTH), lambda b, i: (b, i, 0)),
                  pl.BlockSpec((1, tl, LANES), lambda b, i: (b, i, 0)),
                  pl.BlockSpec((1, GDN_CONV, QKV_WIDTH), lambda b, i: (l, 0, 0)),
                  pl.BlockSpec((1, LANES), lambda b, i: (0, 0)),
                  pl.BlockSpec((1, LANES), lambda b, i: (0, 0)),
                  pl.BlockSpec((1, GDN_DK), lambda b, i: (0, 0)),
                  pl.BlockSpec((1, GDN_CONV - 1, QKV_WIDTH), lambda b, i: (b, 0, 0)),
                  pl.BlockSpec((1, GDN_HEADS, GDN_DK, GDN_DK), lambda b, i: (b, 0, 0, 0))],
        out_specs=(pl.BlockSpec((1, tl, GDN_WIDTH), lambda b, i: (b, i, 0)),
                   pl.BlockSpec((1, GDN_HEADS, GDN_DK, GDN_DK), lambda b, i: (b, 0, 0, 0))),
        scratch_shapes=[pltpu.VMEM((sub + 8, QKV_WIDTH), F32),
                        pltpu.VMEM((sub, QKV_WIDTH), F32),
                        pltpu.VMEM((GDN_HEADS, GDN_DK, GDN_DK), F32)],
        compiler_params=_cp(("parallel", "arbitrary")),
        name="gdn_seq",
    )(qkv, z, ab, conv_w, alog, dtb, nw, conv0, s0)


GDN_STEP_ROWS = 8


def _gdn_step_kernel(qkv_ref, z_ref, ab_ref, cw_ref, alog_ref, dtb_ref, nw_ref, conv0_ref, s0_ref,
                     *rest):
    if len(rest) == 3:
        prev_ref, o_ref, s1_all = rest
        s1_all[0] = prev_ref[...]
        s1_ref = s1_all.at[1]
    else:
        o_ref, s1_all = rest
        s1_ref = s1_all
    nb = GDN_STEP_ROWS
    W = QKV_WIDTH
    conv = cw_ref[0:1, :] * conv0_ref[:, 0:W]
    conv = conv + cw_ref[1:2, :] * conv0_ref[:, W:2 * W]
    conv = conv + cw_ref[2:3, :] * conv0_ref[:, 2 * W:3 * W]
    conv = conv + cw_ref[3:4, :] * qkv_ref[...]
    qc = _silu(conv)
    ab = ab_ref[...]
    eg = jnp.exp(-jnp.exp(alog_ref[...]) * jax.nn.softplus(ab + dtb_ref[...]))
    beta = jax.nn.sigmoid(ab)
    eye = (lax.broadcasted_iota(jnp.int32, (GDN_DK, GDN_DK), 0)
           == lax.broadcasted_iota(jnp.int32, (GDN_DK, GDN_DK), 1)).astype(F32)
    for h in range(GDN_HEADS):
        q = _l2n(qc[:, h * GDN_DK:(h + 1) * GDN_DK]) * (GDN_DK ** -0.5)
        k = _l2n(qc[:, GDN_WIDTH + h * GDN_DK:GDN_WIDTH + (h + 1) * GDN_DK])
        v = qc[:, 2 * GDN_WIDTH + h * GDN_DK:2 * GDN_WIDTH + (h + 1) * GDN_DK]
        kT = _dot_nt(eye, k, HI)
        qT = _dot_nt(eye, q, HI)
        qk = jnp.sum(q * k, axis=-1, keepdims=True)
        for j in range(nb):
            S = s0_ref[0, j, h]
            kc = jnp.broadcast_to(kT[:, j:j + 1], (GDN_DK, GDN_DK))
            qcb = jnp.broadcast_to(qT[:, j:j + 1], (GDN_DK, GDN_DK))
            kS = jnp.sum(kc * S, axis=0, keepdims=True)
            qS = jnp.sum(qcb * S, axis=0, keepdims=True)
            egj = eg[j:j + 1, h:h + 1]
            bj = beta[j:j + 1, GDN_HEADS + h:GDN_HEADS + h + 1]
            v_new = bj * v[j:j + 1, :] - (bj * egj) * kS
            o = egj * qS + qk[j:j + 1, :] * v_new
            s1_ref[j, h] = S * egj + kc * v_new
            zh = z_ref[j:j + 1, h * GDN_DK:(h + 1) * GDN_DK]
            on = o * lax.rsqrt(jnp.mean(o * o, axis=-1, keepdims=True) + NORM_EPS) * nw_ref[...]
            o_ref[j:j + 1, h * GDN_DK:(h + 1) * GDN_DK] = on * _silu(zh)


def _gdn_step(qkv, z, ab, conv_w, alog, dtb, nw, conv0, s_all, l, prev):
    N = qkv.shape[0]
    nb = GDN_STEP_ROWS
    row = lambda w: pl.BlockSpec((nb, w), lambda i: (i, 0))
    const = lambda r, w: pl.BlockSpec((r, w), lambda i: (0, 0))
    sblk = (nb, GDN_HEADS, GDN_DK, GDN_DK)
    one = pl.BlockSpec(sblk, lambda i: (i, 0, 0, 0))
    ins = [qkv, z, ab, conv_w, alog, dtb, nw, conv0, s_all]
    in_specs = [row(QKV_WIDTH), row(GDN_WIDTH), row(LANES), const(GDN_CONV, QKV_WIDTH),
                const(1, LANES), const(1, LANES), const(1, GDN_DK), row(3 * QKV_WIDTH),
                pl.BlockSpec((1,) + sblk, lambda i: (l, i, 0, 0, 0))]
    if prev is None:
        s_shape, s_spec = jax.ShapeDtypeStruct((N,) + sblk[1:], F32), one
    else:
        assert DEPTH == 2 and l == 1
        ins.append(prev)
        in_specs.append(one)
        s_shape = jax.ShapeDtypeStruct((DEPTH, N) + sblk[1:], F32)
        s_spec = pl.BlockSpec((DEPTH,) + sblk, lambda i: (0, i, 0, 0, 0))
    return pl.pallas_call(
        _gdn_step_kernel,
        out_shape=(jax.ShapeDtypeStruct((N, GDN_WIDTH), F32), s_shape),
        grid=(N // nb,),
        in_specs=in_specs,
        out_specs=(row(GDN_WIDTH), s_spec),
        compiler_params=_cp(("parallel",)),
        name="gdn_step",
    )(*ins)


def _merge_kernel(yg_ref, og_ref, ga_ref, gb_ref, x_ref, gt_ref, wglu_ref, wgo_ref, wout_ref,
                  gf_ref, scf_ref, shf_ref, wr_ref,
                  xo_ref, h_ref, lg_ref, *scr, hi, chunked):
    if chunked:
        y_scr = scr[-1]
        scr = scr[:-1]
        nrow = y_scr.shape[1] // S5_T
        for k in range(SLABS):
            for t in range(S5_T):
                y_scr[k, pl.ds(t, nrow, stride=S5_T), :] = (
                    yg_ref[k, 0, :, t * LANES:(t + 1) * LANES].astype(F32))
        y = jnp.concatenate([y_scr[k] for k in range(SLABS)], axis=-1)
    else:
        y = jnp.concatenate([yg_ref[k, 0] for k in range(SLABS)], axis=-1)
    if hi:
        wglu, wgo, wout = wglu_ref[0], wgo_ref[0], wout_ref[0]
        mm = lambda a, w: _dot(a, w, HI)
    else:
        wglu_s, wgo_s, wout_s = scr

        @pl.when((pl.program_id(0) == 0) & (pl.program_id(1) == 0))
        def _():
            wglu_s[...] = wglu_ref[0].astype(BF16)
            wgo_s[...] = wgo_ref[0].astype(BF16)
            wout_s[...] = wout_ref[0].astype(BF16)

        wglu, wgo, wout = wglu_s[...], wgo_s[...], wout_s[...]
        mm = lambda a, w: _dot(a.astype(BF16), w)

    glu = mm(y, wglu)
    branch_a = glu[:, 0:D_MODEL] * jax.nn.sigmoid(glu[:, D_MODEL:])
    branch_b = mm(og_ref[0], wgo)
    merged = ga_ref[0].astype(F32) * branch_a + gb_ref[0].astype(F32) * branch_b
    out = mm(merged, wout)
    x = x_ref[0] + gt_ref[0, 0] * out
    xo_ref[0] = x
    ms = jnp.mean(x * x, axis=-1, keepdims=True)
    h = x * lax.rsqrt(ms + NORM_EPS) * gf_ref[0]
    h = h * (1.0 + scf_ref[0, 0]) + shf_ref[0, 0]
    h_ref[0] = h.astype(h_ref.dtype)
    lg_ref[0] = _dot_nt(wr_ref[0], h, HI)


def _merge(yg, og, ga, gb, x, mod, wglu, wgo, wout, gf, wr, *, l, tm, hi, chunked, h_dtype):
    B, L, D = x.shape
    row = lambda w: pl.BlockSpec((1, tm, w), lambda b, i: (b, i, 0))
    layer = lambda r, w, ll=l: pl.BlockSpec((1, r, w), lambda b, i: (ll, 0, 0))
    scratch = [] if hi else [pltpu.VMEM((S5_WIDTH, 2 * D), BF16), pltpu.VMEM((GDN_WIDTH, D), BF16),
                             pltpu.VMEM((D, D), BF16)]
    if chunked:
        scratch = scratch + [pltpu.VMEM((SLABS, tm, LANES), F32)]
        yg_spec = pl.BlockSpec((SLABS, 1, tm // S5_T, S5_T * LANES), lambda b, i: (0, b, i, 0))
    else:
        yg_spec = pl.BlockSpec((SLABS, 1, tm, LANES), lambda b, i: (0, b, i, 0))
    lg_shape = jax.ShapeDtypeStruct((B, N_EXPERTS, L), F32)
    lg_spec = pl.BlockSpec((1, N_EXPERTS, tm), lambda b, i: (b, 0, i))
    return pl.pallas_call(
        functools.partial(_merge_kernel, hi=hi, chunked=chunked),
        out_shape=(jax.ShapeDtypeStruct((B, L, D), F32),
                   jax.ShapeDtypeStruct((B, L, D), h_dtype),
                   lg_shape),
        grid=(B, L // tm),
        in_specs=[yg_spec,
                  row(GDN_WIDTH), row(D), row(D), row(D), _mod_spec(mod, l, 2, tm),
                  layer(S5_WIDTH, 2 * D), layer(GDN_WIDTH, D), layer(D, D),
                  layer(1, D), _mod_spec(mod, l, 4, tm), _mod_spec(mod, l, 3, tm),
                  layer(N_EXPERTS, D, l // 2)],
        out_specs=(row(D), row(D), lg_spec),
        scratch_shapes=scratch,
        compiler_params=_cp(("arbitrary", "arbitrary")),
        name="merge_out_proj",
    )(yg, og, ga, gb, x, mod, wglu, wgo, wout, gf, mod, mod, wr)


FF_TILE = 512
FFN_ROWS = 512


def _finish(x, gfin_ref, final):
    if not final:
        return x
    ms = jnp.mean(x * x, axis=-1, keepdims=True)
    return x * lax.rsqrt(ms + NORM_EPS) * gfin_ref[...]


def _ffn_kernel(h_ref, x_ref, gt_ref, wg_ref, wu_ref, wd_ref, gfin_ref, o_ref, acc_scr, *, hi, final):
    j = pl.program_id(2)
    if hi:
        wg, wu, wd = wg_ref[...], wu_ref[...], wd_ref[...]
        mm = lambda a, w: _dot(a, w, HI)
    else:
        wg, wu, wd = wg_ref[...].astype(BF16), wu_ref[...].astype(BF16), wd_ref[...].astype(BF16)
        mm = lambda a, w: _dot(a.astype(BF16), w)

    @pl.when(j == 0)
    def _():
        acc_scr[...] = jnp.zeros_like(acc_scr)

    tm = acc_scr.shape[0]
    sub = min(FFN_ROWS, tm)
    for s in range(tm // sub):
        rows = slice(s * sub, (s + 1) * sub)
        hb = h_ref[0, rows, :]
        act = _silu(mm(hb, wg)) * mm(hb, wu)
        acc_scr[rows, :] = acc_scr[rows, :] + mm(act, wd)

    @pl.when(j == pl.num_programs(2) - 1)
    def _():
        o_ref[0] = _finish(x_ref[0] + gt_ref[0, 0] * acc_scr[...], gfin_ref, final)


def _ffn(h, x, mod, w_gu, w_down, gfin, *, l, tm, hi, final):
    B, L, D = x.shape
    nj = D_FF // FF_TILE
    row = pl.BlockSpec((1, tm, D), lambda b, i, j: (b, i, 0))
    return pl.pallas_call(
        functools.partial(_ffn_kernel, hi=hi, final=final),
        out_shape=jax.ShapeDtypeStruct((B, L, D), F32),
        grid=(B, L // tm, nj),
        in_specs=[row, row, _mod_spec(mod, l, 5, tm),
                  pl.BlockSpec((D, FF_TILE), lambda b, i, j: (0, j)),
                  pl.BlockSpec((D, FF_TILE), lambda b, i, j: (0, nj + j)),
                  pl.BlockSpec((FF_TILE, D), lambda b, i, j: (j, 0)),
                  pl.BlockSpec((1, D), lambda b, i, j: (0, 0))],
        out_specs=row,
        scratch_shapes=[pltpu.VMEM((tm, D), F32)],
        compiler_params=_cp(("parallel", "parallel", "arbitrary")),
        name="ffn_dense",
    )(h, x, mod, w_gu, w_gu, w_down, gfin)


ROUTE_TM = 512
ROW_DMA_TM = 256
MOE_SUP = 2048
MOE_SUB = 512
MOE_FF_TILE = 512


def _route_kernel(lg_ref, br_ref, cnt0_ref, slot_ref, wt_ref, cnt_ref, carry_scr, *, cap):
    @pl.when((pl.program_id(0) == 0) & (pl.program_id(1) == 0))
    def _():
        carry_scr[...] = cnt0_ref[...]

    lg = lg_ref[0] + br_ref[...]
    tm = lg.shape[1]
    eidx = lax.broadcasted_iota(jnp.int32, lg.shape, 0)
    m1 = jnp.max(lg, axis=0, keepdims=True)
    i1 = jnp.min(jnp.where(lg == m1, eidx, N_EXPERTS), axis=0, keepdims=True)
    lg2 = jnp.where(eidx == i1, -jnp.inf, lg)
    m2 = jnp.max(lg2, axis=0, keepdims=True)
    i2 = jnp.min(jnp.where(lg2 == m2, eidx, N_EXPERTS), axis=0, keepdims=True)
    e2 = jnp.exp(m2 - m1)
    wt_ref[0, 0:1, :] = 1.0 / (1.0 + e2)
    wt_ref[0, 1:2, :] = e2 / (1.0 + e2)
    sel1 = eidx == i1
    sel2 = eidx == i2
    oh = jnp.where(sel1 | sel2, 1.0, 0.0)
    before = (lax.broadcasted_iota(jnp.int32, (tm, tm), 0)
              < lax.broadcasted_iota(jnp.int32, (tm, tm), 1)).astype(BF16)
    rank = carry_scr[:, 0:1] + _dot(oh.astype(BF16), before)
    r1 = jnp.sum(jnp.where(sel1, rank, 0.0), axis=0, keepdims=True).astype(jnp.int32)
    r2 = jnp.sum(jnp.where(sel2, rank, 0.0), axis=0, keepdims=True).astype(jnp.int32)
    slot_ref[0, 0:1, :] = i1 * cap + r1
    slot_ref[0, 1:2, :] = i2 * cap + r2
    carry_scr[...] = carry_scr[...] + jnp.sum(oh, axis=1, keepdims=True)
    cnt_ref[...] = carry_scr[...]


def _route_slots(lgT, b_r, cnt0, cap):
    B, E, L = lgT.shape
    tm = min(ROUTE_TM, L)
    return pl.pallas_call(
        functools.partial(_route_kernel, cap=cap),
        out_shape=(jax.ShapeDtypeStruct((B, 2, L), jnp.int32),
                   jax.ShapeDtypeStruct((B, 2, L), F32),
                   jax.ShapeDtypeStruct((E, LANES), F32)),
        grid=(B, L // tm),
        in_specs=[pl.BlockSpec((1, E, tm), lambda b, i: (b, 0, i)),
                  pl.BlockSpec((E, 1), lambda b, i: (0, 0)),
                  pl.BlockSpec((E, LANES), lambda b, i: (0, 0))],
        out_specs=(pl.BlockSpec((1, 2, tm), lambda b, i: (b, 0, i)),
                   pl.BlockSpec((1, 2, tm), lambda b, i: (b, 0, i)),
                   pl.BlockSpec((E, LANES), lambda b, i: (0, 0))),
        scratch_shapes=[pltpu.VMEM((E, LANES), F32)],
        compiler_params=_cp(("arbitrary", "arbitrary")),
        name="moe_route",
    )(lgT, b_r.reshape(E, 1), cnt0)


def _row_copy(src, dst, sem):
    return pltpu.make_async_copy(src, dst, sem)


def _slot_rows_kernel(start_ref, per_ref, code_ref, row_ref, *, cap):
    code = code_ref[...]
    shift = cap.bit_length() - 1
    e = lax.shift_right_logical(code, shift)
    r = code & (cap - 1)
    start = jnp.zeros_like(code)
    per = jnp.ones_like(code)
    for k in range(N_EXPERTS):
        start = jnp.where(e == k, start_ref[k], start)
        per = jnp.where(e == k, per_ref[k], per)
    q = jnp.floor((r.astype(F32) + 0.5) / per.astype(F32)).astype(jnp.int32)
    row_ref[...] = start + q * MOE_SUP + (r - q * per)


def _slot_rows(start, per, codes, cap):
    B, _, L = codes.shape
    tm = min(ROUTE_TM, L)
    spec = pl.BlockSpec((1, 2, tm), lambda b, i, st, pe: (b, 0, i))
    return pl.pallas_call(
        functools.partial(_slot_rows_kernel, cap=cap),
        out_shape=jax.ShapeDtypeStruct(codes.shape, jnp.int32),
        grid_spec=pltpu.PrefetchScalarGridSpec(
            num_scalar_prefetch=2, grid=(B, L // tm), in_specs=[spec], out_specs=spec),
        compiler_params=_cp(("parallel", "parallel")),
        name="moe_slot_rows",
    )(start, per, codes)


def _zeros_kernel(o_ref):
    o_ref[...] = jnp.zeros_like(o_ref)


def _zero_rows(n_rows, width):
    return pl.pallas_call(
        _zeros_kernel,
        out_shape=jax.ShapeDtypeStruct((n_rows, width), F32),
        grid=(n_rows // MOE_SUP,),
        out_specs=pl.BlockSpec((MOE_SUP, width), lambda i: (i, 0)),
        compiler_params=_cp(("parallel",)),
        name="moe_zero_rows",
    )()


def _dispatch_kernel(row_ref, h_ref, xs_in_ref, xs_ref, hbuf, sem):
    del xs_in_ref
    tm = h_ref.shape[1]
    t = pl.program_id(0) * pl.num_programs(1) + pl.program_id(1)
    last = pl.num_programs(0) * pl.num_programs(1) - 1
    slot = t % 2
    hbuf[slot] = h_ref[0]

    def issue(r, _):
        for k in range(2):
            row = row_ref[0, k, r]
            _row_copy(hbuf.at[slot, pl.ds(r, 1), :], xs_ref.at[pl.ds(row, 1), :], sem.at[slot]).start()
        return 0

    lax.fori_loop(0, tm, issue, 0, unroll=8)

    def drain(sl):
        for k in range(2):
            _row_copy(hbuf.at[sl], xs_ref.at[pl.ds(0, tm), :], sem.at[sl]).wait()

    @pl.when(t > 0)
    def _():
        drain(1 - slot)

    @pl.when(t == last)
    def _():
        drain(slot)


def _dispatch(rows, h, xs):
    B, L, D = h.shape
    n_rows = xs.shape[0]
    tm = min(ROW_DMA_TM, L)
    return pl.pallas_call(
        _dispatch_kernel,
        out_shape=jax.ShapeDtypeStruct((n_rows, D), F32),
        grid=(B, L // tm),
        in_specs=[pl.BlockSpec((1, 2, tm), lambda b, i: (b, 0, i), memory_space=pltpu.SMEM),
                  pl.BlockSpec((1, tm, D), lambda b, i: (b, i, 0)),
                  pl.BlockSpec(memory_space=pl.ANY)],
        out_specs=pl.BlockSpec(memory_space=pl.ANY),
        scratch_shapes=[pltpu.VMEM((2, tm, D), F32), pltpu.SemaphoreType.DMA((2,))],
        input_output_aliases={2: 0},
        compiler_params=_cp(("arbitrary", "arbitrary")),
        name="moe_dispatch",
    )(rows, h, xs)


def _moe_grp_kernel(ge_ref, gn_ref, x_ref, wg_ref, wu_ref, wd_ref, y_ref, xb_scr):
    g = pl.program_id(0)
    j = pl.program_id(1)
    nsub = gn_ref[g]
    wg = wg_ref[0].astype(BF16)
    wu = wu_ref[0].astype(BF16)
    wd = wd_ref[0].astype(BF16)
    nblk = MOE_SUP // MOE_SUB

    @pl.when(j == 0)
    def _():
        xb_scr[...] = x_ref[...].astype(BF16)
        y_ref[...] = jnp.zeros_like(y_ref)

    def block(s):
        rows = slice(s * MOE_SUB, (s + 1) * MOE_SUB)
        xb = xb_scr[rows, :]
        act = _silu(_dot(xb, wg)) * _dot(xb, wu)
        y_ref[rows, :] = y_ref[rows, :] + _dot(act.astype(BF16), wd)

    for n in range(1, nblk + 1):
        @pl.when(nsub == n)
        def _():
            for s in range(n):
                block(s)


def _moe_groups(counts, n_groups):
    nsup = (counts + MOE_SUP - 1) // MOE_SUP
    div = jnp.maximum(nsup, 1)
    per = jnp.maximum(((counts + div - 1) // div + MOE_SUB - 1) // MOE_SUB * MOE_SUB, MOE_SUB)
    ends = jnp.cumsum(nsup)
    first = ends - nsup
    total = ends[-1]
    g = jnp.arange(n_groups, dtype=jnp.int32)
    gc = jnp.minimum(g, total - 1)
    e_of = jnp.minimum(jnp.sum((gc[:, None] >= ends[None, :]).astype(jnp.int32), axis=1), N_EXPERTS - 1)
    left = jnp.minimum(counts[e_of] - (gc - first[e_of]) * per[e_of], per[e_of])
    nsub = jnp.clip((left + MOE_SUB - 1) // MOE_SUB, 0, MOE_SUP // MOE_SUB)
    gn = jnp.where(g < total, nsub, 0).astype(jnp.int32)
    return e_of, gn, (first * MOE_SUP).astype(jnp.int32), per.astype(jnp.int32)


def _moe_grouped(xs, ge, gn, w_gu, w_down):
    D = xs.shape[1]
    nj = D_FF // MOE_FF_TILE
    ng = xs.shape[0] // MOE_SUP
    jj = lambda j, gn, g: jnp.where(gn[g] > 0, j, nj - 1)
    return pl.pallas_call(
        _moe_grp_kernel,
        out_shape=jax.ShapeDtypeStruct(xs.shape, F32),
        grid_spec=pltpu.PrefetchScalarGridSpec(
            num_scalar_prefetch=2,
            grid=(ng, nj),
            in_specs=[pl.BlockSpec((MOE_SUP, D), lambda g, j, ge, gn: (g, 0)),
                      pl.BlockSpec((1, D, MOE_FF_TILE), lambda g, j, ge, gn: (ge[g], 0, jj(j, gn, g))),
                      pl.BlockSpec((1, D, MOE_FF_TILE), lambda g, j, ge, gn: (ge[g], 0, nj + jj(j, gn, g))),
                      pl.BlockSpec((1, MOE_FF_TILE, D), lambda g, j, ge, gn: (ge[g], jj(j, gn, g), 0))],
            out_specs=pl.BlockSpec((MOE_SUP, D), lambda g, j, ge, gn: (g, 0)),
            scratch_shapes=[pltpu.VMEM((MOE_SUP, D), BF16)],
        ),
        compiler_params=_cp(("arbitrary", "arbitrary")),
        name="moe_experts",
    )(ge, gn, xs, w_gu, w_gu, w_down)


def _combine_kernel(row_ref, next_ref, w_ref, x_ref, gt_ref, gfin_ref, ys_ref, o_ref, g_scr, sem, *, final):
    tm = x_ref.shape[1]
    t = pl.program_id(0) * pl.num_programs(1) + pl.program_id(1)
    last = pl.num_programs(0) * pl.num_programs(1) - 1
    slot = t % 2

    def gather(rows, sl):
        def issue(r, _):
            for k in range(2):
                row = rows[0, k, r]
                _row_copy(ys_ref.at[pl.ds(row, 1), :], g_scr.at[sl, k, pl.ds(r, 1), :], sem.at[sl]).start()
            return 0

        lax.fori_loop(0, tm, issue, 0, unroll=8)

    @pl.when(t == 0)
    def _():
        gather(row_ref, slot)

    @pl.when(t < last)
    def _():
        gather(next_ref, 1 - slot)

    for k in range(2):
        _row_copy(ys_ref.at[pl.ds(0, tm), :], g_scr.at[slot, k], sem.at[slot]).wait()
    w = w_ref[0]
    f = w[:, 0:1] * g_scr[slot, 0] + w[:, 1:2] * g_scr[slot, 1]
    o_ref[0] = _finish(x_ref[0] + gt_ref[0, 0] * f, gfin_ref, final)


def _combine(rows, wts, x, mod, gfin, ys, *, l, final):
    B, L, D = x.shape
    tm = min(ROW_DMA_TM, L)
    gt_spec = _mod_spec(mod, l, 5, tm)
    row = pl.BlockSpec((1, tm, D), lambda b, i: (b, i, 0))
    nl = L // tm

    def next_block(b, i):
        t1 = jnp.minimum(b * nl + i + 1, B * nl - 1)
        return (t1 // nl, 0, t1 % nl)

    return pl.pallas_call(
        functools.partial(_combine_kernel, final=final),
        out_shape=jax.ShapeDtypeStruct((B, L, D), F32),
        grid=(B, L // tm),
        in_specs=[pl.BlockSpec((1, 2, tm), lambda b, i: (b, 0, i), memory_space=pltpu.SMEM),
                  pl.BlockSpec((1, 2, tm), next_block, memory_space=pltpu.SMEM),
                  pl.BlockSpec((1, tm, 2), lambda b, i: (b, i, 0)),
                  row, gt_spec,
                  pl.BlockSpec((1, D), lambda b, i: (0, 0)),
                  pl.BlockSpec(memory_space=pl.ANY)],
        out_specs=row,
        scratch_shapes=[pltpu.VMEM((2, 2, tm, D), F32), pltpu.SemaphoreType.DMA((2,))],
        compiler_params=_cp(("arbitrary", "arbitrary")),
        name="moe_combine",
    )(rows, rows, wts.transpose(0, 2, 1), x, mod, gfin, ys)


def _moe_routed(groups, b_r, w_gu, w_down, gfin, *, l, final):
    D = groups[0][1].shape[-1]
    n_tok = sum(g[1].shape[0] * g[1].shape[1] for g in groups)
    cap = 1 << (n_tok - 1).bit_length()
    n_groups = 2 * n_tok // MOE_SUP + N_EXPERTS
    cnt = jnp.zeros((N_EXPERTS, LANES), F32)
    routed = []
    for _, _, _, lgT in groups:
        codes, wts, cnt = _route_slots(lgT, b_r, cnt, cap)
        routed.append((codes, wts))
    ge, gn, start, per = _moe_groups(cnt[:, 0].astype(jnp.int32), n_groups)
    rows = [_slot_rows(start, per, codes, cap) for codes, _ in routed]
    xs = _zero_rows(n_groups * MOE_SUP, D)
    for (h, _, _, _), r in zip(groups, rows):
        xs = _dispatch(r, h, xs)
    ys = _moe_grouped(xs, ge, gn, w_gu, w_down)
    return [_combine(r, wts, x, mod, gfin, ys, l=l, final=final)
            for (_, x, mod, _), (_, wts), r in zip(groups, routed, rows)]


def _pad_lanes(v):
    return jnp.pad(v.reshape(1, -1), ((0, 0), (0, LANES - v.shape[-1])))


def _mixer_layer(x, mod, states, p, s5m, l, prev_sg, *, seq):
    B, L, D = x.shape
    hi = not seq
    s5r0, s5i0, sg0, sc0 = states
    w_in, w_gates, w_ab = p['w_in_seq' if seq else 'w_in']
    if seq:
        u, qkv, z, ga, gb, ab = _proj_seq(x, p['g_mix'], mod, w_in, w_gates, w_ab, l=l, tm=min(1024, L))
    else:
        u, qkv, z, ga, gb, ab = _proj(x, p['g_mix'], mod, w_in, w_gates, w_ab, l=l, tm=L)
    alog = _pad_lanes(p['gdn_a_log'][l])
    dtb = _pad_lanes(p['gdn_dt_bias'][l])
    nw = p['gdn_norm_w'][l].reshape(1, GDN_DK)
    if seq:
        yg, sfin = _s5_seq(u, s5m['be'], s5m['tp'], s5m['cpm'], s5m['pt'],
                           jnp.zeros((SLABS, B, 1, 2 * SLAB_STATE), F32), s5m['dsk'][l], l)
        sfin = sfin.reshape(SLABS, B, 2, SLAB_STATE).transpose(2, 1, 0, 3)
        sr = sfin[0].reshape(B, S5_GROUPS, S5_STATE)
        si = sfin[1].reshape(B, S5_GROUPS, S5_STATE)
        og, sg = _gdn_seq(qkv, z, ab, p['gdn_conv_w'], alog, dtb, nw,
                          jnp.zeros((B, GDN_CONV - 1, QKV_WIDTH), F32),
                          jnp.zeros((B, GDN_HEADS, GDN_DK, GDN_DK), F32), l)
        cb = qkv[:, L - (GDN_CONV - 1):, :].astype(F32)
    else:
        n = L
        s0 = jnp.concatenate([s5r0[l].reshape(n, SLABS, SLAB_STATE),
                              s5i0[l].reshape(n, SLABS, SLAB_STATE)], axis=-1).transpose(1, 0, 2)
        yg, s1 = _s5_step(u.reshape(SLABS, n, LANES), s5m['bst'], s5m['cpe'], s5m['a1'],
                          s0, s5m['d1'][l], l)
        yg = yg.reshape(SLABS, 1, n, LANES)
        s1 = s1.transpose(1, 0, 2)
        sr = s1[:, :, :SLAB_STATE].reshape(n, S5_GROUPS, S5_STATE)
        si = s1[:, :, SLAB_STATE:].reshape(n, S5_GROUPS, S5_STATE)
        og, sg = _gdn_step(qkv.reshape(n, QKV_WIDTH), z.reshape(n, GDN_WIDTH), ab.reshape(n, LANES),
                           p['gdn_conv_w'][l], alog, dtb, nw,
                           sc0[l].reshape(n, (GDN_CONV - 1) * QKV_WIDTH), sg0, l, prev_sg)
        og = og.reshape(1, n, GDN_WIDTH)
        cb = jnp.concatenate([sc0[l][:, 1:, :], qkv.reshape(n, 1, QKV_WIDTH)], axis=1)
    x, h, lgT = _merge(yg, og, ga, gb, x, mod, p['w_s5_glu'], p['w_gdn_out'], p['w_out'],
                       p['g_ffn'], p['w_router'], l=l, tm=min(512, L), hi=hi, chunked=seq,
                       h_dtype=BF16 if (seq and l % 2 == 0) else F32)
    return x, h, lgT, (sr, si, sg, cb)


def kernel(x_prompt, x_sample, c_prompt, c_sample, state_s5_re, state_s5_im, state_gdn, state_conv,
           g_mix, g_ffn, g_final, w_ada, b_ada, w_in, s5_lambda_re, s5_lambda_im, s5_log_dt,
           s5_b_re, s5_b_im, s5_c_re, s5_c_im, s5_d, w_s5_glu, gdn_conv_w, gdn_a_log, gdn_dt_bias,
           gdn_norm_w, w_gdn_out, w_out, w_ffn_gate_up, w_ffn_down, w_router, b_router,
           w_exp_gate_up, w_exp_down):
    def in_proj_parts(w):
        return w, w[:, :, 2568:], jnp.pad(w[:, :, 2560:2568], ((0, 0), (0, 0), (0, LANES - 8)))

    D_ = x_prompt.shape[-1]
    p = dict(g_mix=g_mix.reshape(DEPTH, 1, D_), g_ffn=g_ffn.reshape(DEPTH, 1, D_), w_s5_glu=w_s5_glu,
             gdn_conv_w=gdn_conv_w, gdn_a_log=gdn_a_log, gdn_dt_bias=gdn_dt_bias,
             gdn_norm_w=gdn_norm_w, w_gdn_out=w_gdn_out, w_out=w_out,
             w_router=w_router.transpose(0, 2, 1), w_in=in_proj_parts(w_in),
             w_in_seq=in_proj_parts(w_in.astype(BF16)))
    nbp, L, D = x_prompt.shape
    nbs = x_sample.shape[0]

    mod = _ada(jnp.concatenate([c_prompt, c_sample], axis=0), w_ada, b_ada)
    mod_p = mod[:, :nbp].reshape(DEPTH, nbp, 1, 6 * D)
    mod_s = mod[:, nbp:].reshape(DEPTH, 1, nbs, 6 * D)

    seg = L // S5_T // 8
    be, bst, cpe, cpm, pt, a1 = _s5_prep(s5_lambda_re, s5_lambda_im, s5_log_dt, s5_b_re, s5_b_im,
                                         s5_c_re, s5_c_im, seg)
    d1 = [s5_d[l].reshape(SLABS, 1, LANES) for l in range(DEPTH)]
    s5m = dict(be=be, bst=bst, cpe=cpe, cpm=cpm, pt=pt, a1=a1, tp=_toep(bst, cpe), d1=d1,
               dsk=[jnp.tile(d, (1, 1, S5_T)) for d in d1])

    xs_ = [x_prompt, x_sample.reshape(1, nbs, D)]
    mods = [mod_p, mod_s]
    states = [(None, None, None, None), (state_s5_re, state_s5_im, state_gdn, state_conv)]
    outs = [[], []]
    gfin = g_final.reshape(1, D)
    for l in range(DEPTH):
        final = l == DEPTH - 1
        mixed = []
        for gi, seq in enumerate((True, False)):
            prev_sg = outs[gi][0][2] if (not seq and final and DEPTH == 2) else None
            x, h, lgT, st = _mixer_layer(xs_[gi], mods[gi], states[gi], p, s5m, l, prev_sg, seq=seq)
            outs[gi].append(st)
            mixed.append((h, x, mods[gi], lgT))
        if l % 2 == 0:
            wgu, wdn = w_ffn_gate_up[l // 2], w_ffn_down[l // 2]
            xs_ = [_ffn(h, x, mod_g, wgu if gi else wgu.astype(BF16), wdn if gi else wdn.astype(BF16), gfin,
                        l=l, tm=min(1024, x.shape[1]), hi=(gi == 1), final=final)
                   for gi, (h, x, mod_g, _) in enumerate(mixed)]
        else:
            xs_ = _moe_routed(mixed, b_router[l // 2], w_exp_gate_up[l // 2], w_exp_down[l // 2], gfin,
                              l=l, final=final)
    y_p, y_s = xs_
    st_p = [jnp.stack([o[i] for o in outs[0]]) for i in range(4)]
    st_s = [outs[1][-1][2] if (i == 2 and DEPTH == 2) else jnp.stack([o[i] for o in outs[1]])
            for i in range(4)]
    return (y_p, y_s.reshape(nbs, 1, D), st_p[0], st_p[1], st_p[2], st_p[3],
            st_s[0], st_s[1], st_s[2], st_s[3])
```

```python
import functools

import jax
import jax.numpy as jnp
from jax import lax
from jax.experimental import pallas as pl
from jax.experimental.pallas import tpu as pltpu

F32 = jnp.float32
BF16 = jnp.bfloat16
HI = lax.Precision.HIGHEST

D_MODEL = 1024
DEPTH = 2
S5_WIDTH = 512
S5_GROUP = 16
S5_GROUPS = 32
S5_STATE = 64
GDN_HEADS = 4
GDN_DK = 128
GDN_WIDTH = 512
GDN_CONV = 4
QKV_WIDTH = 1536
D_FF = 3584
N_EXPERTS = 8
NORM_EPS = 1e-6
L2_EPS = 1e-6

LANES = 128
SLABS = S5_WIDTH // LANES
SLAB_STATE = (S5_GROUPS // SLABS) * S5_STATE
S5_T = 8
S5_SEG_PAD = 4
GDN_C = 128
GDN_BLOCK = 512
GDN_SUB = 256
VMEM_LIMIT = 56 * 1024 * 1024


def _cp(sem, vmem=VMEM_LIMIT):
    return pltpu.CompilerParams(dimension_semantics=sem, vmem_limit_bytes=vmem)


def _dot(a, b, prec=None):
    return jnp.dot(a, b, precision=prec, preferred_element_type=F32)


def _dotb(a, b):
    return jnp.dot(a.astype(BF16), b.astype(BF16), preferred_element_type=F32)


def _dot_nt(a, b, prec=None):
    return lax.dot_general(a, b, (((1,), (1,)), ((), ())), precision=prec,
                           preferred_element_type=F32)


def _dot_tn(a, b, prec=None):
    return lax.dot_general(a, b, (((0,), (0,)), ((), ())), precision=prec,
                           preferred_element_type=F32)


def _silu(x):
    return x * jax.nn.sigmoid(x)


def _ada_kernel(c_ref, w_ref, b_ref, o_ref):
    cs = _silu(c_ref[...])
    o_ref[0] = _dot(cs, w_ref[0], HI) + b_ref[0]


def _ada(c_all, w_ada, b_ada):
    n = c_all.shape[0]
    tn = 1536
    return pl.pallas_call(
        _ada_kernel,
        out_shape=jax.ShapeDtypeStruct((DEPTH, n, 6 * D_MODEL), F32),
        grid=(DEPTH, 6 * D_MODEL // tn),
        in_specs=[pl.BlockSpec((n, D_MODEL), lambda l, j: (0, 0)),
                  pl.BlockSpec((1, D_MODEL, tn), lambda l, j: (l, 0, j)),
                  pl.BlockSpec((1, 1, tn), lambda l, j: (l, 0, j))],
        out_specs=pl.BlockSpec((1, n, tn), lambda l, j: (l, 0, j)),
        compiler_params=_cp(("parallel", "parallel")),
        name="ada_mod",
    )(c_all, w_ada, b_ada.reshape(DEPTH, 1, 6 * D_MODEL))


def _proj_kernel(x_ref, g_ref, sc_ref, sh_ref, w_ref, wg_ref, wab_ref,
                 u_ref, qkv_ref, z_ref, ga_ref, gb_ref, ab_ref, h_scr):
    j = pl.program_id(2)

    @pl.when(j == 0)
    def _():
        x = x_ref[0]
        ms = jnp.mean(x * x, axis=-1, keepdims=True)
        xn = x * lax.rsqrt(ms + NORM_EPS) * g_ref[0]
        h_scr[...] = (xn * (1.0 + sc_ref[0, 0]) + sh_ref[0, 0]).astype(h_scr.dtype)

    def mm(w):
        return _dot(h_scr[...], w, HI)

    @pl.when(j == 0)
    def _():
        res = mm(w_ref[0])
        for k in range(SLABS):
            u_ref[k, 0] = res[:, k * LANES:(k + 1) * LANES]

    @pl.when((j >= 1) & (j <= 3))
    def _():
        qkv_ref[0] = mm(w_ref[0])

    @pl.when(j == 4)
    def _():
        z_ref[0] = mm(w_ref[0])

    @pl.when((j == 5) | (j == 6))
    def _():
        ga_ref[0] = jax.nn.sigmoid(mm(wg_ref[0]))

    @pl.when((j == 7) | (j == 8))
    def _():
        gb_ref[0] = jax.nn.sigmoid(mm(wg_ref[0]))

    @pl.when(j == 9)
    def _():
        ab_ref[0] = mm(wab_ref[0])


def _mod_spec(mod, l, chunk, tm):
    per_row = mod.shape[2] != 1
    D = mod.shape[3] // 6

    def index(b, i, *_):
        return (l, b, i if per_row else 0, chunk)

    return pl.BlockSpec((1, 1, tm if per_row else 1, D), index)


def _proj(x, g, mod, w_in, w_gates, w_ab, *, l, tm):
    B, L, D = x.shape
    tn = 512
    clampi = lambda j, lo, n: jnp.clip(j - lo, 0, n - 1)
    outs = pl.pallas_call(
        _proj_kernel,
        out_shape=(jax.ShapeDtypeStruct((SLABS, B, L, LANES), F32),
                   jax.ShapeDtypeStruct((B, L, QKV_WIDTH), F32),
                   jax.ShapeDtypeStruct((B, L, GDN_WIDTH), F32),
                   jax.ShapeDtypeStruct((B, L, D), F32),
                   jax.ShapeDtypeStruct((B, L, D), F32),
                   jax.ShapeDtypeStruct((B, L, LANES), F32)),
        grid=(B, L // tm, 10),
        in_specs=[pl.BlockSpec((1, tm, D), lambda b, i, j: (b, i, 0)),
                  pl.BlockSpec((1, 1, D), lambda b, i, j: (l, 0, 0)),
                  _mod_spec(mod, l, 1, tm),
                  _mod_spec(mod, l, 0, tm),
                  pl.BlockSpec((1, D, tn), lambda b, i, j: (l, 0, jnp.minimum(j, 4))),
                  pl.BlockSpec((1, D, tn), lambda b, i, j: (l, 0, clampi(j, 5, 4))),
                  pl.BlockSpec((1, D, LANES), lambda b, i, j: (l, 0, 0))],
        out_specs=(pl.BlockSpec((SLABS, 1, tm, LANES), lambda b, i, j: (0, b, i, 0)),
                   pl.BlockSpec((1, tm, tn), lambda b, i, j: (b, i, clampi(j, 1, 3))),
                   pl.BlockSpec((1, tm, tn), lambda b, i, j: (b, i, 0)),
                   pl.BlockSpec((1, tm, tn), lambda b, i, j: (b, i, clampi(j, 5, 2))),
                   pl.BlockSpec((1, tm, tn), lambda b, i, j: (b, i, clampi(j, 7, 2))),
                   pl.BlockSpec((1, tm, LANES), lambda b, i, j: (b, i, 0))),
        scratch_shapes=[pltpu.VMEM((tm, D), F32)],
        compiler_params=_cp(("parallel", "parallel", "arbitrary")),
        name="norm_in_proj",
    )(x, g, mod, mod, w_in, w_gates, w_ab)
    return outs


def _proj_seq_kernel(x_ref, g_ref, sc_ref, sh_ref, w_ref, wg_ref, wab_ref,
                     u_ref, qkv_ref, z_ref, ga_ref, gb_ref, ab_ref, us_scr):
    x = x_ref[0]
    ms = jnp.mean(x * x, axis=-1, keepdims=True)
    xn = x * lax.rsqrt(ms + NORM_EPS) * g_ref[0]
    h = (xn * (1.0 + sc_ref[0, 0]) + sh_ref[0, 0]).astype(BF16)
    res = _dot(h, w_ref[0, :, 0:S5_WIDTH])
    nrow = res.shape[0] // S5_T
    for k in range(SLABS):
        us_scr[...] = res[:, k * LANES:(k + 1) * LANES]
        for t in range(S5_T):
            u_ref[k, 0, :, t * LANES:(t + 1) * LANES] = (
                us_scr[pl.ds(t, nrow, stride=S5_T), :].astype(u_ref.dtype))
    c0 = S5_WIDTH
    qkv_ref[0] = _dot(h, w_ref[0, :, c0:c0 + QKV_WIDTH]).astype(qkv_ref.dtype)
    c0 += QKV_WIDTH
    z_ref[0] = _dot(h, w_ref[0, :, c0:c0 + GDN_WIDTH]).astype(z_ref.dtype)
    D = x.shape[-1]
    ga_ref[0] = jax.nn.sigmoid(_dot(h, wg_ref[0, :, 0:D])).astype(ga_ref.dtype)
    gb_ref[0] = jax.nn.sigmoid(_dot(h, wg_ref[0, :, D:2 * D])).astype(gb_ref.dtype)
    ab_ref[0] = _dot(h, wab_ref[0])


def _proj_seq(x, g, mod, w_in, w_gates, w_ab, *, l, tm):
    B, L, D = x.shape
    n_main = S5_WIDTH + QKV_WIDTH + GDN_WIDTH
    row = lambda w: pl.BlockSpec((1, tm, w), lambda b, i: (b, i, 0))
    return pl.pallas_call(
        _proj_seq_kernel,
        out_shape=(jax.ShapeDtypeStruct((SLABS, B, L // S5_T, S5_T * LANES), BF16),
                   jax.ShapeDtypeStruct((B, L, QKV_WIDTH), BF16),
                   jax.ShapeDtypeStruct((B, L, GDN_WIDTH), BF16),
                   jax.ShapeDtypeStruct((B, L, D), BF16),
                   jax.ShapeDtypeStruct((B, L, D), BF16),
                   jax.ShapeDtypeStruct((B, L, LANES), F32)),
        grid=(B, L // tm),
        in_specs=[row(D),
                  pl.BlockSpec((1, 1, D), lambda b, i: (l, 0, 0)),
                  _mod_spec(mod, l, 1, tm),
                  _mod_spec(mod, l, 0, tm),
                  pl.BlockSpec((1, D, n_main), lambda b, i: (l, 0, 0)),
                  pl.BlockSpec((1, D, 2 * D), lambda b, i: (l, 0, 0)),
                  pl.BlockSpec((1, D, LANES), lambda b, i: (l, 0, 0))],
        out_specs=(pl.BlockSpec((SLABS, 1, tm // S5_T, S5_T * LANES), lambda b, i: (0, b, i, 0)),
                   row(QKV_WIDTH), row(GDN_WIDTH), row(D), row(D), row(LANES)),
        scratch_shapes=[pltpu.VMEM((tm, LANES), F32)],
        compiler_params=_cp(("parallel", "parallel")),
        name="norm_in_proj_seq",
    )(x, g, mod, mod, w_in, w_gates, w_ab)


GROUPS_PER_SLAB = S5_GROUPS // SLABS


def _s5_prep_kernel(lrb, lib, dtb, bre, bim, lrc, lic, dtc, cre, cim, lrn, lin, dtn,
                    be_ref, bst_ref, c0_ref, cpm_ref, tp_ref, pt_ref, a1_ref, cpe_scr, *, seg):
    W = SLAB_STATE

    def disc(lr, li, ldt):
        dt = jnp.exp(ldt)
        mag = jnp.exp(lr * dt)
        return mag * jnp.cos(li * dt), mag * jnp.sin(li * dt)

    def cmul(xr, xi, yr, yi):
        return xr * yr - xi * yi, xr * yi + xi * yr

    lr, li = lrb[0], lib[0]
    ar, ai = disc(lr, li, dtb[0])
    den = lr * lr + li * li
    nr = ar - 1.0
    kr = (nr * lr + ai * li) / den
    ki = (ai * lr - nr * li) / den
    br, bi = bre[0], bim[0]
    bbr = kr * br - ki * bi
    bbi = kr * bi + ki * br
    rgrp = lax.broadcasted_iota(jnp.int32, (LANES, LANES), 0) // S5_GROUP
    lane_hi = lax.broadcasted_iota(jnp.int32, (LANES, LANES), 1) // S5_STATE
    pr, pi = jnp.ones_like(ar), jnp.zeros_like(ar)
    for d in range(S5_T):
        t = S5_T - 1 - d
        for ri, val in enumerate(cmul(pr, pi, bbr, bbi)):
            two = jnp.concatenate([val, val], axis=1)
            for m in range(GROUPS_PER_SLAB // 2):
                tile = jnp.where(rgrp == 2 * m + lane_hi, two, 0.0)
                c0 = ri * W + m * LANES
                be_ref[0, 0, t * LANES:(t + 1) * LANES, c0:c0 + LANES] = tile.astype(BF16)
                if d == 0:
                    bst_ref[0, 0, :, c0:c0 + LANES] = tile
        pr, pi = cmul(pr, pi, ar, ai)

    ar, ai = disc(lrc[0], lic[0], dtc[0])
    cr, ci = cre[0], cim[0]
    own = (lax.broadcasted_iota(jnp.int32, (W, LANES), 0) // S5_STATE
           == lax.broadcasted_iota(jnp.int32, (W, LANES), 1) // S5_GROUP)
    pr, pi = jnp.ones_like(ar), jnp.zeros_like(ar)
    for d in range(S5_T + 1):
        vr, vi = cmul(cr, ci, pr, pi)
        for ri, val in enumerate((vr, -vi)):
            tile = jnp.where(own, val, 0.0)
            cpe_scr[d, ri * W:(ri + 1) * W, :] = tile
            if d == 0:
                c0_ref[0, 0, ri * W:(ri + 1) * W, :] = tile
            if d >= 1:
                cpm_ref[0, 0, ri * W:(ri + 1) * W, (d - 1) * LANES:d * LANES] = tile.astype(BF16)
        pr, pi = cmul(pr, pi, ar, ai)

    bst = bst_ref[0, 0]
    lag = [_dot(bst, cpe_scr[d], HI).astype(BF16) for d in range(S5_T)]
    for dd in range(S5_T // 2):
        tp_ref[0, 0, dd, 0:LANES, 0:LANES] = lag[2 * dd]
        tp_ref[0, 0, dd, LANES:, LANES:] = lag[2 * dd]
        tp_ref[0, 0, dd, 0:LANES, LANES:] = lag[2 * dd + 1]
        tp_ref[0, 0, dd, LANES:, 0:LANES] = lag[2 * dd - 1] if dd else jnp.zeros((LANES, LANES), BF16)

    ar, ai = disc(lrn[0, 0], lin[0, 0], dtn[0, 0])
    a1_ref[0, 0, :, 0:W] = ar
    a1_ref[0, 0, :, W:2 * W] = ai
    tr, ti = ar, ai
    for _ in range(S5_T - 1):
        tr, ti = cmul(tr, ti, ar, ai)
    pr, pi = jnp.ones_like(ar), jnp.zeros_like(ar)
    for i in range(seg + 1):
        pt_ref[0, 0, i:i + 1, 0:W] = pr
        pt_ref[0, 0, i:i + 1, W:2 * W] = pi
        pr, pi = cmul(pr, pi, tr, ti)


def _s5_prep(lam_re, lam_im, log_dt, b_re, b_im, c_re, c_im, seg):
    G, P, C = S5_GROUPS, S5_STATE, S5_GROUP
    W2 = 2 * SLAB_STATE
    dt3 = jnp.broadcast_to(log_dt[:, :, None], (DEPTH, G, P))
    rows_b = lambda a: jnp.repeat(a, C, axis=1)
    bt = lambda a: a.transpose(0, 1, 3, 2).reshape(DEPTH, G * C, P)
    rows_c = lambda a: jnp.broadcast_to(a.reshape(DEPTH, G * P, 1), (DEPTH, G * P, LANES))
    ct = lambda a: jnp.tile(a.transpose(0, 1, 3, 2).reshape(DEPTH, G * P, C), (1, 1, LANES // C))
    nat = lambda a: a.reshape(DEPTH, SLABS, 1, SLAB_STATE)
    args = (rows_b(lam_re), rows_b(lam_im), rows_b(dt3), bt(b_re), bt(b_im),
            rows_c(lam_re), rows_c(lam_im), rows_c(dt3), ct(c_re), ct(c_im),
            nat(lam_re), nat(lam_im), nat(dt3))
    bspec = pl.BlockSpec((1, LANES, P), lambda l, k: (l, k, 0))
    cspec = pl.BlockSpec((1, SLAB_STATE, LANES), lambda l, k: (l, k, 0))
    nspec = pl.BlockSpec((1, 1, 1, SLAB_STATE), lambda l, k: (l, k, 0, 0))
    return pl.pallas_call(
        functools.partial(_s5_prep_kernel, seg=seg),
        out_shape=(jax.ShapeDtypeStruct((DEPTH, SLABS, S5_T * LANES, W2), BF16),
                   jax.ShapeDtypeStruct((DEPTH, SLABS, LANES, W2), F32),
                   jax.ShapeDtypeStruct((DEPTH, SLABS, W2, LANES), F32),
                   jax.ShapeDtypeStruct((DEPTH, SLABS, W2, S5_T * LANES), BF16),
                   jax.ShapeDtypeStruct((DEPTH, SLABS, S5_T // 2, 2 * LANES, 2 * LANES), BF16),
                   jax.ShapeDtypeStruct((DEPTH, SLABS, seg + 1, W2), F32),
                   jax.ShapeDtypeStruct((DEPTH, SLABS, 1, W2), F32)),
        grid=(DEPTH, SLABS),
        in_specs=[bspec] * 5 + [cspec] * 5 + [nspec] * 3,
        out_specs=(pl.BlockSpec((1, 1, S5_T * LANES, W2), lambda l, k: (l, k, 0, 0)),
                   pl.BlockSpec((1, 1, LANES, W2), lambda l, k: (l, k, 0, 0)),
                   pl.BlockSpec((1, 1, W2, LANES), lambda l, k: (l, k, 0, 0)),
                   pl.BlockSpec((1, 1, W2, S5_T * LANES), lambda l, k: (l, k, 0, 0)),
                   pl.BlockSpec((1, 1, S5_T // 2, 2 * LANES, 2 * LANES), lambda l, k: (l, k, 0, 0, 0)),
                   pl.BlockSpec((1, 1, seg + 1, W2), lambda l, k: (l, k, 0, 0)),
                   pl.BlockSpec((1, 1, 1, W2), lambda l, k: (l, k, 0, 0))),
        scratch_shapes=[pltpu.VMEM((S5_T + 1, W2, LANES), F32)],
        compiler_params=_cp(("parallel", "parallel")),
        name="s5_discretize",
    )(*args)


def _s5_seq_kernel(up_ref, be_ref, tp_ref, cpm_ref, pt_ref, s0_ref, dsk_ref,
                   yg_ref, sfin_ref, e_scr, sx_scr, *, nc):
    seg = nc // 8
    W = SLAB_STATE
    nt = W // LANES
    ub = up_ref[0, 0]
    u = ub.astype(F32)
    e = _dot(ub, be_ref[0, 0])
    pitch = e_scr.shape[1] // 8
    for c in range(2 * nt):
        for j in range(8):
            e_scr[c, j * pitch:j * pitch + seg, :] = e[j * seg:(j + 1) * seg, c * LANES:(c + 1) * LANES]

    def tiles(row):
        return [(row[:, c * LANES:(c + 1) * LANES], row[:, W + c * LANES:W + (c + 1) * LANES])
                for c in range(nt)]

    a8 = [(jnp.broadcast_to(r, (8, LANES)), jnp.broadcast_to(i, (8, LANES)))
          for r, i in tiles(pt_ref[0, 0, 1:2, :])]

    def step(i, carry):
        rows = pl.ds(i, 8, stride=pitch)
        new = []
        for c in range(nt):
            sr, si = carry[c]
            ar, ai = a8[c]
            sx_scr[c, rows, :] = sr
            sx_scr[nt + c, rows, :] = si
            new.append((ar * sr - ai * si + e_scr[c, rows, :],
                        ar * si + ai * sr + e_scr[nt + c, rows, :]))
        return tuple(new)

    zero = jnp.zeros((8, LANES), F32)
    ends = lax.fori_loop(0, seg, step, tuple((zero, zero) for _ in range(nt)))

    al = tiles(pt_ref[0, 0, seg:seg + 1, :])
    cur = tiles(s0_ref[0, 0])
    car = []
    for c in range(nt):
        alr, ali = al[c]
        cr, ci = cur[c]
        sr, si = ends[c]
        crs, cis = [], []
        for j in range(8):
            crs.append(cr)
            cis.append(ci)
            cr, ci = (alr * cr - ali * ci + sr[j:j + 1], alr * ci + ali * cr + si[j:j + 1])
        sfin_ref[0, 0, :, c * LANES:(c + 1) * LANES] = cr
        sfin_ref[0, 0, :, W + c * LANES:W + (c + 1) * LANES] = ci
        car.append((jnp.concatenate(crs, axis=0), jnp.concatenate(cis, axis=0)))

    def corr(i, _):
        rows = pl.ds(i, 8, stride=pitch)
        pw = tiles(pt_ref[0, 0, pl.ds(i, 1), :])
        for c in range(nt):
            pr, pi = pw[c]
            cr, ci = car[c]
            sx_scr[c, rows, :] = sx_scr[c, rows, :] + (pr * cr - pi * ci)
            sx_scr[nt + c, rows, :] = sx_scr[nt + c, rows, :] + (pr * ci + pi * cr)
        return 0

    lax.fori_loop(0, seg, corr, 0)

    sx = jnp.concatenate(
        [jnp.concatenate([sx_scr[c, j * pitch:j * pitch + seg, :] for j in range(8)], axis=0)
         for c in range(2 * nt)], axis=-1)
    y = _dot(sx.astype(BF16), cpm_ref[0, 0])
    TW = 2 * LANES
    for tq in range(S5_T // 2):
        acc = y[:, tq * TW:(tq + 1) * TW]
        for tpi in range(tq + 1):
            acc = acc + _dot(ub[:, tpi * TW:(tpi + 1) * TW], tp_ref[0, 0, tq - tpi])
        acc = acc + dsk_ref[0, :, tq * TW:(tq + 1) * TW] * u[:, tq * TW:(tq + 1) * TW]
        yg_ref[0, 0, :, tq * TW:(tq + 1) * TW] = jax.nn.gelu(acc).astype(yg_ref.dtype)


def _s5_seq(up, be_emb, tp, cpm, pt, s0, dsk, l):
    _, B, nc, _ = up.shape
    seg = nc // 8
    W2 = 2 * SLAB_STATE
    yg, sfin = pl.pallas_call(
        functools.partial(_s5_seq_kernel, nc=nc),
        out_shape=(jax.ShapeDtypeStruct((SLABS, B, nc, S5_T * LANES), BF16),
                   jax.ShapeDtypeStruct((SLABS, B, 1, W2), F32)),
        grid=(SLABS, B),
        in_specs=[pl.BlockSpec((1, 1, nc, S5_T * LANES), lambda k, b: (k, b, 0, 0)),
                  pl.BlockSpec((1, 1, S5_T * LANES, W2), lambda k, b: (l, k, 0, 0)),
                  pl.BlockSpec((1, 1, S5_T // 2, 2 * LANES, 2 * LANES), lambda k, b: (l, k, 0, 0, 0)),
                  pl.BlockSpec((1, 1, W2, S5_T * LANES), lambda k, b: (l, k, 0, 0)),
                  pl.BlockSpec((1, 1, seg + 1, W2), lambda k, b: (l, k, 0, 0)),
                  pl.BlockSpec((1, 1, 1, W2), lambda k, b: (k, b, 0, 0)),
                  pl.BlockSpec((1, 1, S5_T * LANES), lambda k, b: (k, 0, 0))],
        out_specs=(pl.BlockSpec((1, 1, nc, S5_T * LANES), lambda k, b: (k, b, 0, 0)),
                   pl.BlockSpec((1, 1, 1, W2), lambda k, b: (k, b, 0, 0))),
        scratch_shapes=[pltpu.VMEM((W2 // LANES, 8 * (seg + S5_SEG_PAD), LANES), F32),
                        pltpu.VMEM((W2 // LANES, 8 * (seg + S5_SEG_PAD), LANES), F32)],
        compiler_params=_cp(("parallel", "parallel")),
        name="s5_seq",
    )(up, be_emb, tp, cpm, pt, s0, dsk)
    return yg, sfin


def _s5_step_kernel(u_ref, b_ref, c_ref, a_ref, s0_ref, d_ref, yg_ref, s1_ref):
    W = SLAB_STATE
    u = u_ref[0]
    bu = _dot(u, b_ref[0, 0], HI)
    ar = a_ref[0, 0, :, 0:W]
    ai = a_ref[0, 0, :, W:2 * W]
    sr = s0_ref[0, :, 0:W]
    si = s0_ref[0, :, W:2 * W]
    nr = ar * sr - ai * si + bu[:, 0:W]
    ni = ar * si + ai * sr + bu[:, W:2 * W]
    s1_ref[0, :, 0:W] = nr
    s1_ref[0, :, W:2 * W] = ni
    s1 = jnp.concatenate([nr, ni], axis=-1)
    y = _dot(s1, c_ref[0, 0], HI) + d_ref[0] * u
    yg_ref[0] = jax.nn.gelu(y)


def _s5_step(u_slab, bst, c0, a1, s0, d1, l):
    _, N, _ = u_slab.shape
    W2 = 2 * SLAB_STATE
    return pl.pallas_call(
        _s5_step_kernel,
        out_shape=(jax.ShapeDtypeStruct((SLABS, N, LANES), F32),
                   jax.ShapeDtypeStruct((SLABS, N, W2), F32)),
        grid=(SLABS,),
        in_specs=[pl.BlockSpec((1, N, LANES), lambda k: (k, 0, 0)),
                  pl.BlockSpec((1, 1, LANES, W2), lambda k: (l, k, 0, 0)),
                  pl.BlockSpec((1, 1, W2, LANES), lambda k: (l, k, 0, 0)),
                  pl.BlockSpec((1, 1, 1, W2), lambda k: (l, k, 0, 0)),
                  pl.BlockSpec((1, N, W2), lambda k: (k, 0, 0)),
                  pl.BlockSpec((1, 1, LANES), lambda k: (k, 0, 0))],
        out_specs=(pl.BlockSpec((1, N, LANES), lambda k: (k, 0, 0)),
                   pl.BlockSpec((1, N, W2), lambda k: (k, 0, 0))),
        compiler_params=_cp(("parallel",)),
        name="s5_step",
    )(u_slab, bst, c0, a1, s0, d1)


def _l2n(x):
    return x * lax.rsqrt(jnp.sum(x * x, axis=-1, keepdims=True) + L2_EPS)


def _split_bf16(x):
    hi = x.astype(BF16)
    return hi, (x - hi.astype(F32)).astype(BF16)


def _unit_lower_solve(As, rhss):
    n = GDN_C
    row = lax.broadcasted_iota(jnp.int32, (n, n), 0)
    col = lax.broadcasted_iota(jnp.int32, (n, n), 1)
    eye = (row == col).astype(F32)
    same8 = (row // 8) == (col // 8)
    Qs = [jnp.where(same8, -A, 0.0) for A in As]
    invs = [eye + Q for Q in Qs]
    for _ in range(2):
        Qs = [_dotb(Q, Q) for Q in Qs]
        invs = [inv + _dotb(inv, Q) for inv, Q in zip(invs, Qs)]
    s = 8
    while s < n:
        sib = ((row // (2 * s)) == (col // (2 * s))) & ((row // s) != (col // s))
        offs = [jnp.where(sib, A, 0.0).astype(BF16) for A in As]
        invb = [inv.astype(BF16) for inv in invs]
        tmp = [_dot(off, ib) for off, ib in zip(offs, invb)]
        invs = [inv - _dot(ib, t.astype(BF16)) for inv, ib, t in zip(invs, invb, tmp)]
        s *= 2
    invb = [inv.astype(BF16) for inv in invs]
    x0s = [_dot(ib, rhs.astype(BF16)) for ib, rhs in zip(invb, rhss)]
    res = []
    for A, x0, rhs in zip(As, x0s, rhss):
        ah, al = _split_bf16(A)
        xh, xl = _split_bf16(x0)
        res.append(rhs - x0 - (_dot(ah, xh) + _dot(ah, xl) + _dot(al, xh)))
    return [x0 + _dot(ib, r.astype(BF16)) for x0, ib, r in zip(x0s, invb, res)]


def _gdn_tile(qc_scr, gc, beta, z_ref, nw, o_ref, s_scr, tl, r0):
    C, DK, H = GDN_C, GDN_DK, GDN_HEADS
    nchunk = tl // C
    probs = [(c, h) for c in range(nchunk) for h in range(H)]
    row = lax.broadcasted_iota(jnp.int32, (C, C), 0)
    col = lax.broadcasted_iota(jnp.int32, (C, C), 1)
    tri = row >= col
    strict = row > col

    def blk(c, off):
        return qc_scr[c * C:(c + 1) * C, off:off + DK]

    q = [_l2n(blk(c, h * DK)) * (DK ** -0.5) for c, h in probs]
    k = [_l2n(blk(c, GDN_WIDTH + h * DK)) for c, h in probs]
    v = [blk(c, 2 * GDN_WIDTH + h * DK) for c, h in probs]
    gcb = [jnp.broadcast_to(gc[c * C:(c + 1) * C, h:h + 1], (C, DK)) for c, h in probs]
    bb = [jnp.broadcast_to(beta[c * C:(c + 1) * C, H + h:H + h + 1], (C, DK)) for c, h in probs]
    decay = []
    for g in gcb:
        diff = g - g.T
        decay.append(jnp.where(tri, jnp.exp(jnp.where(tri, diff, 0.0)), 0.0))
    kbf = [x.astype(BF16) for x in k]
    kb = [x * b for x, b in zip(k, bb)]
    A = [jnp.where(strict, _dot_nt(x.astype(BF16), y) * d, 0.0) for x, y, d in zip(kb, kbf, decay)]
    egc = [jnp.exp(g) for g in gcb]
    rhs = [jnp.concatenate([x * b, y * e], axis=-1) for x, b, y, e in zip(v, bb, kb, egc)]
    sol = _unit_lower_solve(A, rhs)
    attn = [jnp.where(tri, _dot_nt(x.astype(BF16), y) * d, 0.0).astype(BF16)
            for x, y, d in zip(q, kbf, decay)]
    glast = [g[C - 1:C, :] for g in gcb]
    wq = [jnp.concatenate([s[:, DK:], x * e], axis=0).astype(BF16) for s, x, e in zip(sol, q, egc)]
    kg = [(x * jnp.exp(gl - g)).astype(BF16) for x, gl, g in zip(k, glast, gcb)]

    for c in range(nchunk):
        ps = [c * H + h for h in range(H)]
        S = [s_scr[h] for h in range(H)]
        ws = [_dot(wq[p], S[h].astype(BF16)) for h, p in enumerate(ps)]
        v_new = [sol[p][:, 0:DK] - w[0:C] for p, w in zip(ps, ws)]
        vb = [x.astype(BF16) for x in v_new]
        o = [w[C:] + _dot(attn[p], x) for p, w, x in zip(ps, ws, vb)]
        for h, p in enumerate(ps):
            s_scr[h] = S[h] * jnp.exp(glast[p]) + _dot_tn(kg[p], vb[h])
            zh = z_ref[0, r0 + c * C:r0 + (c + 1) * C, h * DK:(h + 1) * DK].astype(F32)
            on = o[h] * lax.rsqrt(jnp.mean(o[h] * o[h], axis=-1, keepdims=True) + NORM_EPS) * nw
            o_ref[0, r0 + c * C:r0 + (c + 1) * C, h * DK:(h + 1) * DK] = (on * _silu(zh)).astype(o_ref.dtype)


def _gdn_seq_kernel(qkv_ref, z_ref, ab_ref, cw_ref, alog_ref, dtb_ref, nw_ref, conv0_ref, s0_ref,
                    o_ref, sfin_ref, xp_scr, qc_scr, s_scr, *, tl):
    lt = pl.program_id(1)

    @pl.when(lt == 0)
    def _():
        xp_scr[0:8, :] = jnp.zeros((8, QKV_WIDTH), F32)
        xp_scr[8 - (GDN_CONV - 1):8, :] = conv0_ref[0]
        s_scr[...] = s0_ref[0]

    sub = qc_scr.shape[0]
    for r0 in range(0, tl, sub):
        xp_scr[8:8 + sub, :] = qkv_ref[0, r0:r0 + sub, :].astype(F32)
        conv = cw_ref[0, 0:1, :] * xp_scr[5:5 + sub, :]
        for j in range(1, GDN_CONV):
            conv = conv + cw_ref[0, j:j + 1, :] * xp_scr[5 + j:5 + j + sub, :]
        xp_scr[0:8, :] = xp_scr[sub:sub + 8, :]
        qc_scr[...] = _silu(conv)

        ab = ab_ref[0, r0:r0 + sub, :]
        g = -jnp.exp(alog_ref[...]) * jax.nn.softplus(ab + dtb_ref[...])
        beta = jax.nn.sigmoid(ab)
        row = lax.broadcasted_iota(jnp.int32, (sub, sub), 0)
        col = lax.broadcasted_iota(jnp.int32, (sub, sub), 1)
        csum = ((row >= col) & ((row // GDN_C) == (col // GDN_C))).astype(F32)
        gc = _dot(csum, g, HI)
        _gdn_tile(qc_scr, gc, beta, z_ref, nw_ref[...], o_ref, s_scr, sub, r0)

    @pl.when(lt == pl.num_programs(1) - 1)
    def _():
        sfin_ref[0] = s_scr[...]


def _gdn_seq(qkv, z, ab, conv_w, alog, dtb, nw, conv0, s0, l):
    B, L, _ = qkv.shape
    tl = min(GDN_BLOCK, L)
    sub = min(GDN_SUB, tl)
    return pl.pallas_call(
        functools.partial(_gdn_seq_kernel, tl=tl),
        out_shape=(jax.ShapeDtypeStruct((B, L, GDN_WIDTH), BF16),
                   jax.ShapeDtypeStruct((B, GDN_HEADS, GDN_DK, GDN_DK), F32)),
        grid=(B, L // tl),
        in_specs=[pl.BlockSpec((1, tl, QKV_WIDTH), lambda b, i: (b, i, 0)),
                  pl.BlockSpec((1, tl, GDN_WIDTH), lambda b, i: (b, i, 0)),
                  pl.BlockSpec((1, tl, LANES), lambda b, i: (b, i, 0)),
                  pl.BlockSpec((1, GDN_CONV, QKV_WIDTH), lambda b, i: (l, 0, 0)),
                  pl.BlockSpec((1, LANES), lambda b, i: (0, 0)),
                  pl.BlockSpec((1, LANES), lambda b, i: (0, 0)),
                  pl.BlockSpec((1, GDN_DK), lambda b, i: (0, 0)),
                  pl.BlockSpec((1, GDN_CONV - 1, QKV_WIDTH), lambda b, i: (b, 0, 0)),
                  pl.BlockSpec((1, GDN_HEADS, GDN_DK, GDN_DK), lambda b, i: (b, 0, 0, 0))],
        out_specs=(pl.BlockSpec((1, tl, GDN_WIDTH), lambda b, i: (b, i, 0)),
                   pl.BlockSpec((1, GDN_HEADS, GDN_DK, GDN_DK), lambda b, i: (b, 0, 0, 0))),
        scratch_shapes=[pltpu.VMEM((sub + 8, QKV_WIDTH), F32),
                        pltpu.VMEM((sub, QKV_WIDTH), F32),
                        pltpu.VMEM((GDN_HEADS, GDN_DK, GDN_DK), F32)],
        compiler_params=_cp(("parallel", "arbitrary")),
        name="gdn_seq",
    )(qkv, z, ab, conv_w, alog, dtb, nw, conv0, s0)


GDN_STEP_ROWS = 8


def _gdn_step_kernel(qkv_ref, z_ref, ab_ref, cw_ref, alog_ref, dtb_ref, nw_ref, conv0_ref, s0_ref,
                     *rest):
    if len(rest) == 3:
        prev_ref, o_ref, s1_all = rest
        s1_all[0] = prev_ref[...]
        s1_ref = s1_all.at[1]
    else:
        o_ref, s1_all = rest
        s1_ref = s1_all
    nb = GDN_STEP_ROWS
    W = QKV_WIDTH
    conv = cw_ref[0:1, :] * conv0_ref[:, 0:W]
    conv = conv + cw_ref[1:2, :] * conv0_ref[:, W:2 * W]
    conv = conv + cw_ref[2:3, :] * conv0_ref[:, 2 * W:3 * W]
    conv = conv + cw_ref[3:4, :] * qkv_ref[...]
    qc = _silu(conv)
    ab = ab_ref[...]
    eg = jnp.exp(-jnp.exp(alog_ref[...]) * jax.nn.softplus(ab + dtb_ref[...]))
    beta = jax.nn.sigmoid(ab)
    eye = (lax.broadcasted_iota(jnp.int32, (GDN_DK, GDN_DK), 0)
           == lax.broadcasted_iota(jnp.int32, (GDN_DK, GDN_DK), 1)).astype(F32)
    for h in range(GDN_HEADS):
        q = _l2n(qc[:, h * GDN_DK:(h + 1) * GDN_DK]) * (GDN_DK ** -0.5)
        k = _l2n(qc[:, GDN_WIDTH + h * GDN_DK:GDN_WIDTH + (h + 1) * GDN_DK])
        v = qc[:, 2 * GDN_WIDTH + h * GDN_DK:2 * GDN_WIDTH + (h + 1) * GDN_DK]
        kT = _dot_nt(eye, k, HI)
        qT = _dot_nt(eye, q, HI)
        qk = jnp.sum(q * k, axis=-1, keepdims=True)
        for j in range(nb):
            S = s0_ref[0, j, h]
            kc = jnp.broadcast_to(kT[:, j:j + 1], (GDN_DK, GDN_DK))
            qcb = jnp.broadcast_to(qT[:, j:j + 1], (GDN_DK, GDN_DK))
            kS = jnp.sum(kc * S, axis=0, keepdims=True)
            qS = jnp.sum(qcb * S, axis=0, keepdims=True)
            egj = eg[j:j + 1, h:h + 1]
            bj = beta[j:j + 1, GDN_HEADS + h:GDN_HEADS + h + 1]
            v_new = bj * v[j:j + 1, :] - (bj * egj) * kS
            o = egj * qS + qk[j:j + 1, :] * v_new
            s1_ref[j, h] = S * egj + kc * v_new
            zh = z_ref[j:j + 1, h * GDN_DK:(h + 1) * GDN_DK]
            on = o * lax.rsqrt(jnp.mean(o * o, axis=-1, keepdims=True) + NORM_EPS) * nw_ref[...]
            o_ref[j:j + 1, h * GDN_DK:(h + 1) * GDN_DK] = on * _silu(zh)


def _gdn_step(qkv, z, ab, conv_w, alog, dtb, nw, conv0, s_all, l, prev):
    N = qkv.shape[0]
    nb = GDN_STEP_ROWS
    row = lambda w: pl.BlockSpec((nb, w), lambda i: (i, 0))
    const = lambda r, w: pl.BlockSpec((r, w), lambda i: (0, 0))
    sblk = (nb, GDN_HEADS, GDN_DK, GDN_DK)
    one = pl.BlockSpec(sblk, lambda i: (i, 0, 0, 0))
    ins = [qkv, z, ab, conv_w, alog, dtb, nw, conv0, s_all]
    in_specs = [row(QKV_WIDTH), row(GDN_WIDTH), row(LANES), const(GDN_CONV, QKV_WIDTH),
                const(1, LANES), const(1, LANES), const(1, GDN_DK), row(3 * QKV_WIDTH),
                pl.BlockSpec((1,) + sblk, lambda i: (l, i, 0, 0, 0))]
    if prev is None:
        s_shape, s_spec = jax.ShapeDtypeStruct((N,) + sblk[1:], F32), one
    else:
        assert DEPTH == 2 and l == 1
        ins.append(prev)
        in_specs.append(one)
        s_shape = jax.ShapeDtypeStruct((DEPTH, N) + sblk[1:], F32)
        s_spec = pl.BlockSpec((DEPTH,) + sblk, lambda i: (0, i, 0, 0, 0))
    return pl.pallas_call(
        _gdn_step_kernel,
        out_shape=(jax.ShapeDtypeStruct((N, GDN_WIDTH), F32), s_shape),
        grid=(N // nb,),
        in_specs=in_specs,
        out_specs=(row(GDN_WIDTH), s_spec),
        compiler_params=_cp(("parallel",)),
        name="gdn_step",
    )(*ins)


def _merge_kernel(yg_ref, og_ref, ga_ref, gb_ref, x_ref, gt_ref, wglu_ref, wgo_ref, wout_ref,
                  gf_ref, scf_ref, shf_ref, wr_ref,
                  xo_ref, h_ref, lg_ref, *scr, hi, chunked):
    if chunked:
        y_scr = scr[-1]
        scr = scr[:-1]
        nrow = y_scr.shape[1] // S5_T
        for k in range(SLABS):
            for t in range(S5_T):
                y_scr[k, pl.ds(t, nrow, stride=S5_T), :] = (
                    yg_ref[k, 0, :, t * LANES:(t + 1) * LANES].astype(F32))
        y = jnp.concatenate([y_scr[k] for k in range(SLABS)], axis=-1)
    else:
        y = jnp.concatenate([yg_ref[k, 0] for k in range(SLABS)], axis=-1)
    if hi:
        wglu, wgo, wout = wglu_ref[0], wgo_ref[0], wout_ref[0]
        mm = lambda a, w: _dot(a, w, HI)
    else:
        wglu_s, wgo_s, wout_s = scr

        @pl.when((pl.program_id(0) == 0) & (pl.program_id(1) == 0))
        def _():
            wglu_s[...] = wglu_ref[0].astype(BF16)
            wgo_s[...] = wgo_ref[0].astype(BF16)
            wout_s[...] = wout_ref[0].astype(BF16)

        wglu, wgo, wout = wglu_s[...], wgo_s[...], wout_s[...]
        mm = lambda a, w: _dot(a.astype(BF16), w)

    glu = mm(y, wglu)
    branch_a = glu[:, 0:D_MODEL] * jax.nn.sigmoid(glu[:, D_MODEL:])
    branch_b = mm(og_ref[0], wgo)
    merged = ga_ref[0].astype(F32) * branch_a + gb_ref[0].astype(F32) * branch_b
    out = mm(merged, wout)
    x = x_ref[0] + gt_ref[0, 0] * out
    xo_ref[0] = x
    ms = jnp.mean(x * x, axis=-1, keepdims=True)
    h = x * lax.rsqrt(ms + NORM_EPS) * gf_ref[0]
    h = h * (1.0 + scf_ref[0, 0]) + shf_ref[0, 0]
    h_ref[0] = h.astype(h_ref.dtype)
    lg_ref[0] = _dot_nt(wr_ref[0], h, HI)


def _merge(yg, og, ga, gb, x, mod, wglu, wgo, wout, gf, wr, *, l, tm, hi, chunked, h_dtype):
    B, L, D = x.shape
    row = lambda w: pl.BlockSpec((1, tm, w), lambda b, i: (b, i, 0))
    layer = lambda r, w, ll=l: pl.BlockSpec((1, r, w), lambda b, i: (ll, 0, 0))
    scratch = [] if hi else [pltpu.VMEM((S5_WIDTH, 2 * D), BF16), pltpu.VMEM((GDN_WIDTH, D), BF16),
                             pltpu.VMEM((D, D), BF16)]
    if chunked:
        scratch = scratch + [pltpu.VMEM((SLABS, tm, LANES), F32)]
        yg_spec = pl.BlockSpec((SLABS, 1, tm // S5_T, S5_T * LANES), lambda b, i: (0, b, i, 0))
    else:
        yg_spec = pl.BlockSpec((SLABS, 1, tm, LANES), lambda b, i: (0, b, i, 0))
    lg_shape = jax.ShapeDtypeStruct((B, N_EXPERTS, L), F32)
    lg_spec = pl.BlockSpec((1, N_EXPERTS, tm), lambda b, i: (b, 0, i))
    return pl.pallas_call(
        functools.partial(_merge_kernel, hi=hi, chunked=chunked),
        out_shape=(jax.ShapeDtypeStruct((B, L, D), F32),
                   jax.ShapeDtypeStruct((B, L, D), h_dtype),
                   lg_shape),
        grid=(B, L // tm),
        in_specs=[yg_spec,
                  row(GDN_WIDTH), row(D), row(D), row(D), _mod_spec(mod, l, 2, tm),
                  layer(S5_WIDTH, 2 * D), layer(GDN_WIDTH, D), layer(D, D),
                  layer(1, D), _mod_spec(mod, l, 4, tm), _mod_spec(mod, l, 3, tm),
                  layer(N_EXPERTS, D, l // 2)],
        out_specs=(row(D), row(D), lg_spec),
        scratch_shapes=scratch,
        compiler_params=_cp(("arbitrary", "arbitrary")),
        name="merge_out_proj",
    )(yg, og, ga, gb, x, mod, wglu, wgo, wout, gf, mod, mod, wr)


FF_TILE = 512
FFN_ROWS = 512


def _finish(x, gfin_ref, final):
    if not final:
        return x
    ms = jnp.mean(x * x, axis=-1, keepdims=True)
    return x * lax.rsqrt(ms + NORM_EPS) * gfin_ref[...]


def _ffn_kernel(h_ref, x_ref, gt_ref, wg_ref, wu_ref, wd_ref, gfin_ref, o_ref, acc_scr, *, hi, final):
    j = pl.program_id(2)
    if hi:
        wg, wu, wd = wg_ref[...], wu_ref[...], wd_ref[...]
        mm = lambda a, w: _dot(a, w, HI)
    else:
        wg, wu, wd = wg_ref[...].astype(BF16), wu_ref[...].astype(BF16), wd_ref[...].astype(BF16)
        mm = lambda a, w: _dot(a.astype(BF16), w)

    @pl.when(j == 0)
    def _():
        acc_scr[...] = jnp.zeros_like(acc_scr)

    tm = acc_scr.shape[0]
    sub = min(FFN_ROWS, tm)
    for s in range(tm // sub):
        rows = slice(s * sub, (s + 1) * sub)
        hb = h_ref[0, rows, :]
        act = _silu(mm(hb, wg)) * mm(hb, wu)
        acc_scr[rows, :] = acc_scr[rows, :] + mm(act, wd)

    @pl.when(j == pl.num_programs(2) - 1)
    def _():
        o_ref[0] = _finish(x_ref[0] + gt_ref[0, 0] * acc_scr[...], gfin_ref, final)


def _ffn(h, x, mod, w_gu, w_down, gfin, *, l, tm, hi, final):
    B, L, D = x.shape
    nj = D_FF // FF_TILE
    row = pl.BlockSpec((1, tm, D), lambda b, i, j: (b, i, 0))
    return pl.pallas_call(
        functools.partial(_ffn_kernel, hi=hi, final=final),
        out_shape=jax.ShapeDtypeStruct((B, L, D), F32),
        grid=(B, L // tm, nj),
        in_specs=[row, row, _mod_spec(mod, l, 5, tm),
                  pl.BlockSpec((D, FF_TILE), lambda b, i, j: (0, j)),
                  pl.BlockSpec((D, FF_TILE), lambda b, i, j: (0, nj + j)),
                  pl.BlockSpec((FF_TILE, D), lambda b, i, j: (j, 0)),
                  pl.BlockSpec((1, D), lambda b, i, j: (0, 0))],
        out_specs=row,
        scratch_shapes=[pltpu.VMEM((tm, D), F32)],
        compiler_params=_cp(("parallel", "parallel", "arbitrary")),
        name="ffn_dense",
    )(h, x, mod, w_gu, w_gu, w_down, gfin)


ROUTE_TM = 512
ROW_DMA_TM = 256
MOE_SUP = 2048
MOE_SUB = 512
MOE_FF_TILE = 512


def _route_kernel(lg_ref, br_ref, cnt0_ref, slot_ref, wt_ref, cnt_ref, carry_scr, *, cap):
    @pl.when((pl.program_id(0) == 0) & (pl.program_id(1) == 0))
    def _():
        carry_scr[...] = cnt0_ref[...]

    lg = lg_ref[0] + br_ref[...]
    tm = lg.shape[1]
    eidx = lax.broadcasted_iota(jnp.int32, lg.shape, 0)
    m1 = jnp.max(lg, axis=0, keepdims=True)
    i1 = jnp.min(jnp.where(lg == m1, eidx, N_EXPERTS), axis=0, keepdims=True)
    lg2 = jnp.where(eidx == i1, -jnp.inf, lg)
    m2 = jnp.max(lg2, axis=0, keepdims=True)
    i2 = jnp.min(jnp.where(lg2 == m2, eidx, N_EXPERTS), axis=0, keepdims=True)
    e2 = jnp.exp(m2 - m1)
    wt_ref[0, 0:1, :] = 1.0 / (1.0 + e2)
    wt_ref[0, 1:2, :] = e2 / (1.0 + e2)
    sel1 = eidx == i1
    sel2 = eidx == i2
    oh = jnp.where(sel1 | sel2, 1.0, 0.0)
    before = (lax.broadcasted_iota(jnp.int32, (tm, tm), 0)
              < lax.broadcasted_iota(jnp.int32, (tm, tm), 1)).astype(BF16)
    rank = carry_scr[:, 0:1] + _dot(oh.astype(BF16), before)
    r1 = jnp.sum(jnp.where(sel1, rank, 0.0), axis=0, keepdims=True).astype(jnp.int32)
    r2 = jnp.sum(jnp.where(sel2, rank, 0.0), axis=0, keepdims=True).astype(jnp.int32)
    slot_ref[0, 0:1, :] = i1 * cap + r1
    slot_ref[0, 1:2, :] = i2 * cap + r2
    carry_scr[...] = carry_scr[...] + jnp.sum(oh, axis=1, keepdims=True)
    cnt_ref[...] = carry_scr[...]


def _route_slots(lgT, b_r, cnt0, cap):
    B, E, L = lgT.shape
    tm = min(ROUTE_TM, L)
    return pl.pallas_call(
        functools.partial(_route_kernel, cap=cap),
        out_shape=(jax.ShapeDtypeStruct((B, 2, L), jnp.int32),
                   jax.ShapeDtypeStruct((B, 2, L), F32),
                   jax.ShapeDtypeStruct((E, LANES), F32)),
        grid=(B, L // tm),
        in_specs=[pl.BlockSpec((1, E, tm), lambda b, i: (b, 0, i)),
                  pl.BlockSpec((E, 1), lambda b, i: (0, 0)),
                  pl.BlockSpec((E, LANES), lambda b, i: (0, 0))],
        out_specs=(pl.BlockSpec((1, 2, tm), lambda b, i: (b, 0, i)),
                   pl.BlockSpec((1, 2, tm), lambda b, i: (b, 0, i)),
                   pl.BlockSpec((E, LANES), lambda b, i: (0, 0))),
        scratch_shapes=[pltpu.VMEM((E, LANES), F32)],
        compiler_params=_cp(("arbitrary", "arbitrary")),
        name="moe_route",
    )(lgT, b_r.reshape(E, 1), cnt0)


def _row_copy(src, dst, sem):
    return pltpu.make_async_copy(src, dst, sem)


def _slot_rows_kernel(start_ref, per_ref, code_ref, row_ref, *, cap):
    code = code_ref[...]
    shift = cap.bit_length() - 1
    e = lax.shift_right_logical(code, shift)
    r = code & (cap - 1)
    start = jnp.zeros_like(code)
    per = jnp.ones_like(code)
    for k in range(N_EXPERTS):
        start = jnp.where(e == k, start_ref[k], start)
        per = jnp.where(e == k, per_ref[k], per)
    q = jnp.floor((r.astype(F32) + 0.5) / per.astype(F32)).astype(jnp.int32)
    row_ref[...] = start + q * MOE_SUP + (r - q * per)


def _slot_rows(start, per, codes, cap):
    B, _, L = codes.shape
    tm = min(ROUTE_TM, L)
    spec = pl.BlockSpec((1, 2, tm), lambda b, i, st, pe: (b, 0, i))
    return pl.pallas_call(
        functools.partial(_slot_rows_kernel, cap=cap),
        out_shape=jax.ShapeDtypeStruct(codes.shape, jnp.int32),
        grid_spec=pltpu.PrefetchScalarGridSpec(
            num_scalar_prefetch=2, grid=(B, L // tm), in_specs=[spec], out_specs=spec),
        compiler_params=_cp(("parallel", "parallel")),
        name="moe_slot_rows",
    )(start, per, codes)


def _zeros_kernel(o_ref):
    o_ref[...] = jnp.zeros_like(o_ref)


def _zero_rows(n_rows, width):
    return pl.pallas_call(
        _zeros_kernel,
        out_shape=jax.ShapeDtypeStruct((n_rows, width), F32),
        grid=(n_rows // MOE_SUP,),
        out_specs=pl.BlockSpec((MOE_SUP, width), lambda i: (i, 0)),
        compiler_params=_cp(("parallel",)),
        name="moe_zero_rows",
    )()


def _dispatch_kernel(row_ref, h_ref, xs_in_ref, xs_ref, hbuf, sem):
    del xs_in_ref
    tm = h_ref.shape[1]
    t = pl.program_id(0) * pl.num_programs(1) + pl.program_id(1)
    last = pl.num_programs(0) * pl.num_programs(1) - 1
    slot = t % 2
    hbuf[slot] = h_ref[0]

    def issue(r, _):
        for k in range(2):
            row = row_ref[0, k, r]
            _row_copy(hbuf.at[slot, pl.ds(r, 1), :], xs_ref.at[pl.ds(row, 1), :], sem.at[slot]).start()
        return 0

    lax.fori_loop(0, tm, issue, 0, unroll=8)

    def drain(sl):
        for k in range(2):
            _row_copy(hbuf.at[sl], xs_ref.at[pl.ds(0, tm), :], sem.at[sl]).wait()

    @pl.when(t > 0)
    def _():
        drain(1 - slot)

    @pl.when(t == last)
    def _():
        drain(slot)


def _dispatch(rows, h, xs):
    B, L, D = h.shape
    n_rows = xs.shape[0]
    tm = min(ROW_DMA_TM, L)
    return pl.pallas_call(
        _dispatch_kernel,
        out_shape=jax.ShapeDtypeStruct((n_rows, D), F32),
        grid=(B, L // tm),
        in_specs=[pl.BlockSpec((1, 2, tm), lambda b, i: (b, 0, i), memory_space=pltpu.SMEM),
                  pl.BlockSpec((1, tm, D), lambda b, i: (b, i, 0)),
                  pl.BlockSpec(memory_space=pl.ANY)],
        out_specs=pl.BlockSpec(memory_space=pl.ANY),
        scratch_shapes=[pltpu.VMEM((2, tm, D), F32), pltpu.SemaphoreType.DMA((2,))],
        input_output_aliases={2: 0},
        compiler_params=_cp(("arbitrary", "arbitrary")),
        name="moe_dispatch",
    )(rows, h, xs)


def _moe_grp_kernel(ge_ref, gn_ref, x_ref, wg_ref, wu_ref, wd_ref, y_ref, xb_scr):
    g = pl.program_id(0)
    j = pl.program_id(1)
    nsub = gn_ref[g]
    wg = wg_ref[0].astype(BF16)
    wu = wu_ref[0].astype(BF16)
    wd = wd_ref[0].astype(BF16)
    nblk = MOE_SUP // MOE_SUB

    @pl.when(j == 0)
    def _():
        xb_scr[...] = x_ref[...].astype(BF16)
        y_ref[...] = jnp.zeros_like(y_ref)

    def block(s):
        rows = slice(s * MOE_SUB, (s + 1) * MOE_SUB)
        xb = xb_scr[rows, :]
        act = _silu(_dot(xb, wg)) * _dot(xb, wu)
        y_ref[rows, :] = y_ref[rows, :] + _dot(act.astype(BF16), wd)

    for n in range(1, nblk + 1):
        @pl.when(nsub == n)
        def _():
            for s in range(n):
                block(s)


def _moe_groups(counts, n_groups):
    nsup = (counts + MOE_SUP - 1) // MOE_SUP
    div = jnp.maximum(nsup, 1)
    per = jnp.maximum(((counts + div - 1) // div + MOE_SUB - 1) // MOE_SUB * MOE_SUB, MOE_SUB)
    ends = jnp.cumsum(nsup)
    first = ends - nsup
    total = ends[-1]
    g = jnp.arange(n_groups, dtype=jnp.int32)
    gc = jnp.minimum(g, total - 1)
    e_of = jnp.minimum(jnp.sum((gc[:, None] >= ends[None, :]).astype(jnp.int32), axis=1), N_EXPERTS - 1)
    left = jnp.minimum(counts[e_of] - (gc - first[e_of]) * per[e_of], per[e_of])
    nsub = jnp.clip((left + MOE_SUB - 1) // MOE_SUB, 0, MOE_SUP // MOE_SUB)
    gn = jnp.where(g < total, nsub, 0).astype(jnp.int32)
    return e_of, gn, (first * MOE_SUP).astype(jnp.int32), per.astype(jnp.int32)


def _moe_grouped(xs, ge, gn, w_gu, w_down):
    D = xs.shape[1]
    nj = D_FF // MOE_FF_TILE
    ng = xs.shape[0] // MOE_SUP
    jj = lambda j, gn, g: jnp.where(gn[g] > 0, j, nj - 1)
    return pl.pallas_call(
        _moe_grp_kernel,
        out_shape=jax.ShapeDtypeStruct(xs.shape, F32),
        grid_spec=pltpu.PrefetchScalarGridSpec(
            num_scalar_prefetch=2,
            grid=(ng, nj),
            in_specs=[pl.BlockSpec((MOE_SUP, D), lambda g, j, ge, gn: (g, 0)),
                      pl.BlockSpec((1, D, MOE_FF_TILE), lambda g, j, ge, gn: (ge[g], 0, jj(j, gn, g))),
                      pl.BlockSpec((1, D, MOE_FF_TILE), lambda g, j, ge, gn: (ge[g], 0, nj + jj(j, gn, g))),
                      pl.BlockSpec((1, MOE_FF_TILE, D), lambda g, j, ge, gn: (ge[g], jj(j, gn, g), 0))],
            out_specs=pl.BlockSpec((MOE_SUP, D), lambda g, j, ge, gn: (g, 0)),
            scratch_shapes=[pltpu.VMEM((MOE_SUP, D), BF16)],
        ),
        compiler_params=_cp(("arbitrary", "arbitrary")),
        name="moe_experts",
    )(ge, gn, xs, w_gu, w_gu, w_down)


def _combine_kernel(row_ref, next_ref, w_ref, x_ref, gt_ref, gfin_ref, ys_ref, o_ref, g_scr, sem, *, final):
    tm = x_ref.shape[1]
    t = pl.program_id(0) * pl.num_programs(1) + pl.program_id(1)
    last = pl.num_programs(0) * pl.num_programs(1) - 1
    slot = t % 2

    def gather(rows, sl):
        def issue(r, _):
            for k in range(2):
                row = rows[0, k, r]
                _row_copy(ys_ref.at[pl.ds(row, 1), :], g_scr.at[sl, k, pl.ds(r, 1), :], sem.at[sl]).start()
            return 0

        lax.fori_loop(0, tm, issue, 0, unroll=8)

    @pl.when(t == 0)
    def _():
        gather(row_ref, slot)

    @pl.when(t < last)
    def _():
        gather(next_ref, 1 - slot)

    for k in range(2):
        _row_copy(ys_ref.at[pl.ds(0, tm), :], g_scr.at[slot, k], sem.at[slot]).wait()
    w = w_ref[0]
    f = w[:, 0:1] * g_scr[slot, 0] + w[:, 1:2] * g_scr[slot, 1]
    o_ref[0] = _finish(x_ref[0] + gt_ref[0, 0] * f, gfin_ref, final)


def _combine(rows, wts, x, mod, gfin, ys, *, l, final):
    B, L, D = x.shape
    tm = min(ROW_DMA_TM, L)
    gt_spec = _mod_spec(mod, l, 5, tm)
    row = pl.BlockSpec((1, tm, D), lambda b, i: (b, i, 0))
    nl = L // tm

    def next_block(b, i):
        t1 = jnp.minimum(b * nl + i + 1, B * nl - 1)
        return (t1 // nl, 0, t1 % nl)

    return pl.pallas_call(
        functools.partial(_combine_kernel, final=final),
        out_shape=jax.ShapeDtypeStruct((B, L, D), F32),
        grid=(B, L // tm),
        in_specs=[pl.BlockSpec((1, 2, tm), lambda b, i: (b, 0, i), memory_space=pltpu.SMEM),
                  pl.BlockSpec((1, 2, tm), next_block, memory_space=pltpu.SMEM),
                  pl.BlockSpec((1, tm, 2), lambda b, i: (b, i, 0)),
                  row, gt_spec,
                  pl.BlockSpec((1, D), lambda b, i: (0, 0)),
                  pl.BlockSpec(memory_space=pl.ANY)],
        out_specs=row,
        scratch_shapes=[pltpu.VMEM((2, 2, tm, D), F32), pltpu.SemaphoreType.DMA((2,))],
        compiler_params=_cp(("arbitrary", "arbitrary")),
        name="moe_combine",
    )(rows, rows, wts.transpose(0, 2, 1), x, mod, gfin, ys)


def _moe_routed(groups, b_r, w_gu, w_down, gfin, *, l, final):
    D = groups[0][1].shape[-1]
    n_tok = sum(g[1].shape[0] * g[1].shape[1] for g in groups)
    cap = 1 << (n_tok - 1).bit_length()
    n_groups = 2 * n_tok // MOE_SUP + N_EXPERTS
    cnt = jnp.zeros((N_EXPERTS, LANES), F32)
    routed = []
    for _, _, _, lgT in groups:
        codes, wts, cnt = _route_slots(lgT, b_r, cnt, cap)
        routed.append((codes, wts))
    ge, gn, start, per = _moe_groups(cnt[:, 0].astype(jnp.int32), n_groups)
    rows = [_slot_rows(start, per, codes, cap) for codes, _ in routed]
    xs = _zero_rows(n_groups * MOE_SUP, D)
    for (h, _, _, _), r in zip(groups, rows):
        xs = _dispatch(r, h, xs)
    ys = _moe_grouped(xs, ge, gn, w_gu, w_down)
    return [_combine(r, wts, x, mod, gfin, ys, l=l, final=final)
            for (_, x, mod, _), (_, wts), r in zip(groups, routed, rows)]


def _pad_lanes(v):
    return jnp.pad(v.reshape(1, -1), ((0, 0), (0, LANES - v.shape[-1])))


def _mixer_layer(x, mod, states, p, s5m, l, prev_sg, *, seq):
    B, L, D = x.shape
    hi = not seq
    s5r0, s5i0, sg0, sc0 = states
    w_in, w_gates, w_ab = p['w_in_seq' if seq else 'w_in']
    if seq:
        u, qkv, z, ga, gb, ab = _proj_seq(x, p['g_mix'], mod, w_in, w_gates, w_ab, l=l, tm=min(1024, L))
    else:
        u, qkv, z, ga, gb, ab = _proj(x, p['g_mix'], mod, w_in, w_gates, w_ab, l=l, tm=L)
    alog = _pad_lanes(p['gdn_a_log'][l])
    dtb = _pad_lanes(p['gdn_dt_bias'][l])
    nw = p['gdn_norm_w'][l].reshape(1, GDN_DK)
    if seq:
        yg, sfin = _s5_seq(u, s5m['be'], s5m['tp'], s5m['cpm'], s5m['pt'],
                           jnp.zeros((SLABS, B, 1, 2 * SLAB_STATE), F32), s5m['dsk'][l], l)
        sfin = sfin.reshape(SLABS, B, 2, SLAB_STATE).transpose(2, 1, 0, 3)
        sr = sfin[0].reshape(B, S5_GROUPS, S5_STATE)
        si = sfin[1].reshape(B, S5_GROUPS, S5_STATE)
        og, sg = _gdn_seq(qkv, z, ab, p['gdn_conv_w'], alog, dtb, nw,
                          jnp.zeros((B, GDN_CONV - 1, QKV_WIDTH), F32),
                          jnp.zeros((B, GDN_HEADS, GDN_DK, GDN_DK), F32), l)
        cb = qkv[:, L - (GDN_CONV - 1):, :].astype(F32)
    else:
        n = L
        s0 = jnp.concatenate([s5r0[l].reshape(n, SLABS, SLAB_STATE),
                              s5i0[l].reshape(n, SLABS, SLAB_STATE)], axis=-1).transpose(1, 0, 2)
        yg, s1 = _s5_step(u.reshape(SLABS, n, LANES), s5m['bst'], s5m['c0'], s5m['a1'],
                          s0, s5m['d1'][l], l)
        yg = yg.reshape(SLABS, 1, n, LANES)
        s1 = s1.transpose(1, 0, 2)
        sr = s1[:, :, :SLAB_STATE].reshape(n, S5_GROUPS, S5_STATE)
        si = s1[:, :, SLAB_STATE:].reshape(n, S5_GROUPS, S5_STATE)
        og, sg = _gdn_step(qkv.reshape(n, QKV_WIDTH), z.reshape(n, GDN_WIDTH), ab.reshape(n, LANES),
                           p['gdn_conv_w'][l], alog, dtb, nw,
                           sc0[l].reshape(n, (GDN_CONV - 1) * QKV_WIDTH), sg0, l, prev_sg)
        og = og.reshape(1, n, GDN_WIDTH)
        cb = jnp.concatenate([sc0[l][:, 1:, :], qkv.reshape(n, 1, QKV_WIDTH)], axis=1)
    x, h, lgT = _merge(yg, og, ga, gb, x, mod, p['w_s5_glu'], p['w_gdn_out'], p['w_out'],
                       p['g_ffn'], p['w_router'], l=l, tm=min(512, L), hi=hi, chunked=seq,
                       h_dtype=BF16 if (seq and l % 2 == 0) else F32)
    return x, h, lgT, (sr, si, sg, cb)


def kernel(x_prompt, x_sample, c_prompt, c_sample, state_s5_re, state_s5_im, state_gdn, state_conv,
           g_mix, g_ffn, g_final, w_ada, b_ada, w_in, s5_lambda_re, s5_lambda_im, s5_log_dt,
           s5_b_re, s5_b_im, s5_c_re, s5_c_im, s5_d, w_s5_glu, gdn_conv_w, gdn_a_log, gdn_dt_bias,
           gdn_norm_w, w_gdn_out, w_out, w_ffn_gate_up, w_ffn_down, w_router, b_router,
           w_exp_gate_up, w_exp_down):
    def in_proj_parts(w):
        return w, w[:, :, 2568:], jnp.pad(w[:, :, 2560:2568], ((0, 0), (0, 0), (0, LANES - 8)))

    D_ = x_prompt.shape[-1]
    p = dict(g_mix=g_mix.reshape(DEPTH, 1, D_), g_ffn=g_ffn.reshape(DEPTH, 1, D_), w_s5_glu=w_s5_glu,
             gdn_conv_w=gdn_conv_w, gdn_a_log=gdn_a_log, gdn_dt_bias=gdn_dt_bias,
             gdn_norm_w=gdn_norm_w, w_gdn_out=w_gdn_out, w_out=w_out,
             w_router=w_router.transpose(0, 2, 1), w_in=in_proj_parts(w_in),
             w_in_seq=in_proj_parts(w_in.astype(BF16)))
    nbp, L, D = x_prompt.shape
    nbs = x_sample.shape[0]

    mod = _ada(jnp.concatenate([c_prompt, c_sample], axis=0), w_ada, b_ada)
    mod_p = mod[:, :nbp].reshape(DEPTH, nbp, 1, 6 * D)
    mod_s = mod[:, nbp:].reshape(DEPTH, 1, nbs, 6 * D)

    seg = L // S5_T // 8
    be, bst, c0, cpm, tp, pt, a1 = _s5_prep(s5_lambda_re, s5_lambda_im, s5_log_dt, s5_b_re, s5_b_im,
                                            s5_c_re, s5_c_im, seg)
    d1 = [s5_d[l].reshape(SLABS, 1, LANES) for l in range(DEPTH)]
    s5m = dict(be=be, bst=bst, c0=c0, cpm=cpm, pt=pt, a1=a1, tp=tp, d1=d1,
               dsk=[jnp.tile(d, (1, 1, S5_T)) for d in d1])

    xs_ = [x_prompt, x_sample.reshape(1, nbs, D)]
    mods = [mod_p, mod_s]
    states = [(None, None, None, None), (state_s5_re, state_s5_im, state_gdn, state_conv)]
    outs = [[], []]
    gfin = g_final.reshape(1, D)
    for l in range(DEPTH):
        final = l == DEPTH - 1
        mixed = []
        for gi, seq in enumerate((True, False)):
            prev_sg = outs[gi][0][2] if (not seq and final and DEPTH == 2) else None
            x, h, lgT, st = _mixer_layer(xs_[gi], mods[gi], states[gi], p, s5m, l, prev_sg, seq=seq)
            outs[gi].append(st)
            mixed.append((h, x, mods[gi], lgT))
        if l % 2 == 0:
            wgu, wdn = w_ffn_gate_up[l // 2], w_ffn_down[l // 2]
            xs_ = [_ffn(h, x, mod_g, wgu if gi else wgu.astype(BF16), wdn if gi else wdn.astype(BF16), gfin,
                        l=l, tm=min(1024, x.shape[1]), hi=(gi == 1), final=final)
                   for gi, (h, x, mod_g, _) in enumerate(mixed)]
        else:
            xs_ = _moe_routed(mixed, b_router[l // 2], w_exp_gate_up[l // 2], w_exp_down[l // 2], gfin,
                              l=l, final=final)
    y_p, y_s = xs_
    st_p = [jnp.stack([o[i] for o in outs[0]]) for i in range(4)]
    st_s = [outs[1][-1][2] if (i == 2 and DEPTH == 2) else jnp.stack([o[i] for o in outs[1]])
            for i in range(4)]
    return (y_p, y_s.reshape(nbs, 1, D), st_p[0], st_p[1], st_p[2], st_p[3],
            st_s[0], st_s[1], st_s[2], st_s[3])
```

```python
import functools

import jax
import jax.numpy as jnp
from jax import lax
from jax.experimental import pallas as pl
from jax.experimental.pallas import tpu as pltpu

F32 = jnp.float32
BF16 = jnp.bfloat16
HI = lax.Precision.HIGHEST

D_MODEL = 1024
DEPTH = 2
S5_WIDTH = 512
S5_GROUP = 16
S5_GROUPS = 32
S5_STATE = 64
GDN_HEADS = 4
GDN_DK = 128
GDN_WIDTH = 512
GDN_CONV = 4
QKV_WIDTH = 1536
D_FF = 3584
N_EXPERTS = 8
NORM_EPS = 1e-6
L2_EPS = 1e-6

LANES = 128
SLABS = S5_WIDTH // LANES
SLAB_STATE = (S5_GROUPS // SLABS) * S5_STATE
S5_T = 8
S5_SEG_PAD = 4
GDN_C = 128
GDN_BLOCK = 512
GDN_SUB = 256
VMEM_LIMIT = 56 * 1024 * 1024


def _cp(sem, vmem=VMEM_LIMIT):
    return pltpu.CompilerParams(dimension_semantics=sem, vmem_limit_bytes=vmem)


def _dot(a, b, prec=None):
    return jnp.dot(a, b, precision=prec, preferred_element_type=F32)


def _dotb(a, b):
    return jnp.dot(a.astype(BF16), b.astype(BF16), preferred_element_type=F32)


def _dot_nt(a, b, prec=None):
    return lax.dot_general(a, b, (((1,), (1,)), ((), ())), precision=prec,
                           preferred_element_type=F32)


def _dot_tn(a, b, prec=None):
    return lax.dot_general(a, b, (((0,), (0,)), ((), ())), precision=prec,
                           preferred_element_type=F32)


def _silu(x):
    return x * jax.nn.sigmoid(x)


def _ada_kernel(c_ref, w_ref, b_ref, o_ref):
    cs = _silu(c_ref[...])
    o_ref[0] = _dot(cs, w_ref[0], HI) + b_ref[0]


def _ada(c_all, w_ada, b_ada):
    n = c_all.shape[0]
    tn = 1536
    return pl.pallas_call(
        _ada_kernel,
        out_shape=jax.ShapeDtypeStruct((DEPTH, n, 6 * D_MODEL), F32),
        grid=(DEPTH, 6 * D_MODEL // tn),
        in_specs=[pl.BlockSpec((n, D_MODEL), lambda l, j: (0, 0)),
                  pl.BlockSpec((1, D_MODEL, tn), lambda l, j: (l, 0, j)),
                  pl.BlockSpec((1, 1, tn), lambda l, j: (l, 0, j))],
        out_specs=pl.BlockSpec((1, n, tn), lambda l, j: (l, 0, j)),
        compiler_params=_cp(("parallel", "parallel")),
        name="ada_mod",
    )(c_all, w_ada, b_ada.reshape(DEPTH, 1, 6 * D_MODEL))


def _proj_kernel(x_ref, g_ref, sc_ref, sh_ref, w_ref, wg_ref, wab_ref,
                 u_ref, qkv_ref, z_ref, ga_ref, gb_ref, ab_ref, h_scr):
    j = pl.program_id(2)

    @pl.when(j == 0)
    def _():
        x = x_ref[0]
        ms = jnp.mean(x * x, axis=-1, keepdims=True)
        xn = x * lax.rsqrt(ms + NORM_EPS) * g_ref[0]
        h_scr[...] = (xn * (1.0 + sc_ref[0, 0]) + sh_ref[0, 0]).astype(h_scr.dtype)

    def mm(w):
        return _dot(h_scr[...], w, HI)

    @pl.when(j == 0)
    def _():
        res = mm(w_ref[0])
        for k in range(SLABS):
            u_ref[k, 0] = res[:, k * LANES:(k + 1) * LANES]

    @pl.when((j >= 1) & (j <= 3))
    def _():
        qkv_ref[0] = mm(w_ref[0])

    @pl.when(j == 4)
    def _():
        z_ref[0] = mm(w_ref[0])

    @pl.when((j == 5) | (j == 6))
    def _():
        ga_ref[0] = jax.nn.sigmoid(mm(wg_ref[0]))

    @pl.when((j == 7) | (j == 8))
    def _():
        gb_ref[0] = jax.nn.sigmoid(mm(wg_ref[0]))

    @pl.when(j == 9)
    def _():
        ab_ref[0] = mm(wab_ref[0])


def _mod_spec(mod, l, chunk, tm):
    per_row = mod.shape[2] != 1
    D = mod.shape[3] // 6

    def index(b, i, *_):
        return (l, b, i if per_row else 0, chunk)

    return pl.BlockSpec((1, 1, tm if per_row else 1, D), index)


def _proj(x, g, mod, w_in, w_gates, w_ab, *, l, tm):
    B, L, D = x.shape
    tn = 512
    clampi = lambda j, lo, n: jnp.clip(j - lo, 0, n - 1)
    outs = pl.pallas_call(
        _proj_kernel,
        out_shape=(jax.ShapeDtypeStruct((SLABS, B, L, LANES), F32),
                   jax.ShapeDtypeStruct((B, L, QKV_WIDTH), F32),
                   jax.ShapeDtypeStruct((B, L, GDN_WIDTH), F32),
                   jax.ShapeDtypeStruct((B, L, D), F32),
                   jax.ShapeDtypeStruct((B, L, D), F32),
                   jax.ShapeDtypeStruct((B, L, LANES), F32)),
        grid=(B, L // tm, 10),
        in_specs=[pl.BlockSpec((1, tm, D), lambda b, i, j: (b, i, 0)),
                  pl.BlockSpec((1, 1, D), lambda b, i, j: (l, 0, 0)),
                  _mod_spec(mod, l, 1, tm),
                  _mod_spec(mod, l, 0, tm),
                  pl.BlockSpec((1, D, tn), lambda b, i, j: (l, 0, jnp.minimum(j, 4))),
                  pl.BlockSpec((1, D, tn), lambda b, i, j: (l, 0, clampi(j, 5, 4))),
                  pl.BlockSpec((1, D, LANES), lambda b, i, j: (l, 0, 0))],
        out_specs=(pl.BlockSpec((SLABS, 1, tm, LANES), lambda b, i, j: (0, b, i, 0)),
                   pl.BlockSpec((1, tm, tn), lambda b, i, j: (b, i, clampi(j, 1, 3))),
                   pl.BlockSpec((1, tm, tn), lambda b, i, j: (b, i, 0)),
                   pl.BlockSpec((1, tm, tn), lambda b, i, j: (b, i, clampi(j, 5, 2))),
                   pl.BlockSpec((1, tm, tn), lambda b, i, j: (b, i, clampi(j, 7, 2))),
                   pl.BlockSpec((1, tm, LANES), lambda b, i, j: (b, i, 0))),
        scratch_shapes=[pltpu.VMEM((tm, D), F32)],
        compiler_params=_cp(("parallel", "parallel", "arbitrary")),
        name="norm_in_proj",
    )(x, g, mod, mod, w_in, w_gates, w_ab)
    return outs


def _proj_seq_kernel(x_ref, g_ref, sc_ref, sh_ref, w_ref, wg_ref, wab_ref,
                     u_ref, qkv_ref, z_ref, ga_ref, gb_ref, ab_ref, us_scr):
    x = x_ref[0]
    ms = jnp.mean(x * x, axis=-1, keepdims=True)
    xn = x * lax.rsqrt(ms + NORM_EPS) * g_ref[0]
    h = (xn * (1.0 + sc_ref[0, 0]) + sh_ref[0, 0]).astype(BF16)
    res = _dot(h, w_ref[0, :, 0:S5_WIDTH])
    nrow = res.shape[0] // S5_T
    for k in range(SLABS):
        us_scr[...] = res[:, k * LANES:(k + 1) * LANES]
        for t in range(S5_T):
            u_ref[k, 0, :, t * LANES:(t + 1) * LANES] = (
                us_scr[pl.ds(t, nrow, stride=S5_T), :].astype(u_ref.dtype))
    c0 = S5_WIDTH
    qkv_ref[0] = _dot(h, w_ref[0, :, c0:c0 + QKV_WIDTH]).astype(qkv_ref.dtype)
    c0 += QKV_WIDTH
    z_ref[0] = _dot(h, w_ref[0, :, c0:c0 + GDN_WIDTH]).astype(z_ref.dtype)
    D = x.shape[-1]
    ga_ref[0] = jax.nn.sigmoid(_dot(h, wg_ref[0, :, 0:D])).astype(ga_ref.dtype)
    gb_ref[0] = jax.nn.sigmoid(_dot(h, wg_ref[0, :, D:2 * D])).astype(gb_ref.dtype)
    ab_ref[0] = _dot(h, wab_ref[0])


def _proj_seq(x, g, mod, w_in, w_gates, w_ab, *, l, tm):
    B, L, D = x.shape
    n_main = S5_WIDTH + QKV_WIDTH + GDN_WIDTH
    row = lambda w: pl.BlockSpec((1, tm, w), lambda b, i: (b, i, 0))
    return pl.pallas_call(
        _proj_seq_kernel,
        out_shape=(jax.ShapeDtypeStruct((SLABS, B, L // S5_T, S5_T * LANES), BF16),
                   jax.ShapeDtypeStruct((B, L, QKV_WIDTH), BF16),
                   jax.ShapeDtypeStruct((B, L, GDN_WIDTH), BF16),
                   jax.ShapeDtypeStruct((B, L, D), BF16),
                   jax.ShapeDtypeStruct((B, L, D), BF16),
                   jax.ShapeDtypeStruct((B, L, LANES), F32)),
        grid=(B, L // tm),
        in_specs=[row(D),
                  pl.BlockSpec((1, 1, D), lambda b, i: (l, 0, 0)),
                  _mod_spec(mod, l, 1, tm),
                  _mod_spec(mod, l, 0, tm),
                  pl.BlockSpec((1, D, n_main), lambda b, i: (l, 0, 0)),
                  pl.BlockSpec((1, D, 2 * D), lambda b, i: (l, 0, 0)),
                  pl.BlockSpec((1, D, LANES), lambda b, i: (l, 0, 0))],
        out_specs=(pl.BlockSpec((SLABS, 1, tm // S5_T, S5_T * LANES), lambda b, i: (0, b, i, 0)),
                   row(QKV_WIDTH), row(GDN_WIDTH), row(D), row(D), row(LANES)),
        scratch_shapes=[pltpu.VMEM((tm, LANES), F32)],
        compiler_params=_cp(("parallel", "parallel")),
        name="norm_in_proj_seq",
    )(x, g, mod, mod, w_in, w_gates, w_ab)


GROUPS_PER_SLAB = S5_GROUPS // SLABS


def _s5_prep_kernel(lrb, lib, dtb, bre, bim, lrc, lic, dtc, cre, cim, lrn, lin, dtn,
                    be_ref, bst_ref, c0_ref, cpm_ref, tp_ref, pt_ref, a1_ref, cpe_scr, *, seg):
    W = SLAB_STATE

    def disc(lr, li, ldt):
        dt = jnp.exp(ldt)
        mag = jnp.exp(lr * dt)
        return mag * jnp.cos(li * dt), mag * jnp.sin(li * dt)

    def cmul(xr, xi, yr, yi):
        return xr * yr - xi * yi, xr * yi + xi * yr

    lr, li = lrb[0], lib[0]
    ar, ai = disc(lr, li, dtb[0])
    den = lr * lr + li * li
    nr = ar - 1.0
    kr = (nr * lr + ai * li) / den
    ki = (ai * lr - nr * li) / den
    br, bi = bre[0], bim[0]
    bbr = kr * br - ki * bi
    bbi = kr * bi + ki * br
    rgrp = lax.broadcasted_iota(jnp.int32, (LANES, LANES), 0) // S5_GROUP
    lane_hi = lax.broadcasted_iota(jnp.int32, (LANES, LANES), 1) // S5_STATE
    pr, pi = jnp.ones_like(ar), jnp.zeros_like(ar)
    for d in range(S5_T):
        t = S5_T - 1 - d
        for ri, val in enumerate(cmul(pr, pi, bbr, bbi)):
            two = jnp.concatenate([val, val], axis=1)
            for m in range(GROUPS_PER_SLAB // 2):
                tile = jnp.where(rgrp == 2 * m + lane_hi, two, 0.0)
                c0 = ri * W + m * LANES
                be_ref[0, 0, t * LANES:(t + 1) * LANES, c0:c0 + LANES] = tile.astype(BF16)
                if d == 0:
                    bst_ref[0, 0, :, c0:c0 + LANES] = tile
        pr, pi = cmul(pr, pi, ar, ai)

    ar, ai = disc(lrc[0], lic[0], dtc[0])
    cr, ci = cre[0], cim[0]
    own = (lax.broadcasted_iota(jnp.int32, (W, LANES), 0) // S5_STATE
           == lax.broadcasted_iota(jnp.int32, (W, LANES), 1) // S5_GROUP)
    pr, pi = jnp.ones_like(ar), jnp.zeros_like(ar)
    for d in range(S5_T + 1):
        vr, vi = cmul(cr, ci, pr, pi)
        for ri, val in enumerate((vr, -vi)):
            tile = jnp.where(own, val, 0.0)
            cpe_scr[d, ri * W:(ri + 1) * W, :] = tile
            if d == 0:
                c0_ref[0, 0, ri * W:(ri + 1) * W, :] = tile
            if d >= 1:
                cpm_ref[0, 0, ri * W:(ri + 1) * W, (d - 1) * LANES:d * LANES] = tile.astype(BF16)
        pr, pi = cmul(pr, pi, ar, ai)

    bst = bst_ref[0, 0]
    lag = [_dot(bst, cpe_scr[d], HI).astype(BF16) for d in range(S5_T)]
    for dd in range(S5_T // 2):
        tp_ref[0, 0, dd, 0:LANES, 0:LANES] = lag[2 * dd]
        tp_ref[0, 0, dd, LANES:, LANES:] = lag[2 * dd]
        tp_ref[0, 0, dd, 0:LANES, LANES:] = lag[2 * dd + 1]
        tp_ref[0, 0, dd, LANES:, 0:LANES] = lag[2 * dd - 1] if dd else jnp.zeros((LANES, LANES), BF16)

    ar, ai = disc(lrn[0, 0], lin[0, 0], dtn[0, 0])
    a1_ref[0, 0, :, 0:W] = ar
    a1_ref[0, 0, :, W:2 * W] = ai
    tr, ti = ar, ai
    for _ in range(S5_T - 1):
        tr, ti = cmul(tr, ti, ar, ai)
    pr, pi = jnp.ones_like(ar), jnp.zeros_like(ar)
    for i in range(seg + 1):
        pt_ref[0, 0, i:i + 1, 0:W] = pr
        pt_ref[0, 0, i:i + 1, W:2 * W] = pi
        pr, pi = cmul(pr, pi, tr, ti)


def _s5_prep(lam_re, lam_im, log_dt, b_re, b_im, c_re, c_im, seg):
    G, P, C = S5_GROUPS, S5_STATE, S5_GROUP
    W2 = 2 * SLAB_STATE
    dt3 = jnp.broadcast_to(log_dt[:, :, None], (DEPTH, G, P))
    rows_b = lambda a: jnp.repeat(a, C, axis=1)
    bt = lambda a: a.transpose(0, 1, 3, 2).reshape(DEPTH, G * C, P)
    rows_c = lambda a: jnp.broadcast_to(a.reshape(DEPTH, G * P, 1), (DEPTH, G * P, LANES))
    ct = lambda a: jnp.tile(a.transpose(0, 1, 3, 2).reshape(DEPTH, G * P, C), (1, 1, LANES // C))
    nat = lambda a: a.reshape(DEPTH, SLABS, 1, SLAB_STATE)
    args = (rows_b(lam_re), rows_b(lam_im), rows_b(dt3), bt(b_re), bt(b_im),
            rows_c(lam_re), rows_c(lam_im), rows_c(dt3), ct(c_re), ct(c_im),
            nat(lam_re), nat(lam_im), nat(dt3))
    bspec = pl.BlockSpec((1, LANES, P), lambda l, k: (l, k, 0))
    cspec = pl.BlockSpec((1, SLAB_STATE, LANES), lambda l, k: (l, k, 0))
    nspec = pl.BlockSpec((1, 1, 1, SLAB_STATE), lambda l, k: (l, k, 0, 0))
    return pl.pallas_call(
        functools.partial(_s5_prep_kernel, seg=seg),
        out_shape=(jax.ShapeDtypeStruct((DEPTH, SLABS, S5_T * LANES, W2), BF16),
                   jax.ShapeDtypeStruct((DEPTH, SLABS, LANES, W2), F32),
                   jax.ShapeDtypeStruct((DEPTH, SLABS, W2, LANES), F32),
                   jax.ShapeDtypeStruct((DEPTH, SLABS, W2, S5_T * LANES), BF16),
                   jax.ShapeDtypeStruct((DEPTH, SLABS, S5_T // 2, 2 * LANES, 2 * LANES), BF16),
                   jax.ShapeDtypeStruct((DEPTH, SLABS, seg + 1, W2), F32),
                   jax.ShapeDtypeStruct((DEPTH, SLABS, 1, W2), F32)),
        grid=(DEPTH, SLABS),
        in_specs=[bspec] * 5 + [cspec] * 5 + [nspec] * 3,
        out_specs=(pl.BlockSpec((1, 1, S5_T * LANES, W2), lambda l, k: (l, k, 0, 0)),
                   pl.BlockSpec((1, 1, LANES, W2), lambda l, k: (l, k, 0, 0)),
                   pl.BlockSpec((1, 1, W2, LANES), lambda l, k: (l, k, 0, 0)),
                   pl.BlockSpec((1, 1, W2, S5_T * LANES), lambda l, k: (l, k, 0, 0)),
                   pl.BlockSpec((1, 1, S5_T // 2, 2 * LANES, 2 * LANES), lambda l, k: (l, k, 0, 0, 0)),
                   pl.BlockSpec((1, 1, seg + 1, W2), lambda l, k: (l, k, 0, 0)),
                   pl.BlockSpec((1, 1, 1, W2), lambda l, k: (l, k, 0, 0))),
        scratch_shapes=[pltpu.VMEM((S5_T + 1, W2, LANES), F32)],
        compiler_params=_cp(("parallel", "parallel")),
        name="s5_discretize",
    )(*args)


def _s5_seq_kernel(up_ref, be_ref, tp_ref, cpm_ref, pt_ref, s0_ref, dsk_ref,
                   yg_ref, sfin_ref, e_scr, sx_scr, *, nc):
    seg = nc // 8
    W = SLAB_STATE
    nt = W // LANES
    ub = up_ref[0, 0]
    u = ub.astype(F32)
    e = _dot(ub, be_ref[0, 0])
    pitch = e_scr.shape[1] // 8
    for c in range(2 * nt):
        for j in range(8):
            e_scr[c, j * pitch:j * pitch + seg, :] = e[j * seg:(j + 1) * seg, c * LANES:(c + 1) * LANES]

    def tiles(row):
        return [(row[:, c * LANES:(c + 1) * LANES], row[:, W + c * LANES:W + (c + 1) * LANES])
                for c in range(nt)]

    a8 = [(jnp.broadcast_to(r, (8, LANES)), jnp.broadcast_to(i, (8, LANES)))
          for r, i in tiles(pt_ref[0, 0, 1:2, :])]

    def step(i, carry):
        rows = pl.ds(i, 8, stride=pitch)
        new = []
        for c in range(nt):
            sr, si = carry[c]
            ar, ai = a8[c]
            sx_scr[c, rows, :] = sr
            sx_scr[nt + c, rows, :] = si
            new.append((ar * sr - ai * si + e_scr[c, rows, :],
                        ar * si + ai * sr + e_scr[nt + c, rows, :]))
        return tuple(new)

    zero = jnp.zeros((8, LANES), F32)
    ends = lax.fori_loop(0, seg, step, tuple((zero, zero) for _ in range(nt)))

    al = tiles(pt_ref[0, 0, seg:seg + 1, :])
    cur = tiles(s0_ref[0, 0])
    car = []
    for c in range(nt):
        alr, ali = al[c]
        cr, ci = cur[c]
        sr, si = ends[c]
        crs, cis = [], []
        for j in range(8):
            crs.append(cr)
            cis.append(ci)
            cr, ci = (alr * cr - ali * ci + sr[j:j + 1], alr * ci + ali * cr + si[j:j + 1])
        sfin_ref[0, 0, :, c * LANES:(c + 1) * LANES] = cr
        sfin_ref[0, 0, :, W + c * LANES:W + (c + 1) * LANES] = ci
        car.append((jnp.concatenate(crs, axis=0), jnp.concatenate(cis, axis=0)))

    def corr(i, _):
        rows = pl.ds(i, 8, stride=pitch)
        pw = tiles(pt_ref[0, 0, pl.ds(i, 1), :])
        for c in range(nt):
            pr, pi = pw[c]
            cr, ci = car[c]
            sx_scr[c, rows, :] = sx_scr[c, rows, :] + (pr * cr - pi * ci)
            sx_scr[nt + c, rows, :] = sx_scr[nt + c, rows, :] + (pr * ci + pi * cr)
        return 0

    lax.fori_loop(0, seg, corr, 0)

    sx = jnp.concatenate(
        [jnp.concatenate([sx_scr[c, j * pitch:j * pitch + seg, :] for j in range(8)], axis=0)
         for c in range(2 * nt)], axis=-1)
    y = _dot(sx.astype(BF16), cpm_ref[0, 0])
    TW = 2 * LANES
    for tq in range(S5_T // 2):
        acc = y[:, tq * TW:(tq + 1) * TW]
        for tpi in range(tq + 1):
            acc = acc + _dot(ub[:, tpi * TW:(tpi + 1) * TW], tp_ref[0, 0, tq - tpi])
        acc = acc + dsk_ref[0, :, tq * TW:(tq + 1) * TW] * u[:, tq * TW:(tq + 1) * TW]
        yg_ref[0, 0, :, tq * TW:(tq + 1) * TW] = jax.nn.gelu(acc).astype(yg_ref.dtype)


def _s5_seq(up, be_emb, tp, cpm, pt, s0, dsk, l):
    _, B, nc, _ = up.shape
    seg = nc // 8
    W2 = 2 * SLAB_STATE
    yg, sfin = pl.pallas_call(
        functools.partial(_s5_seq_kernel, nc=nc),
        out_shape=(jax.ShapeDtypeStruct((SLABS, B, nc, S5_T * LANES), BF16),
                   jax.ShapeDtypeStruct((SLABS, B, 1, W2), F32)),
        grid=(SLABS, B),
        in_specs=[pl.BlockSpec((1, 1, nc, S5_T * LANES), lambda k, b: (k, b, 0, 0)),
                  pl.BlockSpec((1, 1, S5_T * LANES, W2), lambda k, b: (l, k, 0, 0)),
                  pl.BlockSpec((1, 1, S5_T // 2, 2 * LANES, 2 * LANES), lambda k, b: (l, k, 0, 0, 0)),
                  pl.BlockSpec((1, 1, W2, S5_T * LANES), lambda k, b: (l, k, 0, 0)),
                  pl.BlockSpec((1, 1, seg + 1, W2), lambda k, b: (l, k, 0, 0)),
                  pl.BlockSpec((1, 1, 1, W2), lambda k, b: (k, b, 0, 0)),
                  pl.BlockSpec((1, 1, S5_T * LANES), lambda k, b: (k, 0, 0))],
        out_specs=(pl.BlockSpec((1, 1, nc, S5_T * LANES), lambda k, b: (k, b, 0, 0)),
                   pl.BlockSpec((1, 1, 1, W2), lambda k, b: (k, b, 0, 0))),
        scratch_shapes=[pltpu.VMEM((W2 // LANES, 8 * (seg + S5_SEG_PAD), LANES), F32),
                        pltpu.VMEM((W2 // LANES, 8 * (seg + S5_SEG_PAD), LANES), F32)],
        compiler_params=_cp(("parallel", "parallel")),
        name="s5_seq",
    )(up, be_emb, tp, cpm, pt, s0, dsk)
    return yg, sfin


def _s5_step_kernel(u_ref, b_ref, c_ref, a_ref, s0_ref, d_ref, yg_ref, s1_ref):
    W = SLAB_STATE
    u = u_ref[0]
    bu = _dot(u, b_ref[0, 0], HI)
    ar = a_ref[0, 0, :, 0:W]
    ai = a_ref[0, 0, :, W:2 * W]
    sr = s0_ref[0, :, 0:W]
    si = s0_ref[0, :, W:2 * W]
    nr = ar * sr - ai * si + bu[:, 0:W]
    ni = ar * si + ai * sr + bu[:, W:2 * W]
    s1_ref[0, :, 0:W] = nr
    s1_ref[0, :, W:2 * W] = ni
    s1 = jnp.concatenate([nr, ni], axis=-1)
    y = _dot(s1, c_ref[0, 0], HI) + d_ref[0] * u
    yg_ref[0] = jax.nn.gelu(y)


def _s5_step(u_slab, bst, c0, a1, s0, d1, l):
    _, N, _ = u_slab.shape
    W2 = 2 * SLAB_STATE
    return pl.pallas_call(
        _s5_step_kernel,
        out_shape=(jax.ShapeDtypeStruct((SLABS, N, LANES), F32),
                   jax.ShapeDtypeStruct((SLABS, N, W2), F32)),
        grid=(SLABS,),
        in_specs=[pl.BlockSpec((1, N, LANES), lambda k: (k, 0, 0)),
                  pl.BlockSpec((1, 1, LANES, W2), lambda k: (l, k, 0, 0)),
                  pl.BlockSpec((1, 1, W2, LANES), lambda k: (l, k, 0, 0)),
                  pl.BlockSpec((1, 1, 1, W2), lambda k: (l, k, 0, 0)),
                  pl.BlockSpec((1, N, W2), lambda k: (k, 0, 0)),
                  pl.BlockSpec((1, 1, LANES), lambda k: (k, 0, 0))],
        out_specs=(pl.BlockSpec((1, N, LANES), lambda k: (k, 0, 0)),
                   pl.BlockSpec((1, N, W2), lambda k: (k, 0, 0))),
        compiler_params=_cp(("parallel",)),
        name="s5_step",
    )(u_slab, bst, c0, a1, s0, d1)


def _l2n(x):
    return x * lax.rsqrt(jnp.sum(x * x, axis=-1, keepdims=True) + L2_EPS)


def _split_bf16(x):
    hi = x.astype(BF16)
    return hi, (x - hi.astype(F32)).astype(BF16)


def _unit_lower_solve(As, rhss):
    n = GDN_C
    row = lax.broadcasted_iota(jnp.int32, (n, n), 0)
    col = lax.broadcasted_iota(jnp.int32, (n, n), 1)
    eye = (row == col).astype(F32)
    same8 = (row // 8) == (col // 8)
    Qs = [jnp.where(same8, -A, 0.0) for A in As]
    invs = [eye + Q for Q in Qs]
    for _ in range(2):
        Qs = [_dotb(Q, Q) for Q in Qs]
        invs = [inv + _dotb(inv, Q) for inv, Q in zip(invs, Qs)]
    s = 8
    while s < n:
        sib = ((row // (2 * s)) == (col // (2 * s))) & ((row // s) != (col // s))
        offs = [jnp.where(sib, A, 0.0).astype(BF16) for A in As]
        invb = [inv.astype(BF16) for inv in invs]
        tmp = [_dot(off, ib) for off, ib in zip(offs, invb)]
        invs = [inv - _dot(ib, t.astype(BF16)) for inv, ib, t in zip(invs, invb, tmp)]
        s *= 2
    invb = [inv.astype(BF16) for inv in invs]
    x0s = [_dot(ib, rhs.astype(BF16)) for ib, rhs in zip(invb, rhss)]
    res = []
    for A, x0, rhs in zip(As, x0s, rhss):
        ah, al = _split_bf16(A)
        xh, xl = _split_bf16(x0)
        res.append(rhs - x0 - (_dot(ah, xh) + _dot(ah, xl) + _dot(al, xh)))
    return [x0 + _dot(ib, r.astype(BF16)) for x0, ib, r in zip(x0s, invb, res)]


def _gdn_tile(qc_scr, gc, beta, z_ref, nw, o_ref, s_scr, tl, r0):
    C, DK, H = GDN_C, GDN_DK, GDN_HEADS
    nchunk = tl // C
    probs = [(c, h) for c in range(nchunk) for h in range(H)]
    row = lax.broadcasted_iota(jnp.int32, (C, C), 0)
    col = lax.broadcasted_iota(jnp.int32, (C, C), 1)
    tri = row >= col
    strict = row > col

    def blk(c, off):
        return qc_scr[c * C:(c + 1) * C, off:off + DK]

    q = [_l2n(blk(c, h * DK)) * (DK ** -0.5) for c, h in probs]
    k = [_l2n(blk(c, GDN_WIDTH + h * DK)) for c, h in probs]
    v = [blk(c, 2 * GDN_WIDTH + h * DK) for c, h in probs]
    gcb = [jnp.broadcast_to(gc[c * C:(c + 1) * C, h:h + 1], (C, DK)) for c, h in probs]
    bb = [jnp.broadcast_to(beta[c * C:(c + 1) * C, H + h:H + h + 1], (C, DK)) for c, h in probs]
    decay = []
    for g in gcb:
        diff = g - g.T
        decay.append(jnp.where(tri, jnp.exp(jnp.where(tri, diff, 0.0)), 0.0))
    kbf = [x.astype(BF16) for x in k]
    kb = [x * b for x, b in zip(k, bb)]
    A = [jnp.where(strict, _dot_nt(x.astype(BF16), y) * d, 0.0) for x, y, d in zip(kb, kbf, decay)]
    egc = [jnp.exp(g) for g in gcb]
    rhs = [jnp.concatenate([x * b, y * e], axis=-1) for x, b, y, e in zip(v, bb, kb, egc)]
    sol = _unit_lower_solve(A, rhs)
    attn = [jnp.where(tri, _dot_nt(x.astype(BF16), y) * d, 0.0).astype(BF16)
            for x, y, d in zip(q, kbf, decay)]
    glast = [g[C - 1:C, :] for g in gcb]
    wq = [jnp.concatenate([s[:, DK:], x * e], axis=0).astype(BF16) for s, x, e in zip(sol, q, egc)]
    kg = [(x * jnp.exp(gl - g)).astype(BF16) for x, gl, g in zip(k, glast, gcb)]

    for c in range(nchunk):
        ps = [c * H + h for h in range(H)]
        S = [s_scr[h] for h in range(H)]
        ws = [_dot(wq[p], S[h].astype(BF16)) for h, p in enumerate(ps)]
        v_new = [sol[p][:, 0:DK] - w[0:C] for p, w in zip(ps, ws)]
        vb = [x.astype(BF16) for x in v_new]
        o = [w[C:] + _dot(attn[p], x) for p, w, x in zip(ps, ws, vb)]
        for h, p in enumerate(ps):
            s_scr[h] = S[h] * jnp.exp(glast[p]) + _dot_tn(kg[p], vb[h])
            zh = z_ref[0, r0 + c * C:r0 + (c + 1) * C, h * DK:(h + 1) * DK].astype(F32)
            on = o[h] * lax.rsqrt(jnp.mean(o[h] * o[h], axis=-1, keepdims=True) + NORM_EPS) * nw
            o_ref[0, r0 + c * C:r0 + (c + 1) * C, h * DK:(h + 1) * DK] = (on * _silu(zh)).astype(o_ref.dtype)


def _gdn_seq_kernel(qkv_ref, z_ref, ab_ref, cw_ref, alog_ref, dtb_ref, nw_ref, conv0_ref, s0_ref,
                    o_ref, sfin_ref, xp_scr, qc_scr, s_scr, *, tl):
    lt = pl.program_id(1)

    @pl.when(lt == 0)
    def _():
        xp_scr[0:8, :] = jnp.zeros((8, QKV_WIDTH), F32)
        xp_scr[8 - (GDN_CONV - 1):8, :] = conv0_ref[0]
        s_scr[...] = s0_ref[0]

    sub = qc_scr.shape[0]
    for r0 in range(0, tl, sub):
        xp_scr[8:8 + sub, :] = qkv_ref[0, r0:r0 + sub, :].astype(F32)
        conv = cw_ref[0, 0:1, :] * xp_scr[5:5 + sub, :]
        for j in range(1, GDN_CONV):
            conv = conv + cw_ref[0, j:j + 1, :] * xp_scr[5 + j:5 + j + sub, :]
        xp_scr[0:8, :] = xp_scr[sub:sub + 8, :]
        qc_scr[...] = _silu(conv)

        ab = ab_ref[0, r0:r0 + sub, :]
        g = -jnp.exp(alog_ref[...]) * jax.nn.softplus(ab + dtb_ref[...])
        beta = jax.nn.sigmoid(ab)
        row = lax.broadcasted_iota(jnp.int32, (sub, sub), 0)
        col = lax.broadcasted_iota(jnp.int32, (sub, sub), 1)
        csum = ((row >= col) & ((row // GDN_C) == (col // GDN_C))).astype(F32)
        gc = _dot(csum, g, HI)
        _gdn_tile(qc_scr, gc, beta, z_ref, nw_ref[...], o_ref, s_scr, sub, r0)

    @pl.when(lt == pl.num_programs(1) - 1)
    def _():
        sfin_ref[0] = s_scr[...]


def _gdn_seq(qkv, z, ab, conv_w, alog, dtb, nw, conv0, s0, l):
    B, L, _ = qkv.shape
    tl = min(GDN_BLOCK, L)
    sub = min(GDN_SUB, tl)
    return pl.pallas_call(
        functools.partial(_gdn_seq_kernel, tl=tl),
        out_shape=(jax.ShapeDtypeStruct((B, L, GDN_WIDTH), BF16),
                   jax.ShapeDtypeStruct((B, GDN_HEADS, GDN_DK, GDN_DK), F32)),
        grid=(B, L // tl),
        in_specs=[pl.BlockSpec((1, tl, QKV_WIDTH), lambda b, i: (b, i, 0)),
                  pl.BlockSpec((1, tl, GDN_WIDTH), lambda b, i: (b, i, 0)),
                  pl.BlockSpec((1, tl, LANES), lambda b, i: (b, i, 0)),
                  pl.BlockSpec((1, GDN_CONV, QKV_WIDTH), lambda b, i: (l, 0, 0)),
                  pl.BlockSpec((1, LANES), lambda b, i: (0, 0)),
                  pl.BlockSpec((1, LANES), lambda b, i: (0, 0)),
                  pl.BlockSpec((1, GDN_DK), lambda b, i: (0, 0)),
                  pl.BlockSpec((1, GDN_CONV - 1, QKV_WIDTH), lambda b, i: (b, 0, 0)),
                  pl.BlockSpec((1, GDN_HEADS, GDN_DK, GDN_DK), lambda b, i: (b, 0, 0, 0))],
        out_specs=(pl.BlockSpec((1, tl, GDN_WIDTH), lambda b, i: (b, i, 0)),
                   pl.BlockSpec((1, GDN_HEADS, GDN_DK, GDN_DK), lambda b, i: (b, 0, 0, 0))),
        scratch_shapes=[pltpu.VMEM((sub + 8, QKV_WIDTH), F32),
                        pltpu.VMEM((sub, QKV_WIDTH), F32),
                        pltpu.VMEM((GDN_HEADS, GDN_DK, GDN_DK), F32)],
        compiler_params=_cp(("parallel", "arbitrary")),
        name="gdn_seq",
    )(qkv, z, ab, conv_w, alog, dtb, nw, conv0, s0)


GDN_STEP_ROWS = 8


def _gdn_step_kernel(qkv_ref, z_ref, ab_ref, cw_ref, alog_ref, dtb_ref, nw_ref, conv0_ref, s0_ref,
                     *rest):
    if len(rest) == 3:
        prev_ref, o_ref, s1_all = rest
        s1_all[0] = prev_ref[...]
        s1_ref = s1_all.at[1]
    else:
        o_ref, s1_all = rest
        s1_ref = s1_all
    nb = GDN_STEP_ROWS
    W = QKV_WIDTH
    conv = cw_ref[0:1, :] * conv0_ref[:, 0:W]
    conv = conv + cw_ref[1:2, :] * conv0_ref[:, W:2 * W]
    conv = conv + cw_ref[2:3, :] * conv0_ref[:, 2 * W:3 * W]
    conv = conv + cw_ref[3:4, :] * qkv_ref[...]
    qc = _silu(conv)
    ab = ab_ref[...]
    eg = jnp.exp(-jnp.exp(alog_ref[...]) * jax.nn.softplus(ab + dtb_ref[...]))
    beta = jax.nn.sigmoid(ab)
    eye = (lax.broadcasted_iota(jnp.int32, (GDN_DK, GDN_DK), 0)
           == lax.broadcasted_iota(jnp.int32, (GDN_DK, GDN_DK), 1)).astype(F32)
    for h in range(GDN_HEADS):
        q = _l2n(qc[:, h * GDN_DK:(h + 1) * GDN_DK]) * (GDN_DK ** -0.5)
        k = _l2n(qc[:, GDN_WIDTH + h * GDN_DK:GDN_WIDTH + (h + 1) * GDN_DK])
        v = qc[:, 2 * GDN_WIDTH + h * GDN_DK:2 * GDN_WIDTH + (h + 1) * GDN_DK]
        kT = _dot_nt(eye, k, HI)
        qT = _dot_nt(eye, q, HI)
        qk = jnp.sum(q * k, axis=-1, keepdims=True)
        for j in range(nb):
            S = s0_ref[0, j, h]
            kc = jnp.broadcast_to(kT[:, j:j + 1], (GDN_DK, GDN_DK))
            qcb = jnp.broadcast_to(qT[:, j:j + 1], (GDN_DK, GDN_DK))
            kS = jnp.sum(kc * S, axis=0, keepdims=True)
            qS = jnp.sum(qcb * S, axis=0, keepdims=True)
            egj = eg[j:j + 1, h:h + 1]
            bj = beta[j:j + 1, GDN_HEADS + h:GDN_HEADS + h + 1]
            v_new = bj * v[j:j + 1, :] - (bj * egj) * kS
            o = egj * qS + qk[j:j + 1, :] * v_new
            s1_ref[j, h] = S * egj + kc * v_new
            zh = z_ref[j:j + 1, h * GDN_DK:(h + 1) * GDN_DK]
            on = o * lax.rsqrt(jnp.mean(o * o, axis=-1, keepdims=True) + NORM_EPS) * nw_ref[...]
            o_ref[j:j + 1, h * GDN_DK:(h + 1) * GDN_DK] = on * _silu(zh)


def _gdn_step(qkv, z, ab, conv_w, alog, dtb, nw, conv0, s_all, l, prev):
    N = qkv.shape[0]
    nb = GDN_STEP_ROWS
    row = lambda w: pl.BlockSpec((nb, w), lambda i: (i, 0))
    const = lambda r, w: pl.BlockSpec((r, w), lambda i: (0, 0))
    sblk = (nb, GDN_HEADS, GDN_DK, GDN_DK)
    one = pl.BlockSpec(sblk, lambda i: (i, 0, 0, 0))
    ins = [qkv, z, ab, conv_w, alog, dtb, nw, conv0, s_all]
    in_specs = [row(QKV_WIDTH), row(GDN_WIDTH), row(LANES), const(GDN_CONV, QKV_WIDTH),
                const(1, LANES), const(1, LANES), const(1, GDN_DK), row(3 * QKV_WIDTH),
                pl.BlockSpec((1,) + sblk, lambda i: (l, i, 0, 0, 0))]
    if prev is None:
        s_shape, s_spec = jax.ShapeDtypeStruct((N,) + sblk[1:], F32), one
    else:
        assert DEPTH == 2 and l == 1
        ins.append(prev)
        in_specs.append(one)
        s_shape = jax.ShapeDtypeStruct((DEPTH, N) + sblk[1:], F32)
        s_spec = pl.BlockSpec((DEPTH,) + sblk, lambda i: (0, i, 0, 0, 0))
    return pl.pallas_call(
        _gdn_step_kernel,
        out_shape=(jax.ShapeDtypeStruct((N, GDN_WIDTH), F32), s_shape),
        grid=(N // nb,),
        in_specs=in_specs,
        out_specs=(row(GDN_WIDTH), s_spec),
        compiler_params=_cp(("parallel",)),
        name="gdn_step",
    )(*ins)


def _merge_kernel(yg_ref, og_ref, ga_ref, gb_ref, x_ref, gt_ref, wglu_ref, wgo_ref, wout_ref,
                  gf_ref, scf_ref, shf_ref, *rest, hi, chunked, routed):
    if routed:
        wr_ref, xo_ref, h_ref, lg_ref, *scr = rest
    else:
        xo_ref, h_ref, *scr = rest
    if chunked:
        y_scr = scr[-1]
        scr = scr[:-1]
        nrow = y_scr.shape[1] // S5_T
        for k in range(SLABS):
            for t in range(S5_T):
                y_scr[k, pl.ds(t, nrow, stride=S5_T), :] = (
                    yg_ref[k, 0, :, t * LANES:(t + 1) * LANES].astype(F32))
        y = jnp.concatenate([y_scr[k] for k in range(SLABS)], axis=-1)
    else:
        y = jnp.concatenate([yg_ref[k, 0] for k in range(SLABS)], axis=-1)
    if hi:
        wglu, wgo, wout = wglu_ref[0], wgo_ref[0], wout_ref[0]
        mm = lambda a, w: _dot(a, w, HI)
    else:
        wglu_s, wgo_s, wout_s = scr

        @pl.when((pl.program_id(0) == 0) & (pl.program_id(1) == 0))
        def _():
            wglu_s[...] = wglu_ref[0].astype(BF16)
            wgo_s[...] = wgo_ref[0].astype(BF16)
            wout_s[...] = wout_ref[0].astype(BF16)

        wglu, wgo, wout = wglu_s[...], wgo_s[...], wout_s[...]
        mm = lambda a, w: _dot(a.astype(BF16), w)

    glu = mm(y, wglu)
    branch_a = glu[:, 0:D_MODEL] * jax.nn.sigmoid(glu[:, D_MODEL:])
    branch_b = mm(og_ref[0], wgo)
    merged = ga_ref[0].astype(F32) * branch_a + gb_ref[0].astype(F32) * branch_b
    out = mm(merged, wout)
    x = x_ref[0] + gt_ref[0, 0] * out
    xo_ref[0] = x
    ms = jnp.mean(x * x, axis=-1, keepdims=True)
    h = x * lax.rsqrt(ms + NORM_EPS) * gf_ref[0]
    h = h * (1.0 + scf_ref[0, 0]) + shf_ref[0, 0]
    h_ref[0] = h.astype(h_ref.dtype)
    if routed:
        lg_ref[0] = _dot_nt(wr_ref[0], h, HI)


def _merge(yg, og, ga, gb, x, mod, wglu, wgo, wout, gf, wr, *, l, tm, hi, chunked, h_dtype):
    B, L, D = x.shape
    routed = l % 2 == 1
    row = lambda w: pl.BlockSpec((1, tm, w), lambda b, i: (b, i, 0))
    layer = lambda r, w, ll=l: pl.BlockSpec((1, r, w), lambda b, i: (ll, 0, 0))
    scratch = [] if hi else [pltpu.VMEM((S5_WIDTH, 2 * D), BF16), pltpu.VMEM((GDN_WIDTH, D), BF16),
                             pltpu.VMEM((D, D), BF16)]
    if chunked:
        scratch = scratch + [pltpu.VMEM((SLABS, tm, LANES), F32)]
        yg_spec = pl.BlockSpec((SLABS, 1, tm // S5_T, S5_T * LANES), lambda b, i: (0, b, i, 0))
    else:
        yg_spec = pl.BlockSpec((SLABS, 1, tm, LANES), lambda b, i: (0, b, i, 0))
    lg_shape = jax.ShapeDtypeStruct((B, N_EXPERTS, L), F32)
    lg_spec = pl.BlockSpec((1, N_EXPERTS, tm), lambda b, i: (b, 0, i))
    outs = pl.pallas_call(
        functools.partial(_merge_kernel, hi=hi, chunked=chunked, routed=routed),
        out_shape=(jax.ShapeDtypeStruct((B, L, D), F32),
                   jax.ShapeDtypeStruct((B, L, D), h_dtype)) + ((lg_shape,) if routed else ()),
        grid=(B, L // tm),
        in_specs=[yg_spec,
                  row(GDN_WIDTH), row(D), row(D), row(D), _mod_spec(mod, l, 2, tm),
                  layer(S5_WIDTH, 2 * D), layer(GDN_WIDTH, D), layer(D, D),
                  layer(1, D), _mod_spec(mod, l, 4, tm), _mod_spec(mod, l, 3, tm)]
        + ([layer(N_EXPERTS, D, l // 2)] if routed else []),
        out_specs=(row(D), row(D)) + ((lg_spec,) if routed else ()),
        scratch_shapes=scratch,
        compiler_params=_cp(("arbitrary", "arbitrary")),
        name="merge_out_proj",
    )(yg, og, ga, gb, x, mod, wglu, wgo, wout, gf, mod, mod, *((wr,) if routed else ()))
    return outs if routed else (*outs, None)


FF_TILE = 512
FFN_ROWS = 512


def _finish(x, gfin_ref, final):
    if not final:
        return x
    ms = jnp.mean(x * x, axis=-1, keepdims=True)
    return x * lax.rsqrt(ms + NORM_EPS) * gfin_ref[...]


def _ffn_kernel(h_ref, x_ref, gt_ref, wg_ref, wu_ref, wd_ref, gfin_ref, o_ref, acc_scr, *, hi, final):
    j = pl.program_id(2)
    if hi:
        wg, wu, wd = wg_ref[...], wu_ref[...], wd_ref[...]
        mm = lambda a, w: _dot(a, w, HI)
    else:
        wg, wu, wd = wg_ref[...].astype(BF16), wu_ref[...].astype(BF16), wd_ref[...].astype(BF16)
        mm = lambda a, w: _dot(a.astype(BF16), w)

    @pl.when(j == 0)
    def _():
        acc_scr[...] = jnp.zeros_like(acc_scr)

    tm = acc_scr.shape[0]
    sub = min(FFN_ROWS, tm)
    for s in range(tm // sub):
        rows = slice(s * sub, (s + 1) * sub)
        hb = h_ref[0, rows, :]
        act = _silu(mm(hb, wg)) * mm(hb, wu)
        acc_scr[rows, :] = acc_scr[rows, :] + mm(act, wd)

    @pl.when(j == pl.num_programs(2) - 1)
    def _():
        o_ref[0] = _finish(x_ref[0] + gt_ref[0, 0] * acc_scr[...], gfin_ref, final)


def _ffn(h, x, mod, w_gu, w_down, gfin, *, l, tm, hi, final):
    B, L, D = x.shape
    nj = D_FF // FF_TILE
    row = pl.BlockSpec((1, tm, D), lambda b, i, j: (b, i, 0))
    return pl.pallas_call(
        functools.partial(_ffn_kernel, hi=hi, final=final),
        out_shape=jax.ShapeDtypeStruct((B, L, D), F32),
        grid=(B, L // tm, nj),
        in_specs=[row, row, _mod_spec(mod, l, 5, tm),
                  pl.BlockSpec((D, FF_TILE), lambda b, i, j: (0, j)),
                  pl.BlockSpec((D, FF_TILE), lambda b, i, j: (0, nj + j)),
                  pl.BlockSpec((FF_TILE, D), lambda b, i, j: (j, 0)),
                  pl.BlockSpec((1, D), lambda b, i, j: (0, 0))],
        out_specs=row,
        scratch_shapes=[pltpu.VMEM((tm, D), F32)],
        compiler_params=_cp(("parallel", "parallel", "arbitrary")),
        name="ffn_dense",
    )(h, x, mod, w_gu, w_gu, w_down, gfin)


ROUTE_TM = 512
ROW_DMA_TM = 512
MOE_SUP = 2048
MOE_SUB = 512
MOE_FF_TILE = 512


def _route_kernel(lg_ref, br_ref, cnt0_ref, slot_ref, wt_ref, cnt_ref, carry_scr, *, cap):
    @pl.when((pl.program_id(0) == 0) & (pl.program_id(1) == 0))
    def _():
        carry_scr[...] = cnt0_ref[...]

    lg = lg_ref[0] + br_ref[...]
    tm = lg.shape[1]
    eidx = lax.broadcasted_iota(jnp.int32, lg.shape, 0)
    m1 = jnp.max(lg, axis=0, keepdims=True)
    i1 = jnp.min(jnp.where(lg == m1, eidx, N_EXPERTS), axis=0, keepdims=True)
    lg2 = jnp.where(eidx == i1, -jnp.inf, lg)
    m2 = jnp.max(lg2, axis=0, keepdims=True)
    i2 = jnp.min(jnp.where(lg2 == m2, eidx, N_EXPERTS), axis=0, keepdims=True)
    e2 = jnp.exp(m2 - m1)
    wt_ref[0, 0:1, :] = 1.0 / (1.0 + e2)
    wt_ref[0, 1:2, :] = e2 / (1.0 + e2)
    sel1 = eidx == i1
    sel2 = eidx == i2
    oh = jnp.where(sel1 | sel2, 1.0, 0.0)
    before = (lax.broadcasted_iota(jnp.int32, (tm, tm), 0)
              < lax.broadcasted_iota(jnp.int32, (tm, tm), 1)).astype(BF16)
    rank = carry_scr[:, 0:1] + _dot(oh.astype(BF16), before)
    r1 = jnp.sum(jnp.where(sel1, rank, 0.0), axis=0, keepdims=True).astype(jnp.int32)
    r2 = jnp.sum(jnp.where(sel2, rank, 0.0), axis=0, keepdims=True).astype(jnp.int32)
    slot_ref[0, 0:1, :] = i1 * cap + r1
    slot_ref[0, 1:2, :] = i2 * cap + r2
    carry_scr[...] = carry_scr[...] + jnp.sum(oh, axis=1, keepdims=True)
    cnt_ref[...] = carry_scr[...]


def _route_slots(lgT, b_r, cnt0, cap):
    B, E, L = lgT.shape
    tm = min(ROUTE_TM, L)
    return pl.pallas_call(
        functools.partial(_route_kernel, cap=cap),
        out_shape=(jax.ShapeDtypeStruct((B, 2, L), jnp.int32),
                   jax.ShapeDtypeStruct((B, 2, L), F32),
                   jax.ShapeDtypeStruct((E, LANES), F32)),
        grid=(B, L // tm),
        in_specs=[pl.BlockSpec((1, E, tm), lambda b, i: (b, 0, i)),
                  pl.BlockSpec((E, 1), lambda b, i: (0, 0)),
                  pl.BlockSpec((E, LANES), lambda b, i: (0, 0))],
        out_specs=(pl.BlockSpec((1, 2, tm), lambda b, i: (b, 0, i)),
                   pl.BlockSpec((1, 2, tm), lambda b, i: (b, 0, i)),
                   pl.BlockSpec((E, LANES), lambda b, i: (0, 0))),
        scratch_shapes=[pltpu.VMEM((E, LANES), F32)],
        compiler_params=_cp(("arbitrary", "arbitrary")),
        name="moe_route",
    )(lgT, b_r.reshape(E, 1), cnt0)


def _row_copy(src, dst, sem):
    return pltpu.make_async_copy(src, dst, sem)


def _slot_rows_kernel(start_ref, per_ref, code_ref, row_ref, *, cap):
    code = code_ref[...]
    shift = cap.bit_length() - 1
    e = lax.shift_right_logical(code, shift)
    r = code & (cap - 1)
    start = jnp.zeros_like(code)
    per = jnp.ones_like(code)
    for k in range(N_EXPERTS):
        start = jnp.where(e == k, start_ref[k], start)
        per = jnp.where(e == k, per_ref[k], per)
    q = jnp.floor((r.astype(F32) + 0.5) / per.astype(F32)).astype(jnp.int32)
    row_ref[...] = start + q * MOE_SUP + (r - q * per)


def _slot_rows(start, per, codes, cap):
    B, _, L = codes.shape
    tm = min(ROUTE_TM, L)
    spec = pl.BlockSpec((1, 2, tm), lambda b, i, st, pe: (b, 0, i))
    return pl.pallas_call(
        functools.partial(_slot_rows_kernel, cap=cap),
        out_shape=jax.ShapeDtypeStruct(codes.shape, jnp.int32),
        grid_spec=pltpu.PrefetchScalarGridSpec(
            num_scalar_prefetch=2, grid=(B, L // tm), in_specs=[spec], out_specs=spec),
        compiler_params=_cp(("parallel", "parallel")),
        name="moe_slot_rows",
    )(start, per, codes)


def _zeros_kernel(o_ref):
    o_ref[...] = jnp.zeros_like(o_ref)


def _zero_rows(n_rows, width):
    return pl.pallas_call(
        _zeros_kernel,
        out_shape=jax.ShapeDtypeStruct((n_rows, width), F32),
        grid=(n_rows // MOE_SUP,),
        out_specs=pl.BlockSpec((MOE_SUP, width), lambda i: (i, 0)),
        compiler_params=_cp(("parallel",)),
        name="moe_zero_rows",
    )()


def _dispatch_kernel(row_ref, h_ref, xs_in_ref, xs_ref, hbuf, sem):
    del xs_in_ref
    tm = h_ref.shape[1]
    t = pl.program_id(0) * pl.num_programs(1) + pl.program_id(1)
    last = pl.num_programs(0) * pl.num_programs(1) - 1
    slot = t % 2
    hbuf[slot] = h_ref[0]

    def issue(r, _):
        for k in range(2):
            row = row_ref[0, k, r]
            _row_copy(hbuf.at[slot, pl.ds(r, 1), :], xs_ref.at[pl.ds(row, 1), :], sem.at[slot]).start()
        return 0

    lax.fori_loop(0, tm, issue, 0, unroll=8)

    def drain(sl):
        for k in range(2):
            _row_copy(hbuf.at[sl], xs_ref.at[pl.ds(0, tm), :], sem.at[sl]).wait()

    @pl.when(t > 0)
    def _():
        drain(1 - slot)

    @pl.when(t == last)
    def _():
        drain(slot)


def _dispatch(rows, h, xs):
    B, L, D = h.shape
    n_rows = xs.shape[0]
    tm = min(ROW_DMA_TM, L)
    return pl.pallas_call(
        _dispatch_kernel,
        out_shape=jax.ShapeDtypeStruct((n_rows, D), F32),
        grid=(B, L // tm),
        in_specs=[pl.BlockSpec((1, 2, tm), lambda b, i: (b, 0, i), memory_space=pltpu.SMEM),
                  pl.BlockSpec((1, tm, D), lambda b, i: (b, i, 0)),
                  pl.BlockSpec(memory_space=pl.ANY)],
        out_specs=pl.BlockSpec(memory_space=pl.ANY),
        scratch_shapes=[pltpu.VMEM((2, tm, D), F32), pltpu.SemaphoreType.DMA((2,))],
        input_output_aliases={2: 0},
        compiler_params=_cp(("arbitrary", "arbitrary")),
        name="moe_dispatch",
    )(rows, h, xs)


def _moe_grp_kernel(ge_ref, gn_ref, x_ref, wg_ref, wu_ref, wd_ref, y_ref, xb_scr):
    g = pl.program_id(0)
    j = pl.program_id(1)
    nsub = gn_ref[g]
    wg = wg_ref[0].astype(BF16)
    wu = wu_ref[0].astype(BF16)
    wd = wd_ref[0].astype(BF16)
    nblk = MOE_SUP // MOE_SUB

    @pl.when(j == 0)
    def _():
        xb_scr[...] = x_ref[...].astype(BF16)
        y_ref[...] = jnp.zeros_like(y_ref)

    def block(s):
        rows = slice(s * MOE_SUB, (s + 1) * MOE_SUB)
        xb = xb_scr[rows, :]
        act = _silu(_dot(xb, wg)) * _dot(xb, wu)
        y_ref[rows, :] = y_ref[rows, :] + _dot(act.astype(BF16), wd)

    for n in range(1, nblk + 1):
        @pl.when(nsub == n)
        def _():
            for s in range(n):
                block(s)


def _moe_groups(counts, n_groups):
    nsup = (counts + MOE_SUP - 1) // MOE_SUP
    div = jnp.maximum(nsup, 1)
    per = jnp.maximum(((counts + div - 1) // div + MOE_SUB - 1) // MOE_SUB * MOE_SUB, MOE_SUB)
    ends = jnp.cumsum(nsup)
    first = ends - nsup
    total = ends[-1]
    g = jnp.arange(n_groups, dtype=jnp.int32)
    gc = jnp.minimum(g, total - 1)
    e_of = jnp.minimum(jnp.sum((gc[:, None] >= ends[None, :]).astype(jnp.int32), axis=1), N_EXPERTS - 1)
    left = jnp.minimum(counts[e_of] - (gc - first[e_of]) * per[e_of], per[e_of])
    nsub = jnp.clip((left + MOE_SUB - 1) // MOE_SUB, 0, MOE_SUP // MOE_SUB)
    gn = jnp.where(g < total, nsub, 0).astype(jnp.int32)
    return e_of, gn, (first * MOE_SUP).astype(jnp.int32), per.astype(jnp.int32)


def _moe_grouped(xs, ge, gn, w_gu, w_down):
    D = xs.shape[1]
    nj = D_FF // MOE_FF_TILE
    ng = xs.shape[0] // MOE_SUP
    jj = lambda j, gn, g: jnp.where(gn[g] > 0, j, nj - 1)
    return pl.pallas_call(
        _moe_grp_kernel,
        out_shape=jax.ShapeDtypeStruct(xs.shape, F32),
        grid_spec=pltpu.PrefetchScalarGridSpec(
            num_scalar_prefetch=2,
            grid=(ng, nj),
            in_specs=[pl.BlockSpec((MOE_SUP, D), lambda g, j, ge, gn: (g, 0)),
                      pl.BlockSpec((1, D, MOE_FF_TILE), lambda g, j, ge, gn: (ge[g], 0, jj(j, gn, g))),
                      pl.BlockSpec((1, D, MOE_FF_TILE), lambda g, j, ge, gn: (ge[g], 0, nj + jj(j, gn, g))),
                      pl.BlockSpec((1, MOE_FF_TILE, D), lambda g, j, ge, gn: (ge[g], jj(j, gn, g), 0))],
            out_specs=pl.BlockSpec((MOE_SUP, D), lambda g, j, ge, gn: (g, 0)),
            scratch_shapes=[pltpu.VMEM((MOE_SUP, D), BF16)],
        ),
        compiler_params=_cp(("arbitrary", "arbitrary")),
        name="moe_experts",
    )(ge, gn, xs, w_gu, w_gu, w_down)


def _combine_kernel(row_ref, next_ref, w_ref, x_ref, gt_ref, gfin_ref, ys_ref, o_ref, g_scr, sem, *, final):
    tm = x_ref.shape[1]
    t = pl.program_id(0) * pl.num_programs(1) + pl.program_id(1)
    last = pl.num_programs(0) * pl.num_programs(1) - 1
    slot = t % 2

    def gather(rows, sl):
        def issue(r, _):
            for k in range(2):
                row = rows[0, k, r]
                _row_copy(ys_ref.at[pl.ds(row, 1), :], g_scr.at[sl, k, pl.ds(r, 1), :], sem.at[sl]).start()
            return 0

        lax.fori_loop(0, tm, issue, 0, unroll=8)

    @pl.when(t == 0)
    def _():
        gather(row_ref, slot)

    @pl.when(t < last)
    def _():
        gather(next_ref, 1 - slot)

    for k in range(2):
        _row_copy(ys_ref.at[pl.ds(0, tm), :], g_scr.at[slot, k], sem.at[slot]).wait()
    w = w_ref[0]
    f = w[:, 0:1] * g_scr[slot, 0] + w[:, 1:2] * g_scr[slot, 1]
    o_ref[0] = _finish(x_ref[0] + gt_ref[0, 0] * f, gfin_ref, final)


def _combine(rows, wts, x, mod, gfin, ys, *, l, final):
    B, L, D = x.shape
    tm = min(ROW_DMA_TM, L)
    gt_spec = _mod_spec(mod, l, 5, tm)
    row = pl.BlockSpec((1, tm, D), lambda b, i: (b, i, 0))
    nl = L // tm

    def next_block(b, i):
        t1 = jnp.minimum(b * nl + i + 1, B * nl - 1)
        return (t1 // nl, 0, t1 % nl)

    return pl.pallas_call(
        functools.partial(_combine_kernel, final=final),
        out_shape=jax.ShapeDtypeStruct((B, L, D), F32),
        grid=(B, L // tm),
        in_specs=[pl.BlockSpec((1, 2, tm), lambda b, i: (b, 0, i), memory_space=pltpu.SMEM),
                  pl.BlockSpec((1, 2, tm), next_block, memory_space=pltpu.SMEM),
                  pl.BlockSpec((1, tm, 2), lambda b, i: (b, i, 0)),
                  row, gt_spec,
                  pl.BlockSpec((1, D), lambda b, i: (0, 0)),
                  pl.BlockSpec(memory_space=pl.ANY)],
        out_specs=row,
        scratch_shapes=[pltpu.VMEM((2, 2, tm, D), F32), pltpu.SemaphoreType.DMA((2,))],
        compiler_params=_cp(("arbitrary", "arbitrary")),
        name="moe_combine",
    )(rows, rows, wts.transpose(0, 2, 1), x, mod, gfin, ys)


def _moe_routed(groups, b_r, w_gu, w_down, gfin, *, l, final):
    D = groups[0][1].shape[-1]
    n_tok = sum(g[1].shape[0] * g[1].shape[1] for g in groups)
    cap = 1 << (n_tok - 1).bit_length()
    n_groups = 2 * n_tok // MOE_SUP + N_EXPERTS
    cnt = jnp.zeros((N_EXPERTS, LANES), F32)
    routed = []
    for _, _, _, lgT in groups:
        codes, wts, cnt = _route_slots(lgT, b_r, cnt, cap)
        routed.append((codes, wts))
    ge, gn, start, per = _moe_groups(cnt[:, 0].astype(jnp.int32), n_groups)
    rows = [_slot_rows(start, per, codes, cap) for codes, _ in routed]
    xs = _zero_rows(n_groups * MOE_SUP, D)
    for (h, _, _, _), r in zip(groups, rows):
        xs = _dispatch(r, h, xs)
    ys = _moe_grouped(xs, ge, gn, w_gu, w_down)
    return [_combine(r, wts, x, mod, gfin, ys, l=l, final=final)
            for (_, x, mod, _), (_, wts), r in zip(groups, routed, rows)]


def _pad_lanes(v):
    return jnp.pad(v.reshape(1, -1), ((0, 0), (0, LANES - v.shape[-1])))


def _mixer_layer(x, mod, states, p, s5m, l, prev_sg, *, seq):
    B, L, D = x.shape
    hi = not seq
    s5r0, s5i0, sg0, sc0 = states
    w_in, w_gates, w_ab = p['w_in_seq' if seq else 'w_in']
    if seq:
        u, qkv, z, ga, gb, ab = _proj_seq(x, p['g_mix'], mod, w_in, w_gates, w_ab, l=l, tm=min(1024, L))
    else:
        u, qkv, z, ga, gb, ab = _proj(x, p['g_mix'], mod, w_in, w_gates, w_ab, l=l, tm=L)
    alog = _pad_lanes(p['gdn_a_log'][l])
    dtb = _pad_lanes(p['gdn_dt_bias'][l])
    nw = p['gdn_norm_w'][l].reshape(1, GDN_DK)
    if seq:
        yg, sfin = _s5_seq(u, s5m['be'], s5m['tp'], s5m['cpm'], s5m['pt'],
                           jnp.zeros((SLABS, B, 1, 2 * SLAB_STATE), F32), s5m['dsk'][l], l)
        sfin = sfin.reshape(SLABS, B, 2, SLAB_STATE).transpose(2, 1, 0, 3)
        sr = sfin[0].reshape(B, S5_GROUPS, S5_STATE)
        si = sfin[1].reshape(B, S5_GROUPS, S5_STATE)
        og, sg = _gdn_seq(qkv, z, ab, p['gdn_conv_w'], alog, dtb, nw,
                          jnp.zeros((B, GDN_CONV - 1, QKV_WIDTH), F32),
                          jnp.zeros((B, GDN_HEADS, GDN_DK, GDN_DK), F32), l)
        cb = qkv[:, L - (GDN_CONV - 1):, :].astype(F32)
    else:
        n = L
        s0 = jnp.concatenate([s5r0[l].reshape(n, SLABS, SLAB_STATE),
                              s5i0[l].reshape(n, SLABS, SLAB_STATE)], axis=-1).transpose(1, 0, 2)
        yg, s1 = _s5_step(u.reshape(SLABS, n, LANES), s5m['bst'], s5m['c0'], s5m['a1'],
                          s0, s5m['d1'][l], l)
        yg = yg.reshape(SLABS, 1, n, LANES)
        s1 = s1.transpose(1, 0, 2)
        sr = s1[:, :, :SLAB_STATE].reshape(n, S5_GROUPS, S5_STATE)
        si = s1[:, :, SLAB_STATE:].reshape(n, S5_GROUPS, S5_STATE)
        og, sg = _gdn_step(qkv.reshape(n, QKV_WIDTH), z.reshape(n, GDN_WIDTH), ab.reshape(n, LANES),
                           p['gdn_conv_w'][l], alog, dtb, nw,
                           sc0[l].reshape(n, (GDN_CONV - 1) * QKV_WIDTH), sg0, l, prev_sg)
        og = og.reshape(1, n, GDN_WIDTH)
        cb = jnp.concatenate([sc0[l][:, 1:, :], qkv.reshape(n, 1, QKV_WIDTH)], axis=1)
    x, h, lgT = _merge(yg, og, ga, gb, x, mod, p['w_s5_glu'], p['w_gdn_out'], p['w_out'],
                       p['g_ffn'], p['w_router'], l=l, tm=min(512, L), hi=hi, chunked=seq,
                       h_dtype=BF16 if (seq and l % 2 == 0) else F32)
    return x, h, lgT, (sr, si, sg, cb)


def kernel(x_prompt, x_sample, c_prompt, c_sample, state_s5_re, state_s5_im, state_gdn, state_conv,
           g_mix, g_ffn, g_final, w_ada, b_ada, w_in, s5_lambda_re, s5_lambda_im, s5_log_dt,
           s5_b_re, s5_b_im, s5_c_re, s5_c_im, s5_d, w_s5_glu, gdn_conv_w, gdn_a_log, gdn_dt_bias,
           gdn_norm_w, w_gdn_out, w_out, w_ffn_gate_up, w_ffn_down, w_router, b_router,
           w_exp_gate_up, w_exp_down):
    def in_proj_parts(w):
        return w, w[:, :, 2568:], jnp.pad(w[:, :, 2560:2568], ((0, 0), (0, 0), (0, LANES - 8)))

    D_ = x_prompt.shape[-1]
    p = dict(g_mix=g_mix.reshape(DEPTH, 1, D_), g_ffn=g_ffn.reshape(DEPTH, 1, D_), w_s5_glu=w_s5_glu,
             gdn_conv_w=gdn_conv_w, gdn_a_log=gdn_a_log, gdn_dt_bias=gdn_dt_bias,
             gdn_norm_w=gdn_norm_w, w_gdn_out=w_gdn_out, w_out=w_out,
             w_router=w_router.transpose(0, 2, 1), w_in=in_proj_parts(w_in),
             w_in_seq=in_proj_parts(w_in.astype(BF16)))
    nbp, L, D = x_prompt.shape
    nbs = x_sample.shape[0]

    mod = _ada(jnp.concatenate([c_prompt, c_sample], axis=0), w_ada, b_ada)
    mod_p = mod[:, :nbp].reshape(DEPTH, nbp, 1, 6 * D)
    mod_s = mod[:, nbp:].reshape(DEPTH, 1, nbs, 6 * D)

    seg = L // S5_T // 8
    be, bst, c0, cpm, tp, pt, a1 = _s5_prep(s5_lambda_re, s5_lambda_im, s5_log_dt, s5_b_re, s5_b_im,
                                            s5_c_re, s5_c_im, seg)
    d1 = [s5_d[l].reshape(SLABS, 1, LANES) for l in range(DEPTH)]
    s5m = dict(be=be, bst=bst, c0=c0, cpm=cpm, pt=pt, a1=a1, tp=tp, d1=d1,
               dsk=[jnp.tile(d, (1, 1, S5_T)) for d in d1])

    xs_ = [x_prompt, x_sample.reshape(1, nbs, D)]
    mods = [mod_p, mod_s]
    states = [(None, None, None, None), (state_s5_re, state_s5_im, state_gdn, state_conv)]
    outs = [[], []]
    gfin = g_final.reshape(1, D)
    for l in range(DEPTH):
        final = l == DEPTH - 1
        mixed = []
        for gi, seq in enumerate((True, False)):
            prev_sg = outs[gi][0][2] if (not seq and final and DEPTH == 2) else None
            x, h, lgT, st = _mixer_layer(xs_[gi], mods[gi], states[gi], p, s5m, l, prev_sg, seq=seq)
            outs[gi].append(st)
            mixed.append((h, x, mods[gi], lgT))
        if l % 2 == 0:
            wgu, wdn = w_ffn_gate_up[l // 2], w_ffn_down[l // 2]
            xs_ = [_ffn(h, x, mod_g, wgu if gi else wgu.astype(BF16), wdn if gi else wdn.astype(BF16), gfin,
                        l=l, tm=min(1024, x.shape[1]), hi=(gi == 1), final=final)
                   for gi, (h, x, mod_g, _) in enumerate(mixed)]
        else:
            xs_ = _moe_routed(mixed, b_router[l // 2], w_exp_gate_up[l // 2], w_exp_down[l // 2], gfin,
                              l=l, final=final)
    y_p, y_s = xs_
    st_p = [jnp.stack([o[i] for o in outs[0]]) for i in range(4)]
    st_s = [outs[1][-1][2] if (i == 2 and DEPTH == 2) else jnp.stack([o[i] for o in outs[1]])
            for i in range(4)]
    return (y_p, y_s.reshape(nbs, 1, D), st_p[0], st_p[1], st_p[2], st_p[3],
            st_s[0], st_s[1], st_s[2], st_s[3])
```

```python
import functools

import jax
import jax.numpy as jnp
from jax import lax
from jax.experimental import pallas as pl
from jax.experimental.pallas import tpu as pltpu

F32 = jnp.float32
BF16 = jnp.bfloat16
HI = lax.Precision.HIGHEST

D_MODEL = 1024
DEPTH = 2
S5_WIDTH = 512
S5_GROUP = 16
S5_GROUPS = 32
S5_STATE = 64
GDN_HEADS = 4
GDN_DK = 128
GDN_WIDTH = 512
GDN_CONV = 4
QKV_WIDTH = 1536
D_FF = 3584
N_EXPERTS = 8
NORM_EPS = 1e-6
L2_EPS = 1e-6

LANES = 128
SLABS = S5_WIDTH // LANES
SLAB_STATE = (S5_GROUPS // SLABS) * S5_STATE
S5_T = 8
S5_SEG_PAD = 4
GDN_C = 128
GDN_BLOCK = 512
GDN_SUB = 256
VMEM_LIMIT = 56 * 1024 * 1024


def _cp(sem, vmem=VMEM_LIMIT):
    return pltpu.CompilerParams(dimension_semantics=sem, vmem_limit_bytes=vmem)


def _dot(a, b, prec=None):
    return jnp.dot(a, b, precision=prec, preferred_element_type=F32)


def _dotb(a, b):
    return jnp.dot(a.astype(BF16), b.astype(BF16), preferred_element_type=F32)


def _dot_nt(a, b, prec=None):
    return lax.dot_general(a, b, (((1,), (1,)), ((), ())), precision=prec,
                           preferred_element_type=F32)


def _dot_tn(a, b, prec=None):
    return lax.dot_general(a, b, (((0,), (0,)), ((), ())), precision=prec,
                           preferred_element_type=F32)


def _silu(x):
    return x * jax.nn.sigmoid(x)


def _ada_kernel(c_ref, w_ref, b_ref, o_ref):
    cs = _silu(c_ref[...])
    o_ref[0] = _dot(cs, w_ref[0], HI) + b_ref[0]


def _ada(c_all, w_ada, b_ada):
    n = c_all.shape[0]
    tn = 1536
    return pl.pallas_call(
        _ada_kernel,
        out_shape=jax.ShapeDtypeStruct((DEPTH, n, 6 * D_MODEL), F32),
        grid=(DEPTH, 6 * D_MODEL // tn),
        in_specs=[pl.BlockSpec((n, D_MODEL), lambda l, j: (0, 0)),
                  pl.BlockSpec((1, D_MODEL, tn), lambda l, j: (l, 0, j)),
                  pl.BlockSpec((1, 1, tn), lambda l, j: (l, 0, j))],
        out_specs=pl.BlockSpec((1, n, tn), lambda l, j: (l, 0, j)),
        compiler_params=_cp(("parallel", "parallel")),
        name="ada_mod",
    )(c_all, w_ada, b_ada.reshape(DEPTH, 1, 6 * D_MODEL))


def _proj_kernel(x_ref, g_ref, sc_ref, sh_ref, w_ref, wg_ref, wab_ref,
                 u_ref, qkv_ref, z_ref, ga_ref, gb_ref, ab_ref, h_scr):
    j = pl.program_id(2)

    @pl.when(j == 0)
    def _():
        x = x_ref[0]
        ms = jnp.mean(x * x, axis=-1, keepdims=True)
        xn = x * lax.rsqrt(ms + NORM_EPS) * g_ref[0]
        h_scr[...] = (xn * (1.0 + sc_ref[0, 0]) + sh_ref[0, 0]).astype(h_scr.dtype)

    def mm(w):
        return _dot(h_scr[...], w, HI)

    @pl.when(j == 0)
    def _():
        res = mm(w_ref[0])
        for k in range(SLABS):
            u_ref[k, 0] = res[:, k * LANES:(k + 1) * LANES]

    @pl.when((j >= 1) & (j <= 3))
    def _():
        qkv_ref[0] = mm(w_ref[0])

    @pl.when(j == 4)
    def _():
        z_ref[0] = mm(w_ref[0])

    @pl.when((j == 5) | (j == 6))
    def _():
        ga_ref[0] = jax.nn.sigmoid(mm(wg_ref[0]))

    @pl.when((j == 7) | (j == 8))
    def _():
        gb_ref[0] = jax.nn.sigmoid(mm(wg_ref[0]))

    @pl.when(j == 9)
    def _():
        ab_ref[0] = mm(wab_ref[0])


def _mod_spec(mod, l, chunk, tm):
    per_row = mod.shape[2] != 1
    D = mod.shape[3] // 6

    def index(b, i, *_):
        return (l, b, i if per_row else 0, chunk)

    return pl.BlockSpec((1, 1, tm if per_row else 1, D), index)


def _proj(x, g, mod, w_in, w_gates, w_ab, *, l, tm):
    B, L, D = x.shape
    tn = 512
    clampi = lambda j, lo, n: jnp.clip(j - lo, 0, n - 1)
    outs = pl.pallas_call(
        _proj_kernel,
        out_shape=(jax.ShapeDtypeStruct((SLABS, B, L, LANES), F32),
                   jax.ShapeDtypeStruct((B, L, QKV_WIDTH), F32),
                   jax.ShapeDtypeStruct((B, L, GDN_WIDTH), F32),
                   jax.ShapeDtypeStruct((B, L, D), F32),
                   jax.ShapeDtypeStruct((B, L, D), F32),
                   jax.ShapeDtypeStruct((B, L, LANES), F32)),
        grid=(B, L // tm, 10),
        in_specs=[pl.BlockSpec((1, tm, D), lambda b, i, j: (b, i, 0)),
                  pl.BlockSpec((1, 1, D), lambda b, i, j: (l, 0, 0)),
                  _mod_spec(mod, l, 1, tm),
                  _mod_spec(mod, l, 0, tm),
                  pl.BlockSpec((1, D, tn), lambda b, i, j: (l, 0, jnp.minimum(j, 4))),
                  pl.BlockSpec((1, D, tn), lambda b, i, j: (l, 0, clampi(j, 5, 4))),
                  pl.BlockSpec((1, D, LANES), lambda b, i, j: (l, 0, 0))],
        out_specs=(pl.BlockSpec((SLABS, 1, tm, LANES), lambda b, i, j: (0, b, i, 0)),
                   pl.BlockSpec((1, tm, tn), lambda b, i, j: (b, i, clampi(j, 1, 3))),
                   pl.BlockSpec((1, tm, tn), lambda b, i, j: (b, i, 0)),
                   pl.BlockSpec((1, tm, tn), lambda b, i, j: (b, i, clampi(j, 5, 2))),
                   pl.BlockSpec((1, tm, tn), lambda b, i, j: (b, i, clampi(j, 7, 2))),
                   pl.BlockSpec((1, tm, LANES), lambda b, i, j: (b, i, 0))),
        scratch_shapes=[pltpu.VMEM((tm, D), F32)],
        compiler_params=_cp(("parallel", "parallel", "arbitrary")),
        name="norm_in_proj",
    )(x, g, mod, mod, w_in, w_gates, w_ab)
    return outs


def _proj_seq_kernel(x_ref, g_ref, sc_ref, sh_ref, w_ref, wg_ref, wab_ref,
                     u_ref, qkv_ref, z_ref, ga_ref, gb_ref, ab_ref, us_scr):
    x = x_ref[0]
    ms = jnp.mean(x * x, axis=-1, keepdims=True)
    xn = x * lax.rsqrt(ms + NORM_EPS) * g_ref[0]
    h = (xn * (1.0 + sc_ref[0, 0]) + sh_ref[0, 0]).astype(BF16)
    res = _dot(h, w_ref[0, :, 0:S5_WIDTH])
    nrow = res.shape[0] // S5_T
    for k in range(SLABS):
        us_scr[...] = res[:, k * LANES:(k + 1) * LANES]
        for t in range(S5_T):
            u_ref[k, 0, :, t * LANES:(t + 1) * LANES] = (
                us_scr[pl.ds(t, nrow, stride=S5_T), :].astype(u_ref.dtype))
    c0 = S5_WIDTH
    qkv_ref[0] = _dot(h, w_ref[0, :, c0:c0 + QKV_WIDTH]).astype(qkv_ref.dtype)
    c0 += QKV_WIDTH
    z_ref[0] = _dot(h, w_ref[0, :, c0:c0 + GDN_WIDTH]).astype(z_ref.dtype)
    D = x.shape[-1]
    ga_ref[0] = jax.nn.sigmoid(_dot(h, wg_ref[0, :, 0:D])).astype(ga_ref.dtype)
    gb_ref[0] = jax.nn.sigmoid(_dot(h, wg_ref[0, :, D:2 * D])).astype(gb_ref.dtype)
    ab_ref[0] = _dot(h, wab_ref[0])


def _proj_seq(x, g, mod, w_in, w_gates, w_ab, *, l, tm):
    B, L, D = x.shape
    n_main = S5_WIDTH + QKV_WIDTH + GDN_WIDTH
    row = lambda w: pl.BlockSpec((1, tm, w), lambda b, i: (b, i, 0))
    return pl.pallas_call(
        _proj_seq_kernel,
        out_shape=(jax.ShapeDtypeStruct((SLABS, B, L // S5_T, S5_T * LANES), BF16),
                   jax.ShapeDtypeStruct((B, L, QKV_WIDTH), BF16),
                   jax.ShapeDtypeStruct((B, L, GDN_WIDTH), BF16),
                   jax.ShapeDtypeStruct((B, L, D), BF16),
                   jax.ShapeDtypeStruct((B, L, D), BF16),
                   jax.ShapeDtypeStruct((B, L, LANES), F32)),
        grid=(B, L // tm),
        in_specs=[row(D),
                  pl.BlockSpec((1, 1, D), lambda b, i: (l, 0, 0)),
                  _mod_spec(mod, l, 1, tm),
                  _mod_spec(mod, l, 0, tm),
                  pl.BlockSpec((1, D, n_main), lambda b, i: (l, 0, 0)),
                  pl.BlockSpec((1, D, 2 * D), lambda b, i: (l, 0, 0)),
                  pl.BlockSpec((1, D, LANES), lambda b, i: (l, 0, 0))],
        out_specs=(pl.BlockSpec((SLABS, 1, tm // S5_T, S5_T * LANES), lambda b, i: (0, b, i, 0)),
                   row(QKV_WIDTH), row(GDN_WIDTH), row(D), row(D), row(LANES)),
        scratch_shapes=[pltpu.VMEM((tm, LANES), F32)],
        compiler_params=_cp(("parallel", "parallel")),
        name="norm_in_proj_seq",
    )(x, g, mod, mod, w_in, w_gates, w_ab)


GROUPS_PER_SLAB = S5_GROUPS // SLABS


def _s5_prep_kernel(lrb, lib, dtb, bre, bim, lrc, lic, dtc, cre, cim, lrn, lin, dtn,
                    be_ref, bst_ref, c0_ref, cpm_ref, tp_ref, pt_ref, a1_ref, cpe_scr, *, seg):
    W = SLAB_STATE

    def disc(lr, li, ldt):
        dt = jnp.exp(ldt)
        mag = jnp.exp(lr * dt)
        return mag * jnp.cos(li * dt), mag * jnp.sin(li * dt)

    def cmul(xr, xi, yr, yi):
        return xr * yr - xi * yi, xr * yi + xi * yr

    lr, li = lrb[0], lib[0]
    ar, ai = disc(lr, li, dtb[0])
    den = lr * lr + li * li
    nr = ar - 1.0
    kr = (nr * lr + ai * li) / den
    ki = (ai * lr - nr * li) / den
    br, bi = bre[0], bim[0]
    bbr = kr * br - ki * bi
    bbi = kr * bi + ki * br
    rgrp = lax.broadcasted_iota(jnp.int32, (LANES, LANES), 0) // S5_GROUP
    lane_hi = lax.broadcasted_iota(jnp.int32, (LANES, LANES), 1) // S5_STATE
    pr, pi = jnp.ones_like(ar), jnp.zeros_like(ar)
    for d in range(S5_T):
        t = S5_T - 1 - d
        for ri, val in enumerate(cmul(pr, pi, bbr, bbi)):
            two = jnp.concatenate([val, val], axis=1)
            for m in range(GROUPS_PER_SLAB // 2):
                tile = jnp.where(rgrp == 2 * m + lane_hi, two, 0.0)
                c0 = ri * W + m * LANES
                be_ref[0, 0, t * LANES:(t + 1) * LANES, c0:c0 + LANES] = tile.astype(BF16)
                if d == 0:
                    bst_ref[0, 0, :, c0:c0 + LANES] = tile
        pr, pi = cmul(pr, pi, ar, ai)

    ar, ai = disc(lrc[0], lic[0], dtc[0])
    cr, ci = cre[0], cim[0]
    own = (lax.broadcasted_iota(jnp.int32, (W, LANES), 0) // S5_STATE
           == lax.broadcasted_iota(jnp.int32, (W, LANES), 1) // S5_GROUP)
    pr, pi = jnp.ones_like(ar), jnp.zeros_like(ar)
    for d in range(S5_T + 1):
        vr, vi = cmul(cr, ci, pr, pi)
        for ri, val in enumerate((vr, -vi)):
            tile = jnp.where(own, val, 0.0)
            cpe_scr[d, ri * W:(ri + 1) * W, :] = tile
            if d == 0:
                c0_ref[0, 0, ri * W:(ri + 1) * W, :] = tile
            if d >= 1:
                cpm_ref[0, 0, ri * W:(ri + 1) * W, (d - 1) * LANES:d * LANES] = tile.astype(BF16)
        pr, pi = cmul(pr, pi, ar, ai)

    bst = bst_ref[0, 0]
    lag = [_dot(bst, cpe_scr[d], HI).astype(BF16) for d in range(S5_T)]
    for dd in range(S5_T // 2):
        tp_ref[0, 0, dd, 0:LANES, 0:LANES] = lag[2 * dd]
        tp_ref[0, 0, dd, LANES:, LANES:] = lag[2 * dd]
        tp_ref[0, 0, dd, 0:LANES, LANES:] = lag[2 * dd + 1]
        tp_ref[0, 0, dd, LANES:, 0:LANES] = lag[2 * dd - 1] if dd else jnp.zeros((LANES, LANES), BF16)

    ar, ai = disc(lrn[0, 0], lin[0, 0], dtn[0, 0])
    a1_ref[0, 0, :, 0:W] = ar
    a1_ref[0, 0, :, W:2 * W] = ai
    tr, ti = ar, ai
    for _ in range(S5_T - 1):
        tr, ti = cmul(tr, ti, ar, ai)
    pr, pi = jnp.ones_like(ar), jnp.zeros_like(ar)
    for i in range(seg + 1):
        pt_ref[0, 0, i:i + 1, 0:W] = pr
        pt_ref[0, 0, i:i + 1, W:2 * W] = pi
        pr, pi = cmul(pr, pi, tr, ti)


def _s5_prep(lam_re, lam_im, log_dt, b_re, b_im, c_re, c_im, seg):
    G, P, C = S5_GROUPS, S5_STATE, S5_GROUP
    W2 = 2 * SLAB_STATE
    dt3 = jnp.broadcast_to(log_dt[:, :, None], (DEPTH, G, P))
    rows_b = lambda a: jnp.repeat(a, C, axis=1)
    bt = lambda a: a.transpose(0, 1, 3, 2).reshape(DEPTH, G * C, P)
    rows_c = lambda a: jnp.broadcast_to(a.reshape(DEPTH, G * P, 1), (DEPTH, G * P, LANES))
    ct = lambda a: jnp.tile(a.transpose(0, 1, 3, 2).reshape(DEPTH, G * P, C), (1, 1, LANES // C))
    nat = lambda a: a.reshape(DEPTH, SLABS, 1, SLAB_STATE)
    args = (rows_b(lam_re), rows_b(lam_im), rows_b(dt3), bt(b_re), bt(b_im),
            rows_c(lam_re), rows_c(lam_im), rows_c(dt3), ct(c_re), ct(c_im),
            nat(lam_re), nat(lam_im), nat(dt3))
    bspec = pl.BlockSpec((1, LANES, P), lambda l, k: (l, k, 0))
    cspec = pl.BlockSpec((1, SLAB_STATE, LANES), lambda l, k: (l, k, 0))
    nspec = pl.BlockSpec((1, 1, 1, SLAB_STATE), lambda l, k: (l, k, 0, 0))
    return pl.pallas_call(
        functools.partial(_s5_prep_kernel, seg=seg),
        out_shape=(jax.ShapeDtypeStruct((DEPTH, SLABS, S5_T * LANES, W2), BF16),
                   jax.ShapeDtypeStruct((DEPTH, SLABS, LANES, W2), F32),
                   jax.ShapeDtypeStruct((DEPTH, SLABS, W2, LANES), F32),
                   jax.ShapeDtypeStruct((DEPTH, SLABS, W2, S5_T * LANES), BF16),
                   jax.ShapeDtypeStruct((DEPTH, SLABS, S5_T // 2, 2 * LANES, 2 * LANES), BF16),
                   jax.ShapeDtypeStruct((DEPTH, SLABS, seg + 1, W2), F32),
                   jax.ShapeDtypeStruct((DEPTH, SLABS, 1, W2), F32)),
        grid=(DEPTH, SLABS),
        in_specs=[bspec] * 5 + [cspec] * 5 + [nspec] * 3,
        out_specs=(pl.BlockSpec((1, 1, S5_T * LANES, W2), lambda l, k: (l, k, 0, 0)),
                   pl.BlockSpec((1, 1, LANES, W2), lambda l, k: (l, k, 0, 0)),
                   pl.BlockSpec((1, 1, W2, LANES), lambda l, k: (l, k, 0, 0)),
                   pl.BlockSpec((1, 1, W2, S5_T * LANES), lambda l, k: (l, k, 0, 0)),
                   pl.BlockSpec((1, 1, S5_T // 2, 2 * LANES, 2 * LANES), lambda l, k: (l, k, 0, 0, 0)),
                   pl.BlockSpec((1, 1, seg + 1, W2), lambda l, k: (l, k, 0, 0)),
                   pl.BlockSpec((1, 1, 1, W2), lambda l, k: (l, k, 0, 0))),
        scratch_shapes=[pltpu.VMEM((S5_T + 1, W2, LANES), F32)],
        compiler_params=_cp(("parallel", "parallel")),
        name="s5_discretize",
    )(*args)


def _s5_seq_kernel(up_ref, be_ref, tp_ref, cpm_ref, pt_ref, s0_ref, dsk_ref,
                   yg_ref, sfin_ref, e_scr, sx_scr, *, nc):
    seg = nc // 8
    W = SLAB_STATE
    nt = W // LANES
    ub = up_ref[0, 0]
    u = ub.astype(F32)
    e = _dot(ub, be_ref[0, 0])
    pitch = e_scr.shape[1] // 8
    for c in range(2 * nt):
        for j in range(8):
            e_scr[c, j * pitch:j * pitch + seg, :] = e[j * seg:(j + 1) * seg, c * LANES:(c + 1) * LANES]

    def tiles(row):
        return [(row[:, c * LANES:(c + 1) * LANES], row[:, W + c * LANES:W + (c + 1) * LANES])
                for c in range(nt)]

    a8 = [(jnp.broadcast_to(r, (8, LANES)), jnp.broadcast_to(i, (8, LANES)))
          for r, i in tiles(pt_ref[0, 0, 1:2, :])]

    def step(i, carry):
        rows = pl.ds(i, 8, stride=pitch)
        new = []
        for c in range(nt):
            sr, si = carry[c]
            ar, ai = a8[c]
            sx_scr[c, rows, :] = sr
            sx_scr[nt + c, rows, :] = si
            new.append((ar * sr - ai * si + e_scr[c, rows, :],
                        ar * si + ai * sr + e_scr[nt + c, rows, :]))
        return tuple(new)

    zero = jnp.zeros((8, LANES), F32)
    ends = lax.fori_loop(0, seg, step, tuple((zero, zero) for _ in range(nt)))

    al = tiles(pt_ref[0, 0, seg:seg + 1, :])
    cur = tiles(s0_ref[0, 0])
    car = []
    for c in range(nt):
        alr, ali = al[c]
        cr, ci = cur[c]
        sr, si = ends[c]
        crs, cis = [], []
        for j in range(8):
            crs.append(cr)
            cis.append(ci)
            cr, ci = (alr * cr - ali * ci + sr[j:j + 1], alr * ci + ali * cr + si[j:j + 1])
        sfin_ref[0, 0, :, c * LANES:(c + 1) * LANES] = cr
        sfin_ref[0, 0, :, W + c * LANES:W + (c + 1) * LANES] = ci
        car.append((jnp.concatenate(crs, axis=0), jnp.concatenate(cis, axis=0)))

    def corr(i, _):
        rows = pl.ds(i, 8, stride=pitch)
        pw = tiles(pt_ref[0, 0, pl.ds(i, 1), :])
        for c in range(nt):
            pr, pi = pw[c]
            cr, ci = car[c]
            sx_scr[c, rows, :] = sx_scr[c, rows, :] + (pr * cr - pi * ci)
            sx_scr[nt + c, rows, :] = sx_scr[nt + c, rows, :] + (pr * ci + pi * cr)
        return 0

    lax.fori_loop(0, seg, corr, 0)

    sx = jnp.concatenate(
        [jnp.concatenate([sx_scr[c, j * pitch:j * pitch + seg, :] for j in range(8)], axis=0)
         for c in range(2 * nt)], axis=-1)
    y = _dot(sx.astype(BF16), cpm_ref[0, 0])
    TW = 2 * LANES
    for tq in range(S5_T // 2):
        acc = y[:, tq * TW:(tq + 1) * TW]
        for tpi in range(tq + 1):
            acc = acc + _dot(ub[:, tpi * TW:(tpi + 1) * TW], tp_ref[0, 0, tq - tpi])
        acc = acc + dsk_ref[0, :, tq * TW:(tq + 1) * TW] * u[:, tq * TW:(tq + 1) * TW]
        yg_ref[0, 0, :, tq * TW:(tq + 1) * TW] = jax.nn.gelu(acc).astype(yg_ref.dtype)


def _s5_seq(up, be_emb, tp, cpm, pt, s0, dsk, l):
    _, B, nc, _ = up.shape
    seg = nc // 8
    W2 = 2 * SLAB_STATE
    yg, sfin = pl.pallas_call(
        functools.partial(_s5_seq_kernel, nc=nc),
        out_shape=(jax.ShapeDtypeStruct((SLABS, B, nc, S5_T * LANES), BF16),
                   jax.ShapeDtypeStruct((SLABS, B, 1, W2), F32)),
        grid=(SLABS, B),
        in_specs=[pl.BlockSpec((1, 1, nc, S5_T * LANES), lambda k, b: (k, b, 0, 0)),
                  pl.BlockSpec((1, 1, S5_T * LANES, W2), lambda k, b: (l, k, 0, 0)),
                  pl.BlockSpec((1, 1, S5_T // 2, 2 * LANES, 2 * LANES), lambda k, b: (l, k, 0, 0, 0)),
                  pl.BlockSpec((1, 1, W2, S5_T * LANES), lambda k, b: (l, k, 0, 0)),
                  pl.BlockSpec((1, 1, seg + 1, W2), lambda k, b: (l, k, 0, 0)),
                  pl.BlockSpec((1, 1, 1, W2), lambda k, b: (k, b, 0, 0)),
                  pl.BlockSpec((1, 1, S5_T * LANES), lambda k, b: (k, 0, 0))],
        out_specs=(pl.BlockSpec((1, 1, nc, S5_T * LANES), lambda k, b: (k, b, 0, 0)),
                   pl.BlockSpec((1, 1, 1, W2), lambda k, b: (k, b, 0, 0))),
        scratch_shapes=[pltpu.VMEM((W2 // LANES, 8 * (seg + S5_SEG_PAD), LANES), F32),
                        pltpu.VMEM((W2 // LANES, 8 * (seg + S5_SEG_PAD), LANES), F32)],
        compiler_params=_cp(("parallel", "parallel")),
        name="s5_seq",
    )(up, be_emb, tp, cpm, pt, s0, dsk)
    return yg, sfin


def _s5_step_kernel(u_ref, b_ref, c_ref, a_ref, s0_ref, d_ref, yg_ref, s1_ref):
    W = SLAB_STATE
    u = u_ref[0]
    bu = _dot(u, b_ref[0, 0], HI)
    ar = a_ref[0, 0, :, 0:W]
    ai = a_ref[0, 0, :, W:2 * W]
    sr = s0_ref[0, :, 0:W]
    si = s0_ref[0, :, W:2 * W]
    nr = ar * sr - ai * si + bu[:, 0:W]
    ni = ar * si + ai * sr + bu[:, W:2 * W]
    s1_ref[0, :, 0:W] = nr
    s1_ref[0, :, W:2 * W] = ni
    s1 = jnp.concatenate([nr, ni], axis=-1)
    y = _dot(s1, c_ref[0, 0], HI) + d_ref[0] * u
    yg_ref[0] = jax.nn.gelu(y)


def _s5_step(u_slab, bst, c0, a1, s0, d1, l):
    _, N, _ = u_slab.shape
    W2 = 2 * SLAB_STATE
    return pl.pallas_call(
        _s5_step_kernel,
        out_shape=(jax.ShapeDtypeStruct((SLABS, N, LANES), F32),
                   jax.ShapeDtypeStruct((SLABS, N, W2), F32)),
        grid=(SLABS,),
        in_specs=[pl.BlockSpec((1, N, LANES), lambda k: (k, 0, 0)),
                  pl.BlockSpec((1, 1, LANES, W2), lambda k: (l, k, 0, 0)),
                  pl.BlockSpec((1, 1, W2, LANES), lambda k: (l, k, 0, 0)),
                  pl.BlockSpec((1, 1, 1, W2), lambda k: (l, k, 0, 0)),
                  pl.BlockSpec((1, N, W2), lambda k: (k, 0, 0)),
                  pl.BlockSpec((1, 1, LANES), lambda k: (k, 0, 0))],
        out_specs=(pl.BlockSpec((1, N, LANES), lambda k: (k, 0, 0)),
                   pl.BlockSpec((1, N, W2), lambda k: (k, 0, 0))),
        compiler_params=_cp(("parallel",)),
        name="s5_step",
    )(u_slab, bst, c0, a1, s0, d1)


def _l2n(x):
    return x * lax.rsqrt(jnp.sum(x * x, axis=-1, keepdims=True) + L2_EPS)


def _split_bf16(x):
    hi = x.astype(BF16)
    return hi, (x - hi.astype(F32)).astype(BF16)


def _unit_lower_solve(As, rhss):
    n = GDN_C
    row = lax.broadcasted_iota(jnp.int32, (n, n), 0)
    col = lax.broadcasted_iota(jnp.int32, (n, n), 1)
    eye = (row == col).astype(F32)
    same8 = (row // 8) == (col // 8)
    Qs = [jnp.where(same8, -A, 0.0) for A in As]
    invs = [eye + Q for Q in Qs]
    for _ in range(2):
        Qs = [_dotb(Q, Q) for Q in Qs]
        invs = [inv + _dotb(inv, Q) for inv, Q in zip(invs, Qs)]
    s = 8
    while s < n:
        sib = ((row // (2 * s)) == (col // (2 * s))) & ((row // s) != (col // s))
        offs = [jnp.where(sib, A, 0.0).astype(BF16) for A in As]
        invb = [inv.astype(BF16) for inv in invs]
        tmp = [_dot(off, ib) for off, ib in zip(offs, invb)]
        invs = [inv - _dot(ib, t.astype(BF16)) for inv, ib, t in zip(invs, invb, tmp)]
        s *= 2
    invb = [inv.astype(BF16) for inv in invs]
    x0s = [_dot(ib, rhs.astype(BF16)) for ib, rhs in zip(invb, rhss)]
    res = []
    for A, x0, rhs in zip(As, x0s, rhss):
        ah, al = _split_bf16(A)
        xh, xl = _split_bf16(x0)
        res.append(rhs - x0 - (_dot(ah, xh) + _dot(ah, xl) + _dot(al, xh)))
    return [x0 + _dot(ib, r.astype(BF16)) for x0, ib, r in zip(x0s, invb, res)]


def _gdn_tile(qc_scr, gc, beta, z_ref, nw, o_ref, s_scr, tl, r0):
    C, DK, H = GDN_C, GDN_DK, GDN_HEADS
    nchunk = tl // C
    probs = [(c, h) for c in range(nchunk) for h in range(H)]
    row = lax.broadcasted_iota(jnp.int32, (C, C), 0)
    col = lax.broadcasted_iota(jnp.int32, (C, C), 1)
    tri = row >= col
    strict = row > col

    def blk(c, off):
        return qc_scr[c * C:(c + 1) * C, off:off + DK]

    q = [_l2n(blk(c, h * DK)) * (DK ** -0.5) for c, h in probs]
    k = [_l2n(blk(c, GDN_WIDTH + h * DK)) for c, h in probs]
    v = [blk(c, 2 * GDN_WIDTH + h * DK) for c, h in probs]
    gcb = [jnp.broadcast_to(gc[c * C:(c + 1) * C, h:h + 1], (C, DK)) for c, h in probs]
    bb = [jnp.broadcast_to(beta[c * C:(c + 1) * C, H + h:H + h + 1], (C, DK)) for c, h in probs]
    decay = []
    for g in gcb:
        diff = g - g.T
        decay.append(jnp.where(tri, jnp.exp(jnp.where(tri, diff, 0.0)), 0.0))
    kbf = [x.astype(BF16) for x in k]
    kb = [x * b for x, b in zip(k, bb)]
    A = [jnp.where(strict, _dot_nt(x.astype(BF16), y) * d, 0.0) for x, y, d in zip(kb, kbf, decay)]
    egc = [jnp.exp(g) for g in gcb]
    rhs = [jnp.concatenate([x * b, y * e], axis=-1) for x, b, y, e in zip(v, bb, kb, egc)]
    sol = _unit_lower_solve(A, rhs)
    attn = [jnp.where(tri, _dot_nt(x.astype(BF16), y) * d, 0.0).astype(BF16)
            for x, y, d in zip(q, kbf, decay)]
    glast = [g[C - 1:C, :] for g in gcb]
    wq = [jnp.concatenate([s[:, DK:], x * e], axis=0).astype(BF16) for s, x, e in zip(sol, q, egc)]
    kg = [(x * jnp.exp(gl - g)).astype(BF16) for x, gl, g in zip(k, glast, gcb)]

    for c in range(nchunk):
        ps = [c * H + h for h in range(H)]
        S = [s_scr[h] for h in range(H)]
        ws = [_dot(wq[p], S[h].astype(BF16)) for h, p in enumerate(ps)]
        v_new = [sol[p][:, 0:DK] - w[0:C] for p, w in zip(ps, ws)]
        vb = [x.astype(BF16) for x in v_new]
        o = [w[C:] + _dot(attn[p], x) for p, w, x in zip(ps, ws, vb)]
        for h, p in enumerate(ps):
            s_scr[h] = S[h] * jnp.exp(glast[p]) + _dot_tn(kg[p], vb[h])
            zh = z_ref[0, r0 + c * C:r0 + (c + 1) * C, h * DK:(h + 1) * DK].astype(F32)
            on = o[h] * lax.rsqrt(jnp.mean(o[h] * o[h], axis=-1, keepdims=True) + NORM_EPS) * nw
            o_ref[0, r0 + c * C:r0 + (c + 1) * C, h * DK:(h + 1) * DK] = (on * _silu(zh)).astype(o_ref.dtype)


def _gdn_seq_kernel(qkv_ref, z_ref, ab_ref, cw_ref, alog_ref, dtb_ref, nw_ref, conv0_ref, s0_ref,
                    o_ref, sfin_ref, xp_scr, qc_scr, s_scr, *, tl):
    lt = pl.program_id(1)

    @pl.when(lt == 0)
    def _():
        xp_scr[0:8, :] = jnp.zeros((8, QKV_WIDTH), F32)
        xp_scr[8 - (GDN_CONV - 1):8, :] = conv0_ref[0]
        s_scr[...] = s0_ref[0]

    sub = qc_scr.shape[0]
    for r0 in range(0, tl, sub):
        xp_scr[8:8 + sub, :] = qkv_ref[0, r0:r0 + sub, :].astype(F32)
        conv = cw_ref[0, 0:1, :] * xp_scr[5:5 + sub, :]
        for j in range(1, GDN_CONV):
            conv = conv + cw_ref[0, j:j + 1, :] * xp_scr[5 + j:5 + j + sub, :]
        xp_scr[0:8, :] = xp_scr[sub:sub + 8, :]
        qc_scr[...] = _silu(conv)

        ab = ab_ref[0, r0:r0 + sub, :]
        g = -jnp.exp(alog_ref[...]) * jax.nn.softplus(ab + dtb_ref[...])
        beta = jax.nn.sigmoid(ab)
        row = lax.broadcasted_iota(jnp.int32, (sub, sub), 0)
        col = lax.broadcasted_iota(jnp.int32, (sub, sub), 1)
        csum = ((row >= col) & ((row // GDN_C) == (col // GDN_C))).astype(F32)
        gc = _dot(csum, g, HI)
        _gdn_tile(qc_scr, gc, beta, z_ref, nw_ref[...], o_ref, s_scr, sub, r0)

    @pl.when(lt == pl.num_programs(1) - 1)
    def _():
        sfin_ref[0] = s_scr[...]


def _gdn_seq(qkv, z, ab, conv_w, alog, dtb, nw, conv0, s0, l):
    B, L, _ = qkv.shape
    tl = min(GDN_BLOCK, L)
    sub = min(GDN_SUB, tl)
    return pl.pallas_call(
        functools.partial(_gdn_seq_kernel, tl=tl),
        out_shape=(jax.ShapeDtypeStruct((B, L, GDN_WIDTH), BF16),
                   jax.ShapeDtypeStruct((B, GDN_HEADS, GDN_DK, GDN_DK), F32)),
        grid=(B, L // tl),
        in_specs=[pl.BlockSpec((1, tl, QKV_WIDTH), lambda b, i: (b, i, 0)),
                  pl.BlockSpec((1, tl, GDN_WIDTH), lambda b, i: (b, i, 0)),
                  pl.BlockSpec((1, tl, LANES), lambda b, i: (b, i, 0)),
                  pl.BlockSpec((1, GDN_CONV, QKV_WIDTH), lambda b, i: (l, 0, 0)),
                  pl.BlockSpec((1, LANES), lambda b, i: (0, 0)),
                  pl.BlockSpec((1, LANES), lambda b, i: (0, 0)),
                  pl.BlockSpec((1, GDN_DK), lambda b, i: (0, 0)),
                  pl.BlockSpec((1, GDN_CONV - 1, QKV_WIDTH), lambda b, i: (b, 0, 0)),
                  pl.BlockSpec((1, GDN_HEADS, GDN_DK, GDN_DK), lambda b, i: (b, 0, 0, 0))],
        out_specs=(pl.BlockSpec((1, tl, GDN_WIDTH), lambda b, i: (b, i, 0)),
                   pl.BlockSpec((1, GDN_HEADS, GDN_DK, GDN_DK), lambda b, i: (b, 0, 0, 0))),
        scratch_shapes=[pltpu.VMEM((sub + 8, QKV_WIDTH), F32),
                        pltpu.VMEM((sub, QKV_WIDTH), F32),
                        pltpu.VMEM((GDN_HEADS, GDN_DK, GDN_DK), F32)],
        compiler_params=_cp(("parallel", "arbitrary")),
        name="gdn_seq",
    )(qkv, z, ab, conv_w, alog, dtb, nw, conv0, s0)


GDN_STEP_ROWS = 8


def _gdn_step_kernel(qkv_ref, z_ref, ab_ref, cw_ref, alog_ref, dtb_ref, nw_ref, conv0_ref, s0_ref,
                     *rest):
    if len(rest) == 3:
        prev_ref, o_ref, s1_all = rest
        s1_all[0] = prev_ref[...]
        s1_ref = s1_all.at[1]
    else:
        o_ref, s1_all = rest
        s1_ref = s1_all
    nb = GDN_STEP_ROWS
    W = QKV_WIDTH
    conv = cw_ref[0:1, :] * conv0_ref[:, 0:W]
    conv = conv + cw_ref[1:2, :] * conv0_ref[:, W:2 * W]
    conv = conv + cw_ref[2:3, :] * conv0_ref[:, 2 * W:3 * W]
    conv = conv + cw_ref[3:4, :] * qkv_ref[...]
    qc = _silu(conv)
    ab = ab_ref[...]
    eg = jnp.exp(-jnp.exp(alog_ref[...]) * jax.nn.softplus(ab + dtb_ref[...]))
    beta = jax.nn.sigmoid(ab)
    eye = (lax.broadcasted_iota(jnp.int32, (GDN_DK, GDN_DK), 0)
           == lax.broadcasted_iota(jnp.int32, (GDN_DK, GDN_DK), 1)).astype(F32)
    for h in range(GDN_HEADS):
        q = _l2n(qc[:, h * GDN_DK:(h + 1) * GDN_DK]) * (GDN_DK ** -0.5)
        k = _l2n(qc[:, GDN_WIDTH + h * GDN_DK:GDN_WIDTH + (h + 1) * GDN_DK])
        v = qc[:, 2 * GDN_WIDTH + h * GDN_DK:2 * GDN_WIDTH + (h + 1) * GDN_DK]
        kT = _dot_nt(eye, k, HI)
        qT = _dot_nt(eye, q, HI)
        qk = jnp.sum(q * k, axis=-1, keepdims=True)
        for j in range(nb):
            S = s0_ref[0, j, h]
            kc = jnp.broadcast_to(kT[:, j:j + 1], (GDN_DK, GDN_DK))
            qcb = jnp.broadcast_to(qT[:, j:j + 1], (GDN_DK, GDN_DK))
            kS = jnp.sum(kc * S, axis=0, keepdims=True)
            qS = jnp.sum(qcb * S, axis=0, keepdims=True)
            egj = eg[j:j + 1, h:h + 1]
            bj = beta[j:j + 1, GDN_HEADS + h:GDN_HEADS + h + 1]
            v_new = bj * v[j:j + 1, :] - (bj * egj) * kS
            o = egj * qS + qk[j:j + 1, :] * v_new
            s1_ref[j, h] = S * egj + kc * v_new
            zh = z_ref[j:j + 1, h * GDN_DK:(h + 1) * GDN_DK]
            on = o * lax.rsqrt(jnp.mean(o * o, axis=-1, keepdims=True) + NORM_EPS) * nw_ref[...]
            o_ref[j:j + 1, h * GDN_DK:(h + 1) * GDN_DK] = on * _silu(zh)


def _gdn_step(qkv, z, ab, conv_w, alog, dtb, nw, conv0, s_all, l, prev):
    N = qkv.shape[0]
    nb = GDN_STEP_ROWS
    row = lambda w: pl.BlockSpec((nb, w), lambda i: (i, 0))
    const = lambda r, w: pl.BlockSpec((r, w), lambda i: (0, 0))
    sblk = (nb, GDN_HEADS, GDN_DK, GDN_DK)
    one = pl.BlockSpec(sblk, lambda i: (i, 0, 0, 0))
    ins = [qkv, z, ab, conv_w, alog, dtb, nw, conv0, s_all]
    in_specs = [row(QKV_WIDTH), row(GDN_WIDTH), row(LANES), const(GDN_CONV, QKV_WIDTH),
                const(1, LANES), const(1, LANES), const(1, GDN_DK), row(3 * QKV_WIDTH),
                pl.BlockSpec((1,) + sblk, lambda i: (l, i, 0, 0, 0))]
    if prev is None:
        s_shape, s_spec = jax.ShapeDtypeStruct((N,) + sblk[1:], F32), one
    else:
        assert DEPTH == 2 and l == 1
        ins.append(prev)
        in_specs.append(one)
        s_shape = jax.ShapeDtypeStruct((DEPTH, N) + sblk[1:], F32)
        s_spec = pl.BlockSpec((DEPTH,) + sblk, lambda i: (0, i, 0, 0, 0))
    return pl.pallas_call(
        _gdn_step_kernel,
        out_shape=(jax.ShapeDtypeStruct((N, GDN_WIDTH), F32), s_shape),
        grid=(N // nb,),
        in_specs=in_specs,
        out_specs=(row(GDN_WIDTH), s_spec),
        compiler_params=_cp(("parallel",)),
        name="gdn_step",
    )(*ins)


def _merge_kernel(yg_ref, og_ref, ga_ref, gb_ref, x_ref, gt_ref, wglu_ref, wgo_ref, wout_ref,
                  gf_ref, scf_ref, shf_ref, *rest, hi, chunked, routed):
    if routed:
        wr_ref, xo_ref, h_ref, lg_ref, *scr = rest
    else:
        xo_ref, h_ref, *scr = rest
    if chunked:
        y_scr = scr[-1]
        scr = scr[:-1]
        nrow = y_scr.shape[1] // S5_T
        for k in range(SLABS):
            for t in range(S5_T):
                y_scr[k, pl.ds(t, nrow, stride=S5_T), :] = (
                    yg_ref[k, 0, :, t * LANES:(t + 1) * LANES].astype(F32))
        y = jnp.concatenate([y_scr[k] for k in range(SLABS)], axis=-1)
    else:
        y = jnp.concatenate([yg_ref[k, 0] for k in range(SLABS)], axis=-1)
    if hi:
        wglu, wgo, wout = wglu_ref[0], wgo_ref[0], wout_ref[0]
        mm = lambda a, w: _dot(a, w, HI)
    else:
        wglu_s, wgo_s, wout_s = scr

        @pl.when((pl.program_id(0) == 0) & (pl.program_id(1) == 0))
        def _():
            wglu_s[...] = wglu_ref[0].astype(BF16)
            wgo_s[...] = wgo_ref[0].astype(BF16)
            wout_s[...] = wout_ref[0].astype(BF16)

        wglu, wgo, wout = wglu_s[...], wgo_s[...], wout_s[...]
        mm = lambda a, w: _dot(a.astype(BF16), w)

    glu = mm(y, wglu)
    branch_a = glu[:, 0:D_MODEL] * jax.nn.sigmoid(glu[:, D_MODEL:])
    branch_b = mm(og_ref[0], wgo)
    merged = ga_ref[0].astype(F32) * branch_a + gb_ref[0].astype(F32) * branch_b
    out = mm(merged, wout)
    x = x_ref[0] + gt_ref[0, 0] * out
    xo_ref[0] = x
    ms = jnp.mean(x * x, axis=-1, keepdims=True)
    h = x * lax.rsqrt(ms + NORM_EPS) * gf_ref[0]
    h = h * (1.0 + scf_ref[0, 0]) + shf_ref[0, 0]
    h_ref[0] = h.astype(h_ref.dtype)
    if routed:
        lg_ref[0] = _dot_nt(wr_ref[0], h, HI)


def _merge(yg, og, ga, gb, x, mod, wglu, wgo, wout, gf, wr, *, l, tm, hi, chunked, h_dtype):
    B, L, D = x.shape
    routed = l % 2 == 1
    row = lambda w: pl.BlockSpec((1, tm, w), lambda b, i: (b, i, 0))
    layer = lambda r, w, ll=l: pl.BlockSpec((1, r, w), lambda b, i: (ll, 0, 0))
    scratch = [] if hi else [pltpu.VMEM((S5_WIDTH, 2 * D), BF16), pltpu.VMEM((GDN_WIDTH, D), BF16),
                             pltpu.VMEM((D, D), BF16)]
    if chunked:
        scratch = scratch + [pltpu.VMEM((SLABS, tm, LANES), F32)]
        yg_spec = pl.BlockSpec((SLABS, 1, tm // S5_T, S5_T * LANES), lambda b, i: (0, b, i, 0))
    else:
        yg_spec = pl.BlockSpec((SLABS, 1, tm, LANES), lambda b, i: (0, b, i, 0))
    lg_shape = jax.ShapeDtypeStruct((B, N_EXPERTS, L), F32)
    lg_spec = pl.BlockSpec((1, N_EXPERTS, tm), lambda b, i: (b, 0, i))
    outs = pl.pallas_call(
        functools.partial(_merge_kernel, hi=hi, chunked=chunked, routed=routed),
        out_shape=(jax.ShapeDtypeStruct((B, L, D), F32),
                   jax.ShapeDtypeStruct((B, L, D), h_dtype)) + ((lg_shape,) if routed else ()),
        grid=(B, L // tm),
        in_specs=[yg_spec,
                  row(GDN_WIDTH), row(D), row(D), row(D), _mod_spec(mod, l, 2, tm),
                  layer(S5_WIDTH, 2 * D), layer(GDN_WIDTH, D), layer(D, D),
                  layer(1, D), _mod_spec(mod, l, 4, tm), _mod_spec(mod, l, 3, tm)]
        + ([layer(N_EXPERTS, D, l // 2)] if routed else []),
        out_specs=(row(D), row(D)) + ((lg_spec,) if routed else ()),
        scratch_shapes=scratch,
        compiler_params=_cp(("arbitrary", "arbitrary")),
        name="merge_out_proj",
    )(yg, og, ga, gb, x, mod, wglu, wgo, wout, gf, mod, mod, *((wr,) if routed else ()))
    return outs if routed else (*outs, None)


FF_TILE = 512
FFN_ROWS = 512


def _finish(x, gfin_ref, final):
    if not final:
        return x
    ms = jnp.mean(x * x, axis=-1, keepdims=True)
    return x * lax.rsqrt(ms + NORM_EPS) * gfin_ref[...]


def _ffn_kernel(h_ref, x_ref, gt_ref, wg_ref, wu_ref, wd_ref, gfin_ref, o_ref, acc_scr, *, hi, final):
    j = pl.program_id(2)
    if hi:
        wg, wu, wd = wg_ref[...], wu_ref[...], wd_ref[...]
        mm = lambda a, w: _dot(a, w, HI)
    else:
        wg, wu, wd = wg_ref[...].astype(BF16), wu_ref[...].astype(BF16), wd_ref[...].astype(BF16)
        mm = lambda a, w: _dot(a.astype(BF16), w)

    @pl.when(j == 0)
    def _():
        acc_scr[...] = jnp.zeros_like(acc_scr)

    tm = acc_scr.shape[0]
    sub = min(FFN_ROWS, tm)
    for s in range(tm // sub):
        rows = slice(s * sub, (s + 1) * sub)
        hb = h_ref[0, rows, :]
        act = _silu(mm(hb, wg)) * mm(hb, wu)
        acc_scr[rows, :] = acc_scr[rows, :] + mm(act, wd)

    @pl.when(j == pl.num_programs(2) - 1)
    def _():
        o_ref[0] = _finish(x_ref[0] + gt_ref[0, 0] * acc_scr[...], gfin_ref, final)


def _ffn(h, x, mod, w_gu, w_down, gfin, *, l, tm, hi, final):
    B, L, D = x.shape
    nj = D_FF // FF_TILE
    row = pl.BlockSpec((1, tm, D), lambda b, i, j: (b, i, 0))
    return pl.pallas_call(
        functools.partial(_ffn_kernel, hi=hi, final=final),
        out_shape=jax.ShapeDtypeStruct((B, L, D), F32),
        grid=(B, L // tm, nj),
        in_specs=[row, row, _mod_spec(mod, l, 5, tm),
                  pl.BlockSpec((D, FF_TILE), lambda b, i, j: (0, j)),
                  pl.BlockSpec((D, FF_TILE), lambda b, i, j: (0, nj + j)),
                  pl.BlockSpec((FF_TILE, D), lambda b, i, j: (j, 0)),
                  pl.BlockSpec((1, D), lambda b, i, j: (0, 0))],
        out_specs=row,
        scratch_shapes=[pltpu.VMEM((tm, D), F32)],
        compiler_params=_cp(("parallel", "parallel", "arbitrary")),
        name="ffn_dense",
    )(h, x, mod, w_gu, w_gu, w_down, gfin)


ROUTE_TM = 512
ROW_DMA_TM = 512
MOE_SUP = 1536
MOE_SUB = 512
MOE_FF_TILE = 512


def _route_kernel(lg_ref, br_ref, cnt0_ref, slot_ref, wt_ref, cnt_ref, carry_scr, *, cap):
    @pl.when((pl.program_id(0) == 0) & (pl.program_id(1) == 0))
    def _():
        carry_scr[...] = cnt0_ref[...]

    lg = lg_ref[0] + br_ref[...]
    tm = lg.shape[1]
    eidx = lax.broadcasted_iota(jnp.int32, lg.shape, 0)
    m1 = jnp.max(lg, axis=0, keepdims=True)
    i1 = jnp.min(jnp.where(lg == m1, eidx, N_EXPERTS), axis=0, keepdims=True)
    lg2 = jnp.where(eidx == i1, -jnp.inf, lg)
    m2 = jnp.max(lg2, axis=0, keepdims=True)
    i2 = jnp.min(jnp.where(lg2 == m2, eidx, N_EXPERTS), axis=0, keepdims=True)
    e2 = jnp.exp(m2 - m1)
    wt_ref[0, 0:1, :] = 1.0 / (1.0 + e2)
    wt_ref[0, 1:2, :] = e2 / (1.0 + e2)
    sel1 = eidx == i1
    sel2 = eidx == i2
    oh = jnp.where(sel1 | sel2, 1.0, 0.0)
    before = (lax.broadcasted_iota(jnp.int32, (tm, tm), 0)
              < lax.broadcasted_iota(jnp.int32, (tm, tm), 1)).astype(BF16)
    rank = carry_scr[:, 0:1] + _dot(oh.astype(BF16), before)
    r1 = jnp.sum(jnp.where(sel1, rank, 0.0), axis=0, keepdims=True).astype(jnp.int32)
    r2 = jnp.sum(jnp.where(sel2, rank, 0.0), axis=0, keepdims=True).astype(jnp.int32)
    slot_ref[0, 0:1, :] = i1 * cap + r1
    slot_ref[0, 1:2, :] = i2 * cap + r2
    carry_scr[...] = carry_scr[...] + jnp.sum(oh, axis=1, keepdims=True)
    cnt_ref[...] = carry_scr[...]


def _route_slots(lgT, b_r, cnt0, cap):
    B, E, L = lgT.shape
    tm = min(ROUTE_TM, L)
    return pl.pallas_call(
        functools.partial(_route_kernel, cap=cap),
        out_shape=(jax.ShapeDtypeStruct((B, 2, L), jnp.int32),
                   jax.ShapeDtypeStruct((B, 2, L), F32),
                   jax.ShapeDtypeStruct((E, LANES), F32)),
        grid=(B, L // tm),
        in_specs=[pl.BlockSpec((1, E, tm), lambda b, i: (b, 0, i)),
                  pl.BlockSpec((E, 1), lambda b, i: (0, 0)),
                  pl.BlockSpec((E, LANES), lambda b, i: (0, 0))],
        out_specs=(pl.BlockSpec((1, 2, tm), lambda b, i: (b, 0, i)),
                   pl.BlockSpec((1, 2, tm), lambda b, i: (b, 0, i)),
                   pl.BlockSpec((E, LANES), lambda b, i: (0, 0))),
        scratch_shapes=[pltpu.VMEM((E, LANES), F32)],
        compiler_params=_cp(("arbitrary", "arbitrary")),
        name="moe_route",
    )(lgT, b_r.reshape(E, 1), cnt0)


def _row_copy(src, dst, sem):
    return pltpu.make_async_copy(src, dst, sem)


def _slot_rows_kernel(start_ref, per_ref, code_ref, row_ref, *, cap):
    code = code_ref[...]
    shift = cap.bit_length() - 1
    e = lax.shift_right_logical(code, shift)
    r = code & (cap - 1)
    start = jnp.zeros_like(code)
    per = jnp.ones_like(code)
    for k in range(N_EXPERTS):
        start = jnp.where(e == k, start_ref[k], start)
        per = jnp.where(e == k, per_ref[k], per)
    q = jnp.floor((r.astype(F32) + 0.5) / per.astype(F32)).astype(jnp.int32)
    row_ref[...] = start + q * MOE_SUP + (r - q * per)


def _slot_rows(start, per, codes, cap):
    B, _, L = codes.shape
    tm = min(ROUTE_TM, L)
    spec = pl.BlockSpec((1, 2, tm), lambda b, i, st, pe: (b, 0, i))
    return pl.pallas_call(
        functools.partial(_slot_rows_kernel, cap=cap),
        out_shape=jax.ShapeDtypeStruct(codes.shape, jnp.int32),
        grid_spec=pltpu.PrefetchScalarGridSpec(
            num_scalar_prefetch=2, grid=(B, L // tm), in_specs=[spec], out_specs=spec),
        compiler_params=_cp(("parallel", "parallel")),
        name="moe_slot_rows",
    )(start, per, codes)


def _zeros_kernel(o_ref):
    o_ref[...] = jnp.zeros_like(o_ref)


def _zero_rows(n_rows, width):
    return pl.pallas_call(
        _zeros_kernel,
        out_shape=jax.ShapeDtypeStruct((n_rows, width), F32),
        grid=(n_rows // MOE_SUP,),
        out_specs=pl.BlockSpec((MOE_SUP, width), lambda i: (i, 0)),
        compiler_params=_cp(("parallel",)),
        name="moe_zero_rows",
    )()


def _dispatch_kernel(row_ref, h_ref, xs_in_ref, xs_ref, hbuf, sem):
    del xs_in_ref
    tm = h_ref.shape[1]
    t = pl.program_id(0) * pl.num_programs(1) + pl.program_id(1)
    last = pl.num_programs(0) * pl.num_programs(1) - 1
    slot = t % 2
    hbuf[slot] = h_ref[0]

    def issue(r, _):
        for k in range(2):
            row = row_ref[0, k, r]
            _row_copy(hbuf.at[slot, pl.ds(r, 1), :], xs_ref.at[pl.ds(row, 1), :], sem.at[slot]).start()
        return 0

    lax.fori_loop(0, tm, issue, 0, unroll=8)

    def drain(sl):
        for k in range(2):
            _row_copy(hbuf.at[sl], xs_ref.at[pl.ds(0, tm), :], sem.at[sl]).wait()

    @pl.when(t > 0)
    def _():
        drain(1 - slot)

    @pl.when(t == last)
    def _():
        drain(slot)


def _dispatch(rows, h, xs):
    B, L, D = h.shape
    n_rows = xs.shape[0]
    tm = min(ROW_DMA_TM, L)
    return pl.pallas_call(
        _dispatch_kernel,
        out_shape=jax.ShapeDtypeStruct((n_rows, D), F32),
        grid=(B, L // tm),
        in_specs=[pl.BlockSpec((1, 2, tm), lambda b, i: (b, 0, i), memory_space=pltpu.SMEM),
                  pl.BlockSpec((1, tm, D), lambda b, i: (b, i, 0)),
                  pl.BlockSpec(memory_space=pl.ANY)],
        out_specs=pl.BlockSpec(memory_space=pl.ANY),
        scratch_shapes=[pltpu.VMEM((2, tm, D), F32), pltpu.SemaphoreType.DMA((2,))],
        input_output_aliases={2: 0},
        compiler_params=_cp(("arbitrary", "arbitrary")),
        name="moe_dispatch",
    )(rows, h, xs)


def _moe_grp_kernel(ge_ref, gn_ref, x_ref, wg_ref, wu_ref, wd_ref, y_ref, xb_scr):
    g = pl.program_id(0)
    j = pl.program_id(1)
    nsub = gn_ref[g]
    wg = wg_ref[0].astype(BF16)
    wu = wu_ref[0].astype(BF16)
    wd = wd_ref[0].astype(BF16)
    nblk = MOE_SUP // MOE_SUB

    @pl.when(j == 0)
    def _():
        xb_scr[...] = x_ref[...].astype(BF16)
        y_ref[...] = jnp.zeros_like(y_ref)

    def block(s):
        rows = slice(s * MOE_SUB, (s + 1) * MOE_SUB)
        xb = xb_scr[rows, :]
        act = _silu(_dot(xb, wg)) * _dot(xb, wu)
        y_ref[rows, :] = y_ref[rows, :] + _dot(act.astype(BF16), wd)

    for n in range(1, nblk + 1):
        @pl.when(nsub == n)
        def _():
            for s in range(n):
                block(s)


def _moe_groups(counts, n_groups):
    nsup = (counts + MOE_SUP - 1) // MOE_SUP
    div = jnp.maximum(nsup, 1)
    per = jnp.maximum(((counts + div - 1) // div + MOE_SUB - 1) // MOE_SUB * MOE_SUB, MOE_SUB)
    ends = jnp.cumsum(nsup)
    first = ends - nsup
    total = ends[-1]
    g = jnp.arange(n_groups, dtype=jnp.int32)
    gc = jnp.minimum(g, total - 1)
    e_of = jnp.minimum(jnp.sum((gc[:, None] >= ends[None, :]).astype(jnp.int32), axis=1), N_EXPERTS - 1)
    left = jnp.minimum(counts[e_of] - (gc - first[e_of]) * per[e_of], per[e_of])
    nsub = jnp.clip((left + MOE_SUB - 1) // MOE_SUB, 0, MOE_SUP // MOE_SUB)
    gn = jnp.where(g < total, nsub, 0).astype(jnp.int32)
    return e_of, gn, (first * MOE_SUP).astype(jnp.int32), per.astype(jnp.int32)


def _moe_grouped(xs, ge, gn, w_gu, w_down):
    D = xs.shape[1]
    nj = D_FF // MOE_FF_TILE
    ng = xs.shape[0] // MOE_SUP
    jj = lambda j, gn, g: jnp.where(gn[g] > 0, j, nj - 1)
    return pl.pallas_call(
        _moe_grp_kernel,
        out_shape=jax.ShapeDtypeStruct(xs.shape, F32),
        grid_spec=pltpu.PrefetchScalarGridSpec(
            num_scalar_prefetch=2,
            grid=(ng, nj),
            in_specs=[pl.BlockSpec((MOE_SUP, D), lambda g, j, ge, gn: (g, 0)),
                      pl.BlockSpec((1, D, MOE_FF_TILE), lambda g, j, ge, gn: (ge[g], 0, jj(j, gn, g))),
                      pl.BlockSpec((1, D, MOE_FF_TILE), lambda g, j, ge, gn: (ge[g], 0, nj + jj(j, gn, g))),
                      pl.BlockSpec((1, MOE_FF_TILE, D), lambda g, j, ge, gn: (ge[g], jj(j, gn, g), 0))],
            out_specs=pl.BlockSpec((MOE_SUP, D), lambda g, j, ge, gn: (g, 0)),
            scratch_shapes=[pltpu.VMEM((MOE_SUP, D), BF16)],
        ),
        compiler_params=_cp(("arbitrary", "arbitrary")),
        name="moe_experts",
    )(ge, gn, xs, w_gu, w_gu, w_down)


def _combine_kernel(row_ref, next_ref, w_ref, x_ref, gt_ref, gfin_ref, ys_ref, o_ref, g_scr, sem, *, final):
    tm = x_ref.shape[1]
    t = pl.program_id(0) * pl.num_programs(1) + pl.program_id(1)
    last = pl.num_programs(0) * pl.num_programs(1) - 1
    slot = t % 2

    def gather(rows, sl):
        def issue(r, _):
            for k in range(2):
                row = rows[0, k, r]
                _row_copy(ys_ref.at[pl.ds(row, 1), :], g_scr.at[sl, k, pl.ds(r, 1), :], sem.at[sl]).start()
            return 0

        lax.fori_loop(0, tm, issue, 0, unroll=8)

    @pl.when(t == 0)
    def _():
        gather(row_ref, slot)

    @pl.when(t < last)
    def _():
        gather(next_ref, 1 - slot)

    for k in range(2):
        _row_copy(ys_ref.at[pl.ds(0, tm), :], g_scr.at[slot, k], sem.at[slot]).wait()
    w = w_ref[0]
    f = w[:, 0:1] * g_scr[slot, 0] + w[:, 1:2] * g_scr[slot, 1]
    o_ref[0] = _finish(x_ref[0] + gt_ref[0, 0] * f, gfin_ref, final)


def _combine(rows, wts, x, mod, gfin, ys, *, l, final):
    B, L, D = x.shape
    tm = min(ROW_DMA_TM, L)
    gt_spec = _mod_spec(mod, l, 5, tm)
    row = pl.BlockSpec((1, tm, D), lambda b, i: (b, i, 0))
    nl = L // tm

    def next_block(b, i):
        t1 = jnp.minimum(b * nl + i + 1, B * nl - 1)
        return (t1 // nl, 0, t1 % nl)

    return pl.pallas_call(
        functools.partial(_combine_kernel, final=final),
        out_shape=jax.ShapeDtypeStruct((B, L, D), F32),
        grid=(B, L // tm),
        in_specs=[pl.BlockSpec((1, 2, tm), lambda b, i: (b, 0, i), memory_space=pltpu.SMEM),
                  pl.BlockSpec((1, 2, tm), next_block, memory_space=pltpu.SMEM),
                  pl.BlockSpec((1, tm, 2), lambda b, i: (b, i, 0)),
                  row, gt_spec,
                  pl.BlockSpec((1, D), lambda b, i: (0, 0)),
                  pl.BlockSpec(memory_space=pl.ANY)],
        out_specs=row,
        scratch_shapes=[pltpu.VMEM((2, 2, tm, D), F32), pltpu.SemaphoreType.DMA((2,))],
        compiler_params=_cp(("arbitrary", "arbitrary")),
        name="moe_combine",
    )(rows, rows, wts.transpose(0, 2, 1), x, mod, gfin, ys)


def _moe_routed(groups, b_r, w_gu, w_down, gfin, *, l, final):
    D = groups[0][1].shape[-1]
    n_tok = sum(g[1].shape[0] * g[1].shape[1] for g in groups)
    cap = 1 << (n_tok - 1).bit_length()
    n_groups = 2 * n_tok // MOE_SUP + N_EXPERTS
    cnt = jnp.zeros((N_EXPERTS, LANES), F32)
    routed = []
    for _, _, _, lgT in groups:
        codes, wts, cnt = _route_slots(lgT, b_r, cnt, cap)
        routed.append((codes, wts))
    ge, gn, start, per = _moe_groups(cnt[:, 0].astype(jnp.int32), n_groups)
    rows = [_slot_rows(start, per, codes, cap) for codes, _ in routed]
    xs = _zero_rows(n_groups * MOE_SUP, D)
    for (h, _, _, _), r in zip(groups, rows):
        xs = _dispatch(r, h, xs)
    ys = _moe_grouped(xs, ge, gn, w_gu, w_down)
    return [_combine(r, wts, x, mod, gfin, ys, l=l, final=final)
            for (_, x, mod, _), (_, wts), r in zip(groups, routed, rows)]


def _pad_lanes(v):
    return jnp.pad(v.reshape(1, -1), ((0, 0), (0, LANES - v.shape[-1])))


def _mixer_layer(x, mod, states, p, s5m, l, prev_sg, *, seq):
    B, L, D = x.shape
    hi = not seq
    s5r0, s5i0, sg0, sc0 = states
    w_in, w_gates, w_ab = p['w_in_seq' if seq else 'w_in']
    if seq:
        u, qkv, z, ga, gb, ab = _proj_seq(x, p['g_mix'], mod, w_in, w_gates, w_ab, l=l, tm=min(1024, L))
    else:
        u, qkv, z, ga, gb, ab = _proj(x, p['g_mix'], mod, w_in, w_gates, w_ab, l=l, tm=L)
    alog = _pad_lanes(p['gdn_a_log'][l])
    dtb = _pad_lanes(p['gdn_dt_bias'][l])
    nw = p['gdn_norm_w'][l].reshape(1, GDN_DK)
    if seq:
        yg, sfin = _s5_seq(u, s5m['be'], s5m['tp'], s5m['cpm'], s5m['pt'],
                           jnp.zeros((SLABS, B, 1, 2 * SLAB_STATE), F32), s5m['dsk'][l], l)
        sfin = sfin.reshape(SLABS, B, 2, SLAB_STATE).transpose(2, 1, 0, 3)
        sr = sfin[0].reshape(B, S5_GROUPS, S5_STATE)
        si = sfin[1].reshape(B, S5_GROUPS, S5_STATE)
        og, sg = _gdn_seq(qkv, z, ab, p['gdn_conv_w'], alog, dtb, nw,
                          jnp.zeros((B, GDN_CONV - 1, QKV_WIDTH), F32),
                          jnp.zeros((B, GDN_HEADS, GDN_DK, GDN_DK), F32), l)
        cb = qkv[:, L - (GDN_CONV - 1):, :].astype(F32)
    else:
        n = L
        s0 = jnp.concatenate([s5r0[l].reshape(n, SLABS, SLAB_STATE),
                              s5i0[l].reshape(n, SLABS, SLAB_STATE)], axis=-1).transpose(1, 0, 2)
        yg, s1 = _s5_step(u.reshape(SLABS, n, LANES), s5m['bst'], s5m['c0'], s5m['a1'],
                          s0, s5m['d1'][l], l)
        yg = yg.reshape(SLABS, 1, n, LANES)
        s1 = s1.transpose(1, 0, 2)
        sr = s1[:, :, :SLAB_STATE].reshape(n, S5_GROUPS, S5_STATE)
        si = s1[:, :, SLAB_STATE:].reshape(n, S5_GROUPS, S5_STATE)
        og, sg = _gdn_step(qkv.reshape(n, QKV_WIDTH), z.reshape(n, GDN_WIDTH), ab.reshape(n, LANES),
                           p['gdn_conv_w'][l], alog, dtb, nw,
                           sc0[l].reshape(n, (GDN_CONV - 1) * QKV_WIDTH), sg0, l, prev_sg)
        og = og.reshape(1, n, GDN_WIDTH)
        cb = jnp.concatenate([sc0[l][:, 1:, :], qkv.reshape(n, 1, QKV_WIDTH)], axis=1)
    x, h, lgT = _merge(yg, og, ga, gb, x, mod, p['w_s5_glu'], p['w_gdn_out'], p['w_out'],
                       p['g_ffn'], p['w_router'], l=l, tm=min(512, L), hi=hi, chunked=seq,
                       h_dtype=BF16 if (seq and l % 2 == 0) else F32)
    return x, h, lgT, (sr, si, sg, cb)


def kernel(x_prompt, x_sample, c_prompt, c_sample, state_s5_re, state_s5_im, state_gdn, state_conv,
           g_mix, g_ffn, g_final, w_ada, b_ada, w_in, s5_lambda_re, s5_lambda_im, s5_log_dt,
           s5_b_re, s5_b_im, s5_c_re, s5_c_im, s5_d, w_s5_glu, gdn_conv_w, gdn_a_log, gdn_dt_bias,
           gdn_norm_w, w_gdn_out, w_out, w_ffn_gate_up, w_ffn_down, w_router, b_router,
           w_exp_gate_up, w_exp_down):
    def in_proj_parts(w):
        return w, w[:, :, 2568:], jnp.pad(w[:, :, 2560:2568], ((0, 0), (0, 0), (0, LANES - 8)))

    D_ = x_prompt.shape[-1]
    p = dict(g_mix=g_mix.reshape(DEPTH, 1, D_), g_ffn=g_ffn.reshape(DEPTH, 1, D_), w_s5_glu=w_s5_glu,
             gdn_conv_w=gdn_conv_w, gdn_a_log=gdn_a_log, gdn_dt_bias=gdn_dt_bias,
             gdn_norm_w=gdn_norm_w, w_gdn_out=w_gdn_out, w_out=w_out,
             w_router=w_router.transpose(0, 2, 1), w_in=in_proj_parts(w_in),
             w_in_seq=in_proj_parts(w_in.astype(BF16)))
    nbp, L, D = x_prompt.shape
    nbs = x_sample.shape[0]

    mod = _ada(jnp.concatenate([c_prompt, c_sample], axis=0), w_ada, b_ada)
    mod_p = mod[:, :nbp].reshape(DEPTH, nbp, 1, 6 * D)
    mod_s = mod[:, nbp:].reshape(DEPTH, 1, nbs, 6 * D)

    seg = L // S5_T // 8
    be, bst, c0, cpm, tp, pt, a1 = _s5_prep(s5_lambda_re, s5_lambda_im, s5_log_dt, s5_b_re, s5_b_im,
                                            s5_c_re, s5_c_im, seg)
    d1 = [s5_d[l].reshape(SLABS, 1, LANES) for l in range(DEPTH)]
    s5m = dict(be=be, bst=bst, c0=c0, cpm=cpm, pt=pt, a1=a1, tp=tp, d1=d1,
               dsk=[jnp.tile(d, (1, 1, S5_T)) for d in d1])

    xs_ = [x_prompt, x_sample.reshape(1, nbs, D)]
    mods = [mod_p, mod_s]
    states = [(None, None, None, None), (state_s5_re, state_s5_im, state_gdn, state_conv)]
    outs = [[], []]
    gfin = g_final.reshape(1, D)
    for l in range(DEPTH):
        final = l == DEPTH - 1
        mixed = []
        for gi, seq in enumerate((True, False)):
            prev_sg = outs[gi][0][2] if (not seq and final and DEPTH == 2) else None
            x, h, lgT, st = _mixer_layer(xs_[gi], mods[gi], states[gi], p, s5m, l, prev_sg, seq=seq)
            outs[gi].append(st)
            mixed.append((h, x, mods[gi], lgT))
        if l % 2 == 0:
            wgu, wdn = w_ffn_gate_up[l // 2], w_ffn_down[l // 2]
            xs_ = [_ffn(h, x, mod_g, wgu if gi else wgu.astype(BF16), wdn if gi else wdn.astype(BF16), gfin,
                        l=l, tm=min(1024, x.shape[1]), hi=(gi == 1), final=final)
                   for gi, (h, x, mod_g, _) in enumerate(mixed)]
        else:
            xs_ = _moe_routed(mixed, b_router[l // 2], w_exp_gate_up[l // 2], w_exp_down[l // 2], gfin,
                              l=l, final=final)
    y_p, y_s = xs_
    st_p = [jnp.stack([o[i] for o in outs[0]]) for i in range(4)]
    st_s = [outs[1][-1][2] if (i == 2 and DEPTH == 2) else jnp.stack([o[i] for o in outs[1]])
            for i in range(4)]
    return (y_p, y_s.reshape(nbs, 1, D), st_p[0], st_p[1], st_p[2], st_p[3],
            st_s[0], st_s[1], st_s[2], st_s[3])
```

```python
import functools

import jax
import jax.numpy as jnp
from jax import lax
from jax.experimental import pallas as pl
from jax.experimental.pallas import tpu as pltpu

F32 = jnp.float32
BF16 = jnp.bfloat16
HI = lax.Precision.HIGHEST

D_MODEL = 1024
DEPTH = 2
S5_WIDTH = 512
S5_GROUP = 16
S5_GROUPS = 32
S5_STATE = 64
GDN_HEADS = 4
GDN_DK = 128
GDN_WIDTH = 512
GDN_CONV = 4
QKV_WIDTH = 1536
D_FF = 3584
N_EXPERTS = 8
NORM_EPS = 1e-6
L2_EPS = 1e-6

LANES = 128
SLABS = S5_WIDTH // LANES
SLAB_STATE = (S5_GROUPS // SLABS) * S5_STATE
S5_T = 8
S5_SEG_PAD = 4
GDN_C = 128
GDN_BLOCK = 512
GDN_SUB = 256
VMEM_LIMIT = 56 * 1024 * 1024


def _cp(sem, vmem=VMEM_LIMIT):
    return pltpu.CompilerParams(dimension_semantics=sem, vmem_limit_bytes=vmem)


def _dot(a, b, prec=None):
    return jnp.dot(a, b, precision=prec, preferred_element_type=F32)


def _dotb(a, b):
    return jnp.dot(a.astype(BF16), b.astype(BF16), preferred_element_type=F32)


def _dot_nt(a, b, prec=None):
    return lax.dot_general(a, b, (((1,), (1,)), ((), ())), precision=prec,
                           preferred_element_type=F32)


def _dot_tn(a, b, prec=None):
    return lax.dot_general(a, b, (((0,), (0,)), ((), ())), precision=prec,
                           preferred_element_type=F32)


def _dot3(a, b):
    ah, al = _split_bf16(a)
    bh, bl = _split_bf16(b)
    return _dot(ah, bh) + (_dot(ah, bl) + _dot(al, bh))


def _silu(x):
    return x * jax.nn.sigmoid(x)


def _ada_kernel(c_ref, w_ref, b_ref, o_ref):
    cs = _silu(c_ref[...])
    o_ref[0] = _dot3(cs, w_ref[0]) + b_ref[0]


def _ada(c_all, w_ada, b_ada):
    n = c_all.shape[0]
    tn = 1536
    return pl.pallas_call(
        _ada_kernel,
        out_shape=jax.ShapeDtypeStruct((DEPTH, n, 6 * D_MODEL), F32),
        grid=(DEPTH, 6 * D_MODEL // tn),
        in_specs=[pl.BlockSpec((n, D_MODEL), lambda l, j: (0, 0)),
                  pl.BlockSpec((1, D_MODEL, tn), lambda l, j: (l, 0, j)),
                  pl.BlockSpec((1, 1, tn), lambda l, j: (l, 0, j))],
        out_specs=pl.BlockSpec((1, n, tn), lambda l, j: (l, 0, j)),
        compiler_params=_cp(("parallel", "parallel")),
        name="ada_mod",
    )(c_all, w_ada, b_ada.reshape(DEPTH, 1, 6 * D_MODEL))


def _proj_kernel(x_ref, g_ref, sc_ref, sh_ref, w_ref, wg_ref, wab_ref,
                 u_ref, qkv_ref, z_ref, ga_ref, gb_ref, ab_ref, h_scr):
    j = pl.program_id(2)

    @pl.when(j == 0)
    def _():
        x = x_ref[0]
        ms = jnp.mean(x * x, axis=-1, keepdims=True)
        xn = x * lax.rsqrt(ms + NORM_EPS) * g_ref[0]
        h_scr[...] = (xn * (1.0 + sc_ref[0, 0]) + sh_ref[0, 0]).astype(h_scr.dtype)

    def mm(w):
        return _dot3(h_scr[...], w)

    @pl.when(j == 0)
    def _():
        res = mm(w_ref[0])
        for k in range(SLABS):
            u_ref[k, 0] = res[:, k * LANES:(k + 1) * LANES]

    @pl.when((j >= 1) & (j <= 3))
    def _():
        qkv_ref[0] = mm(w_ref[0])

    @pl.when(j == 4)
    def _():
        z_ref[0] = mm(w_ref[0])

    @pl.when((j == 5) | (j == 6))
    def _():
        ga_ref[0] = jax.nn.sigmoid(mm(wg_ref[0]))

    @pl.when((j == 7) | (j == 8))
    def _():
        gb_ref[0] = jax.nn.sigmoid(mm(wg_ref[0]))

    @pl.when(j == 9)
    def _():
        ab_ref[0] = mm(wab_ref[0])


def _mod_spec(mod, l, chunk, tm):
    per_row = mod.shape[2] != 1
    D = mod.shape[3] // 6

    def index(b, i, *_):
        return (l, b, i if per_row else 0, chunk)

    return pl.BlockSpec((1, 1, tm if per_row else 1, D), index)


def _proj(x, g, mod, w_in, w_gates, w_ab, *, l, tm):
    B, L, D = x.shape
    tn = 512
    clampi = lambda j, lo, n: jnp.clip(j - lo, 0, n - 1)
    outs = pl.pallas_call(
        _proj_kernel,
        out_shape=(jax.ShapeDtypeStruct((SLABS, B, L, LANES), F32),
                   jax.ShapeDtypeStruct((B, L, QKV_WIDTH), F32),
                   jax.ShapeDtypeStruct((B, L, GDN_WIDTH), F32),
                   jax.ShapeDtypeStruct((B, L, D), F32),
                   jax.ShapeDtypeStruct((B, L, D), F32),
                   jax.ShapeDtypeStruct((B, L, LANES), F32)),
        grid=(B, L // tm, 10),
        in_specs=[pl.BlockSpec((1, tm, D), lambda b, i, j: (b, i, 0)),
                  pl.BlockSpec((1, 1, D), lambda b, i, j: (l, 0, 0)),
                  _mod_spec(mod, l, 1, tm),
                  _mod_spec(mod, l, 0, tm),
                  pl.BlockSpec((1, D, tn), lambda b, i, j: (l, 0, jnp.minimum(j, 4))),
                  pl.BlockSpec((1, D, tn), lambda b, i, j: (l, 0, clampi(j, 5, 4))),
                  pl.BlockSpec((1, D, LANES), lambda b, i, j: (l, 0, 0))],
        out_specs=(pl.BlockSpec((SLABS, 1, tm, LANES), lambda b, i, j: (0, b, i, 0)),
                   pl.BlockSpec((1, tm, tn), lambda b, i, j: (b, i, clampi(j, 1, 3))),
                   pl.BlockSpec((1, tm, tn), lambda b, i, j: (b, i, 0)),
                   pl.BlockSpec((1, tm, tn), lambda b, i, j: (b, i, clampi(j, 5, 2))),
                   pl.BlockSpec((1, tm, tn), lambda b, i, j: (b, i, clampi(j, 7, 2))),
                   pl.BlockSpec((1, tm, LANES), lambda b, i, j: (b, i, 0))),
        scratch_shapes=[pltpu.VMEM((tm, D), F32)],
        compiler_params=_cp(("parallel", "parallel", "arbitrary")),
        name="norm_in_proj",
    )(x, g, mod, mod, w_in, w_gates, w_ab)
    return outs


def _proj_seq_kernel(x_ref, g_ref, sc_ref, sh_ref, w_ref, wg_ref, wab_ref,
                     u_ref, qkv_ref, z_ref, ga_ref, gb_ref, ab_ref, us_scr):
    x = x_ref[0]
    ms = jnp.mean(x * x, axis=-1, keepdims=True)
    xn = x * lax.rsqrt(ms + NORM_EPS) * g_ref[0]
    h = (xn * (1.0 + sc_ref[0, 0]) + sh_ref[0, 0]).astype(BF16)
    res = _dot(h, w_ref[0, :, 0:S5_WIDTH])
    nrow = res.shape[0] // S5_T
    for k in range(SLABS):
        us_scr[...] = res[:, k * LANES:(k + 1) * LANES]
        for t in range(S5_T):
            u_ref[k, 0, :, t * LANES:(t + 1) * LANES] = (
                us_scr[pl.ds(t, nrow, stride=S5_T), :].astype(u_ref.dtype))
    c0 = S5_WIDTH
    qkv_ref[0] = _dot(h, w_ref[0, :, c0:c0 + QKV_WIDTH]).astype(qkv_ref.dtype)
    c0 += QKV_WIDTH
    z_ref[0] = _dot(h, w_ref[0, :, c0:c0 + GDN_WIDTH]).astype(z_ref.dtype)
    D = x.shape[-1]
    ga_ref[0] = jax.nn.sigmoid(_dot(h, wg_ref[0, :, 0:D])).astype(ga_ref.dtype)
    gb_ref[0] = jax.nn.sigmoid(_dot(h, wg_ref[0, :, D:2 * D])).astype(gb_ref.dtype)
    ab_ref[0] = _dot(h, wab_ref[0])


def _proj_seq(x, g, mod, w_in, w_gates, w_ab, *, l, tm):
    B, L, D = x.shape
    n_main = S5_WIDTH + QKV_WIDTH + GDN_WIDTH
    row = lambda w: pl.BlockSpec((1, tm, w), lambda b, i: (b, i, 0))
    return pl.pallas_call(
        _proj_seq_kernel,
        out_shape=(jax.ShapeDtypeStruct((SLABS, B, L // S5_T, S5_T * LANES), BF16),
                   jax.ShapeDtypeStruct((B, L, QKV_WIDTH), BF16),
                   jax.ShapeDtypeStruct((B, L, GDN_WIDTH), BF16),
                   jax.ShapeDtypeStruct((B, L, D), BF16),
                   jax.ShapeDtypeStruct((B, L, D), BF16),
                   jax.ShapeDtypeStruct((B, L, LANES), F32)),
        grid=(B, L // tm),
        in_specs=[row(D),
                  pl.BlockSpec((1, 1, D), lambda b, i: (l, 0, 0)),
                  _mod_spec(mod, l, 1, tm),
                  _mod_spec(mod, l, 0, tm),
                  pl.BlockSpec((1, D, n_main), lambda b, i: (l, 0, 0)),
                  pl.BlockSpec((1, D, 2 * D), lambda b, i: (l, 0, 0)),
                  pl.BlockSpec((1, D, LANES), lambda b, i: (l, 0, 0))],
        out_specs=(pl.BlockSpec((SLABS, 1, tm // S5_T, S5_T * LANES), lambda b, i: (0, b, i, 0)),
                   row(QKV_WIDTH), row(GDN_WIDTH), row(D), row(D), row(LANES)),
        scratch_shapes=[pltpu.VMEM((tm, LANES), F32)],
        compiler_params=_cp(("parallel", "parallel")),
        name="norm_in_proj_seq",
    )(x, g, mod, mod, w_in, w_gates, w_ab)


GROUPS_PER_SLAB = S5_GROUPS // SLABS


def _s5_prep_kernel(lrb, lib, dtb, bre, bim, lrc, lic, dtc, cre, cim, lrn, lin, dtn,
                    be_ref, bst_ref, c0_ref, cpm_ref, tp_ref, pt_ref, a1_ref, cpe_scr, *, seg):
    W = SLAB_STATE

    def disc(lr, li, ldt):
        dt = jnp.exp(ldt)
        mag = jnp.exp(lr * dt)
        return mag * jnp.cos(li * dt), mag * jnp.sin(li * dt)

    def cmul(xr, xi, yr, yi):
        return xr * yr - xi * yi, xr * yi + xi * yr

    lr, li = lrb[0], lib[0]
    ar, ai = disc(lr, li, dtb[0])
    den = lr * lr + li * li
    nr = ar - 1.0
    kr = (nr * lr + ai * li) / den
    ki = (ai * lr - nr * li) / den
    br, bi = bre[0], bim[0]
    bbr = kr * br - ki * bi
    bbi = kr * bi + ki * br
    rgrp = lax.broadcasted_iota(jnp.int32, (LANES, LANES), 0) // S5_GROUP
    lane_hi = lax.broadcasted_iota(jnp.int32, (LANES, LANES), 1) // S5_STATE
    pr, pi = jnp.ones_like(ar), jnp.zeros_like(ar)
    for d in range(S5_T):
        t = S5_T - 1 - d
        for ri, val in enumerate(cmul(pr, pi, bbr, bbi)):
            two = jnp.concatenate([val, val], axis=1)
            for m in range(GROUPS_PER_SLAB // 2):
                tile = jnp.where(rgrp == 2 * m + lane_hi, two, 0.0)
                c0 = ri * W + m * LANES
                be_ref[0, 0, t * LANES:(t + 1) * LANES, c0:c0 + LANES] = tile.astype(BF16)
                if d == 0:
                    bst_ref[0, 0, :, c0:c0 + LANES] = tile
        pr, pi = cmul(pr, pi, ar, ai)

    ar, ai = disc(lrc[0], lic[0], dtc[0])
    cr, ci = cre[0], cim[0]
    own = (lax.broadcasted_iota(jnp.int32, (W, LANES), 0) // S5_STATE
           == lax.broadcasted_iota(jnp.int32, (W, LANES), 1) // S5_GROUP)
    pr, pi = jnp.ones_like(ar), jnp.zeros_like(ar)
    for d in range(S5_T + 1):
        vr, vi = cmul(cr, ci, pr, pi)
        for ri, val in enumerate((vr, -vi)):
            tile = jnp.where(own, val, 0.0)
            cpe_scr[d, ri * W:(ri + 1) * W, :] = tile
            if d == 0:
                c0_ref[0, 0, ri * W:(ri + 1) * W, :] = tile
            if d >= 1:
                cpm_ref[0, 0, ri * W:(ri + 1) * W, (d - 1) * LANES:d * LANES] = tile.astype(BF16)
        pr, pi = cmul(pr, pi, ar, ai)

    bst = bst_ref[0, 0]
    lag = [_dot(bst, cpe_scr[d], HI).astype(BF16) for d in range(S5_T)]
    for dd in range(S5_T // 2):
        tp_ref[0, 0, dd, 0:LANES, 0:LANES] = lag[2 * dd]
        tp_ref[0, 0, dd, LANES:, LANES:] = lag[2 * dd]
        tp_ref[0, 0, dd, 0:LANES, LANES:] = lag[2 * dd + 1]
        tp_ref[0, 0, dd, LANES:, 0:LANES] = lag[2 * dd - 1] if dd else jnp.zeros((LANES, LANES), BF16)

    ar, ai = disc(lrn[0, 0], lin[0, 0], dtn[0, 0])
    a1_ref[0, 0, :, 0:W] = ar
    a1_ref[0, 0, :, W:2 * W] = ai
    tr, ti = ar, ai
    for _ in range(S5_T - 1):
        tr, ti = cmul(tr, ti, ar, ai)
    pr, pi = jnp.ones_like(ar), jnp.zeros_like(ar)
    for i in range(seg + 1):
        pt_ref[0, 0, i:i + 1, 0:W] = pr
        pt_ref[0, 0, i:i + 1, W:2 * W] = pi
        pr, pi = cmul(pr, pi, tr, ti)


def _s5_prep(lam_re, lam_im, log_dt, b_re, b_im, c_re, c_im, seg):
    G, P, C = S5_GROUPS, S5_STATE, S5_GROUP
    W2 = 2 * SLAB_STATE
    dt3 = jnp.broadcast_to(log_dt[:, :, None], (DEPTH, G, P))
    rows_b = lambda a: jnp.repeat(a, C, axis=1)
    bt = lambda a: a.transpose(0, 1, 3, 2).reshape(DEPTH, G * C, P)
    rows_c = lambda a: jnp.broadcast_to(a.reshape(DEPTH, G * P, 1), (DEPTH, G * P, LANES))
    ct = lambda a: jnp.tile(a.transpose(0, 1, 3, 2).reshape(DEPTH, G * P, C), (1, 1, LANES // C))
    nat = lambda a: a.reshape(DEPTH, SLABS, 1, SLAB_STATE)
    args = (rows_b(lam_re), rows_b(lam_im), rows_b(dt3), bt(b_re), bt(b_im),
            rows_c(lam_re), rows_c(lam_im), rows_c(dt3), ct(c_re), ct(c_im),
            nat(lam_re), nat(lam_im), nat(dt3))
    bspec = pl.BlockSpec((1, LANES, P), lambda l, k: (l, k, 0))
    cspec = pl.BlockSpec((1, SLAB_STATE, LANES), lambda l, k: (l, k, 0))
    nspec = pl.BlockSpec((1, 1, 1, SLAB_STATE), lambda l, k: (l, k, 0, 0))
    return pl.pallas_call(
        functools.partial(_s5_prep_kernel, seg=seg),
        out_shape=(jax.ShapeDtypeStruct((DEPTH, SLABS, S5_T * LANES, W2), BF16),
                   jax.ShapeDtypeStruct((DEPTH, SLABS, LANES, W2), F32),
                   jax.ShapeDtypeStruct((DEPTH, SLABS, W2, LANES), F32),
                   jax.ShapeDtypeStruct((DEPTH, SLABS, W2, S5_T * LANES), BF16),
                   jax.ShapeDtypeStruct((DEPTH, SLABS, S5_T // 2, 2 * LANES, 2 * LANES), BF16),
                   jax.ShapeDtypeStruct((DEPTH, SLABS, seg + 1, W2), F32),
                   jax.ShapeDtypeStruct((DEPTH, SLABS, 1, W2), F32)),
        grid=(DEPTH, SLABS),
        in_specs=[bspec] * 5 + [cspec] * 5 + [nspec] * 3,
        out_specs=(pl.BlockSpec((1, 1, S5_T * LANES, W2), lambda l, k: (l, k, 0, 0)),
                   pl.BlockSpec((1, 1, LANES, W2), lambda l, k: (l, k, 0, 0)),
                   pl.BlockSpec((1, 1, W2, LANES), lambda l, k: (l, k, 0, 0)),
                   pl.BlockSpec((1, 1, W2, S5_T * LANES), lambda l, k: (l, k, 0, 0)),
                   pl.BlockSpec((1, 1, S5_T // 2, 2 * LANES, 2 * LANES), lambda l, k: (l, k, 0, 0, 0)),
                   pl.BlockSpec((1, 1, seg + 1, W2), lambda l, k: (l, k, 0, 0)),
                   pl.BlockSpec((1, 1, 1, W2), lambda l, k: (l, k, 0, 0))),
        scratch_shapes=[pltpu.VMEM((S5_T + 1, W2, LANES), F32)],
        compiler_params=_cp(("parallel", "parallel")),
        name="s5_discretize",
    )(*args)


def _s5_seq_kernel(up_ref, be_ref, tp_ref, cpm_ref, pt_ref, s0_ref, dsk_ref,
                   yg_ref, sfin_ref, e_scr, sx_scr, *, nc):
    seg = nc // 8
    W = SLAB_STATE
    nt = W // LANES
    ub = up_ref[0, 0]
    u = ub.astype(F32)
    e = _dot(ub, be_ref[0, 0])
    pitch = e_scr.shape[1] // 8
    for c in range(2 * nt):
        for j in range(8):
            e_scr[c, j * pitch:j * pitch + seg, :] = e[j * seg:(j + 1) * seg, c * LANES:(c + 1) * LANES]

    def tiles(row):
        return [(row[:, c * LANES:(c + 1) * LANES], row[:, W + c * LANES:W + (c + 1) * LANES])
                for c in range(nt)]

    a8 = [(jnp.broadcast_to(r, (8, LANES)), jnp.broadcast_to(i, (8, LANES)))
          for r, i in tiles(pt_ref[0, 0, 1:2, :])]

    def step(i, carry):
        rows = pl.ds(i, 8, stride=pitch)
        new = []
        for c in range(nt):
            sr, si = carry[c]
            ar, ai = a8[c]
            sx_scr[c, rows, :] = sr
            sx_scr[nt + c, rows, :] = si
            new.append((ar * sr - ai * si + e_scr[c, rows, :],
                        ar * si + ai * sr + e_scr[nt + c, rows, :]))
        return tuple(new)

    zero = jnp.zeros((8, LANES), F32)
    ends = lax.fori_loop(0, seg, step, tuple((zero, zero) for _ in range(nt)))

    al = tiles(pt_ref[0, 0, seg:seg + 1, :])
    cur = tiles(s0_ref[0, 0])
    car = []
    for c in range(nt):
        alr, ali = al[c]
        cr, ci = cur[c]
        sr, si = ends[c]
        crs, cis = [], []
        for j in range(8):
            crs.append(cr)
            cis.append(ci)
            cr, ci = (alr * cr - ali * ci + sr[j:j + 1], alr * ci + ali * cr + si[j:j + 1])
        sfin_ref[0, 0, :, c * LANES:(c + 1) * LANES] = cr
        sfin_ref[0, 0, :, W + c * LANES:W + (c + 1) * LANES] = ci
        car.append((jnp.concatenate(crs, axis=0), jnp.concatenate(cis, axis=0)))

    def corr(i, _):
        rows = pl.ds(i, 8, stride=pitch)
        pw = tiles(pt_ref[0, 0, pl.ds(i, 1), :])
        for c in range(nt):
            pr, pi = pw[c]
            cr, ci = car[c]
            sx_scr[c, rows, :] = sx_scr[c, rows, :] + (pr * cr - pi * ci)
            sx_scr[nt + c, rows, :] = sx_scr[nt + c, rows, :] + (pr * ci + pi * cr)
        return 0

    lax.fori_loop(0, seg, corr, 0)

    sx = jnp.concatenate(
        [jnp.concatenate([sx_scr[c, j * pitch:j * pitch + seg, :] for j in range(8)], axis=0)
         for c in range(2 * nt)], axis=-1)
    y = _dot(sx.astype(BF16), cpm_ref[0, 0])
    TW = 2 * LANES
    for tq in range(S5_T // 2):
        acc = y[:, tq * TW:(tq + 1) * TW]
        for tpi in range(tq + 1):
            acc = acc + _dot(ub[:, tpi * TW:(tpi + 1) * TW], tp_ref[0, 0, tq - tpi])
        acc = acc + dsk_ref[0, :, tq * TW:(tq + 1) * TW] * u[:, tq * TW:(tq + 1) * TW]
        yg_ref[0, 0, :, tq * TW:(tq + 1) * TW] = jax.nn.gelu(acc).astype(yg_ref.dtype)


def _s5_seq(up, be_emb, tp, cpm, pt, s0, dsk, l):
    _, B, nc, _ = up.shape
    seg = nc // 8
    W2 = 2 * SLAB_STATE
    yg, sfin = pl.pallas_call(
        functools.partial(_s5_seq_kernel, nc=nc),
        out_shape=(jax.ShapeDtypeStruct((SLABS, B, nc, S5_T * LANES), BF16),
                   jax.ShapeDtypeStruct((SLABS, B, 1, W2), F32)),
        grid=(SLABS, B),
        in_specs=[pl.BlockSpec((1, 1, nc, S5_T * LANES), lambda k, b: (k, b, 0, 0)),
                  pl.BlockSpec((1, 1, S5_T * LANES, W2), lambda k, b: (l, k, 0, 0)),
                  pl.BlockSpec((1, 1, S5_T // 2, 2 * LANES, 2 * LANES), lambda k, b: (l, k, 0, 0, 0)),
                  pl.BlockSpec((1, 1, W2, S5_T * LANES), lambda k, b: (l, k, 0, 0)),
                  pl.BlockSpec((1, 1, seg + 1, W2), lambda k, b: (l, k, 0, 0)),
                  pl.BlockSpec((1, 1, 1, W2), lambda k, b: (k, b, 0, 0)),
                  pl.BlockSpec((1, 1, S5_T * LANES), lambda k, b: (k, 0, 0))],
        out_specs=(pl.BlockSpec((1, 1, nc, S5_T * LANES), lambda k, b: (k, b, 0, 0)),
                   pl.BlockSpec((1, 1, 1, W2), lambda k, b: (k, b, 0, 0))),
        scratch_shapes=[pltpu.VMEM((W2 // LANES, 8 * (seg + S5_SEG_PAD), LANES), F32),
                        pltpu.VMEM((W2 // LANES, 8 * (seg + S5_SEG_PAD), LANES), F32)],
        compiler_params=_cp(("parallel", "parallel")),
        name="s5_seq",
    )(up, be_emb, tp, cpm, pt, s0, dsk)
    return yg, sfin


def _s5_step_kernel(u_ref, b_ref, c_ref, a_ref, s0_ref, d_ref, yg_ref, s1_ref):
    W = SLAB_STATE
    u = u_ref[0]
    bu = _dot(u, b_ref[0, 0], HI)
    ar = a_ref[0, 0, :, 0:W]
    ai = a_ref[0, 0, :, W:2 * W]
    sr = s0_ref[0, :, 0:W]
    si = s0_ref[0, :, W:2 * W]
    nr = ar * sr - ai * si + bu[:, 0:W]
    ni = ar * si + ai * sr + bu[:, W:2 * W]
    s1_ref[0, :, 0:W] = nr
    s1_ref[0, :, W:2 * W] = ni
    s1 = jnp.concatenate([nr, ni], axis=-1)
    y = _dot(s1, c_ref[0, 0], HI) + d_ref[0] * u
    yg_ref[0] = jax.nn.gelu(y)


def _s5_step(u_slab, bst, c0, a1, s0, d1, l):
    _, N, _ = u_slab.shape
    W2 = 2 * SLAB_STATE
    return pl.pallas_call(
        _s5_step_kernel,
        out_shape=(jax.ShapeDtypeStruct((SLABS, N, LANES), F32),
                   jax.ShapeDtypeStruct((SLABS, N, W2), F32)),
        grid=(SLABS,),
        in_specs=[pl.BlockSpec((1, N, LANES), lambda k: (k, 0, 0)),
                  pl.BlockSpec((1, 1, LANES, W2), lambda k: (l, k, 0, 0)),
                  pl.BlockSpec((1, 1, W2, LANES), lambda k: (l, k, 0, 0)),
                  pl.BlockSpec((1, 1, 1, W2), lambda k: (l, k, 0, 0)),
                  pl.BlockSpec((1, N, W2), lambda k: (k, 0, 0)),
                  pl.BlockSpec((1, 1, LANES), lambda k: (k, 0, 0))],
        out_specs=(pl.BlockSpec((1, N, LANES), lambda k: (k, 0, 0)),
                   pl.BlockSpec((1, N, W2), lambda k: (k, 0, 0))),
        compiler_params=_cp(("parallel",)),
        name="s5_step",
    )(u_slab, bst, c0, a1, s0, d1)


def _l2n(x):
    return x * lax.rsqrt(jnp.sum(x * x, axis=-1, keepdims=True) + L2_EPS)


def _split_bf16(x):
    hi = x.astype(BF16)
    return hi, (x - hi.astype(F32)).astype(BF16)


def _unit_lower_solve(As, rhss):
    n = GDN_C
    row = lax.broadcasted_iota(jnp.int32, (n, n), 0)
    col = lax.broadcasted_iota(jnp.int32, (n, n), 1)
    eye = (row == col).astype(F32)
    same8 = (row // 8) == (col // 8)
    Qs = [jnp.where(same8, -A, 0.0) for A in As]
    invs = [eye + Q for Q in Qs]
    for _ in range(2):
        Qs = [_dotb(Q, Q) for Q in Qs]
        invs = [inv + _dotb(inv, Q) for inv, Q in zip(invs, Qs)]
    s = 8
    while s < n:
        sib = ((row // (2 * s)) == (col // (2 * s))) & ((row // s) != (col // s))
        offs = [jnp.where(sib, A, 0.0).astype(BF16) for A in As]
        invb = [inv.astype(BF16) for inv in invs]
        tmp = [_dot(off, ib) for off, ib in zip(offs, invb)]
        invs = [inv - _dot(ib, t.astype(BF16)) for inv, ib, t in zip(invs, invb, tmp)]
        s *= 2
    invb = [inv.astype(BF16) for inv in invs]
    x0s = [_dot(ib, rhs.astype(BF16)) for ib, rhs in zip(invb, rhss)]
    res = []
    for A, x0, rhs in zip(As, x0s, rhss):
        ah, al = _split_bf16(A)
        xh, xl = _split_bf16(x0)
        res.append(rhs - x0 - (_dot(ah, xh) + _dot(ah, xl) + _dot(al, xh)))
    return [x0 + _dot(ib, r.astype(BF16)) for x0, ib, r in zip(x0s, invb, res)]


def _gdn_tile(qc_scr, gc, beta, z_ref, nw, o_ref, s_scr, tl, r0):
    C, DK, H = GDN_C, GDN_DK, GDN_HEADS
    nchunk = tl // C
    probs = [(c, h) for c in range(nchunk) for h in range(H)]
    row = lax.broadcasted_iota(jnp.int32, (C, C), 0)
    col = lax.broadcasted_iota(jnp.int32, (C, C), 1)
    tri = row >= col
    strict = row > col

    def blk(c, off):
        return qc_scr[c * C:(c + 1) * C, off:off + DK]

    q = [_l2n(blk(c, h * DK)) * (DK ** -0.5) for c, h in probs]
    k = [_l2n(blk(c, GDN_WIDTH + h * DK)) for c, h in probs]
    v = [blk(c, 2 * GDN_WIDTH + h * DK) for c, h in probs]
    gcb = [jnp.broadcast_to(gc[c * C:(c + 1) * C, h:h + 1], (C, DK)) for c, h in probs]
    bb = [jnp.broadcast_to(beta[c * C:(c + 1) * C, H + h:H + h + 1], (C, DK)) for c, h in probs]
    decay = []
    for g in gcb:
        diff = g - g.T
        decay.append(jnp.where(tri, jnp.exp(jnp.where(tri, diff, 0.0)), 0.0))
    kbf = [x.astype(BF16) for x in k]
    kb = [x * b for x, b in zip(k, bb)]
    A = [jnp.where(strict, _dot_nt(x.astype(BF16), y) * d, 0.0) for x, y, d in zip(kb, kbf, decay)]
    egc = [jnp.exp(g) for g in gcb]
    rhs = [jnp.concatenate([x * b, y * e], axis=-1) for x, b, y, e in zip(v, bb, kb, egc)]
    sol = _unit_lower_solve(A, rhs)
    attn = [jnp.where(tri, _dot_nt(x.astype(BF16), y) * d, 0.0).astype(BF16)
            for x, y, d in zip(q, kbf, decay)]
    glast = [g[C - 1:C, :] for g in gcb]
    wq = [jnp.concatenate([s[:, DK:], x * e], axis=0).astype(BF16) for s, x, e in zip(sol, q, egc)]
    kg = [(x * jnp.exp(gl - g)).astype(BF16) for x, gl, g in zip(k, glast, gcb)]

    for c in range(nchunk):
        ps = [c * H + h for h in range(H)]
        S = [s_scr[h] for h in range(H)]
        ws = [_dot(wq[p], S[h].astype(BF16)) for h, p in enumerate(ps)]
        v_new = [sol[p][:, 0:DK] - w[0:C] for p, w in zip(ps, ws)]
        vb = [x.astype(BF16) for x in v_new]
        o = [w[C:] + _dot(attn[p], x) for p, w, x in zip(ps, ws, vb)]
        for h, p in enumerate(ps):
            s_scr[h] = S[h] * jnp.exp(glast[p]) + _dot_tn(kg[p], vb[h])
            zh = z_ref[0, r0 + c * C:r0 + (c + 1) * C, h * DK:(h + 1) * DK].astype(F32)
            on = o[h] * lax.rsqrt(jnp.mean(o[h] * o[h], axis=-1, keepdims=True) + NORM_EPS) * nw
            o_ref[0, r0 + c * C:r0 + (c + 1) * C, h * DK:(h + 1) * DK] = (on * _silu(zh)).astype(o_ref.dtype)


def _gdn_seq_kernel(qkv_ref, z_ref, ab_ref, cw_ref, alog_ref, dtb_ref, nw_ref, conv0_ref, s0_ref,
                    o_ref, sfin_ref, xp_scr, qc_scr, s_scr, *, tl):
    lt = pl.program_id(1)

    @pl.when(lt == 0)
    def _():
        xp_scr[0:8, :] = jnp.zeros((8, QKV_WIDTH), F32)
        xp_scr[8 - (GDN_CONV - 1):8, :] = conv0_ref[0]
        s_scr[...] = s0_ref[0]

    sub = qc_scr.shape[0]
    for r0 in range(0, tl, sub):
        xp_scr[8:8 + sub, :] = qkv_ref[0, r0:r0 + sub, :].astype(F32)
        conv = cw_ref[0, 0:1, :] * xp_scr[5:5 + sub, :]
        for j in range(1, GDN_CONV):
            conv = conv + cw_ref[0, j:j + 1, :] * xp_scr[5 + j:5 + j + sub, :]
        xp_scr[0:8, :] = xp_scr[sub:sub + 8, :]
        qc_scr[...] = _silu(conv)

        ab = ab_ref[0, r0:r0 + sub, :]
        g = -jnp.exp(alog_ref[...]) * jax.nn.softplus(ab + dtb_ref[...])
        beta = jax.nn.sigmoid(ab)
        row = lax.broadcasted_iota(jnp.int32, (sub, sub), 0)
        col = lax.broadcasted_iota(jnp.int32, (sub, sub), 1)
        csum = ((row >= col) & ((row // GDN_C) == (col // GDN_C))).astype(F32)
        gc = _dot(csum, g, HI)
        _gdn_tile(qc_scr, gc, beta, z_ref, nw_ref[...], o_ref, s_scr, sub, r0)

    @pl.when(lt == pl.num_programs(1) - 1)
    def _():
        sfin_ref[0] = s_scr[...]


def _gdn_seq(qkv, z, ab, conv_w, alog, dtb, nw, conv0, s0, l):
    B, L, _ = qkv.shape
    tl = min(GDN_BLOCK, L)
    sub = min(GDN_SUB, tl)
    return pl.pallas_call(
        functools.partial(_gdn_seq_kernel, tl=tl),
        out_shape=(jax.ShapeDtypeStruct((B, L, GDN_WIDTH), BF16),
                   jax.ShapeDtypeStruct((B, GDN_HEADS, GDN_DK, GDN_DK), F32)),
        grid=(B, L // tl),
        in_specs=[pl.BlockSpec((1, tl, QKV_WIDTH), lambda b, i: (b, i, 0)),
                  pl.BlockSpec((1, tl, GDN_WIDTH), lambda b, i: (b, i, 0)),
                  pl.BlockSpec((1, tl, LANES), lambda b, i: (b, i, 0)),
                  pl.BlockSpec((1, GDN_CONV, QKV_WIDTH), lambda b, i: (l, 0, 0)),
                  pl.BlockSpec((1, LANES), lambda b, i: (0, 0)),
                  pl.BlockSpec((1, LANES), lambda b, i: (0, 0)),
                  pl.BlockSpec((1, GDN_DK), lambda b, i: (0, 0)),
                  pl.BlockSpec((1, GDN_CONV - 1, QKV_WIDTH), lambda b, i: (b, 0, 0)),
                  pl.BlockSpec((1, GDN_HEADS, GDN_DK, GDN_DK), lambda b, i: (b, 0, 0, 0))],
        out_specs=(pl.BlockSpec((1, tl, GDN_WIDTH), lambda b, i: (b, i, 0)),
                   pl.BlockSpec((1, GDN_HEADS, GDN_DK, GDN_DK), lambda b, i: (b, 0, 0, 0))),
        scratch_shapes=[pltpu.VMEM((sub + 8, QKV_WIDTH), F32),
                        pltpu.VMEM((sub, QKV_WIDTH), F32),
                        pltpu.VMEM((GDN_HEADS, GDN_DK, GDN_DK), F32)],
        compiler_params=_cp(("parallel", "arbitrary")),
        name="gdn_seq",
    )(qkv, z, ab, conv_w, alog, dtb, nw, conv0, s0)


GDN_STEP_ROWS = 8


def _gdn_step_kernel(qkv_ref, z_ref, ab_ref, cw_ref, alog_ref, dtb_ref, nw_ref, conv0_ref, s0_ref,
                     *rest):
    if len(rest) == 3:
        prev_ref, o_ref, s1_all = rest
        s1_all[0] = prev_ref[...]
        s1_ref = s1_all.at[1]
    else:
        o_ref, s1_all = rest
        s1_ref = s1_all
    nb = GDN_STEP_ROWS
    W = QKV_WIDTH
    conv = cw_ref[0:1, :] * conv0_ref[:, 0:W]
    conv = conv + cw_ref[1:2, :] * conv0_ref[:, W:2 * W]
    conv = conv + cw_ref[2:3, :] * conv0_ref[:, 2 * W:3 * W]
    conv = conv + cw_ref[3:4, :] * qkv_ref[...]
    qc = _silu(conv)
    ab = ab_ref[...]
    eg = jnp.exp(-jnp.exp(alog_ref[...]) * jax.nn.softplus(ab + dtb_ref[...]))
    beta = jax.nn.sigmoid(ab)
    eye = (lax.broadcasted_iota(jnp.int32, (GDN_DK, GDN_DK), 0)
           == lax.broadcasted_iota(jnp.int32, (GDN_DK, GDN_DK), 1)).astype(F32)
    for h in range(GDN_HEADS):
        q = _l2n(qc[:, h * GDN_DK:(h + 1) * GDN_DK]) * (GDN_DK ** -0.5)
        k = _l2n(qc[:, GDN_WIDTH + h * GDN_DK:GDN_WIDTH + (h + 1) * GDN_DK])
        v = qc[:, 2 * GDN_WIDTH + h * GDN_DK:2 * GDN_WIDTH + (h + 1) * GDN_DK]
        kT = _dot_nt(eye, k, HI)
        qT = _dot_nt(eye, q, HI)
        qk = jnp.sum(q * k, axis=-1, keepdims=True)
        for j in range(nb):
            S = s0_ref[0, j, h]
            kc = jnp.broadcast_to(kT[:, j:j + 1], (GDN_DK, GDN_DK))
            qcb = jnp.broadcast_to(qT[:, j:j + 1], (GDN_DK, GDN_DK))
            kS = jnp.sum(kc * S, axis=0, keepdims=True)
            qS = jnp.sum(qcb * S, axis=0, keepdims=True)
            egj = eg[j:j + 1, h:h + 1]
            bj = beta[j:j + 1, GDN_HEADS + h:GDN_HEADS + h + 1]
            v_new = bj * v[j:j + 1, :] - (bj * egj) * kS
            o = egj * qS + qk[j:j + 1, :] * v_new
            s1_ref[j, h] = S * egj + kc * v_new
            zh = z_ref[j:j + 1, h * GDN_DK:(h + 1) * GDN_DK]
            on = o * lax.rsqrt(jnp.mean(o * o, axis=-1, keepdims=True) + NORM_EPS) * nw_ref[...]
            o_ref[j:j + 1, h * GDN_DK:(h + 1) * GDN_DK] = on * _silu(zh)


def _gdn_step(qkv, z, ab, conv_w, alog, dtb, nw, conv0, s_all, l, prev):
    N = qkv.shape[0]
    nb = GDN_STEP_ROWS
    row = lambda w: pl.BlockSpec((nb, w), lambda i: (i, 0))
    const = lambda r, w: pl.BlockSpec((r, w), lambda i: (0, 0))
    sblk = (nb, GDN_HEADS, GDN_DK, GDN_DK)
    one = pl.BlockSpec(sblk, lambda i: (i, 0, 0, 0))
    ins = [qkv, z, ab, conv_w, alog, dtb, nw, conv0, s_all]
    in_specs = [row(QKV_WIDTH), row(GDN_WIDTH), row(LANES), const(GDN_CONV, QKV_WIDTH),
                const(1, LANES), const(1, LANES), const(1, GDN_DK), row(3 * QKV_WIDTH),
                pl.BlockSpec((1,) + sblk, lambda i: (l, i, 0, 0, 0))]
    if prev is None:
        s_shape, s_spec = jax.ShapeDtypeStruct((N,) + sblk[1:], F32), one
    else:
        assert DEPTH == 2 and l == 1
        ins.append(prev)
        in_specs.append(one)
        s_shape = jax.ShapeDtypeStruct((DEPTH, N) + sblk[1:], F32)
        s_spec = pl.BlockSpec((DEPTH,) + sblk, lambda i: (0, i, 0, 0, 0))
    return pl.pallas_call(
        _gdn_step_kernel,
        out_shape=(jax.ShapeDtypeStruct((N, GDN_WIDTH), F32), s_shape),
        grid=(N // nb,),
        in_specs=in_specs,
        out_specs=(row(GDN_WIDTH), s_spec),
        compiler_params=_cp(("parallel",)),
        name="gdn_step",
    )(*ins)


def _merge_kernel(yg_ref, og_ref, ga_ref, gb_ref, x_ref, gt_ref, wglu_ref, wgo_ref, wout_ref,
                  gf_ref, scf_ref, shf_ref, *rest, hi, chunked, routed):
    if routed:
        wr_ref, xo_ref, h_ref, lg_ref, *scr = rest
    else:
        xo_ref, h_ref, *scr = rest
    if chunked:
        y_scr = scr[-1]
        scr = scr[:-1]
        nrow = y_scr.shape[1] // S5_T
        for k in range(SLABS):
            for t in range(S5_T):
                y_scr[k, pl.ds(t, nrow, stride=S5_T), :] = (
                    yg_ref[k, 0, :, t * LANES:(t + 1) * LANES].astype(F32))
        y = jnp.concatenate([y_scr[k] for k in range(SLABS)], axis=-1)
    else:
        y = jnp.concatenate([yg_ref[k, 0] for k in range(SLABS)], axis=-1)
    if hi:
        wglu, wgo, wout = wglu_ref[0], wgo_ref[0], wout_ref[0]
        mm = _dot3
    else:
        wglu_s, wgo_s, wout_s = scr

        @pl.when((pl.program_id(0) == 0) & (pl.program_id(1) == 0))
        def _():
            wglu_s[...] = wglu_ref[0].astype(BF16)
            wgo_s[...] = wgo_ref[0].astype(BF16)
            wout_s[...] = wout_ref[0].astype(BF16)

        wglu, wgo, wout = wglu_s[...], wgo_s[...], wout_s[...]
        mm = lambda a, w: _dot(a.astype(BF16), w)

    glu = mm(y, wglu)
    branch_a = glu[:, 0:D_MODEL] * jax.nn.sigmoid(glu[:, D_MODEL:])
    branch_b = mm(og_ref[0], wgo)
    merged = ga_ref[0].astype(F32) * branch_a + gb_ref[0].astype(F32) * branch_b
    out = mm(merged, wout)
    x = x_ref[0] + gt_ref[0, 0] * out
    xo_ref[0] = x
    ms = jnp.mean(x * x, axis=-1, keepdims=True)
    h = x * lax.rsqrt(ms + NORM_EPS) * gf_ref[0]
    h = h * (1.0 + scf_ref[0, 0]) + shf_ref[0, 0]
    h_ref[0] = h.astype(h_ref.dtype)
    if routed:
        lg_ref[0] = _dot_nt(wr_ref[0], h, HI)


def _merge(yg, og, ga, gb, x, mod, wglu, wgo, wout, gf, wr, *, l, tm, hi, chunked, h_dtype):
    B, L, D = x.shape
    routed = l % 2 == 1
    row = lambda w: pl.BlockSpec((1, tm, w), lambda b, i: (b, i, 0))
    layer = lambda r, w, ll=l: pl.BlockSpec((1, r, w), lambda b, i: (ll, 0, 0))
    scratch = [] if hi else [pltpu.VMEM((S5_WIDTH, 2 * D), BF16), pltpu.VMEM((GDN_WIDTH, D), BF16),
                             pltpu.VMEM((D, D), BF16)]
    if chunked:
        scratch = scratch + [pltpu.VMEM((SLABS, tm, LANES), F32)]
        yg_spec = pl.BlockSpec((SLABS, 1, tm // S5_T, S5_T * LANES), lambda b, i: (0, b, i, 0))
    else:
        yg_spec = pl.BlockSpec((SLABS, 1, tm, LANES), lambda b, i: (0, b, i, 0))
    lg_shape = jax.ShapeDtypeStruct((B, N_EXPERTS, L), F32)
    lg_spec = pl.BlockSpec((1, N_EXPERTS, tm), lambda b, i: (b, 0, i))
    outs = pl.pallas_call(
        functools.partial(_merge_kernel, hi=hi, chunked=chunked, routed=routed),
        out_shape=(jax.ShapeDtypeStruct((B, L, D), F32),
                   jax.ShapeDtypeStruct((B, L, D), h_dtype)) + ((lg_shape,) if routed else ()),
        grid=(B, L // tm),
        in_specs=[yg_spec,
                  row(GDN_WIDTH), row(D), row(D), row(D), _mod_spec(mod, l, 2, tm),
                  layer(S5_WIDTH, 2 * D), layer(GDN_WIDTH, D), layer(D, D),
                  layer(1, D), _mod_spec(mod, l, 4, tm), _mod_spec(mod, l, 3, tm)]
        + ([layer(N_EXPERTS, D, l // 2)] if routed else []),
        out_specs=(row(D), row(D)) + ((lg_spec,) if routed else ()),
        scratch_shapes=scratch,
        compiler_params=_cp(("arbitrary", "arbitrary")),
        name="merge_out_proj",
    )(yg, og, ga, gb, x, mod, wglu, wgo, wout, gf, mod, mod, *((wr,) if routed else ()))
    return outs if routed else (*outs, None)


FF_TILE = 512
FFN_ROWS = 512


def _finish(x, gfin_ref, final):
    if not final:
        return x
    ms = jnp.mean(x * x, axis=-1, keepdims=True)
    return x * lax.rsqrt(ms + NORM_EPS) * gfin_ref[...]


def _ffn_kernel(h_ref, x_ref, gt_ref, wg_ref, wu_ref, wd_ref, gfin_ref, o_ref, acc_scr, *, hi, final):
    j = pl.program_id(2)
    if hi:
        wg, wu, wd = wg_ref[...], wu_ref[...], wd_ref[...]
        mm = _dot3
    else:
        wg, wu, wd = wg_ref[...].astype(BF16), wu_ref[...].astype(BF16), wd_ref[...].astype(BF16)
        mm = lambda a, w: _dot(a.astype(BF16), w)

    @pl.when(j == 0)
    def _():
        acc_scr[...] = jnp.zeros_like(acc_scr)

    tm = acc_scr.shape[0]
    sub = min(FFN_ROWS, tm)
    for s in range(tm // sub):
        rows = slice(s * sub, (s + 1) * sub)
        hb = h_ref[0, rows, :]
        act = _silu(mm(hb, wg)) * mm(hb, wu)
        acc_scr[rows, :] = acc_scr[rows, :] + mm(act, wd)

    @pl.when(j == pl.num_programs(2) - 1)
    def _():
        o_ref[0] = _finish(x_ref[0] + gt_ref[0, 0] * acc_scr[...], gfin_ref, final)


def _ffn(h, x, mod, w_gu, w_down, gfin, *, l, tm, hi, final):
    B, L, D = x.shape
    nj = D_FF // FF_TILE
    row = pl.BlockSpec((1, tm, D), lambda b, i, j: (b, i, 0))
    return pl.pallas_call(
        functools.partial(_ffn_kernel, hi=hi, final=final),
        out_shape=jax.ShapeDtypeStruct((B, L, D), F32),
        grid=(B, L // tm, nj),
        in_specs=[row, row, _mod_spec(mod, l, 5, tm),
                  pl.BlockSpec((D, FF_TILE), lambda b, i, j: (0, j)),
                  pl.BlockSpec((D, FF_TILE), lambda b, i, j: (0, nj + j)),
                  pl.BlockSpec((FF_TILE, D), lambda b, i, j: (j, 0)),
                  pl.BlockSpec((1, D), lambda b, i, j: (0, 0))],
        out_specs=row,
        scratch_shapes=[pltpu.VMEM((tm, D), F32)],
        compiler_params=_cp(("parallel", "parallel", "arbitrary")),
        name="ffn_dense",
    )(h, x, mod, w_gu, w_gu, w_down, gfin)


ROUTE_TM = 512
ROW_DMA_TM = 512
MOE_SUP = 2048
MOE_SUB = 512
MOE_FF_TILE = 512


def _route_kernel(lg_ref, br_ref, cnt0_ref, slot_ref, wt_ref, cnt_ref, carry_scr, *, cap):
    @pl.when((pl.program_id(0) == 0) & (pl.program_id(1) == 0))
    def _():
        carry_scr[...] = cnt0_ref[...]

    lg = lg_ref[0] + br_ref[...]
    tm = lg.shape[1]
    eidx = lax.broadcasted_iota(jnp.int32, lg.shape, 0)
    m1 = jnp.max(lg, axis=0, keepdims=True)
    i1 = jnp.min(jnp.where(lg == m1, eidx, N_EXPERTS), axis=0, keepdims=True)
    lg2 = jnp.where(eidx == i1, -jnp.inf, lg)
    m2 = jnp.max(lg2, axis=0, keepdims=True)
    i2 = jnp.min(jnp.where(lg2 == m2, eidx, N_EXPERTS), axis=0, keepdims=True)
    e2 = jnp.exp(m2 - m1)
    wt_ref[0, 0:1, :] = 1.0 / (1.0 + e2)
    wt_ref[0, 1:2, :] = e2 / (1.0 + e2)
    sel1 = eidx == i1
    sel2 = eidx == i2
    oh = jnp.where(sel1 | sel2, 1.0, 0.0)
    before = (lax.broadcasted_iota(jnp.int32, (tm, tm), 0)
              < lax.broadcasted_iota(jnp.int32, (tm, tm), 1)).astype(BF16)
    rank = carry_scr[:, 0:1] + _dot(oh.astype(BF16), before)
    r1 = jnp.sum(jnp.where(sel1, rank, 0.0), axis=0, keepdims=True).astype(jnp.int32)
    r2 = jnp.sum(jnp.where(sel2, rank, 0.0), axis=0, keepdims=True).astype(jnp.int32)
    slot_ref[0, 0:1, :] = i1 * cap + r1
    slot_ref[0, 1:2, :] = i2 * cap + r2
    carry_scr[...] = carry_scr[...] + jnp.sum(oh, axis=1, keepdims=True)
    cnt_ref[...] = carry_scr[...]


def _route_slots(lgT, b_r, cnt0, cap):
    B, E, L = lgT.shape
    tm = min(ROUTE_TM, L)
    return pl.pallas_call(
        functools.partial(_route_kernel, cap=cap),
        out_shape=(jax.ShapeDtypeStruct((B, 2, L), jnp.int32),
                   jax.ShapeDtypeStruct((B, 2, L), F32),
                   jax.ShapeDtypeStruct((E, LANES), F32)),
        grid=(B, L // tm),
        in_specs=[pl.BlockSpec((1, E, tm), lambda b, i: (b, 0, i)),
                  pl.BlockSpec((E, 1), lambda b, i: (0, 0)),
                  pl.BlockSpec((E, LANES), lambda b, i: (0, 0))],
        out_specs=(pl.BlockSpec((1, 2, tm), lambda b, i: (b, 0, i)),
                   pl.BlockSpec((1, 2, tm), lambda b, i: (b, 0, i)),
                   pl.BlockSpec((E, LANES), lambda b, i: (0, 0))),
        scratch_shapes=[pltpu.VMEM((E, LANES), F32)],
        compiler_params=_cp(("arbitrary", "arbitrary")),
        name="moe_route",
    )(lgT, b_r.reshape(E, 1), cnt0)


def _row_copy(src, dst, sem):
    return pltpu.make_async_copy(src, dst, sem)


def _slot_rows_kernel(start_ref, per_ref, code_ref, row_ref, *, cap):
    code = code_ref[...]
    shift = cap.bit_length() - 1
    e = lax.shift_right_logical(code, shift)
    r = code & (cap - 1)
    start = jnp.zeros_like(code)
    per = jnp.ones_like(code)
    for k in range(N_EXPERTS):
        start = jnp.where(e == k, start_ref[k], start)
        per = jnp.where(e == k, per_ref[k], per)
    q = jnp.floor((r.astype(F32) + 0.5) / per.astype(F32)).astype(jnp.int32)
    row_ref[...] = start + q * MOE_SUP + (r - q * per)


def _slot_rows(start, per, codes, cap):
    B, _, L = codes.shape
    tm = min(ROUTE_TM, L)
    spec = pl.BlockSpec((1, 2, tm), lambda b, i, st, pe: (b, 0, i))
    return pl.pallas_call(
        functools.partial(_slot_rows_kernel, cap=cap),
        out_shape=jax.ShapeDtypeStruct(codes.shape, jnp.int32),
        grid_spec=pltpu.PrefetchScalarGridSpec(
            num_scalar_prefetch=2, grid=(B, L // tm), in_specs=[spec], out_specs=spec),
        compiler_params=_cp(("parallel", "parallel")),
        name="moe_slot_rows",
    )(start, per, codes)


def _zeros_kernel(o_ref):
    o_ref[...] = jnp.zeros_like(o_ref)


def _zero_rows(n_rows, width):
    return pl.pallas_call(
        _zeros_kernel,
        out_shape=jax.ShapeDtypeStruct((n_rows, width), F32),
        grid=(n_rows // MOE_SUP,),
        out_specs=pl.BlockSpec((MOE_SUP, width), lambda i: (i, 0)),
        compiler_params=_cp(("parallel",)),
        name="moe_zero_rows",
    )()


def _dispatch_kernel(row_ref, h_ref, xs_in_ref, xs_ref, hbuf, sem):
    del xs_in_ref
    tm = h_ref.shape[1]
    t = pl.program_id(0) * pl.num_programs(1) + pl.program_id(1)
    last = pl.num_programs(0) * pl.num_programs(1) - 1
    slot = t % 2
    hbuf[slot] = h_ref[0]

    def issue(r, _):
        for k in range(2):
            row = row_ref[0, k, r]
            _row_copy(hbuf.at[slot, pl.ds(r, 1), :], xs_ref.at[pl.ds(row, 1), :], sem.at[slot]).start()
        return 0

    lax.fori_loop(0, tm, issue, 0, unroll=8)

    def drain(sl):
        for k in range(2):
            _row_copy(hbuf.at[sl], xs_ref.at[pl.ds(0, tm), :], sem.at[sl]).wait()

    @pl.when(t > 0)
    def _():
        drain(1 - slot)

    @pl.when(t == last)
    def _():
        drain(slot)


def _dispatch(rows, h, xs):
    B, L, D = h.shape
    n_rows = xs.shape[0]
    tm = min(ROW_DMA_TM, L)
    return pl.pallas_call(
        _dispatch_kernel,
        out_shape=jax.ShapeDtypeStruct((n_rows, D), F32),
        grid=(B, L // tm),
        in_specs=[pl.BlockSpec((1, 2, tm), lambda b, i: (b, 0, i), memory_space=pltpu.SMEM),
                  pl.BlockSpec((1, tm, D), lambda b, i: (b, i, 0)),
                  pl.BlockSpec(memory_space=pl.ANY)],
        out_specs=pl.BlockSpec(memory_space=pl.ANY),
        scratch_shapes=[pltpu.VMEM((2, tm, D), F32), pltpu.SemaphoreType.DMA((2,))],
        input_output_aliases={2: 0},
        compiler_params=_cp(("arbitrary", "arbitrary")),
        name="moe_dispatch",
    )(rows, h, xs)


def _moe_grp_kernel(ge_ref, gn_ref, x_ref, wg_ref, wu_ref, wd_ref, y_ref, xb_scr):
    g = pl.program_id(0)
    j = pl.program_id(1)
    nsub = gn_ref[g]
    wg = wg_ref[0].astype(BF16)
    wu = wu_ref[0].astype(BF16)
    wd = wd_ref[0].astype(BF16)
    nblk = MOE_SUP // MOE_SUB

    @pl.when(j == 0)
    def _():
        xb_scr[...] = x_ref[...].astype(BF16)
        y_ref[...] = jnp.zeros_like(y_ref)

    def block(s):
        rows = slice(s * MOE_SUB, (s + 1) * MOE_SUB)
        xb = xb_scr[rows, :]
        act = _silu(_dot(xb, wg)) * _dot(xb, wu)
        y_ref[rows, :] = y_ref[rows, :] + _dot(act.astype(BF16), wd)

    for n in range(1, nblk + 1):
        @pl.when(nsub == n)
        def _():
            for s in range(n):
                block(s)


def _moe_groups(counts, n_groups):
    nsup = (counts + MOE_SUP - 1) // MOE_SUP
    div = jnp.maximum(nsup, 1)
    per = jnp.maximum(((counts + div - 1) // div + MOE_SUB - 1) // MOE_SUB * MOE_SUB, MOE_SUB)
    ends = jnp.cumsum(nsup)
    first = ends - nsup
    total = ends[-1]
    g = jnp.arange(n_groups, dtype=jnp.int32)
    gc = jnp.minimum(g, total - 1)
    e_of = jnp.minimum(jnp.sum((gc[:, None] >= ends[None, :]).astype(jnp.int32), axis=1), N_EXPERTS - 1)
    left = jnp.minimum(counts[e_of] - (gc - first[e_of]) * per[e_of], per[e_of])
    nsub = jnp.clip((left + MOE_SUB - 1) // MOE_SUB, 0, MOE_SUP // MOE_SUB)
    gn = jnp.where(g < total, nsub, 0).astype(jnp.int32)
    return e_of, gn, (first * MOE_SUP).astype(jnp.int32), per.astype(jnp.int32)


def _moe_grouped(xs, ge, gn, w_gu, w_down):
    D = xs.shape[1]
    nj = D_FF // MOE_FF_TILE
    ng = xs.shape[0] // MOE_SUP
    jj = lambda j, gn, g: jnp.where(gn[g] > 0, j, nj - 1)
    return pl.pallas_call(
        _moe_grp_kernel,
        out_shape=jax.ShapeDtypeStruct(xs.shape, F32),
        grid_spec=pltpu.PrefetchScalarGridSpec(
            num_scalar_prefetch=2,
            grid=(ng, nj),
            in_specs=[pl.BlockSpec((MOE_SUP, D), lambda g, j, ge, gn: (g, 0)),
                      pl.BlockSpec((1, D, MOE_FF_TILE), lambda g, j, ge, gn: (ge[g], 0, jj(j, gn, g))),
                      pl.BlockSpec((1, D, MOE_FF_TILE), lambda g, j, ge, gn: (ge[g], 0, nj + jj(j, gn, g))),
                      pl.BlockSpec((1, MOE_FF_TILE, D), lambda g, j, ge, gn: (ge[g], jj(j, gn, g), 0))],
            out_specs=pl.BlockSpec((MOE_SUP, D), lambda g, j, ge, gn: (g, 0)),
            scratch_shapes=[pltpu.VMEM((MOE_SUP, D), BF16)],
        ),
        compiler_params=_cp(("arbitrary", "arbitrary")),
        name="moe_experts",
    )(ge, gn, xs, w_gu, w_gu, w_down)


def _combine_kernel(row_ref, next_ref, w_ref, x_ref, gt_ref, gfin_ref, ys_ref, o_ref, g_scr, sem, *, final):
    tm = x_ref.shape[1]
    t = pl.program_id(0) * pl.num_programs(1) + pl.program_id(1)
    last = pl.num_programs(0) * pl.num_programs(1) - 1
    slot = t % 2

    def gather(rows, sl):
        def issue(r, _):
            for k in range(2):
                row = rows[0, k, r]
                _row_copy(ys_ref.at[pl.ds(row, 1), :], g_scr.at[sl, k, pl.ds(r, 1), :], sem.at[sl]).start()
            return 0

        lax.fori_loop(0, tm, issue, 0, unroll=8)

    @pl.when(t == 0)
    def _():
        gather(row_ref, slot)

    @pl.when(t < last)
    def _():
        gather(next_ref, 1 - slot)

    for k in range(2):
        _row_copy(ys_ref.at[pl.ds(0, tm), :], g_scr.at[slot, k], sem.at[slot]).wait()
    w = w_ref[0]
    f = w[:, 0:1] * g_scr[slot, 0] + w[:, 1:2] * g_scr[slot, 1]
    o_ref[0] = _finish(x_ref[0] + gt_ref[0, 0] * f, gfin_ref, final)


def _combine(rows, wts, x, mod, gfin, ys, *, l, final):
    B, L, D = x.shape
    tm = min(ROW_DMA_TM, L)
    gt_spec = _mod_spec(mod, l, 5, tm)
    row = pl.BlockSpec((1, tm, D), lambda b, i: (b, i, 0))
    nl = L // tm

    def next_block(b, i):
        t1 = jnp.minimum(b * nl + i + 1, B * nl - 1)
        return (t1 // nl, 0, t1 % nl)

    return pl.pallas_call(
        functools.partial(_combine_kernel, final=final),
        out_shape=jax.ShapeDtypeStruct((B, L, D), F32),
        grid=(B, L // tm),
        in_specs=[pl.BlockSpec((1, 2, tm), lambda b, i: (b, 0, i), memory_space=pltpu.SMEM),
                  pl.BlockSpec((1, 2, tm), next_block, memory_space=pltpu.SMEM),
                  pl.BlockSpec((1, tm, 2), lambda b, i: (b, i, 0)),
                  row, gt_spec,
                  pl.BlockSpec((1, D), lambda b, i: (0, 0)),
                  pl.BlockSpec(memory_space=pl.ANY)],
        out_specs=row,
        scratch_shapes=[pltpu.VMEM((2, 2, tm, D), F32), pltpu.SemaphoreType.DMA((2,))],
        compiler_params=_cp(("arbitrary", "arbitrary")),
        name="moe_combine",
    )(rows, rows, wts.transpose(0, 2, 1), x, mod, gfin, ys)


def _moe_routed(groups, b_r, w_gu, w_down, gfin, *, l, final):
    D = groups[0][1].shape[-1]
    n_tok = sum(g[1].shape[0] * g[1].shape[1] for g in groups)
    cap = 1 << (n_tok - 1).bit_length()
    n_groups = 2 * n_tok // MOE_SUP + N_EXPERTS
    cnt = jnp.zeros((N_EXPERTS, LANES), F32)
    routed = []
    for _, _, _, lgT in groups:
        codes, wts, cnt = _route_slots(lgT, b_r, cnt, cap)
        routed.append((codes, wts))
    ge, gn, start, per = _moe_groups(cnt[:, 0].astype(jnp.int32), n_groups)
    rows = [_slot_rows(start, per, codes, cap) for codes, _ in routed]
    xs = _zero_rows(n_groups * MOE_SUP, D)
    for (h, _, _, _), r in zip(groups, rows):
        xs = _dispatch(r, h, xs)
    ys = _moe_grouped(xs, ge, gn, w_gu, w_down)
    return [_combine(r, wts, x, mod, gfin, ys, l=l, final=final)
            for (_, x, mod, _), (_, wts), r in zip(groups, routed, rows)]


def _pad_lanes(v):
    return jnp.pad(v.reshape(1, -1), ((0, 0), (0, LANES - v.shape[-1])))


def _mixer_layer(x, mod, states, p, s5m, l, prev_sg, *, seq):
    B, L, D = x.shape
    hi = not seq
    s5r0, s5i0, sg0, sc0 = states
    w_in, w_gates, w_ab = p['w_in_seq' if seq else 'w_in']
    if seq:
        u, qkv, z, ga, gb, ab = _proj_seq(x, p['g_mix'], mod, w_in, w_gates, w_ab, l=l, tm=min(1024, L))
    else:
        u, qkv, z, ga, gb, ab = _proj(x, p['g_mix'], mod, w_in, w_gates, w_ab, l=l, tm=L)
    alog = _pad_lanes(p['gdn_a_log'][l])
    dtb = _pad_lanes(p['gdn_dt_bias'][l])
    nw = p['gdn_norm_w'][l].reshape(1, GDN_DK)
    if seq:
        yg, sfin = _s5_seq(u, s5m['be'], s5m['tp'], s5m['cpm'], s5m['pt'],
                           jnp.zeros((SLABS, B, 1, 2 * SLAB_STATE), F32), s5m['dsk'][l], l)
        sfin = sfin.reshape(SLABS, B, 2, SLAB_STATE).transpose(2, 1, 0, 3)
        sr = sfin[0].reshape(B, S5_GROUPS, S5_STATE)
        si = sfin[1].reshape(B, S5_GROUPS, S5_STATE)
        og, sg = _gdn_seq(qkv, z, ab, p['gdn_conv_w'], alog, dtb, nw,
                          jnp.zeros((B, GDN_CONV - 1, QKV_WIDTH), F32),
                          jnp.zeros((B, GDN_HEADS, GDN_DK, GDN_DK), F32), l)
        cb = qkv[:, L - (GDN_CONV - 1):, :].astype(F32)
    else:
        n = L
        s0 = jnp.concatenate([s5r0[l].reshape(n, SLABS, SLAB_STATE),
                              s5i0[l].reshape(n, SLABS, SLAB_STATE)], axis=-1).transpose(1, 0, 2)
        yg, s1 = _s5_step(u.reshape(SLABS, n, LANES), s5m['bst'], s5m['c0'], s5m['a1'],
                          s0, s5m['d1'][l], l)
        yg = yg.reshape(SLABS, 1, n, LANES)
        s1 = s1.transpose(1, 0, 2)
        sr = s1[:, :, :SLAB_STATE].reshape(n, S5_GROUPS, S5_STATE)
        si = s1[:, :, SLAB_STATE:].reshape(n, S5_GROUPS, S5_STATE)
        og, sg = _gdn_step(qkv.reshape(n, QKV_WIDTH), z.reshape(n, GDN_WIDTH), ab.reshape(n, LANES),
                           p['gdn_conv_w'][l], alog, dtb, nw,
                           sc0[l].reshape(n, (GDN_CONV - 1) * QKV_WIDTH), sg0, l, prev_sg)
        og = og.reshape(1, n, GDN_WIDTH)
        cb = jnp.concatenate([sc0[l][:, 1:, :], qkv.reshape(n, 1, QKV_WIDTH)], axis=1)
    x, h, lgT = _merge(yg, og, ga, gb, x, mod, p['w_s5_glu'], p['w_gdn_out'], p['w_out'],
                       p['g_ffn'], p['w_router'], l=l, tm=min(512, L), hi=hi, chunked=seq,
                       h_dtype=BF16 if (seq and l % 2 == 0) else F32)
    return x, h, lgT, (sr, si, sg, cb)


def kernel(x_prompt, x_sample, c_prompt, c_sample, state_s5_re, state_s5_im, state_gdn, state_conv,
           g_mix, g_ffn, g_final, w_ada, b_ada, w_in, s5_lambda_re, s5_lambda_im, s5_log_dt,
           s5_b_re, s5_b_im, s5_c_re, s5_c_im, s5_d, w_s5_glu, gdn_conv_w, gdn_a_log, gdn_dt_bias,
           gdn_norm_w, w_gdn_out, w_out, w_ffn_gate_up, w_ffn_down, w_router, b_router,
           w_exp_gate_up, w_exp_down):
    def in_proj_parts(w):
        return w, w[:, :, 2568:], jnp.pad(w[:, :, 2560:2568], ((0, 0), (0, 0), (0, LANES - 8)))

    D_ = x_prompt.shape[-1]
    p = dict(g_mix=g_mix.reshape(DEPTH, 1, D_), g_ffn=g_ffn.reshape(DEPTH, 1, D_), w_s5_glu=w_s5_glu,
             gdn_conv_w=gdn_conv_w, gdn_a_log=gdn_a_log, gdn_dt_bias=gdn_dt_bias,
             gdn_norm_w=gdn_norm_w, w_gdn_out=w_gdn_out, w_out=w_out,
             w_router=w_router.transpose(0, 2, 1), w_in=in_proj_parts(w_in),
             w_in_seq=in_proj_parts(w_in.astype(BF16)))
    nbp, L, D = x_prompt.shape
    nbs = x_sample.shape[0]

    mod = _ada(jnp.concatenate([c_prompt, c_sample], axis=0), w_ada, b_ada)
    mod_p = mod[:, :nbp].reshape(DEPTH, nbp, 1, 6 * D)
    mod_s = mod[:, nbp:].reshape(DEPTH, 1, nbs, 6 * D)

    seg = L // S5_T // 8
    be, bst, c0, cpm, tp, pt, a1 = _s5_prep(s5_lambda_re, s5_lambda_im, s5_log_dt, s5_b_re, s5_b_im,
                                            s5_c_re, s5_c_im, seg)
    d1 = [s5_d[l].reshape(SLABS, 1, LANES) for l in range(DEPTH)]
    s5m = dict(be=be, bst=bst, c0=c0, cpm=cpm, pt=pt, a1=a1, tp=tp, d1=d1,
               dsk=[jnp.tile(d, (1, 1, S5_T)) for d in d1])

    xs_ = [x_prompt, x_sample.reshape(1, nbs, D)]
    mods = [mod_p, mod_s]
    states = [(None, None, None, None), (state_s5_re, state_s5_im, state_gdn, state_conv)]
    outs = [[], []]
    gfin = g_final.reshape(1, D)
    for l in range(DEPTH):
        final = l == DEPTH - 1
        mixed = []
        for gi, seq in enumerate((True, False)):
            prev_sg = outs[gi][0][2] if (not seq and final and DEPTH == 2) else None
            x, h, lgT, st = _mixer_layer(xs_[gi], mods[gi], states[gi], p, s5m, l, prev_sg, seq=seq)
            outs[gi].append(st)
            mixed.append((h, x, mods[gi], lgT))
        if l % 2 == 0:
            wgu, wdn = w_ffn_gate_up[l // 2], w_ffn_down[l // 2]
            xs_ = [_ffn(h, x, mod_g, wgu if gi else wgu.astype(BF16), wdn if gi else wdn.astype(BF16), gfin,
                        l=l, tm=min(1024, x.shape[1]), hi=(gi == 1), final=final)
                   for gi, (h, x, mod_g, _) in enumerate(mixed)]
        else:
            xs_ = _moe_routed(mixed, b_router[l // 2], w_exp_gate_up[l // 2], w_exp_down[l // 2], gfin,
                              l=l, final=final)
    y_p, y_s = xs_
    st_p = [jnp.stack([o[i] for o in outs[0]]) for i in range(4)]
    st_s = [outs[1][-1][2] if (i == 2 and DEPTH == 2) else jnp.stack([o[i] for o in outs[1]])
            for i in range(4)]
    return (y_p, y_s.reshape(nbs, 1, D), st_p[0], st_p[1], st_p[2], st_p[3],
            st_s[0], st_s[1], st_s[2], st_s[3])
```

```python
import functools

import jax
import jax.numpy as jnp
from jax import lax
from jax.experimental import pallas as pl
from jax.experimental.pallas import tpu as pltpu

F32 = jnp.float32
BF16 = jnp.bfloat16
HI = lax.Precision.HIGHEST

D_MODEL = 1024
DEPTH = 2
S5_WIDTH = 512
S5_GROUP = 16
S5_GROUPS = 32
S5_STATE = 64
GDN_HEADS = 4
GDN_DK = 128
GDN_WIDTH = 512
GDN_CONV = 4
QKV_WIDTH = 1536
D_FF = 3584
N_EXPERTS = 8
NORM_EPS = 1e-6
L2_EPS = 1e-6

LANES = 128
SLABS = S5_WIDTH // LANES
SLAB_STATE = (S5_GROUPS // SLABS) * S5_STATE
S5_T = 8
S5_SEG_PAD = 4
GDN_C = 128
GDN_BLOCK = 512
GDN_SUB = 256
VMEM_LIMIT = 56 * 1024 * 1024


def _cp(sem, vmem=VMEM_LIMIT):
    return pltpu.CompilerParams(dimension_semantics=sem, vmem_limit_bytes=vmem)


def _dot(a, b, prec=None):
    return jnp.dot(a, b, precision=prec, preferred_element_type=F32)


def _dotb(a, b):
    return jnp.dot(a.astype(BF16), b.astype(BF16), preferred_element_type=F32)


def _dot_nt(a, b, prec=None):
    return lax.dot_general(a, b, (((1,), (1,)), ((), ())), precision=prec,
                           preferred_element_type=F32)


def _dot_tn(a, b, prec=None):
    return lax.dot_general(a, b, (((0,), (0,)), ((), ())), precision=prec,
                           preferred_element_type=F32)


def _dot3(a, b):
    ah, al = _split_bf16(a)
    bh, bl = _split_bf16(b)
    return _dot(ah, bh) + (_dot(ah, bl) + _dot(al, bh))


def _silu(x):
    return x * jax.nn.sigmoid(x)


def _ada_kernel(c_ref, w_ref, b_ref, o_ref):
    cs = _silu(c_ref[...])
    o_ref[0] = _dot3(cs, w_ref[0]) + b_ref[0]


def _ada(c_all, w_ada, b_ada):
    n = c_all.shape[0]
    tn = 1536
    return pl.pallas_call(
        _ada_kernel,
        out_shape=jax.ShapeDtypeStruct((DEPTH, n, 6 * D_MODEL), F32),
        grid=(DEPTH, 6 * D_MODEL // tn),
        in_specs=[pl.BlockSpec((n, D_MODEL), lambda l, j: (0, 0)),
                  pl.BlockSpec((1, D_MODEL, tn), lambda l, j: (l, 0, j)),
                  pl.BlockSpec((1, 1, tn), lambda l, j: (l, 0, j))],
        out_specs=pl.BlockSpec((1, n, tn), lambda l, j: (l, 0, j)),
        compiler_params=_cp(("parallel", "parallel")),
        name="ada_mod",
    )(c_all, w_ada, b_ada.reshape(DEPTH, 1, 6 * D_MODEL))


def _proj_kernel(x_ref, g_ref, sc_ref, sh_ref, w_ref, wg_ref, wab_ref,
                 u_ref, qkv_ref, z_ref, ga_ref, gb_ref, ab_ref, h_scr):
    j = pl.program_id(2)

    @pl.when(j == 0)
    def _():
        x = x_ref[0]
        ms = jnp.mean(x * x, axis=-1, keepdims=True)
        xn = x * lax.rsqrt(ms + NORM_EPS) * g_ref[0]
        h_scr[...] = (xn * (1.0 + sc_ref[0, 0]) + sh_ref[0, 0]).astype(h_scr.dtype)

    def mm(w):
        return _dot3(h_scr[...], w)

    @pl.when(j == 0)
    def _():
        res = mm(w_ref[0])
        for k in range(SLABS):
            u_ref[k, 0] = res[:, k * LANES:(k + 1) * LANES]

    @pl.when((j >= 1) & (j <= 3))
    def _():
        qkv_ref[0] = mm(w_ref[0])

    @pl.when(j == 4)
    def _():
        z_ref[0] = mm(w_ref[0])

    @pl.when((j == 5) | (j == 6))
    def _():
        ga_ref[0] = jax.nn.sigmoid(mm(wg_ref[0]))

    @pl.when((j == 7) | (j == 8))
    def _():
        gb_ref[0] = jax.nn.sigmoid(mm(wg_ref[0]))

    @pl.when(j == 9)
    def _():
        ab_ref[0] = mm(wab_ref[0])


def _mod_spec(mod, l, chunk, tm):
    per_row = mod.shape[2] != 1
    D = mod.shape[3] // 6

    def index(b, i, *_):
        return (l, b, i if per_row else 0, chunk)

    return pl.BlockSpec((1, 1, tm if per_row else 1, D), index)


def _proj(x, g, mod, w_in, w_gates, w_ab, *, l, tm):
    B, L, D = x.shape
    tn = 512
    clampi = lambda j, lo, n: jnp.clip(j - lo, 0, n - 1)
    outs = pl.pallas_call(
        _proj_kernel,
        out_shape=(jax.ShapeDtypeStruct((SLABS, B, L, LANES), F32),
                   jax.ShapeDtypeStruct((B, L, QKV_WIDTH), F32),
                   jax.ShapeDtypeStruct((B, L, GDN_WIDTH), F32),
                   jax.ShapeDtypeStruct((B, L, D), F32),
                   jax.ShapeDtypeStruct((B, L, D), F32),
                   jax.ShapeDtypeStruct((B, L, LANES), F32)),
        grid=(B, L // tm, 10),
        in_specs=[pl.BlockSpec((1, tm, D), lambda b, i, j: (b, i, 0)),
                  pl.BlockSpec((1, 1, D), lambda b, i, j: (l, 0, 0)),
                  _mod_spec(mod, l, 1, tm),
                  _mod_spec(mod, l, 0, tm),
                  pl.BlockSpec((1, D, tn), lambda b, i, j: (l, 0, jnp.minimum(j, 4))),
                  pl.BlockSpec((1, D, tn), lambda b, i, j: (l, 0, clampi(j, 5, 4))),
                  pl.BlockSpec((1, D, LANES), lambda b, i, j: (l, 0, 0))],
        out_specs=(pl.BlockSpec((SLABS, 1, tm, LANES), lambda b, i, j: (0, b, i, 0)),
                   pl.BlockSpec((1, tm, tn), lambda b, i, j: (b, i, clampi(j, 1, 3))),
                   pl.BlockSpec((1, tm, tn), lambda b, i, j: (b, i, 0)),
                   pl.BlockSpec((1, tm, tn), lambda b, i, j: (b, i, clampi(j, 5, 2))),
                   pl.BlockSpec((1, tm, tn), lambda b, i, j: (b, i, clampi(j, 7, 2))),
                   pl.BlockSpec((1, tm, LANES), lambda b, i, j: (b, i, 0))),
        scratch_shapes=[pltpu.VMEM((tm, D), F32)],
        compiler_params=_cp(("parallel", "parallel", "arbitrary")),
        name="norm_in_proj",
    )(x, g, mod, mod, w_in, w_gates, w_ab)
    return outs


def _proj_seq_kernel(x_ref, g_ref, sc_ref, sh_ref, w_ref, wg_ref, wab_ref,
                     u_ref, qkv_ref, z_ref, ga_ref, gb_ref, ab_ref, us_scr):
    x = x_ref[0]
    ms = jnp.mean(x * x, axis=-1, keepdims=True)
    xn = x * lax.rsqrt(ms + NORM_EPS) * g_ref[0]
    h = (xn * (1.0 + sc_ref[0, 0]) + sh_ref[0, 0]).astype(BF16)
    res = _dot(h, w_ref[0, :, 0:S5_WIDTH])
    nrow = res.shape[0] // S5_T
    for k in range(SLABS):
        us_scr[...] = res[:, k * LANES:(k + 1) * LANES]
        for t in range(S5_T):
            u_ref[k, 0, :, t * LANES:(t + 1) * LANES] = (
                us_scr[pl.ds(t, nrow, stride=S5_T), :].astype(u_ref.dtype))
    c0 = S5_WIDTH
    qkv_ref[0] = _dot(h, w_ref[0, :, c0:c0 + QKV_WIDTH]).astype(qkv_ref.dtype)
    c0 += QKV_WIDTH
    z_ref[0] = _dot(h, w_ref[0, :, c0:c0 + GDN_WIDTH]).astype(z_ref.dtype)
    D = x.shape[-1]
    ga_ref[0] = jax.nn.sigmoid(_dot(h, wg_ref[0, :, 0:D])).astype(ga_ref.dtype)
    gb_ref[0] = jax.nn.sigmoid(_dot(h, wg_ref[0, :, D:2 * D])).astype(gb_ref.dtype)
    ab_ref[0] = _dot(h, wab_ref[0])


def _proj_seq(x, g, mod, w_in, w_gates, w_ab, *, l, tm):
    B, L, D = x.shape
    n_main = S5_WIDTH + QKV_WIDTH + GDN_WIDTH
    row = lambda w: pl.BlockSpec((1, tm, w), lambda b, i: (b, i, 0))
    return pl.pallas_call(
        _proj_seq_kernel,
        out_shape=(jax.ShapeDtypeStruct((SLABS, B, L // S5_T, S5_T * LANES), BF16),
                   jax.ShapeDtypeStruct((B, L, QKV_WIDTH), BF16),
                   jax.ShapeDtypeStruct((B, L, GDN_WIDTH), BF16),
                   jax.ShapeDtypeStruct((B, L, D), BF16),
                   jax.ShapeDtypeStruct((B, L, D), BF16),
                   jax.ShapeDtypeStruct((B, L, LANES), F32)),
        grid=(B, L // tm),
        in_specs=[row(D),
                  pl.BlockSpec((1, 1, D), lambda b, i: (l, 0, 0)),
                  _mod_spec(mod, l, 1, tm),
                  _mod_spec(mod, l, 0, tm),
                  pl.BlockSpec((1, D, n_main), lambda b, i: (l, 0, 0)),
                  pl.BlockSpec((1, D, 2 * D), lambda b, i: (l, 0, 0)),
                  pl.BlockSpec((1, D, LANES), lambda b, i: (l, 0, 0))],
        out_specs=(pl.BlockSpec((SLABS, 1, tm // S5_T, S5_T * LANES), lambda b, i: (0, b, i, 0)),
                   row(QKV_WIDTH), row(GDN_WIDTH), row(D), row(D), row(LANES)),
        scratch_shapes=[pltpu.VMEM((tm, LANES), F32)],
        compiler_params=_cp(("parallel", "parallel")),
        name="norm_in_proj_seq",
    )(x, g, mod, mod, w_in, w_gates, w_ab)


GROUPS_PER_SLAB = S5_GROUPS // SLABS


def _s5_prep_kernel(lrb, lib, dtb, bre, bim, lrc, lic, dtc, cre, cim, lrn, lin, dtn,
                    be_ref, bst_ref, c0_ref, cpm_ref, tp_ref, pt_ref, a1_ref, cpe_scr, *, seg):
    W = SLAB_STATE

    def disc(lr, li, ldt):
        dt = jnp.exp(ldt)
        mag = jnp.exp(lr * dt)
        return mag * jnp.cos(li * dt), mag * jnp.sin(li * dt)

    def cmul(xr, xi, yr, yi):
        return xr * yr - xi * yi, xr * yi + xi * yr

    lr, li = lrb[0], lib[0]
    ar, ai = disc(lr, li, dtb[0])
    den = lr * lr + li * li
    nr = ar - 1.0
    kr = (nr * lr + ai * li) / den
    ki = (ai * lr - nr * li) / den
    br, bi = bre[0], bim[0]
    bbr = kr * br - ki * bi
    bbi = kr * bi + ki * br
    rgrp = lax.broadcasted_iota(jnp.int32, (LANES, LANES), 0) // S5_GROUP
    lane_hi = lax.broadcasted_iota(jnp.int32, (LANES, LANES), 1) // S5_STATE
    pr, pi = jnp.ones_like(ar), jnp.zeros_like(ar)
    for d in range(S5_T):
        t = S5_T - 1 - d
        for ri, val in enumerate(cmul(pr, pi, bbr, bbi)):
            two = jnp.concatenate([val, val], axis=1)
            for m in range(GROUPS_PER_SLAB // 2):
                tile = jnp.where(rgrp == 2 * m + lane_hi, two, 0.0)
                c0 = ri * W + m * LANES
                be_ref[0, 0, t * LANES:(t + 1) * LANES, c0:c0 + LANES] = tile.astype(BF16)
                if d == 0:
                    bst_ref[0, 0, :, c0:c0 + LANES] = tile
        pr, pi = cmul(pr, pi, ar, ai)

    ar, ai = disc(lrc[0], lic[0], dtc[0])
    cr, ci = cre[0], cim[0]
    own = (lax.broadcasted_iota(jnp.int32, (W, LANES), 0) // S5_STATE
           == lax.broadcasted_iota(jnp.int32, (W, LANES), 1) // S5_GROUP)
    pr, pi = jnp.ones_like(ar), jnp.zeros_like(ar)
    for d in range(S5_T + 1):
        vr, vi = cmul(cr, ci, pr, pi)
        for ri, val in enumerate((vr, -vi)):
            tile = jnp.where(own, val, 0.0)
            cpe_scr[d, ri * W:(ri + 1) * W, :] = tile
            if d == 0:
                c0_ref[0, 0, ri * W:(ri + 1) * W, :] = tile
            if d >= 1:
                cpm_ref[0, 0, ri * W:(ri + 1) * W, (d - 1) * LANES:d * LANES] = tile.astype(BF16)
        pr, pi = cmul(pr, pi, ar, ai)

    bst = bst_ref[0, 0]
    lag = [_dot(bst, cpe_scr[d], HI).astype(BF16) for d in range(S5_T)]
    for dd in range(S5_T // 2):
        tp_ref[0, 0, dd, 0:LANES, 0:LANES] = lag[2 * dd]
        tp_ref[0, 0, dd, LANES:, LANES:] = lag[2 * dd]
        tp_ref[0, 0, dd, 0:LANES, LANES:] = lag[2 * dd + 1]
        tp_ref[0, 0, dd, LANES:, 0:LANES] = lag[2 * dd - 1] if dd else jnp.zeros((LANES, LANES), BF16)

    ar, ai = disc(lrn[0, 0], lin[0, 0], dtn[0, 0])
    a1_ref[0, 0, :, 0:W] = ar
    a1_ref[0, 0, :, W:2 * W] = ai
    tr, ti = ar, ai
    for _ in range(S5_T - 1):
        tr, ti = cmul(tr, ti, ar, ai)
    pr, pi = jnp.ones_like(ar), jnp.zeros_like(ar)
    for i in range(seg + 1):
        pt_ref[0, 0, i:i + 1, 0:W] = pr
        pt_ref[0, 0, i:i + 1, W:2 * W] = pi
        pr, pi = cmul(pr, pi, tr, ti)


def _s5_prep(lam_re, lam_im, log_dt, b_re, b_im, c_re, c_im, seg):
    G, P, C = S5_GROUPS, S5_STATE, S5_GROUP
    W2 = 2 * SLAB_STATE
    dt3 = jnp.broadcast_to(log_dt[:, :, None], (DEPTH, G, P))
    rows_b = lambda a: jnp.repeat(a, C, axis=1)
    bt = lambda a: a.transpose(0, 1, 3, 2).reshape(DEPTH, G * C, P)
    rows_c = lambda a: jnp.broadcast_to(a.reshape(DEPTH, G * P, 1), (DEPTH, G * P, LANES))
    ct = lambda a: jnp.tile(a.transpose(0, 1, 3, 2).reshape(DEPTH, G * P, C), (1, 1, LANES // C))
    nat = lambda a: a.reshape(DEPTH, SLABS, 1, SLAB_STATE)
    args = (rows_b(lam_re), rows_b(lam_im), rows_b(dt3), bt(b_re), bt(b_im),
            rows_c(lam_re), rows_c(lam_im), rows_c(dt3), ct(c_re), ct(c_im),
            nat(lam_re), nat(lam_im), nat(dt3))
    bspec = pl.BlockSpec((1, LANES, P), lambda l, k: (l, k, 0))
    cspec = pl.BlockSpec((1, SLAB_STATE, LANES), lambda l, k: (l, k, 0))
    nspec = pl.BlockSpec((1, 1, 1, SLAB_STATE), lambda l, k: (l, k, 0, 0))
    return pl.pallas_call(
        functools.partial(_s5_prep_kernel, seg=seg),
        out_shape=(jax.ShapeDtypeStruct((DEPTH, SLABS, S5_T * LANES, W2), BF16),
                   jax.ShapeDtypeStruct((DEPTH, SLABS, LANES, W2), F32),
                   jax.ShapeDtypeStruct((DEPTH, SLABS, W2, LANES), F32),
                   jax.ShapeDtypeStruct((DEPTH, SLABS, W2, S5_T * LANES), BF16),
                   jax.ShapeDtypeStruct((DEPTH, SLABS, S5_T // 2, 2 * LANES, 2 * LANES), BF16),
                   jax.ShapeDtypeStruct((DEPTH, SLABS, seg + 1, W2), F32),
                   jax.ShapeDtypeStruct((DEPTH, SLABS, 1, W2), F32)),
        grid=(DEPTH, SLABS),
        in_specs=[bspec] * 5 + [cspec] * 5 + [nspec] * 3,
        out_specs=(pl.BlockSpec((1, 1, S5_T * LANES, W2), lambda l, k: (l, k, 0, 0)),
                   pl.BlockSpec((1, 1, LANES, W2), lambda l, k: (l, k, 0, 0)),
                   pl.BlockSpec((1, 1, W2, LANES), lambda l, k: (l, k, 0, 0)),
                   pl.BlockSpec((1, 1, W2, S5_T * LANES), lambda l, k: (l, k, 0, 0)),
                   pl.BlockSpec((1, 1, S5_T // 2, 2 * LANES, 2 * LANES), lambda l, k: (l, k, 0, 0, 0)),
                   pl.BlockSpec((1, 1, seg + 1, W2), lambda l, k: (l, k, 0, 0)),
                   pl.BlockSpec((1, 1, 1, W2), lambda l, k: (l, k, 0, 0))),
        scratch_shapes=[pltpu.VMEM((S5_T + 1, W2, LANES), F32)],
        compiler_params=_cp(("parallel", "parallel")),
        name="s5_discretize",
    )(*args)


def _s5_seq_kernel(up_ref, be_ref, tp_ref, cpm_ref, pt_ref, s0_ref, dsk_ref,
                   yg_ref, sfin_ref, e_scr, sx_scr, *, nc):
    seg = nc // 8
    W = SLAB_STATE
    nt = W // LANES
    ub = up_ref[0, 0]
    u = ub.astype(F32)
    e = _dot(ub, be_ref[0, 0])
    pitch = e_scr.shape[1] // 8
    for c in range(2 * nt):
        for j in range(8):
            e_scr[c, j * pitch:j * pitch + seg, :] = e[j * seg:(j + 1) * seg, c * LANES:(c + 1) * LANES]

    def tiles(row):
        return [(row[:, c * LANES:(c + 1) * LANES], row[:, W + c * LANES:W + (c + 1) * LANES])
                for c in range(nt)]

    a8 = [(jnp.broadcast_to(r, (8, LANES)), jnp.broadcast_to(i, (8, LANES)))
          for r, i in tiles(pt_ref[0, 0, 1:2, :])]

    def step(i, carry):
        rows = pl.ds(i, 8, stride=pitch)
        new = []
        for c in range(nt):
            sr, si = carry[c]
            ar, ai = a8[c]
            sx_scr[c, rows, :] = sr
            sx_scr[nt + c, rows, :] = si
            new.append((ar * sr - ai * si + e_scr[c, rows, :],
                        ar * si + ai * sr + e_scr[nt + c, rows, :]))
        return tuple(new)

    zero = jnp.zeros((8, LANES), F32)
    ends = lax.fori_loop(0, seg, step, tuple((zero, zero) for _ in range(nt)))

    al = tiles(pt_ref[0, 0, seg:seg + 1, :])
    cur = tiles(s0_ref[0, 0])
    car = []
    for c in range(nt):
        alr, ali = al[c]
        cr, ci = cur[c]
        sr, si = ends[c]
        crs, cis = [], []
        for j in range(8):
            crs.append(cr)
            cis.append(ci)
            cr, ci = (alr * cr - ali * ci + sr[j:j + 1], alr * ci + ali * cr + si[j:j + 1])
        sfin_ref[0, 0, :, c * LANES:(c + 1) * LANES] = cr
        sfin_ref[0, 0, :, W + c * LANES:W + (c + 1) * LANES] = ci
        car.append((jnp.concatenate(crs, axis=0), jnp.concatenate(cis, axis=0)))

    def corr(i, _):
        rows = pl.ds(i, 8, stride=pitch)
        pw = tiles(pt_ref[0, 0, pl.ds(i, 1), :])
        for c in range(nt):
            pr, pi = pw[c]
            cr, ci = car[c]
            sx_scr[c, rows, :] = sx_scr[c, rows, :] + (pr * cr - pi * ci)
            sx_scr[nt + c, rows, :] = sx_scr[nt + c, rows, :] + (pr * ci + pi * cr)
        return 0

    lax.fori_loop(0, seg, corr, 0)

    sx = jnp.concatenate(
        [jnp.concatenate([sx_scr[c, j * pitch:j * pitch + seg, :] for j in range(8)], axis=0)
         for c in range(2 * nt)], axis=-1)
    y = _dot(sx.astype(BF16), cpm_ref[0, 0])
    TW = 2 * LANES
    for tq in range(S5_T // 2):
        acc = y[:, tq * TW:(tq + 1) * TW]
        for tpi in range(tq + 1):
            acc = acc + _dot(ub[:, tpi * TW:(tpi + 1) * TW], tp_ref[0, 0, tq - tpi])
        acc = acc + dsk_ref[0, :, tq * TW:(tq + 1) * TW] * u[:, tq * TW:(tq + 1) * TW]
        yg_ref[0, 0, :, tq * TW:(tq + 1) * TW] = jax.nn.gelu(acc).astype(yg_ref.dtype)


def _s5_seq(up, be_emb, tp, cpm, pt, s0, dsk, l):
    _, B, nc, _ = up.shape
    seg = nc // 8
    W2 = 2 * SLAB_STATE
    yg, sfin = pl.pallas_call(
        functools.partial(_s5_seq_kernel, nc=nc),
        out_shape=(jax.ShapeDtypeStruct((SLABS, B, nc, S5_T * LANES), BF16),
                   jax.ShapeDtypeStruct((SLABS, B, 1, W2), F32)),
        grid=(SLABS, B),
        in_specs=[pl.BlockSpec((1, 1, nc, S5_T * LANES), lambda k, b: (k, b, 0, 0)),
                  pl.BlockSpec((1, 1, S5_T * LANES, W2), lambda k, b: (l, k, 0, 0)),
                  pl.BlockSpec((1, 1, S5_T // 2, 2 * LANES, 2 * LANES), lambda k, b: (l, k, 0, 0, 0)),
                  pl.BlockSpec((1, 1, W2, S5_T * LANES), lambda k, b: (l, k, 0, 0)),
                  pl.BlockSpec((1, 1, seg + 1, W2), lambda k, b: (l, k, 0, 0)),
                  pl.BlockSpec((1, 1, 1, W2), lambda k, b: (k, b, 0, 0)),
                  pl.BlockSpec((1, 1, S5_T * LANES), lambda k, b: (k, 0, 0))],
        out_specs=(pl.BlockSpec((1, 1, nc, S5_T * LANES), lambda k, b: (k, b, 0, 0)),
                   pl.BlockSpec((1, 1, 1, W2), lambda k, b: (k, b, 0, 0))),
        scratch_shapes=[pltpu.VMEM((W2 // LANES, 8 * (seg + S5_SEG_PAD), LANES), F32),
                        pltpu.VMEM((W2 // LANES, 8 * (seg + S5_SEG_PAD), LANES), F32)],
        compiler_params=_cp(("parallel", "parallel")),
        name="s5_seq",
    )(up, be_emb, tp, cpm, pt, s0, dsk)
    return yg, sfin


def _s5_step_kernel(u_ref, b_ref, c_ref, a_ref, s0_ref, d_ref, yg_ref, s1_ref):
    W = SLAB_STATE
    u = u_ref[0]
    bu = _dot(u, b_ref[0, 0], HI)
    ar = a_ref[0, 0, :, 0:W]
    ai = a_ref[0, 0, :, W:2 * W]
    sr = s0_ref[0, :, 0:W]
    si = s0_ref[0, :, W:2 * W]
    nr = ar * sr - ai * si + bu[:, 0:W]
    ni = ar * si + ai * sr + bu[:, W:2 * W]
    s1_ref[0, :, 0:W] = nr
    s1_ref[0, :, W:2 * W] = ni
    s1 = jnp.concatenate([nr, ni], axis=-1)
    y = _dot(s1, c_ref[0, 0], HI) + d_ref[0] * u
    yg_ref[0] = jax.nn.gelu(y)


def _s5_step(u_slab, bst, c0, a1, s0, d1, l):
    _, N, _ = u_slab.shape
    W2 = 2 * SLAB_STATE
    return pl.pallas_call(
        _s5_step_kernel,
        out_shape=(jax.ShapeDtypeStruct((SLABS, N, LANES), F32),
                   jax.ShapeDtypeStruct((SLABS, N, W2), F32)),
        grid=(SLABS,),
        in_specs=[pl.BlockSpec((1, N, LANES), lambda k: (k, 0, 0)),
                  pl.BlockSpec((1, 1, LANES, W2), lambda k: (l, k, 0, 0)),
                  pl.BlockSpec((1, 1, W2, LANES), lambda k: (l, k, 0, 0)),
                  pl.BlockSpec((1, 1, 1, W2), lambda k: (l, k, 0, 0)),
                  pl.BlockSpec((1, N, W2), lambda k: (k, 0, 0)),
                  pl.BlockSpec((1, 1, LANES), lambda k: (k, 0, 0))],
        out_specs=(pl.BlockSpec((1, N, LANES), lambda k: (k, 0, 0)),
                   pl.BlockSpec((1, N, W2), lambda k: (k, 0, 0))),
        compiler_params=_cp(("parallel",)),
        name="s5_step",
    )(u_slab, bst, c0, a1, s0, d1)


def _l2n(x):
    return x * lax.rsqrt(jnp.sum(x * x, axis=-1, keepdims=True) + L2_EPS)


def _split_bf16(x):
    hi = x.astype(BF16)
    return hi, (x - hi.astype(F32)).astype(BF16)


def _unit_lower_solve(As, rhss):
    n = GDN_C
    row = lax.broadcasted_iota(jnp.int32, (n, n), 0)
    col = lax.broadcasted_iota(jnp.int32, (n, n), 1)
    eye = (row == col).astype(F32)
    same8 = (row // 8) == (col // 8)
    Qs = [jnp.where(same8, -A, 0.0) for A in As]
    invs = [eye + Q for Q in Qs]
    for _ in range(2):
        Qs = [_dotb(Q, Q) for Q in Qs]
        invs = [inv + _dotb(inv, Q) for inv, Q in zip(invs, Qs)]
    s = 8
    while s < n:
        sib = ((row // (2 * s)) == (col // (2 * s))) & ((row // s) != (col // s))
        offs = [jnp.where(sib, A, 0.0).astype(BF16) for A in As]
        invb = [inv.astype(BF16) for inv in invs]
        tmp = [_dot(off, ib) for off, ib in zip(offs, invb)]
        invs = [inv - _dot(ib, t.astype(BF16)) for inv, ib, t in zip(invs, invb, tmp)]
        s *= 2
    invb = [inv.astype(BF16) for inv in invs]
    x0s = [_dot(ib, rhs.astype(BF16)) for ib, rhs in zip(invb, rhss)]
    res = []
    for A, x0, rhs in zip(As, x0s, rhss):
        ah, al = _split_bf16(A)
        xh, xl = _split_bf16(x0)
        res.append(rhs - x0 - (_dot(ah, xh) + _dot(ah, xl) + _dot(al, xh)))
    return [x0 + _dot(ib, r.astype(BF16)) for x0, ib, r in zip(x0s, invb, res)]


def _gdn_tile(qc_scr, gc, beta, z_ref, nw, o_ref, s_scr, tl, r0):
    C, DK, H = GDN_C, GDN_DK, GDN_HEADS
    nchunk = tl // C
    probs = [(c, h) for c in range(nchunk) for h in range(H)]
    row = lax.broadcasted_iota(jnp.int32, (C, C), 0)
    col = lax.broadcasted_iota(jnp.int32, (C, C), 1)
    tri = row >= col
    strict = row > col

    def blk(c, off):
        return qc_scr[c * C:(c + 1) * C, off:off + DK]

    q = [_l2n(blk(c, h * DK)) * (DK ** -0.5) for c, h in probs]
    k = [_l2n(blk(c, GDN_WIDTH + h * DK)) for c, h in probs]
    v = [blk(c, 2 * GDN_WIDTH + h * DK) for c, h in probs]
    gcb = [jnp.broadcast_to(gc[c * C:(c + 1) * C, h:h + 1], (C, DK)) for c, h in probs]
    bb = [jnp.broadcast_to(beta[c * C:(c + 1) * C, H + h:H + h + 1], (C, DK)) for c, h in probs]
    decay = []
    for g in gcb:
        diff = g - g.T
        decay.append(jnp.where(tri, jnp.exp(jnp.where(tri, diff, 0.0)), 0.0))
    kbf = [x.astype(BF16) for x in k]
    kb = [x * b for x, b in zip(k, bb)]
    A = [jnp.where(strict, _dot_nt(x.astype(BF16), y) * d, 0.0) for x, y, d in zip(kb, kbf, decay)]
    egc = [jnp.exp(g) for g in gcb]
    rhs = [jnp.concatenate([x * b, y * e], axis=-1) for x, b, y, e in zip(v, bb, kb, egc)]
    sol = _unit_lower_solve(A, rhs)
    attn = [jnp.where(tri, _dot_nt(x.astype(BF16), y) * d, 0.0).astype(BF16)
            for x, y, d in zip(q, kbf, decay)]
    glast = [g[C - 1:C, :] for g in gcb]
    wq = [jnp.concatenate([s[:, DK:], x * e], axis=0).astype(BF16) for s, x, e in zip(sol, q, egc)]
    kg = [(x * jnp.exp(gl - g)).astype(BF16) for x, gl, g in zip(k, glast, gcb)]

    for c in range(nchunk):
        ps = [c * H + h for h in range(H)]
        S = [s_scr[h] for h in range(H)]
        ws = [_dot(wq[p], S[h].astype(BF16)) for h, p in enumerate(ps)]
        v_new = [sol[p][:, 0:DK] - w[0:C] for p, w in zip(ps, ws)]
        vb = [x.astype(BF16) for x in v_new]
        o = [w[C:] + _dot(attn[p], x) for p, w, x in zip(ps, ws, vb)]
        for h, p in enumerate(ps):
            s_scr[h] = S[h] * jnp.exp(glast[p]) + _dot_tn(kg[p], vb[h])
            zh = z_ref[0, r0 + c * C:r0 + (c + 1) * C, h * DK:(h + 1) * DK].astype(F32)
            on = o[h] * lax.rsqrt(jnp.mean(o[h] * o[h], axis=-1, keepdims=True) + NORM_EPS) * nw
            o_ref[0, r0 + c * C:r0 + (c + 1) * C, h * DK:(h + 1) * DK] = (on * _silu(zh)).astype(o_ref.dtype)


def _gdn_seq_kernel(qkv_ref, z_ref, ab_ref, cw_ref, alog_ref, dtb_ref, nw_ref, conv0_ref, s0_ref,
                    o_ref, sfin_ref, xp_scr, qc_scr, s_scr, *, tl):
    lt = pl.program_id(1)

    @pl.when(lt == 0)
    def _():
        xp_scr[0:8, :] = jnp.zeros((8, QKV_WIDTH), F32)
        xp_scr[8 - (GDN_CONV - 1):8, :] = conv0_ref[0]
        s_scr[...] = s0_ref[0]

    sub = qc_scr.shape[0]
    for r0 in range(0, tl, sub):
        xp_scr[8:8 + sub, :] = qkv_ref[0, r0:r0 + sub, :].astype(F32)
        conv = cw_ref[0, 0:1, :] * xp_scr[5:5 + sub, :]
        for j in range(1, GDN_CONV):
            conv = conv + cw_ref[0, j:j + 1, :] * xp_scr[5 + j:5 + j + sub, :]
        xp_scr[0:8, :] = xp_scr[sub:sub + 8, :]
        qc_scr[...] = _silu(conv)

        ab = ab_ref[0, r0:r0 + sub, :]
        g = -jnp.exp(alog_ref[...]) * jax.nn.softplus(ab + dtb_ref[...])
        beta = jax.nn.sigmoid(ab)
        row = lax.broadcasted_iota(jnp.int32, (sub, sub), 0)
        col = lax.broadcasted_iota(jnp.int32, (sub, sub), 1)
        csum = ((row >= col) & ((row // GDN_C) == (col // GDN_C))).astype(F32)
        gc = _dot(csum, g, HI)
        _gdn_tile(qc_scr, gc, beta, z_ref, nw_ref[...], o_ref, s_scr, sub, r0)

    @pl.when(lt == pl.num_programs(1) - 1)
    def _():
        sfin_ref[0] = s_scr[...]


def _gdn_seq(qkv, z, ab, conv_w, alog, dtb, nw, conv0, s0, l):
    B, L, _ = qkv.shape
    tl = min(GDN_BLOCK, L)
    sub = min(GDN_SUB, tl)
    return pl.pallas_call(
        functools.partial(_gdn_seq_kernel, tl=tl),
        out_shape=(jax.ShapeDtypeStruct((B, L, GDN_WIDTH), BF16),
                   jax.ShapeDtypeStruct((B, GDN_HEADS, GDN_DK, GDN_DK), F32)),
        grid=(B, L // tl),
        in_specs=[pl.BlockSpec((1, tl, QKV_WIDTH), lambda b, i: (b, i, 0)),
                  pl.BlockSpec((1, tl, GDN_WIDTH), lambda b, i: (b, i, 0)),
                  pl.BlockSpec((1, tl, LANES), lambda b, i: (b, i, 0)),
                  pl.BlockSpec((1, GDN_CONV, QKV_WIDTH), lambda b, i: (l, 0, 0)),
                  pl.BlockSpec((1, LANES), lambda b, i: (0, 0)),
                  pl.BlockSpec((1, LANES), lambda b, i: (0, 0)),
                  pl.BlockSpec((1, GDN_DK), lambda b, i: (0, 0)),
                  pl.BlockSpec((1, GDN_CONV - 1, QKV_WIDTH), lambda b, i: (b, 0, 0)),
                  pl.BlockSpec((1, GDN_HEADS, GDN_DK, GDN_DK), lambda b, i: (b, 0, 0, 0))],
        out_specs=(pl.BlockSpec((1, tl, GDN_WIDTH), lambda b, i: (b, i, 0)),
                   pl.BlockSpec((1, GDN_HEADS, GDN_DK, GDN_DK), lambda b, i: (b, 0, 0, 0))),
        scratch_shapes=[pltpu.VMEM((sub + 8, QKV_WIDTH), F32),
                        pltpu.VMEM((sub, QKV_WIDTH), F32),
                        pltpu.VMEM((GDN_HEADS, GDN_DK, GDN_DK), F32)],
        compiler_params=_cp(("parallel", "arbitrary")),
        name="gdn_seq",
    )(qkv, z, ab, conv_w, alog, dtb, nw, conv0, s0)


GDN_STEP_ROWS = 8


def _gdn_step_kernel(qkv_ref, z_ref, ab_ref, cw_ref, alog_ref, dtb_ref, nw_ref, conv0_ref, s0_ref,
                     *rest):
    if len(rest) == 3:
        prev_ref, o_ref, s1_all = rest
        s1_all[0] = prev_ref[...]
        s1_ref = s1_all.at[1]
    else:
        o_ref, s1_all = rest
        s1_ref = s1_all
    nb = GDN_STEP_ROWS
    W = QKV_WIDTH
    conv = cw_ref[0:1, :] * conv0_ref[:, 0:W]
    conv = conv + cw_ref[1:2, :] * conv0_ref[:, W:2 * W]
    conv = conv + cw_ref[2:3, :] * conv0_ref[:, 2 * W:3 * W]
    conv = conv + cw_ref[3:4, :] * qkv_ref[...]
    qc = _silu(conv)
    ab = ab_ref[...]
    eg = jnp.exp(-jnp.exp(alog_ref[...]) * jax.nn.softplus(ab + dtb_ref[...]))
    beta = jax.nn.sigmoid(ab)
    eye = (lax.broadcasted_iota(jnp.int32, (GDN_DK, GDN_DK), 0)
           == lax.broadcasted_iota(jnp.int32, (GDN_DK, GDN_DK), 1)).astype(F32)
    for h in range(GDN_HEADS):
        q = _l2n(qc[:, h * GDN_DK:(h + 1) * GDN_DK]) * (GDN_DK ** -0.5)
        k = _l2n(qc[:, GDN_WIDTH + h * GDN_DK:GDN_WIDTH + (h + 1) * GDN_DK])
        v = qc[:, 2 * GDN_WIDTH + h * GDN_DK:2 * GDN_WIDTH + (h + 1) * GDN_DK]
        kT = _dot_nt(eye, k, HI)
        qT = _dot_nt(eye, q, HI)
        qk = jnp.sum(q * k, axis=-1, keepdims=True)
        for j in range(nb):
            S = s0_ref[0, j, h]
            kc = jnp.broadcast_to(kT[:, j:j + 1], (GDN_DK, GDN_DK))
            qcb = jnp.broadcast_to(qT[:, j:j + 1], (GDN_DK, GDN_DK))
            kS = jnp.sum(kc * S, axis=0, keepdims=True)
            qS = jnp.sum(qcb * S, axis=0, keepdims=True)
            egj = eg[j:j + 1, h:h + 1]
            bj = beta[j:j + 1, GDN_HEADS + h:GDN_HEADS + h + 1]
            v_new = bj * v[j:j + 1, :] - (bj * egj) * kS
            o = egj * qS + qk[j:j + 1, :] * v_new
            s1_ref[j, h] = S * egj + kc * v_new
            zh = z_ref[j:j + 1, h * GDN_DK:(h + 1) * GDN_DK]
            on = o * lax.rsqrt(jnp.mean(o * o, axis=-1, keepdims=True) + NORM_EPS) * nw_ref[...]
            o_ref[j:j + 1, h * GDN_DK:(h + 1) * GDN_DK] = on * _silu(zh)


def _gdn_step(qkv, z, ab, conv_w, alog, dtb, nw, conv0, s_all, l, prev):
    N = qkv.shape[0]
    nb = GDN_STEP_ROWS
    row = lambda w: pl.BlockSpec((nb, w), lambda i: (i, 0))
    const = lambda r, w: pl.BlockSpec((r, w), lambda i: (0, 0))
    sblk = (nb, GDN_HEADS, GDN_DK, GDN_DK)
    one = pl.BlockSpec(sblk, lambda i: (i, 0, 0, 0))
    ins = [qkv, z, ab, conv_w, alog, dtb, nw, conv0, s_all]
    in_specs = [row(QKV_WIDTH), row(GDN_WIDTH), row(LANES), const(GDN_CONV, QKV_WIDTH),
                const(1, LANES), const(1, LANES), const(1, GDN_DK), row(3 * QKV_WIDTH),
                pl.BlockSpec((1,) + sblk, lambda i: (l, i, 0, 0, 0))]
    if prev is None:
        s_shape, s_spec = jax.ShapeDtypeStruct((N,) + sblk[1:], F32), one
    else:
        assert DEPTH == 2 and l == 1
        ins.append(prev)
        in_specs.append(one)
        s_shape = jax.ShapeDtypeStruct((DEPTH, N) + sblk[1:], F32)
        s_spec = pl.BlockSpec((DEPTH,) + sblk, lambda i: (0, i, 0, 0, 0))
    return pl.pallas_call(
        _gdn_step_kernel,
        out_shape=(jax.ShapeDtypeStruct((N, GDN_WIDTH), F32), s_shape),
        grid=(N // nb,),
        in_specs=in_specs,
        out_specs=(row(GDN_WIDTH), s_spec),
        compiler_params=_cp(("parallel",)),
        name="gdn_step",
    )(*ins)


def _merge_kernel(yg_ref, og_ref, ga_ref, gb_ref, x_ref, gt_ref, wglu_ref, wgo_ref, wout_ref,
                  gf_ref, scf_ref, shf_ref, *rest, hi, chunked, routed):
    if routed:
        wr_ref, xo_ref, h_ref, lg_ref, *scr = rest
    else:
        xo_ref, h_ref, *scr = rest
    if chunked:
        y_scr = scr[-1]
        scr = scr[:-1]
        nrow = y_scr.shape[1] // S5_T
        for k in range(SLABS):
            for t in range(S5_T):
                y_scr[k, pl.ds(t, nrow, stride=S5_T), :] = (
                    yg_ref[k, 0, :, t * LANES:(t + 1) * LANES].astype(F32))
        y = jnp.concatenate([y_scr[k] for k in range(SLABS)], axis=-1)
    else:
        y = jnp.concatenate([yg_ref[k, 0] for k in range(SLABS)], axis=-1)
    if hi:
        wglu, wgo, wout = wglu_ref[0], wgo_ref[0], wout_ref[0]
        mm = _dot3
    else:
        wglu_s, wgo_s, wout_s = scr

        @pl.when((pl.program_id(0) == 0) & (pl.program_id(1) == 0))
        def _():
            wglu_s[...] = wglu_ref[0].astype(BF16)
            wgo_s[...] = wgo_ref[0].astype(BF16)
            wout_s[...] = wout_ref[0].astype(BF16)

        wglu, wgo, wout = wglu_s[...], wgo_s[...], wout_s[...]
        mm = lambda a, w: _dot(a.astype(BF16), w)

    glu = mm(y, wglu)
    branch_a = glu[:, 0:D_MODEL] * jax.nn.sigmoid(glu[:, D_MODEL:])
    branch_b = mm(og_ref[0], wgo)
    merged = ga_ref[0].astype(F32) * branch_a + gb_ref[0].astype(F32) * branch_b
    out = mm(merged, wout)
    x = x_ref[0] + gt_ref[0, 0] * out
    xo_ref[0] = x
    ms = jnp.mean(x * x, axis=-1, keepdims=True)
    h = x * lax.rsqrt(ms + NORM_EPS) * gf_ref[0]
    h = h * (1.0 + scf_ref[0, 0]) + shf_ref[0, 0]
    h_ref[0] = h.astype(h_ref.dtype)
    if routed:
        lg_ref[0] = _dot_nt(wr_ref[0], h, HI)


def _merge(yg, og, ga, gb, x, mod, wglu, wgo, wout, gf, wr, *, l, tm, hi, chunked, h_dtype):
    B, L, D = x.shape
    routed = l % 2 == 1
    row = lambda w: pl.BlockSpec((1, tm, w), lambda b, i: (b, i, 0))
    layer = lambda r, w, ll=l: pl.BlockSpec((1, r, w), lambda b, i: (ll, 0, 0))
    scratch = [] if hi else [pltpu.VMEM((S5_WIDTH, 2 * D), BF16), pltpu.VMEM((GDN_WIDTH, D), BF16),
                             pltpu.VMEM((D, D), BF16)]
    if chunked:
        scratch = scratch + [pltpu.VMEM((SLABS, tm, LANES), F32)]
        yg_spec = pl.BlockSpec((SLABS, 1, tm // S5_T, S5_T * LANES), lambda b, i: (0, b, i, 0))
    else:
        yg_spec = pl.BlockSpec((SLABS, 1, tm, LANES), lambda b, i: (0, b, i, 0))
    lg_shape = jax.ShapeDtypeStruct((B, N_EXPERTS, L), F32)
    lg_spec = pl.BlockSpec((1, N_EXPERTS, tm), lambda b, i: (b, 0, i))
    outs = pl.pallas_call(
        functools.partial(_merge_kernel, hi=hi, chunked=chunked, routed=routed),
        out_shape=(jax.ShapeDtypeStruct((B, L, D), F32),
                   jax.ShapeDtypeStruct((B, L, D), h_dtype)) + ((lg_shape,) if routed else ()),
        grid=(B, L // tm),
        in_specs=[yg_spec,
                  row(GDN_WIDTH), row(D), row(D), row(D), _mod_spec(mod, l, 2, tm),
                  layer(S5_WIDTH, 2 * D), layer(GDN_WIDTH, D), layer(D, D),
                  layer(1, D), _mod_spec(mod, l, 4, tm), _mod_spec(mod, l, 3, tm)]
        + ([layer(N_EXPERTS, D, l // 2)] if routed else []),
        out_specs=(row(D), row(D)) + ((lg_spec,) if routed else ()),
        scratch_shapes=scratch,
        compiler_params=_cp(("arbitrary", "arbitrary")),
        name="merge_out_proj",
    )(yg, og, ga, gb, x, mod, wglu, wgo, wout, gf, mod, mod, *((wr,) if routed else ()))
    return outs if routed else (*outs, None)


FF_TILE = 512
FFN_ROWS = 512


def _finish(x, gfin_ref, final):
    if not final:
        return x
    ms = jnp.mean(x * x, axis=-1, keepdims=True)
    return x * lax.rsqrt(ms + NORM_EPS) * gfin_ref[...]


def _ffn_kernel(h_ref, x_ref, gt_ref, wg_ref, wu_ref, wd_ref, gfin_ref, o_ref, acc_scr, *, hi, final):
    j = pl.program_id(2)
    if hi:
        wg, wu, wd = wg_ref[...], wu_ref[...], wd_ref[...]
        mm = _dot3
    else:
        wg, wu, wd = wg_ref[...].astype(BF16), wu_ref[...].astype(BF16), wd_ref[...].astype(BF16)
        mm = lambda a, w: _dot(a.astype(BF16), w)

    @pl.when(j == 0)
    def _():
        acc_scr[...] = jnp.zeros_like(acc_scr)

    tm = acc_scr.shape[0]
    sub = min(FFN_ROWS, tm)
    for s in range(tm // sub):
        rows = slice(s * sub, (s + 1) * sub)
        hb = h_ref[0, rows, :]
        act = _silu(mm(hb, wg)) * mm(hb, wu)
        acc_scr[rows, :] = acc_scr[rows, :] + mm(act, wd)

    @pl.when(j == pl.num_programs(2) - 1)
    def _():
        o_ref[0] = _finish(x_ref[0] + gt_ref[0, 0] * acc_scr[...], gfin_ref, final)


def _ffn(h, x, mod, w_gu, w_down, gfin, *, l, tm, hi, final):
    B, L, D = x.shape
    nj = D_FF // FF_TILE
    row = pl.BlockSpec((1, tm, D), lambda b, i, j: (b, i, 0))
    return pl.pallas_call(
        functools.partial(_ffn_kernel, hi=hi, final=final),
        out_shape=jax.ShapeDtypeStruct((B, L, D), F32),
        grid=(B, L // tm, nj),
        in_specs=[row, row, _mod_spec(mod, l, 5, tm),
                  pl.BlockSpec((D, FF_TILE), lambda b, i, j: (0, j)),
                  pl.BlockSpec((D, FF_TILE), lambda b, i, j: (0, nj + j)),
                  pl.BlockSpec((FF_TILE, D), lambda b, i, j: (j, 0)),
                  pl.BlockSpec((1, D), lambda b, i, j: (0, 0))],
        out_specs=row,
        scratch_shapes=[pltpu.VMEM((tm, D), F32)],
        compiler_params=_cp(("parallel", "parallel", "arbitrary")),
        name="ffn_dense",
    )(h, x, mod, w_gu, w_gu, w_down, gfin)


ROUTE_TM = 512
ROW_DMA_TM = 512
MOE_SUP = 2048
MOE_SUB = 512
MOE_FF_TILE = 512


def _route_kernel(lg_ref, br_ref, cnt0_ref, slot_ref, wt_ref, cnt_ref, carry_scr, *, cap):
    @pl.when((pl.program_id(0) == 0) & (pl.program_id(1) == 0))
    def _():
        carry_scr[...] = cnt0_ref[...]

    lg = lg_ref[0] + br_ref[...]
    tm = lg.shape[1]
    eidx = lax.broadcasted_iota(jnp.int32, lg.shape, 0)
    m1 = jnp.max(lg, axis=0, keepdims=True)
    i1 = jnp.min(jnp.where(lg == m1, eidx, N_EXPERTS), axis=0, keepdims=True)
    lg2 = jnp.where(eidx == i1, -jnp.inf, lg)
    m2 = jnp.max(lg2, axis=0, keepdims=True)
    i2 = jnp.min(jnp.where(lg2 == m2, eidx, N_EXPERTS), axis=0, keepdims=True)
    e2 = jnp.exp(m2 - m1)
    wt_ref[0, 0:1, :] = 1.0 / (1.0 + e2)
    wt_ref[0, 1:2, :] = e2 / (1.0 + e2)
    sel1 = eidx == i1
    sel2 = eidx == i2
    oh = jnp.where(sel1 | sel2, 1.0, 0.0)
    before = (lax.broadcasted_iota(jnp.int32, (tm, tm), 0)
              < lax.broadcasted_iota(jnp.int32, (tm, tm), 1)).astype(BF16)
    rank = carry_scr[:, 0:1] + _dot(oh.astype(BF16), before)
    r1 = jnp.sum(jnp.where(sel1, rank, 0.0), axis=0, keepdims=True).astype(jnp.int32)
    r2 = jnp.sum(jnp.where(sel2, rank, 0.0), axis=0, keepdims=True).astype(jnp.int32)
    slot_ref[0, 0:1, :] = i1 * cap + r1
    slot_ref[0, 1:2, :] = i2 * cap + r2
    carry_scr[...] = carry_scr[...] + jnp.sum(oh, axis=1, keepdims=True)
    cnt_ref[...] = carry_scr[...]


def _route_slots(lgT, b_r, cnt0, cap):
    B, E, L = lgT.shape
    tm = min(ROUTE_TM, L)
    return pl.pallas_call(
        functools.partial(_route_kernel, cap=cap),
        out_shape=(jax.ShapeDtypeStruct((B, 2, L), jnp.int32),
                   jax.ShapeDtypeStruct((B, 2, L), F32),
                   jax.ShapeDtypeStruct((E, LANES), F32)),
        grid=(B, L // tm),
        in_specs=[pl.BlockSpec((1, E, tm), lambda b, i: (b, 0, i)),
                  pl.BlockSpec((E, 1), lambda b, i: (0, 0)),
                  pl.BlockSpec((E, LANES), lambda b, i: (0, 0))],
        out_specs=(pl.BlockSpec((1, 2, tm), lambda b, i: (b, 0, i)),
                   pl.BlockSpec((1, 2, tm), lambda b, i: (b, 0, i)),
                   pl.BlockSpec((E, LANES), lambda b, i: (0, 0))),
        scratch_shapes=[pltpu.VMEM((E, LANES), F32)],
        compiler_params=_cp(("arbitrary", "arbitrary")),
        name="moe_route",
    )(lgT, b_r.reshape(E, 1), cnt0)


def _row_copy(src, dst, sem):
    return pltpu.make_async_copy(src, dst, sem)


def _slot_rows_kernel(start_ref, per_ref, code_ref, row_ref, *, cap):
    code = code_ref[...]
    shift = cap.bit_length() - 1
    e = lax.shift_right_logical(code, shift)
    r = code & (cap - 1)
    start = jnp.zeros_like(code)
    per = jnp.ones_like(code)
    for k in range(N_EXPERTS):
        start = jnp.where(e == k, start_ref[k], start)
        per = jnp.where(e == k, per_ref[k], per)
    q = jnp.floor((r.astype(F32) + 0.5) / per.astype(F32)).astype(jnp.int32)
    row_ref[...] = start + q * MOE_SUP + (r - q * per)


def _slot_rows(start, per, codes, cap):
    B, _, L = codes.shape
    tm = min(ROUTE_TM, L)
    spec = pl.BlockSpec((1, 2, tm), lambda b, i, st, pe: (b, 0, i))
    return pl.pallas_call(
        functools.partial(_slot_rows_kernel, cap=cap),
        out_shape=jax.ShapeDtypeStruct(codes.shape, jnp.int32),
        grid_spec=pltpu.PrefetchScalarGridSpec(
            num_scalar_prefetch=2, grid=(B, L // tm), in_specs=[spec], out_specs=spec),
        compiler_params=_cp(("parallel", "parallel")),
        name="moe_slot_rows",
    )(start, per, codes)


def _zeros_kernel(o_ref):
    o_ref[...] = jnp.zeros_like(o_ref)


def _zero_rows(n_rows, width):
    return pl.pallas_call(
        _zeros_kernel,
        out_shape=jax.ShapeDtypeStruct((n_rows, width), F32),
        grid=(n_rows // MOE_SUP,),
        out_specs=pl.BlockSpec((MOE_SUP, width), lambda i: (i, 0)),
        compiler_params=_cp(("parallel",)),
        name="moe_zero_rows",
    )()


def _dispatch_kernel(row_ref, h_ref, xs_in_ref, xs_ref, hbuf, sem):
    del xs_in_ref
    tm = h_ref.shape[1]
    t = pl.program_id(0) * pl.num_programs(1) + pl.program_id(1)
    last = pl.num_programs(0) * pl.num_programs(1) - 1
    slot = t % 2
    hbuf[slot] = h_ref[0]

    def issue(r, _):
        for k in range(2):
            row = row_ref[0, k, r]
            _row_copy(hbuf.at[slot, pl.ds(r, 1), :], xs_ref.at[pl.ds(row, 1), :],
                      sem.at[slot]).start(priority=k)
        return 0

    lax.fori_loop(0, tm, issue, 0, unroll=8)

    def drain(sl):
        for k in range(2):
            _row_copy(hbuf.at[sl], xs_ref.at[pl.ds(0, tm), :], sem.at[sl]).wait()

    @pl.when(t > 0)
    def _():
        drain(1 - slot)

    @pl.when(t == last)
    def _():
        drain(slot)


def _dispatch(rows, h, xs):
    B, L, D = h.shape
    n_rows = xs.shape[0]
    tm = min(ROW_DMA_TM, L)
    return pl.pallas_call(
        _dispatch_kernel,
        out_shape=jax.ShapeDtypeStruct((n_rows, D), F32),
        grid=(B, L // tm),
        in_specs=[pl.BlockSpec((1, 2, tm), lambda b, i: (b, 0, i), memory_space=pltpu.SMEM),
                  pl.BlockSpec((1, tm, D), lambda b, i: (b, i, 0)),
                  pl.BlockSpec(memory_space=pl.ANY)],
        out_specs=pl.BlockSpec(memory_space=pl.ANY),
        scratch_shapes=[pltpu.VMEM((2, tm, D), F32), pltpu.SemaphoreType.DMA((2,))],
        input_output_aliases={2: 0},
        compiler_params=_cp(("arbitrary", "arbitrary")),
        name="moe_dispatch",
    )(rows, h, xs)


def _moe_grp_kernel(ge_ref, gn_ref, x_ref, wg_ref, wu_ref, wd_ref, y_ref, xb_scr):
    g = pl.program_id(0)
    j = pl.program_id(1)
    nsub = gn_ref[g]
    wg = wg_ref[0].astype(BF16)
    wu = wu_ref[0].astype(BF16)
    wd = wd_ref[0].astype(BF16)
    nblk = MOE_SUP // MOE_SUB

    @pl.when(j == 0)
    def _():
        xb_scr[...] = x_ref[...].astype(BF16)
        y_ref[...] = jnp.zeros_like(y_ref)

    def block(s):
        rows = slice(s * MOE_SUB, (s + 1) * MOE_SUB)
        xb = xb_scr[rows, :]
        act = _silu(_dot(xb, wg)) * _dot(xb, wu)
        y_ref[rows, :] = y_ref[rows, :] + _dot(act.astype(BF16), wd)

    for n in range(1, nblk + 1):
        @pl.when(nsub == n)
        def _():
            for s in range(n):
                block(s)


def _moe_groups(counts, n_groups):
    nsup = (counts + MOE_SUP - 1) // MOE_SUP
    div = jnp.maximum(nsup, 1)
    per = jnp.maximum(((counts + div - 1) // div + MOE_SUB - 1) // MOE_SUB * MOE_SUB, MOE_SUB)
    ends = jnp.cumsum(nsup)
    first = ends - nsup
    total = ends[-1]
    g = jnp.arange(n_groups, dtype=jnp.int32)
    gc = jnp.minimum(g, total - 1)
    e_of = jnp.minimum(jnp.sum((gc[:, None] >= ends[None, :]).astype(jnp.int32), axis=1), N_EXPERTS - 1)
    left = jnp.minimum(counts[e_of] - (gc - first[e_of]) * per[e_of], per[e_of])
    nsub = jnp.clip((left + MOE_SUB - 1) // MOE_SUB, 0, MOE_SUP // MOE_SUB)
    gn = jnp.where(g < total, nsub, 0).astype(jnp.int32)
    return e_of, gn, (first * MOE_SUP).astype(jnp.int32), per.astype(jnp.int32)


def _moe_grouped(xs, ge, gn, w_gu, w_down):
    D = xs.shape[1]
    nj = D_FF // MOE_FF_TILE
    ng = xs.shape[0] // MOE_SUP
    jj = lambda j, gn, g: jnp.where(gn[g] > 0, j, nj - 1)
    return pl.pallas_call(
        _moe_grp_kernel,
        out_shape=jax.ShapeDtypeStruct(xs.shape, F32),
        grid_spec=pltpu.PrefetchScalarGridSpec(
            num_scalar_prefetch=2,
            grid=(ng, nj),
            in_specs=[pl.BlockSpec((MOE_SUP, D), lambda g, j, ge, gn: (g, 0)),
                      pl.BlockSpec((1, D, MOE_FF_TILE), lambda g, j, ge, gn: (ge[g], 0, jj(j, gn, g))),
                      pl.BlockSpec((1, D, MOE_FF_TILE), lambda g, j, ge, gn: (ge[g], 0, nj + jj(j, gn, g))),
                      pl.BlockSpec((1, MOE_FF_TILE, D), lambda g, j, ge, gn: (ge[g], jj(j, gn, g), 0))],
            out_specs=pl.BlockSpec((MOE_SUP, D), lambda g, j, ge, gn: (g, 0)),
            scratch_shapes=[pltpu.VMEM((MOE_SUP, D), BF16)],
        ),
        compiler_params=_cp(("arbitrary", "arbitrary")),
        name="moe_experts",
    )(ge, gn, xs, w_gu, w_gu, w_down)


def _combine_kernel(row_ref, next_ref, w_ref, x_ref, gt_ref, gfin_ref, ys_ref, o_ref, g_scr, sem, *, final):
    tm = x_ref.shape[1]
    t = pl.program_id(0) * pl.num_programs(1) + pl.program_id(1)
    last = pl.num_programs(0) * pl.num_programs(1) - 1
    slot = t % 2

    def gather(rows, sl):
        def issue(r, _):
            for k in range(2):
                row = rows[0, k, r]
                _row_copy(ys_ref.at[pl.ds(row, 1), :], g_scr.at[sl, k, pl.ds(r, 1), :],
                          sem.at[sl]).start(priority=k)
            return 0

        lax.fori_loop(0, tm, issue, 0, unroll=8)

    @pl.when(t == 0)
    def _():
        gather(row_ref, slot)

    @pl.when(t < last)
    def _():
        gather(next_ref, 1 - slot)

    for k in range(2):
        _row_copy(ys_ref.at[pl.ds(0, tm), :], g_scr.at[slot, k], sem.at[slot]).wait()
    w = w_ref[0]
    f = w[:, 0:1] * g_scr[slot, 0] + w[:, 1:2] * g_scr[slot, 1]
    o_ref[0] = _finish(x_ref[0] + gt_ref[0, 0] * f, gfin_ref, final)


def _combine(rows, wts, x, mod, gfin, ys, *, l, final):
    B, L, D = x.shape
    tm = min(ROW_DMA_TM, L)
    gt_spec = _mod_spec(mod, l, 5, tm)
    row = pl.BlockSpec((1, tm, D), lambda b, i: (b, i, 0))
    nl = L // tm

    def next_block(b, i):
        t1 = jnp.minimum(b * nl + i + 1, B * nl - 1)
        return (t1 // nl, 0, t1 % nl)

    return pl.pallas_call(
        functools.partial(_combine_kernel, final=final),
        out_shape=jax.ShapeDtypeStruct((B, L, D), F32),
        grid=(B, L // tm),
        in_specs=[pl.BlockSpec((1, 2, tm), lambda b, i: (b, 0, i), memory_space=pltpu.SMEM),
                  pl.BlockSpec((1, 2, tm), next_block, memory_space=pltpu.SMEM),
                  pl.BlockSpec((1, tm, 2), lambda b, i: (b, i, 0)),
                  row, gt_spec,
                  pl.BlockSpec((1, D), lambda b, i: (0, 0)),
                  pl.BlockSpec(memory_space=pl.ANY)],
        out_specs=row,
        scratch_shapes=[pltpu.VMEM((2, 2, tm, D), F32), pltpu.SemaphoreType.DMA((2,))],
        compiler_params=_cp(("arbitrary", "arbitrary")),
        name="moe_combine",
    )(rows, rows, wts.transpose(0, 2, 1), x, mod, gfin, ys)


def _moe_routed(groups, b_r, w_gu, w_down, gfin, *, l, final):
    D = groups[0][1].shape[-1]
    n_tok = sum(g[1].shape[0] * g[1].shape[1] for g in groups)
    cap = 1 << (n_tok - 1).bit_length()
    n_groups = 2 * n_tok // MOE_SUP + N_EXPERTS
    cnt = jnp.zeros((N_EXPERTS, LANES), F32)
    routed = []
    for _, _, _, lgT in groups:
        codes, wts, cnt = _route_slots(lgT, b_r, cnt, cap)
        routed.append((codes, wts))
    ge, gn, start, per = _moe_groups(cnt[:, 0].astype(jnp.int32), n_groups)
    rows = [_slot_rows(start, per, codes, cap) for codes, _ in routed]
    xs = _zero_rows(n_groups * MOE_SUP, D)
    for (h, _, _, _), r in zip(groups, rows):
        xs = _dispatch(r, h, xs)
    ys = _moe_grouped(xs, ge, gn, w_gu, w_down)
    return [_combine(r, wts, x, mod, gfin, ys, l=l, final=final)
            for (_, x, mod, _), (_, wts), r in zip(groups, routed, rows)]


def _pad_lanes(v):
    return jnp.pad(v.reshape(1, -1), ((0, 0), (0, LANES - v.shape[-1])))


def _mixer_layer(x, mod, states, p, s5m, l, prev_sg, *, seq):
    B, L, D = x.shape
    hi = not seq
    s5r0, s5i0, sg0, sc0 = states
    w_in, w_gates, w_ab = p['w_in_seq' if seq else 'w_in']
    if seq:
        u, qkv, z, ga, gb, ab = _proj_seq(x, p['g_mix'], mod, w_in, w_gates, w_ab, l=l, tm=min(1024, L))
    else:
        u, qkv, z, ga, gb, ab = _proj(x, p['g_mix'], mod, w_in, w_gates, w_ab, l=l, tm=L)
    alog = _pad_lanes(p['gdn_a_log'][l])
    dtb = _pad_lanes(p['gdn_dt_bias'][l])
    nw = p['gdn_norm_w'][l].reshape(1, GDN_DK)
    if seq:
        yg, sfin = _s5_seq(u, s5m['be'], s5m['tp'], s5m['cpm'], s5m['pt'],
                           jnp.zeros((SLABS, B, 1, 2 * SLAB_STATE), F32), s5m['dsk'][l], l)
        sfin = sfin.reshape(SLABS, B, 2, SLAB_STATE).transpose(2, 1, 0, 3)
        sr = sfin[0].reshape(B, S5_GROUPS, S5_STATE)
        si = sfin[1].reshape(B, S5_GROUPS, S5_STATE)
        og, sg = _gdn_seq(qkv, z, ab, p['gdn_conv_w'], alog, dtb, nw,
                          jnp.zeros((B, GDN_CONV - 1, QKV_WIDTH), F32),
                          jnp.zeros((B, GDN_HEADS, GDN_DK, GDN_DK), F32), l)
        cb = qkv[:, L - (GDN_CONV - 1):, :].astype(F32)
    else:
        n = L
        s0 = jnp.concatenate([s5r0[l].reshape(n, SLABS, SLAB_STATE),
                              s5i0[l].reshape(n, SLABS, SLAB_STATE)], axis=-1).transpose(1, 0, 2)
        yg, s1 = _s5_step(u.reshape(SLABS, n, LANES), s5m['bst'], s5m['c0'], s5m['a1'],
                          s0, s5m['d1'][l], l)
        yg = yg.reshape(SLABS, 1, n, LANES)
        s1 = s1.transpose(1, 0, 2)
        sr = s1[:, :, :SLAB_STATE].reshape(n, S5_GROUPS, S5_STATE)
        si = s1[:, :, SLAB_STATE:].reshape(n, S5_GROUPS, S5_STATE)
        og, sg = _gdn_step(qkv.reshape(n, QKV_WIDTH), z.reshape(n, GDN_WIDTH), ab.reshape(n, LANES),
                           p['gdn_conv_w'][l], alog, dtb, nw,
                           sc0[l].reshape(n, (GDN_CONV - 1) * QKV_WIDTH), sg0, l, prev_sg)
        og = og.reshape(1, n, GDN_WIDTH)
        cb = jnp.concatenate([sc0[l][:, 1:, :], qkv.reshape(n, 1, QKV_WIDTH)], axis=1)
    x, h, lgT = _merge(yg, og, ga, gb, x, mod, p['w_s5_glu'], p['w_gdn_out'], p['w_out'],
                       p['g_ffn'], p['w_router'], l=l, tm=min(512, L), hi=hi, chunked=seq,
                       h_dtype=BF16 if (seq and l % 2 == 0) else F32)
    return x, h, lgT, (sr, si, sg, cb)


def kernel(x_prompt, x_sample, c_prompt, c_sample, state_s5_re, state_s5_im, state_gdn, state_conv,
           g_mix, g_ffn, g_final, w_ada, b_ada, w_in, s5_lambda_re, s5_lambda_im, s5_log_dt,
           s5_b_re, s5_b_im, s5_c_re, s5_c_im, s5_d, w_s5_glu, gdn_conv_w, gdn_a_log, gdn_dt_bias,
           gdn_norm_w, w_gdn_out, w_out, w_ffn_gate_up, w_ffn_down, w_router, b_router,
           w_exp_gate_up, w_exp_down):
    def in_proj_parts(w):
        return w, w[:, :, 2568:], jnp.pad(w[:, :, 2560:2568], ((0, 0), (0, 0), (0, LANES - 8)))

    D_ = x_prompt.shape[-1]
    p = dict(g_mix=g_mix.reshape(DEPTH, 1, D_), g_ffn=g_ffn.reshape(DEPTH, 1, D_), w_s5_glu=w_s5_glu,
             gdn_conv_w=gdn_conv_w, gdn_a_log=gdn_a_log, gdn_dt_bias=gdn_dt_bias,
             gdn_norm_w=gdn_norm_w, w_gdn_out=w_gdn_out, w_out=w_out,
             w_router=w_router.transpose(0, 2, 1), w_in=in_proj_parts(w_in),
             w_in_seq=in_proj_parts(w_in.astype(BF16)))
    nbp, L, D = x_prompt.shape
    nbs = x_sample.shape[0]

    mod = _ada(jnp.concatenate([c_prompt, c_sample], axis=0), w_ada, b_ada)
    mod_p = mod[:, :nbp].reshape(DEPTH, nbp, 1, 6 * D)
    mod_s = mod[:, nbp:].reshape(DEPTH, 1, nbs, 6 * D)

    seg = L // S5_T // 8
    be, bst, c0, cpm, tp, pt, a1 = _s5_prep(s5_lambda_re, s5_lambda_im, s5_log_dt, s5_b_re, s5_b_im,
                                            s5_c_re, s5_c_im, seg)
    d1 = [s5_d[l].reshape(SLABS, 1, LANES) for l in range(DEPTH)]
    s5m = dict(be=be, bst=bst, c0=c0, cpm=cpm, pt=pt, a1=a1, tp=tp, d1=d1,
               dsk=[jnp.tile(d, (1, 1, S5_T)) for d in d1])

    xs_ = [x_prompt, x_sample.reshape(1, nbs, D)]
    mods = [mod_p, mod_s]
    states = [(None, None, None, None), (state_s5_re, state_s5_im, state_gdn, state_conv)]
    outs = [[], []]
    gfin = g_final.reshape(1, D)
    for l in range(DEPTH):
        final = l == DEPTH - 1
        mixed = []
        for gi, seq in enumerate((True, False)):
            prev_sg = outs[gi][0][2] if (not seq and final and DEPTH == 2) else None
            x, h, lgT, st = _mixer_layer(xs_[gi], mods[gi], states[gi], p, s5m, l, prev_sg, seq=seq)
            outs[gi].append(st)
            mixed.append((h, x, mods[gi], lgT))
        if l % 2 == 0:
            wgu, wdn = w_ffn_gate_up[l // 2], w_ffn_down[l // 2]
            xs_ = [_ffn(h, x, mod_g, wgu if gi else wgu.astype(BF16), wdn if gi else wdn.astype(BF16), gfin,
                        l=l, tm=min(1024, x.shape[1]), hi=(gi == 1), final=final)
                   for gi, (h, x, mod_g, _) in enumerate(mixed)]
        else:
            xs_ = _moe_routed(mixed, b_router[l // 2], w_exp_gate_up[l // 2], w_exp_down[l // 2], gfin,
                              l=l, final=final)
    y_p, y_s = xs_
    st_p = [jnp.stack([o[i] for o in outs[0]]) for i in range(4)]
    st_s = [outs[1][-1][2] if (i == 2 and DEPTH == 2) else jnp.stack([o[i] for o in outs[1]])
            for i in range(4)]
    return (y_p, y_s.reshape(nbs, 1, D), st_p[0], st_p[1], st_p[2], st_p[3],
            st_s[0], st_s[1], st_s[2], st_s[3])
```

```python
import functools

import jax
import jax.numpy as jnp
from jax import lax
from jax.experimental import pallas as pl
from jax.experimental.pallas import tpu as pltpu

F32 = jnp.float32
BF16 = jnp.bfloat16
HI = lax.Precision.HIGHEST

D_MODEL = 1024
DEPTH = 2
S5_WIDTH = 512
S5_GROUP = 16
S5_GROUPS = 32
S5_STATE = 64
GDN_HEADS = 4
GDN_DK = 128
GDN_WIDTH = 512
GDN_CONV = 4
QKV_WIDTH = 1536
D_FF = 3584
N_EXPERTS = 8
NORM_EPS = 1e-6
L2_EPS = 1e-6

LANES = 128
SLABS = S5_WIDTH // LANES
SLAB_STATE = (S5_GROUPS // SLABS) * S5_STATE
S5_T = 8
S5_SEG_PAD = 4
GDN_C = 128
GDN_BLOCK = 512
GDN_SUB = 256
VMEM_LIMIT = 56 * 1024 * 1024


def _cp(sem, vmem=VMEM_LIMIT):
    return pltpu.CompilerParams(dimension_semantics=sem, vmem_limit_bytes=vmem)


def _dot(a, b, prec=None):
    return jnp.dot(a, b, precision=prec, preferred_element_type=F32)


def _dotb(a, b):
    return jnp.dot(a.astype(BF16), b.astype(BF16), preferred_element_type=F32)


def _dot_nt(a, b, prec=None):
    return lax.dot_general(a, b, (((1,), (1,)), ((), ())), precision=prec,
                           preferred_element_type=F32)


def _dot_tn(a, b, prec=None):
    return lax.dot_general(a, b, (((0,), (0,)), ((), ())), precision=prec,
                           preferred_element_type=F32)


def _dot3(a, b):
    ah, al = _split_bf16(a)
    bh, bl = _split_bf16(b)
    return _dot(ah, bh) + (_dot(ah, bl) + _dot(al, bh))


def _silu(x):
    return x * jax.nn.sigmoid(x)


def _ada_kernel(c_ref, w_ref, b_ref, o_ref):
    cs = _silu(c_ref[...])
    o_ref[0] = _dot3(cs, w_ref[0]) + b_ref[0]


def _ada(c_all, w_ada, b_ada):
    n = c_all.shape[0]
    tn = 1536
    return pl.pallas_call(
        _ada_kernel,
        out_shape=jax.ShapeDtypeStruct((DEPTH, n, 6 * D_MODEL), F32),
        grid=(DEPTH, 6 * D_MODEL // tn),
        in_specs=[pl.BlockSpec((n, D_MODEL), lambda l, j: (0, 0)),
                  pl.BlockSpec((1, D_MODEL, tn), lambda l, j: (l, 0, j)),
                  pl.BlockSpec((1, 1, tn), lambda l, j: (l, 0, j))],
        out_specs=pl.BlockSpec((1, n, tn), lambda l, j: (l, 0, j)),
        compiler_params=_cp(("parallel", "parallel")),
        name="ada_mod",
    )(c_all, w_ada, b_ada.reshape(DEPTH, 1, 6 * D_MODEL))


def _proj_kernel(x_ref, g_ref, sc_ref, sh_ref, w_ref, wg_ref, wab_ref,
                 u_ref, qkv_ref, z_ref, ga_ref, gb_ref, ab_ref, h_scr):
    j = pl.program_id(2)

    @pl.when(j == 0)
    def _():
        x = x_ref[0]
        ms = jnp.mean(x * x, axis=-1, keepdims=True)
        xn = x * lax.rsqrt(ms + NORM_EPS) * g_ref[0]
        h_scr[...] = (xn * (1.0 + sc_ref[0, 0]) + sh_ref[0, 0]).astype(h_scr.dtype)

    def mm(w):
        return _dot3(h_scr[...], w)

    @pl.when(j == 0)
    def _():
        res = mm(w_ref[0])
        for k in range(SLABS):
            u_ref[k, 0] = res[:, k * LANES:(k + 1) * LANES]

    @pl.when((j >= 1) & (j <= 3))
    def _():
        qkv_ref[0] = mm(w_ref[0])

    @pl.when(j == 4)
    def _():
        z_ref[0] = mm(w_ref[0])

    @pl.when((j == 5) | (j == 6))
    def _():
        ga_ref[0] = jax.nn.sigmoid(mm(wg_ref[0]))

    @pl.when((j == 7) | (j == 8))
    def _():
        gb_ref[0] = jax.nn.sigmoid(mm(wg_ref[0]))

    @pl.when(j == 9)
    def _():
        ab_ref[0] = mm(wab_ref[0])


def _mod_spec(mod, l, chunk, tm):
    per_row = mod.shape[2] != 1
    D = mod.shape[3] // 6

    def index(b, i, *_):
        return (l, b, i if per_row else 0, chunk)

    return pl.BlockSpec((1, 1, tm if per_row else 1, D), index)


def _proj(x, g, mod, w_in, w_gates, w_ab, *, l, tm):
    B, L, D = x.shape
    tn = 512
    clampi = lambda j, lo, n: jnp.clip(j - lo, 0, n - 1)
    outs = pl.pallas_call(
        _proj_kernel,
        out_shape=(jax.ShapeDtypeStruct((SLABS, B, L, LANES), F32),
                   jax.ShapeDtypeStruct((B, L, QKV_WIDTH), F32),
                   jax.ShapeDtypeStruct((B, L, GDN_WIDTH), F32),
                   jax.ShapeDtypeStruct((B, L, D), F32),
                   jax.ShapeDtypeStruct((B, L, D), F32),
                   jax.ShapeDtypeStruct((B, L, LANES), F32)),
        grid=(B, L // tm, 10),
        in_specs=[pl.BlockSpec((1, tm, D), lambda b, i, j: (b, i, 0)),
                  pl.BlockSpec((1, 1, D), lambda b, i, j: (l, 0, 0)),
                  _mod_spec(mod, l, 1, tm),
                  _mod_spec(mod, l, 0, tm),
                  pl.BlockSpec((1, D, tn), lambda b, i, j: (l, 0, jnp.minimum(j, 4))),
                  pl.BlockSpec((1, D, tn), lambda b, i, j: (l, 0, clampi(j, 5, 4))),
                  pl.BlockSpec((1, D, LANES), lambda b, i, j: (l, 0, 0))],
        out_specs=(pl.BlockSpec((SLABS, 1, tm, LANES), lambda b, i, j: (0, b, i, 0)),
                   pl.BlockSpec((1, tm, tn), lambda b, i, j: (b, i, clampi(j, 1, 3))),
                   pl.BlockSpec((1, tm, tn), lambda b, i, j: (b, i, 0)),
                   pl.BlockSpec((1, tm, tn), lambda b, i, j: (b, i, clampi(j, 5, 2))),
                   pl.BlockSpec((1, tm, tn), lambda b, i, j: (b, i, clampi(j, 7, 2))),
                   pl.BlockSpec((1, tm, LANES), lambda b, i, j: (b, i, 0))),
        scratch_shapes=[pltpu.VMEM((tm, D), F32)],
        compiler_params=_cp(("parallel", "parallel", "arbitrary")),
        name="norm_in_proj",
    )(x, g, mod, mod, w_in, w_gates, w_ab)
    return outs


def _proj_seq_kernel(x_ref, g_ref, sc_ref, sh_ref, w_ref, wg_ref, wab_ref,
                     u_ref, qkv_ref, z_ref, ga_ref, gb_ref, ab_ref, us_scr):
    x = x_ref[0]
    ms = jnp.mean(x * x, axis=-1, keepdims=True)
    xn = x * lax.rsqrt(ms + NORM_EPS) * g_ref[0]
    h = (xn * (1.0 + sc_ref[0, 0]) + sh_ref[0, 0]).astype(BF16)
    res = _dot(h, w_ref[0, :, 0:S5_WIDTH])
    nrow = res.shape[0] // S5_T
    for k in range(SLABS):
        us_scr[...] = res[:, k * LANES:(k + 1) * LANES]
        for t in range(S5_T):
            u_ref[k, 0, :, t * LANES:(t + 1) * LANES] = (
                us_scr[pl.ds(t, nrow, stride=S5_T), :].astype(u_ref.dtype))
    c0 = S5_WIDTH
    qkv_ref[0] = _dot(h, w_ref[0, :, c0:c0 + QKV_WIDTH]).astype(qkv_ref.dtype)
    c0 += QKV_WIDTH
    z_ref[0] = _dot(h, w_ref[0, :, c0:c0 + GDN_WIDTH]).astype(z_ref.dtype)
    D = x.shape[-1]
    ga_ref[0] = jax.nn.sigmoid(_dot(h, wg_ref[0, :, 0:D])).astype(ga_ref.dtype)
    gb_ref[0] = jax.nn.sigmoid(_dot(h, wg_ref[0, :, D:2 * D])).astype(gb_ref.dtype)
    ab_ref[0] = _dot(h, wab_ref[0])


def _proj_seq(x, g, mod, w_in, w_gates, w_ab, *, l, tm):
    B, L, D = x.shape
    n_main = S5_WIDTH + QKV_WIDTH + GDN_WIDTH
    row = lambda w: pl.BlockSpec((1, tm, w), lambda b, i: (b, i, 0))
    return pl.pallas_call(
        _proj_seq_kernel,
        out_shape=(jax.ShapeDtypeStruct((SLABS, B, L // S5_T, S5_T * LANES), BF16),
                   jax.ShapeDtypeStruct((B, L, QKV_WIDTH), BF16),
                   jax.ShapeDtypeStruct((B, L, GDN_WIDTH), BF16),
                   jax.ShapeDtypeStruct((B, L, D), BF16),
                   jax.ShapeDtypeStruct((B, L, D), BF16),
                   jax.ShapeDtypeStruct((B, L, LANES), F32)),
        grid=(B, L // tm),
        in_specs=[row(D),
                  pl.BlockSpec((1, 1, D), lambda b, i: (l, 0, 0)),
                  _mod_spec(mod, l, 1, tm),
                  _mod_spec(mod, l, 0, tm),
                  pl.BlockSpec((1, D, n_main), lambda b, i: (l, 0, 0)),
                  pl.BlockSpec((1, D, 2 * D), lambda b, i: (l, 0, 0)),
                  pl.BlockSpec((1, D, LANES), lambda b, i: (l, 0, 0))],
        out_specs=(pl.BlockSpec((SLABS, 1, tm // S5_T, S5_T * LANES), lambda b, i: (0, b, i, 0)),
                   row(QKV_WIDTH), row(GDN_WIDTH), row(D), row(D), row(LANES)),
        scratch_shapes=[pltpu.VMEM((tm, LANES), F32)],
        compiler_params=_cp(("parallel", "parallel")),
        name="norm_in_proj_seq",
    )(x, g, mod, mod, w_in, w_gates, w_ab)


GROUPS_PER_SLAB = S5_GROUPS // SLABS


def _s5_prep_kernel(lrb, lib, dtb, bre, bim, lrc, lic, dtc, cre, cim, lrn, lin, dtn,
                    be_ref, bst_ref, c0_ref, cpm_ref, tp_ref, pt_ref, a1_ref, cpe_scr, *, seg):
    W = SLAB_STATE

    def disc(lr, li, ldt):
        dt = jnp.exp(ldt)
        mag = jnp.exp(lr * dt)
        return mag * jnp.cos(li * dt), mag * jnp.sin(li * dt)

    def cmul(xr, xi, yr, yi):
        return xr * yr - xi * yi, xr * yi + xi * yr

    lr, li = lrb[0], lib[0]
    ar, ai = disc(lr, li, dtb[0])
    den = lr * lr + li * li
    nr = ar - 1.0
    kr = (nr * lr + ai * li) / den
    ki = (ai * lr - nr * li) / den
    br, bi = bre[0], bim[0]
    bbr = kr * br - ki * bi
    bbi = kr * bi + ki * br
    rgrp = lax.broadcasted_iota(jnp.int32, (LANES, LANES), 0) // S5_GROUP
    lane_hi = lax.broadcasted_iota(jnp.int32, (LANES, LANES), 1) // S5_STATE
    pr, pi = jnp.ones_like(ar), jnp.zeros_like(ar)
    for d in range(S5_T):
        t = S5_T - 1 - d
        for ri, val in enumerate(cmul(pr, pi, bbr, bbi)):
            two = jnp.concatenate([val, val], axis=1)
            for m in range(GROUPS_PER_SLAB // 2):
                tile = jnp.where(rgrp == 2 * m + lane_hi, two, 0.0)
                c0 = ri * W + m * LANES
                be_ref[0, 0, t * LANES:(t + 1) * LANES, c0:c0 + LANES] = tile.astype(BF16)
                if d == 0:
                    bst_ref[0, 0, :, c0:c0 + LANES] = tile
        pr, pi = cmul(pr, pi, ar, ai)

    ar, ai = disc(lrc[0], lic[0], dtc[0])
    cr, ci = cre[0], cim[0]
    own = (lax.broadcasted_iota(jnp.int32, (W, LANES), 0) // S5_STATE
           == lax.broadcasted_iota(jnp.int32, (W, LANES), 1) // S5_GROUP)
    pr, pi = jnp.ones_like(ar), jnp.zeros_like(ar)
    for d in range(S5_T + 1):
        vr, vi = cmul(cr, ci, pr, pi)
        for ri, val in enumerate((vr, -vi)):
            tile = jnp.where(own, val, 0.0)
            cpe_scr[d, ri * W:(ri + 1) * W, :] = tile
            if d == 0:
                c0_ref[0, 0, ri * W:(ri + 1) * W, :] = tile
            if d >= 1:
                cpm_ref[0, 0, ri * W:(ri + 1) * W, (d - 1) * LANES:d * LANES] = tile.astype(BF16)
        pr, pi = cmul(pr, pi, ar, ai)

    bst = bst_ref[0, 0]
    lag = [_dot(bst, cpe_scr[d], HI).astype(BF16) for d in range(S5_T)]
    for dd in range(S5_T // 2):
        tp_ref[0, 0, dd, 0:LANES, 0:LANES] = lag[2 * dd]
        tp_ref[0, 0, dd, LANES:, LANES:] = lag[2 * dd]
        tp_ref[0, 0, dd, 0:LANES, LANES:] = lag[2 * dd + 1]
        tp_ref[0, 0, dd, LANES:, 0:LANES] = lag[2 * dd - 1] if dd else jnp.zeros((LANES, LANES), BF16)

    ar, ai = disc(lrn[0, 0], lin[0, 0], dtn[0, 0])
    a1_ref[0, 0, :, 0:W] = ar
    a1_ref[0, 0, :, W:2 * W] = ai
    tr, ti = ar, ai
    for _ in range(S5_T - 1):
        tr, ti = cmul(tr, ti, ar, ai)
    pr, pi = jnp.ones_like(ar), jnp.zeros_like(ar)
    for i in range(seg + 1):
        pt_ref[0, 0, i:i + 1, 0:W] = pr
        pt_ref[0, 0, i:i + 1, W:2 * W] = pi
        pr, pi = cmul(pr, pi, tr, ti)


def _s5_prep(lam_re, lam_im, log_dt, b_re, b_im, c_re, c_im, seg):
    G, P, C = S5_GROUPS, S5_STATE, S5_GROUP
    W2 = 2 * SLAB_STATE
    dt3 = jnp.broadcast_to(log_dt[:, :, None], (DEPTH, G, P))
    rows_b = lambda a: jnp.repeat(a, C, axis=1)
    bt = lambda a: a.transpose(0, 1, 3, 2).reshape(DEPTH, G * C, P)
    rows_c = lambda a: jnp.broadcast_to(a.reshape(DEPTH, G * P, 1), (DEPTH, G * P, LANES))
    ct = lambda a: jnp.tile(a.transpose(0, 1, 3, 2).reshape(DEPTH, G * P, C), (1, 1, LANES // C))
    nat = lambda a: a.reshape(DEPTH, SLABS, 1, SLAB_STATE)
    args = (rows_b(lam_re), rows_b(lam_im), rows_b(dt3), bt(b_re), bt(b_im),
            rows_c(lam_re), rows_c(lam_im), rows_c(dt3), ct(c_re), ct(c_im),
            nat(lam_re), nat(lam_im), nat(dt3))
    bspec = pl.BlockSpec((1, LANES, P), lambda l, k: (l, k, 0))
    cspec = pl.BlockSpec((1, SLAB_STATE, LANES), lambda l, k: (l, k, 0))
    nspec = pl.BlockSpec((1, 1, 1, SLAB_STATE), lambda l, k: (l, k, 0, 0))
    return pl.pallas_call(
        functools.partial(_s5_prep_kernel, seg=seg),
        out_shape=(jax.ShapeDtypeStruct((DEPTH, SLABS, S5_T * LANES, W2), BF16),
                   jax.ShapeDtypeStruct((DEPTH, SLABS, LANES, W2), F32),
                   jax.ShapeDtypeStruct((DEPTH, SLABS, W2, LANES), F32),
                   jax.ShapeDtypeStruct((DEPTH, SLABS, W2, S5_T * LANES), BF16),
                   jax.ShapeDtypeStruct((DEPTH, SLABS, S5_T // 2, 2 * LANES, 2 * LANES), BF16),
                   jax.ShapeDtypeStruct((DEPTH, SLABS, seg + 1, W2), F32),
                   jax.ShapeDtypeStruct((DEPTH, SLABS, 1, W2), F32)),
        grid=(DEPTH, SLABS),
        in_specs=[bspec] * 5 + [cspec] * 5 + [nspec] * 3,
        out_specs=(pl.BlockSpec((1, 1, S5_T * LANES, W2), lambda l, k: (l, k, 0, 0)),
                   pl.BlockSpec((1, 1, LANES, W2), lambda l, k: (l, k, 0, 0)),
                   pl.BlockSpec((1, 1, W2, LANES), lambda l, k: (l, k, 0, 0)),
                   pl.BlockSpec((1, 1, W2, S5_T * LANES), lambda l, k: (l, k, 0, 0)),
                   pl.BlockSpec((1, 1, S5_T // 2, 2 * LANES, 2 * LANES), lambda l, k: (l, k, 0, 0, 0)),
                   pl.BlockSpec((1, 1, seg + 1, W2), lambda l, k: (l, k, 0, 0)),
                   pl.BlockSpec((1, 1, 1, W2), lambda l, k: (l, k, 0, 0))),
        scratch_shapes=[pltpu.VMEM((S5_T + 1, W2, LANES), F32)],
        compiler_params=_cp(("parallel", "parallel")),
        name="s5_discretize",
    )(*args)


def _s5_seq_kernel(up_ref, be_ref, tp_ref, cpm_ref, pt_ref, s0_ref, dsk_ref,
                   yg_ref, sfin_ref, e_scr, sx_scr, *, nc):
    seg = nc // 8
    W = SLAB_STATE
    nt = W // LANES
    ub = up_ref[0, 0]
    u = ub.astype(F32)
    e = _dot(ub, be_ref[0, 0])
    pitch = e_scr.shape[1] // 8
    for c in range(2 * nt):
        for j in range(8):
            e_scr[c, j * pitch:j * pitch + seg, :] = e[j * seg:(j + 1) * seg, c * LANES:(c + 1) * LANES]

    def tiles(row):
        return [(row[:, c * LANES:(c + 1) * LANES], row[:, W + c * LANES:W + (c + 1) * LANES])
                for c in range(nt)]

    a8 = [(jnp.broadcast_to(r, (8, LANES)), jnp.broadcast_to(i, (8, LANES)))
          for r, i in tiles(pt_ref[0, 0, 1:2, :])]

    def step(i, carry):
        rows = pl.ds(i, 8, stride=pitch)
        new = []
        for c in range(nt):
            sr, si = carry[c]
            ar, ai = a8[c]
            sx_scr[c, rows, :] = sr
            sx_scr[nt + c, rows, :] = si
            new.append((ar * sr - ai * si + e_scr[c, rows, :],
                        ar * si + ai * sr + e_scr[nt + c, rows, :]))
        return tuple(new)

    zero = jnp.zeros((8, LANES), F32)
    ends = lax.fori_loop(0, seg, step, tuple((zero, zero) for _ in range(nt)))

    al = tiles(pt_ref[0, 0, seg:seg + 1, :])
    cur = tiles(s0_ref[0, 0])
    car = []
    for c in range(nt):
        alr, ali = al[c]
        cr, ci = cur[c]
        sr, si = ends[c]
        crs, cis = [], []
        for j in range(8):
            crs.append(cr)
            cis.append(ci)
            cr, ci = (alr * cr - ali * ci + sr[j:j + 1], alr * ci + ali * cr + si[j:j + 1])
        sfin_ref[0, 0, :, c * LANES:(c + 1) * LANES] = cr
        sfin_ref[0, 0, :, W + c * LANES:W + (c + 1) * LANES] = ci
        car.append((jnp.concatenate(crs, axis=0), jnp.concatenate(cis, axis=0)))

    def corr(i, _):
        rows = pl.ds(i, 8, stride=pitch)
        pw = tiles(pt_ref[0, 0, pl.ds(i, 1), :])
        for c in range(nt):
            pr, pi = pw[c]
            cr, ci = car[c]
            sx_scr[c, rows, :] = sx_scr[c, rows, :] + (pr * cr - pi * ci)
            sx_scr[nt + c, rows, :] = sx_scr[nt + c, rows, :] + (pr * ci + pi * cr)
        return 0

    lax.fori_loop(0, seg, corr, 0)

    sx = jnp.concatenate(
        [jnp.concatenate([sx_scr[c, j * pitch:j * pitch + seg, :] for j in range(8)], axis=0)
         for c in range(2 * nt)], axis=-1)
    y = _dot(sx.astype(BF16), cpm_ref[0, 0])
    TW = 2 * LANES
    for tq in range(S5_T // 2):
        acc = y[:, tq * TW:(tq + 1) * TW]
        for tpi in range(tq + 1):
            acc = acc + _dot(ub[:, tpi * TW:(tpi + 1) * TW], tp_ref[0, 0, tq - tpi])
        acc = acc + dsk_ref[0, :, tq * TW:(tq + 1) * TW] * u[:, tq * TW:(tq + 1) * TW]
        yg_ref[0, 0, :, tq * TW:(tq + 1) * TW] = jax.nn.gelu(acc).astype(yg_ref.dtype)


def _s5_seq(up, be_emb, tp, cpm, pt, s0, dsk, l):
    _, B, nc, _ = up.shape
    seg = nc // 8
    W2 = 2 * SLAB_STATE
    yg, sfin = pl.pallas_call(
        functools.partial(_s5_seq_kernel, nc=nc),
        out_shape=(jax.ShapeDtypeStruct((SLABS, B, nc, S5_T * LANES), BF16),
                   jax.ShapeDtypeStruct((SLABS, B, 1, W2), F32)),
        grid=(SLABS, B),
        in_specs=[pl.BlockSpec((1, 1, nc, S5_T * LANES), lambda k, b: (k, b, 0, 0)),
                  pl.BlockSpec((1, 1, S5_T * LANES, W2), lambda k, b: (l, k, 0, 0)),
                  pl.BlockSpec((1, 1, S5_T // 2, 2 * LANES, 2 * LANES), lambda k, b: (l, k, 0, 0, 0)),
                  pl.BlockSpec((1, 1, W2, S5_T * LANES), lambda k, b: (l, k, 0, 0)),
                  pl.BlockSpec((1, 1, seg + 1, W2), lambda k, b: (l, k, 0, 0)),
                  pl.BlockSpec((1, 1, 1, W2), lambda k, b: (k, b, 0, 0)),
                  pl.BlockSpec((1, 1, S5_T * LANES), lambda k, b: (k, 0, 0))],
        out_specs=(pl.BlockSpec((1, 1, nc, S5_T * LANES), lambda k, b: (k, b, 0, 0)),
                   pl.BlockSpec((1, 1, 1, W2), lambda k, b: (k, b, 0, 0))),
        scratch_shapes=[pltpu.VMEM((W2 // LANES, 8 * (seg + S5_SEG_PAD), LANES), F32),
                        pltpu.VMEM((W2 // LANES, 8 * (seg + S5_SEG_PAD), LANES), F32)],
        compiler_params=_cp(("parallel", "parallel")),
        name="s5_seq",
    )(up, be_emb, tp, cpm, pt, s0, dsk)
    return yg, sfin


def _s5_step_kernel(u_ref, b_ref, c_ref, a_ref, s0_ref, d_ref, yg_ref, s1_ref):
    W = SLAB_STATE
    u = u_ref[0]
    bu = _dot(u, b_ref[0, 0], HI)
    ar = a_ref[0, 0, :, 0:W]
    ai = a_ref[0, 0, :, W:2 * W]
    sr = s0_ref[0, :, 0:W]
    si = s0_ref[0, :, W:2 * W]
    nr = ar * sr - ai * si + bu[:, 0:W]
    ni = ar * si + ai * sr + bu[:, W:2 * W]
    s1_ref[0, :, 0:W] = nr
    s1_ref[0, :, W:2 * W] = ni
    s1 = jnp.concatenate([nr, ni], axis=-1)
    y = _dot(s1, c_ref[0, 0], HI) + d_ref[0] * u
    yg_ref[0] = jax.nn.gelu(y)


def _s5_step(u_slab, bst, c0, a1, s0, d1, l):
    _, N, _ = u_slab.shape
    W2 = 2 * SLAB_STATE
    return pl.pallas_call(
        _s5_step_kernel,
        out_shape=(jax.ShapeDtypeStruct((SLABS, N, LANES), F32),
                   jax.ShapeDtypeStruct((SLABS, N, W2), F32)),
        grid=(SLABS,),
        in_specs=[pl.BlockSpec((1, N, LANES), lambda k: (k, 0, 0)),
                  pl.BlockSpec((1, 1, LANES, W2), lambda k: (l, k, 0, 0)),
                  pl.BlockSpec((1, 1, W2, LANES), lambda k: (l, k, 0, 0)),
                  pl.BlockSpec((1, 1, 1, W2), lambda k: (l, k, 0, 0)),
                  pl.BlockSpec((1, N, W2), lambda k: (k, 0, 0)),
                  pl.BlockSpec((1, 1, LANES), lambda k: (k, 0, 0))],
        out_specs=(pl.BlockSpec((1, N, LANES), lambda k: (k, 0, 0)),
                   pl.BlockSpec((1, N, W2), lambda k: (k, 0, 0))),
        compiler_params=_cp(("parallel",)),
        name="s5_step",
    )(u_slab, bst, c0, a1, s0, d1)


def _l2n(x):
    return x * lax.rsqrt(jnp.sum(x * x, axis=-1, keepdims=True) + L2_EPS)


def _split_bf16(x):
    hi = x.astype(BF16)
    return hi, (x - hi.astype(F32)).astype(BF16)


def _unit_lower_solve(As, rhss):
    n = GDN_C
    row = lax.broadcasted_iota(jnp.int32, (n, n), 0)
    col = lax.broadcasted_iota(jnp.int32, (n, n), 1)
    eye = (row == col).astype(F32)
    same8 = (row // 8) == (col // 8)
    Qs = [jnp.where(same8, -A, 0.0) for A in As]
    invs = [eye + Q for Q in Qs]
    for _ in range(2):
        Qs = [_dotb(Q, Q) for Q in Qs]
        invs = [inv + _dotb(inv, Q) for inv, Q in zip(invs, Qs)]
    s = 8
    while s < n:
        sib = ((row // (2 * s)) == (col // (2 * s))) & ((row // s) != (col // s))
        offs = [jnp.where(sib, A, 0.0).astype(BF16) for A in As]
        invb = [inv.astype(BF16) for inv in invs]
        tmp = [_dot(off, ib) for off, ib in zip(offs, invb)]
        invs = [inv - _dot(ib, t.astype(BF16)) for inv, ib, t in zip(invs, invb, tmp)]
        s *= 2
    invb = [inv.astype(BF16) for inv in invs]
    x0s = [_dot(ib, rhs.astype(BF16)) for ib, rhs in zip(invb, rhss)]
    res = []
    for A, x0, rhs in zip(As, x0s, rhss):
        ah, al = _split_bf16(A)
        xh, xl = _split_bf16(x0)
        res.append(rhs - x0 - (_dot(ah, xh) + _dot(ah, xl) + _dot(al, xh)))
    return [x0 + _dot(ib, r.astype(BF16)) for x0, ib, r in zip(x0s, invb, res)]


def _gdn_tile(qc_scr, gc, beta, z_ref, nw, o_ref, s_scr, tl, r0):
    C, DK, H = GDN_C, GDN_DK, GDN_HEADS
    nchunk = tl // C
    probs = [(c, h) for c in range(nchunk) for h in range(H)]
    row = lax.broadcasted_iota(jnp.int32, (C, C), 0)
    col = lax.broadcasted_iota(jnp.int32, (C, C), 1)
    tri = row >= col
    strict = row > col

    def blk(c, off):
        return qc_scr[c * C:(c + 1) * C, off:off + DK]

    q = [_l2n(blk(c, h * DK)) * (DK ** -0.5) for c, h in probs]
    k = [_l2n(blk(c, GDN_WIDTH + h * DK)) for c, h in probs]
    v = [blk(c, 2 * GDN_WIDTH + h * DK) for c, h in probs]
    gcb = [jnp.broadcast_to(gc[c * C:(c + 1) * C, h:h + 1], (C, DK)) for c, h in probs]
    bb = [jnp.broadcast_to(beta[c * C:(c + 1) * C, H + h:H + h + 1], (C, DK)) for c, h in probs]
    decay = []
    for g in gcb:
        diff = g - g.T
        decay.append(jnp.where(tri, jnp.exp(jnp.where(tri, diff, 0.0)), 0.0))
    kbf = [x.astype(BF16) for x in k]
    kb = [x * b for x, b in zip(k, bb)]
    A = [jnp.where(strict, _dot_nt(x.astype(BF16), y) * d, 0.0) for x, y, d in zip(kb, kbf, decay)]
    egc = [jnp.exp(g) for g in gcb]
    rhs = [jnp.concatenate([x * b, y * e], axis=-1) for x, b, y, e in zip(v, bb, kb, egc)]
    sol = _unit_lower_solve(A, rhs)
    attn = [jnp.where(tri, _dot_nt(x.astype(BF16), y) * d, 0.0).astype(BF16)
            for x, y, d in zip(q, kbf, decay)]
    glast = [g[C - 1:C, :] for g in gcb]
    wq = [jnp.concatenate([s[:, DK:], x * e], axis=0).astype(BF16) for s, x, e in zip(sol, q, egc)]
    kg = [(x * jnp.exp(gl - g)).astype(BF16) for x, gl, g in zip(k, glast, gcb)]

    for c in range(nchunk):
        ps = [c * H + h for h in range(H)]
        S = [s_scr[h] for h in range(H)]
        ws = [_dot(wq[p], S[h].astype(BF16)) for h, p in enumerate(ps)]
        v_new = [sol[p][:, 0:DK] - w[0:C] for p, w in zip(ps, ws)]
        vb = [x.astype(BF16) for x in v_new]
        o = [w[C:] + _dot(attn[p], x) for p, w, x in zip(ps, ws, vb)]
        for h, p in enumerate(ps):
            s_scr[h] = S[h] * jnp.exp(glast[p]) + _dot_tn(kg[p], vb[h])
            zh = z_ref[0, r0 + c * C:r0 + (c + 1) * C, h * DK:(h + 1) * DK].astype(F32)
            on = o[h] * lax.rsqrt(jnp.mean(o[h] * o[h], axis=-1, keepdims=True) + NORM_EPS) * nw
            o_ref[0, r0 + c * C:r0 + (c + 1) * C, h * DK:(h + 1) * DK] = (on * _silu(zh)).astype(o_ref.dtype)


def _gdn_seq_kernel(qkv_ref, z_ref, ab_ref, cw_ref, alog_ref, dtb_ref, nw_ref, conv0_ref, s0_ref,
                    o_ref, sfin_ref, xp_scr, qc_scr, s_scr, *, tl):
    lt = pl.program_id(1)

    @pl.when(lt == 0)
    def _():
        xp_scr[0:8, :] = jnp.zeros((8, QKV_WIDTH), F32)
        xp_scr[8 - (GDN_CONV - 1):8, :] = conv0_ref[0]
        s_scr[...] = s0_ref[0]

    sub = qc_scr.shape[0]
    for r0 in range(0, tl, sub):
        xp_scr[8:8 + sub, :] = qkv_ref[0, r0:r0 + sub, :].astype(F32)
        conv = cw_ref[0, 0:1, :] * xp_scr[5:5 + sub, :]
        for j in range(1, GDN_CONV):
            conv = conv + cw_ref[0, j:j + 1, :] * xp_scr[5 + j:5 + j + sub, :]
        xp_scr[0:8, :] = xp_scr[sub:sub + 8, :]
        qc_scr[...] = _silu(conv)

        ab = ab_ref[0, r0:r0 + sub, :]
        g = -jnp.exp(alog_ref[...]) * jax.nn.softplus(ab + dtb_ref[...])
        beta = jax.nn.sigmoid(ab)
        row = lax.broadcasted_iota(jnp.int32, (sub, sub), 0)
        col = lax.broadcasted_iota(jnp.int32, (sub, sub), 1)
        csum = ((row >= col) & ((row // GDN_C) == (col // GDN_C))).astype(F32)
        gc = _dot(csum, g, HI)
        _gdn_tile(qc_scr, gc, beta, z_ref, nw_ref[...], o_ref, s_scr, sub, r0)

    @pl.when(lt == pl.num_programs(1) - 1)
    def _():
        sfin_ref[0] = s_scr[...]


def _gdn_seq(qkv, z, ab, conv_w, alog, dtb, nw, conv0, s0, l):
    B, L, _ = qkv.shape
    tl = min(GDN_BLOCK, L)
    sub = min(GDN_SUB, tl)
    return pl.pallas_call(
        functools.partial(_gdn_seq_kernel, tl=tl),
        out_shape=(jax.ShapeDtypeStruct((B, L, GDN_WIDTH), BF16),
                   jax.ShapeDtypeStruct((B, GDN_HEADS, GDN_DK, GDN_DK), F32)),
        grid=(B, L // tl),
        in_specs=[pl.BlockSpec((1, tl, QKV_WIDTH), lambda b, i: (b, i, 0)),
                  pl.BlockSpec((1, tl, GDN_WIDTH), lambda b, i: (b, i, 0)),
                  pl.BlockSpec((1, tl, LANES), lambda b, i: (b, i, 0)),
                  pl.BlockSpec((1, GDN_CONV, QKV_WIDTH), lambda b, i: (l, 0, 0)),
                  pl.BlockSpec((1, LANES), lambda b, i: (0, 0)),
                  pl.BlockSpec((1, LANES), lambda b, i: (0, 0)),
                  pl.BlockSpec((1, GDN_DK), lambda b, i: (0, 0)),
                  pl.BlockSpec((1, GDN_CONV - 1, QKV_WIDTH), lambda b, i: (b, 0, 0)),
                  pl.BlockSpec((1, GDN_HEADS, GDN_DK, GDN_DK), lambda b, i: (b, 0, 0, 0))],
        out_specs=(pl.BlockSpec((1, tl, GDN_WIDTH), lambda b, i: (b, i, 0)),
                   pl.BlockSpec((1, GDN_HEADS, GDN_DK, GDN_DK), lambda b, i: (b, 0, 0, 0))),
        scratch_shapes=[pltpu.VMEM((sub + 8, QKV_WIDTH), F32),
                        pltpu.VMEM((sub, QKV_WIDTH), F32),
                        pltpu.VMEM((GDN_HEADS, GDN_DK, GDN_DK), F32)],
        compiler_params=_cp(("parallel", "arbitrary")),
        name="gdn_seq",
    )(qkv, z, ab, conv_w, alog, dtb, nw, conv0, s0)


GDN_STEP_ROWS = 8


def _gdn_step_kernel(qkv_ref, z_ref, ab_ref, cw_ref, alog_ref, dtb_ref, nw_ref, conv0_ref, s0_ref,
                     *rest):
    if len(rest) == 3:
        prev_ref, o_ref, s1_all = rest
        s1_all[0] = prev_ref[...]
        s1_ref = s1_all.at[1]
    else:
        o_ref, s1_all = rest
        s1_ref = s1_all
    nb = GDN_STEP_ROWS
    W = QKV_WIDTH
    conv = cw_ref[0:1, :] * conv0_ref[:, 0:W]
    conv = conv + cw_ref[1:2, :] * conv0_ref[:, W:2 * W]
    conv = conv + cw_ref[2:3, :] * conv0_ref[:, 2 * W:3 * W]
    conv = conv + cw_ref[3:4, :] * qkv_ref[...]
    qc = _silu(conv)
    ab = ab_ref[...]
    eg = jnp.exp(-jnp.exp(alog_ref[...]) * jax.nn.softplus(ab + dtb_ref[...]))
    beta = jax.nn.sigmoid(ab)
    eye = (lax.broadcasted_iota(jnp.int32, (GDN_DK, GDN_DK), 0)
           == lax.broadcasted_iota(jnp.int32, (GDN_DK, GDN_DK), 1)).astype(F32)
    for h in range(GDN_HEADS):
        q = _l2n(qc[:, h * GDN_DK:(h + 1) * GDN_DK]) * (GDN_DK ** -0.5)
        k = _l2n(qc[:, GDN_WIDTH + h * GDN_DK:GDN_WIDTH + (h + 1) * GDN_DK])
        v = qc[:, 2 * GDN_WIDTH + h * GDN_DK:2 * GDN_WIDTH + (h + 1) * GDN_DK]
        kT = _dot_nt(eye, k, HI)
        qT = _dot_nt(eye, q, HI)
        qk = jnp.sum(q * k, axis=-1, keepdims=True)
        for j in range(nb):
            S = s0_ref[0, j, h]
            kc = jnp.broadcast_to(kT[:, j:j + 1], (GDN_DK, GDN_DK))
            qcb = jnp.broadcast_to(qT[:, j:j + 1], (GDN_DK, GDN_DK))
            kS = jnp.sum(kc * S, axis=0, keepdims=True)
            qS = jnp.sum(qcb * S, axis=0, keepdims=True)
            egj = eg[j:j + 1, h:h + 1]
            bj = beta[j:j + 1, GDN_HEADS + h:GDN_HEADS + h + 1]
            v_new = bj * v[j:j + 1, :] - (bj * egj) * kS
            o = egj * qS + qk[j:j + 1, :] * v_new
            s1_ref[j, h] = S * egj + kc * v_new
            zh = z_ref[j:j + 1, h * GDN_DK:(h + 1) * GDN_DK]
            on = o * lax.rsqrt(jnp.mean(o * o, axis=-1, keepdims=True) + NORM_EPS) * nw_ref[...]
            o_ref[j:j + 1, h * GDN_DK:(h + 1) * GDN_DK] = on * _silu(zh)


def _gdn_step(qkv, z, ab, conv_w, alog, dtb, nw, conv0, s_all, l, prev):
    N = qkv.shape[0]
    nb = GDN_STEP_ROWS
    row = lambda w: pl.BlockSpec((nb, w), lambda i: (i, 0))
    const = lambda r, w: pl.BlockSpec((r, w), lambda i: (0, 0))
    sblk = (nb, GDN_HEADS, GDN_DK, GDN_DK)
    one = pl.BlockSpec(sblk, lambda i: (i, 0, 0, 0))
    ins = [qkv, z, ab, conv_w, alog, dtb, nw, conv0, s_all]
    in_specs = [row(QKV_WIDTH), row(GDN_WIDTH), row(LANES), const(GDN_CONV, QKV_WIDTH),
                const(1, LANES), const(1, LANES), const(1, GDN_DK), row(3 * QKV_WIDTH),
                pl.BlockSpec((1,) + sblk, lambda i: (l, i, 0, 0, 0))]
    if prev is None:
        s_shape, s_spec = jax.ShapeDtypeStruct((N,) + sblk[1:], F32), one
    else:
        assert DEPTH == 2 and l == 1
        ins.append(prev)
        in_specs.append(one)
        s_shape = jax.ShapeDtypeStruct((DEPTH, N) + sblk[1:], F32)
        s_spec = pl.BlockSpec((DEPTH,) + sblk, lambda i: (0, i, 0, 0, 0))
    return pl.pallas_call(
        _gdn_step_kernel,
        out_shape=(jax.ShapeDtypeStruct((N, GDN_WIDTH), F32), s_shape),
        grid=(N // nb,),
        in_specs=in_specs,
        out_specs=(row(GDN_WIDTH), s_spec),
        compiler_params=_cp(("parallel",)),
        name="gdn_step",
    )(*ins)


def _merge_kernel(yg_ref, og_ref, ga_ref, gb_ref, x_ref, gt_ref, wglu_ref, wgo_ref, wout_ref,
                  gf_ref, scf_ref, shf_ref, *rest, hi, chunked, routed):
    if routed:
        wr_ref, xo_ref, h_ref, lg_ref, *scr = rest
    else:
        xo_ref, h_ref, *scr = rest
    if chunked:
        y_scr = scr[-1]
        scr = scr[:-1]
        nrow = y_scr.shape[1] // S5_T
        for k in range(SLABS):
            for t in range(S5_T):
                y_scr[k, pl.ds(t, nrow, stride=S5_T), :] = (
                    yg_ref[k, 0, :, t * LANES:(t + 1) * LANES].astype(F32))
        y = jnp.concatenate([y_scr[k] for k in range(SLABS)], axis=-1)
    else:
        y = jnp.concatenate([yg_ref[k, 0] for k in range(SLABS)], axis=-1)
    if hi:
        wglu, wgo, wout = wglu_ref[0], wgo_ref[0], wout_ref[0]
        mm = _dot3
    else:
        wglu_s, wgo_s, wout_s = scr

        @pl.when((pl.program_id(0) == 0) & (pl.program_id(1) == 0))
        def _():
            wglu_s[...] = wglu_ref[0].astype(BF16)
            wgo_s[...] = wgo_ref[0].astype(BF16)
            wout_s[...] = wout_ref[0].astype(BF16)

        wglu, wgo, wout = wglu_s[...], wgo_s[...], wout_s[...]
        mm = lambda a, w: _dot(a.astype(BF16), w)

    glu = mm(y, wglu)
    branch_a = glu[:, 0:D_MODEL] * jax.nn.sigmoid(glu[:, D_MODEL:])
    branch_b = mm(og_ref[0], wgo)
    merged = ga_ref[0].astype(F32) * branch_a + gb_ref[0].astype(F32) * branch_b
    out = mm(merged, wout)
    x = x_ref[0] + gt_ref[0, 0] * out
    xo_ref[0] = x
    ms = jnp.mean(x * x, axis=-1, keepdims=True)
    h = x * lax.rsqrt(ms + NORM_EPS) * gf_ref[0]
    h = h * (1.0 + scf_ref[0, 0]) + shf_ref[0, 0]
    h_ref[0] = h.astype(h_ref.dtype)
    if routed:
        lg_ref[0] = _dot_nt(wr_ref[0], h, HI)


def _merge(yg, og, ga, gb, x, mod, wglu, wgo, wout, gf, wr, *, l, tm, hi, chunked, h_dtype):
    B, L, D = x.shape
    routed = l % 2 == 1
    row = lambda w: pl.BlockSpec((1, tm, w), lambda b, i: (b, i, 0))
    layer = lambda r, w, ll=l: pl.BlockSpec((1, r, w), lambda b, i: (ll, 0, 0))
    scratch = [] if hi else [pltpu.VMEM((S5_WIDTH, 2 * D), BF16), pltpu.VMEM((GDN_WIDTH, D), BF16),
                             pltpu.VMEM((D, D), BF16)]
    if chunked:
        scratch = scratch + [pltpu.VMEM((SLABS, tm, LANES), F32)]
        yg_spec = pl.BlockSpec((SLABS, 1, tm // S5_T, S5_T * LANES), lambda b, i: (0, b, i, 0))
    else:
        yg_spec = pl.BlockSpec((SLABS, 1, tm, LANES), lambda b, i: (0, b, i, 0))
    lg_shape = jax.ShapeDtypeStruct((B, N_EXPERTS, L), F32)
    lg_spec = pl.BlockSpec((1, N_EXPERTS, tm), lambda b, i: (b, 0, i))
    outs = pl.pallas_call(
        functools.partial(_merge_kernel, hi=hi, chunked=chunked, routed=routed),
        out_shape=(jax.ShapeDtypeStruct((B, L, D), F32),
                   jax.ShapeDtypeStruct((B, L, D), h_dtype)) + ((lg_shape,) if routed else ()),
        grid=(B, L // tm),
        in_specs=[yg_spec,
                  row(GDN_WIDTH), row(D), row(D), row(D), _mod_spec(mod, l, 2, tm),
                  layer(S5_WIDTH, 2 * D), layer(GDN_WIDTH, D), layer(D, D),
                  layer(1, D), _mod_spec(mod, l, 4, tm), _mod_spec(mod, l, 3, tm)]
        + ([layer(N_EXPERTS, D, l // 2)] if routed else []),
        out_specs=(row(D), row(D)) + ((lg_spec,) if routed else ()),
        scratch_shapes=scratch,
        compiler_params=_cp(("arbitrary", "arbitrary")),
        name="merge_out_proj",
    )(yg, og, ga, gb, x, mod, wglu, wgo, wout, gf, mod, mod, *((wr,) if routed else ()))
    return outs if routed else (*outs, None)


FF_TILE = 512
FFN_ROWS = 512


def _finish(x, gfin_ref, final):
    if not final:
        return x
    ms = jnp.mean(x * x, axis=-1, keepdims=True)
    return x * lax.rsqrt(ms + NORM_EPS) * gfin_ref[...]


def _ffn_kernel(h_ref, x_ref, gt_ref, wg_ref, wu_ref, wd_ref, gfin_ref, o_ref, acc_scr, *, hi, final):
    j = pl.program_id(2)
    if hi:
        wg, wu, wd = wg_ref[...], wu_ref[...], wd_ref[...]
        mm = _dot3
    else:
        wg, wu, wd = wg_ref[...].astype(BF16), wu_ref[...].astype(BF16), wd_ref[...].astype(BF16)
        mm = lambda a, w: _dot(a.astype(BF16), w)

    @pl.when(j == 0)
    def _():
        acc_scr[...] = jnp.zeros_like(acc_scr)

    tm = acc_scr.shape[0]
    sub = min(FFN_ROWS, tm)
    for s in range(tm // sub):
        rows = slice(s * sub, (s + 1) * sub)
        hb = h_ref[0, rows, :]
        act = _silu(mm(hb, wg)) * mm(hb, wu)
        acc_scr[rows, :] = acc_scr[rows, :] + mm(act, wd)

    @pl.when(j == pl.num_programs(2) - 1)
    def _():
        o_ref[0] = _finish(x_ref[0] + gt_ref[0, 0] * acc_scr[...], gfin_ref, final)


def _ffn(h, x, mod, w_gu, w_down, gfin, *, l, tm, hi, final):
    B, L, D = x.shape
    nj = D_FF // FF_TILE
    row = pl.BlockSpec((1, tm, D), lambda b, i, j: (b, i, 0))
    return pl.pallas_call(
        functools.partial(_ffn_kernel, hi=hi, final=final),
        out_shape=jax.ShapeDtypeStruct((B, L, D), F32),
        grid=(B, L // tm, nj),
        in_specs=[row, row, _mod_spec(mod, l, 5, tm),
                  pl.BlockSpec((D, FF_TILE), lambda b, i, j: (0, j)),
                  pl.BlockSpec((D, FF_TILE), lambda b, i, j: (0, nj + j)),
                  pl.BlockSpec((FF_TILE, D), lambda b, i, j: (j, 0)),
                  pl.BlockSpec((1, D), lambda b, i, j: (0, 0))],
        out_specs=row,
        scratch_shapes=[pltpu.VMEM((tm, D), F32)],
        compiler_params=_cp(("parallel", "parallel", "arbitrary")),
        name="ffn_dense",
    )(h, x, mod, w_gu, w_gu, w_down, gfin)


ROUTE_TM = 512
ROW_DMA_TM = 512
MOE_SUP = 2048
MOE_SUB = 512
MOE_FF_TILE = 512


def _route_kernel(lg_ref, br_ref, cnt0_ref, slot_ref, wt_ref, cnt_ref, carry_scr, *, cap):
    @pl.when((pl.program_id(0) == 0) & (pl.program_id(1) == 0))
    def _():
        carry_scr[...] = cnt0_ref[...]

    lg = lg_ref[0] + br_ref[...]
    tm = lg.shape[1]
    eidx = lax.broadcasted_iota(jnp.int32, lg.shape, 0)
    m1 = jnp.max(lg, axis=0, keepdims=True)
    i1 = jnp.min(jnp.where(lg == m1, eidx, N_EXPERTS), axis=0, keepdims=True)
    lg2 = jnp.where(eidx == i1, -jnp.inf, lg)
    m2 = jnp.max(lg2, axis=0, keepdims=True)
    i2 = jnp.min(jnp.where(lg2 == m2, eidx, N_EXPERTS), axis=0, keepdims=True)
    e2 = jnp.exp(m2 - m1)
    wt_ref[0, 0:1, :] = 1.0 / (1.0 + e2)
    wt_ref[0, 1:2, :] = e2 / (1.0 + e2)
    sel1 = eidx == i1
    sel2 = eidx == i2
    oh = jnp.where(sel1 | sel2, 1.0, 0.0)
    before = (lax.broadcasted_iota(jnp.int32, (tm, tm), 0)
              < lax.broadcasted_iota(jnp.int32, (tm, tm), 1)).astype(BF16)
    rank = carry_scr[:, 0:1] + _dot(oh.astype(BF16), before)
    r1 = jnp.sum(jnp.where(sel1, rank, 0.0), axis=0, keepdims=True).astype(jnp.int32)
    r2 = jnp.sum(jnp.where(sel2, rank, 0.0), axis=0, keepdims=True).astype(jnp.int32)
    slot_ref[0, 0:1, :] = i1 * cap + r1
    slot_ref[0, 1:2, :] = i2 * cap + r2
    carry_scr[...] = carry_scr[...] + jnp.sum(oh, axis=1, keepdims=True)
    cnt_ref[...] = carry_scr[...]


def _route_slots(lgT, b_r, cnt0, cap):
    B, E, L = lgT.shape
    tm = min(ROUTE_TM, L)
    return pl.pallas_call(
        functools.partial(_route_kernel, cap=cap),
        out_shape=(jax.ShapeDtypeStruct((B, 2, L), jnp.int32),
                   jax.ShapeDtypeStruct((B, 2, L), F32),
                   jax.ShapeDtypeStruct((E, LANES), F32)),
        grid=(B, L // tm),
        in_specs=[pl.BlockSpec((1, E, tm), lambda b, i: (b, 0, i)),
                  pl.BlockSpec((E, 1), lambda b, i: (0, 0)),
                  pl.BlockSpec((E, LANES), lambda b, i: (0, 0))],
        out_specs=(pl.BlockSpec((1, 2, tm), lambda b, i: (b, 0, i)),
                   pl.BlockSpec((1, 2, tm), lambda b, i: (b, 0, i)),
                   pl.BlockSpec((E, LANES), lambda b, i: (0, 0))),
        scratch_shapes=[pltpu.VMEM((E, LANES), F32)],
        compiler_params=_cp(("arbitrary", "arbitrary")),
        name="moe_route",
    )(lgT, b_r.reshape(E, 1), cnt0)


def _row_copy(src, dst, sem):
    return pltpu.make_async_copy(src, dst, sem)


def _slot_rows_kernel(start_ref, per_ref, code_ref, row_ref, *, cap):
    code = code_ref[...]
    shift = cap.bit_length() - 1
    e = lax.shift_right_logical(code, shift)
    r = code & (cap - 1)
    start = jnp.zeros_like(code)
    per = jnp.ones_like(code)
    for k in range(N_EXPERTS):
        start = jnp.where(e == k, start_ref[k], start)
        per = jnp.where(e == k, per_ref[k], per)
    q = jnp.floor((r.astype(F32) + 0.5) / per.astype(F32)).astype(jnp.int32)
    row_ref[...] = start + q * MOE_SUP + (r - q * per)


def _slot_rows(start, per, codes, cap):
    B, _, L = codes.shape
    tm = min(ROUTE_TM, L)
    spec = pl.BlockSpec((1, 2, tm), lambda b, i, st, pe: (b, 0, i))
    return pl.pallas_call(
        functools.partial(_slot_rows_kernel, cap=cap),
        out_shape=jax.ShapeDtypeStruct(codes.shape, jnp.int32),
        grid_spec=pltpu.PrefetchScalarGridSpec(
            num_scalar_prefetch=2, grid=(B, L // tm), in_specs=[spec], out_specs=spec),
        compiler_params=_cp(("parallel", "parallel")),
        name="moe_slot_rows",
    )(start, per, codes)


def _zeros_kernel(o_ref):
    o_ref[...] = jnp.zeros_like(o_ref)


def _zero_rows(n_rows, width):
    return pl.pallas_call(
        _zeros_kernel,
        out_shape=jax.ShapeDtypeStruct((n_rows, width), F32),
        grid=(n_rows // MOE_SUP,),
        out_specs=pl.BlockSpec((MOE_SUP, width), lambda i: (i, 0)),
        compiler_params=_cp(("parallel",)),
        name="moe_zero_rows",
    )()


def _dispatch_kernel(row_ref, h_ref, xs_in_ref, xs_ref, hbuf, sem):
    del xs_in_ref
    tm = h_ref.shape[1]
    t = pl.program_id(0) * pl.num_programs(1) + pl.program_id(1)
    last = pl.num_programs(0) * pl.num_programs(1) - 1
    slot = t % 2
    hbuf[slot] = h_ref[0]

    def issue(r, _):
        for k in range(2):
            row = row_ref[0, k, r]
            _row_copy(hbuf.at[slot, pl.ds(r, 1), :], xs_ref.at[pl.ds(row, 1), :], sem.at[slot]).start()
        return 0

    lax.fori_loop(0, tm, issue, 0, unroll=8)

    def drain(sl):
        for k in range(2):
            _row_copy(hbuf.at[sl], xs_ref.at[pl.ds(0, tm), :], sem.at[sl]).wait()

    @pl.when(t > 0)
    def _():
        drain(1 - slot)

    @pl.when(t == last)
    def _():
        drain(slot)


def _dispatch(rows, h, xs):
    B, L, D = h.shape
    n_rows = xs.shape[0]
    tm = min(ROW_DMA_TM, L)
    return pl.pallas_call(
        _dispatch_kernel,
        out_shape=jax.ShapeDtypeStruct((n_rows, D), F32),
        grid=(B, L // tm),
        in_specs=[pl.BlockSpec((1, 2, tm), lambda b, i: (b, 0, i), memory_space=pltpu.SMEM),
                  pl.BlockSpec((1, tm, D), lambda b, i: (b, i, 0)),
                  pl.BlockSpec(memory_space=pl.ANY)],
        out_specs=pl.BlockSpec(memory_space=pl.ANY),
        scratch_shapes=[pltpu.VMEM((2, tm, D), F32), pltpu.SemaphoreType.DMA((2,))],
        input_output_aliases={2: 0},
        compiler_params=pltpu.CompilerParams(dimension_semantics=("arbitrary", "arbitrary"),
                                             vmem_limit_bytes=VMEM_LIMIT, disable_bounds_checks=True),
        name="moe_dispatch",
    )(rows, h, xs)


def _moe_grp_kernel(ge_ref, gn_ref, x_ref, wg_ref, wu_ref, wd_ref, y_ref, xb_scr):
    g = pl.program_id(0)
    j = pl.program_id(1)
    nsub = gn_ref[g]
    wg = wg_ref[0].astype(BF16)
    wu = wu_ref[0].astype(BF16)
    wd = wd_ref[0].astype(BF16)
    nblk = MOE_SUP // MOE_SUB

    @pl.when(j == 0)
    def _():
        xb_scr[...] = x_ref[...].astype(BF16)
        y_ref[...] = jnp.zeros_like(y_ref)

    def block(s):
        rows = slice(s * MOE_SUB, (s + 1) * MOE_SUB)
        xb = xb_scr[rows, :]
        act = _silu(_dot(xb, wg)) * _dot(xb, wu)
        y_ref[rows, :] = y_ref[rows, :] + _dot(act.astype(BF16), wd)

    for n in range(1, nblk + 1):
        @pl.when(nsub == n)
        def _():
            for s in range(n):
                block(s)


def _moe_groups(counts, n_groups):
    nsup = (counts + MOE_SUP - 1) // MOE_SUP
    div = jnp.maximum(nsup, 1)
    per = jnp.maximum(((counts + div - 1) // div + MOE_SUB - 1) // MOE_SUB * MOE_SUB, MOE_SUB)
    ends = jnp.cumsum(nsup)
    first = ends - nsup
    total = ends[-1]
    g = jnp.arange(n_groups, dtype=jnp.int32)
    gc = jnp.minimum(g, total - 1)
    e_of = jnp.minimum(jnp.sum((gc[:, None] >= ends[None, :]).astype(jnp.int32), axis=1), N_EXPERTS - 1)
    left = jnp.minimum(counts[e_of] - (gc - first[e_of]) * per[e_of], per[e_of])
    nsub = jnp.clip((left + MOE_SUB - 1) // MOE_SUB, 0, MOE_SUP // MOE_SUB)
    gn = jnp.where(g < total, nsub, 0).astype(jnp.int32)
    return e_of, gn, (first * MOE_SUP).astype(jnp.int32), per.astype(jnp.int32)


def _moe_grouped(xs, ge, gn, w_gu, w_down):
    D = xs.shape[1]
    nj = D_FF // MOE_FF_TILE
    ng = xs.shape[0] // MOE_SUP
    jj = lambda j, gn, g: jnp.where(gn[g] > 0, j, nj - 1)
    return pl.pallas_call(
        _moe_grp_kernel,
        out_shape=jax.ShapeDtypeStruct(xs.shape, F32),
        grid_spec=pltpu.PrefetchScalarGridSpec(
            num_scalar_prefetch=2,
            grid=(ng, nj),
            in_specs=[pl.BlockSpec((MOE_SUP, D), lambda g, j, ge, gn: (g, 0)),
                      pl.BlockSpec((1, D, MOE_FF_TILE), lambda g, j, ge, gn: (ge[g], 0, jj(j, gn, g))),
                      pl.BlockSpec((1, D, MOE_FF_TILE), lambda g, j, ge, gn: (ge[g], 0, nj + jj(j, gn, g))),
                      pl.BlockSpec((1, MOE_FF_TILE, D), lambda g, j, ge, gn: (ge[g], jj(j, gn, g), 0))],
            out_specs=pl.BlockSpec((MOE_SUP, D), lambda g, j, ge, gn: (g, 0)),
            scratch_shapes=[pltpu.VMEM((MOE_SUP, D), BF16)],
        ),
        compiler_params=_cp(("arbitrary", "arbitrary")),
        name="moe_experts",
    )(ge, gn, xs, w_gu, w_gu, w_down)


def _combine_kernel(row_ref, next_ref, w_ref, x_ref, gt_ref, gfin_ref, ys_ref, o_ref, g_scr, sem, *, final):
    tm = x_ref.shape[1]
    t = pl.program_id(0) * pl.num_programs(1) + pl.program_id(1)
    last = pl.num_programs(0) * pl.num_programs(1) - 1
    slot = t % 2

    def gather(rows, sl):
        def issue(r, _):
            for k in range(2):
                row = rows[0, k, r]
                _row_copy(ys_ref.at[pl.ds(row, 1), :], g_scr.at[sl, k, pl.ds(r, 1), :], sem.at[sl]).start()
            return 0

        lax.fori_loop(0, tm, issue, 0, unroll=8)

    @pl.when(t == 0)
    def _():
        gather(row_ref, slot)

    @pl.when(t < last)
    def _():
        gather(next_ref, 1 - slot)

    for k in range(2):
        _row_copy(ys_ref.at[pl.ds(0, tm), :], g_scr.at[slot, k], sem.at[slot]).wait()
    w = w_ref[0]
    f = w[:, 0:1] * g_scr[slot, 0] + w[:, 1:2] * g_scr[slot, 1]
    o_ref[0] = _finish(x_ref[0] + gt_ref[0, 0] * f, gfin_ref, final)


def _combine(rows, wts, x, mod, gfin, ys, *, l, final):
    B, L, D = x.shape
    tm = min(ROW_DMA_TM, L)
    gt_spec = _mod_spec(mod, l, 5, tm)
    row = pl.BlockSpec((1, tm, D), lambda b, i: (b, i, 0))
    nl = L // tm

    def next_block(b, i):
        t1 = jnp.minimum(b * nl + i + 1, B * nl - 1)
        return (t1 // nl, 0, t1 % nl)

    return pl.pallas_call(
        functools.partial(_combine_kernel, final=final),
        out_shape=jax.ShapeDtypeStruct((B, L, D), F32),
        grid=(B, L // tm),
        in_specs=[pl.BlockSpec((1, 2, tm), lambda b, i: (b, 0, i), memory_space=pltpu.SMEM),
                  pl.BlockSpec((1, 2, tm), next_block, memory_space=pltpu.SMEM),
                  pl.BlockSpec((1, tm, 2), lambda b, i: (b, i, 0)),
                  row, gt_spec,
                  pl.BlockSpec((1, D), lambda b, i: (0, 0)),
                  pl.BlockSpec(memory_space=pl.ANY)],
        out_specs=row,
        scratch_shapes=[pltpu.VMEM((2, 2, tm, D), F32), pltpu.SemaphoreType.DMA((2,))],
        compiler_params=pltpu.CompilerParams(dimension_semantics=("arbitrary", "arbitrary"),
                                             vmem_limit_bytes=VMEM_LIMIT, disable_bounds_checks=True),
        name="moe_combine",
    )(rows, rows, wts.transpose(0, 2, 1), x, mod, gfin, ys)


def _moe_routed(groups, b_r, w_gu, w_down, gfin, *, l, final):
    D = groups[0][1].shape[-1]
    n_tok = sum(g[1].shape[0] * g[1].shape[1] for g in groups)
    cap = 1 << (n_tok - 1).bit_length()
    n_groups = 2 * n_tok // MOE_SUP + N_EXPERTS
    cnt = jnp.zeros((N_EXPERTS, LANES), F32)
    routed = []
    for _, _, _, lgT in groups:
        codes, wts, cnt = _route_slots(lgT, b_r, cnt, cap)
        routed.append((codes, wts))
    ge, gn, start, per = _moe_groups(cnt[:, 0].astype(jnp.int32), n_groups)
    rows = [_slot_rows(start, per, codes, cap) for codes, _ in routed]
    xs = _zero_rows(n_groups * MOE_SUP, D)
    for (h, _, _, _), r in zip(groups, rows):
        xs = _dispatch(r, h, xs)
    ys = _moe_grouped(xs, ge, gn, w_gu, w_down)
    return [_combine(r, wts, x, mod, gfin, ys, l=l, final=final)
            for (_, x, mod, _), (_, wts), r in zip(groups, routed, rows)]


def _pad_lanes(v):
    return jnp.pad(v.reshape(1, -1), ((0, 0), (0, LANES - v.shape[-1])))


def _mixer_layer(x, mod, states, p, s5m, l, prev_sg, *, seq):
    B, L, D = x.shape
    hi = not seq
    s5r0, s5i0, sg0, sc0 = states
    w_in, w_gates, w_ab = p['w_in_seq' if seq else 'w_in']
    if seq:
        u, qkv, z, ga, gb, ab = _proj_seq(x, p['g_mix'], mod, w_in, w_gates, w_ab, l=l, tm=min(1024, L))
    else:
        u, qkv, z, ga, gb, ab = _proj(x, p['g_mix'], mod, w_in, w_gates, w_ab, l=l, tm=L)
    alog = _pad_lanes(p['gdn_a_log'][l])
    dtb = _pad_lanes(p['gdn_dt_bias'][l])
    nw = p['gdn_norm_w'][l].reshape(1, GDN_DK)
    if seq:
        yg, sfin = _s5_seq(u, s5m['be'], s5m['tp'], s5m['cpm'], s5m['pt'],
                           jnp.zeros((SLABS, B, 1, 2 * SLAB_STATE), F32), s5m['dsk'][l], l)
        sfin = sfin.reshape(SLABS, B, 2, SLAB_STATE).transpose(2, 1, 0, 3)
        sr = sfin[0].reshape(B, S5_GROUPS, S5_STATE)
        si = sfin[1].reshape(B, S5_GROUPS, S5_STATE)
        og, sg = _gdn_seq(qkv, z, ab, p['gdn_conv_w'], alog, dtb, nw,
                          jnp.zeros((B, GDN_CONV - 1, QKV_WIDTH), F32),
                          jnp.zeros((B, GDN_HEADS, GDN_DK, GDN_DK), F32), l)
        cb = qkv[:, L - (GDN_CONV - 1):, :].astype(F32)
    else:
        n = L
        s0 = jnp.concatenate([s5r0[l].reshape(n, SLABS, SLAB_STATE),
                              s5i0[l].reshape(n, SLABS, SLAB_STATE)], axis=-1).transpose(1, 0, 2)
        yg, s1 = _s5_step(u.reshape(SLABS, n, LANES), s5m['bst'], s5m['c0'], s5m['a1'],
                          s0, s5m['d1'][l], l)
        yg = yg.reshape(SLABS, 1, n, LANES)
        s1 = s1.transpose(1, 0, 2)
        sr = s1[:, :, :SLAB_STATE].reshape(n, S5_GROUPS, S5_STATE)
        si = s1[:, :, SLAB_STATE:].reshape(n, S5_GROUPS, S5_STATE)
        og, sg = _gdn_step(qkv.reshape(n, QKV_WIDTH), z.reshape(n, GDN_WIDTH), ab.reshape(n, LANES),
                           p['gdn_conv_w'][l], alog, dtb, nw,
                           sc0[l].reshape(n, (GDN_CONV - 1) * QKV_WIDTH), sg0, l, prev_sg)
        og = og.reshape(1, n, GDN_WIDTH)
        cb = jnp.concatenate([sc0[l][:, 1:, :], qkv.reshape(n, 1, QKV_WIDTH)], axis=1)
    x, h, lgT = _merge(yg, og, ga, gb, x, mod, p['w_s5_glu'], p['w_gdn_out'], p['w_out'],
                       p['g_ffn'], p['w_router'], l=l, tm=min(512, L), hi=hi, chunked=seq,
                       h_dtype=BF16 if (seq and l % 2 == 0) else F32)
    return x, h, lgT, (sr, si, sg, cb)


def kernel(x_prompt, x_sample, c_prompt, c_sample, state_s5_re, state_s5_im, state_gdn, state_conv,
           g_mix, g_ffn, g_final, w_ada, b_ada, w_in, s5_lambda_re, s5_lambda_im, s5_log_dt,
           s5_b_re, s5_b_im, s5_c_re, s5_c_im, s5_d, w_s5_glu, gdn_conv_w, gdn_a_log, gdn_dt_bias,
           gdn_norm_w, w_gdn_out, w_out, w_ffn_gate_up, w_ffn_down, w_router, b_router,
           w_exp_gate_up, w_exp_down):
    def in_proj_parts(w):
        return w, w[:, :, 2568:], jnp.pad(w[:, :, 2560:2568], ((0, 0), (0, 0), (0, LANES - 8)))

    D_ = x_prompt.shape[-1]
    p = dict(g_mix=g_mix.reshape(DEPTH, 1, D_), g_ffn=g_ffn.reshape(DEPTH, 1, D_), w_s5_glu=w_s5_glu,
             gdn_conv_w=gdn_conv_w, gdn_a_log=gdn_a_log, gdn_dt_bias=gdn_dt_bias,
             gdn_norm_w=gdn_norm_w, w_gdn_out=w_gdn_out, w_out=w_out,
             w_router=w_router.transpose(0, 2, 1), w_in=in_proj_parts(w_in),
             w_in_seq=in_proj_parts(w_in.astype(BF16)))
    nbp, L, D = x_prompt.shape
    nbs = x_sample.shape[0]

    mod = _ada(jnp.concatenate([c_prompt, c_sample], axis=0), w_ada, b_ada)
    mod_p = mod[:, :nbp].reshape(DEPTH, nbp, 1, 6 * D)
    mod_s = mod[:, nbp:].reshape(DEPTH, 1, nbs, 6 * D)

    seg = L // S5_T // 8
    be, bst, c0, cpm, tp, pt, a1 = _s5_prep(s5_lambda_re, s5_lambda_im, s5_log_dt, s5_b_re, s5_b_im,
                                            s5_c_re, s5_c_im, seg)
    d1 = [s5_d[l].reshape(SLABS, 1, LANES) for l in range(DEPTH)]
    s5m = dict(be=be, bst=bst, c0=c0, cpm=cpm, pt=pt, a1=a1, tp=tp, d1=d1,
               dsk=[jnp.tile(d, (1, 1, S5_T)) for d in d1])

    xs_ = [x_prompt, x_sample.reshape(1, nbs, D)]
    mods = [mod_p, mod_s]
    states = [(None, None, None, None), (state_s5_re, state_s5_im, state_gdn, state_conv)]
    outs = [[], []]
    gfin = g_final.reshape(1, D)
    for l in range(DEPTH):
        final = l == DEPTH - 1
        mixed = []
        for gi, seq in enumerate((True, False)):
            prev_sg = outs[gi][0][2] if (not seq and final and DEPTH == 2) else None
            x, h, lgT, st = _mixer_layer(xs_[gi], mods[gi], states[gi], p, s5m, l, prev_sg, seq=seq)
            outs[gi].append(st)
            mixed.append((h, x, mods[gi], lgT))
        if l % 2 == 0:
            wgu, wdn = w_ffn_gate_up[l // 2], w_ffn_down[l // 2]
            xs_ = [_ffn(h, x, mod_g, wgu if gi else wgu.astype(BF16), wdn if gi else wdn.astype(BF16), gfin,
                        l=l, tm=min(1024, x.shape[1]), hi=(gi == 1), final=final)
                   for gi, (h, x, mod_g, _) in enumerate(mixed)]
        else:
            xs_ = _moe_routed(mixed, b_router[l // 2], w_exp_gate_up[l // 2], w_exp_down[l // 2], gfin,
                              l=l, final=final)
    y_p, y_s = xs_
    st_p = [jnp.stack([o[i] for o in outs[0]]) for i in range(4)]
    st_s = [outs[1][-1][2] if (i == 2 and DEPTH == 2) else jnp.stack([o[i] for o in outs[1]])
            for i in range(4)]
    return (y_p, y_s.reshape(nbs, 1, D), st_p[0], st_p[1], st_p[2], st_p[3],
            st_s[0], st_s[1], st_s[2], st_s[3])
```
